```python
import math
import jax, jax.numpy as jnp
from jax import lax
import numpy as np

D_MODEL = 1024
BATCH = 8
SEQ = 4096
DEPTH = 1

MIX_WIDTH = 2 * D_MODEL
SSD_WIDTH = MIX_WIDTH // 2
SSD_HEAD_DIM = 64
SSD_HEADS = SSD_WIDTH // SSD_HEAD_DIM
SSD_GROUPS = 2
SSD_STATE = 128
SSD_CONV = 4
CHUNK = 128
CONF_WIDTH = MIX_WIDTH - SSD_WIDTH
CONF_KERNEL = 31
D_FF = 4 * D_MODEL
PLE_DIM = 256
EPS = 1e-6
XBC_WIDTH = SSD_WIDTH + 2 * SSD_GROUPS * SSD_STATE
IN_WIDTH = SSD_WIDTH + XBC_WIDTH + SSD_HEADS + 2 * CONF_WIDTH

kernel_name = "hybrid_ssd_conformer_block"


def rmsnorm(x, g):
    xf = x.astype(jnp.float32)
    y = xf * lax.rsqrt(jnp.mean(xf * xf, axis=-1, keepdims=True) + EPS)
    return (y * g.astype(jnp.float32)).astype(x.dtype)


def gated_group_rmsnorm(y, z, g):
    v = (y * jax.nn.silu(z)).astype(jnp.float32)
    shp = v.shape
    v = v.reshape(shp[:-1] + (SSD_GROUPS, shp[-1] // SSD_GROUPS))
    v = v * lax.rsqrt(jnp.mean(v * v, axis=-1, keepdims=True) + EPS)
    return (v.reshape(shp) * g.astype(jnp.float32)).astype(y.dtype)


def layernorm(x, g, b):
    xf = x.astype(jnp.float32)
    mu = jnp.mean(xf, axis=-1, keepdims=True)
    xc = xf - mu
    y = xc * lax.rsqrt(jnp.mean(xc * xc, axis=-1, keepdims=True) + EPS)
    return (y * g.astype(jnp.float32) + b.astype(jnp.float32)).astype(x.dtype)


def causal_depthwise_conv(x, w, b):
    k = w.shape[0]
    y = lax.conv_general_dilated(
        x, w[:, None, :].astype(x.dtype), window_strides=(1,), padding=[(k - 1, 0)],
        dimension_numbers=('NWC', 'WIO', 'NWC'), feature_group_count=x.shape[-1])
    return y + b.astype(x.dtype)


def ssd_chunked(x, dt, A, Bm, Cm):
    b, l, h, p = x.shape
    g, n = Bm.shape[-2:]
    e = h // g
    c = l // CHUNK
    dtype = x.dtype
    Xc = (x * dt[..., None].astype(dtype)).reshape(b, c, CHUNK, g, e, p)
    Bc = Bm.reshape(b, c, CHUNK, g, n)
    Cc = Cm.reshape(b, c, CHUNK, g, n)
    a = jnp.transpose((dt * A).reshape(b, c, CHUNK, g, e), (0, 3, 4, 1, 2))
    a_cs = jnp.cumsum(a, axis=-1)
    causal = jnp.tril(jnp.ones((CHUNK, CHUNK), dtype=bool))
    seg = a_cs[..., :, None] - a_cs[..., None, :]
    Lmat = jnp.exp(jnp.where(causal, seg, -jnp.inf)).astype(dtype)
    scores = jnp.einsum('bclgn,bcsgn->bgcls', Cc, Bc)
    y_diag = jnp.einsum('bgecls,bcsgep->bclgep', scores[:, :, None] * Lmat, Xc)
    decay_to_end = jnp.exp(a_cs[..., -1:] - a_cs).astype(dtype)
    chunk_states = jnp.einsum('bclgn,bgecl,bclgep->cbgepn', Bc, decay_to_end, Xc)
    chunk_decay = jnp.moveaxis(jnp.exp(a_cs[..., -1]).astype(dtype), -1, 0)

    def step(state, inp):
        dec, new = inp
        return state * dec[..., None, None] + new, state

    init = jnp.zeros(chunk_states.shape[1:], dtype)
    _, prev_states = lax.scan(step, init, (chunk_decay, chunk_states))
    decay_from_start = jnp.exp(a_cs).astype(dtype)
    y_off = jnp.einsum('bclgn,cbgepn,bgecl->bclgep', Cc, prev_states, decay_from_start)
    return (y_diag + y_off).reshape(b, l, h, p)


def _fwd_setup_inputs(seed: int = 0) -> dict:
    key = jax.random.key(seed)
    ks = jax.random.split(key, 32)
    L = DEPTH
    f32 = jnp.float32

    def nrm(k, shape, scale):
        return jax.random.normal(k, shape, f32) * scale

    def gain(k, shape):
        return 1.0 + 0.01 * jax.random.normal(k, shape, f32)

    dt0 = jnp.exp(jax.random.uniform(ks[6], (L, SSD_HEADS), f32,
                                     minval=math.log(1e-3), maxval=math.log(1e-1)))
    dt_bias = dt0 + jnp.log(-jnp.expm1(-dt0))
    return {
        "x": nrm(ks[0], (BATCH, SEQ, D_MODEL), 1.0),
        "p": nrm(ks[1], (DEPTH, BATCH, SEQ, PLE_DIM), 1.0),
        "mix_norm_g": gain(ks[2], (L, D_MODEL)),
        "w_in": nrm(ks[3], (L, D_MODEL, IN_WIDTH), D_MODEL ** -0.5),
        "ssd_conv_w": nrm(ks[4], (L, SSD_CONV, XBC_WIDTH), SSD_CONV ** -0.5),
        "ssd_conv_b": nrm(ks[5], (L, XBC_WIDTH), 0.01),
        "dt_bias": dt_bias,
        "A_log": jnp.log(jax.random.uniform(ks[7], (L, SSD_HEADS), f32, minval=1.0, maxval=16.0)),
        "D_skip": gain(ks[8], (L, SSD_HEADS)),
        "ssd_norm_g": gain(ks[9], (L, SSD_WIDTH)),
        "conf_dw_w": nrm(ks[10], (L, CONF_KERNEL, CONF_WIDTH), CONF_KERNEL ** -0.5),
        "conf_dw_b": nrm(ks[11], (L, CONF_WIDTH), 0.01),
        "conf_ln_g": gain(ks[12], (L, CONF_WIDTH)),
        "conf_ln_b": nrm(ks[13], (L, CONF_WIDTH), 0.01),
        "w_out": nrm(ks[14], (L, MIX_WIDTH, D_MODEL), MIX_WIDTH ** -0.5),
        "mlp_norm_g": gain(ks[15], (L, D_MODEL)),
        "w_up": nrm(ks[16], (L, D_MODEL, D_FF), D_MODEL ** -0.5),
        "w_down": nrm(ks[17], (L, D_FF, D_MODEL), D_FF ** -0.5),
        "ple_gate_norm_g": gain(ks[18], (L, D_MODEL)),
        "w_ple_gate": nrm(ks[19], (L, D_MODEL, D_MODEL), D_MODEL ** -0.5),
        "b_ple_gate": nrm(ks[20], (L, D_MODEL), 0.01),
        "w_ple": nrm(ks[21], (L, PLE_DIM, D_MODEL), PLE_DIM ** -0.5),
        "ple_norm_g": gain(ks[22], (L, D_MODEL)),
        "final_norm_g": gain(ks[23], (D_MODEL,)),
    }


def _fwd_reference(x, p, mix_norm_g, w_in, ssd_conv_w, ssd_conv_b, dt_bias, A_log, D_skip,
              ssd_norm_g, conf_dw_w, conf_dw_b, conf_ln_g, conf_ln_b, w_out, mlp_norm_g,
              w_up, w_down, ple_gate_norm_g, w_ple_gate, b_ple_gate, w_ple, ple_norm_g,
              final_norm_g):
    b, l, _ = x.shape
    split_at = np.cumsum([SSD_WIDTH, XBC_WIDTH, SSD_HEADS, CONF_WIDTH]).tolist()
    xbc_split = [SSD_WIDTH, SSD_WIDTH + SSD_GROUPS * SSD_STATE]
    h = x
    for i in range(DEPTH):
        u = rmsnorm(h, mix_norm_g[i])
        proj = u @ w_in[i].astype(u.dtype)
        z, xbc, dt_raw, conf_val, conf_gate = jnp.split(proj, split_at, axis=-1)

        xbc = jax.nn.silu(causal_depthwise_conv(xbc, ssd_conv_w[i], ssd_conv_b[i]))
        xs, Bm, Cm = jnp.split(xbc, xbc_split, axis=-1)
        dt = jax.nn.softplus(dt_raw.astype(jnp.float32) + dt_bias[i].astype(jnp.float32))
        A = -jnp.exp(A_log[i].astype(jnp.float32))
        xh = xs.reshape(b, l, SSD_HEADS, SSD_HEAD_DIM)
        y = ssd_chunked(xh, dt, A,
                        Bm.reshape(b, l, SSD_GROUPS, SSD_STATE),
                        Cm.reshape(b, l, SSD_GROUPS, SSD_STATE))
        y = (y + xh * D_skip[i].astype(xh.dtype)[:, None]).reshape(b, l, SSD_WIDTH)
        y_ssd = gated_group_rmsnorm(y, z, ssd_norm_g[i])

        v = conf_val * jax.nn.sigmoid(conf_gate)
        v = causal_depthwise_conv(v, conf_dw_w[i], conf_dw_b[i])
        y_conf = jax.nn.silu(layernorm(v, conf_ln_g[i], conf_ln_b[i]))

        mixed = jnp.concatenate([y_ssd, y_conf], axis=-1)
        h = h + mixed @ w_out[i].astype(mixed.dtype)

        u = rmsnorm(h, mlp_norm_g[i])
        hid = jax.nn.relu(u @ w_up[i].astype(u.dtype))
        h = h + (hid * hid) @ w_down[i].astype(hid.dtype)

        gate = jax.nn.sigmoid(rmsnorm(h, ple_gate_norm_g[i]) @ w_ple_gate[i].astype(h.dtype)
                              + b_ple_gate[i].astype(h.dtype))
        emb = rmsnorm(p[i].astype(h.dtype) @ w_ple[i].astype(h.dtype), ple_norm_g[i])
        h = h + gate * emb
    return rmsnorm(h, final_norm_g)


import jax as _jax
import jax.numpy as _jnp

TWIN_FORMAT = 'train_step'
FWD_PARAMS = ['x', 'p', 'mix_norm_g', 'w_in', 'ssd_conv_w', 'ssd_conv_b', 'dt_bias', 'A_log', 'D_skip', 'ssd_norm_g', 'conf_dw_w', 'conf_dw_b', 'conf_ln_g', 'conf_ln_b', 'w_out', 'mlp_norm_g', 'w_up', 'w_down', 'ple_gate_norm_g', 'w_ple_gate', 'b_ple_gate', 'w_ple', 'ple_norm_g', 'final_norm_g']
TWIN_WEIGHTS = ['mix_norm_g', 'w_in', 'ssd_conv_w', 'ssd_conv_b', 'dt_bias', 'A_log', 'D_skip', 'ssd_norm_g', 'conf_dw_w', 'conf_dw_b', 'conf_ln_g', 'conf_ln_b', 'w_out', 'mlp_norm_g', 'w_up', 'w_down', 'ple_gate_norm_g', 'w_ple_gate', 'b_ple_gate', 'w_ple', 'ple_norm_g', 'final_norm_g']
TWIN_DIFF_INPUT = 'x'
TWIN_INPUTS = ['x', 'p', 'mix_norm_g', 'w_in', 'ssd_conv_w', 'ssd_conv_b', 'dt_bias', 'A_log', 'D_skip', 'ssd_norm_g', 'conf_dw_w', 'conf_dw_b', 'conf_ln_g', 'conf_ln_b', 'w_out', 'mlp_norm_g', 'w_up', 'w_down', 'ple_gate_norm_g', 'w_ple_gate', 'b_ple_gate', 'w_ple', 'ple_norm_g', 'final_norm_g', 'loss_target', 'm_mix_norm_g', 'm_w_in', 'm_ssd_conv_w', 'm_ssd_conv_b', 'm_dt_bias', 'm_A_log', 'm_D_skip', 'm_ssd_norm_g', 'm_conf_dw_w', 'm_conf_dw_b', 'm_conf_ln_g', 'm_conf_ln_b', 'm_w_out', 'm_mlp_norm_g', 'm_w_up', 'm_w_down', 'm_ple_gate_norm_g', 'm_w_ple_gate', 'm_b_ple_gate', 'm_w_ple', 'm_ple_norm_g', 'm_final_norm_g', 'v_mix_norm_g', 'v_w_in', 'v_ssd_conv_w', 'v_ssd_conv_b', 'v_dt_bias', 'v_A_log', 'v_D_skip', 'v_ssd_norm_g', 'v_conf_dw_w', 'v_conf_dw_b', 'v_conf_ln_g', 'v_conf_ln_b', 'v_w_out', 'v_mlp_norm_g', 'v_w_up', 'v_w_down', 'v_ple_gate_norm_g', 'v_w_ple_gate', 'v_b_ple_gate', 'v_w_ple', 'v_ple_norm_g', 'v_final_norm_g']
TWIN_OUTPUTS = ['loss', 'grad_x', 'grad_mix_norm_g', 'grad_w_in', 'grad_ssd_conv_w', 'grad_ssd_conv_b', 'grad_dt_bias', 'grad_A_log', 'grad_D_skip', 'grad_ssd_norm_g', 'grad_conf_dw_w', 'grad_conf_dw_b', 'grad_conf_ln_g', 'grad_conf_ln_b', 'grad_w_out', 'grad_mlp_norm_g', 'grad_w_up', 'grad_w_down', 'grad_ple_gate_norm_g', 'grad_w_ple_gate', 'grad_b_ple_gate', 'grad_w_ple', 'grad_ple_norm_g', 'grad_final_norm_g', 'delta_mix_norm_g', 'delta_w_in', 'delta_ssd_conv_w', 'delta_ssd_conv_b', 'delta_dt_bias', 'delta_A_log', 'delta_D_skip', 'delta_ssd_norm_g', 'delta_conf_dw_w', 'delta_conf_dw_b', 'delta_conf_ln_g', 'delta_conf_ln_b', 'delta_w_out', 'delta_mlp_norm_g', 'delta_w_up', 'delta_w_down', 'delta_ple_gate_norm_g', 'delta_w_ple_gate', 'delta_b_ple_gate', 'delta_w_ple', 'delta_ple_norm_g', 'delta_final_norm_g', 'new_m_mix_norm_g', 'new_m_w_in', 'new_m_ssd_conv_w', 'new_m_ssd_conv_b', 'new_m_dt_bias', 'new_m_A_log', 'new_m_D_skip', 'new_m_ssd_norm_g', 'new_m_conf_dw_w', 'new_m_conf_dw_b', 'new_m_conf_ln_g', 'new_m_conf_ln_b', 'new_m_w_out', 'new_m_mlp_norm_g', 'new_m_w_up', 'new_m_w_down', 'new_m_ple_gate_norm_g', 'new_m_w_ple_gate', 'new_m_b_ple_gate', 'new_m_w_ple', 'new_m_ple_norm_g', 'new_m_final_norm_g', 'new_v_mix_norm_g', 'new_v_w_in', 'new_v_ssd_conv_w', 'new_v_ssd_conv_b', 'new_v_dt_bias', 'new_v_A_log', 'new_v_D_skip', 'new_v_ssd_norm_g', 'new_v_conf_dw_w', 'new_v_conf_dw_b', 'new_v_conf_ln_g', 'new_v_conf_ln_b', 'new_v_w_out', 'new_v_mlp_norm_g', 'new_v_w_up', 'new_v_w_down', 'new_v_ple_gate_norm_g', 'new_v_w_ple_gate', 'new_v_b_ple_gate', 'new_v_w_ple', 'new_v_ple_norm_g', 'new_v_final_norm_g']
TWIN_LEAF_KINDS = {'loss': 'loss', 'grad_x': 'grad_x', 'grad_mix_norm_g': 'grad_w', 'grad_w_in': 'grad_w', 'grad_ssd_conv_w': 'grad_w', 'grad_ssd_conv_b': 'grad_w', 'grad_dt_bias': 'grad_w', 'grad_A_log': 'grad_w', 'grad_D_skip': 'grad_w', 'grad_ssd_norm_g': 'grad_w', 'grad_conf_dw_w': 'grad_w', 'grad_conf_dw_b': 'grad_w', 'grad_conf_ln_g': 'grad_w', 'grad_conf_ln_b': 'grad_w', 'grad_w_out': 'grad_w', 'grad_mlp_norm_g': 'grad_w', 'grad_w_up': 'grad_w', 'grad_w_down': 'grad_w', 'grad_ple_gate_norm_g': 'grad_w', 'grad_w_ple_gate': 'grad_w', 'grad_b_ple_gate': 'grad_w', 'grad_w_ple': 'grad_w', 'grad_ple_norm_g': 'grad_w', 'grad_final_norm_g': 'grad_w', 'delta_mix_norm_g': 'delta_w', 'delta_w_in': 'delta_w', 'delta_ssd_conv_w': 'delta_w', 'delta_ssd_conv_b': 'delta_w', 'delta_dt_bias': 'delta_w', 'delta_A_log': 'delta_w', 'delta_D_skip': 'delta_w', 'delta_ssd_norm_g': 'delta_w', 'delta_conf_dw_w': 'delta_w', 'delta_conf_dw_b': 'delta_w', 'delta_conf_ln_g': 'delta_w', 'delta_conf_ln_b': 'delta_w', 'delta_w_out': 'delta_w', 'delta_mlp_norm_g': 'delta_w', 'delta_w_up': 'delta_w', 'delta_w_down': 'delta_w', 'delta_ple_gate_norm_g': 'delta_w', 'delta_w_ple_gate': 'delta_w', 'delta_b_ple_gate': 'delta_w', 'delta_w_ple': 'delta_w', 'delta_ple_norm_g': 'delta_w', 'delta_final_norm_g': 'delta_w', 'new_m_mix_norm_g': 'new_m', 'new_m_w_in': 'new_m', 'new_m_ssd_conv_w': 'new_m', 'new_m_ssd_conv_b': 'new_m', 'new_m_dt_bias': 'new_m', 'new_m_A_log': 'new_m', 'new_m_D_skip': 'new_m', 'new_m_ssd_norm_g': 'new_m', 'new_m_conf_dw_w': 'new_m', 'new_m_conf_dw_b': 'new_m', 'new_m_conf_ln_g': 'new_m', 'new_m_conf_ln_b': 'new_m', 'new_m_w_out': 'new_m', 'new_m_mlp_norm_g': 'new_m', 'new_m_w_up': 'new_m', 'new_m_w_down': 'new_m', 'new_m_ple_gate_norm_g': 'new_m', 'new_m_w_ple_gate': 'new_m', 'new_m_b_ple_gate': 'new_m', 'new_m_w_ple': 'new_m', 'new_m_ple_norm_g': 'new_m', 'new_m_final_norm_g': 'new_m', 'new_v_mix_norm_g': 'new_v', 'new_v_w_in': 'new_v', 'new_v_ssd_conv_w': 'new_v', 'new_v_ssd_conv_b': 'new_v', 'new_v_dt_bias': 'new_v', 'new_v_A_log': 'new_v', 'new_v_D_skip': 'new_v', 'new_v_ssd_norm_g': 'new_v', 'new_v_conf_dw_w': 'new_v', 'new_v_conf_dw_b': 'new_v', 'new_v_conf_ln_g': 'new_v', 'new_v_conf_ln_b': 'new_v', 'new_v_w_out': 'new_v', 'new_v_mlp_norm_g': 'new_v', 'new_v_w_up': 'new_v', 'new_v_w_down': 'new_v', 'new_v_ple_gate_norm_g': 'new_v', 'new_v_w_ple_gate': 'new_v', 'new_v_b_ple_gate': 'new_v', 'new_v_w_ple': 'new_v', 'new_v_ple_norm_g': 'new_v', 'new_v_final_norm_g': 'new_v'}


def _forward(args):
    return _fwd_reference(*[args[k] for k in FWD_PARAMS])


def _output_shape():
    out = _jax.eval_shape(lambda: _forward(_fwd_setup_inputs(0)))
    return out.shape, out.dtype

N_MICROBATCH = 1
ADAM_LR = 0.001
ADAM_B1 = 0.9
ADAM_B2 = 0.999
ADAM_EPS = 1e-08
ADAM_WD = 0.01
ADAM_STEP = 10
PER_EXAMPLE_BATCH_AXIS = {'x': 0, 'p': 1, 'loss_target': 0}
SHARED_INPUTS = []
_WEIGHT_DTYPES = {'mix_norm_g': _jnp.float32, 'w_in': _jnp.float32, 'ssd_conv_w': _jnp.float32, 'ssd_conv_b': _jnp.float32, 'dt_bias': _jnp.float32, 'A_log': _jnp.float32, 'D_skip': _jnp.float32, 'ssd_norm_g': _jnp.float32, 'conf_dw_w': _jnp.float32, 'conf_dw_b': _jnp.float32, 'conf_ln_g': _jnp.float32, 'conf_ln_b': _jnp.float32, 'w_out': _jnp.float32, 'mlp_norm_g': _jnp.float32, 'w_up': _jnp.float32, 'w_down': _jnp.float32, 'ple_gate_norm_g': _jnp.float32, 'w_ple_gate': _jnp.float32, 'b_ple_gate': _jnp.float32, 'w_ple': _jnp.float32, 'ple_norm_g': _jnp.float32, 'final_norm_g': _jnp.float32}
MOMENT_SCALE = {'mix_norm_g': 1.717430e-01, 'w_in': 7.889346e-02, 'ssd_conv_w': 8.916496e-02, 'ssd_conv_b': 1.201040e-01, 'dt_bias': 3.470893e-01, 'A_log': 3.482812e-01, 'D_skip': 6.439458e-01, 'ssd_norm_g': 1.019275e-01, 'conf_dw_w': 6.197526e-02, 'conf_dw_b': 1.299054e-01, 'conf_ln_g': 7.342091e-02, 'conf_ln_b': 7.451497e-02, 'w_out': 1.170216e-01, 'mlp_norm_g': 1.295265e-01, 'w_up': 6.769941e-02, 'w_down': 1.187138e-01, 'ple_gate_norm_g': 1.985967e-02, 'w_ple_gate': 2.026585e-02, 'b_ple_gate': 2.219288e-02, 'w_ple': 5.177734e-02, 'ple_norm_g': 5.752518e-02, 'final_norm_g': 3.219725e+01}


def _to_microbatches(a, axis):
    t = _jnp.moveaxis(a, axis, 0)
    t = t.reshape((N_MICROBATCH, t.shape[0] // N_MICROBATCH) + t.shape[1:])
    return _jnp.moveaxis(t, 1, axis + 1)


def setup_inputs(seed: int = 0) -> dict:
    inp = _fwd_setup_inputs(seed)
    key = _jax.random.fold_in(_jax.random.key(seed), 7919)
    shape, _ = _output_shape()
    out = dict(inp)
    out["loss_target"] = _jax.random.normal(_jax.random.fold_in(key, 0), shape, _jnp.float32)
    for i, name in enumerate(TWIN_WEIGHTS):
        w = inp[name].astype(_jnp.float32)
        if MOMENT_SCALE is None:
            s = _jnp.sqrt(_jnp.mean(_jnp.square(w)) + 1e-30)
        else:
            s = MOMENT_SCALE[name]
        km, kv = _jax.random.split(_jax.random.fold_in(key, i + 1))
        out[name] = w
        out["m_" + name] = s * _jax.random.normal(km, w.shape, _jnp.float32)
        out["v_" + name] = (s * s) * _jax.random.uniform(kv, w.shape, _jnp.float32, 0.5, 1.5)
    if N_MICROBATCH > 1:
        for name, axis in PER_EXAMPLE_BATCH_AXIS.items():
            out[name] = _to_microbatches(out[name], axis)
    return {'x': out['x'], 'p': out['p'], 'mix_norm_g': out['mix_norm_g'], 'w_in': out['w_in'], 'ssd_conv_w': out['ssd_conv_w'], 'ssd_conv_b': out['ssd_conv_b'], 'dt_bias': out['dt_bias'], 'A_log': out['A_log'], 'D_skip': out['D_skip'], 'ssd_norm_g': out['ssd_norm_g'], 'conf_dw_w': out['conf_dw_w'], 'conf_dw_b': out['conf_dw_b'], 'conf_ln_g': out['conf_ln_g'], 'conf_ln_b': out['conf_ln_b'], 'w_out': out['w_out'], 'mlp_norm_g': out['mlp_norm_g'], 'w_up': out['w_up'], 'w_down': out['w_down'], 'ple_gate_norm_g': out['ple_gate_norm_g'], 'w_ple_gate': out['w_ple_gate'], 'b_ple_gate': out['b_ple_gate'], 'w_ple': out['w_ple'], 'ple_norm_g': out['ple_norm_g'], 'final_norm_g': out['final_norm_g'], 'loss_target': out['loss_target'], 'm_mix_norm_g': out['m_mix_norm_g'], 'm_w_in': out['m_w_in'], 'm_ssd_conv_w': out['m_ssd_conv_w'], 'm_ssd_conv_b': out['m_ssd_conv_b'], 'm_dt_bias': out['m_dt_bias'], 'm_A_log': out['m_A_log'], 'm_D_skip': out['m_D_skip'], 'm_ssd_norm_g': out['m_ssd_norm_g'], 'm_conf_dw_w': out['m_conf_dw_w'], 'm_conf_dw_b': out['m_conf_dw_b'], 'm_conf_ln_g': out['m_conf_ln_g'], 'm_conf_ln_b': out['m_conf_ln_b'], 'm_w_out': out['m_w_out'], 'm_mlp_norm_g': out['m_mlp_norm_g'], 'm_w_up': out['m_w_up'], 'm_w_down': out['m_w_down'], 'm_ple_gate_norm_g': out['m_ple_gate_norm_g'], 'm_w_ple_gate': out['m_w_ple_gate'], 'm_b_ple_gate': out['m_b_ple_gate'], 'm_w_ple': out['m_w_ple'], 'm_ple_norm_g': out['m_ple_norm_g'], 'm_final_norm_g': out['m_final_norm_g'], 'v_mix_norm_g': out['v_mix_norm_g'], 'v_w_in': out['v_w_in'], 'v_ssd_conv_w': out['v_ssd_conv_w'], 'v_ssd_conv_b': out['v_ssd_conv_b'], 'v_dt_bias': out['v_dt_bias'], 'v_A_log': out['v_A_log'], 'v_D_skip': out['v_D_skip'], 'v_ssd_norm_g': out['v_ssd_norm_g'], 'v_conf_dw_w': out['v_conf_dw_w'], 'v_conf_dw_b': out['v_conf_dw_b'], 'v_conf_ln_g': out['v_conf_ln_g'], 'v_conf_ln_b': out['v_conf_ln_b'], 'v_w_out': out['v_w_out'], 'v_mlp_norm_g': out['v_mlp_norm_g'], 'v_w_up': out['v_w_up'], 'v_w_down': out['v_w_down'], 'v_ple_gate_norm_g': out['v_ple_gate_norm_g'], 'v_w_ple_gate': out['v_w_ple_gate'], 'v_b_ple_gate': out['v_b_ple_gate'], 'v_w_ple': out['v_w_ple'], 'v_ple_norm_g': out['v_ple_norm_g'], 'v_final_norm_g': out['v_final_norm_g']}


def _loss(weights, diff, rest, loss_target):
    with _jax.named_scope("forward"):
        args = {**rest, TWIN_DIFF_INPUT: diff, **{k: w.astype(_WEIGHT_DTYPES[k]) for k, w in weights.items()}}
        y = _forward(args)
    with _jax.named_scope("loss_head"):
        err = _jnp.square(y.astype(_jnp.float32) - loss_target)
        return 0.5 * _jnp.sum(_jnp.mean(err, axis=-1)) if err.ndim else 0.5 * err


def _adamw(w, g, m, v):
    m = ADAM_B1 * m + (1.0 - ADAM_B1) * g
    v = ADAM_B2 * v + (1.0 - ADAM_B2) * _jnp.square(g)
    m_hat = m / (1.0 - ADAM_B1 ** ADAM_STEP)
    v_hat = v / (1.0 - ADAM_B2 ** ADAM_STEP)
    delta = -ADAM_LR * (m_hat / (_jnp.sqrt(v_hat) + ADAM_EPS) + ADAM_WD * w)
    return delta, m, v


def reference(x, p, mix_norm_g, w_in, ssd_conv_w, ssd_conv_b, dt_bias, A_log, D_skip, ssd_norm_g, conf_dw_w, conf_dw_b, conf_ln_g, conf_ln_b, w_out, mlp_norm_g, w_up, w_down, ple_gate_norm_g, w_ple_gate, b_ple_gate, w_ple, ple_norm_g, final_norm_g, loss_target, m_mix_norm_g, m_w_in, m_ssd_conv_w, m_ssd_conv_b, m_dt_bias, m_A_log, m_D_skip, m_ssd_norm_g, m_conf_dw_w, m_conf_dw_b, m_conf_ln_g, m_conf_ln_b, m_w_out, m_mlp_norm_g, m_w_up, m_w_down, m_ple_gate_norm_g, m_w_ple_gate, m_b_ple_gate, m_w_ple, m_ple_norm_g, m_final_norm_g, v_mix_norm_g, v_w_in, v_ssd_conv_w, v_ssd_conv_b, v_dt_bias, v_A_log, v_D_skip, v_ssd_norm_g, v_conf_dw_w, v_conf_dw_b, v_conf_ln_g, v_conf_ln_b, v_w_out, v_mlp_norm_g, v_w_up, v_w_down, v_ple_gate_norm_g, v_w_ple_gate, v_b_ple_gate, v_w_ple, v_ple_norm_g, v_final_norm_g):
    given = dict(x=x, p=p, mix_norm_g=mix_norm_g, w_in=w_in, ssd_conv_w=ssd_conv_w, ssd_conv_b=ssd_conv_b, dt_bias=dt_bias, A_log=A_log, D_skip=D_skip, ssd_norm_g=ssd_norm_g, conf_dw_w=conf_dw_w, conf_dw_b=conf_dw_b, conf_ln_g=conf_ln_g, conf_ln_b=conf_ln_b, w_out=w_out, mlp_norm_g=mlp_norm_g, w_up=w_up, w_down=w_down, ple_gate_norm_g=ple_gate_norm_g, w_ple_gate=w_ple_gate, b_ple_gate=b_ple_gate, w_ple=w_ple, ple_norm_g=ple_norm_g, final_norm_g=final_norm_g, loss_target=loss_target, m_mix_norm_g=m_mix_norm_g, m_w_in=m_w_in, m_ssd_conv_w=m_ssd_conv_w, m_ssd_conv_b=m_ssd_conv_b, m_dt_bias=m_dt_bias, m_A_log=m_A_log, m_D_skip=m_D_skip, m_ssd_norm_g=m_ssd_norm_g, m_conf_dw_w=m_conf_dw_w, m_conf_dw_b=m_conf_dw_b, m_conf_ln_g=m_conf_ln_g, m_conf_ln_b=m_conf_ln_b, m_w_out=m_w_out, m_mlp_norm_g=m_mlp_norm_g, m_w_up=m_w_up, m_w_down=m_w_down, m_ple_gate_norm_g=m_ple_gate_norm_g, m_w_ple_gate=m_w_ple_gate, m_b_ple_gate=m_b_ple_gate, m_w_ple=m_w_ple, m_ple_norm_g=m_ple_norm_g, m_final_norm_g=m_final_norm_g, v_mix_norm_g=v_mix_norm_g, v_w_in=v_w_in, v_ssd_conv_w=v_ssd_conv_w, v_ssd_conv_b=v_ssd_conv_b, v_dt_bias=v_dt_bias, v_A_log=v_A_log, v_D_skip=v_D_skip, v_ssd_norm_g=v_ssd_norm_g, v_conf_dw_w=v_conf_dw_w, v_conf_dw_b=v_conf_dw_b, v_conf_ln_g=v_conf_ln_g, v_conf_ln_b=v_conf_ln_b, v_w_out=v_w_out, v_mlp_norm_g=v_mlp_norm_g, v_w_up=v_w_up, v_w_down=v_w_down, v_ple_gate_norm_g=v_ple_gate_norm_g, v_w_ple_gate=v_w_ple_gate, v_b_ple_gate=v_b_ple_gate, v_w_ple=v_w_ple, v_ple_norm_g=v_ple_norm_g, v_final_norm_g=v_final_norm_g)
    weights = {n: given[n] for n in TWIN_WEIGHTS}
    shared = {n: given[n] for n in SHARED_INPUTS}
    per_example = {n: given[n] for n in ['x', 'p']}
    grad_fn = _jax.value_and_grad(_loss, argnums=(0, 1))

    def one_microbatch(ex, loss_target):
        ex = dict(ex)
        diff = ex.pop(TWIN_DIFF_INPUT)
        return grad_fn(weights, diff, {**shared, **ex}, loss_target)

    if N_MICROBATCH == 1:
        loss, (grad_w, grad_x) = one_microbatch(per_example, given["loss_target"])
    else:
        def body(carry, xs):
            loss_sum, grad_sum = carry
            l_k, (gw_k, gx_k) = one_microbatch(xs[0], xs[1])
            with _jax.named_scope("update"):
                return (loss_sum + l_k, _jax.tree.map(_jnp.add, grad_sum, gw_k)), gx_k

        init = (_jnp.zeros((), _jnp.float32), _jax.tree.map(_jnp.zeros_like, weights))
        (loss, grad_w), grad_x = _jax.lax.scan(body, init, (per_example, given["loss_target"]))
    with _jax.named_scope("update"):
        delta_w, new_m, new_v = {}, {}, {}
        for n in TWIN_WEIGHTS:
            delta_w[n], new_m[n], new_v[n] = _adamw(weights[n], grad_w[n], given["m_" + n], given["v_" + n])
    return (loss, grad_x, *[grad_w[n] for n in TWIN_WEIGHTS], *[delta_w[n] for n in TWIN_WEIGHTS],
            *[new_m[n] for n in TWIN_WEIGHTS], *[new_v[n] for n in TWIN_WEIGHTS])
```

```python
import functools

import jax
import jax.numpy as jnp
from jax import lax
from jax.experimental import pallas as pl
from jax.experimental.pallas import tpu as pltpu

f32 = jnp.float32
bf16 = jnp.bfloat16

EPS = 1e-6
D_MODEL = 1024
SSD_HEADS = 16
SSD_HEAD_DIM = 64
SSD_GROUPS = 2
SSD_STATE = 128
SSD_CONV = 4
CHUNK = 128
XBC_WIDTH = 1536
CONF_KERNEL = 31
D_FF = 4096
PLE_DIM = 256
IN_WIDTH = 4624
N_DEV = 8
DT_PAD = 128
SEG_Z = (0, 1024)
SEG_XBC = (1024, 2560)
SEG_CV = (2560, 3584)
SEG_CG = (3584, 4608)
SEG_DT = (4608, 4736)
IN_PERM_WIDTH = 4736

ADAM_LR = 0.001
ADAM_B1 = 0.9
ADAM_B2 = 0.999
ADAM_EPS = 1e-08
ADAM_WD = 0.01
ADAM_STEP = 10

VMEM_LIMIT_BYTES = 56 * 1024 * 1024
TOKEN_TILE = 256
SMALL_ROWS = 16

HBM_SPEC = pl.BlockSpec(memory_space=pltpu.HBM)
MESH_ID = pl.DeviceIdType.MESH


def _params(*semantics):
    return pltpu.CompilerParams(dimension_semantics=semantics, vmem_limit_bytes=VMEM_LIMIT_BYTES)


def _nn(a, b):
    return lax.dot_general(a, b, (((1,), (0,)), ((), ())), preferred_element_type=f32)


def _nt(a, b):
    return lax.dot_general(a, b, (((1,), (1,)), ((), ())), preferred_element_type=f32)


def _tn(a, b):
    return lax.dot_general(a, b, (((0,), (0,)), ((), ())), preferred_element_type=f32)


@jax.custom_vjp
def bnn(a, b):
    return _nn(a.astype(bf16), b.astype(bf16))


def _bnn_fwd(a, b):
    ab, bb = a.astype(bf16), b.astype(bf16)
    return _nn(ab, bb), (ab, bb)


def _bnn_bwd(res, g):
    ab, bb = res
    gb = g.astype(bf16)
    return _nt(gb, bb), _tn(ab, gb)


bnn.defvjp(_bnn_fwd, _bnn_bwd)


@jax.custom_vjp
def bnt(a, b):
    return _nt(a.astype(bf16), b.astype(bf16))


def _bnt_fwd(a, b):
    ab, bb = a.astype(bf16), b.astype(bf16)
    return _nt(ab, bb), (ab, bb)


def _bnt_bwd(res, g):
    ab, bb = res
    gb = g.astype(bf16)
    return _nn(gb, bb), _tn(gb, ab)


bnt.defvjp(_bnt_fwd, _bnt_bwd)


@jax.custom_vjp
def btn(a, b):
    return _tn(a.astype(bf16), b.astype(bf16))


def _btn_fwd(a, b):
    ab, bb = a.astype(bf16), b.astype(bf16)
    return _tn(ab, bb), (ab, bb)


def _btn_bwd(res, g):
    ab, bb = res
    gb = g.astype(bf16)
    return _nt(bb, gb), _nn(ab, gb)


btn.defvjp(_btn_fwd, _btn_bwd)


def _rms(x, g):
    return x * lax.rsqrt(jnp.mean(x * x, axis=-1, keepdims=True) + EPS) * g


def _gated_norm(y, z, g):
    v = y * jax.nn.silu(z)
    half = v.shape[-1] // SSD_GROUPS
    parts = []
    for k in range(SSD_GROUPS):
        vk = v[:, k * half:(k + 1) * half]
        parts.append(vk * lax.rsqrt(jnp.mean(vk * vk, axis=-1, keepdims=True) + EPS))
    return jnp.concatenate(parts, axis=-1) * g


def _ln_silu(v, g, b):
    mu = jnp.mean(v, axis=-1, keepdims=True)
    xc = v - mu
    y = xc * lax.rsqrt(jnp.mean(xc * xc, axis=-1, keepdims=True) + EPS) * g + b
    return jax.nn.silu(y)


def _acc_init(i, *refs):
    @pl.when(i == 0)
    def _():
        for r in refs:
            r[...] = jnp.zeros_like(r)


def _row_spec(tm, n):
    return pl.BlockSpec((tm, n), lambda i: (i, 0))


def _const_spec(shape):
    nd = len(shape)
    return pl.BlockSpec(shape, lambda i: (0,) * nd)


def _prev_halo_spec(tm, halo, n):
    return pl.BlockSpec((halo, n), lambda i: (jnp.maximum(i * (tm // halo) - 1, 0), 0))


def _next_halo_spec(tm, halo, n, t):
    return pl.BlockSpec((halo, n), lambda i: (jnp.minimum((i + 1) * (tm // halo), t // halo - 1), 0))


def _in_proj(x, g1, w_in_p):
    t = x.shape[0]
    tm = min(TOKEN_TILE, t)
    segs = (SEG_Z, SEG_XBC, SEG_CV, SEG_CG, SEG_DT)

    def body(x_ref, g_ref, w_ref, u_ref, z_ref, xbc_ref, cv_ref, cg_ref, dt_ref):
        u = _rms(x_ref[...], g_ref[...]).astype(bf16)
        u_ref[...] = u
        for (lo, hi), o_ref in zip(segs[:4], (z_ref, xbc_ref, cv_ref, cg_ref)):
            o_ref[...] = _nn(u, w_ref[:, lo:hi])
        dt_ref[...] = _nn(u, w_ref[:, SEG_DT[0]:SEG_DT[1]])[:, :SSD_HEADS]

    outs = (jax.ShapeDtypeStruct((t, D_MODEL), bf16), jax.ShapeDtypeStruct((t, 1024), f32),
            jax.ShapeDtypeStruct((t, XBC_WIDTH), f32), jax.ShapeDtypeStruct((t, 1024), f32),
            jax.ShapeDtypeStruct((t, 1024), f32), jax.ShapeDtypeStruct((t, SSD_HEADS), f32))
    return pl.pallas_call(
        body, name="in_proj", grid=(t // tm,),
        in_specs=[_row_spec(tm, D_MODEL), _const_spec((1, D_MODEL)), _const_spec((D_MODEL, IN_PERM_WIDTH))],
        out_specs=[_row_spec(tm, D_MODEL), _row_spec(tm, 1024), _row_spec(tm, XBC_WIDTH), _row_spec(tm, 1024),
                   _row_spec(tm, 1024), _row_spec(tm, SSD_HEADS)],
        out_shape=outs, compiler_params=_params("arbitrary"),
    )(x, g1, w_in_p)


def _conv4_tile(ext_ref, w_ref, b_ref, tm):
    acc = b_ref[...] + w_ref[0:1, :] * ext_ref[pl.ds(8 - 3, tm), :]
    for k in range(1, SSD_CONV):
        acc = acc + w_ref[k:k + 1, :] * ext_ref[pl.ds(8 - 3 + k, tm), :]
    return acc


def _ssd_pre(xbc_raw, dt_raw, cw, cb, dt_bias):
    t = xbc_raw.shape[0]
    tm = min(TOKEN_TILE, t)

    def body(cur_ref, halo_ref, dt_ref, w_ref, b_ref, dtb_ref, act_ref, dto_ref, ext_ref):
        i = pl.program_id(0)
        ext_ref[0:8, :] = jnp.where(i == 0, 0.0, halo_ref[...])
        ext_ref[8:, :] = cur_ref[...]
        act_ref[...] = jax.nn.silu(_conv4_tile(ext_ref, w_ref, b_ref, tm))
        dto_ref[...] = jax.nn.softplus(dt_ref[...] + dtb_ref[...])

    return pl.pallas_call(
        body, name="ssd_pre", grid=(t // tm,),
        in_specs=[_row_spec(tm, XBC_WIDTH), _prev_halo_spec(tm, 8, XBC_WIDTH), _row_spec(tm, SSD_HEADS),
                  _const_spec((8, XBC_WIDTH)), _const_spec((1, XBC_WIDTH)), _const_spec((1, SSD_HEADS))],
        out_specs=[_row_spec(tm, XBC_WIDTH), _row_spec(tm, SSD_HEADS)],
        out_shape=(jax.ShapeDtypeStruct((t, XBC_WIDTH), f32), jax.ShapeDtypeStruct((t, SSD_HEADS), f32)),
        scratch_shapes=[pltpu.VMEM((tm + 8, XBC_WIDTH), f32)],
        compiler_params=_params("arbitrary"),
    )(xbc_raw, xbc_raw, dt_raw, cw, cb, dt_bias)


def _ssd_consts():
    r = lax.broadcasted_iota(jnp.int32, (CHUNK, CHUNK), 0)
    c = lax.broadcasted_iota(jnp.int32, (CHUNK, CHUNK), 1)
    causal = r >= c
    tril = causal.astype(f32)
    triu = (r <= c).astype(f32)
    hh = lax.broadcasted_iota(jnp.int32, (SSD_HEADS, D_MODEL), 0)
    jj = lax.broadcasted_iota(jnp.int32, (SSD_HEADS, D_MODEL), 1)
    expand = (lax.shift_right_logical(jj, 6) == hh).astype(f32)
    lane = lax.broadcasted_iota(jnp.int32, (1, 2 * SSD_HEAD_DIM), 1)
    m0 = (lane < SSD_HEAD_DIM).astype(f32)
    return causal, tril, triu, expand, m0, 1.0 - m0


def _ssd_chunk(xs, bm, cm, dt, s_prev, a_log, d_skip, consts):
    causal, tril, triu, expand, m0, m1 = consts
    hp = lax.Precision.HIGHEST
    a = dt * (-jnp.exp(a_log))
    cs = jnp.dot(tril, a, precision=hp)
    cs_t = lax.dot_general(a, triu, (((0,), (0,)), ((), ())), precision=hp)
    dt_e = jnp.dot(dt, expand, precision=hp)
    cs_e = jnp.dot(cs, expand, precision=hp)
    d_e = jnp.dot(jnp.broadcast_to(d_skip, (8, SSD_HEADS)), expand, precision=hp)[0:1, :]
    xc = xs * dt_e
    cs_last = cs_e[CHUNK - 1:CHUNK, :]
    x_dec = xc * jnp.exp(cs_last - cs_e)
    dec_start = jnp.exp(cs_e)
    chunk_dec = jnp.exp(cs_last)
    gw = D_MODEL // SSD_GROUPS
    ys, states = [], []
    for g in range(SSD_GROUPS):
        bg = bm[:, g * SSD_STATE:(g + 1) * SSD_STATE]
        cg = cm[:, g * SSD_STATE:(g + 1) * SSD_STATE]
        sp = s_prev[:, g * gw:(g + 1) * gw]
        states.append(sp * chunk_dec[:, g * gw:(g + 1) * gw] + btn(bg, x_dec[:, g * gw:(g + 1) * gw]))
        y_off = bnn(cg, sp) * dec_start[:, g * gw:(g + 1) * gw]
        scores = bnt(cg, bg)
        pieces = []
        for pr in range(gw // (2 * SSD_HEAD_DIM)):
            lo = g * gw + pr * 2 * SSD_HEAD_DIM
            xp = xc[:, lo:lo + 2 * SSD_HEAD_DIM]
            acc = None
            for h, mk in ((lo // SSD_HEAD_DIM, m0), (lo // SSD_HEAD_DIM + 1, m1)):
                seg = cs[:, h:h + 1] - cs_t[h:h + 1, :]
                mat = scores * jnp.exp(jnp.where(causal, seg, -jnp.inf))
                term = bnn(mat, xp * mk)
                acc = term if acc is None else acc + term
            pieces.append(acc)
        ys.append(jnp.concatenate(pieces, axis=-1) + y_off)
    y = jnp.concatenate(ys, axis=-1) + xs * d_e
    return y, jnp.concatenate(states, axis=-1)


def _ssd_scan(xbc_act, dt, a_log, d_skip):
    t = xbc_act.shape[0]
    nc = t // CHUNK

    def body(xbc_ref, dt_ref, al_ref, dk_ref, y_ref, sp_ref, state_ref):
        c = pl.program_id(0)

        @pl.when(c == 0)
        def _():
            state_ref[...] = jnp.zeros_like(state_ref)

        s_prev = state_ref[...]
        sp_ref[...] = s_prev
        y, s_next = _ssd_chunk(xbc_ref[:, 0:1024], xbc_ref[:, 1024:1280], xbc_ref[:, 1280:1536], dt_ref[...],
                               s_prev, al_ref[...], dk_ref[...], _ssd_consts())
        y_ref[...] = y
        state_ref[...] = s_next

    return pl.pallas_call(
        body, name="ssd_scan", grid=(nc,),
        in_specs=[_row_spec(CHUNK, XBC_WIDTH), _row_spec(CHUNK, SSD_HEADS), _const_spec((1, SSD_HEADS)),
                  _const_spec((1, SSD_HEADS))],
        out_specs=[_row_spec(CHUNK, D_MODEL), pl.BlockSpec((None, SSD_STATE, D_MODEL), lambda c: (c, 0, 0))],
        out_shape=(jax.ShapeDtypeStruct((t, D_MODEL), f32), jax.ShapeDtypeStruct((nc, SSD_STATE, D_MODEL), f32)),
        scratch_shapes=[pltpu.VMEM((SSD_STATE, D_MODEL), f32)],
        compiler_params=_params("arbitrary"),
    )(xbc_act, dt, a_log, d_skip)


def _fill_glu_ext(ext_ref, cv_ref, cg_ref, hcv_ref, hcg_ref, i):
    ext_ref[0:32, :] = jnp.where(i == 0, 0.0, hcv_ref[...] * jax.nn.sigmoid(hcg_ref[...]))
    ext_ref[32:, :] = cv_ref[...] * jax.nn.sigmoid(cg_ref[...])


def _conf_conv(cv, cg, w, b):
    t = cv.shape[0]
    tm = min(TOKEN_TILE, t)

    def body(cv_ref, cg_ref, hcv_ref, hcg_ref, w_ref, b_ref, o_ref, ext_ref):
        _fill_glu_ext(ext_ref, cv_ref, cg_ref, hcv_ref, hcg_ref, pl.program_id(0))
        acc = b_ref[...] + w_ref[0:1, :] * ext_ref[pl.ds(2, tm), :]
        for k in range(1, CONF_KERNEL):
            acc = acc + w_ref[k:k + 1, :] * ext_ref[pl.ds(2 + k, tm), :]
        o_ref[...] = acc

    return pl.pallas_call(
        body, name="conf_conv", grid=(t // tm,),
        in_specs=[_row_spec(tm, 1024), _row_spec(tm, 1024), _prev_halo_spec(tm, 32, 1024), _prev_halo_spec(tm, 32, 1024),
                  _const_spec((32, 1024)), _const_spec((1, 1024))],
        out_specs=_row_spec(tm, 1024), out_shape=jax.ShapeDtypeStruct((t, 1024), f32),
        scratch_shapes=[pltpu.VMEM((tm + 32, 1024), f32)],
        compiler_params=_params("arbitrary"),
    )(cv, cg, cv, cg, w, b)


def _out_proj(y, z, v2, x, g_ssd, ln_g, ln_b, w_out, g2):
    t = x.shape[0]
    tm = min(TOKEN_TILE, t)

    def body(y_ref, z_ref, v_ref, x_ref, gs_ref, lg_ref, lb_ref, w_ref, g2_ref, ys_ref, yc_ref, h1_ref, u2_ref):
        ys = _gated_norm(y_ref[...], z_ref[...], gs_ref[...]).astype(bf16)
        yc = _ln_silu(v_ref[...], lg_ref[...], lb_ref[...]).astype(bf16)
        ys_ref[...] = ys
        yc_ref[...] = yc
        h1 = x_ref[...] + _nn(ys, w_ref[0:1024, :]) + _nn(yc, w_ref[1024:2048, :])
        h1_ref[...] = h1
        u2_ref[...] = _rms(h1, g2_ref[...]).astype(bf16)

    vec = _const_spec((1, 1024))
    row = _row_spec(tm, 1024)
    return pl.pallas_call(
        body, name="out_proj", grid=(t // tm,),
        in_specs=[row, row, row, row, vec, vec, vec, _const_spec((2048, 1024)), vec],
        out_specs=[row, row, row, row],
        out_shape=(jax.ShapeDtypeStruct((t, 1024), bf16), jax.ShapeDtypeStruct((t, 1024), bf16),
                   jax.ShapeDtypeStruct((t, 1024), f32), jax.ShapeDtypeStruct((t, 1024), bf16)),
        compiler_params=_params("arbitrary"),
    )(y, z, v2, x, g_ssd, ln_g, ln_b, w_out, g2)


def _mlp_up(u2, w_up):
    t = u2.shape[0]
    tm = min(TOKEN_TILE, t)
    blk = 512

    def body(u_ref, w_ref, pre_ref, hs_ref):
        u = u_ref[...]
        for k in range(D_FF // blk):
            pre = _nn(u, w_ref[:, k * blk:(k + 1) * blk])
            pre_ref[:, k * blk:(k + 1) * blk] = pre.astype(bf16)
            r = jnp.maximum(pre, 0.0)
            hs_ref[:, k * blk:(k + 1) * blk] = (r * r).astype(bf16)

    return pl.pallas_call(
        body, name="mlp_up", grid=(t // tm,),
        in_specs=[_row_spec(tm, D_MODEL), _const_spec((D_MODEL, D_FF))],
        out_specs=[_row_spec(tm, D_FF), _row_spec(tm, D_FF)],
        out_shape=(jax.ShapeDtypeStruct((t, D_FF), bf16), jax.ShapeDtypeStruct((t, D_FF), bf16)),
        compiler_params=_params("arbitrary"),
    )(u2, w_up)


def _mlp_down(h1, hs, w_down):
    t = h1.shape[0]
    tm = min(TOKEN_TILE, t)

    def body(h_ref, hs_ref, w_ref, o_ref):
        o_ref[...] = h_ref[...] + _nn(hs_ref[...], w_ref[...])

    return pl.pallas_call(
        body, name="mlp_down", grid=(t // tm,),
        in_specs=[_row_spec(tm, D_MODEL), _row_spec(tm, D_FF), _const_spec((D_FF, D_MODEL))],
        out_specs=_row_spec(tm, D_MODEL), out_shape=jax.ShapeDtypeStruct((t, D_MODEL), f32),
        compiler_params=_params("arbitrary"),
    )(h1, hs, w_down)


TAIL_LOSS, TAIL_FINAL_G, TAIL_PLE_G, TAIL_GATE_B, TAIL_GATE_NORM_G = 0, 1, 2, 3, 4


def _tail(h2, p, target, g3, w_gate, b_gate, w_ple, ple_g, fin_g):
    t = h2.shape[0]
    tm = min(TOKEN_TILE, t)

    def body(h_ref, p_ref, t_ref, g3_ref, wg_ref, bg_ref, wp_ref, pg_ref, fg_ref,
             dh_ref, dhb_ref, dwg_ref, dwp_ref, acc_ref):
        i = pl.program_id(0)
        _acc_init(i, dwg_ref, dwp_ref, acc_ref)
        h2v = h_ref[...]
        tgt = t_ref[...]
        u3, vjp_u3 = jax.vjp(_rms, h2v, g3_ref[...])
        u3b = u3.astype(bf16)
        pb = p_ref[...].astype(bf16)
        gate_pre = _nn(u3b, wg_ref[...]) + bg_ref[...]
        emb_pre = _nn(pb, wp_ref[...])

        def tail_fn(hv, gp, ep, pg, fg):
            h3 = hv + jax.nn.sigmoid(gp) * _rms(ep, pg)
            err = _rms(h3, fg) - tgt
            return 0.5 * jnp.mean(err * err, axis=-1, keepdims=True)

        loss_tok, vjp_tail = jax.vjp(tail_fn, h2v, gate_pre, emb_pre, pg_ref[...], fg_ref[...])
        dh_a, dgp, dep, dpg, dfg = vjp_tail(jnp.ones_like(loss_tok))
        dgpb = dgp.astype(bf16)
        dh_b, dg3 = vjp_u3(_nt(dgpb, wg_ref[...]))
        dh = dh_a + dh_b
        dh_ref[...] = dh
        dhb_ref[...] = dh.astype(bf16)
        dwg_ref[...] += _tn(u3b, dgpb)
        dwp_ref[...] += _tn(pb, dep.astype(bf16))
        acc_ref[TAIL_LOSS:TAIL_LOSS + 1, :] += jnp.broadcast_to(jnp.sum(loss_tok, axis=0, keepdims=True), (1, 1024))
        acc_ref[TAIL_FINAL_G:TAIL_FINAL_G + 1, :] += dfg
        acc_ref[TAIL_PLE_G:TAIL_PLE_G + 1, :] += dpg
        acc_ref[TAIL_GATE_B:TAIL_GATE_B + 1, :] += jnp.sum(dgp, axis=0, keepdims=True)
        acc_ref[TAIL_GATE_NORM_G:TAIL_GATE_NORM_G + 1, :] += dg3

    vec = _const_spec((1, 1024))
    row = _row_spec(tm, 1024)
    return pl.pallas_call(
        body, name="tail", grid=(t // tm,),
        in_specs=[row, _row_spec(tm, PLE_DIM), row, vec, _const_spec((1024, 1024)), vec, _const_spec((PLE_DIM, 1024)), vec, vec],
        out_specs=[row, row, _const_spec((1024, 1024)), _const_spec((PLE_DIM, 1024)), _const_spec((8, 1024))],
        out_shape=(jax.ShapeDtypeStruct((t, 1024), f32), jax.ShapeDtypeStruct((t, 1024), bf16),
                   jax.ShapeDtypeStruct((1024, 1024), f32), jax.ShapeDtypeStruct((PLE_DIM, 1024), f32),
                   jax.ShapeDtypeStruct((8, 1024), f32)),
        compiler_params=_params("arbitrary"),
    )(h2, p, target, g3, w_gate, b_gate, w_ple, ple_g, fin_g)


def _mlp_bwd(dh2, dh2b, pre, h1, g2, w_down, w_up):
    t = dh2.shape[0]
    tm = min(TOKEN_TILE, t)
    blk = 512

    def body(dh_ref, dhb_ref, pre_ref, h1_ref, g_ref, wd_hbm, wu_hbm, dpre_ref, dh1_ref, dh1b_ref, acc_ref,
             wd_ref, wu_ref):
        i = pl.program_id(0)
        _acc_init(i, acc_ref)

        @pl.when(i == 0)
        def _():
            pltpu.sync_copy(wd_hbm, wd_ref)
            pltpu.sync_copy(wu_hbm, wu_ref)

        dhb = dhb_ref[...]
        du2 = jnp.zeros((tm, D_MODEL), f32)
        for k in range(D_FF // blk):
            dhs = _nt(dhb, wd_ref[k * blk:(k + 1) * blk, :])
            dpre = (dhs * (2.0 * jnp.maximum(pre_ref[:, k * blk:(k + 1) * blk].astype(f32), 0.0))).astype(bf16)
            dpre_ref[:, k * blk:(k + 1) * blk] = dpre
            du2 = du2 + _nt(dpre, wu_ref[:, k * blk:(k + 1) * blk])
        _, vjp_u2 = jax.vjp(_rms, h1_ref[...], g_ref[...])
        d, dg = vjp_u2(du2)
        dh1 = dh_ref[...] + d
        dh1_ref[...] = dh1
        dh1b_ref[...] = dh1.astype(bf16)
        acc_ref[0:1, :] += dg

    row = _row_spec(tm, 1024)
    return pl.pallas_call(
        body, name="mlp_bwd", grid=(t // tm,),
        in_specs=[row, row, _row_spec(tm, D_FF), row, _const_spec((1, 1024)), HBM_SPEC, HBM_SPEC],
        out_specs=[_row_spec(tm, D_FF), row, row, _const_spec((8, 1024))],
        out_shape=(jax.ShapeDtypeStruct((t, D_FF), bf16), jax.ShapeDtypeStruct((t, 1024), f32),
                   jax.ShapeDtypeStruct((t, 1024), bf16), jax.ShapeDtypeStruct((8, 1024), f32)),
        scratch_shapes=[pltpu.VMEM((D_FF, D_MODEL), bf16), pltpu.VMEM((D_MODEL, D_FF), bf16)],
        compiler_params=_params("arbitrary"),
    )(dh2, dh2b, pre, h1, g2, w_down, w_up)


OPB_SSD_G, OPB_LN_G, OPB_LN_B, OPB_CONV_B = 0, 1, 2, 3


def _out_proj_bwd(dh1b, y, z, v2, g_ssd, ln_g, ln_b, w_out):
    t = y.shape[0]
    tm = min(TOKEN_TILE, t)

    def body(dh_ref, y_ref, z_ref, v_ref, gs_ref, lg_ref, lb_ref, w_ref, dy_ref, dz_ref, dv_ref, acc_ref):
        i = pl.program_id(0)
        _acc_init(i, acc_ref)
        dhb = dh_ref[...]
        dys = _nt(dhb, w_ref[0:1024, :])
        dyc = _nt(dhb, w_ref[1024:2048, :])
        _, vjp_g = jax.vjp(_gated_norm, y_ref[...], z_ref[...], gs_ref[...])
        dy, dz, dgs = vjp_g(dys)
        _, vjp_l = jax.vjp(_ln_silu, v_ref[...], lg_ref[...], lb_ref[...])
        dv, dlg, dlb = vjp_l(dyc)
        dy_ref[...] = dy
        dz_ref[...] = dz.astype(bf16)
        dv_ref[...] = dv
        acc_ref[OPB_SSD_G:OPB_SSD_G + 1, :] += dgs
        acc_ref[OPB_LN_G:OPB_LN_G + 1, :] += dlg
        acc_ref[OPB_LN_B:OPB_LN_B + 1, :] += dlb
        acc_ref[OPB_CONV_B:OPB_CONV_B + 1, :] += jnp.sum(dv, axis=0, keepdims=True)

    vec = _const_spec((1, 1024))
    row = _row_spec(tm, 1024)
    return pl.pallas_call(
        body, name="out_proj_bwd", grid=(t // tm,),
        in_specs=[row, row, row, row, vec, vec, vec, _const_spec((2048, 1024))],
        out_specs=[row, row, row, _const_spec((8, 1024))],
        out_shape=(jax.ShapeDtypeStruct((t, 1024), f32), jax.ShapeDtypeStruct((t, 1024), bf16),
                   jax.ShapeDtypeStruct((t, 1024), f32), jax.ShapeDtypeStruct((8, 1024), f32)),
        compiler_params=_params("arbitrary"),
    )(dh1b, y, z, v2, g_ssd, ln_g, ln_b, w_out)


def _conf_conv_bwd(dv2, cv, cg, w):
    t = cv.shape[0]
    tm = min(TOKEN_TILE, t)

    def body(dv_ref, dvn_ref, cv_ref, cg_ref, hcv_ref, hcg_ref, w_ref, dcv_ref, dcg_ref, dw_ref, ext_ref, dext_ref):
        i = pl.program_id(0)
        _acc_init(i, dw_ref)
        _fill_glu_ext(ext_ref, cv_ref, cg_ref, hcv_ref, hcg_ref, i)
        dcur = dv_ref[...]
        dext_ref[0:tm, :] = dcur
        dext_ref[tm:, :] = jnp.where(i == pl.num_programs(0) - 1, 0.0, dvn_ref[...])
        dglu = w_ref[0:1, :] * dext_ref[pl.ds(CONF_KERNEL - 1, tm), :]
        for k in range(1, CONF_KERNEL):
            dglu = dglu + w_ref[k:k + 1, :] * dext_ref[pl.ds(CONF_KERNEL - 1 - k, tm), :]
        for k in range(CONF_KERNEL):
            dw_ref[k:k + 1, :] += jnp.sum(dcur * ext_ref[pl.ds(2 + k, tm), :], axis=0, keepdims=True)
        sg = jax.nn.sigmoid(cg_ref[...])
        cvv = cv_ref[...]
        dcv_ref[...] = (dglu * sg).astype(bf16)
        dcg_ref[...] = (dglu * cvv * sg * (1.0 - sg)).astype(bf16)

    row = _row_spec(tm, 1024)
    return pl.pallas_call(
        body, name="conf_conv_bwd", grid=(t // tm,),
        in_specs=[row, _next_halo_spec(tm, 32, 1024, t), row, row, _prev_halo_spec(tm, 32, 1024),
                  _prev_halo_spec(tm, 32, 1024), _const_spec((32, 1024))],
        out_specs=[row, row, _const_spec((32, 1024))],
        out_shape=(jax.ShapeDtypeStruct((t, 1024), bf16), jax.ShapeDtypeStruct((t, 1024), bf16),
                   jax.ShapeDtypeStruct((32, 1024), f32)),
        scratch_shapes=[pltpu.VMEM((tm + 32, 1024), f32), pltpu.VMEM((tm + 32, 1024), f32)],
        compiler_params=_params("arbitrary"),
    )(dv2, dv2, cv, cg, cv, cg, w)


def _ssd_scan_bwd(xbc_act, dt, s_prev, dy, a_log, d_skip):
    t = xbc_act.shape[0]
    nc = t // CHUNK

    def body(xbc_ref, dt_ref, sp_ref, dy_ref, al_ref, dk_ref, dxbc_ref, ddt_ref, dal_ref, ddk_ref, ds_ref):
        i = pl.program_id(0)
        _acc_init(i, ds_ref, dal_ref, ddk_ref)
        consts = _ssd_consts()
        _, vjp_c = jax.vjp(
            functools.partial(_ssd_chunk, consts=consts),
            xbc_ref[:, 0:1024], xbc_ref[:, 1024:1280], xbc_ref[:, 1280:1536], dt_ref[...], sp_ref[...],
            al_ref[...], dk_ref[...])
        dxs, dbm, dcm, ddt, dsp, dal, ddk = vjp_c((dy_ref[...], ds_ref[...]))
        dxbc_ref[:, 0:1024] = dxs
        dxbc_ref[:, 1024:1280] = dbm
        dxbc_ref[:, 1280:1536] = dcm
        ddt_ref[...] = ddt
        ds_ref[...] = dsp
        dal_ref[...] += dal
        ddk_ref[...] += ddk

    rev = lambda i: (nc - 1 - i, 0)
    return pl.pallas_call(
        body, name="ssd_scan_bwd", grid=(nc,),
        in_specs=[pl.BlockSpec((CHUNK, XBC_WIDTH), rev), pl.BlockSpec((CHUNK, SSD_HEADS), rev),
                  pl.BlockSpec((None, SSD_STATE, D_MODEL), lambda i: (nc - 1 - i, 0, 0)),
                  pl.BlockSpec((CHUNK, D_MODEL), rev), _const_spec((1, SSD_HEADS)), _const_spec((1, SSD_HEADS))],
        out_specs=[pl.BlockSpec((CHUNK, XBC_WIDTH), rev), pl.BlockSpec((CHUNK, SSD_HEADS), rev),
                   _const_spec((1, SSD_HEADS)), _const_spec((1, SSD_HEADS))],
        out_shape=(jax.ShapeDtypeStruct((t, XBC_WIDTH), f32), jax.ShapeDtypeStruct((t, SSD_HEADS), f32),
                   jax.ShapeDtypeStruct((1, SSD_HEADS), f32), jax.ShapeDtypeStruct((1, SSD_HEADS), f32)),
        scratch_shapes=[pltpu.VMEM((SSD_STATE, D_MODEL), f32)],
        compiler_params=_params("arbitrary"),
    )(xbc_act, dt, s_prev, dy, a_log, d_skip)


def _ssd_pre_bwd(xbc_raw, dxbc_act, ddt, dt_raw, cw, cb, dt_bias):
    t = xbc_raw.shape[0]
    tm = min(TOKEN_TILE, t)

    def body(cur_ref, halo_ref, dact_ref, ddt_ref, dtr_ref, w_ref, b_ref, dtb_ref,
             dco_ref, ddtr_ref, dw_ref, db_ref, ddtb_ref, ext_ref):
        i = pl.program_id(0)
        _acc_init(i, dw_ref, db_ref, ddtb_ref)
        ext_ref[0:8, :] = jnp.where(i == 0, 0.0, halo_ref[...])
        ext_ref[8:, :] = cur_ref[...]
        co = _conv4_tile(ext_ref, w_ref, b_ref, tm)
        sg = jax.nn.sigmoid(co)
        dco = dact_ref[...] * sg * (1.0 + co * (1.0 - sg))
        dco_ref[...] = dco
        db_ref[...] += jnp.sum(dco, axis=0, keepdims=True)
        for k in range(SSD_CONV):
            dw_ref[k:k + 1, :] += jnp.sum(dco * ext_ref[pl.ds(8 - 3 + k, tm), :], axis=0, keepdims=True)
        ddtr = ddt_ref[...] * jax.nn.sigmoid(dtr_ref[...] + dtb_ref[...])
        ddtb_ref[...] += jnp.sum(ddtr, axis=0, keepdims=True)
        r = lax.broadcasted_iota(jnp.int32, (SSD_HEADS, DT_PAD), 0)
        c = lax.broadcasted_iota(jnp.int32, (SSD_HEADS, DT_PAD), 1)
        ddtr_ref[...] = jnp.dot(ddtr, (r == c).astype(f32), precision=lax.Precision.HIGHEST).astype(bf16)

    return pl.pallas_call(
        body, name="ssd_pre_bwd", grid=(t // tm,),
        in_specs=[_row_spec(tm, XBC_WIDTH), _prev_halo_spec(tm, 8, XBC_WIDTH), _row_spec(tm, XBC_WIDTH),
                  _row_spec(tm, SSD_HEADS), _row_spec(tm, SSD_HEADS), _const_spec((8, XBC_WIDTH)),
                  _const_spec((1, XBC_WIDTH)), _const_spec((1, SSD_HEADS))],
        out_specs=[_row_spec(tm, XBC_WIDTH), _row_spec(tm, DT_PAD), _const_spec((8, XBC_WIDTH)),
                   _const_spec((1, XBC_WIDTH)), _const_spec((1, SSD_HEADS))],
        out_shape=(jax.ShapeDtypeStruct((t, XBC_WIDTH), f32), jax.ShapeDtypeStruct((t, DT_PAD), bf16),
                   jax.ShapeDtypeStruct((8, XBC_WIDTH), f32), jax.ShapeDtypeStruct((1, XBC_WIDTH), f32),
                   jax.ShapeDtypeStruct((1, SSD_HEADS), f32)),
        scratch_shapes=[pltpu.VMEM((tm + 8, XBC_WIDTH), f32)],
        compiler_params=_params("arbitrary"),
    )(xbc_raw, xbc_raw, dxbc_act, ddt, dt_raw, cw, cb, dt_bias)


def _conv4_bwd_data(dco, cw):
    t = dco.shape[0]
    tm = min(TOKEN_TILE, t)

    def body(cur_ref, nxt_ref, w_ref, o_ref, ext_ref):
        i = pl.program_id(0)
        ext_ref[0:tm, :] = cur_ref[...]
        ext_ref[tm:, :] = jnp.where(i == pl.num_programs(0) - 1, 0.0, nxt_ref[...])
        acc = w_ref[0:1, :] * ext_ref[pl.ds(SSD_CONV - 1, tm), :]
        for k in range(1, SSD_CONV):
            acc = acc + w_ref[k:k + 1, :] * ext_ref[pl.ds(SSD_CONV - 1 - k, tm), :]
        o_ref[...] = acc.astype(bf16)

    return pl.pallas_call(
        body, name="conv4_bwd_data", grid=(t // tm,),
        in_specs=[_row_spec(tm, XBC_WIDTH), _next_halo_spec(tm, 8, XBC_WIDTH, t), _const_spec((8, XBC_WIDTH))],
        out_specs=_row_spec(tm, XBC_WIDTH), out_shape=jax.ShapeDtypeStruct((t, XBC_WIDTH), bf16),
        scratch_shapes=[pltpu.VMEM((tm + 8, XBC_WIDTH), f32)],
        compiler_params=_params("arbitrary"),
    )(dco, dco, cw)


def _in_proj_bwd(dz, dxbc, dcv, dcg, ddt, x, dh1, g1, w_in_p):
    t = x.shape[0]
    tm = min(TOKEN_TILE, t)
    segs = (SEG_Z, SEG_XBC, SEG_CV, SEG_CG, SEG_DT)

    def body(dz_ref, dxbc_ref, dcv_ref, dcg_ref, ddt_ref, x_ref, dh_ref, g_ref, w_ref, dx_ref, acc_ref):
        i = pl.program_id(0)
        _acc_init(i, acc_ref)
        du = jnp.zeros((tm, D_MODEL), f32)
        for (lo, hi), r in zip(segs, (dz_ref, dxbc_ref, dcv_ref, dcg_ref, ddt_ref)):
            du = du + _nt(r[...], w_ref[:, lo:hi])
        _, vjp_u = jax.vjp(_rms, x_ref[...], g_ref[...])
        d, dg = vjp_u(du)
        dx_ref[...] = dh_ref[...] + d
        acc_ref[0:1, :] += dg

    row = _row_spec(tm, 1024)
    return pl.pallas_call(
        body, name="in_proj_bwd", grid=(t // tm,),
        in_specs=[row, _row_spec(tm, XBC_WIDTH), row, row, _row_spec(tm, DT_PAD), row, row, _const_spec((1, 1024)),
                  _const_spec((D_MODEL, IN_PERM_WIDTH))],
        out_specs=[row, _const_spec((8, 1024))],
        out_shape=(jax.ShapeDtypeStruct((t, 1024), f32), jax.ShapeDtypeStruct((8, 1024), f32)),
        compiler_params=_params("arbitrary"),
    )(dz, dxbc, dcv, dcg, ddt, x, dh1, g1, w_in_p)


def _mm_tn(a, b, tk, tn, name):
    t, kk = a.shape
    n = b.shape[1]
    tk, tn = min(tk, kk), min(tn, n)

    def body(a_ref, b_ref, o_ref):
        o_ref[...] = _tn(a_ref[...], b_ref[...]).astype(bf16)

    return pl.pallas_call(
        body, name=name, grid=(kk // tk, n // tn),
        in_specs=[pl.BlockSpec((t, tk), lambda i, j: (0, i)), pl.BlockSpec((t, tn), lambda i, j: (0, j))],
        out_specs=pl.BlockSpec((tk, tn), lambda i, j: (i, j)),
        out_shape=jax.ShapeDtypeStruct((kk, n), bf16),
        compiler_params=_params("arbitrary", "arbitrary"),
    )(a, b)


_REF_DT = (2560, 2576)


def _perm_w_in(w_full):
    pad = jnp.zeros((w_full.shape[0], DT_PAD - SSD_HEADS), w_full.dtype)
    return jnp.concatenate([w_full[:, :_REF_DT[0]], w_full[:, _REF_DT[1]:], w_full[:, _REF_DT[0]:_REF_DT[1]], pad], axis=1)


def _unperm_dw_in(segs):
    dz, dxbc, dcv, dcg, ddt = segs
    return jnp.concatenate([dz, dxbc, ddt[:, :SSD_HEADS], dcv, dcg], axis=1)


def _local_step(x, p, target, w, s):
    u, z, xbc_raw, cv, cg, dt_raw = _in_proj(x, s["mix_norm_g"], w["w_in"])
    xbc_act, dt = _ssd_pre(xbc_raw, dt_raw, w["ssd_conv_w"], s["ssd_conv_b"], s["dt_bias"])
    y, s_prev = _ssd_scan(xbc_act, dt, s["A_log"], s["D_skip"])
    v2 = _conf_conv(cv, cg, w["conf_dw_w"], s["conf_dw_b"])
    ys, yc, h1, u2 = _out_proj(y, z, v2, x, s["ssd_norm_g"], s["conf_ln_g"], s["conf_ln_b"], w["w_out"], s["mlp_norm_g"])
    pre, hs = _mlp_up(u2, w["w_up"])
    h2 = _mlp_down(h1, hs, w["w_down"])
    dh2, dh2b, dwg, dwp, tail_acc = _tail(h2, p, target, s["ple_gate_norm_g"], w["w_ple_gate"], s["b_ple_gate"],
                                          w["w_ple"], s["ple_norm_g"], s["final_norm_g"])
    dpre, dh1, dh1b, mlp_acc = _mlp_bwd(dh2, dh2b, pre, h1, s["mlp_norm_g"], w["w_down"], w["w_up"])
    dy, dz, dv2, opb_acc = _out_proj_bwd(dh1b, y, z, v2, s["ssd_norm_g"], s["conf_ln_g"], s["conf_ln_b"], w["w_out"])
    dcv, dcg, dconf_w = _conf_conv_bwd(dv2, cv, cg, w["conf_dw_w"])
    dxbc_act, ddt, d_alog, d_dskip = _ssd_scan_bwd(xbc_act, dt, s_prev, dy, s["A_log"], s["D_skip"])
    dco, ddt_raw, dconv_w, dconv_b, d_dtb = _ssd_pre_bwd(xbc_raw, dxbc_act, ddt, dt_raw, w["ssd_conv_w"],
                                                        s["ssd_conv_b"], s["dt_bias"])
    dxbc_raw = _conv4_bwd_data(dco, w["ssd_conv_w"])
    grad_x, inp_acc = _in_proj_bwd(dz, dxbc_raw, dcv, dcg, ddt_raw, x, dh1, s["mix_norm_g"], w["w_in"])

    gw = {
        "w_in": [_mm_tn(u, d, 1024, 512, "dw_in_" + n) for n, d in
                 (("z", dz), ("xbc", dxbc_raw), ("cv", dcv), ("cg", dcg), ("dt", ddt_raw))],
        "w_out": [_mm_tn(ys, dh1b, 1024, 512, "dw_out_ssd"), _mm_tn(yc, dh1b, 1024, 512, "dw_out_conf")],
        "w_up": _mm_tn(u2, dpre, 1024, 512, "dw_up"),
        "w_down": _mm_tn(hs, dh2b, 512, 1024, "dw_down"),
        "w_ple_gate": dwg,
        "w_ple": dwp,
        "ssd_conv_w": dconv_w,
        "conf_dw_w": dconf_w,
    }
    gs = {
        "loss": tail_acc[TAIL_LOSS:TAIL_LOSS + 1, 0:1],
        "mix_norm_g": inp_acc[0:1], "ssd_conv_b": dconv_b, "dt_bias": d_dtb, "A_log": d_alog, "D_skip": d_dskip,
        "ssd_norm_g": opb_acc[OPB_SSD_G:OPB_SSD_G + 1], "conf_dw_b": opb_acc[OPB_CONV_B:OPB_CONV_B + 1],
        "conf_ln_g": opb_acc[OPB_LN_G:OPB_LN_G + 1], "conf_ln_b": opb_acc[OPB_LN_B:OPB_LN_B + 1],
        "mlp_norm_g": mlp_acc[0:1], "ple_gate_norm_g": tail_acc[TAIL_GATE_NORM_G:TAIL_GATE_NORM_G + 1],
        "b_ple_gate": tail_acc[TAIL_GATE_B:TAIL_GATE_B + 1], "ple_norm_g": tail_acc[TAIL_PLE_G:TAIL_PLE_G + 1],
        "final_norm_g": tail_acc[TAIL_FINAL_G:TAIL_FINAL_G + 1],
    }
    return grad_x, gw, gs


def _mesh_pos():
    return lax.axis_index("x"), lax.axis_index("y"), lax.axis_index("c")


def _other_chips(x, y):
    return [(1 - x, y), (x, 1 - y), (1 - x, 1 - y)]


def _all_gather(arrs, name):
    n = len(arrs)

    def body(*refs):
        ins, outs = refs[:n], refs[n:2 * n]
        send_sems, recv_sems, local_sems = refs[2 * n:]
        x, y, c = _mesh_pos()
        me = 4 * x + 2 * y + c
        sibling = (x, y, 1 - c)
        chips = _other_chips(x, y)

        def copy(a, k, block, to, src=None):
            dst = outs[a].at[block]
            return pltpu.make_async_remote_copy(
                src_ref=dst if src is None else src, dst_ref=dst, send_sem=send_sems.at[a, k],
                recv_sem=recv_sems.at[a, k], device_id=to, device_id_type=MESH_ID)

        mine = [pltpu.make_async_copy(ins[a], outs[a].at[me], local_sems.at[a]) for a in range(n)]
        for cp in mine:
            cp.start()
        first = []
        for a in range(n):
            first.append(copy(a, 0, me, sibling, src=ins[a]))
            first += [copy(a, 1 + j, me, (px, py, c), src=ins[a]) for j, (px, py) in enumerate(chips)]
        for cp in first:
            cp.start()
        passed = []
        for j, (px, py) in enumerate(chips):
            for a in range(n):
                blk = 4 * px + 2 * py + c
                copy(a, 1 + j, blk, (x, y, c)).wait_recv()
                cp = copy(a, 4 + j, blk, sibling)
                cp.start()
                passed.append(cp)
        for a in range(n):
            copy(a, 0, 4 * x + 2 * y + (1 - c), (x, y, c)).wait_recv()
        for j, (px, py) in enumerate(chips):
            for a in range(n):
                copy(a, 4 + j, 4 * px + 2 * py + (1 - c), (x, y, c)).wait_recv()
        for cp in first + passed:
            cp.wait_send()
        for cp in mine:
            cp.wait()

    return pl.pallas_call(
        body, name=name,
        in_specs=[HBM_SPEC] * n, out_specs=[HBM_SPEC] * n,
        out_shape=[jax.ShapeDtypeStruct((N_DEV,) + a.shape, a.dtype) for a in arrs],
        scratch_shapes=[pltpu.SemaphoreType.DMA((n, 7)), pltpu.SemaphoreType.DMA((n, 7)), pltpu.SemaphoreType.DMA((n,))],
    )(*arrs)


def _exchange_sibling(gs):
    n = len(gs)

    def body(*refs):
        ins, outs = refs[:n], refs[n:2 * n]
        send_sems, recv_sems = refs[2 * n:]
        x, y, c = _mesh_pos()
        copies = []
        for a in range(n):
            for j in range(4):
                copies.append(pltpu.make_async_remote_copy(
                    src_ref=ins[a].at[2 * j + (1 - c)], dst_ref=outs[a].at[j], send_sem=send_sems.at[a, j],
                    recv_sem=recv_sems.at[a, j], device_id=(x, y, 1 - c), device_id_type=MESH_ID))
        for cp in copies:
            cp.start()
        for cp in copies:
            cp.wait_recv()
        for cp in copies:
            cp.wait_send()

    return pl.pallas_call(
        body, name="rs_sibling",
        in_specs=[HBM_SPEC] * n, out_specs=[HBM_SPEC] * n,
        out_shape=[jax.ShapeDtypeStruct((4,) + g.shape[1:], g.dtype) for g in gs],
        scratch_shapes=[pltpu.SemaphoreType.DMA((n, 4)), pltpu.SemaphoreType.DMA((n, 4))],
    )(*gs)


def _chip_sum(g, r, core, name):
    _, rows, cols = g.shape
    tr = 256 if rows % 256 == 0 else rows

    def body(c_ref, g_ref, r_ref, o_ref):
        o_ref[...] = (g_ref[...].astype(f32) + r_ref[...].astype(f32)).astype(o_ref.dtype)

    grid_spec = pltpu.PrefetchScalarGridSpec(
        num_scalar_prefetch=1, grid=(4, rows // tr),
        in_specs=[pl.BlockSpec((None, tr, cols), lambda j, i, c_ref: (2 * j + c_ref[0], i, 0)),
                  pl.BlockSpec((None, tr, cols), lambda j, i, c_ref: (j, i, 0))],
        out_specs=pl.BlockSpec((None, tr, cols), lambda j, i, c_ref: (j, i, 0)))
    return pl.pallas_call(
        body, name=name, grid_spec=grid_spec, out_shape=jax.ShapeDtypeStruct((4, rows, cols), g.dtype),
        compiler_params=_params("arbitrary", "arbitrary"),
    )(core, g, r)


def _exchange_chips(cs):
    n = len(cs)

    def body(*refs):
        ins, outs = refs[:n], refs[n:2 * n]
        send_sems, recv_sems, local_sems = refs[2 * n:]
        x, y, c = _mesh_pos()
        my_chip = 2 * x + y
        chips = _other_chips(x, y)
        mine = [pltpu.make_async_copy(ins[a].at[my_chip], outs[a].at[my_chip], local_sems.at[a]) for a in range(n)]
        for cp in mine:
            cp.start()
        sends, recvs = [], []
        for a in range(n):
            for k, (px, py) in enumerate(chips):
                sends.append(pltpu.make_async_remote_copy(
                    src_ref=ins[a].at[2 * px + py], dst_ref=outs[a].at[my_chip], send_sem=send_sems.at[a, k],
                    recv_sem=recv_sems.at[a, k], device_id=(px, py, c), device_id_type=MESH_ID))
                recvs.append(pltpu.make_async_remote_copy(
                    src_ref=ins[a].at[my_chip], dst_ref=outs[a].at[2 * px + py], send_sem=send_sems.at[a, k],
                    recv_sem=recv_sems.at[a, k], device_id=(px, py, c), device_id_type=MESH_ID))
        for cp in sends:
            cp.start()
        for cp in recvs:
            cp.wait_recv()
        for cp in sends:
            cp.wait_send()
        for cp in mine:
            cp.wait()

    return pl.pallas_call(
        body, name="rs_chips",
        in_specs=[HBM_SPEC] * n, out_specs=[HBM_SPEC] * n,
        out_shape=[jax.ShapeDtypeStruct(g.shape, g.dtype) for g in cs],
        scratch_shapes=[pltpu.SemaphoreType.DMA((n, 3)), pltpu.SemaphoreType.DMA((n, 3)), pltpu.SemaphoreType.DMA((n,))],
    )(*cs)


def _adamw_math(w, g, m, v):
    m = ADAM_B1 * m + (1.0 - ADAM_B1) * g
    v = ADAM_B2 * v + (1.0 - ADAM_B2) * (g * g)
    m_hat = m / (1.0 - ADAM_B1 ** ADAM_STEP)
    v_hat = v / (1.0 - ADAM_B2 ** ADAM_STEP)
    delta = -ADAM_LR * (m_hat / (jnp.sqrt(v_hat) + ADAM_EPS) + ADAM_WD * w)
    return delta, m, v


def _adamw_big(parts, w, m, v, name):
    rows, cols = w.shape
    tr = 256 if rows % 256 == 0 else rows

    def body(p_ref, w_ref, m_ref, v_ref, g_ref, d_ref, mo_ref, vo_ref):
        g = p_ref[0].astype(f32)
        for j in range(1, 4):
            g = g + p_ref[j].astype(f32)
        d, mn, vn = _adamw_math(w_ref[...], g, m_ref[...], v_ref[...])
        g_ref[...] = g
        d_ref[...] = d
        mo_ref[...] = mn
        vo_ref[...] = vn

    row = pl.BlockSpec((tr, cols), lambda i: (i, 0))
    shp = jax.ShapeDtypeStruct((rows, cols), f32)
    return pl.pallas_call(
        body, name=name, grid=(rows // tr,),
        in_specs=[pl.BlockSpec((4, tr, cols), lambda i: (0, i, 0)), row, row, row],
        out_specs=[row, row, row, row], out_shape=(shp, shp, shp, shp),
        compiler_params=_params("arbitrary"),
    )(parts, w, m, v)


def _sum_devices(parts):
    _, rows, cols = parts.shape

    def body(p_ref, o_ref):
        g = p_ref[0]
        for j in range(1, N_DEV):
            g = g + p_ref[j]
        o_ref[...] = g

    return pl.pallas_call(body, name="sum_small", out_shape=jax.ShapeDtypeStruct((rows, cols), f32))(parts)


def _adamw_small(w, g, m, v):
    def body(w_ref, g_ref, m_ref, v_ref, d_ref, mo_ref, vo_ref):
        d, mn, vn = _adamw_math(w_ref[...], g_ref[...], m_ref[...], v_ref[...])
        d_ref[...] = d
        mo_ref[...] = mn
        vo_ref[...] = vn

    shp = jax.ShapeDtypeStruct(w.shape, f32)
    return pl.pallas_call(body, name="adamw_small", out_shape=(shp, shp, shp))(w, g, m, v)


def _pack_rows(arrs):
    rows = []
    for a in arrs:
        flat = a.reshape(-1).astype(f32)
        pad = (-flat.shape[0]) % 1024
        rows.append(jnp.pad(flat, (0, pad)).reshape(-1, 1024))
    buf = jnp.concatenate(rows, axis=0)
    return jnp.pad(buf, ((0, (-buf.shape[0]) % 8), (0, 0)))


def _unpack_rows(buf, shapes):
    out, r = [], 0
    for shp in shapes:
        size = 1
        for d in shp:
            size *= d
        nr = -(-size // 1024)
        out.append(buf[r:r + nr].reshape(-1)[:size].reshape(shp))
        r += nr
    return out


_REPLICATED = ("mix_norm_g", "ssd_conv_b", "dt_bias", "A_log", "D_skip", "ssd_norm_g", "conf_dw_b", "conf_ln_g",
               "conf_ln_b", "mlp_norm_g", "ple_gate_norm_g", "b_ple_gate", "ple_norm_g", "final_norm_g")
_CONV_WEIGHTS = ("ssd_conv_w", "conf_dw_w")
_BIG = ("w_in", "w_out", "w_up", "w_down", "w_ple_gate", "w_ple")
_WEIGHT_ORDER = ("mix_norm_g", "w_in", "ssd_conv_w", "ssd_conv_b", "dt_bias", "A_log", "D_skip", "ssd_norm_g", "conf_dw_w",
                 "conf_dw_b", "conf_ln_g", "conf_ln_b", "w_out", "mlp_norm_g", "w_up", "w_down", "ple_gate_norm_g",
                 "w_ple_gate", "b_ple_gate", "w_ple", "ple_norm_g", "final_norm_g")


def _blocks_of_columns(a):
    r, c8 = a.shape
    return jnp.transpose(a.reshape(r, N_DEV, c8 // N_DEV), (1, 0, 2))


def _columns_of_blocks(a):
    _, r, c = a.shape
    return jnp.transpose(a, (1, 0, 2)).reshape(r, N_DEV * c)


def kernel(x, p, mix_norm_g, w_in, ssd_conv_w, ssd_conv_b, dt_bias, A_log, D_skip, ssd_norm_g, conf_dw_w, conf_dw_b, conf_ln_g, conf_ln_b, w_out, mlp_norm_g, w_up, w_down, ple_gate_norm_g, w_ple_gate, b_ple_gate, w_ple, ple_norm_g, final_norm_g, loss_target, m_mix_norm_g, m_w_in, m_ssd_conv_w, m_ssd_conv_b, m_dt_bias, m_A_log, m_D_skip, m_ssd_norm_g, m_conf_dw_w, m_conf_dw_b, m_conf_ln_g, m_conf_ln_b, m_w_out, m_mlp_norm_g, m_w_up, m_w_down, m_ple_gate_norm_g, m_w_ple_gate, m_b_ple_gate, m_w_ple, m_ple_norm_g, m_final_norm_g, v_mix_norm_g, v_w_in, v_ssd_conv_w, v_ssd_conv_b, v_dt_bias, v_A_log, v_D_skip, v_ssd_norm_g, v_conf_dw_w, v_conf_dw_b, v_conf_ln_g, v_conf_ln_b, v_w_out, v_mlp_norm_g, v_w_up, v_w_down, v_ple_gate_norm_g, v_w_ple_gate, v_b_ple_gate, v_w_ple, v_ple_norm_g, v_final_norm_g):
    wts = dict(mix_norm_g=mix_norm_g, w_in=w_in, ssd_conv_w=ssd_conv_w, ssd_conv_b=ssd_conv_b, dt_bias=dt_bias, A_log=A_log,
               D_skip=D_skip, ssd_norm_g=ssd_norm_g, conf_dw_w=conf_dw_w, conf_dw_b=conf_dw_b, conf_ln_g=conf_ln_g,
               conf_ln_b=conf_ln_b, w_out=w_out, mlp_norm_g=mlp_norm_g, w_up=w_up, w_down=w_down,
               ple_gate_norm_g=ple_gate_norm_g, w_ple_gate=w_ple_gate, b_ple_gate=b_ple_gate, w_ple=w_ple,
               ple_norm_g=ple_norm_g, final_norm_g=final_norm_g)
    mom1 = dict(mix_norm_g=m_mix_norm_g, w_in=m_w_in, ssd_conv_w=m_ssd_conv_w, ssd_conv_b=m_ssd_conv_b, dt_bias=m_dt_bias,
                A_log=m_A_log, D_skip=m_D_skip, ssd_norm_g=m_ssd_norm_g, conf_dw_w=m_conf_dw_w, conf_dw_b=m_conf_dw_b,
                conf_ln_g=m_conf_ln_g, conf_ln_b=m_conf_ln_b, w_out=m_w_out, mlp_norm_g=m_mlp_norm_g, w_up=m_w_up,
                w_down=m_w_down, ple_gate_norm_g=m_ple_gate_norm_g, w_ple_gate=m_w_ple_gate, b_ple_gate=m_b_ple_gate,
                w_ple=m_w_ple, ple_norm_g=m_ple_norm_g, final_norm_g=m_final_norm_g)
    mom2 = dict(mix_norm_g=v_mix_norm_g, w_in=v_w_in, ssd_conv_w=v_ssd_conv_w, ssd_conv_b=v_ssd_conv_b, dt_bias=v_dt_bias,
                A_log=v_A_log, D_skip=v_D_skip, ssd_norm_g=v_ssd_norm_g, conf_dw_w=v_conf_dw_w, conf_dw_b=v_conf_dw_b,
                conf_ln_g=v_conf_ln_g, conf_ln_b=v_conf_ln_b, w_out=v_w_out, mlp_norm_g=v_mlp_norm_g, w_up=v_w_up,
                w_down=v_w_down, ple_gate_norm_g=v_ple_gate_norm_g, w_ple_gate=v_w_ple_gate, b_ple_gate=v_b_ple_gate,
                w_ple=v_w_ple, ple_norm_g=v_ple_norm_g, final_norm_g=v_final_norm_g)
    xc, yc_, cc = _mesh_pos()
    me = 4 * xc + 2 * yc_ + cc

    shards = [wts[n][0].astype(bf16) for n in _BIG] + [wts[n][0] for n in _CONV_WEIGHTS]
    gathered = dict(zip(_BIG + _CONV_WEIGHTS, _all_gather(shards, "gather_weights")))
    full = {
        "w_in": _perm_w_in(_columns_of_blocks(gathered["w_in"])),
        "w_out": gathered["w_out"].reshape(2048, D_MODEL),
        "w_up": _columns_of_blocks(gathered["w_up"]),
        "w_down": gathered["w_down"].reshape(D_FF, D_MODEL),
        "w_ple_gate": gathered["w_ple_gate"].reshape(D_MODEL, D_MODEL),
        "w_ple": _columns_of_blocks(gathered["w_ple"]),
        "ssd_conv_w": jnp.pad(_columns_of_blocks(gathered["ssd_conv_w"]), ((0, 8 - SSD_CONV), (0, 0))),
        "conf_dw_w": jnp.pad(_columns_of_blocks(gathered["conf_dw_w"]), ((0, 32 - CONF_KERNEL), (0, 0))),
    }
    small = {n: wts[n].reshape(1, -1) for n in _REPLICATED}

    grad_x, gw, gs = _local_step(x[0], p[0, 0], loss_target[0], full, small)

    by_block = [
        _blocks_of_columns(_unperm_dw_in(gw["w_in"])),
        jnp.concatenate(gw["w_out"], axis=0).reshape(N_DEV, 256, D_MODEL),
        _blocks_of_columns(gw["w_up"]),
        gw["w_down"].reshape(N_DEV, 512, D_MODEL),
        gw["w_ple_gate"].astype(bf16).reshape(N_DEV, 128, D_MODEL),
        _blocks_of_columns(gw["w_ple"].astype(bf16)),
    ]
    from_sibling = _exchange_sibling(by_block)
    core = cc.astype(jnp.int32).reshape(1)
    chip_sums = [_chip_sum(g, r, core, "chip_sum_" + n) for n, g, r in zip(_BIG, by_block, from_sibling)]
    parts = dict(zip(_BIG, _exchange_chips(chip_sums)))

    small_names = _REPLICATED + _CONV_WEIGHTS
    small_shapes = [wts[n].shape for n in _REPLICATED] + [(SSD_CONV, XBC_WIDTH), (CONF_KERNEL, D_MODEL)]
    packed = _pack_rows([gs[n] for n in _REPLICATED] + [gw["ssd_conv_w"][:SSD_CONV], gw["conf_dw_w"][:CONF_KERNEL], gs["loss"]])
    (all_small,) = _all_gather([packed], "gather_small")
    summed = _unpack_rows(_sum_devices(all_small), small_shapes + [(1,)])
    loss = summed[-1][0]
    grads = dict(zip(small_names, summed[:-1]))
    grads["ssd_conv_w"] = lax.dynamic_slice_in_dim(grads["ssd_conv_w"], me * 192, 192, axis=1)[None]
    grads["conf_dw_w"] = lax.dynamic_slice_in_dim(grads["conf_dw_w"], me * 128, 128, axis=1)[None]

    delta, new_m, new_v = {}, {}, {}
    for n in _BIG:
        shp = wts[n].shape
        g, d, mn, vn = _adamw_big(parts[n], wts[n][0], mom1[n][0], mom2[n][0], "adamw_" + n)
        grads[n], delta[n], new_m[n], new_v[n] = g.reshape(shp), d.reshape(shp), mn.reshape(shp), vn.reshape(shp)
    shapes = [wts[n].shape for n in small_names]
    d_s, m_s, v_s = _adamw_small(_pack_rows([wts[n] for n in small_names]), _pack_rows([grads[n] for n in small_names]),
                                 _pack_rows([mom1[n] for n in small_names]), _pack_rows([mom2[n] for n in small_names]))
    delta.update(zip(small_names, _unpack_rows(d_s, shapes)))
    new_m.update(zip(small_names, _unpack_rows(m_s, shapes)))
    new_v.update(zip(small_names, _unpack_rows(v_s, shapes)))

    return (loss, grad_x[None], *[grads[n] for n in _WEIGHT_ORDER], *[delta[n] for n in _WEIGHT_ORDER],
            *[new_m[n] for n in _WEIGHT_ORDER], *[new_v[n] for n in _WEIGHT_ORDER])
```

```python
import functools

import jax
import jax.numpy as jnp
from jax import lax
from jax.experimental import pallas as pl
from jax.experimental.pallas import tpu as pltpu

f32 = jnp.float32
bf16 = jnp.bfloat16

EPS = 1e-6
D_MODEL = 1024
SSD_HEADS = 16
SSD_HEAD_DIM = 64
SSD_GROUPS = 2
SSD_STATE = 128
SSD_CONV = 4
CHUNK = 128
XBC_WIDTH = 1536
CONF_KERNEL = 31
D_FF = 4096
PLE_DIM = 256
IN_WIDTH = 4624
N_DEV = 8
DT_PAD = 128
SEG_Z = (0, 1024)
SEG_XBC = (1024, 2560)
SEG_CV = (2560, 3584)
SEG_CG = (3584, 4608)
SEG_DT = (4608, 4736)
IN_PERM_WIDTH = 4736

ADAM_LR = 0.001
ADAM_B1 = 0.9
ADAM_B2 = 0.999
ADAM_EPS = 1e-08
ADAM_WD = 0.01
ADAM_STEP = 10

VMEM_LIMIT_BYTES = 56 * 1024 * 1024
TOKEN_TILE = 256
SMALL_ROWS = 16

HBM_SPEC = pl.BlockSpec(memory_space=pltpu.HBM)
MESH_ID = pl.DeviceIdType.MESH


def _params(*semantics):
    return pltpu.CompilerParams(dimension_semantics=semantics, vmem_limit_bytes=VMEM_LIMIT_BYTES)


def _nn(a, b):
    return lax.dot_general(a, b, (((1,), (0,)), ((), ())), preferred_element_type=f32)


def _nt(a, b):
    return lax.dot_general(a, b, (((1,), (1,)), ((), ())), preferred_element_type=f32)


def _tn(a, b):
    return lax.dot_general(a, b, (((0,), (0,)), ((), ())), preferred_element_type=f32)


@jax.custom_vjp
def bnn(a, b):
    return _nn(a.astype(bf16), b.astype(bf16))


def _bnn_fwd(a, b):
    ab, bb = a.astype(bf16), b.astype(bf16)
    return _nn(ab, bb), (ab, bb)


def _bnn_bwd(res, g):
    ab, bb = res
    gb = g.astype(bf16)
    return _nt(gb, bb), _tn(ab, gb)


bnn.defvjp(_bnn_fwd, _bnn_bwd)


@jax.custom_vjp
def bnt(a, b):
    return _nt(a.astype(bf16), b.astype(bf16))


def _bnt_fwd(a, b):
    ab, bb = a.astype(bf16), b.astype(bf16)
    return _nt(ab, bb), (ab, bb)


def _bnt_bwd(res, g):
    ab, bb = res
    gb = g.astype(bf16)
    return _nn(gb, bb), _tn(gb, ab)


bnt.defvjp(_bnt_fwd, _bnt_bwd)


@jax.custom_vjp
def btn(a, b):
    return _tn(a.astype(bf16), b.astype(bf16))


def _btn_fwd(a, b):
    ab, bb = a.astype(bf16), b.astype(bf16)
    return _tn(ab, bb), (ab, bb)


def _btn_bwd(res, g):
    ab, bb = res
    gb = g.astype(bf16)
    return _nt(bb, gb), _nn(ab, gb)


btn.defvjp(_btn_fwd, _btn_bwd)


def _rms(x, g):
    return x * lax.rsqrt(jnp.mean(x * x, axis=-1, keepdims=True) + EPS) * g


def _gated_norm(y, z, g):
    v = y * jax.nn.silu(z)
    half = v.shape[-1] // SSD_GROUPS
    parts = []
    for k in range(SSD_GROUPS):
        vk = v[:, k * half:(k + 1) * half]
        parts.append(vk * lax.rsqrt(jnp.mean(vk * vk, axis=-1, keepdims=True) + EPS))
    return jnp.concatenate(parts, axis=-1) * g


def _ln_silu(v, g, b):
    mu = jnp.mean(v, axis=-1, keepdims=True)
    xc = v - mu
    y = xc * lax.rsqrt(jnp.mean(xc * xc, axis=-1, keepdims=True) + EPS) * g + b
    return jax.nn.silu(y)


def _acc_init(i, *refs):
    @pl.when(i == 0)
    def _():
        for r in refs:
            r[...] = jnp.zeros_like(r)


def _after_token(body, token):
    if token is None:
        return body, [], []

    def body_after(tok_ref, *refs):
        del tok_ref
        body(*refs)

    return body_after, [token], [pl.BlockSpec(memory_space=pl.ANY)]


def _row_spec(tm, n):
    return pl.BlockSpec((tm, n), lambda i: (i, 0))


def _const_spec(shape):
    nd = len(shape)
    return pl.BlockSpec(shape, lambda i: (0,) * nd)


def _prev_halo_spec(tm, halo, n):
    return pl.BlockSpec((halo, n), lambda i: (jnp.maximum(i * (tm // halo) - 1, 0), 0))


def _next_halo_spec(tm, halo, n, t):
    return pl.BlockSpec((halo, n), lambda i: (jnp.minimum((i + 1) * (tm // halo), t // halo - 1), 0))


def _in_proj(x, g1, w_in_p, token=None):
    t = x.shape[0]
    tm = min(TOKEN_TILE, t)
    segs = (SEG_Z, SEG_XBC, SEG_CV, SEG_CG, SEG_DT)

    def body(x_ref, g_ref, w_ref, u_ref, z_ref, xbc_ref, cv_ref, cg_ref, dt_ref):
        u = _rms(x_ref[...], g_ref[...]).astype(bf16)
        u_ref[...] = u
        for (lo, hi), o_ref in zip(segs[:4], (z_ref, xbc_ref, cv_ref, cg_ref)):
            o_ref[...] = _nn(u, w_ref[:, lo:hi])
        dt_ref[...] = _nn(u, w_ref[:, SEG_DT[0]:SEG_DT[1]])[:, :SSD_HEADS]

    outs = (jax.ShapeDtypeStruct((t, D_MODEL), bf16), jax.ShapeDtypeStruct((t, 1024), f32),
            jax.ShapeDtypeStruct((t, XBC_WIDTH), f32), jax.ShapeDtypeStruct((t, 1024), f32),
            jax.ShapeDtypeStruct((t, 1024), f32), jax.ShapeDtypeStruct((t, SSD_HEADS), f32))
    body, tok, tok_spec = _after_token(body, token)
    return pl.pallas_call(
        body, name="in_proj", grid=(t // tm,),
        in_specs=tok_spec + [_row_spec(tm, D_MODEL), _const_spec((1, D_MODEL)), _const_spec((D_MODEL, IN_PERM_WIDTH))],
        out_specs=[_row_spec(tm, D_MODEL), _row_spec(tm, 1024), _row_spec(tm, XBC_WIDTH), _row_spec(tm, 1024),
                   _row_spec(tm, 1024), _row_spec(tm, SSD_HEADS)],
        out_shape=outs, compiler_params=_params("arbitrary"),
    )(*tok, x, g1, w_in_p)


def _conv4_tile(ext_ref, w_ref, b_ref, tm):
    acc = b_ref[...] + w_ref[0:1, :] * ext_ref[pl.ds(8 - 3, tm), :]
    for k in range(1, SSD_CONV):
        acc = acc + w_ref[k:k + 1, :] * ext_ref[pl.ds(8 - 3 + k, tm), :]
    return acc


def _ssd_pre(xbc_raw, dt_raw, cw, cb, dt_bias):
    t = xbc_raw.shape[0]
    tm = min(TOKEN_TILE, t)

    def body(cur_ref, halo_ref, dt_ref, w_ref, b_ref, dtb_ref, act_ref, dto_ref, ext_ref):
        i = pl.program_id(0)
        ext_ref[0:8, :] = jnp.where(i == 0, 0.0, halo_ref[...])
        ext_ref[8:, :] = cur_ref[...]
        act_ref[...] = jax.nn.silu(_conv4_tile(ext_ref, w_ref, b_ref, tm))
        dto_ref[...] = jax.nn.softplus(dt_ref[...] + dtb_ref[...])

    return pl.pallas_call(
        body, name="ssd_pre", grid=(t // tm,),
        in_specs=[_row_spec(tm, XBC_WIDTH), _prev_halo_spec(tm, 8, XBC_WIDTH), _row_spec(tm, SSD_HEADS),
                  _const_spec((8, XBC_WIDTH)), _const_spec((1, XBC_WIDTH)), _const_spec((1, SSD_HEADS))],
        out_specs=[_row_spec(tm, XBC_WIDTH), _row_spec(tm, SSD_HEADS)],
        out_shape=(jax.ShapeDtypeStruct((t, XBC_WIDTH), f32), jax.ShapeDtypeStruct((t, SSD_HEADS), f32)),
        scratch_shapes=[pltpu.VMEM((tm + 8, XBC_WIDTH), f32)],
        compiler_params=_params("arbitrary"),
    )(xbc_raw, xbc_raw, dt_raw, cw, cb, dt_bias)


def _ssd_consts():
    r = lax.broadcasted_iota(jnp.int32, (CHUNK, CHUNK), 0)
    c = lax.broadcasted_iota(jnp.int32, (CHUNK, CHUNK), 1)
    causal = r >= c
    tril = causal.astype(f32)
    triu = (r <= c).astype(f32)
    hh = lax.broadcasted_iota(jnp.int32, (SSD_HEADS, D_MODEL), 0)
    jj = lax.broadcasted_iota(jnp.int32, (SSD_HEADS, D_MODEL), 1)
    expand = (lax.shift_right_logical(jj, 6) == hh).astype(f32)
    lane = lax.broadcasted_iota(jnp.int32, (1, 2 * SSD_HEAD_DIM), 1)
    m0 = (lane < SSD_HEAD_DIM).astype(f32)
    return causal, tril, triu, expand, m0, 1.0 - m0


def _ssd_chunk(xs, bm, cm, dt, s_prev, a_log, d_skip, consts):
    causal, tril, triu, expand, m0, m1 = consts
    hp = lax.Precision.HIGHEST
    a = dt * (-jnp.exp(a_log))
    cs = jnp.dot(tril, a, precision=hp)
    cs_t = lax.dot_general(a, triu, (((0,), (0,)), ((), ())), precision=hp)
    dt_e = jnp.dot(dt, expand, precision=hp)
    cs_e = jnp.dot(cs, expand, precision=hp)
    d_e = jnp.dot(jnp.broadcast_to(d_skip, (8, SSD_HEADS)), expand, precision=hp)[0:1, :]
    xc = xs * dt_e
    cs_last = cs_e[CHUNK - 1:CHUNK, :]
    x_dec = xc * jnp.exp(cs_last - cs_e)
    dec_start = jnp.exp(cs_e)
    chunk_dec = jnp.exp(cs_last)
    gw = D_MODEL // SSD_GROUPS
    ys, states = [], []
    for g in range(SSD_GROUPS):
        bg = bm[:, g * SSD_STATE:(g + 1) * SSD_STATE]
        cg = cm[:, g * SSD_STATE:(g + 1) * SSD_STATE]
        sp = s_prev[:, g * gw:(g + 1) * gw]
        states.append(sp * chunk_dec[:, g * gw:(g + 1) * gw] + btn(bg, x_dec[:, g * gw:(g + 1) * gw]))
        y_off = bnn(cg, sp) * dec_start[:, g * gw:(g + 1) * gw]
        scores = bnt(cg, bg)
        pieces = []
        for pr in range(gw // (2 * SSD_HEAD_DIM)):
            lo = g * gw + pr * 2 * SSD_HEAD_DIM
            xp = xc[:, lo:lo + 2 * SSD_HEAD_DIM]
            acc = None
            for h, mk in ((lo // SSD_HEAD_DIM, m0), (lo // SSD_HEAD_DIM + 1, m1)):
                seg = cs[:, h:h + 1] - cs_t[h:h + 1, :]
                mat = scores * jnp.exp(jnp.where(causal, seg, -jnp.inf))
                term = bnn(mat, xp * mk)
                acc = term if acc is None else acc + term
            pieces.append(acc)
        ys.append(jnp.concatenate(pieces, axis=-1) + y_off)
    y = jnp.concatenate(ys, axis=-1) + xs * d_e
    return y, jnp.concatenate(states, axis=-1)


def _ssd_scan(xbc_act, dt, a_log, d_skip):
    t = xbc_act.shape[0]
    nc = t // CHUNK

    def body(xbc_ref, dt_ref, al_ref, dk_ref, y_ref, sp_ref, state_ref):
        c = pl.program_id(0)

        @pl.when(c == 0)
        def _():
            state_ref[...] = jnp.zeros_like(state_ref)

        s_prev = state_ref[...]
        sp_ref[...] = s_prev
        y, s_next = _ssd_chunk(xbc_ref[:, 0:1024], xbc_ref[:, 1024:1280], xbc_ref[:, 1280:1536], dt_ref[...],
                               s_prev, al_ref[...], dk_ref[...], _ssd_consts())
        y_ref[...] = y
        state_ref[...] = s_next

    return pl.pallas_call(
        body, name="ssd_scan", grid=(nc,),
        in_specs=[_row_spec(CHUNK, XBC_WIDTH), _row_spec(CHUNK, SSD_HEADS), _const_spec((1, SSD_HEADS)),
                  _const_spec((1, SSD_HEADS))],
        out_specs=[_row_spec(CHUNK, D_MODEL), pl.BlockSpec((None, SSD_STATE, D_MODEL), lambda c: (c, 0, 0))],
        out_shape=(jax.ShapeDtypeStruct((t, D_MODEL), f32), jax.ShapeDtypeStruct((nc, SSD_STATE, D_MODEL), f32)),
        scratch_shapes=[pltpu.VMEM((SSD_STATE, D_MODEL), f32)],
        compiler_params=_params("arbitrary"),
    )(xbc_act, dt, a_log, d_skip)


def _fill_glu_ext(ext_ref, cv_ref, cg_ref, hcv_ref, hcg_ref, i):
    ext_ref[0:32, :] = jnp.where(i == 0, 0.0, hcv_ref[...] * jax.nn.sigmoid(hcg_ref[...]))
    ext_ref[32:, :] = cv_ref[...] * jax.nn.sigmoid(cg_ref[...])


def _conf_conv(cv, cg, w, b):
    t = cv.shape[0]
    tm = min(TOKEN_TILE, t)

    def body(cv_ref, cg_ref, hcv_ref, hcg_ref, w_ref, b_ref, o_ref, ext_ref):
        _fill_glu_ext(ext_ref, cv_ref, cg_ref, hcv_ref, hcg_ref, pl.program_id(0))
        acc = b_ref[...] + w_ref[0:1, :] * ext_ref[pl.ds(2, tm), :]
        for k in range(1, CONF_KERNEL):
            acc = acc + w_ref[k:k + 1, :] * ext_ref[pl.ds(2 + k, tm), :]
        o_ref[...] = acc

    return pl.pallas_call(
        body, name="conf_conv", grid=(t // tm,),
        in_specs=[_row_spec(tm, 1024), _row_spec(tm, 1024), _prev_halo_spec(tm, 32, 1024), _prev_halo_spec(tm, 32, 1024),
                  _const_spec((32, 1024)), _const_spec((1, 1024))],
        out_specs=_row_spec(tm, 1024), out_shape=jax.ShapeDtypeStruct((t, 1024), f32),
        scratch_shapes=[pltpu.VMEM((tm + 32, 1024), f32)],
        compiler_params=_params("arbitrary"),
    )(cv, cg, cv, cg, w, b)


def _out_proj(y, z, v2, x, g_ssd, ln_g, ln_b, w_out, g2):
    t = x.shape[0]
    tm = min(TOKEN_TILE, t)

    def body(y_ref, z_ref, v_ref, x_ref, gs_ref, lg_ref, lb_ref, w_ref, g2_ref, ys_ref, yc_ref, h1_ref, u2_ref):
        ys = _gated_norm(y_ref[...], z_ref[...], gs_ref[...]).astype(bf16)
        yc = _ln_silu(v_ref[...], lg_ref[...], lb_ref[...]).astype(bf16)
        ys_ref[...] = ys
        yc_ref[...] = yc
        h1 = x_ref[...] + _nn(ys, w_ref[0:1024, :]) + _nn(yc, w_ref[1024:2048, :])
        h1_ref[...] = h1
        u2_ref[...] = _rms(h1, g2_ref[...]).astype(bf16)

    vec = _const_spec((1, 1024))
    row = _row_spec(tm, 1024)
    return pl.pallas_call(
        body, name="out_proj", grid=(t // tm,),
        in_specs=[row, row, row, row, vec, vec, vec, _const_spec((2048, 1024)), vec],
        out_specs=[row, row, row, row],
        out_shape=(jax.ShapeDtypeStruct((t, 1024), bf16), jax.ShapeDtypeStruct((t, 1024), bf16),
                   jax.ShapeDtypeStruct((t, 1024), f32), jax.ShapeDtypeStruct((t, 1024), bf16)),
        compiler_params=_params("arbitrary"),
    )(y, z, v2, x, g_ssd, ln_g, ln_b, w_out, g2)


def _mlp_up(u2, w_up):
    t = u2.shape[0]
    tm = min(TOKEN_TILE, t)
    blk = D_FF // N_DEV

    def body(u_ref, w_ref, pre_ref, hs_ref):
        u = u_ref[...]
        for k in range(N_DEV):
            pre = _nn(u, w_ref[k])
            pre_ref[:, k * blk:(k + 1) * blk] = pre.astype(bf16)
            r = jnp.maximum(pre, 0.0)
            hs_ref[:, k * blk:(k + 1) * blk] = (r * r).astype(bf16)

    return pl.pallas_call(
        body, name="mlp_up", grid=(t // tm,),
        in_specs=[_row_spec(tm, D_MODEL), _const_spec((N_DEV, D_MODEL, D_FF // N_DEV))],
        out_specs=[_row_spec(tm, D_FF), _row_spec(tm, D_FF)],
        out_shape=(jax.ShapeDtypeStruct((t, D_FF), bf16), jax.ShapeDtypeStruct((t, D_FF), bf16)),
        compiler_params=_params("arbitrary"),
    )(u2, w_up)


def _mlp_down(h1, hs, w_down):
    t = h1.shape[0]
    tm = min(TOKEN_TILE, t)

    def body(h_ref, hs_ref, w_ref, o_ref):
        o_ref[...] = h_ref[...] + _nn(hs_ref[...], w_ref[...])

    return pl.pallas_call(
        body, name="mlp_down", grid=(t // tm,),
        in_specs=[_row_spec(tm, D_MODEL), _row_spec(tm, D_FF), _const_spec((D_FF, D_MODEL))],
        out_specs=_row_spec(tm, D_MODEL), out_shape=jax.ShapeDtypeStruct((t, D_MODEL), f32),
        compiler_params=_params("arbitrary"),
    )(h1, hs, w_down)


TAIL_LOSS, TAIL_FINAL_G, TAIL_PLE_G, TAIL_GATE_B, TAIL_GATE_NORM_G = 0, 1, 2, 3, 4


def _tail(h2, p, target, g3, w_gate, b_gate, w_ple, ple_g, fin_g):
    t = h2.shape[0]
    tm = min(TOKEN_TILE, t)

    def body(h_ref, p_ref, t_ref, g3_ref, wg_ref, bg_ref, wp_ref, pg_ref, fg_ref,
             dh_ref, dhb_ref, dwg_ref, dwp_ref, acc_ref):
        i = pl.program_id(0)
        _acc_init(i, dwg_ref, dwp_ref, acc_ref)
        h2v = h_ref[...]
        tgt = t_ref[...]
        u3, vjp_u3 = jax.vjp(_rms, h2v, g3_ref[...])
        u3b = u3.astype(bf16)
        pb = p_ref[...].astype(bf16)
        gate_pre = _nn(u3b, wg_ref[...]) + bg_ref[...]
        emb_pre = _nn(pb, wp_ref[...])

        def tail_fn(hv, gp, ep, pg, fg):
            h3 = hv + jax.nn.sigmoid(gp) * _rms(ep, pg)
            err = _rms(h3, fg) - tgt
            return 0.5 * jnp.mean(err * err, axis=-1, keepdims=True)

        loss_tok, vjp_tail = jax.vjp(tail_fn, h2v, gate_pre, emb_pre, pg_ref[...], fg_ref[...])
        dh_a, dgp, dep, dpg, dfg = vjp_tail(jnp.ones_like(loss_tok))
        dgpb = dgp.astype(bf16)
        dh_b, dg3 = vjp_u3(_nt(dgpb, wg_ref[...]))
        dh = dh_a + dh_b
        dh_ref[...] = dh
        dhb_ref[...] = dh.astype(bf16)
        dwg_ref[...] += _tn(u3b, dgpb)
        dwp_ref[...] += _tn(pb, dep.astype(bf16))
        acc_ref[TAIL_LOSS:TAIL_LOSS + 1, :] += jnp.broadcast_to(jnp.sum(loss_tok, axis=0, keepdims=True), (1, 1024))
        acc_ref[TAIL_FINAL_G:TAIL_FINAL_G + 1, :] += dfg
        acc_ref[TAIL_PLE_G:TAIL_PLE_G + 1, :] += dpg
        acc_ref[TAIL_GATE_B:TAIL_GATE_B + 1, :] += jnp.sum(dgp, axis=0, keepdims=True)
        acc_ref[TAIL_GATE_NORM_G:TAIL_GATE_NORM_G + 1, :] += dg3

    vec = _const_spec((1, 1024))
    row = _row_spec(tm, 1024)
    return pl.pallas_call(
        body, name="tail", grid=(t // tm,),
        in_specs=[row, _row_spec(tm, PLE_DIM), row, vec, _const_spec((1024, 1024)), vec, _const_spec((PLE_DIM, 1024)), vec, vec],
        out_specs=[row, row, _const_spec((1024, 1024)), _const_spec((PLE_DIM, 1024)), _const_spec((8, 1024))],
        out_shape=(jax.ShapeDtypeStruct((t, 1024), f32), jax.ShapeDtypeStruct((t, 1024), bf16),
                   jax.ShapeDtypeStruct((1024, 1024), f32), jax.ShapeDtypeStruct((PLE_DIM, 1024), f32),
                   jax.ShapeDtypeStruct((8, 1024), f32)),
        compiler_params=_params("arbitrary"),
    )(h2, p, target, g3, w_gate, b_gate, w_ple, ple_g, fin_g)


def _mlp_bwd(dh2, dh2b, pre, h1, g2, w_down, w_up, token=None):
    t = dh2.shape[0]
    tm = min(TOKEN_TILE, t)
    blk = D_FF // N_DEV

    def body(dh_ref, dhb_ref, pre_ref, h1_ref, g_ref, wd_hbm, wu_hbm, dpre_ref, dh1_ref, dh1b_ref, acc_ref,
             wd_ref, wu_ref):
        i = pl.program_id(0)
        _acc_init(i, acc_ref)

        @pl.when(i == 0)
        def _():
            pltpu.sync_copy(wd_hbm, wd_ref)
            pltpu.sync_copy(wu_hbm, wu_ref)

        dhb = dhb_ref[...]
        du2 = jnp.zeros((tm, D_MODEL), f32)
        for k in range(D_FF // blk):
            dhs = _nt(dhb, wd_ref[k * blk:(k + 1) * blk, :])
            dpre = (dhs * (2.0 * jnp.maximum(pre_ref[:, k * blk:(k + 1) * blk].astype(f32), 0.0))).astype(bf16)
            dpre_ref[:, k * blk:(k + 1) * blk] = dpre
            du2 = du2 + _nt(dpre, wu_ref[k])
        _, vjp_u2 = jax.vjp(_rms, h1_ref[...], g_ref[...])
        d, dg = vjp_u2(du2)
        dh1 = dh_ref[...] + d
        dh1_ref[...] = dh1
        dh1b_ref[...] = dh1.astype(bf16)
        acc_ref[0:1, :] += dg

    row = _row_spec(tm, 1024)
    body, tok, tok_spec = _after_token(body, token)
    return pl.pallas_call(
        body, name="mlp_bwd", grid=(t // tm,),
        in_specs=tok_spec + [row, row, _row_spec(tm, D_FF), row, _const_spec((1, 1024)), HBM_SPEC, HBM_SPEC],
        out_specs=[_row_spec(tm, D_FF), row, row, _const_spec((8, 1024))],
        out_shape=(jax.ShapeDtypeStruct((t, D_FF), bf16), jax.ShapeDtypeStruct((t, 1024), f32),
                   jax.ShapeDtypeStruct((t, 1024), bf16), jax.ShapeDtypeStruct((8, 1024), f32)),
        scratch_shapes=[pltpu.VMEM((D_FF, D_MODEL), bf16), pltpu.VMEM((N_DEV, D_MODEL, D_FF // N_DEV), bf16)],
        compiler_params=_params("arbitrary"),
    )(*tok, dh2, dh2b, pre, h1, g2, w_down, w_up)


OPB_SSD_G, OPB_LN_G, OPB_LN_B, OPB_CONV_B = 0, 1, 2, 3


def _out_proj_bwd(dh1b, y, z, v2, g_ssd, ln_g, ln_b, w_out, token=None):
    t = y.shape[0]
    tm = min(TOKEN_TILE, t)

    def body(dh_ref, y_ref, z_ref, v_ref, gs_ref, lg_ref, lb_ref, w_ref, dy_ref, dz_ref, dv_ref, acc_ref):
        i = pl.program_id(0)
        _acc_init(i, acc_ref)
        dhb = dh_ref[...]
        dys = _nt(dhb, w_ref[0:1024, :])
        dyc = _nt(dhb, w_ref[1024:2048, :])
        _, vjp_g = jax.vjp(_gated_norm, y_ref[...], z_ref[...], gs_ref[...])
        dy, dz, dgs = vjp_g(dys)
        _, vjp_l = jax.vjp(_ln_silu, v_ref[...], lg_ref[...], lb_ref[...])
        dv, dlg, dlb = vjp_l(dyc)
        dy_ref[...] = dy
        dz_ref[...] = dz.astype(bf16)
        dv_ref[...] = dv
        acc_ref[OPB_SSD_G:OPB_SSD_G + 1, :] += dgs
        acc_ref[OPB_LN_G:OPB_LN_G + 1, :] += dlg
        acc_ref[OPB_LN_B:OPB_LN_B + 1, :] += dlb
        acc_ref[OPB_CONV_B:OPB_CONV_B + 1, :] += jnp.sum(dv, axis=0, keepdims=True)

    vec = _const_spec((1, 1024))
    row = _row_spec(tm, 1024)
    body, tok, tok_spec = _after_token(body, token)
    return pl.pallas_call(
        body, name="out_proj_bwd", grid=(t // tm,),
        in_specs=tok_spec + [row, row, row, row, vec, vec, vec, _const_spec((2048, 1024))],
        out_specs=[row, row, row, _const_spec((8, 1024))],
        out_shape=(jax.ShapeDtypeStruct((t, 1024), f32), jax.ShapeDtypeStruct((t, 1024), bf16),
                   jax.ShapeDtypeStruct((t, 1024), f32), jax.ShapeDtypeStruct((8, 1024), f32)),
        compiler_params=_params("arbitrary"),
    )(*tok, dh1b, y, z, v2, g_ssd, ln_g, ln_b, w_out)


def _conf_conv_bwd(dv2, cv, cg, w):
    t = cv.shape[0]
    tm = min(TOKEN_TILE, t)

    def body(dv_ref, dvn_ref, cv_ref, cg_ref, hcv_ref, hcg_ref, w_ref, dcv_ref, dcg_ref, dw_ref, ext_ref, dext_ref):
        i = pl.program_id(0)
        _acc_init(i, dw_ref)
        _fill_glu_ext(ext_ref, cv_ref, cg_ref, hcv_ref, hcg_ref, i)
        dcur = dv_ref[...]
        dext_ref[0:tm, :] = dcur
        dext_ref[tm:, :] = jnp.where(i == pl.num_programs(0) - 1, 0.0, dvn_ref[...])
        dglu = w_ref[0:1, :] * dext_ref[pl.ds(CONF_KERNEL - 1, tm), :]
        for k in range(1, CONF_KERNEL):
            dglu = dglu + w_ref[k:k + 1, :] * dext_ref[pl.ds(CONF_KERNEL - 1 - k, tm), :]
        for k in range(CONF_KERNEL):
            dw_ref[k:k + 1, :] += jnp.sum(dcur * ext_ref[pl.ds(2 + k, tm), :], axis=0, keepdims=True)
        sg = jax.nn.sigmoid(cg_ref[...])
        cvv = cv_ref[...]
        dcv_ref[...] = (dglu * sg).astype(bf16)
        dcg_ref[...] = (dglu * cvv * sg * (1.0 - sg)).astype(bf16)

    row = _row_spec(tm, 1024)
    return pl.pallas_call(
        body, name="conf_conv_bwd", grid=(t // tm,),
        in_specs=[row, _next_halo_spec(tm, 32, 1024, t), row, row, _prev_halo_spec(tm, 32, 1024),
                  _prev_halo_spec(tm, 32, 1024), _const_spec((32, 1024))],
        out_specs=[row, row, _const_spec((32, 1024))],
        out_shape=(jax.ShapeDtypeStruct((t, 1024), bf16), jax.ShapeDtypeStruct((t, 1024), bf16),
                   jax.ShapeDtypeStruct((32, 1024), f32)),
        scratch_shapes=[pltpu.VMEM((tm + 32, 1024), f32), pltpu.VMEM((tm + 32, 1024), f32)],
        compiler_params=_params("arbitrary"),
    )(dv2, dv2, cv, cg, cv, cg, w)


def _ssd_scan_bwd(xbc_act, dt, s_prev, dy, a_log, d_skip):
    t = xbc_act.shape[0]
    nc = t // CHUNK

    def body(xbc_ref, dt_ref, sp_ref, dy_ref, al_ref, dk_ref, dxbc_ref, ddt_ref, dal_ref, ddk_ref, ds_ref):
        i = pl.program_id(0)
        _acc_init(i, ds_ref, dal_ref, ddk_ref)
        consts = _ssd_consts()
        _, vjp_c = jax.vjp(
            functools.partial(_ssd_chunk, consts=consts),
            xbc_ref[:, 0:1024], xbc_ref[:, 1024:1280], xbc_ref[:, 1280:1536], dt_ref[...], sp_ref[...],
            al_ref[...], dk_ref[...])
        dxs, dbm, dcm, ddt, dsp, dal, ddk = vjp_c((dy_ref[...], ds_ref[...]))
        dxbc_ref[:, 0:1024] = dxs
        dxbc_ref[:, 1024:1280] = dbm
        dxbc_ref[:, 1280:1536] = dcm
        ddt_ref[...] = ddt
        ds_ref[...] = dsp
        dal_ref[...] += dal
        ddk_ref[...] += ddk

    rev = lambda i: (nc - 1 - i, 0)
    return pl.pallas_call(
        body, name="ssd_scan_bwd", grid=(nc,),
        in_specs=[pl.BlockSpec((CHUNK, XBC_WIDTH), rev), pl.BlockSpec((CHUNK, SSD_HEADS), rev),
                  pl.BlockSpec((None, SSD_STATE, D_MODEL), lambda i: (nc - 1 - i, 0, 0)),
                  pl.BlockSpec((CHUNK, D_MODEL), rev), _const_spec((1, SSD_HEADS)), _const_spec((1, SSD_HEADS))],
        out_specs=[pl.BlockSpec((CHUNK, XBC_WIDTH), rev), pl.BlockSpec((CHUNK, SSD_HEADS), rev),
                   _const_spec((1, SSD_HEADS)), _const_spec((1, SSD_HEADS))],
        out_shape=(jax.ShapeDtypeStruct((t, XBC_WIDTH), f32), jax.ShapeDtypeStruct((t, SSD_HEADS), f32),
                   jax.ShapeDtypeStruct((1, SSD_HEADS), f32), jax.ShapeDtypeStruct((1, SSD_HEADS), f32)),
        scratch_shapes=[pltpu.VMEM((SSD_STATE, D_MODEL), f32)],
        compiler_params=_params("arbitrary"),
    )(xbc_act, dt, s_prev, dy, a_log, d_skip)


def _ssd_pre_bwd(xbc_raw, dxbc_act, ddt, dt_raw, cw, cb, dt_bias):
    t = xbc_raw.shape[0]
    tm = min(TOKEN_TILE, t)

    def body(cur_ref, halo_ref, dact_ref, ddt_ref, dtr_ref, w_ref, b_ref, dtb_ref,
             dco_ref, ddtr_ref, dw_ref, db_ref, ddtb_ref, ext_ref):
        i = pl.program_id(0)
        _acc_init(i, dw_ref, db_ref, ddtb_ref)
        ext_ref[0:8, :] = jnp.where(i == 0, 0.0, halo_ref[...])
        ext_ref[8:, :] = cur_ref[...]
        co = _conv4_tile(ext_ref, w_ref, b_ref, tm)
        sg = jax.nn.sigmoid(co)
        dco = dact_ref[...] * sg * (1.0 + co * (1.0 - sg))
        dco_ref[...] = dco
        db_ref[...] += jnp.sum(dco, axis=0, keepdims=True)
        for k in range(SSD_CONV):
            dw_ref[k:k + 1, :] += jnp.sum(dco * ext_ref[pl.ds(8 - 3 + k, tm), :], axis=0, keepdims=True)
        ddtr = ddt_ref[...] * jax.nn.sigmoid(dtr_ref[...] + dtb_ref[...])
        ddtb_ref[...] += jnp.sum(ddtr, axis=0, keepdims=True)
        r = lax.broadcasted_iota(jnp.int32, (SSD_HEADS, DT_PAD), 0)
        c = lax.broadcasted_iota(jnp.int32, (SSD_HEADS, DT_PAD), 1)
        ddtr_ref[...] = jnp.dot(ddtr, (r == c).astype(f32), precision=lax.Precision.HIGHEST).astype(bf16)

    return pl.pallas_call(
        body, name="ssd_pre_bwd", grid=(t // tm,),
        in_specs=[_row_spec(tm, XBC_WIDTH), _prev_halo_spec(tm, 8, XBC_WIDTH), _row_spec(tm, XBC_WIDTH),
                  _row_spec(tm, SSD_HEADS), _row_spec(tm, SSD_HEADS), _const_spec((8, XBC_WIDTH)),
                  _const_spec((1, XBC_WIDTH)), _const_spec((1, SSD_HEADS))],
        out_specs=[_row_spec(tm, XBC_WIDTH), _row_spec(tm, DT_PAD), _const_spec((8, XBC_WIDTH)),
                   _const_spec((1, XBC_WIDTH)), _const_spec((1, SSD_HEADS))],
        out_shape=(jax.ShapeDtypeStruct((t, XBC_WIDTH), f32), jax.ShapeDtypeStruct((t, DT_PAD), bf16),
                   jax.ShapeDtypeStruct((8, XBC_WIDTH), f32), jax.ShapeDtypeStruct((1, XBC_WIDTH), f32),
                   jax.ShapeDtypeStruct((1, SSD_HEADS), f32)),
        scratch_shapes=[pltpu.VMEM((tm + 8, XBC_WIDTH), f32)],
        compiler_params=_params("arbitrary"),
    )(xbc_raw, xbc_raw, dxbc_act, ddt, dt_raw, cw, cb, dt_bias)


def _conv4_bwd_data(dco, cw):
    t = dco.shape[0]
    tm = min(TOKEN_TILE, t)

    def body(cur_ref, nxt_ref, w_ref, o_ref, ext_ref):
        i = pl.program_id(0)
        ext_ref[0:tm, :] = cur_ref[...]
        ext_ref[tm:, :] = jnp.where(i == pl.num_programs(0) - 1, 0.0, nxt_ref[...])
        acc = w_ref[0:1, :] * ext_ref[pl.ds(SSD_CONV - 1, tm), :]
        for k in range(1, SSD_CONV):
            acc = acc + w_ref[k:k + 1, :] * ext_ref[pl.ds(SSD_CONV - 1 - k, tm), :]
        o_ref[...] = acc.astype(bf16)

    return pl.pallas_call(
        body, name="conv4_bwd_data", grid=(t // tm,),
        in_specs=[_row_spec(tm, XBC_WIDTH), _next_halo_spec(tm, 8, XBC_WIDTH, t), _const_spec((8, XBC_WIDTH))],
        out_specs=_row_spec(tm, XBC_WIDTH), out_shape=jax.ShapeDtypeStruct((t, XBC_WIDTH), bf16),
        scratch_shapes=[pltpu.VMEM((tm + 8, XBC_WIDTH), f32)],
        compiler_params=_params("arbitrary"),
    )(dco, dco, cw)


def _in_proj_bwd(dz, dxbc, dcv, dcg, ddt, x, dh1, g1, w_in_p):
    t = x.shape[0]
    tm = min(TOKEN_TILE, t)
    segs = (SEG_Z, SEG_XBC, SEG_CV, SEG_CG, SEG_DT)

    def body(dz_ref, dxbc_ref, dcv_ref, dcg_ref, ddt_ref, x_ref, dh_ref, g_ref, w_ref, dx_ref, acc_ref):
        i = pl.program_id(0)
        _acc_init(i, acc_ref)
        du = jnp.zeros((tm, D_MODEL), f32)
        for (lo, hi), r in zip(segs, (dz_ref, dxbc_ref, dcv_ref, dcg_ref, ddt_ref)):
            du = du + _nt(r[...], w_ref[:, lo:hi])
        _, vjp_u = jax.vjp(_rms, x_ref[...], g_ref[...])
        d, dg = vjp_u(du)
        dx_ref[...] = dh_ref[...] + d
        acc_ref[0:1, :] += dg

    row = _row_spec(tm, 1024)
    return pl.pallas_call(
        body, name="in_proj_bwd", grid=(t // tm,),
        in_specs=[row, _row_spec(tm, XBC_WIDTH), row, row, _row_spec(tm, DT_PAD), row, row, _const_spec((1, 1024)),
                  _const_spec((D_MODEL, IN_PERM_WIDTH))],
        out_specs=[row, _const_spec((8, 1024))],
        out_shape=(jax.ShapeDtypeStruct((t, 1024), f32), jax.ShapeDtypeStruct((8, 1024), f32)),
        compiler_params=_params("arbitrary"),
    )(dz, dxbc, dcv, dcg, ddt, x, dh1, g1, w_in_p)


def _mm_tn(a, b, tk, tn, name, column_blocks=False):
    t, kk = a.shape
    n = b.shape[1]
    tk, tn = min(tk, kk), min(tn, n)

    def body(a_ref, b_ref, o_ref):
        o_ref[...] = _tn(a_ref[...], b_ref[...]).astype(bf16)

    if column_blocks:
        assert tk == kk
        out_spec = pl.BlockSpec((None, tk, tn), lambda i, j: (j, 0, 0))
        out_shape = jax.ShapeDtypeStruct((n // tn, kk, tn), bf16)
    else:
        out_spec = pl.BlockSpec((tk, tn), lambda i, j: (i, j))
        out_shape = jax.ShapeDtypeStruct((kk, n), bf16)
    return pl.pallas_call(
        body, name=name, grid=(kk // tk, n // tn),
        in_specs=[pl.BlockSpec((t, tk), lambda i, j: (0, i)), pl.BlockSpec((t, tn), lambda i, j: (0, j))],
        out_specs=out_spec, out_shape=out_shape,
        compiler_params=_params("arbitrary", "arbitrary"),
    )(a, b)


_REF_DT = (2560, 2576)


def _perm_w_in(w_full):
    pad = jnp.zeros((w_full.shape[0], DT_PAD - SSD_HEADS), w_full.dtype)
    return jnp.concatenate([w_full[:, :_REF_DT[0]], w_full[:, _REF_DT[1]:], w_full[:, _REF_DT[0]:_REF_DT[1]], pad], axis=1)


def _unperm_dw_in(segs):
    dz, dxbc, dcv, dcg, ddt = segs
    return jnp.concatenate([dz, dxbc, ddt[:, :SSD_HEADS], dcv, dcg], axis=1)


class _LocalWeights:
    def __init__(self, w):
        self.w = w
        self.sent = {}

    def start_token(self):
        return None

    def weight(self, name, after=()):
        del after
        return self.w[name]

    def send_grads(self, grads):
        self.sent.update(grads)
        return None


def _local_step(x, p, target, s, comm):
    conv_w, conf_w = comm.weight("ssd_conv_w"), comm.weight("conf_dw_w")
    w_in = comm.weight("w_in")
    u, z, xbc_raw, cv, cg, dt_raw = _in_proj(x, s["mix_norm_g"], w_in, token=comm.start_token())
    xbc_act, dt = _ssd_pre(xbc_raw, dt_raw, conv_w, s["ssd_conv_b"], s["dt_bias"])
    y, s_prev = _ssd_scan(xbc_act, dt, s["A_log"], s["D_skip"])
    v2 = _conf_conv(cv, cg, conf_w, s["conf_dw_b"])
    w_out = comm.weight("w_out", after=(y, v2))
    ys, yc, h1, u2 = _out_proj(y, z, v2, x, s["ssd_norm_g"], s["conf_ln_g"], s["conf_ln_b"], w_out, s["mlp_norm_g"])
    w_up = comm.weight("w_up", after=(u2,))
    pre, hs = _mlp_up(u2, w_up)
    w_down = comm.weight("w_down", after=(hs,))
    h2 = _mlp_down(h1, hs, w_down)
    w_gate, w_ple = comm.weight("w_ple_gate", after=(h2,)), comm.weight("w_ple", after=(h2,))
    dh2, dh2b, dwg, dwp, tail_acc = _tail(h2, p, target, s["ple_gate_norm_g"], w_gate, s["b_ple_gate"], w_ple,
                                          s["ple_norm_g"], s["final_norm_g"])
    token = comm.send_grads({"w_ple_gate": dwg, "w_ple": dwp})
    dpre, dh1, dh1b, mlp_acc = _mlp_bwd(dh2, dh2b, pre, h1, s["mlp_norm_g"], w_down, w_up, token=token)
    token = comm.send_grads({
        "w_down": _mm_tn(hs, dh2b, 512, 1024, "dw_down"),
        "w_up": _mm_tn(u2, dpre, 1024, D_FF // N_DEV, "dw_up", column_blocks=True),
        "w_out": [_mm_tn(ys, dh1b, 1024, 512, "dw_out_ssd"), _mm_tn(yc, dh1b, 1024, 512, "dw_out_conf")],
    })
    dy, dz, dv2, opb_acc = _out_proj_bwd(dh1b, y, z, v2, s["ssd_norm_g"], s["conf_ln_g"], s["conf_ln_b"], w_out,
                                         token=token)
    dcv, dcg, dconf_w = _conf_conv_bwd(dv2, cv, cg, conf_w)
    dxbc_act, ddt, d_alog, d_dskip = _ssd_scan_bwd(xbc_act, dt, s_prev, dy, s["A_log"], s["D_skip"])
    dco, ddt_raw, dconv_w, dconv_b, d_dtb = _ssd_pre_bwd(xbc_raw, dxbc_act, ddt, dt_raw, conv_w, s["ssd_conv_b"],
                                                        s["dt_bias"])
    dxbc_raw = _conv4_bwd_data(dco, conv_w)
    grad_x, inp_acc = _in_proj_bwd(dz, dxbc_raw, dcv, dcg, ddt_raw, x, dh1, s["mix_norm_g"], w_in)
    comm.send_grads({"w_in": [_mm_tn(u, d, 1024, 512, "dw_in_" + n) for n, d in
                              (("z", dz), ("xbc", dxbc_raw), ("cv", dcv), ("cg", dcg), ("dt", ddt_raw))]})
    gconv = {"ssd_conv_w": dconv_w, "conf_dw_w": dconf_w}
    gs = {
        "loss": tail_acc[TAIL_LOSS:TAIL_LOSS + 1, 0:1],
        "mix_norm_g": inp_acc[0:1], "ssd_conv_b": dconv_b, "dt_bias": d_dtb, "A_log": d_alog, "D_skip": d_dskip,
        "ssd_norm_g": opb_acc[OPB_SSD_G:OPB_SSD_G + 1], "conf_dw_b": opb_acc[OPB_CONV_B:OPB_CONV_B + 1],
        "conf_ln_g": opb_acc[OPB_LN_G:OPB_LN_G + 1], "conf_ln_b": opb_acc[OPB_LN_B:OPB_LN_B + 1],
        "mlp_norm_g": mlp_acc[0:1], "ple_gate_norm_g": tail_acc[TAIL_GATE_NORM_G:TAIL_GATE_NORM_G + 1],
        "b_ple_gate": tail_acc[TAIL_GATE_B:TAIL_GATE_B + 1], "ple_norm_g": tail_acc[TAIL_PLE_G:TAIL_PLE_G + 1],
        "final_norm_g": tail_acc[TAIL_FINAL_G:TAIL_FINAL_G + 1],
    }
    return grad_x, gconv, gs


def _mesh_pos():
    return lax.axis_index("x"), lax.axis_index("y"), lax.axis_index("c")


def _other_chips(x, y):
    return [(1 - x, y), (x, 1 - y), (1 - x, 1 - y)]


def _all_gather(arrs, name):
    n = len(arrs)

    def body(*refs):
        ins, outs = refs[:n], refs[n:2 * n]
        send_sems, recv_sems, local_sems = refs[2 * n:]
        x, y, c = _mesh_pos()
        me = 4 * x + 2 * y + c
        sibling = (x, y, 1 - c)
        chips = _other_chips(x, y)

        def copy(a, k, block, to, src=None):
            dst = outs[a].at[block]
            return pltpu.make_async_remote_copy(
                src_ref=dst if src is None else src, dst_ref=dst, send_sem=send_sems.at[a, k],
                recv_sem=recv_sems.at[a, k], device_id=to, device_id_type=MESH_ID)

        mine = [pltpu.make_async_copy(ins[a], outs[a].at[me], local_sems.at[a]) for a in range(n)]
        for cp in mine:
            cp.start()
        first = []
        for a in range(n):
            first.append(copy(a, 0, me, sibling, src=ins[a]))
            first += [copy(a, 1 + j, me, (px, py, c), src=ins[a]) for j, (px, py) in enumerate(chips)]
        for cp in first:
            cp.start()
        passed = []
        for j, (px, py) in enumerate(chips):
            for a in range(n):
                blk = 4 * px + 2 * py + c
                copy(a, 1 + j, blk, (x, y, c)).wait_recv()
                cp = copy(a, 4 + j, blk, sibling)
                cp.start()
                passed.append(cp)
        for a in range(n):
            copy(a, 0, 4 * x + 2 * y + (1 - c), (x, y, c)).wait_recv()
        for j, (px, py) in enumerate(chips):
            for a in range(n):
                copy(a, 4 + j, 4 * px + 2 * py + (1 - c), (x, y, c)).wait_recv()
        for cp in first + passed:
            cp.wait_send()
        for cp in mine:
            cp.wait()

    return pl.pallas_call(
        body, name=name,
        in_specs=[HBM_SPEC] * n, out_specs=[HBM_SPEC] * n,
        out_shape=[jax.ShapeDtypeStruct((N_DEV,) + a.shape, a.dtype) for a in arrs],
        scratch_shapes=[pltpu.SemaphoreType.DMA((n, 7)), pltpu.SemaphoreType.DMA((n, 7)), pltpu.SemaphoreType.DMA((n,))],
    )(*arrs)


def _exchange_sibling(gs):
    n = len(gs)

    def body(*refs):
        ins, outs = refs[:n], refs[n:2 * n]
        send_sems, recv_sems = refs[2 * n:]
        x, y, c = _mesh_pos()
        copies = []
        for a in range(n):
            for j in range(4):
                copies.append(pltpu.make_async_remote_copy(
                    src_ref=ins[a].at[2 * j + (1 - c)], dst_ref=outs[a].at[j], send_sem=send_sems.at[a, j],
                    recv_sem=recv_sems.at[a, j], device_id=(x, y, 1 - c), device_id_type=MESH_ID))
        for cp in copies:
            cp.start()
        for cp in copies:
            cp.wait_recv()
        for cp in copies:
            cp.wait_send()

    return pl.pallas_call(
        body, name="rs_sibling",
        in_specs=[HBM_SPEC] * n, out_specs=[HBM_SPEC] * n,
        out_shape=[jax.ShapeDtypeStruct((4,) + g.shape[1:], g.dtype) for g in gs],
        scratch_shapes=[pltpu.SemaphoreType.DMA((n, 4)), pltpu.SemaphoreType.DMA((n, 4))],
    )(*gs)


def _chip_sum(g, r, core, name):
    _, rows, cols = g.shape
    tr = 256 if rows % 256 == 0 else rows

    def body(c_ref, g_ref, r_ref, o_ref):
        o_ref[...] = (g_ref[...].astype(f32) + r_ref[...].astype(f32)).astype(o_ref.dtype)

    grid_spec = pltpu.PrefetchScalarGridSpec(
        num_scalar_prefetch=1, grid=(4, rows // tr),
        in_specs=[pl.BlockSpec((None, tr, cols), lambda j, i, c_ref: (2 * j + c_ref[0], i, 0)),
                  pl.BlockSpec((None, tr, cols), lambda j, i, c_ref: (j, i, 0))],
        out_specs=pl.BlockSpec((None, tr, cols), lambda j, i, c_ref: (j, i, 0)))
    return pl.pallas_call(
        body, name=name, grid_spec=grid_spec, out_shape=jax.ShapeDtypeStruct((4, rows, cols), g.dtype),
        compiler_params=_params("arbitrary", "arbitrary"),
    )(core, g, r)


def _exchange_chips(cs):
    n = len(cs)

    def body(*refs):
        ins, outs = refs[:n], refs[n:2 * n]
        send_sems, recv_sems, local_sems = refs[2 * n:]
        x, y, c = _mesh_pos()
        my_chip = 2 * x + y
        chips = _other_chips(x, y)
        mine = [pltpu.make_async_copy(ins[a].at[my_chip], outs[a].at[my_chip], local_sems.at[a]) for a in range(n)]
        for cp in mine:
            cp.start()
        sends, recvs = [], []
        for a in range(n):
            for k, (px, py) in enumerate(chips):
                sends.append(pltpu.make_async_remote_copy(
                    src_ref=ins[a].at[2 * px + py], dst_ref=outs[a].at[my_chip], send_sem=send_sems.at[a, k],
                    recv_sem=recv_sems.at[a, k], device_id=(px, py, c), device_id_type=MESH_ID))
                recvs.append(pltpu.make_async_remote_copy(
                    src_ref=ins[a].at[my_chip], dst_ref=outs[a].at[2 * px + py], send_sem=send_sems.at[a, k],
                    recv_sem=recv_sems.at[a, k], device_id=(px, py, c), device_id_type=MESH_ID))
        for cp in sends:
            cp.start()
        for cp in recvs:
            cp.wait_recv()
        for cp in sends:
            cp.wait_send()
        for cp in mine:
            cp.wait()

    return pl.pallas_call(
        body, name="rs_chips",
        in_specs=[HBM_SPEC] * n, out_specs=[HBM_SPEC] * n,
        out_shape=[jax.ShapeDtypeStruct(g.shape, g.dtype) for g in cs],
        scratch_shapes=[pltpu.SemaphoreType.DMA((n, 3)), pltpu.SemaphoreType.DMA((n, 3)), pltpu.SemaphoreType.DMA((n,))],
    )(*cs)


_PEER_FLIPS = ((0, 0, 1), (1, 0, 0), (0, 1, 0), (1, 1, 0), (1, 0, 1), (0, 1, 1), (1, 1, 1))
SEM_SPEC = pl.BlockSpec(memory_space=pltpu.SEMAPHORE)
ANY_SPEC = pl.BlockSpec(memory_space=pl.ANY)


def _flip(v, d):
    return 1 - v if d else v


def _peers(x, y, c):
    out = []
    for dx, dy, dc in _PEER_FLIPS:
        px, py, pc = _flip(x, dx), _flip(y, dy), _flip(c, dc)
        out.append(((px, py, pc), 4 * px + 2 * py + pc))
    return out


def _exchange_copy(src_ref, land_ref, send_sems, recv_sems, k, peer, peer_block, my_block, by_block, outgoing):
    src = src_ref.at[peer_block] if by_block else src_ref
    dst = land_ref.at[my_block if outgoing else peer_block]
    return pltpu.make_async_remote_copy(src_ref=src, dst_ref=dst, send_sem=send_sems.at[k], recv_sem=recv_sems.at[k],
                                        device_id=peer, device_id_type=MESH_ID)


def _exchange_start(srcs, lands, by_block, name, after):
    n = len(srcs)

    def body(*refs):
        src_refs, land_refs = refs[1:1 + n], refs[1 + n:1 + 2 * n]
        outs = refs[1 + 2 * n:]
        send, recv, token = outs[:n], outs[n:2 * n], outs[4 * n]
        x, y, c = _mesh_pos()
        me = 4 * x + 2 * y + c
        for a in range(n):
            for k, (peer, blk) in enumerate(_peers(x, y, c)):
                _exchange_copy(src_refs[a], land_refs[a], send[a], recv[a], k, peer, blk, me, by_block, True).start()
        token[...] = jnp.zeros_like(token)

    sems = [pltpu.SemaphoreType.DMA((N_DEV - 1,))] * (2 * n)
    thru = [pltpu.HBM(a.shape, a.dtype) for a in list(srcs) + list(lands)]
    res = pl.pallas_call(
        body, name=name,
        in_specs=[ANY_SPEC] + [HBM_SPEC] * (2 * n),
        out_specs=[SEM_SPEC] * (2 * n) + [HBM_SPEC] * (2 * n) + [pl.BlockSpec(memory_space=pltpu.VMEM)],
        out_shape=sems + thru + [jax.ShapeDtypeStruct((8, 128), f32)],
        input_output_aliases={1 + i: 2 * n + i for i in range(2 * n)},
        compiler_params=pltpu.CompilerParams(has_side_effects=pltpu.SideEffectType.DATAFLOW_SIDE_EFFECTING),
    )(after, *[pltpu.with_memory_space_constraint(a, pltpu.HBM) for a in list(srcs) + list(lands)])
    states = [(res[2 * n + a], res[3 * n + a], res[a], res[n + a]) for a in range(n)]
    return states, res[4 * n]


def _exchange_wait(states, by_block, name, after):
    n, na = len(states), len(after)

    def body(*refs):
        src_refs, land_refs = refs[:n], refs[n:2 * n]
        send, recv = refs[2 * n:3 * n], refs[3 * n:4 * n]
        x, y, c = _mesh_pos()
        me = 4 * x + 2 * y + c
        for a in range(n):
            for k, (peer, blk) in enumerate(_peers(x, y, c)):
                _exchange_copy(src_refs[a], land_refs[a], send[a], recv[a], k, peer, blk, me, by_block, True).wait_send()
                _exchange_copy(src_refs[a], land_refs[a], send[a], recv[a], k, peer, blk, me, by_block, False).wait_recv()

    srcs, lands = [s[0] for s in states], [s[1] for s in states]
    res = pl.pallas_call(
        body, name=name,
        in_specs=[HBM_SPEC] * (2 * n) + [SEM_SPEC] * (2 * n) + [ANY_SPEC] * na,
        out_specs=[HBM_SPEC] * (2 * n),
        out_shape=[pltpu.HBM(a.shape, a.dtype) for a in srcs + lands],
        input_output_aliases={i: i for i in range(2 * n)},
        compiler_params=pltpu.CompilerParams(has_side_effects=pltpu.SideEffectType.DATAFLOW_SIDE_EFFECTING),
    )(*srcs, *lands, *[s[2] for s in states], *[s[3] for s in states], *after)
    return list(res[n:2 * n])


def _with_own_block(own, me):
    buf = lax.empty((N_DEV,) + own.shape, own.dtype)
    return lax.dynamic_update_slice(buf, own[None], (me,) + (0,) * own.ndim)


def _adamw_math(w, g, m, v):
    m = ADAM_B1 * m + (1.0 - ADAM_B1) * g
    v = ADAM_B2 * v + (1.0 - ADAM_B2) * (g * g)
    m_hat = m / (1.0 - ADAM_B1 ** ADAM_STEP)
    v_hat = v / (1.0 - ADAM_B2 ** ADAM_STEP)
    delta = -ADAM_LR * (m_hat / (jnp.sqrt(v_hat) + ADAM_EPS) + ADAM_WD * w)
    return delta, m, v


def _adamw_big(parts, w, m, v, name):
    rows, cols = w.shape
    tr = 256 if rows % 256 == 0 else rows
    nparts = parts.shape[0]

    def body(p_ref, w_ref, m_ref, v_ref, g_ref, d_ref, mo_ref, vo_ref):
        g = p_ref[0].astype(f32)
        for j in range(1, nparts):
            g = g + p_ref[j].astype(f32)
        d, mn, vn = _adamw_math(w_ref[...], g, m_ref[...], v_ref[...])
        g_ref[...] = g
        d_ref[...] = d
        mo_ref[...] = mn
        vo_ref[...] = vn

    row = pl.BlockSpec((tr, cols), lambda i: (i, 0))
    shp = jax.ShapeDtypeStruct((rows, cols), f32)
    return pl.pallas_call(
        body, name=name, grid=(rows // tr,),
        in_specs=[pl.BlockSpec((nparts, tr, cols), lambda i: (0, i, 0)), row, row, row],
        out_specs=[row, row, row, row], out_shape=(shp, shp, shp, shp),
        compiler_params=_params("arbitrary"),
    )(parts, w, m, v)


def _sum_devices(parts):
    _, rows, cols = parts.shape

    def body(p_ref, o_ref):
        g = p_ref[0]
        for j in range(1, N_DEV):
            g = g + p_ref[j]
        o_ref[...] = g

    return pl.pallas_call(body, name="sum_small", out_shape=jax.ShapeDtypeStruct((rows, cols), f32))(parts)


def _adamw_small(w, g, m, v):
    def body(w_ref, g_ref, m_ref, v_ref, d_ref, mo_ref, vo_ref):
        d, mn, vn = _adamw_math(w_ref[...], g_ref[...], m_ref[...], v_ref[...])
        d_ref[...] = d
        mo_ref[...] = mn
        vo_ref[...] = vn

    shp = jax.ShapeDtypeStruct(w.shape, f32)
    return pl.pallas_call(body, name="adamw_small", out_shape=(shp, shp, shp))(w, g, m, v)


def _pack_rows(arrs):
    rows = []
    for a in arrs:
        flat = a.reshape(-1).astype(f32)
        pad = (-flat.shape[0]) % (8 * 1024)
        rows.append(jnp.pad(flat, (0, pad)).reshape(-1, 1024))
    return jnp.concatenate(rows, axis=0)


def _unpack_rows(buf, shapes):
    out, r = [], 0
    for shp in shapes:
        size = 1
        for d in shp:
            size *= d
        nr = 8 * -(-size // (8 * 1024))
        out.append(buf[r:r + nr].reshape(-1)[:size].reshape(shp))
        r += nr
    return out


_REPLICATED = ("mix_norm_g", "ssd_conv_b", "dt_bias", "A_log", "D_skip", "ssd_norm_g", "conf_dw_b", "conf_ln_g",
               "conf_ln_b", "mlp_norm_g", "ple_gate_norm_g", "b_ple_gate", "ple_norm_g", "final_norm_g")
_CONV_WEIGHTS = ("ssd_conv_w", "conf_dw_w")
_BIG = ("w_in", "w_out", "w_up", "w_down", "w_ple_gate", "w_ple")
_WEIGHT_ORDER = ("mix_norm_g", "w_in", "ssd_conv_w", "ssd_conv_b", "dt_bias", "A_log", "D_skip", "ssd_norm_g", "conf_dw_w",
                 "conf_dw_b", "conf_ln_g", "conf_ln_b", "w_out", "mlp_norm_g", "w_up", "w_down", "ple_gate_norm_g",
                 "w_ple_gate", "b_ple_gate", "w_ple", "ple_norm_g", "final_norm_g")


def _blocks_of_columns(a):
    r, c8 = a.shape
    return jnp.transpose(a.reshape(r, N_DEV, c8 // N_DEV), (1, 0, 2))


def _columns_of_blocks(a):
    _, r, c = a.shape
    return jnp.transpose(a, (1, 0, 2)).reshape(r, N_DEV * c)


_LATER = ("w_out", "w_up", "w_down", "w_ple_gate", "w_ple")
_WHOLE = {
    "w_out": lambda a: a.reshape(2048, D_MODEL),
    "w_up": lambda a: a,
    "w_down": lambda a: a.reshape(D_FF, D_MODEL),
    "w_ple_gate": lambda a: a.reshape(D_MODEL, D_MODEL),
    "w_ple": _columns_of_blocks,
}
_BY_BLOCK = {
    "w_out": lambda g: jnp.concatenate(g, axis=0).reshape(N_DEV, 256, D_MODEL),
    "w_up": lambda g: g,
    "w_down": lambda g: g.reshape(N_DEV, 512, D_MODEL),
    "w_ple_gate": lambda g: g.astype(bf16).reshape(N_DEV, 128, D_MODEL),
    "w_ple": lambda g: _blocks_of_columns(g.astype(bf16)),
}


class _StepComm:
    def __init__(self, me, ready, gathers, token):
        self.me, self.ready, self.gathers, self.token = me, ready, gathers, token
        self.sent = []
        self.w_in_grads = None

    def start_token(self):
        return self.token

    def weight(self, name, after=()):
        if name not in self.ready:
            (land,) = _exchange_wait([self.gathers[name]], False, "gather_wait_" + name, list(after))
            self.ready[name] = _WHOLE[name](land)
        return self.ready[name]

    def send_grads(self, grads):
        if "w_in" in grads:
            self.w_in_grads = grads["w_in"]
            return None
        names = list(grads)
        blocks = [_BY_BLOCK[n](grads[n]) for n in names]
        lands = [_with_own_block(lax.dynamic_index_in_dim(b, self.me, 0, keepdims=False), self.me) for b in blocks]
        states, self.token = _exchange_start(blocks, lands, True, "scatter_start_" + names[0], self.token)
        self.sent.append((names, states))
        return self.token


def kernel(x, p, mix_norm_g, w_in, ssd_conv_w, ssd_conv_b, dt_bias, A_log, D_skip, ssd_norm_g, conf_dw_w, conf_dw_b, conf_ln_g, conf_ln_b, w_out, mlp_norm_g, w_up, w_down, ple_gate_norm_g, w_ple_gate, b_ple_gate, w_ple, ple_norm_g, final_norm_g, loss_target, m_mix_norm_g, m_w_in, m_ssd_conv_w, m_ssd_conv_b, m_dt_bias, m_A_log, m_D_skip, m_ssd_norm_g, m_conf_dw_w, m_conf_dw_b, m_conf_ln_g, m_conf_ln_b, m_w_out, m_mlp_norm_g, m_w_up, m_w_down, m_ple_gate_norm_g, m_w_ple_gate, m_b_ple_gate, m_w_ple, m_ple_norm_g, m_final_norm_g, v_mix_norm_g, v_w_in, v_ssd_conv_w, v_ssd_conv_b, v_dt_bias, v_A_log, v_D_skip, v_ssd_norm_g, v_conf_dw_w, v_conf_dw_b, v_conf_ln_g, v_conf_ln_b, v_w_out, v_mlp_norm_g, v_w_up, v_w_down, v_ple_gate_norm_g, v_w_ple_gate, v_b_ple_gate, v_w_ple, v_ple_norm_g, v_final_norm_g):
    wts = dict(mix_norm_g=mix_norm_g, w_in=w_in, ssd_conv_w=ssd_conv_w, ssd_conv_b=ssd_conv_b, dt_bias=dt_bias, A_log=A_log,
               D_skip=D_skip, ssd_norm_g=ssd_norm_g, conf_dw_w=conf_dw_w, conf_dw_b=conf_dw_b, conf_ln_g=conf_ln_g,
               conf_ln_b=conf_ln_b, w_out=w_out, mlp_norm_g=mlp_norm_g, w_up=w_up, w_down=w_down,
               ple_gate_norm_g=ple_gate_norm_g, w_ple_gate=w_ple_gate, b_ple_gate=b_ple_gate, w_ple=w_ple,
               ple_norm_g=ple_norm_g, final_norm_g=final_norm_g)
    mom1 = dict(mix_norm_g=m_mix_norm_g, w_in=m_w_in, ssd_conv_w=m_ssd_conv_w, ssd_conv_b=m_ssd_conv_b, dt_bias=m_dt_bias,
                A_log=m_A_log, D_skip=m_D_skip, ssd_norm_g=m_ssd_norm_g, conf_dw_w=m_conf_dw_w, conf_dw_b=m_conf_dw_b,
                conf_ln_g=m_conf_ln_g, conf_ln_b=m_conf_ln_b, w_out=m_w_out, mlp_norm_g=m_mlp_norm_g, w_up=m_w_up,
                w_down=m_w_down, ple_gate_norm_g=m_ple_gate_norm_g, w_ple_gate=m_w_ple_gate, b_ple_gate=m_b_ple_gate,
                w_ple=m_w_ple, ple_norm_g=m_ple_norm_g, final_norm_g=m_final_norm_g)
    mom2 = dict(mix_norm_g=v_mix_norm_g, w_in=v_w_in, ssd_conv_w=v_ssd_conv_w, ssd_conv_b=v_ssd_conv_b, dt_bias=v_dt_bias,
                A_log=v_A_log, D_skip=v_D_skip, ssd_norm_g=v_ssd_norm_g, conf_dw_w=v_conf_dw_w, conf_dw_b=v_conf_dw_b,
                conf_ln_g=v_conf_ln_g, conf_ln_b=v_conf_ln_b, w_out=v_w_out, mlp_norm_g=v_mlp_norm_g, w_up=v_w_up,
                w_down=v_w_down, ple_gate_norm_g=v_ple_gate_norm_g, w_ple_gate=v_w_ple_gate, b_ple_gate=v_b_ple_gate,
                w_ple=v_w_ple, ple_norm_g=v_ple_norm_g, final_norm_g=v_final_norm_g)
    x_pos, y_pos, c_pos = _mesh_pos()
    me = 4 * x_pos + 2 * y_pos + c_pos

    first = _all_gather([wts["w_in"][0].astype(bf16), wts["ssd_conv_w"][0], wts["conf_dw_w"][0]], "gather_first")
    ready = {
        "w_in": _perm_w_in(_columns_of_blocks(first[0])),
        "ssd_conv_w": jnp.pad(_columns_of_blocks(first[1]), ((0, 8 - SSD_CONV), (0, 0))),
        "conf_dw_w": jnp.pad(_columns_of_blocks(first[2]), ((0, 32 - CONF_KERNEL), (0, 0))),
    }
    shards = [wts[n][0].astype(bf16) for n in _LATER]
    states, token = _exchange_start(shards, [_with_own_block(sh, me) for sh in shards], False, "gather_start", first[1])
    comm = _StepComm(me, ready, dict(zip(_LATER, states)), token)
    small = {n: wts[n].reshape(1, -1) for n in _REPLICATED}

    grad_x, gconv, gs = _local_step(x[0], p[0, 0], loss_target[0], small, comm)

    by_block = [_blocks_of_columns(_unperm_dw_in(comm.w_in_grads))]
    from_sibling = _exchange_sibling(by_block)
    core = c_pos.astype(jnp.int32).reshape(1)
    parts = {"w_in": _exchange_chips([_chip_sum(by_block[0], from_sibling[0], core, "chip_sum_w_in")])[0]}
    for names, sent in comm.sent:
        parts.update(zip(names, _exchange_wait(sent, True, "scatter_wait_" + names[0], [grad_x])))

    small_names = _REPLICATED + _CONV_WEIGHTS
    small_shapes = [wts[n].shape for n in _REPLICATED] + [(SSD_CONV, XBC_WIDTH), (CONF_KERNEL, D_MODEL)]
    packed = _pack_rows([gs[n] for n in _REPLICATED] + [gconv["ssd_conv_w"][:SSD_CONV], gconv["conf_dw_w"][:CONF_KERNEL],
                                                       gs["loss"]])
    (all_small,) = _all_gather([packed], "gather_small")
    summed = _unpack_rows(_sum_devices(all_small), small_shapes + [(1,)])
    loss = summed[-1][0]
    grads = dict(zip(small_names, summed[:-1]))
    grads["ssd_conv_w"] = lax.dynamic_slice_in_dim(grads["ssd_conv_w"], me * 192, 192, axis=1)[None]
    grads["conf_dw_w"] = lax.dynamic_slice_in_dim(grads["conf_dw_w"], me * 128, 128, axis=1)[None]

    delta, new_m, new_v = {}, {}, {}
    for n in _BIG:
        shp = wts[n].shape
        g, d, mn, vn = _adamw_big(parts[n], wts[n][0], mom1[n][0], mom2[n][0], "adamw_" + n)
        grads[n], delta[n], new_m[n], new_v[n] = g.reshape(shp), d.reshape(shp), mn.reshape(shp), vn.reshape(shp)
    shapes = [wts[n].shape for n in small_names]
    d_s, m_s, v_s = _adamw_small(_pack_rows([wts[n] for n in small_names]), _pack_rows([grads[n] for n in small_names]),
                                 _pack_rows([mom1[n] for n in small_names]), _pack_rows([mom2[n] for n in small_names]))
    delta.update(zip(small_names, _unpack_rows(d_s, shapes)))
    new_m.update(zip(small_names, _unpack_rows(m_s, shapes)))
    new_v.update(zip(small_names, _unpack_rows(v_s, shapes)))

    return (loss, grad_x[None], *[grads[n] for n in _WEIGHT_ORDER], *[delta[n] for n in _WEIGHT_ORDER],
            *[new_m[n] for n in _WEIGHT_ORDER], *[new_v[n] for n in _WEIGHT_ORDER])
```

```python
import functools

import jax
import jax.numpy as jnp
from jax import lax
from jax.experimental import pallas as pl
from jax.experimental.pallas import tpu as pltpu

f32 = jnp.float32
bf16 = jnp.bfloat16

EPS = 1e-6
D_MODEL = 1024
SSD_HEADS = 16
SSD_HEAD_DIM = 64
SSD_GROUPS = 2
SSD_STATE = 128
SSD_CONV = 4
CHUNK = 128
XBC_WIDTH = 1536
CONF_KERNEL = 31
D_FF = 4096
PLE_DIM = 256
IN_WIDTH = 4624
N_DEV = 8
DT_PAD = 128
SEG_Z = (0, 1024)
SEG_XBC = (1024, 2560)
SEG_CV = (2560, 3584)
SEG_CG = (3584, 4608)
SEG_DT = (4608, 4736)
IN_PERM_WIDTH = 4736

ADAM_LR = 0.001
ADAM_B1 = 0.9
ADAM_B2 = 0.999
ADAM_EPS = 1e-08
ADAM_WD = 0.01
ADAM_STEP = 10

VMEM_LIMIT_BYTES = 56 * 1024 * 1024
TOKEN_TILE = 256
SMALL_ROWS = 16

HBM_SPEC = pl.BlockSpec(memory_space=pltpu.HBM)
MESH_ID = pl.DeviceIdType.MESH


def _params(*semantics):
    return pltpu.CompilerParams(dimension_semantics=semantics, vmem_limit_bytes=VMEM_LIMIT_BYTES)


def _nn(a, b):
    return lax.dot_general(a, b, (((1,), (0,)), ((), ())), preferred_element_type=f32)


def _nt(a, b):
    return lax.dot_general(a, b, (((1,), (1,)), ((), ())), preferred_element_type=f32)


def _tn(a, b):
    return lax.dot_general(a, b, (((0,), (0,)), ((), ())), preferred_element_type=f32)


@jax.custom_vjp
def bnn(a, b):
    return _nn(a.astype(bf16), b.astype(bf16))


def _bnn_fwd(a, b):
    ab, bb = a.astype(bf16), b.astype(bf16)
    return _nn(ab, bb), (ab, bb)


def _bnn_bwd(res, g):
    ab, bb = res
    gb = g.astype(bf16)
    return _nt(gb, bb), _tn(ab, gb)


bnn.defvjp(_bnn_fwd, _bnn_bwd)


@jax.custom_vjp
def bnt(a, b):
    return _nt(a.astype(bf16), b.astype(bf16))


def _bnt_fwd(a, b):
    ab, bb = a.astype(bf16), b.astype(bf16)
    return _nt(ab, bb), (ab, bb)


def _bnt_bwd(res, g):
    ab, bb = res
    gb = g.astype(bf16)
    return _nn(gb, bb), _tn(gb, ab)


bnt.defvjp(_bnt_fwd, _bnt_bwd)


@jax.custom_vjp
def btn(a, b):
    return _tn(a.astype(bf16), b.astype(bf16))


def _btn_fwd(a, b):
    ab, bb = a.astype(bf16), b.astype(bf16)
    return _tn(ab, bb), (ab, bb)


def _btn_bwd(res, g):
    ab, bb = res
    gb = g.astype(bf16)
    return _nt(bb, gb), _nn(ab, gb)


btn.defvjp(_btn_fwd, _btn_bwd)


def _rms(x, g):
    return x * lax.rsqrt(jnp.mean(x * x, axis=-1, keepdims=True) + EPS) * g


def _gated_norm(y, z, g):
    v = y * jax.nn.silu(z)
    half = v.shape[-1] // SSD_GROUPS
    parts = []
    for k in range(SSD_GROUPS):
        vk = v[:, k * half:(k + 1) * half]
        parts.append(vk * lax.rsqrt(jnp.mean(vk * vk, axis=-1, keepdims=True) + EPS))
    return jnp.concatenate(parts, axis=-1) * g


def _ln_silu(v, g, b):
    mu = jnp.mean(v, axis=-1, keepdims=True)
    xc = v - mu
    y = xc * lax.rsqrt(jnp.mean(xc * xc, axis=-1, keepdims=True) + EPS) * g + b
    return jax.nn.silu(y)


def _acc_init(i, *refs):
    @pl.when(i == 0)
    def _():
        for r in refs:
            r[...] = jnp.zeros_like(r)


def _after_token(body, token):
    if token is None:
        return body, [], []

    def body_after(tok_ref, *refs):
        del tok_ref
        body(*refs)

    return body_after, [token], [pl.BlockSpec(memory_space=pl.ANY)]


def _row_spec(tm, n):
    return pl.BlockSpec((tm, n), lambda i: (i, 0))


def _const_spec(shape):
    nd = len(shape)
    return pl.BlockSpec(shape, lambda i: (0,) * nd)


def _prev_halo_spec(tm, halo, n):
    return pl.BlockSpec((halo, n), lambda i: (jnp.maximum(i * (tm // halo) - 1, 0), 0))


def _next_halo_spec(tm, halo, n, t):
    return pl.BlockSpec((halo, n), lambda i: (jnp.minimum((i + 1) * (tm // halo), t // halo - 1), 0))


def _in_proj(x, g1, w_in_p, token=None):
    t = x.shape[0]
    tm = min(TOKEN_TILE, t)
    segs = (SEG_Z, SEG_XBC, SEG_CV, SEG_CG, SEG_DT)

    def body(x_ref, g_ref, w_ref, u_ref, z_ref, xbc_ref, cv_ref, cg_ref, dt_ref):
        u = _rms(x_ref[...], g_ref[...]).astype(bf16)
        u_ref[...] = u
        for (lo, hi), o_ref in zip(segs[:4], (z_ref, xbc_ref, cv_ref, cg_ref)):
            o_ref[...] = _nn(u, w_ref[:, lo:hi])
        dt_ref[...] = _nn(u, w_ref[:, SEG_DT[0]:SEG_DT[1]])[:, :SSD_HEADS]

    outs = (jax.ShapeDtypeStruct((t, D_MODEL), bf16), jax.ShapeDtypeStruct((t, 1024), f32),
            jax.ShapeDtypeStruct((t, XBC_WIDTH), f32), jax.ShapeDtypeStruct((t, 1024), f32),
            jax.ShapeDtypeStruct((t, 1024), f32), jax.ShapeDtypeStruct((t, SSD_HEADS), f32))
    body, tok, tok_spec = _after_token(body, token)
    return pl.pallas_call(
        body, name="in_proj", grid=(t // tm,),
        in_specs=tok_spec + [_row_spec(tm, D_MODEL), _const_spec((1, D_MODEL)), _const_spec((D_MODEL, IN_PERM_WIDTH))],
        out_specs=[_row_spec(tm, D_MODEL), _row_spec(tm, 1024), _row_spec(tm, XBC_WIDTH), _row_spec(tm, 1024),
                   _row_spec(tm, 1024), _row_spec(tm, SSD_HEADS)],
        out_shape=outs, compiler_params=_params("arbitrary"),
    )(*tok, x, g1, w_in_p)


def _conv4_tile(ext_ref, w_ref, b_ref, tm):
    acc = b_ref[...] + w_ref[0:1, :] * ext_ref[pl.ds(8 - 3, tm), :]
    for k in range(1, SSD_CONV):
        acc = acc + w_ref[k:k + 1, :] * ext_ref[pl.ds(8 - 3 + k, tm), :]
    return acc


def _ssd_pre(xbc_raw, dt_raw, cw, cb, dt_bias):
    t = xbc_raw.shape[0]
    tm = min(TOKEN_TILE, t)

    def body(cur_ref, halo_ref, dt_ref, w_ref, b_ref, dtb_ref, act_ref, dto_ref, ext_ref):
        i = pl.program_id(0)
        ext_ref[0:8, :] = jnp.where(i == 0, 0.0, halo_ref[...])
        ext_ref[8:, :] = cur_ref[...]
        act_ref[...] = jax.nn.silu(_conv4_tile(ext_ref, w_ref, b_ref, tm))
        dto_ref[...] = jax.nn.softplus(dt_ref[...] + dtb_ref[...])

    return pl.pallas_call(
        body, name="ssd_pre", grid=(t // tm,),
        in_specs=[_row_spec(tm, XBC_WIDTH), _prev_halo_spec(tm, 8, XBC_WIDTH), _row_spec(tm, SSD_HEADS),
                  _const_spec((8, XBC_WIDTH)), _const_spec((1, XBC_WIDTH)), _const_spec((1, SSD_HEADS))],
        out_specs=[_row_spec(tm, XBC_WIDTH), _row_spec(tm, SSD_HEADS)],
        out_shape=(jax.ShapeDtypeStruct((t, XBC_WIDTH), f32), jax.ShapeDtypeStruct((t, SSD_HEADS), f32)),
        scratch_shapes=[pltpu.VMEM((tm + 8, XBC_WIDTH), f32)],
        compiler_params=_params("arbitrary"),
    )(xbc_raw, xbc_raw, dt_raw, cw, cb, dt_bias)


def _ssd_consts():
    r = lax.broadcasted_iota(jnp.int32, (CHUNK, CHUNK), 0)
    c = lax.broadcasted_iota(jnp.int32, (CHUNK, CHUNK), 1)
    causal = r >= c
    tril = causal.astype(f32)
    triu = (r <= c).astype(f32)
    hh = lax.broadcasted_iota(jnp.int32, (SSD_HEADS, D_MODEL), 0)
    jj = lax.broadcasted_iota(jnp.int32, (SSD_HEADS, D_MODEL), 1)
    expand = (lax.shift_right_logical(jj, 6) == hh).astype(f32)
    lane = lax.broadcasted_iota(jnp.int32, (1, 2 * SSD_HEAD_DIM), 1)
    m0 = (lane < SSD_HEAD_DIM).astype(f32)
    return causal, tril, triu, expand, m0, 1.0 - m0


def _ssd_chunk(xs, bm, cm, dt, s_prev, a_log, d_skip, consts):
    causal, tril, triu, expand, m0, m1 = consts
    hp = lax.Precision.HIGHEST
    a = dt * (-jnp.exp(a_log))
    cs = jnp.dot(tril, a, precision=hp)
    cs_t = lax.dot_general(a, triu, (((0,), (0,)), ((), ())), precision=hp)
    dt_e = jnp.dot(dt, expand, precision=hp)
    cs_e = jnp.dot(cs, expand, precision=hp)
    d_e = jnp.dot(jnp.broadcast_to(d_skip, (8, SSD_HEADS)), expand, precision=hp)[0:1, :]
    xc = xs * dt_e
    cs_last = cs_e[CHUNK - 1:CHUNK, :]
    x_dec = xc * jnp.exp(cs_last - cs_e)
    dec_start = jnp.exp(cs_e)
    chunk_dec = jnp.exp(cs_last)
    gw = D_MODEL // SSD_GROUPS
    ys, states = [], []
    for g in range(SSD_GROUPS):
        bg = bm[:, g * SSD_STATE:(g + 1) * SSD_STATE]
        cg = cm[:, g * SSD_STATE:(g + 1) * SSD_STATE]
        sp = s_prev[:, g * gw:(g + 1) * gw]
        states.append(sp * chunk_dec[:, g * gw:(g + 1) * gw] + btn(bg, x_dec[:, g * gw:(g + 1) * gw]))
        y_off = bnn(cg, sp) * dec_start[:, g * gw:(g + 1) * gw]
        scores = bnt(cg, bg)
        pieces = []
        for pr in range(gw // (2 * SSD_HEAD_DIM)):
            lo = g * gw + pr * 2 * SSD_HEAD_DIM
            xp = xc[:, lo:lo + 2 * SSD_HEAD_DIM]
            acc = None
            for h, mk in ((lo // SSD_HEAD_DIM, m0), (lo // SSD_HEAD_DIM + 1, m1)):
                seg = cs[:, h:h + 1] - cs_t[h:h + 1, :]
                mat = scores * jnp.exp(jnp.where(causal, seg, -jnp.inf))
                term = bnn(mat, xp * mk)
                acc = term if acc is None else acc + term
            pieces.append(acc)
        ys.append(jnp.concatenate(pieces, axis=-1) + y_off)
    y = jnp.concatenate(ys, axis=-1) + xs * d_e
    return y, jnp.concatenate(states, axis=-1)


def _ssd_scan(xbc_act, dt, a_log, d_skip):
    t = xbc_act.shape[0]
    nc = t // CHUNK

    def body(xbc_ref, dt_ref, al_ref, dk_ref, y_ref, sp_ref, state_ref):
        c = pl.program_id(0)

        @pl.when(c == 0)
        def _():
            state_ref[...] = jnp.zeros_like(state_ref)

        s_prev = state_ref[...]
        sp_ref[...] = s_prev
        y, s_next = _ssd_chunk(xbc_ref[:, 0:1024], xbc_ref[:, 1024:1280], xbc_ref[:, 1280:1536], dt_ref[...],
                               s_prev, al_ref[...], dk_ref[...], _ssd_consts())
        y_ref[...] = y
        state_ref[...] = s_next

    return pl.pallas_call(
        body, name="ssd_scan", grid=(nc,),
        in_specs=[_row_spec(CHUNK, XBC_WIDTH), _row_spec(CHUNK, SSD_HEADS), _const_spec((1, SSD_HEADS)),
                  _const_spec((1, SSD_HEADS))],
        out_specs=[_row_spec(CHUNK, D_MODEL), pl.BlockSpec((None, SSD_STATE, D_MODEL), lambda c: (c, 0, 0))],
        out_shape=(jax.ShapeDtypeStruct((t, D_MODEL), f32), jax.ShapeDtypeStruct((nc, SSD_STATE, D_MODEL), f32)),
        scratch_shapes=[pltpu.VMEM((SSD_STATE, D_MODEL), f32)],
        compiler_params=_params("arbitrary"),
    )(xbc_act, dt, a_log, d_skip)


CONV_HALO = 32
CONV_BLOCK = 32
LANE_TILE = 128
SUBLANES = 8


def _fill_glu_ext(ext_ref, sg_ref, cv_ref, cg_ref, hcv_ref, hcg_ref, i):
    ext_ref[0:CONV_HALO, :] = jnp.where(i == 0, 0.0, hcv_ref[...] * jax.nn.sigmoid(hcg_ref[...]))
    sg = jax.nn.sigmoid(cg_ref[...])
    if sg_ref is not None:
        sg_ref[...] = sg
    ext_ref[CONV_HALO:, :] = cv_ref[...] * sg


def _shifted_copies(src_ref, dst_ref, rows):
    for s in range(1, SUBLANES):
        dst_ref[s, 0:rows, :] = src_ref[pl.ds(s, rows), :]


def _tap_source(src_ref, shifted_ref, offset):
    s = offset % SUBLANES
    return (src_ref if s == 0 else shifted_ref.at[s]), offset - s


def _conv_rows(src_ref, shifted_ref, taps, offsets, lanes, r0, init):
    accs = [init] * (CONV_BLOCK // SUBLANES)
    for k, off in enumerate(offsets):
        ref, base = _tap_source(src_ref, shifted_ref, off)
        for j in range(len(accs)):
            accs[j] = accs[j] + taps[k] * ref[pl.ds(r0 + base + SUBLANES * j, SUBLANES), lanes]
    return accs


def _conf_conv(cv, cg, w, b):
    t = cv.shape[0]
    tm = min(TOKEN_TILE, t)
    ext_rows = tm + CONV_HALO
    offsets = [CONV_HALO - (CONF_KERNEL - 1) + k for k in range(CONF_KERNEL)]

    def body(cv_ref, cg_ref, hcv_ref, hcg_ref, w_ref, b_ref, o_ref, ext_ref, shifted_ref):
        _fill_glu_ext(ext_ref, None, cv_ref, cg_ref, hcv_ref, hcg_ref, pl.program_id(0))
        _shifted_copies(ext_ref, shifted_ref, ext_rows - SUBLANES)
        for lb in range(1024 // LANE_TILE):
            lanes = slice(lb * LANE_TILE, (lb + 1) * LANE_TILE)
            taps = [jnp.broadcast_to(w_ref[k:k + 1, lanes], (SUBLANES, LANE_TILE)) for k in range(CONF_KERNEL)]
            bias = jnp.broadcast_to(b_ref[:, lanes], (SUBLANES, LANE_TILE))

            def rows(rb, carry, lanes=lanes, taps=taps, bias=bias):
                r0 = pl.multiple_of(rb * CONV_BLOCK, CONV_BLOCK)
                accs = _conv_rows(ext_ref, shifted_ref, taps, offsets, lanes, r0, bias)
                for j, a in enumerate(accs):
                    o_ref[pl.ds(r0 + SUBLANES * j, SUBLANES), lanes] = a
                return carry

            lax.fori_loop(0, tm // CONV_BLOCK, rows, 0)

    return pl.pallas_call(
        body, name="conf_conv", grid=(t // tm,),
        in_specs=[_row_spec(tm, 1024), _row_spec(tm, 1024), _prev_halo_spec(tm, CONV_HALO, 1024),
                  _prev_halo_spec(tm, CONV_HALO, 1024), _const_spec((32, 1024)), _const_spec((1, 1024))],
        out_specs=_row_spec(tm, 1024), out_shape=jax.ShapeDtypeStruct((t, 1024), f32),
        scratch_shapes=[pltpu.VMEM((ext_rows, 1024), f32), pltpu.VMEM((SUBLANES, ext_rows, 1024), f32)],
        compiler_params=_params("arbitrary"),
    )(cv, cg, cv, cg, w, b)


def _out_proj(y, z, v2, x, g_ssd, ln_g, ln_b, w_out, g2):
    t = x.shape[0]
    tm = min(TOKEN_TILE, t)

    def body(y_ref, z_ref, v_ref, x_ref, gs_ref, lg_ref, lb_ref, w_ref, g2_ref, ys_ref, yc_ref, h1_ref, u2_ref):
        ys = _gated_norm(y_ref[...], z_ref[...], gs_ref[...]).astype(bf16)
        yc = _ln_silu(v_ref[...], lg_ref[...], lb_ref[...]).astype(bf16)
        ys_ref[...] = ys
        yc_ref[...] = yc
        h1 = x_ref[...] + _nn(ys, w_ref[0:1024, :]) + _nn(yc, w_ref[1024:2048, :])
        h1_ref[...] = h1
        u2_ref[...] = _rms(h1, g2_ref[...]).astype(bf16)

    vec = _const_spec((1, 1024))
    row = _row_spec(tm, 1024)
    return pl.pallas_call(
        body, name="out_proj", grid=(t // tm,),
        in_specs=[row, row, row, row, vec, vec, vec, _const_spec((2048, 1024)), vec],
        out_specs=[row, row, row, row],
        out_shape=(jax.ShapeDtypeStruct((t, 1024), bf16), jax.ShapeDtypeStruct((t, 1024), bf16),
                   jax.ShapeDtypeStruct((t, 1024), f32), jax.ShapeDtypeStruct((t, 1024), bf16)),
        compiler_params=_params("arbitrary"),
    )(y, z, v2, x, g_ssd, ln_g, ln_b, w_out, g2)


def _mlp_up(u2, w_up):
    t = u2.shape[0]
    tm = min(TOKEN_TILE, t)
    blk = D_FF // N_DEV

    def body(u_ref, w_ref, pre_ref, hs_ref):
        u = u_ref[...]
        for k in range(N_DEV):
            pre = _nn(u, w_ref[k])
            pre_ref[:, k * blk:(k + 1) * blk] = pre.astype(bf16)
            r = jnp.maximum(pre, 0.0)
            hs_ref[:, k * blk:(k + 1) * blk] = (r * r).astype(bf16)

    return pl.pallas_call(
        body, name="mlp_up", grid=(t // tm,),
        in_specs=[_row_spec(tm, D_MODEL), _const_spec((N_DEV, D_MODEL, D_FF // N_DEV))],
        out_specs=[_row_spec(tm, D_FF), _row_spec(tm, D_FF)],
        out_shape=(jax.ShapeDtypeStruct((t, D_FF), bf16), jax.ShapeDtypeStruct((t, D_FF), bf16)),
        compiler_params=_params("arbitrary"),
    )(u2, w_up)


def _mlp_down(h1, hs, w_down):
    t = h1.shape[0]
    tm = min(TOKEN_TILE, t)

    def body(h_ref, hs_ref, w_ref, o_ref):
        o_ref[...] = h_ref[...] + _nn(hs_ref[...], w_ref[...])

    return pl.pallas_call(
        body, name="mlp_down", grid=(t // tm,),
        in_specs=[_row_spec(tm, D_MODEL), _row_spec(tm, D_FF), _const_spec((D_FF, D_MODEL))],
        out_specs=_row_spec(tm, D_MODEL), out_shape=jax.ShapeDtypeStruct((t, D_MODEL), f32),
        compiler_params=_params("arbitrary"),
    )(h1, hs, w_down)


TAIL_LOSS, TAIL_FINAL_G, TAIL_PLE_G, TAIL_GATE_B, TAIL_GATE_NORM_G = 0, 1, 2, 3, 4


def _tail(h2, p, target, g3, w_gate, b_gate, w_ple, ple_g, fin_g):
    t = h2.shape[0]
    tm = min(TOKEN_TILE, t)

    def body(h_ref, p_ref, t_ref, g3_ref, wg_ref, bg_ref, wp_ref, pg_ref, fg_ref,
             dh_ref, dhb_ref, dwg_ref, dwp_ref, acc_ref):
        i = pl.program_id(0)
        _acc_init(i, dwg_ref, dwp_ref, acc_ref)
        h2v = h_ref[...]
        tgt = t_ref[...]
        u3, vjp_u3 = jax.vjp(_rms, h2v, g3_ref[...])
        u3b = u3.astype(bf16)
        pb = p_ref[...].astype(bf16)
        gate_pre = _nn(u3b, wg_ref[...]) + bg_ref[...]
        emb_pre = _nn(pb, wp_ref[...])

        def tail_fn(hv, gp, ep, pg, fg):
            h3 = hv + jax.nn.sigmoid(gp) * _rms(ep, pg)
            err = _rms(h3, fg) - tgt
            return 0.5 * jnp.mean(err * err, axis=-1, keepdims=True)

        loss_tok, vjp_tail = jax.vjp(tail_fn, h2v, gate_pre, emb_pre, pg_ref[...], fg_ref[...])
        dh_a, dgp, dep, dpg, dfg = vjp_tail(jnp.ones_like(loss_tok))
        dgpb = dgp.astype(bf16)
        dh_b, dg3 = vjp_u3(_nt(dgpb, wg_ref[...]))
        dh = dh_a + dh_b
        dh_ref[...] = dh
        dhb_ref[...] = dh.astype(bf16)
        dwg_ref[...] += _tn(u3b, dgpb)
        dwp_ref[...] += _tn(pb, dep.astype(bf16))
        acc_ref[TAIL_LOSS:TAIL_LOSS + 1, :] += jnp.broadcast_to(jnp.sum(loss_tok, axis=0, keepdims=True), (1, 1024))
        acc_ref[TAIL_FINAL_G:TAIL_FINAL_G + 1, :] += dfg
        acc_ref[TAIL_PLE_G:TAIL_PLE_G + 1, :] += dpg
        acc_ref[TAIL_GATE_B:TAIL_GATE_B + 1, :] += jnp.sum(dgp, axis=0, keepdims=True)
        acc_ref[TAIL_GATE_NORM_G:TAIL_GATE_NORM_G + 1, :] += dg3

    vec = _const_spec((1, 1024))
    row = _row_spec(tm, 1024)
    return pl.pallas_call(
        body, name="tail", grid=(t // tm,),
        in_specs=[row, _row_spec(tm, PLE_DIM), row, vec, _const_spec((1024, 1024)), vec, _const_spec((PLE_DIM, 1024)), vec, vec],
        out_specs=[row, row, _const_spec((1024, 1024)), _const_spec((PLE_DIM, 1024)), _const_spec((8, 1024))],
        out_shape=(jax.ShapeDtypeStruct((t, 1024), f32), jax.ShapeDtypeStruct((t, 1024), bf16),
                   jax.ShapeDtypeStruct((1024, 1024), f32), jax.ShapeDtypeStruct((PLE_DIM, 1024), f32),
                   jax.ShapeDtypeStruct((8, 1024), f32)),
        compiler_params=_params("arbitrary"),
    )(h2, p, target, g3, w_gate, b_gate, w_ple, ple_g, fin_g)


def _mlp_bwd(dh2, dh2b, pre, h1, g2, w_down, w_up, token=None):
    t = dh2.shape[0]
    tm = min(TOKEN_TILE, t)
    blk = D_FF // N_DEV

    def body(dh_ref, dhb_ref, pre_ref, h1_ref, g_ref, wd_hbm, wu_hbm, dpre_ref, dh1_ref, dh1b_ref, acc_ref,
             wd_ref, wu_ref):
        i = pl.program_id(0)
        _acc_init(i, acc_ref)

        @pl.when(i == 0)
        def _():
            pltpu.sync_copy(wd_hbm, wd_ref)
            pltpu.sync_copy(wu_hbm, wu_ref)

        dhb = dhb_ref[...]
        du2 = jnp.zeros((tm, D_MODEL), f32)
        for k in range(D_FF // blk):
            dhs = _nt(dhb, wd_ref[k * blk:(k + 1) * blk, :])
            dpre = (dhs * (2.0 * jnp.maximum(pre_ref[:, k * blk:(k + 1) * blk].astype(f32), 0.0))).astype(bf16)
            dpre_ref[:, k * blk:(k + 1) * blk] = dpre
            du2 = du2 + _nt(dpre, wu_ref[k])
        _, vjp_u2 = jax.vjp(_rms, h1_ref[...], g_ref[...])
        d, dg = vjp_u2(du2)
        dh1 = dh_ref[...] + d
        dh1_ref[...] = dh1
        dh1b_ref[...] = dh1.astype(bf16)
        acc_ref[0:1, :] += dg

    row = _row_spec(tm, 1024)
    body, tok, tok_spec = _after_token(body, token)
    return pl.pallas_call(
        body, name="mlp_bwd", grid=(t // tm,),
        in_specs=tok_spec + [row, row, _row_spec(tm, D_FF), row, _const_spec((1, 1024)), HBM_SPEC, HBM_SPEC],
        out_specs=[_row_spec(tm, D_FF), row, row, _const_spec((8, 1024))],
        out_shape=(jax.ShapeDtypeStruct((t, D_FF), bf16), jax.ShapeDtypeStruct((t, 1024), f32),
                   jax.ShapeDtypeStruct((t, 1024), bf16), jax.ShapeDtypeStruct((8, 1024), f32)),
        scratch_shapes=[pltpu.VMEM((D_FF, D_MODEL), bf16), pltpu.VMEM((N_DEV, D_MODEL, D_FF // N_DEV), bf16)],
        compiler_params=_params("arbitrary"),
    )(*tok, dh2, dh2b, pre, h1, g2, w_down, w_up)


OPB_SSD_G, OPB_LN_G, OPB_LN_B, OPB_CONV_B = 0, 1, 2, 3


def _out_proj_bwd(dh1b, y, z, v2, g_ssd, ln_g, ln_b, w_out, token=None):
    t = y.shape[0]
    tm = min(TOKEN_TILE, t)

    def body(dh_ref, y_ref, z_ref, v_ref, gs_ref, lg_ref, lb_ref, w_ref, dy_ref, dz_ref, dv_ref, acc_ref):
        i = pl.program_id(0)
        _acc_init(i, acc_ref)
        dhb = dh_ref[...]
        dys = _nt(dhb, w_ref[0:1024, :])
        dyc = _nt(dhb, w_ref[1024:2048, :])
        _, vjp_g = jax.vjp(_gated_norm, y_ref[...], z_ref[...], gs_ref[...])
        dy, dz, dgs = vjp_g(dys)
        _, vjp_l = jax.vjp(_ln_silu, v_ref[...], lg_ref[...], lb_ref[...])
        dv, dlg, dlb = vjp_l(dyc)
        dy_ref[...] = dy
        dz_ref[...] = dz.astype(bf16)
        dv_ref[...] = dv
        acc_ref[OPB_SSD_G:OPB_SSD_G + 1, :] += dgs
        acc_ref[OPB_LN_G:OPB_LN_G + 1, :] += dlg
        acc_ref[OPB_LN_B:OPB_LN_B + 1, :] += dlb
        acc_ref[OPB_CONV_B:OPB_CONV_B + 1, :] += jnp.sum(dv, axis=0, keepdims=True)

    vec = _const_spec((1, 1024))
    row = _row_spec(tm, 1024)
    body, tok, tok_spec = _after_token(body, token)
    return pl.pallas_call(
        body, name="out_proj_bwd", grid=(t // tm,),
        in_specs=tok_spec + [row, row, row, row, vec, vec, vec, _const_spec((2048, 1024))],
        out_specs=[row, row, row, _const_spec((8, 1024))],
        out_shape=(jax.ShapeDtypeStruct((t, 1024), f32), jax.ShapeDtypeStruct((t, 1024), bf16),
                   jax.ShapeDtypeStruct((t, 1024), f32), jax.ShapeDtypeStruct((8, 1024), f32)),
        compiler_params=_params("arbitrary"),
    )(*tok, dh1b, y, z, v2, g_ssd, ln_g, ln_b, w_out)


def _conf_conv_bwd(dv2, cv, cg, w):
    t = cv.shape[0]
    tm = min(TOKEN_TILE, t)
    ext_rows = tm + CONV_HALO
    fwd_offsets = [CONV_HALO - (CONF_KERNEL - 1) + k for k in range(CONF_KERNEL)]
    bwd_offsets = [CONF_KERNEL - 1 - k for k in range(CONF_KERNEL)]

    def body(dv_ref, dvn_ref, cv_ref, cg_ref, hcv_ref, hcg_ref, w_ref, dcv_ref, dcg_ref, dw_ref,
             ext_ref, dext_ref, sg_ref, shifted_ref):
        i = pl.program_id(0)
        _acc_init(i, dw_ref)
        _fill_glu_ext(ext_ref, sg_ref, cv_ref, cg_ref, hcv_ref, hcg_ref, i)
        dext_ref[0:tm, :] = dv_ref[...]
        dext_ref[tm:, :] = jnp.where(i == pl.num_programs(0) - 1, 0.0, dvn_ref[...])

        _shifted_copies(ext_ref, shifted_ref, ext_rows - SUBLANES)
        for lb in range(1024 // LANE_TILE):
            lanes = slice(lb * LANE_TILE, (lb + 1) * LANE_TILE)

            def rows_w(rb, accs, lanes=lanes):
                r0 = pl.multiple_of(rb * CONV_BLOCK, CONV_BLOCK)
                accs = list(accs)
                for j in range(CONV_BLOCK // SUBLANES):
                    d = dext_ref[pl.ds(r0 + SUBLANES * j, SUBLANES), lanes]
                    for k, off in enumerate(fwd_offsets):
                        ref, base = _tap_source(ext_ref, shifted_ref, off)
                        accs[k] = accs[k] + d * ref[pl.ds(r0 + base + SUBLANES * j, SUBLANES), lanes]
                return tuple(accs)

            zero = jnp.zeros((SUBLANES, LANE_TILE), f32)
            accs = lax.fori_loop(0, tm // CONV_BLOCK, rows_w, (zero,) * CONF_KERNEL)
            for k in range(CONF_KERNEL):
                dw_ref[k:k + 1, lanes] += jnp.sum(accs[k], axis=0, keepdims=True)

        _shifted_copies(dext_ref, shifted_ref, ext_rows - SUBLANES)
        for lb in range(1024 // LANE_TILE):
            lanes = slice(lb * LANE_TILE, (lb + 1) * LANE_TILE)
            taps = [jnp.broadcast_to(w_ref[k:k + 1, lanes], (SUBLANES, LANE_TILE)) for k in range(CONF_KERNEL)]

            def rows_x(rb, carry, lanes=lanes, taps=taps):
                r0 = pl.multiple_of(rb * CONV_BLOCK, CONV_BLOCK)
                zero = jnp.zeros((SUBLANES, LANE_TILE), f32)
                dglu = jnp.concatenate(_conv_rows(dext_ref, shifted_ref, taps, bwd_offsets, lanes, r0, zero), axis=0)
                sg = sg_ref[pl.ds(r0, CONV_BLOCK), lanes]
                cvv = cv_ref[pl.ds(r0, CONV_BLOCK), lanes]
                dcv_ref[pl.ds(r0, CONV_BLOCK), lanes] = (dglu * sg).astype(bf16)
                dcg_ref[pl.ds(r0, CONV_BLOCK), lanes] = (dglu * cvv * sg * (1.0 - sg)).astype(bf16)
                return carry

            lax.fori_loop(0, tm // CONV_BLOCK, rows_x, 0)

    row = _row_spec(tm, 1024)
    return pl.pallas_call(
        body, name="conf_conv_bwd", grid=(t // tm,),
        in_specs=[row, _next_halo_spec(tm, CONV_HALO, 1024, t), row, row, _prev_halo_spec(tm, CONV_HALO, 1024),
                  _prev_halo_spec(tm, CONV_HALO, 1024), _const_spec((32, 1024))],
        out_specs=[row, row, _const_spec((32, 1024))],
        out_shape=(jax.ShapeDtypeStruct((t, 1024), bf16), jax.ShapeDtypeStruct((t, 1024), bf16),
                   jax.ShapeDtypeStruct((32, 1024), f32)),
        scratch_shapes=[pltpu.VMEM((ext_rows, 1024), f32), pltpu.VMEM((ext_rows, 1024), f32), pltpu.VMEM((tm, 1024), f32),
                        pltpu.VMEM((SUBLANES, ext_rows, 1024), f32)],
        compiler_params=_params("arbitrary"),
    )(dv2, dv2, cv, cg, cv, cg, w)


def _ssd_scan_bwd(xbc_act, dt, s_prev, dy, a_log, d_skip):
    t = xbc_act.shape[0]
    nc = t // CHUNK

    def body(xbc_ref, dt_ref, sp_ref, dy_ref, al_ref, dk_ref, dxbc_ref, ddt_ref, dal_ref, ddk_ref, ds_ref):
        i = pl.program_id(0)
        _acc_init(i, ds_ref, dal_ref, ddk_ref)
        consts = _ssd_consts()
        _, vjp_c = jax.vjp(
            functools.partial(_ssd_chunk, consts=consts),
            xbc_ref[:, 0:1024], xbc_ref[:, 1024:1280], xbc_ref[:, 1280:1536], dt_ref[...], sp_ref[...],
            al_ref[...], dk_ref[...])
        dxs, dbm, dcm, ddt, dsp, dal, ddk = vjp_c((dy_ref[...], ds_ref[...]))
        dxbc_ref[:, 0:1024] = dxs
        dxbc_ref[:, 1024:1280] = dbm
        dxbc_ref[:, 1280:1536] = dcm
        ddt_ref[...] = ddt
        ds_ref[...] = dsp
        dal_ref[...] += dal
        ddk_ref[...] += ddk

    rev = lambda i: (nc - 1 - i, 0)
    return pl.pallas_call(
        body, name="ssd_scan_bwd", grid=(nc,),
        in_specs=[pl.BlockSpec((CHUNK, XBC_WIDTH), rev), pl.BlockSpec((CHUNK, SSD_HEADS), rev),
                  pl.BlockSpec((None, SSD_STATE, D_MODEL), lambda i: (nc - 1 - i, 0, 0)),
                  pl.BlockSpec((CHUNK, D_MODEL), rev), _const_spec((1, SSD_HEADS)), _const_spec((1, SSD_HEADS))],
        out_specs=[pl.BlockSpec((CHUNK, XBC_WIDTH), rev), pl.BlockSpec((CHUNK, SSD_HEADS), rev),
                   _const_spec((1, SSD_HEADS)), _const_spec((1, SSD_HEADS))],
        out_shape=(jax.ShapeDtypeStruct((t, XBC_WIDTH), f32), jax.ShapeDtypeStruct((t, SSD_HEADS), f32),
                   jax.ShapeDtypeStruct((1, SSD_HEADS), f32), jax.ShapeDtypeStruct((1, SSD_HEADS), f32)),
        scratch_shapes=[pltpu.VMEM((SSD_STATE, D_MODEL), f32)],
        compiler_params=_params("arbitrary"),
    )(xbc_act, dt, s_prev, dy, a_log, d_skip)


def _ssd_pre_bwd(xbc_raw, dxbc_act, ddt, dt_raw, cw, cb, dt_bias):
    t = xbc_raw.shape[0]
    tm = min(TOKEN_TILE, t)

    def body(cur_ref, halo_ref, dact_ref, ddt_ref, dtr_ref, w_ref, b_ref, dtb_ref,
             dco_ref, ddtr_ref, dw_ref, db_ref, ddtb_ref, ext_ref):
        i = pl.program_id(0)
        _acc_init(i, dw_ref, db_ref, ddtb_ref)
        ext_ref[0:8, :] = jnp.where(i == 0, 0.0, halo_ref[...])
        ext_ref[8:, :] = cur_ref[...]
        co = _conv4_tile(ext_ref, w_ref, b_ref, tm)
        sg = jax.nn.sigmoid(co)
        dco = dact_ref[...] * sg * (1.0 + co * (1.0 - sg))
        dco_ref[...] = dco
        db_ref[...] += jnp.sum(dco, axis=0, keepdims=True)
        for k in range(SSD_CONV):
            dw_ref[k:k + 1, :] += jnp.sum(dco * ext_ref[pl.ds(8 - 3 + k, tm), :], axis=0, keepdims=True)
        ddtr = ddt_ref[...] * jax.nn.sigmoid(dtr_ref[...] + dtb_ref[...])
        ddtb_ref[...] += jnp.sum(ddtr, axis=0, keepdims=True)
        r = lax.broadcasted_iota(jnp.int32, (SSD_HEADS, DT_PAD), 0)
        c = lax.broadcasted_iota(jnp.int32, (SSD_HEADS, DT_PAD), 1)
        ddtr_ref[...] = jnp.dot(ddtr, (r == c).astype(f32), precision=lax.Precision.HIGHEST).astype(bf16)

    return pl.pallas_call(
        body, name="ssd_pre_bwd", grid=(t // tm,),
        in_specs=[_row_spec(tm, XBC_WIDTH), _prev_halo_spec(tm, 8, XBC_WIDTH), _row_spec(tm, XBC_WIDTH),
                  _row_spec(tm, SSD_HEADS), _row_spec(tm, SSD_HEADS), _const_spec((8, XBC_WIDTH)),
                  _const_spec((1, XBC_WIDTH)), _const_spec((1, SSD_HEADS))],
        out_specs=[_row_spec(tm, XBC_WIDTH), _row_spec(tm, DT_PAD), _const_spec((8, XBC_WIDTH)),
                   _const_spec((1, XBC_WIDTH)), _const_spec((1, SSD_HEADS))],
        out_shape=(jax.ShapeDtypeStruct((t, XBC_WIDTH), f32), jax.ShapeDtypeStruct((t, DT_PAD), bf16),
                   jax.ShapeDtypeStruct((8, XBC_WIDTH), f32), jax.ShapeDtypeStruct((1, XBC_WIDTH), f32),
                   jax.ShapeDtypeStruct((1, SSD_HEADS), f32)),
        scratch_shapes=[pltpu.VMEM((tm + 8, XBC_WIDTH), f32)],
        compiler_params=_params("arbitrary"),
    )(xbc_raw, xbc_raw, dxbc_act, ddt, dt_raw, cw, cb, dt_bias)


def _conv4_bwd_data(dco, cw):
    t = dco.shape[0]
    tm = min(TOKEN_TILE, t)

    def body(cur_ref, nxt_ref, w_ref, o_ref, ext_ref):
        i = pl.program_id(0)
        ext_ref[0:tm, :] = cur_ref[...]
        ext_ref[tm:, :] = jnp.where(i == pl.num_programs(0) - 1, 0.0, nxt_ref[...])
        acc = w_ref[0:1, :] * ext_ref[pl.ds(SSD_CONV - 1, tm), :]
        for k in range(1, SSD_CONV):
            acc = acc + w_ref[k:k + 1, :] * ext_ref[pl.ds(SSD_CONV - 1 - k, tm), :]
        o_ref[...] = acc.astype(bf16)

    return pl.pallas_call(
        body, name="conv4_bwd_data", grid=(t // tm,),
        in_specs=[_row_spec(tm, XBC_WIDTH), _next_halo_spec(tm, 8, XBC_WIDTH, t), _const_spec((8, XBC_WIDTH))],
        out_specs=_row_spec(tm, XBC_WIDTH), out_shape=jax.ShapeDtypeStruct((t, XBC_WIDTH), bf16),
        scratch_shapes=[pltpu.VMEM((tm + 8, XBC_WIDTH), f32)],
        compiler_params=_params("arbitrary"),
    )(dco, dco, cw)


def _in_proj_bwd(dz, dxbc, dcv, dcg, ddt, x, dh1, g1, w_in_p, token=None):
    t = x.shape[0]
    tm = min(TOKEN_TILE, t)
    segs = (SEG_Z, SEG_XBC, SEG_CV, SEG_CG, SEG_DT)

    def body(dz_ref, dxbc_ref, dcv_ref, dcg_ref, ddt_ref, x_ref, dh_ref, g_ref, w_ref, dx_ref, acc_ref):
        i = pl.program_id(0)
        _acc_init(i, acc_ref)
        du = jnp.zeros((tm, D_MODEL), f32)
        for (lo, hi), r in zip(segs, (dz_ref, dxbc_ref, dcv_ref, dcg_ref, ddt_ref)):
            du = du + _nt(r[...], w_ref[:, lo:hi])
        _, vjp_u = jax.vjp(_rms, x_ref[...], g_ref[...])
        d, dg = vjp_u(du)
        dx_ref[...] = dh_ref[...] + d
        acc_ref[0:1, :] += dg

    row = _row_spec(tm, 1024)
    body, tok, tok_spec = _after_token(body, token)
    return pl.pallas_call(
        body, name="in_proj_bwd", grid=(t // tm,),
        in_specs=tok_spec + [row, _row_spec(tm, XBC_WIDTH), row, row, _row_spec(tm, DT_PAD), row, row, _const_spec((1, 1024)),
                  _const_spec((D_MODEL, IN_PERM_WIDTH))],
        out_specs=[row, _const_spec((8, 1024))],
        out_shape=(jax.ShapeDtypeStruct((t, 1024), f32), jax.ShapeDtypeStruct((8, 1024), f32)),
        compiler_params=_params("arbitrary"),
    )(*tok, dz, dxbc, dcv, dcg, ddt, x, dh1, g1, w_in_p)


def _mm_tn(a, b, tk, tn, name, column_blocks=False):
    t, kk = a.shape
    n = b.shape[1]
    tk, tn = min(tk, kk), min(tn, n)

    def body(a_ref, b_ref, o_ref):
        o_ref[...] = _tn(a_ref[...], b_ref[...]).astype(bf16)

    if column_blocks:
        assert tk == kk
        out_spec = pl.BlockSpec((None, tk, tn), lambda i, j: (j, 0, 0))
        out_shape = jax.ShapeDtypeStruct((n // tn, kk, tn), bf16)
    else:
        out_spec = pl.BlockSpec((tk, tn), lambda i, j: (i, j))
        out_shape = jax.ShapeDtypeStruct((kk, n), bf16)
    return pl.pallas_call(
        body, name=name, grid=(kk // tk, n // tn),
        in_specs=[pl.BlockSpec((t, tk), lambda i, j: (0, i)), pl.BlockSpec((t, tn), lambda i, j: (0, j))],
        out_specs=out_spec, out_shape=out_shape,
        compiler_params=_params("arbitrary", "arbitrary"),
    )(a, b)


_REF_DT = (2560, 2576)


def _perm_w_in(w_full):
    pad = jnp.zeros((w_full.shape[0], DT_PAD - SSD_HEADS), w_full.dtype)
    return jnp.concatenate([w_full[:, :_REF_DT[0]], w_full[:, _REF_DT[1]:], w_full[:, _REF_DT[0]:_REF_DT[1]], pad], axis=1)


def _unperm_dw_in(segs):
    dz, dxbc, dcv, dcg, ddt = segs
    return jnp.concatenate([dz, dxbc, ddt[:, :SSD_HEADS], dcv, dcg], axis=1)


class _LocalWeights:
    def __init__(self, w):
        self.w = w
        self.sent = {}

    def start_token(self):
        return None

    def weight(self, name, after=()):
        del after
        return self.w[name]

    def send_grads(self, grads):
        self.sent.update(grads)
        return None


def _local_step(x, p, target, s, comm):
    conv_w, conf_w = comm.weight("ssd_conv_w"), comm.weight("conf_dw_w")
    w_in = comm.weight("w_in")
    u, z, xbc_raw, cv, cg, dt_raw = _in_proj(x, s["mix_norm_g"], w_in, token=comm.start_token())
    xbc_act, dt = _ssd_pre(xbc_raw, dt_raw, conv_w, s["ssd_conv_b"], s["dt_bias"])
    y, s_prev = _ssd_scan(xbc_act, dt, s["A_log"], s["D_skip"])
    v2 = _conf_conv(cv, cg, conf_w, s["conf_dw_b"])
    w_out = comm.weight("w_out", after=(y, v2))
    ys, yc, h1, u2 = _out_proj(y, z, v2, x, s["ssd_norm_g"], s["conf_ln_g"], s["conf_ln_b"], w_out, s["mlp_norm_g"])
    w_up = comm.weight("w_up", after=(u2,))
    pre, hs = _mlp_up(u2, w_up)
    w_down = comm.weight("w_down", after=(hs,))
    h2 = _mlp_down(h1, hs, w_down)
    w_gate, w_ple = comm.weight("w_ple_gate", after=(h2,)), comm.weight("w_ple", after=(h2,))
    dh2, dh2b, dwg, dwp, tail_acc = _tail(h2, p, target, s["ple_gate_norm_g"], w_gate, s["b_ple_gate"], w_ple,
                                          s["ple_norm_g"], s["final_norm_g"])
    token = comm.send_grads({"w_ple_gate": dwg, "w_ple": dwp})
    dpre, dh1, dh1b, mlp_acc = _mlp_bwd(dh2, dh2b, pre, h1, s["mlp_norm_g"], w_down, w_up, token=token)
    token = comm.send_grads({
        "w_down": _mm_tn(hs, dh2b, 512, 1024, "dw_down"),
        "w_up": _mm_tn(u2, dpre, 1024, D_FF // N_DEV, "dw_up", column_blocks=True),
        "w_out": [_mm_tn(ys, dh1b, 1024, 512, "dw_out_ssd"), _mm_tn(yc, dh1b, 1024, 512, "dw_out_conf")],
    })
    dy, dz, dv2, opb_acc = _out_proj_bwd(dh1b, y, z, v2, s["ssd_norm_g"], s["conf_ln_g"], s["conf_ln_b"], w_out,
                                         token=token)
    dcv, dcg, dconf_w = _conf_conv_bwd(dv2, cv, cg, conf_w)
    dxbc_act, ddt, d_alog, d_dskip = _ssd_scan_bwd(xbc_act, dt, s_prev, dy, s["A_log"], s["D_skip"])
    dco, ddt_raw, dconv_w, dconv_b, d_dtb = _ssd_pre_bwd(xbc_raw, dxbc_act, ddt, dt_raw, conv_w, s["ssd_conv_b"],
                                                        s["dt_bias"])
    dxbc_raw = _conv4_bwd_data(dco, conv_w)
    token = comm.send_grads({"w_in": [_mm_tn(u, d, 1024, 512, "dw_in_" + n) for n, d in
                                      (("z", dz), ("xbc", dxbc_raw), ("cv", dcv), ("cg", dcg), ("dt", ddt_raw))]})
    grad_x, inp_acc = _in_proj_bwd(dz, dxbc_raw, dcv, dcg, ddt_raw, x, dh1, s["mix_norm_g"], w_in, token=token)
    acc = {"in_proj": inp_acc, "out_proj": opb_acc, "mlp": mlp_acc, "tail": tail_acc, "ssd_conv_b": dconv_b,
           "dt_bias": d_dtb, "A_log": d_alog, "D_skip": d_dskip, "ssd_conv_w": dconv_w, "conf_dw_w": dconf_w}
    return grad_x, acc


def _mesh_pos():
    return lax.axis_index("x"), lax.axis_index("y"), lax.axis_index("c")


def _other_chips(x, y):
    return [(1 - x, y), (x, 1 - y), (1 - x, 1 - y)]


def _all_gather(arrs, name):
    n = len(arrs)

    def body(*refs):
        ins, outs = refs[:n], refs[n:2 * n]
        send_sems, recv_sems, local_sems = refs[2 * n:]
        x, y, c = _mesh_pos()
        me = 4 * x + 2 * y + c
        sibling = (x, y, 1 - c)
        chips = _other_chips(x, y)

        def copy(a, k, block, to, src=None):
            dst = outs[a].at[block]
            return pltpu.make_async_remote_copy(
                src_ref=dst if src is None else src, dst_ref=dst, send_sem=send_sems.at[a, k],
                recv_sem=recv_sems.at[a, k], device_id=to, device_id_type=MESH_ID)

        mine = [pltpu.make_async_copy(ins[a], outs[a].at[me], local_sems.at[a]) for a in range(n)]
        for cp in mine:
            cp.start()
        first = []
        for a in range(n):
            first.append(copy(a, 0, me, sibling, src=ins[a]))
            first += [copy(a, 1 + j, me, (px, py, c), src=ins[a]) for j, (px, py) in enumerate(chips)]
        for cp in first:
            cp.start()
        passed = []
        for j, (px, py) in enumerate(chips):
            for a in range(n):
                blk = 4 * px + 2 * py + c
                copy(a, 1 + j, blk, (x, y, c)).wait_recv()
                cp = copy(a, 4 + j, blk, sibling)
                cp.start()
                passed.append(cp)
        for a in range(n):
            copy(a, 0, 4 * x + 2 * y + (1 - c), (x, y, c)).wait_recv()
        for j, (px, py) in enumerate(chips):
            for a in range(n):
                copy(a, 4 + j, 4 * px + 2 * py + (1 - c), (x, y, c)).wait_recv()
        for cp in first + passed:
            cp.wait_send()
        for cp in mine:
            cp.wait()

    return pl.pallas_call(
        body, name=name,
        in_specs=[HBM_SPEC] * n, out_specs=[HBM_SPEC] * n,
        out_shape=[jax.ShapeDtypeStruct((N_DEV,) + a.shape, a.dtype) for a in arrs],
        scratch_shapes=[pltpu.SemaphoreType.DMA((n, 7)), pltpu.SemaphoreType.DMA((n, 7)), pltpu.SemaphoreType.DMA((n,))],
    )(*arrs)


_PEER_FLIPS = ((0, 0, 1), (1, 0, 0), (0, 1, 0), (1, 1, 0), (1, 0, 1), (0, 1, 1), (1, 1, 1))
SEM_SPEC = pl.BlockSpec(memory_space=pltpu.SEMAPHORE)
ANY_SPEC = pl.BlockSpec(memory_space=pl.ANY)


def _flip(v, d):
    return 1 - v if d else v


def _peers(x, y, c):
    out = []
    for dx, dy, dc in _PEER_FLIPS:
        px, py, pc = _flip(x, dx), _flip(y, dy), _flip(c, dc)
        out.append(((px, py, pc), 4 * px + 2 * py + pc))
    return out


def _exchange_copy(src_ref, land_ref, send_sems, recv_sems, k, peer, peer_block, my_block, by_block, outgoing):
    src = src_ref.at[peer_block] if by_block else src_ref
    dst = land_ref.at[my_block if outgoing else peer_block]
    return pltpu.make_async_remote_copy(src_ref=src, dst_ref=dst, send_sem=send_sems.at[k], recv_sem=recv_sems.at[k],
                                        device_id=peer, device_id_type=MESH_ID)


def _exchange_start(srcs, lands, by_block, name, after):
    n = len(srcs)

    def body(*refs):
        src_refs, land_refs = refs[1:1 + n], refs[1 + n:1 + 2 * n]
        outs = refs[1 + 2 * n:]
        send, recv, token = outs[:n], outs[n:2 * n], outs[4 * n]
        x, y, c = _mesh_pos()
        me = 4 * x + 2 * y + c
        for a in range(n):
            for k, (peer, blk) in enumerate(_peers(x, y, c)):
                _exchange_copy(src_refs[a], land_refs[a], send[a], recv[a], k, peer, blk, me, by_block, True).start()
        token[...] = jnp.zeros_like(token)

    sems = [pltpu.SemaphoreType.DMA((N_DEV - 1,))] * (2 * n)
    thru = [pltpu.HBM(a.shape, a.dtype) for a in list(srcs) + list(lands)]
    res = pl.pallas_call(
        body, name=name,
        in_specs=[ANY_SPEC] + [HBM_SPEC] * (2 * n),
        out_specs=[SEM_SPEC] * (2 * n) + [HBM_SPEC] * (2 * n) + [pl.BlockSpec(memory_space=pltpu.VMEM)],
        out_shape=sems + thru + [jax.ShapeDtypeStruct((8, 128), f32)],
        input_output_aliases={1 + i: 2 * n + i for i in range(2 * n)},
        compiler_params=pltpu.CompilerParams(has_side_effects=pltpu.SideEffectType.DATAFLOW_SIDE_EFFECTING),
    )(after, *[pltpu.with_memory_space_constraint(a, pltpu.HBM) for a in list(srcs) + list(lands)])
    states = [(res[2 * n + a], res[3 * n + a], res[a], res[n + a]) for a in range(n)]
    return states, res[4 * n]


def _exchange_wait(states, by_block, name, after):
    n, na = len(states), len(after)

    def body(*refs):
        src_refs, land_refs = refs[:n], refs[n:2 * n]
        send, recv = refs[2 * n:3 * n], refs[3 * n:4 * n]
        x, y, c = _mesh_pos()
        me = 4 * x + 2 * y + c
        for a in range(n):
            for k, (peer, blk) in enumerate(_peers(x, y, c)):
                _exchange_copy(src_refs[a], land_refs[a], send[a], recv[a], k, peer, blk, me, by_block, True).wait_send()
                _exchange_copy(src_refs[a], land_refs[a], send[a], recv[a], k, peer, blk, me, by_block, False).wait_recv()

    srcs, lands = [s[0] for s in states], [s[1] for s in states]
    res = pl.pallas_call(
        body, name=name,
        in_specs=[HBM_SPEC] * (2 * n) + [SEM_SPEC] * (2 * n) + [ANY_SPEC] * na,
        out_specs=[HBM_SPEC] * (2 * n),
        out_shape=[pltpu.HBM(a.shape, a.dtype) for a in srcs + lands],
        input_output_aliases={i: i for i in range(2 * n)},
        compiler_params=pltpu.CompilerParams(has_side_effects=pltpu.SideEffectType.DATAFLOW_SIDE_EFFECTING),
    )(*srcs, *lands, *[s[2] for s in states], *[s[3] for s in states], *after)
    return list(res[n:2 * n])


def _with_own_block(own, me):
    buf = lax.empty((N_DEV,) + own.shape, own.dtype)
    return lax.dynamic_update_slice(buf, own[None], (me,) + (0,) * own.ndim)


def _adamw_math(w, g, m, v):
    m = ADAM_B1 * m + (1.0 - ADAM_B1) * g
    v = ADAM_B2 * v + (1.0 - ADAM_B2) * (g * g)
    m_hat = m / (1.0 - ADAM_B1 ** ADAM_STEP)
    v_hat = v / (1.0 - ADAM_B2 ** ADAM_STEP)
    delta = -ADAM_LR * (m_hat / (jnp.sqrt(v_hat) + ADAM_EPS) + ADAM_WD * w)
    return delta, m, v


def _adamw_big(parts, w, m, v, name):
    rows, cols = w.shape
    tr = 256 if rows % 256 == 0 else rows
    nparts = parts.shape[0]

    def body(p_ref, w_ref, m_ref, v_ref, g_ref, d_ref, mo_ref, vo_ref):
        g = p_ref[0].astype(f32)
        for j in range(1, nparts):
            g = g + p_ref[j].astype(f32)
        d, mn, vn = _adamw_math(w_ref[...], g, m_ref[...], v_ref[...])
        g_ref[...] = g
        d_ref[...] = d
        mo_ref[...] = mn
        vo_ref[...] = vn

    row = pl.BlockSpec((tr, cols), lambda i: (i, 0))
    shp = jax.ShapeDtypeStruct((rows, cols), f32)
    return pl.pallas_call(
        body, name=name, grid=(rows // tr,),
        in_specs=[pl.BlockSpec((nparts, tr, cols), lambda i: (0, i, 0)), row, row, row],
        out_specs=[row, row, row, row], out_shape=(shp, shp, shp, shp),
        compiler_params=_params("arbitrary"),
    )(parts, w, m, v)


PACK_ROWS = 56
_PACK_AT = {
    "mix_norm_g": (0, 0, 1024), "ssd_norm_g": (1, 0, 1024), "conf_ln_g": (2, 0, 1024), "conf_ln_b": (3, 0, 1024),
    "conf_dw_b": (4, 0, 1024), "mlp_norm_g": (5, 0, 1024), "final_norm_g": (7, 0, 1024), "ple_norm_g": (8, 0, 1024),
    "b_ple_gate": (9, 0, 1024), "ple_gate_norm_g": (10, 0, 1024), "dt_bias": (13, 0, 16), "A_log": (13, 128, 16),
    "D_skip": (13, 256, 16),
}
PACK_LOSS_ROW = 6
PACK_CONV_B_ROW = 11
PACK_CONV_W_ROW = 14
PACK_CONF_W_ROW = 24


def _pack_small(acc):
    def body(inp, opb, mlp, tail, cb, dtb, alog, dskip, cw, fw, o_ref):
        o_ref[...] = jnp.zeros_like(o_ref)
        rows = {"mix_norm_g": inp[0:1, :], "mlp_norm_g": mlp[0:1, :],
                "ssd_norm_g": opb[OPB_SSD_G:OPB_SSD_G + 1, :], "conf_ln_g": opb[OPB_LN_G:OPB_LN_G + 1, :],
                "conf_ln_b": opb[OPB_LN_B:OPB_LN_B + 1, :], "conf_dw_b": opb[OPB_CONV_B:OPB_CONV_B + 1, :],
                "final_norm_g": tail[TAIL_FINAL_G:TAIL_FINAL_G + 1, :], "ple_norm_g": tail[TAIL_PLE_G:TAIL_PLE_G + 1, :],
                "b_ple_gate": tail[TAIL_GATE_B:TAIL_GATE_B + 1, :],
                "ple_gate_norm_g": tail[TAIL_GATE_NORM_G:TAIL_GATE_NORM_G + 1, :],
                "dt_bias": dtb[...], "A_log": alog[...], "D_skip": dskip[...]}
        for name, val in rows.items():
            r, lo, width = _PACK_AT[name]
            o_ref[r:r + 1, lo:lo + width] = val
        o_ref[PACK_LOSS_ROW:PACK_LOSS_ROW + 1, :] = tail[TAIL_LOSS:TAIL_LOSS + 1, :]
        o_ref[PACK_CONV_B_ROW:PACK_CONV_B_ROW + 1, :] = cb[:, 0:1024]
        o_ref[PACK_CONV_B_ROW + 1:PACK_CONV_B_ROW + 2, 0:512] = cb[:, 1024:XBC_WIDTH]
        for k in range(SSD_CONV):
            o_ref[PACK_CONV_W_ROW + k:PACK_CONV_W_ROW + k + 1, :] = cw[k:k + 1, 0:1024]
            o_ref[PACK_CONV_W_ROW + SSD_CONV + k:PACK_CONV_W_ROW + SSD_CONV + k + 1, 0:512] = cw[k:k + 1, 1024:XBC_WIDTH]
        o_ref[PACK_CONF_W_ROW:PACK_CONF_W_ROW + 32, :] = fw[...]

    return pl.pallas_call(body, name="pack_small", out_shape=jax.ShapeDtypeStruct((PACK_ROWS, 1024), f32))(
        acc["in_proj"], acc["out_proj"], acc["mlp"], acc["tail"], acc["ssd_conv_b"], acc["dt_bias"], acc["A_log"],
        acc["D_skip"], acc["ssd_conv_w"], acc["conf_dw_w"])


def _small_update(all_small, w, m, v):
    names = _REPLICATED

    def body(all_ref, *refs):
        ins, outs = refs[:3 * len(names)], refs[3 * len(names):]
        s = all_ref[0]
        for j in range(1, N_DEV):
            s = s + all_ref[j]
        outs[0][...] = s[PACK_LOSS_ROW:PACK_LOSS_ROW + 1, 0:1]
        outs[1][...] = jnp.concatenate([s[PACK_CONV_W_ROW:PACK_CONV_W_ROW + SSD_CONV, :],
                                        s[PACK_CONV_W_ROW + SSD_CONV:PACK_CONV_W_ROW + 2 * SSD_CONV, 0:512]], axis=1)
        outs[2][...] = s[PACK_CONF_W_ROW:PACK_CONF_W_ROW + CONF_KERNEL, :]
        for i, name in enumerate(names):
            if name == "ssd_conv_b":
                g = jnp.concatenate([s[PACK_CONV_B_ROW:PACK_CONV_B_ROW + 1, :],
                                     s[PACK_CONV_B_ROW + 1:PACK_CONV_B_ROW + 2, 0:512]], axis=1)
            else:
                r, lo, width = _PACK_AT[name]
                g = s[r:r + 1, lo:lo + width]
            d, mn, vn = _adamw_math(ins[3 * i][...], g, ins[3 * i + 1][...], ins[3 * i + 2][...])
            for o_ref, val in zip(outs[3 + 4 * i:7 + 4 * i], (g, d, mn, vn)):
                o_ref[...] = val

    shapes = [jax.ShapeDtypeStruct((1, 1), f32), jax.ShapeDtypeStruct((SSD_CONV, XBC_WIDTH), f32),
              jax.ShapeDtypeStruct((CONF_KERNEL, D_MODEL), f32)]
    operands = []
    for name in names:
        operands += [w[name], m[name], v[name]]
        shapes += [jax.ShapeDtypeStruct(w[name].shape, f32)] * 4
    res = pl.pallas_call(body, name="small_update", out_shape=shapes)(all_small, *operands)
    per_name = {name: tuple(res[3 + 4 * i:7 + 4 * i]) for i, name in enumerate(names)}
    return res[0], per_name, res[1], res[2]


def _adamw_filters(g, w, m, v):
    n = len(g)

    def body(*refs):
        ins, outs = refs[:4 * n], refs[4 * n:]
        for i in range(n):
            d, mn, vn = _adamw_math(ins[4 * i + 1][...], ins[4 * i][...], ins[4 * i + 2][...], ins[4 * i + 3][...])
            for o_ref, val in zip(outs[3 * i:3 * i + 3], (d, mn, vn)):
                o_ref[...] = val

    operands, shapes = [], []
    for i in range(n):
        operands += [g[i], w[i], m[i], v[i]]
        shapes += [jax.ShapeDtypeStruct(w[i].shape, f32)] * 3
    res = pl.pallas_call(body, name="adamw_filters", out_shape=shapes)(*operands)
    return [tuple(res[3 * i:3 * i + 3]) for i in range(n)]


_REPLICATED = ("mix_norm_g", "ssd_conv_b", "dt_bias", "A_log", "D_skip", "ssd_norm_g", "conf_dw_b", "conf_ln_g",
               "conf_ln_b", "mlp_norm_g", "ple_gate_norm_g", "b_ple_gate", "ple_norm_g", "final_norm_g")
_CONV_WEIGHTS = ("ssd_conv_w", "conf_dw_w")
_BIG = ("w_in", "w_out", "w_up", "w_down", "w_ple_gate", "w_ple")
_WEIGHT_ORDER = ("mix_norm_g", "w_in", "ssd_conv_w", "ssd_conv_b", "dt_bias", "A_log", "D_skip", "ssd_norm_g", "conf_dw_w",
                 "conf_dw_b", "conf_ln_g", "conf_ln_b", "w_out", "mlp_norm_g", "w_up", "w_down", "ple_gate_norm_g",
                 "w_ple_gate", "b_ple_gate", "w_ple", "ple_norm_g", "final_norm_g")


def _blocks_of_columns(a):
    r, c8 = a.shape
    return jnp.transpose(a.reshape(r, N_DEV, c8 // N_DEV), (1, 0, 2))


def _columns_of_blocks(a):
    _, r, c = a.shape
    return jnp.transpose(a, (1, 0, 2)).reshape(r, N_DEV * c)


_LATER = ("w_out", "w_up", "w_down", "w_ple_gate", "w_ple")
_WHOLE = {
    "w_out": lambda a: a.reshape(2048, D_MODEL),
    "w_up": lambda a: a,
    "w_down": lambda a: a.reshape(D_FF, D_MODEL),
    "w_ple_gate": lambda a: a.reshape(D_MODEL, D_MODEL),
    "w_ple": _columns_of_blocks,
}
_BY_BLOCK = {
    "w_in": lambda g: _blocks_of_columns(_unperm_dw_in(g)),
    "w_out": lambda g: jnp.concatenate(g, axis=0).reshape(N_DEV, 256, D_MODEL),
    "w_up": lambda g: g,
    "w_down": lambda g: g.reshape(N_DEV, 512, D_MODEL),
    "w_ple_gate": lambda g: g.astype(bf16).reshape(N_DEV, 128, D_MODEL),
    "w_ple": lambda g: _blocks_of_columns(g.astype(bf16)),
}


class _StepComm:
    def __init__(self, me, ready, gathers, token):
        self.me, self.ready, self.gathers, self.token = me, ready, gathers, token
        self.sent = []

    def start_token(self):
        return self.token

    def weight(self, name, after=()):
        if name not in self.ready:
            (land,) = _exchange_wait([self.gathers[name]], False, "gather_wait_" + name, list(after))
            self.ready[name] = _WHOLE[name](land)
        return self.ready[name]

    def send_grads(self, grads):
        names = list(grads)
        blocks = [_BY_BLOCK[n](grads[n]) for n in names]
        lands = [_with_own_block(lax.dynamic_index_in_dim(b, self.me, 0, keepdims=False), self.me) for b in blocks]
        states, self.token = _exchange_start(blocks, lands, True, "scatter_start_" + names[0], self.token)
        self.sent.append((names, states))
        return self.token


def kernel(x, p, mix_norm_g, w_in, ssd_conv_w, ssd_conv_b, dt_bias, A_log, D_skip, ssd_norm_g, conf_dw_w, conf_dw_b, conf_ln_g, conf_ln_b, w_out, mlp_norm_g, w_up, w_down, ple_gate_norm_g, w_ple_gate, b_ple_gate, w_ple, ple_norm_g, final_norm_g, loss_target, m_mix_norm_g, m_w_in, m_ssd_conv_w, m_ssd_conv_b, m_dt_bias, m_A_log, m_D_skip, m_ssd_norm_g, m_conf_dw_w, m_conf_dw_b, m_conf_ln_g, m_conf_ln_b, m_w_out, m_mlp_norm_g, m_w_up, m_w_down, m_ple_gate_norm_g, m_w_ple_gate, m_b_ple_gate, m_w_ple, m_ple_norm_g, m_final_norm_g, v_mix_norm_g, v_w_in, v_ssd_conv_w, v_ssd_conv_b, v_dt_bias, v_A_log, v_D_skip, v_ssd_norm_g, v_conf_dw_w, v_conf_dw_b, v_conf_ln_g, v_conf_ln_b, v_w_out, v_mlp_norm_g, v_w_up, v_w_down, v_ple_gate_norm_g, v_w_ple_gate, v_b_ple_gate, v_w_ple, v_ple_norm_g, v_final_norm_g):
    wts = dict(mix_norm_g=mix_norm_g, w_in=w_in, ssd_conv_w=ssd_conv_w, ssd_conv_b=ssd_conv_b, dt_bias=dt_bias, A_log=A_log,
               D_skip=D_skip, ssd_norm_g=ssd_norm_g, conf_dw_w=conf_dw_w, conf_dw_b=conf_dw_b, conf_ln_g=conf_ln_g,
               conf_ln_b=conf_ln_b, w_out=w_out, mlp_norm_g=mlp_norm_g, w_up=w_up, w_down=w_down,
               ple_gate_norm_g=ple_gate_norm_g, w_ple_gate=w_ple_gate, b_ple_gate=b_ple_gate, w_ple=w_ple,
               ple_norm_g=ple_norm_g, final_norm_g=final_norm_g)
    mom1 = dict(mix_norm_g=m_mix_norm_g, w_in=m_w_in, ssd_conv_w=m_ssd_conv_w, ssd_conv_b=m_ssd_conv_b, dt_bias=m_dt_bias,
                A_log=m_A_log, D_skip=m_D_skip, ssd_norm_g=m_ssd_norm_g, conf_dw_w=m_conf_dw_w, conf_dw_b=m_conf_dw_b,
                conf_ln_g=m_conf_ln_g, conf_ln_b=m_conf_ln_b, w_out=m_w_out, mlp_norm_g=m_mlp_norm_g, w_up=m_w_up,
                w_down=m_w_down, ple_gate_norm_g=m_ple_gate_norm_g, w_ple_gate=m_w_ple_gate, b_ple_gate=m_b_ple_gate,
                w_ple=m_w_ple, ple_norm_g=m_ple_norm_g, final_norm_g=m_final_norm_g)
    mom2 = dict(mix_norm_g=v_mix_norm_g, w_in=v_w_in, ssd_conv_w=v_ssd_conv_w, ssd_conv_b=v_ssd_conv_b, dt_bias=v_dt_bias,
                A_log=v_A_log, D_skip=v_D_skip, ssd_norm_g=v_ssd_norm_g, conf_dw_w=v_conf_dw_w, conf_dw_b=v_conf_dw_b,
                conf_ln_g=v_conf_ln_g, conf_ln_b=v_conf_ln_b, w_out=v_w_out, mlp_norm_g=v_mlp_norm_g, w_up=v_w_up,
                w_down=v_w_down, ple_gate_norm_g=v_ple_gate_norm_g, w_ple_gate=v_w_ple_gate, b_ple_gate=v_b_ple_gate,
                w_ple=v_w_ple, ple_norm_g=v_ple_norm_g, final_norm_g=v_final_norm_g)
    x_pos, y_pos, c_pos = _mesh_pos()
    me = 4 * x_pos + 2 * y_pos + c_pos

    first = _all_gather([wts["w_in"][0].astype(bf16), wts["ssd_conv_w"][0], wts["conf_dw_w"][0]], "gather_first")
    ready = {
        "w_in": _perm_w_in(_columns_of_blocks(first[0])),
        "ssd_conv_w": jnp.pad(_columns_of_blocks(first[1]), ((0, 8 - SSD_CONV), (0, 0))),
        "conf_dw_w": jnp.pad(_columns_of_blocks(first[2]), ((0, 32 - CONF_KERNEL), (0, 0))),
    }
    shards = [wts[n][0].astype(bf16) for n in _LATER]
    states, token = _exchange_start(shards, [_with_own_block(sh, me) for sh in shards], False, "gather_start", first[1])
    comm = _StepComm(me, ready, dict(zip(_LATER, states)), token)
    small = {n: wts[n].reshape(1, -1) for n in _REPLICATED}

    grad_x, acc = _local_step(x[0], p[0, 0], loss_target[0], small, comm)

    packed = _pack_small(acc)
    (small_state,), _ = _exchange_start([packed], [_with_own_block(packed, me)], False, "small_start", comm.token)
    grads, delta, new_m, new_v = {}, {}, {}, {}

    def adamw_big(names, sent, after):
        lands = _exchange_wait(sent, True, "scatter_wait_" + names[0], after)
        for n, land in zip(names, lands):
            shp = wts[n].shape
            g, d, mn, vn = _adamw_big(land, wts[n][0], mom1[n][0], mom2[n][0], "adamw_" + n)
            grads[n], delta[n], new_m[n], new_v[n] = g.reshape(shp), d.reshape(shp), mn.reshape(shp), vn.reshape(shp)

    for names, sent in comm.sent[:-1]:
        adamw_big(names, sent, [grad_x])
    (all_small,) = _exchange_wait([small_state], False, "small_wait", [delta[n] for n in _LATER])
    as_row = lambda d: {n: d[n].reshape(1, -1) for n in _REPLICATED}
    loss, per_name, conv_w_sum, conf_w_sum = _small_update(all_small, as_row(wts), as_row(mom1), as_row(mom2))
    for n in _REPLICATED:
        grads[n], delta[n], new_m[n], new_v[n] = [a.reshape(wts[n].shape) for a in per_name[n]]
    filt_g = [lax.dynamic_slice_in_dim(conv_w_sum, me * 192, 192, axis=1),
              lax.dynamic_slice_in_dim(conf_w_sum, me * 128, 128, axis=1)]
    filt = _adamw_filters(filt_g, [wts[n][0] for n in _CONV_WEIGHTS], [mom1[n][0] for n in _CONV_WEIGHTS],
                          [mom2[n][0] for n in _CONV_WEIGHTS])
    for n, g, (d, mn, vn) in zip(_CONV_WEIGHTS, filt_g, filt):
        grads[n], delta[n], new_m[n], new_v[n] = g[None], d[None], mn[None], vn[None]
    adamw_big(*comm.sent[-1], [delta[n] for n in _LATER])

    return (loss.reshape(()), grad_x[None], *[grads[n] for n in _WEIGHT_ORDER], *[delta[n] for n in _WEIGHT_ORDER],
            *[new_m[n] for n in _WEIGHT_ORDER], *[new_v[n] for n in _WEIGHT_ORDER])
```

```python
import functools

import jax
import jax.numpy as jnp
from jax import lax
from jax.experimental import pallas as pl
from jax.experimental.pallas import tpu as pltpu

f32 = jnp.float32
bf16 = jnp.bfloat16

EPS = 1e-6
D_MODEL = 1024
SSD_HEADS = 16
SSD_HEAD_DIM = 64
SSD_GROUPS = 2
SSD_STATE = 128
SSD_CONV = 4
CHUNK = 128
XBC_WIDTH = 1536
CONF_KERNEL = 31
D_FF = 4096
PLE_DIM = 256
IN_WIDTH = 4624
N_DEV = 8
SEG_Z = (0, 1024)
SEG_XBC = (1024, 2560)
SEG_DT = (2560, 2576)
SEG_CV = (2576, 3600)
SEG_CG = (3600, 4624)
IN_SEGS = (SEG_Z, SEG_XBC, SEG_DT, SEG_CV, SEG_CG)

ADAM_LR = 0.001
ADAM_B1 = 0.9
ADAM_B2 = 0.999
ADAM_EPS = 1e-08
ADAM_WD = 0.01
ADAM_STEP = 10

VMEM_LIMIT_BYTES = 56 * 1024 * 1024
TOKEN_TILE = 256
SMALL_ROWS = 16

HBM_SPEC = pl.BlockSpec(memory_space=pltpu.HBM)
MESH_ID = pl.DeviceIdType.MESH


def _params(*semantics):
    return pltpu.CompilerParams(dimension_semantics=semantics, vmem_limit_bytes=VMEM_LIMIT_BYTES)


def _nn(a, b):
    return lax.dot_general(a, b, (((1,), (0,)), ((), ())), preferred_element_type=f32)


def _nt(a, b):
    return lax.dot_general(a, b, (((1,), (1,)), ((), ())), preferred_element_type=f32)


def _tn(a, b):
    return lax.dot_general(a, b, (((0,), (0,)), ((), ())), preferred_element_type=f32)


@jax.custom_vjp
def bnn(a, b):
    return _nn(a.astype(bf16), b.astype(bf16))


def _bnn_fwd(a, b):
    ab, bb = a.astype(bf16), b.astype(bf16)
    return _nn(ab, bb), (ab, bb)


def _bnn_bwd(res, g):
    ab, bb = res
    gb = g.astype(bf16)
    return _nt(gb, bb), _tn(ab, gb)


bnn.defvjp(_bnn_fwd, _bnn_bwd)


@jax.custom_vjp
def bnt(a, b):
    return _nt(a.astype(bf16), b.astype(bf16))


def _bnt_fwd(a, b):
    ab, bb = a.astype(bf16), b.astype(bf16)
    return _nt(ab, bb), (ab, bb)


def _bnt_bwd(res, g):
    ab, bb = res
    gb = g.astype(bf16)
    return _nn(gb, bb), _tn(gb, ab)


bnt.defvjp(_bnt_fwd, _bnt_bwd)


@jax.custom_vjp
def btn(a, b):
    return _tn(a.astype(bf16), b.astype(bf16))


def _btn_fwd(a, b):
    ab, bb = a.astype(bf16), b.astype(bf16)
    return _tn(ab, bb), (ab, bb)


def _btn_bwd(res, g):
    ab, bb = res
    gb = g.astype(bf16)
    return _nt(bb, gb), _nn(ab, gb)


btn.defvjp(_btn_fwd, _btn_bwd)

CUMSUM_PASSES = 3
EXPAND_PASSES = 2


def _bf16_pieces(a, passes):
    pieces, rest = [], a
    for _ in range(passes):
        piece = rest.astype(bf16)
        pieces.append(piece)
        rest = rest - piece.astype(f32)
    return pieces


def _make_dot01(form, passes):
    fwd, bwd = {"right": (lambda b, p: _nn(p, b), lambda b, g: _nt(g, b)),
                "left": (lambda b, p: _nn(b, p), lambda b, g: _tn(b, g)),
                "tn": (lambda b, p: _tn(p, b), lambda b, g: _nt(b, g))}[form]

    def run(op, b, v):
        out = None
        for piece in _bf16_pieces(v, passes):
            term = op(b, piece)
            out = term if out is None else out + term
        return out

    @jax.custom_vjp
    def product(b, v):
        return run(fwd, b, v)

    product.defvjp(lambda b, v: (run(fwd, b, v), b), lambda b, g: (jnp.zeros_like(b), run(bwd, b, g)))
    return product


_DOT01 = {(form, passes): _make_dot01(form, passes)
          for form, passes in (("left", CUMSUM_PASSES), ("tn", CUMSUM_PASSES), ("right", EXPAND_PASSES))}


def _dot01(b01, a, form, passes):
    return _DOT01[form, passes](b01.astype(bf16), a)


def _rms(x, g):
    return x * lax.rsqrt(jnp.mean(x * x, axis=-1, keepdims=True) + EPS) * g


def _gated_norm(y, z, g):
    v = y * jax.nn.silu(z)
    half = v.shape[-1] // SSD_GROUPS
    parts = []
    for k in range(SSD_GROUPS):
        vk = v[:, k * half:(k + 1) * half]
        parts.append(vk * lax.rsqrt(jnp.mean(vk * vk, axis=-1, keepdims=True) + EPS))
    return jnp.concatenate(parts, axis=-1) * g


def _ln_silu(v, g, b):
    mu = jnp.mean(v, axis=-1, keepdims=True)
    xc = v - mu
    y = xc * lax.rsqrt(jnp.mean(xc * xc, axis=-1, keepdims=True) + EPS) * g + b
    return jax.nn.silu(y)


def _acc_init(i, *refs):
    @pl.when(i == 0)
    def _():
        for r in refs:
            r[...] = jnp.zeros_like(r)


def _after_token(body, token):
    if token is None:
        return body, [], []

    def body_after(tok_ref, *refs):
        del tok_ref
        body(*refs)

    return body_after, [token], [pl.BlockSpec(memory_space=pl.ANY)]


def _row_spec(tm, n):
    return pl.BlockSpec((tm, n), lambda i: (i, 0))


def _const_spec(shape):
    nd = len(shape)
    return pl.BlockSpec(shape, lambda i: (0,) * nd)


def _prev_halo_spec(tm, halo, n):
    return pl.BlockSpec((halo, n), lambda i: (jnp.maximum(i * (tm // halo) - 1, 0), 0))


def _next_halo_spec(tm, halo, n, t):
    return pl.BlockSpec((halo, n), lambda i: (jnp.minimum((i + 1) * (tm // halo), t // halo - 1), 0))


def _in_proj(x, g1, w_in_p, token=None):
    t = x.shape[0]
    tm = min(TOKEN_TILE, t)

    def body(x_ref, g_ref, w_ref, u_ref, z_ref, xbc_ref, dt_ref, cv_ref, cg_ref):
        u = _rms(x_ref[...], g_ref[...]).astype(bf16)
        u_ref[...] = u
        for (lo, hi), o_ref in zip(IN_SEGS, (z_ref, xbc_ref, dt_ref, cv_ref, cg_ref)):
            o_ref[...] = _nt(u, w_ref[lo:hi, :])

    widths = [hi - lo for lo, hi in IN_SEGS]
    outs = [jax.ShapeDtypeStruct((t, D_MODEL), bf16)] + [jax.ShapeDtypeStruct((t, n), f32) for n in widths]
    body, tok, tok_spec = _after_token(body, token)
    return pl.pallas_call(
        body, name="in_proj", grid=(t // tm,),
        in_specs=tok_spec + [_row_spec(tm, D_MODEL), _const_spec((1, D_MODEL)), _const_spec((IN_WIDTH, D_MODEL))],
        out_specs=[_row_spec(tm, D_MODEL)] + [_row_spec(tm, n) for n in widths],
        out_shape=outs, compiler_params=_params("arbitrary"),
    )(*tok, x, g1, w_in_p)


def _conv4_tile(ext_ref, w_ref, b_ref, tm):
    acc = b_ref[...] + w_ref[0:1, :] * ext_ref[pl.ds(8 - 3, tm), :]
    for k in range(1, SSD_CONV):
        acc = acc + w_ref[k:k + 1, :] * ext_ref[pl.ds(8 - 3 + k, tm), :]
    return acc


def _ssd_pre(xbc_raw, dt_raw, cw, cb, dt_bias):
    t = xbc_raw.shape[0]
    tm = min(TOKEN_TILE, t)

    def body(cur_ref, halo_ref, dt_ref, w_ref, b_ref, dtb_ref, act_ref, dto_ref, ext_ref):
        i = pl.program_id(0)
        ext_ref[0:8, :] = jnp.where(i == 0, 0.0, halo_ref[...])
        ext_ref[8:, :] = cur_ref[...]
        act_ref[...] = jax.nn.silu(_conv4_tile(ext_ref, w_ref, b_ref, tm))
        dto_ref[...] = jax.nn.softplus(dt_ref[...] + dtb_ref[...])

    return pl.pallas_call(
        body, name="ssd_pre", grid=(t // tm,),
        in_specs=[_row_spec(tm, XBC_WIDTH), _prev_halo_spec(tm, 8, XBC_WIDTH), _row_spec(tm, SSD_HEADS),
                  _const_spec((8, XBC_WIDTH)), _const_spec((1, XBC_WIDTH)), _const_spec((1, SSD_HEADS))],
        out_specs=[_row_spec(tm, XBC_WIDTH), _row_spec(tm, SSD_HEADS)],
        out_shape=(jax.ShapeDtypeStruct((t, XBC_WIDTH), f32), jax.ShapeDtypeStruct((t, SSD_HEADS), f32)),
        scratch_shapes=[pltpu.VMEM((tm + 8, XBC_WIDTH), f32)],
        compiler_params=_params("arbitrary"),
    )(xbc_raw, xbc_raw, dt_raw, cw, cb, dt_bias)


def _ssd_consts():
    r = lax.broadcasted_iota(jnp.int32, (CHUNK, CHUNK), 0)
    c = lax.broadcasted_iota(jnp.int32, (CHUNK, CHUNK), 1)
    causal = r >= c
    tril = causal.astype(f32)
    triu = (r <= c).astype(f32)
    hh = lax.broadcasted_iota(jnp.int32, (SSD_HEADS, D_MODEL), 0)
    jj = lax.broadcasted_iota(jnp.int32, (SSD_HEADS, D_MODEL), 1)
    expand = (lax.shift_right_logical(jj, 6) == hh).astype(f32)
    lane = lax.broadcasted_iota(jnp.int32, (1, 2 * SSD_HEAD_DIM), 1)
    m0 = (lane < SSD_HEAD_DIM).astype(f32)
    return causal, tril, triu, expand, m0, 1.0 - m0


def _ssd_chunk(xs, bm, cm, dt, s_prev, a_log, d_skip, consts):
    causal, tril, triu, expand, m0, m1 = consts
    a = dt * (-jnp.exp(a_log))
    cs = _dot01(tril, a, "left", CUMSUM_PASSES)
    cs_t = _dot01(triu, a, "tn", CUMSUM_PASSES)
    cs_last = cs[CHUNK - 1:CHUNK, :]
    per_head = jnp.concatenate([dt, jnp.exp(cs_last - cs), jnp.exp(cs), jnp.broadcast_to(jnp.exp(cs_last), (8, SSD_HEADS)),
                                jnp.broadcast_to(d_skip, (8, SSD_HEADS))], axis=0)
    per_channel = _dot01(expand, per_head, "right", EXPAND_PASSES)
    dt_e, dec_end, dec_start = (per_channel[i * CHUNK:(i + 1) * CHUNK] for i in range(3))
    chunk_dec = per_channel[3 * CHUNK:3 * CHUNK + 1]
    d_e = per_channel[3 * CHUNK + 8:3 * CHUNK + 9]
    xc = xs * dt_e
    x_dec = xc * dec_end
    gw = D_MODEL // SSD_GROUPS
    ys, states = [], []
    for g in range(SSD_GROUPS):
        bg = bm[:, g * SSD_STATE:(g + 1) * SSD_STATE]
        cg = cm[:, g * SSD_STATE:(g + 1) * SSD_STATE]
        sp = s_prev[:, g * gw:(g + 1) * gw]
        states.append(sp * chunk_dec[:, g * gw:(g + 1) * gw] + btn(bg, x_dec[:, g * gw:(g + 1) * gw]))
        y_off = bnn(cg, sp) * dec_start[:, g * gw:(g + 1) * gw]
        scores = bnt(cg, bg)
        pieces = []
        for pr in range(gw // (2 * SSD_HEAD_DIM)):
            lo = g * gw + pr * 2 * SSD_HEAD_DIM
            xp = xc[:, lo:lo + 2 * SSD_HEAD_DIM]
            acc = None
            for h, mk in ((lo // SSD_HEAD_DIM, m0), (lo // SSD_HEAD_DIM + 1, m1)):
                seg = cs[:, h:h + 1] - cs_t[h:h + 1, :]
                mat = scores * jnp.exp(jnp.where(causal, seg, -jnp.inf))
                term = bnn(mat, xp * mk)
                acc = term if acc is None else acc + term
            pieces.append(acc)
        ys.append(jnp.concatenate(pieces, axis=-1) + y_off)
    y = jnp.concatenate(ys, axis=-1) + xs * d_e
    return y, jnp.concatenate(states, axis=-1)


def _ssd_scan(xbc_act, dt, a_log, d_skip):
    t = xbc_act.shape[0]
    nc = t // CHUNK

    def body(xbc_ref, dt_ref, al_ref, dk_ref, y_ref, sp_ref, state_ref):
        c = pl.program_id(0)

        @pl.when(c == 0)
        def _():
            state_ref[...] = jnp.zeros_like(state_ref)

        s_prev = state_ref[...]
        sp_ref[...] = s_prev
        y, s_next = _ssd_chunk(xbc_ref[:, 0:1024], xbc_ref[:, 1024:1280], xbc_ref[:, 1280:1536], dt_ref[...],
                               s_prev, al_ref[...], dk_ref[...], _ssd_consts())
        y_ref[...] = y
        state_ref[...] = s_next

    return pl.pallas_call(
        body, name="ssd_scan", grid=(nc,),
        in_specs=[_row_spec(CHUNK, XBC_WIDTH), _row_spec(CHUNK, SSD_HEADS), _const_spec((1, SSD_HEADS)),
                  _const_spec((1, SSD_HEADS))],
        out_specs=[_row_spec(CHUNK, D_MODEL), pl.BlockSpec((None, SSD_STATE, D_MODEL), lambda c: (c, 0, 0))],
        out_shape=(jax.ShapeDtypeStruct((t, D_MODEL), f32), jax.ShapeDtypeStruct((nc, SSD_STATE, D_MODEL), f32)),
        scratch_shapes=[pltpu.VMEM((SSD_STATE, D_MODEL), f32)],
        compiler_params=_params("arbitrary"),
    )(xbc_act, dt, a_log, d_skip)


CONV_HALO = 32
CONV_BLOCK = 32
LANE_TILE = 128
SUBLANES = 8


def _fill_glu_ext(ext_ref, sg_ref, cv_ref, cg_ref, hcv_ref, hcg_ref, i):
    ext_ref[0:CONV_HALO, :] = jnp.where(i == 0, 0.0, hcv_ref[...] * jax.nn.sigmoid(hcg_ref[...]))
    sg = jax.nn.sigmoid(cg_ref[...])
    if sg_ref is not None:
        sg_ref[...] = sg
    ext_ref[CONV_HALO:, :] = cv_ref[...] * sg


def _shifted_copies(src_ref, dst_ref, rows):
    for s in range(1, SUBLANES):
        dst_ref[s, 0:rows, :] = src_ref[pl.ds(s, rows), :]


def _tap_source(src_ref, shifted_ref, offset):
    s = offset % SUBLANES
    return (src_ref if s == 0 else shifted_ref.at[s]), offset - s


def _conv_rows(src_ref, shifted_ref, taps, offsets, lanes, r0, init):
    accs = [init] * (CONV_BLOCK // SUBLANES)
    for k, off in enumerate(offsets):
        ref, base = _tap_source(src_ref, shifted_ref, off)
        for j in range(len(accs)):
            accs[j] = accs[j] + taps[k] * ref[pl.ds(r0 + base + SUBLANES * j, SUBLANES), lanes]
    return accs


def _conf_conv(cv, cg, w, b):
    t = cv.shape[0]
    tm = min(TOKEN_TILE, t)
    ext_rows = tm + CONV_HALO
    offsets = [CONV_HALO - (CONF_KERNEL - 1) + k for k in range(CONF_KERNEL)]

    def body(cv_ref, cg_ref, hcv_ref, hcg_ref, w_ref, b_ref, o_ref, ext_ref, shifted_ref):
        _fill_glu_ext(ext_ref, None, cv_ref, cg_ref, hcv_ref, hcg_ref, pl.program_id(0))
        _shifted_copies(ext_ref, shifted_ref, ext_rows - SUBLANES)
        for lb in range(1024 // LANE_TILE):
            lanes = slice(lb * LANE_TILE, (lb + 1) * LANE_TILE)
            taps = [jnp.broadcast_to(w_ref[k:k + 1, lanes], (SUBLANES, LANE_TILE)) for k in range(CONF_KERNEL)]
            bias = jnp.broadcast_to(b_ref[:, lanes], (SUBLANES, LANE_TILE))

            def rows(rb, carry, lanes=lanes, taps=taps, bias=bias):
                r0 = pl.multiple_of(rb * CONV_BLOCK, CONV_BLOCK)
                accs = _conv_rows(ext_ref, shifted_ref, taps, offsets, lanes, r0, bias)
                for j, a in enumerate(accs):
                    o_ref[pl.ds(r0 + SUBLANES * j, SUBLANES), lanes] = a
                return carry

            lax.fori_loop(0, tm // CONV_BLOCK, rows, 0)

    return pl.pallas_call(
        body, name="conf_conv", grid=(t // tm,),
        in_specs=[_row_spec(tm, 1024), _row_spec(tm, 1024), _prev_halo_spec(tm, CONV_HALO, 1024),
                  _prev_halo_spec(tm, CONV_HALO, 1024), _const_spec((32, 1024)), _const_spec((1, 1024))],
        out_specs=_row_spec(tm, 1024), out_shape=jax.ShapeDtypeStruct((t, 1024), f32),
        scratch_shapes=[pltpu.VMEM((ext_rows, 1024), f32), pltpu.VMEM((SUBLANES, ext_rows, 1024), f32)],
        compiler_params=_params("arbitrary"),
    )(cv, cg, cv, cg, w, b)


def _out_proj(y, z, v2, x, g_ssd, ln_g, ln_b, w_out, g2):
    t = x.shape[0]
    tm = min(TOKEN_TILE, t)

    def body(y_ref, z_ref, v_ref, x_ref, gs_ref, lg_ref, lb_ref, w_ref, g2_ref, ys_ref, yc_ref, h1_ref, u2_ref):
        ys = _gated_norm(y_ref[...], z_ref[...], gs_ref[...]).astype(bf16)
        yc = _ln_silu(v_ref[...], lg_ref[...], lb_ref[...]).astype(bf16)
        ys_ref[...] = ys
        yc_ref[...] = yc
        h1 = x_ref[...] + _nn(ys, w_ref[0:1024, :]) + _nn(yc, w_ref[1024:2048, :])
        h1_ref[...] = h1
        u2_ref[...] = _rms(h1, g2_ref[...]).astype(bf16)

    vec = _const_spec((1, 1024))
    row = _row_spec(tm, 1024)
    return pl.pallas_call(
        body, name="out_proj", grid=(t // tm,),
        in_specs=[row, row, row, row, vec, vec, vec, _const_spec((2048, 1024)), vec],
        out_specs=[row, row, row, row],
        out_shape=(jax.ShapeDtypeStruct((t, 1024), bf16), jax.ShapeDtypeStruct((t, 1024), bf16),
                   jax.ShapeDtypeStruct((t, 1024), f32), jax.ShapeDtypeStruct((t, 1024), bf16)),
        compiler_params=_params("arbitrary"),
    )(y, z, v2, x, g_ssd, ln_g, ln_b, w_out, g2)


def _mlp_up(u2, w_up):
    t = u2.shape[0]
    tm = min(TOKEN_TILE, t)
    blk = D_FF // N_DEV

    def body(u_ref, w_ref, pre_ref, hs_ref):
        u = u_ref[...]
        for k in range(N_DEV):
            pre = _nn(u, w_ref[k])
            pre_ref[:, k * blk:(k + 1) * blk] = pre.astype(bf16)
            r = jnp.maximum(pre, 0.0)
            hs_ref[:, k * blk:(k + 1) * blk] = (r * r).astype(bf16)

    return pl.pallas_call(
        body, name="mlp_up", grid=(t // tm,),
        in_specs=[_row_spec(tm, D_MODEL), _const_spec((N_DEV, D_MODEL, D_FF // N_DEV))],
        out_specs=[_row_spec(tm, D_FF), _row_spec(tm, D_FF)],
        out_shape=(jax.ShapeDtypeStruct((t, D_FF), bf16), jax.ShapeDtypeStruct((t, D_FF), bf16)),
        compiler_params=_params("arbitrary"),
    )(u2, w_up)


def _mlp_down(h1, hs, w_down):
    t = h1.shape[0]
    tm = min(TOKEN_TILE, t)

    def body(h_ref, hs_ref, w_ref, o_ref):
        o_ref[...] = h_ref[...] + _nn(hs_ref[...], w_ref[...])

    return pl.pallas_call(
        body, name="mlp_down", grid=(t // tm,),
        in_specs=[_row_spec(tm, D_MODEL), _row_spec(tm, D_FF), _const_spec((D_FF, D_MODEL))],
        out_specs=_row_spec(tm, D_MODEL), out_shape=jax.ShapeDtypeStruct((t, D_MODEL), f32),
        compiler_params=_params("arbitrary"),
    )(h1, hs, w_down)


TAIL_LOSS, TAIL_FINAL_G, TAIL_PLE_G, TAIL_GATE_B, TAIL_GATE_NORM_G = 0, 1, 2, 3, 4


def _tail(h2, p, target, g3, w_gate, b_gate, w_ple, ple_g, fin_g):
    t = h2.shape[0]
    tm = min(TOKEN_TILE, t)

    def body(h_ref, p_ref, t_ref, g3_ref, wg_ref, bg_ref, wp_ref, pg_ref, fg_ref,
             dh_ref, dhb_ref, dwg_ref, dwp_ref, acc_ref):
        i = pl.program_id(0)
        _acc_init(i, dwg_ref, dwp_ref, acc_ref)
        h2v = h_ref[...]
        tgt = t_ref[...]
        u3, vjp_u3 = jax.vjp(_rms, h2v, g3_ref[...])
        u3b = u3.astype(bf16)
        pb = p_ref[...].astype(bf16)
        gate_pre = _nn(u3b, wg_ref[...]) + bg_ref[...]
        emb_pre = _nn(pb, wp_ref[...])

        def tail_fn(hv, gp, ep, pg, fg):
            h3 = hv + jax.nn.sigmoid(gp) * _rms(ep, pg)
            err = _rms(h3, fg) - tgt
            return 0.5 * jnp.mean(err * err, axis=-1, keepdims=True)

        loss_tok, vjp_tail = jax.vjp(tail_fn, h2v, gate_pre, emb_pre, pg_ref[...], fg_ref[...])
        dh_a, dgp, dep, dpg, dfg = vjp_tail(jnp.ones_like(loss_tok))
        dgpb = dgp.astype(bf16)
        dh_b, dg3 = vjp_u3(_nt(dgpb, wg_ref[...]))
        dh = dh_a + dh_b
        dh_ref[...] = dh
        dhb_ref[...] = dh.astype(bf16)
        dwg_ref[...] += _tn(u3b, dgpb)
        dwp_ref[...] += _tn(pb, dep.astype(bf16))
        acc_ref[TAIL_LOSS:TAIL_LOSS + 1, :] += jnp.broadcast_to(jnp.sum(loss_tok, axis=0, keepdims=True), (1, 1024))
        acc_ref[TAIL_FINAL_G:TAIL_FINAL_G + 1, :] += dfg
        acc_ref[TAIL_PLE_G:TAIL_PLE_G + 1, :] += dpg
        acc_ref[TAIL_GATE_B:TAIL_GATE_B + 1, :] += jnp.sum(dgp, axis=0, keepdims=True)
        acc_ref[TAIL_GATE_NORM_G:TAIL_GATE_NORM_G + 1, :] += dg3

    vec = _const_spec((1, 1024))
    row = _row_spec(tm, 1024)
    return pl.pallas_call(
        body, name="tail", grid=(t // tm,),
        in_specs=[row, _row_spec(tm, PLE_DIM), row, vec, _const_spec((1024, 1024)), vec, _const_spec((PLE_DIM, 1024)), vec, vec],
        out_specs=[row, row, _const_spec((1024, 1024)), _const_spec((PLE_DIM, 1024)), _const_spec((8, 1024))],
        out_shape=(jax.ShapeDtypeStruct((t, 1024), f32), jax.ShapeDtypeStruct((t, 1024), bf16),
                   jax.ShapeDtypeStruct((1024, 1024), f32), jax.ShapeDtypeStruct((PLE_DIM, 1024), f32),
                   jax.ShapeDtypeStruct((8, 1024), f32)),
        compiler_params=_params("arbitrary"),
    )(h2, p, target, g3, w_gate, b_gate, w_ple, ple_g, fin_g)


def _mlp_bwd(dh2, dh2b, pre, h1, g2, w_down, w_up, token=None):
    t = dh2.shape[0]
    tm = min(TOKEN_TILE, t)
    blk = D_FF // N_DEV

    def body(dh_ref, dhb_ref, pre_ref, h1_ref, g_ref, wd_hbm, wu_hbm, dpre_ref, dh1_ref, dh1b_ref, acc_ref,
             wd_ref, wu_ref):
        i = pl.program_id(0)
        _acc_init(i, acc_ref)

        @pl.when(i == 0)
        def _():
            pltpu.sync_copy(wd_hbm, wd_ref)
            pltpu.sync_copy(wu_hbm, wu_ref)

        dhb = dhb_ref[...]
        du2 = jnp.zeros((tm, D_MODEL), f32)
        for k in range(D_FF // blk):
            dhs = _nt(dhb, wd_ref[k * blk:(k + 1) * blk, :])
            dpre = (dhs * (2.0 * jnp.maximum(pre_ref[:, k * blk:(k + 1) * blk].astype(f32), 0.0))).astype(bf16)
            dpre_ref[:, k * blk:(k + 1) * blk] = dpre
            du2 = du2 + _nt(dpre, wu_ref[k])
        _, vjp_u2 = jax.vjp(_rms, h1_ref[...], g_ref[...])
        d, dg = vjp_u2(du2)
        dh1 = dh_ref[...] + d
        dh1_ref[...] = dh1
        dh1b_ref[...] = dh1.astype(bf16)
        acc_ref[0:1, :] += dg

    row = _row_spec(tm, 1024)
    body, tok, tok_spec = _after_token(body, token)
    return pl.pallas_call(
        body, name="mlp_bwd", grid=(t // tm,),
        in_specs=tok_spec + [row, row, _row_spec(tm, D_FF), row, _const_spec((1, 1024)), HBM_SPEC, HBM_SPEC],
        out_specs=[_row_spec(tm, D_FF), row, row, _const_spec((8, 1024))],
        out_shape=(jax.ShapeDtypeStruct((t, D_FF), bf16), jax.ShapeDtypeStruct((t, 1024), f32),
                   jax.ShapeDtypeStruct((t, 1024), bf16), jax.ShapeDtypeStruct((8, 1024), f32)),
        scratch_shapes=[pltpu.VMEM((D_FF, D_MODEL), bf16), pltpu.VMEM((N_DEV, D_MODEL, D_FF // N_DEV), bf16)],
        compiler_params=_params("arbitrary"),
    )(*tok, dh2, dh2b, pre, h1, g2, w_down, w_up)


OPB_SSD_G, OPB_LN_G, OPB_LN_B, OPB_CONV_B = 0, 1, 2, 3


def _out_proj_bwd(dh1b, y, z, v2, g_ssd, ln_g, ln_b, w_out, token=None):
    t = y.shape[0]
    tm = min(TOKEN_TILE, t)

    def body(dh_ref, y_ref, z_ref, v_ref, gs_ref, lg_ref, lb_ref, w_ref, dy_ref, dz_ref, dv_ref, acc_ref):
        i = pl.program_id(0)
        _acc_init(i, acc_ref)
        dhb = dh_ref[...]
        dys = _nt(dhb, w_ref[0:1024, :])
        dyc = _nt(dhb, w_ref[1024:2048, :])
        _, vjp_g = jax.vjp(_gated_norm, y_ref[...], z_ref[...], gs_ref[...])
        dy, dz, dgs = vjp_g(dys)
        _, vjp_l = jax.vjp(_ln_silu, v_ref[...], lg_ref[...], lb_ref[...])
        dv, dlg, dlb = vjp_l(dyc)
        dy_ref[...] = dy
        dz_ref[...] = dz.astype(bf16)
        dv_ref[...] = dv
        acc_ref[OPB_SSD_G:OPB_SSD_G + 1, :] += dgs
        acc_ref[OPB_LN_G:OPB_LN_G + 1, :] += dlg
        acc_ref[OPB_LN_B:OPB_LN_B + 1, :] += dlb
        acc_ref[OPB_CONV_B:OPB_CONV_B + 1, :] += jnp.sum(dv, axis=0, keepdims=True)

    vec = _const_spec((1, 1024))
    row = _row_spec(tm, 1024)
    body, tok, tok_spec = _after_token(body, token)
    return pl.pallas_call(
        body, name="out_proj_bwd", grid=(t // tm,),
        in_specs=tok_spec + [row, row, row, row, vec, vec, vec, _const_spec((2048, 1024))],
        out_specs=[row, row, row, _const_spec((8, 1024))],
        out_shape=(jax.ShapeDtypeStruct((t, 1024), f32), jax.ShapeDtypeStruct((t, 1024), bf16),
                   jax.ShapeDtypeStruct((t, 1024), f32), jax.ShapeDtypeStruct((8, 1024), f32)),
        compiler_params=_params("arbitrary"),
    )(*tok, dh1b, y, z, v2, g_ssd, ln_g, ln_b, w_out)


def _conf_conv_bwd(dv2, cv, cg, w):
    t = cv.shape[0]
    tm = min(TOKEN_TILE, t)
    ext_rows = tm + CONV_HALO
    fwd_offsets = [CONV_HALO - (CONF_KERNEL - 1) + k for k in range(CONF_KERNEL)]
    bwd_offsets = [CONF_KERNEL - 1 - k for k in range(CONF_KERNEL)]

    def body(dv_ref, dvn_ref, cv_ref, cg_ref, hcv_ref, hcg_ref, w_ref, dcv_ref, dcg_ref, dw_ref,
             ext_ref, dext_ref, sg_ref, shifted_ref):
        i = pl.program_id(0)
        _acc_init(i, dw_ref)
        _fill_glu_ext(ext_ref, sg_ref, cv_ref, cg_ref, hcv_ref, hcg_ref, i)
        dext_ref[0:tm, :] = dv_ref[...]
        dext_ref[tm:, :] = jnp.where(i == pl.num_programs(0) - 1, 0.0, dvn_ref[...])

        _shifted_copies(ext_ref, shifted_ref, ext_rows - SUBLANES)
        for lb in range(1024 // LANE_TILE):
            lanes = slice(lb * LANE_TILE, (lb + 1) * LANE_TILE)

            def rows_w(rb, accs, lanes=lanes):
                r0 = pl.multiple_of(rb * CONV_BLOCK, CONV_BLOCK)
                accs = list(accs)
                for j in range(CONV_BLOCK // SUBLANES):
                    d = dext_ref[pl.ds(r0 + SUBLANES * j, SUBLANES), lanes]
                    for k, off in enumerate(fwd_offsets):
                        ref, base = _tap_source(ext_ref, shifted_ref, off)
                        accs[k] = accs[k] + d * ref[pl.ds(r0 + base + SUBLANES * j, SUBLANES), lanes]
                return tuple(accs)

            zero = jnp.zeros((SUBLANES, LANE_TILE), f32)
            accs = lax.fori_loop(0, tm // CONV_BLOCK, rows_w, (zero,) * CONF_KERNEL)
            for k in range(CONF_KERNEL):
                dw_ref[k:k + 1, lanes] += jnp.sum(accs[k], axis=0, keepdims=True)

        _shifted_copies(dext_ref, shifted_ref, ext_rows - SUBLANES)
        for lb in range(1024 // LANE_TILE):
            lanes = slice(lb * LANE_TILE, (lb + 1) * LANE_TILE)
            taps = [jnp.broadcast_to(w_ref[k:k + 1, lanes], (SUBLANES, LANE_TILE)) for k in range(CONF_KERNEL)]

            def rows_x(rb, carry, lanes=lanes, taps=taps):
                r0 = pl.multiple_of(rb * CONV_BLOCK, CONV_BLOCK)
                zero = jnp.zeros((SUBLANES, LANE_TILE), f32)
                dglu = jnp.concatenate(_conv_rows(dext_ref, shifted_ref, taps, bwd_offsets, lanes, r0, zero), axis=0)
                sg = sg_ref[pl.ds(r0, CONV_BLOCK), lanes]
                cvv = cv_ref[pl.ds(r0, CONV_BLOCK), lanes]
                dcv_ref[pl.ds(r0, CONV_BLOCK), lanes] = (dglu * sg).astype(bf16)
                dcg_ref[pl.ds(r0, CONV_BLOCK), lanes] = (dglu * cvv * sg * (1.0 - sg)).astype(bf16)
                return carry

            lax.fori_loop(0, tm // CONV_BLOCK, rows_x, 0)

    row = _row_spec(tm, 1024)
    return pl.pallas_call(
        body, name="conf_conv_bwd", grid=(t // tm,),
        in_specs=[row, _next_halo_spec(tm, CONV_HALO, 1024, t), row, row, _prev_halo_spec(tm, CONV_HALO, 1024),
                  _prev_halo_spec(tm, CONV_HALO, 1024), _const_spec((32, 1024))],
        out_specs=[row, row, _const_spec((32, 1024))],
        out_shape=(jax.ShapeDtypeStruct((t, 1024), bf16), jax.ShapeDtypeStruct((t, 1024), bf16),
                   jax.ShapeDtypeStruct((32, 1024), f32)),
        scratch_shapes=[pltpu.VMEM((ext_rows, 1024), f32), pltpu.VMEM((ext_rows, 1024), f32), pltpu.VMEM((tm, 1024), f32),
                        pltpu.VMEM((SUBLANES, ext_rows, 1024), f32)],
        compiler_params=_params("arbitrary"),
    )(dv2, dv2, cv, cg, cv, cg, w)


def _ssd_scan_bwd(xbc_act, dt, s_prev, dy, a_log, d_skip):
    t = xbc_act.shape[0]
    nc = t // CHUNK

    def body(xbc_ref, dt_ref, sp_ref, dy_ref, al_ref, dk_ref, dxbc_ref, ddt_ref, dal_ref, ddk_ref, ds_ref):
        i = pl.program_id(0)
        _acc_init(i, ds_ref, dal_ref, ddk_ref)
        consts = _ssd_consts()
        _, vjp_c = jax.vjp(
            functools.partial(_ssd_chunk, consts=consts),
            xbc_ref[:, 0:1024], xbc_ref[:, 1024:1280], xbc_ref[:, 1280:1536], dt_ref[...], sp_ref[...],
            al_ref[...], dk_ref[...])
        dxs, dbm, dcm, ddt, dsp, dal, ddk = vjp_c((dy_ref[...], ds_ref[...]))
        dxbc_ref[:, 0:1024] = dxs
        dxbc_ref[:, 1024:1280] = dbm
        dxbc_ref[:, 1280:1536] = dcm
        ddt_ref[...] = ddt
        ds_ref[...] = dsp
        dal_ref[...] += dal
        ddk_ref[...] += ddk

    rev = lambda i: (nc - 1 - i, 0)
    return pl.pallas_call(
        body, name="ssd_scan_bwd", grid=(nc,),
        in_specs=[pl.BlockSpec((CHUNK, XBC_WIDTH), rev), pl.BlockSpec((CHUNK, SSD_HEADS), rev),
                  pl.BlockSpec((None, SSD_STATE, D_MODEL), lambda i: (nc - 1 - i, 0, 0)),
                  pl.BlockSpec((CHUNK, D_MODEL), rev), _const_spec((1, SSD_HEADS)), _const_spec((1, SSD_HEADS))],
        out_specs=[pl.BlockSpec((CHUNK, XBC_WIDTH), rev), pl.BlockSpec((CHUNK, SSD_HEADS), rev),
                   _const_spec((1, SSD_HEADS)), _const_spec((1, SSD_HEADS))],
        out_shape=(jax.ShapeDtypeStruct((t, XBC_WIDTH), f32), jax.ShapeDtypeStruct((t, SSD_HEADS), f32),
                   jax.ShapeDtypeStruct((1, SSD_HEADS), f32), jax.ShapeDtypeStruct((1, SSD_HEADS), f32)),
        scratch_shapes=[pltpu.VMEM((SSD_STATE, D_MODEL), f32)],
        compiler_params=_params("arbitrary"),
    )(xbc_act, dt, s_prev, dy, a_log, d_skip)


def _ssd_pre_bwd(xbc_raw, dxbc_act, ddt, dt_raw, cw, cb, dt_bias):
    t = xbc_raw.shape[0]
    tm = min(TOKEN_TILE, t)

    def body(cur_ref, halo_ref, dact_ref, ddt_ref, dtr_ref, w_ref, b_ref, dtb_ref,
             dco_ref, ddtr_ref, dw_ref, db_ref, ddtb_ref, ext_ref):
        i = pl.program_id(0)
        _acc_init(i, dw_ref, db_ref, ddtb_ref)
        ext_ref[0:8, :] = jnp.where(i == 0, 0.0, halo_ref[...])
        ext_ref[8:, :] = cur_ref[...]
        co = _conv4_tile(ext_ref, w_ref, b_ref, tm)
        sg = jax.nn.sigmoid(co)
        dco = dact_ref[...] * sg * (1.0 + co * (1.0 - sg))
        dco_ref[...] = dco
        db_ref[...] += jnp.sum(dco, axis=0, keepdims=True)
        for k in range(SSD_CONV):
            dw_ref[k:k + 1, :] += jnp.sum(dco * ext_ref[pl.ds(8 - 3 + k, tm), :], axis=0, keepdims=True)
        ddtr = ddt_ref[...] * jax.nn.sigmoid(dtr_ref[...] + dtb_ref[...])
        ddtb_ref[...] += jnp.sum(ddtr, axis=0, keepdims=True)
        ddtr_ref[...] = ddtr.astype(bf16)

    return pl.pallas_call(
        body, name="ssd_pre_bwd", grid=(t // tm,),
        in_specs=[_row_spec(tm, XBC_WIDTH), _prev_halo_spec(tm, 8, XBC_WIDTH), _row_spec(tm, XBC_WIDTH),
                  _row_spec(tm, SSD_HEADS), _row_spec(tm, SSD_HEADS), _const_spec((8, XBC_WIDTH)),
                  _const_spec((1, XBC_WIDTH)), _const_spec((1, SSD_HEADS))],
        out_specs=[_row_spec(tm, XBC_WIDTH), _row_spec(tm, SSD_HEADS), _const_spec((8, XBC_WIDTH)),
                   _const_spec((1, XBC_WIDTH)), _const_spec((1, SSD_HEADS))],
        out_shape=(jax.ShapeDtypeStruct((t, XBC_WIDTH), f32), jax.ShapeDtypeStruct((t, SSD_HEADS), bf16),
                   jax.ShapeDtypeStruct((8, XBC_WIDTH), f32), jax.ShapeDtypeStruct((1, XBC_WIDTH), f32),
                   jax.ShapeDtypeStruct((1, SSD_HEADS), f32)),
        scratch_shapes=[pltpu.VMEM((tm + 8, XBC_WIDTH), f32)],
        compiler_params=_params("arbitrary"),
    )(xbc_raw, xbc_raw, dxbc_act, ddt, dt_raw, cw, cb, dt_bias)


def _conv4_bwd_data(dco, cw):
    t = dco.shape[0]
    tm = min(TOKEN_TILE, t)

    def body(cur_ref, nxt_ref, w_ref, o_ref, ext_ref):
        i = pl.program_id(0)
        ext_ref[0:tm, :] = cur_ref[...]
        ext_ref[tm:, :] = jnp.where(i == pl.num_programs(0) - 1, 0.0, nxt_ref[...])
        acc = w_ref[0:1, :] * ext_ref[pl.ds(SSD_CONV - 1, tm), :]
        for k in range(1, SSD_CONV):
            acc = acc + w_ref[k:k + 1, :] * ext_ref[pl.ds(SSD_CONV - 1 - k, tm), :]
        o_ref[...] = acc.astype(bf16)

    return pl.pallas_call(
        body, name="conv4_bwd_data", grid=(t // tm,),
        in_specs=[_row_spec(tm, XBC_WIDTH), _next_halo_spec(tm, 8, XBC_WIDTH, t), _const_spec((8, XBC_WIDTH))],
        out_specs=_row_spec(tm, XBC_WIDTH), out_shape=jax.ShapeDtypeStruct((t, XBC_WIDTH), bf16),
        scratch_shapes=[pltpu.VMEM((tm + 8, XBC_WIDTH), f32)],
        compiler_params=_params("arbitrary"),
    )(dco, dco, cw)


def _in_proj_bwd(dproj, x, dh1, g1, w_in_p, token=None):
    t = x.shape[0]
    tm = min(TOKEN_TILE, t)

    def body(dz_ref, dxbc_ref, ddt_ref, dcv_ref, dcg_ref, x_ref, dh_ref, g_ref, w_ref, dx_ref, acc_ref):
        i = pl.program_id(0)
        _acc_init(i, acc_ref)
        du = jnp.zeros((tm, D_MODEL), f32)
        for (lo, hi), r in zip(IN_SEGS, (dz_ref, dxbc_ref, ddt_ref, dcv_ref, dcg_ref)):
            du = du + _nn(r[...], w_ref[lo:hi, :])
        _, vjp_u = jax.vjp(_rms, x_ref[...], g_ref[...])
        d, dg = vjp_u(du)
        dx_ref[...] = dh_ref[...] + d
        acc_ref[0:1, :] += dg

    row = _row_spec(tm, 1024)
    body, tok, tok_spec = _after_token(body, token)
    return pl.pallas_call(
        body, name="in_proj_bwd", grid=(t // tm,),
        in_specs=tok_spec + [_row_spec(tm, hi - lo) for lo, hi in IN_SEGS] + [row, row, _const_spec((1, 1024)),
                                                                               _const_spec((IN_WIDTH, D_MODEL))],
        out_specs=[row, _const_spec((8, 1024))],
        out_shape=(jax.ShapeDtypeStruct((t, 1024), f32), jax.ShapeDtypeStruct((8, 1024), f32)),
        compiler_params=_params("arbitrary"),
    )(*tok, *dproj, x, dh1, g1, w_in_p)


def _mm_tn(a, b, tk, tn, name, column_blocks=False):
    t, kk = a.shape
    n = b.shape[1]
    tk, tn = min(tk, kk), min(tn, n)

    def body(a_ref, b_ref, o_ref):
        o_ref[...] = _tn(a_ref[...], b_ref[...]).astype(bf16)

    if column_blocks:
        assert tk == kk
        out_spec = pl.BlockSpec((None, tk, tn), lambda i, j: (j, 0, 0))
        out_shape = jax.ShapeDtypeStruct((n // tn, kk, tn), bf16)
    else:
        out_spec = pl.BlockSpec((tk, tn), lambda i, j: (i, j))
        out_shape = jax.ShapeDtypeStruct((kk, n), bf16)
    return pl.pallas_call(
        body, name=name, grid=(kk // tk, n // tn),
        in_specs=[pl.BlockSpec((t, tk), lambda i, j: (0, i)), pl.BlockSpec((t, tn), lambda i, j: (0, j))],
        out_specs=out_spec, out_shape=out_shape,
        compiler_params=_params("arbitrary", "arbitrary"),
    )(a, b)


class _LocalWeights:
    def __init__(self, w):
        self.w = w
        self.sent = {}

    def start_token(self):
        return None

    def weight(self, name, after=()):
        del after
        return self.w[name]

    def send_grads(self, grads):
        self.sent.update(grads)
        return None


def _local_step(x, p, target, s, comm):
    conv_w, conf_w = comm.weight("ssd_conv_w"), comm.weight("conf_dw_w")
    w_in = comm.weight("w_in")
    u, z, xbc_raw, dt_raw, cv, cg = _in_proj(x, s["mix_norm_g"], w_in, token=comm.start_token())
    xbc_act, dt = _ssd_pre(xbc_raw, dt_raw, conv_w, s["ssd_conv_b"], s["dt_bias"])
    y, s_prev = _ssd_scan(xbc_act, dt, s["A_log"], s["D_skip"])
    v2 = _conf_conv(cv, cg, conf_w, s["conf_dw_b"])
    w_out = comm.weight("w_out", after=(y, v2))
    ys, yc, h1, u2 = _out_proj(y, z, v2, x, s["ssd_norm_g"], s["conf_ln_g"], s["conf_ln_b"], w_out, s["mlp_norm_g"])
    w_up = comm.weight("w_up", after=(u2,))
    pre, hs = _mlp_up(u2, w_up)
    w_down = comm.weight("w_down", after=(hs,))
    h2 = _mlp_down(h1, hs, w_down)
    w_gate, w_ple = comm.weight("w_ple_gate", after=(h2,)), comm.weight("w_ple", after=(h2,))
    dh2, dh2b, dwg, dwp, tail_acc = _tail(h2, p, target, s["ple_gate_norm_g"], w_gate, s["b_ple_gate"], w_ple,
                                          s["ple_norm_g"], s["final_norm_g"])
    token = comm.send_grads({"w_ple_gate": dwg, "w_ple": dwp})
    dpre, dh1, dh1b, mlp_acc = _mlp_bwd(dh2, dh2b, pre, h1, s["mlp_norm_g"], w_down, w_up, token=token)
    token = comm.send_grads({
        "w_down": _mm_tn(hs, dh2b, 512, 1024, "dw_down"),
        "w_up": _mm_tn(u2, dpre, 1024, D_FF // N_DEV, "dw_up", column_blocks=True),
        "w_out": [_mm_tn(ys, dh1b, 1024, 512, "dw_out_ssd"), _mm_tn(yc, dh1b, 1024, 512, "dw_out_conf")],
    })
    dy, dz, dv2, opb_acc = _out_proj_bwd(dh1b, y, z, v2, s["ssd_norm_g"], s["conf_ln_g"], s["conf_ln_b"], w_out,
                                         token=token)
    dcv, dcg, dconf_w = _conf_conv_bwd(dv2, cv, cg, conf_w)
    dxbc_act, ddt, d_alog, d_dskip = _ssd_scan_bwd(xbc_act, dt, s_prev, dy, s["A_log"], s["D_skip"])
    dco, ddt_raw, dconv_w, dconv_b, d_dtb = _ssd_pre_bwd(xbc_raw, dxbc_act, ddt, dt_raw, conv_w, s["ssd_conv_b"],
                                                        s["dt_bias"])
    dxbc_raw = _conv4_bwd_data(dco, conv_w)
    dproj = (dz, dxbc_raw, ddt_raw, dcv, dcg)
    token = comm.send_grads({"w_in": [_mm_tn(d, u, 512, 1024, "dw_in_" + n)
                                      for n, d in zip(("z", "xbc", "dt", "cv", "cg"), dproj)]})
    grad_x, inp_acc = _in_proj_bwd(dproj, x, dh1, s["mix_norm_g"], w_in, token=token)
    acc = {"in_proj": inp_acc, "out_proj": opb_acc, "mlp": mlp_acc, "tail": tail_acc, "ssd_conv_b": dconv_b,
           "dt_bias": d_dtb, "A_log": d_alog, "D_skip": d_dskip, "ssd_conv_w": dconv_w, "conf_dw_w": dconf_w}
    return grad_x, acc


def _mesh_pos():
    return lax.axis_index("x"), lax.axis_index("y"), lax.axis_index("c")


def _other_chips(x, y):
    return [(1 - x, y), (x, 1 - y), (1 - x, 1 - y)]


def _all_gather(arrs, name):
    n = len(arrs)

    def body(*refs):
        ins, outs = refs[:n], refs[n:2 * n]
        send_sems, recv_sems, local_sems = refs[2 * n:]
        x, y, c = _mesh_pos()
        me = 4 * x + 2 * y + c
        sibling = (x, y, 1 - c)
        chips = _other_chips(x, y)

        def copy(a, k, block, to, src=None):
            dst = outs[a].at[block]
            return pltpu.make_async_remote_copy(
                src_ref=dst if src is None else src, dst_ref=dst, send_sem=send_sems.at[a, k],
                recv_sem=recv_sems.at[a, k], device_id=to, device_id_type=MESH_ID)

        mine = [pltpu.make_async_copy(ins[a], outs[a].at[me], local_sems.at[a]) for a in range(n)]
        for cp in mine:
            cp.start()
        first = []
        for a in range(n):
            first.append(copy(a, 0, me, sibling, src=ins[a]))
            first += [copy(a, 1 + j, me, (px, py, c), src=ins[a]) for j, (px, py) in enumerate(chips)]
        for cp in first:
            cp.start()
        passed = []
        for j, (px, py) in enumerate(chips):
            for a in range(n):
                blk = 4 * px + 2 * py + c
                copy(a, 1 + j, blk, (x, y, c)).wait_recv()
                cp = copy(a, 4 + j, blk, sibling)
                cp.start()
                passed.append(cp)
        for a in range(n):
            copy(a, 0, 4 * x + 2 * y + (1 - c), (x, y, c)).wait_recv()
        for j, (px, py) in enumerate(chips):
            for a in range(n):
                copy(a, 4 + j, 4 * px + 2 * py + (1 - c), (x, y, c)).wait_recv()
        for cp in first + passed:
            cp.wait_send()
        for cp in mine:
            cp.wait()

    return pl.pallas_call(
        body, name=name,
        in_specs=[HBM_SPEC] * n, out_specs=[HBM_SPEC] * n,
        out_shape=[jax.ShapeDtypeStruct((N_DEV,) + a.shape, a.dtype) for a in arrs],
        scratch_shapes=[pltpu.SemaphoreType.DMA((n, 7)), pltpu.SemaphoreType.DMA((n, 7)), pltpu.SemaphoreType.DMA((n,))],
    )(*arrs)


_PEER_FLIPS = ((0, 0, 1), (1, 0, 0), (0, 1, 0), (1, 1, 0), (1, 0, 1), (0, 1, 1), (1, 1, 1))
SEM_SPEC = pl.BlockSpec(memory_space=pltpu.SEMAPHORE)
ANY_SPEC = pl.BlockSpec(memory_space=pl.ANY)


def _flip(v, d):
    return 1 - v if d else v


def _peers(x, y, c):
    out = []
    for dx, dy, dc in _PEER_FLIPS:
        px, py, pc = _flip(x, dx), _flip(y, dy), _flip(c, dc)
        out.append(((px, py, pc), 4 * px + 2 * py + pc))
    return out


def _exchange_copy(src_ref, land_ref, send_sems, recv_sems, k, peer, peer_block, my_block, by_block, outgoing):
    src = src_ref.at[peer_block] if by_block else src_ref
    dst = land_ref.at[my_block if outgoing else peer_block]
    return pltpu.make_async_remote_copy(src_ref=src, dst_ref=dst, send_sem=send_sems.at[k], recv_sem=recv_sems.at[k],
                                        device_id=peer, device_id_type=MESH_ID)


def _exchange_start(srcs, lands, by_block, name, after):
    n = len(srcs)

    def body(*refs):
        src_refs, land_refs = refs[1:1 + n], refs[1 + n:1 + 2 * n]
        outs = refs[1 + 2 * n:]
        send, recv, token = outs[:n], outs[n:2 * n], outs[4 * n]
        x, y, c = _mesh_pos()
        me = 4 * x + 2 * y + c
        for a in range(n):
            for k, (peer, blk) in enumerate(_peers(x, y, c)):
                _exchange_copy(src_refs[a], land_refs[a], send[a], recv[a], k, peer, blk, me, by_block, True).start()
        token[...] = jnp.zeros_like(token)

    sems = [pltpu.SemaphoreType.DMA((N_DEV - 1,))] * (2 * n)
    thru = [pltpu.HBM(a.shape, a.dtype) for a in list(srcs) + list(lands)]
    res = pl.pallas_call(
        body, name=name,
        in_specs=[ANY_SPEC] + [HBM_SPEC] * (2 * n),
        out_specs=[SEM_SPEC] * (2 * n) + [HBM_SPEC] * (2 * n) + [pl.BlockSpec(memory_space=pltpu.VMEM)],
        out_shape=sems + thru + [jax.ShapeDtypeStruct((8, 128), f32)],
        input_output_aliases={1 + i: 2 * n + i for i in range(2 * n)},
        compiler_params=pltpu.CompilerParams(has_side_effects=pltpu.SideEffectType.DATAFLOW_SIDE_EFFECTING),
    )(after, *[pltpu.with_memory_space_constraint(a, pltpu.HBM) for a in list(srcs) + list(lands)])
    states = [(res[2 * n + a], res[3 * n + a], res[a], res[n + a]) for a in range(n)]
    return states, res[4 * n]


def _exchange_wait(states, by_block, name, after):
    n, na = len(states), len(after)

    def body(*refs):
        src_refs, land_refs = refs[:n], refs[n:2 * n]
        send, recv = refs[2 * n:3 * n], refs[3 * n:4 * n]
        x, y, c = _mesh_pos()
        me = 4 * x + 2 * y + c
        for a in range(n):
            for k, (peer, blk) in enumerate(_peers(x, y, c)):
                _exchange_copy(src_refs[a], land_refs[a], send[a], recv[a], k, peer, blk, me, by_block, True).wait_send()
                _exchange_copy(src_refs[a], land_refs[a], send[a], recv[a], k, peer, blk, me, by_block, False).wait_recv()

    srcs, lands = [s[0] for s in states], [s[1] for s in states]
    res = pl.pallas_call(
        body, name=name,
        in_specs=[HBM_SPEC] * (2 * n) + [SEM_SPEC] * (2 * n) + [ANY_SPEC] * na,
        out_specs=[HBM_SPEC] * (2 * n),
        out_shape=[pltpu.HBM(a.shape, a.dtype) for a in srcs + lands],
        input_output_aliases={i: i for i in range(2 * n)},
        compiler_params=pltpu.CompilerParams(has_side_effects=pltpu.SideEffectType.DATAFLOW_SIDE_EFFECTING),
    )(*srcs, *lands, *[s[2] for s in states], *[s[3] for s in states], *after)
    return list(res[n:2 * n])


def _with_own_block(own, me):
    buf = lax.empty((N_DEV,) + own.shape, own.dtype)
    return lax.dynamic_update_slice(buf, own[None], (me,) + (0,) * own.ndim)


def _adamw_math(w, g, m, v):
    m = ADAM_B1 * m + (1.0 - ADAM_B1) * g
    v = ADAM_B2 * v + (1.0 - ADAM_B2) * (g * g)
    m_hat = m / (1.0 - ADAM_B1 ** ADAM_STEP)
    v_hat = v / (1.0 - ADAM_B2 ** ADAM_STEP)
    delta = -ADAM_LR * (m_hat / (jnp.sqrt(v_hat) + ADAM_EPS) + ADAM_WD * w)
    return delta, m, v


def _adamw_big(parts, w, m, v, name):
    rows, cols = w.shape
    tr, tc = (256, cols) if rows % 256 == 0 else (rows, 256)
    nparts = parts.shape[0]

    def body(p_ref, w_ref, m_ref, v_ref, g_ref, d_ref, mo_ref, vo_ref):
        g = p_ref[0].astype(f32)
        for j in range(1, nparts):
            g = g + p_ref[j].astype(f32)
        d, mn, vn = _adamw_math(w_ref[...], g, m_ref[...], v_ref[...])
        g_ref[...] = g
        d_ref[...] = d
        mo_ref[...] = mn
        vo_ref[...] = vn

    tile = pl.BlockSpec((tr, tc), lambda i, j: (i, j))
    shp = jax.ShapeDtypeStruct((rows, cols), f32)
    return pl.pallas_call(
        body, name=name, grid=(rows // tr, cols // tc),
        in_specs=[pl.BlockSpec((nparts, tr, tc), lambda i, j: (0, i, j)), tile, tile, tile],
        out_specs=[tile, tile, tile, tile], out_shape=(shp, shp, shp, shp),
        compiler_params=_params("arbitrary", "arbitrary"),
    )(parts, w, m, v)


PACK_ROWS = 56
_PACK_AT = {
    "mix_norm_g": (0, 0, 1024), "ssd_norm_g": (1, 0, 1024), "conf_ln_g": (2, 0, 1024), "conf_ln_b": (3, 0, 1024),
    "conf_dw_b": (4, 0, 1024), "mlp_norm_g": (5, 0, 1024), "final_norm_g": (7, 0, 1024), "ple_norm_g": (8, 0, 1024),
    "b_ple_gate": (9, 0, 1024), "ple_gate_norm_g": (10, 0, 1024), "dt_bias": (13, 0, 16), "A_log": (13, 128, 16),
    "D_skip": (13, 256, 16),
}
PACK_LOSS_ROW = 6
PACK_CONV_B_ROW = 11
PACK_CONV_W_ROW = 14
PACK_CONF_W_ROW = 24


def _pack_small(acc):
    def body(inp, opb, mlp, tail, cb, dtb, alog, dskip, cw, fw, o_ref):
        o_ref[...] = jnp.zeros_like(o_ref)
        rows = {"mix_norm_g": inp[0:1, :], "mlp_norm_g": mlp[0:1, :],
                "ssd_norm_g": opb[OPB_SSD_G:OPB_SSD_G + 1, :], "conf_ln_g": opb[OPB_LN_G:OPB_LN_G + 1, :],
                "conf_ln_b": opb[OPB_LN_B:OPB_LN_B + 1, :], "conf_dw_b": opb[OPB_CONV_B:OPB_CONV_B + 1, :],
                "final_norm_g": tail[TAIL_FINAL_G:TAIL_FINAL_G + 1, :], "ple_norm_g": tail[TAIL_PLE_G:TAIL_PLE_G + 1, :],
                "b_ple_gate": tail[TAIL_GATE_B:TAIL_GATE_B + 1, :],
                "ple_gate_norm_g": tail[TAIL_GATE_NORM_G:TAIL_GATE_NORM_G + 1, :],
                "dt_bias": dtb[...], "A_log": alog[...], "D_skip": dskip[...]}
        for name, val in rows.items():
            r, lo, width = _PACK_AT[name]
            o_ref[r:r + 1, lo:lo + width] = val
        o_ref[PACK_LOSS_ROW:PACK_LOSS_ROW + 1, :] = tail[TAIL_LOSS:TAIL_LOSS + 1, :]
        o_ref[PACK_CONV_B_ROW:PACK_CONV_B_ROW + 1, :] = cb[:, 0:1024]
        o_ref[PACK_CONV_B_ROW + 1:PACK_CONV_B_ROW + 2, 0:512] = cb[:, 1024:XBC_WIDTH]
        for k in range(SSD_CONV):
            o_ref[PACK_CONV_W_ROW + k:PACK_CONV_W_ROW + k + 1, :] = cw[k:k + 1, 0:1024]
            o_ref[PACK_CONV_W_ROW + SSD_CONV + k:PACK_CONV_W_ROW + SSD_CONV + k + 1, 0:512] = cw[k:k + 1, 1024:XBC_WIDTH]
        o_ref[PACK_CONF_W_ROW:PACK_CONF_W_ROW + 32, :] = fw[...]

    return pl.pallas_call(body, name="pack_small", out_shape=jax.ShapeDtypeStruct((PACK_ROWS, 1024), f32))(
        acc["in_proj"], acc["out_proj"], acc["mlp"], acc["tail"], acc["ssd_conv_b"], acc["dt_bias"], acc["A_log"],
        acc["D_skip"], acc["ssd_conv_w"], acc["conf_dw_w"])


def _small_update(all_small, w, m, v):
    names = _REPLICATED

    def body(all_ref, *refs):
        ins, outs = refs[:3 * len(names)], refs[3 * len(names):]
        s = all_ref[0]
        for j in range(1, N_DEV):
            s = s + all_ref[j]
        outs[0][...] = s[PACK_LOSS_ROW:PACK_LOSS_ROW + 1, 0:1]
        outs[1][...] = jnp.concatenate([s[PACK_CONV_W_ROW:PACK_CONV_W_ROW + SSD_CONV, :],
                                        s[PACK_CONV_W_ROW + SSD_CONV:PACK_CONV_W_ROW + 2 * SSD_CONV, 0:512]], axis=1)
        outs[2][...] = s[PACK_CONF_W_ROW:PACK_CONF_W_ROW + CONF_KERNEL, :]
        for i, name in enumerate(names):
            if name == "ssd_conv_b":
                g = jnp.concatenate([s[PACK_CONV_B_ROW:PACK_CONV_B_ROW + 1, :],
                                     s[PACK_CONV_B_ROW + 1:PACK_CONV_B_ROW + 2, 0:512]], axis=1)
            else:
                r, lo, width = _PACK_AT[name]
                g = s[r:r + 1, lo:lo + width]
            d, mn, vn = _adamw_math(ins[3 * i][...], g, ins[3 * i + 1][...], ins[3 * i + 2][...])
            for o_ref, val in zip(outs[3 + 4 * i:7 + 4 * i], (g, d, mn, vn)):
                o_ref[...] = val

    shapes = [jax.ShapeDtypeStruct((1, 1), f32), jax.ShapeDtypeStruct((SSD_CONV, XBC_WIDTH), f32),
              jax.ShapeDtypeStruct((CONF_KERNEL, D_MODEL), f32)]
    operands = []
    for name in names:
        operands += [w[name], m[name], v[name]]
        shapes += [jax.ShapeDtypeStruct(w[name].shape, f32)] * 4
    res = pl.pallas_call(body, name="small_update", out_shape=shapes)(all_small, *operands)
    per_name = {name: tuple(res[3 + 4 * i:7 + 4 * i]) for i, name in enumerate(names)}
    return res[0], per_name, res[1], res[2]


def _adamw_filters(g, w, m, v):
    n = len(g)

    def body(*refs):
        ins, outs = refs[:4 * n], refs[4 * n:]
        for i in range(n):
            d, mn, vn = _adamw_math(ins[4 * i + 1][...], ins[4 * i][...], ins[4 * i + 2][...], ins[4 * i + 3][...])
            for o_ref, val in zip(outs[3 * i:3 * i + 3], (d, mn, vn)):
                o_ref[...] = val

    operands, shapes = [], []
    for i in range(n):
        operands += [g[i], w[i], m[i], v[i]]
        shapes += [jax.ShapeDtypeStruct(w[i].shape, f32)] * 3
    res = pl.pallas_call(body, name="adamw_filters", out_shape=shapes)(*operands)
    return [tuple(res[3 * i:3 * i + 3]) for i in range(n)]


_REPLICATED = ("mix_norm_g", "ssd_conv_b", "dt_bias", "A_log", "D_skip", "ssd_norm_g", "conf_dw_b", "conf_ln_g",
               "conf_ln_b", "mlp_norm_g", "ple_gate_norm_g", "b_ple_gate", "ple_norm_g", "final_norm_g")
_CONV_WEIGHTS = ("ssd_conv_w", "conf_dw_w")
_BIG = ("w_in", "w_out", "w_up", "w_down", "w_ple_gate", "w_ple")
_WEIGHT_ORDER = ("mix_norm_g", "w_in", "ssd_conv_w", "ssd_conv_b", "dt_bias", "A_log", "D_skip", "ssd_norm_g", "conf_dw_w",
                 "conf_dw_b", "conf_ln_g", "conf_ln_b", "w_out", "mlp_norm_g", "w_up", "w_down", "ple_gate_norm_g",
                 "w_ple_gate", "b_ple_gate", "w_ple", "ple_norm_g", "final_norm_g")


def _blocks_of_columns(a):
    r, c8 = a.shape
    return jnp.transpose(a.reshape(r, N_DEV, c8 // N_DEV), (1, 0, 2))


def _columns_of_blocks(a):
    _, r, c = a.shape
    return jnp.transpose(a, (1, 0, 2)).reshape(r, N_DEV * c)


_LATER = ("w_out", "w_up", "w_down", "w_ple_gate", "w_ple")
_WHOLE = {
    "w_out": lambda a: a.reshape(2048, D_MODEL),
    "w_up": lambda a: a,
    "w_down": lambda a: a.reshape(D_FF, D_MODEL),
    "w_ple_gate": lambda a: a.reshape(D_MODEL, D_MODEL),
    "w_ple": _columns_of_blocks,
}
_BY_BLOCK = {
    "w_in": lambda g: jnp.concatenate(g, axis=0).reshape(N_DEV, IN_WIDTH // N_DEV, D_MODEL),
    "w_out": lambda g: jnp.concatenate(g, axis=0).reshape(N_DEV, 256, D_MODEL),
    "w_up": lambda g: g,
    "w_down": lambda g: g.reshape(N_DEV, 512, D_MODEL),
    "w_ple_gate": lambda g: g.astype(bf16).reshape(N_DEV, 128, D_MODEL),
    "w_ple": lambda g: _blocks_of_columns(g.astype(bf16)),
}


class _StepComm:
    def __init__(self, me, ready, gathers, token):
        self.me, self.ready, self.gathers, self.token = me, ready, gathers, token
        self.sent = []

    def start_token(self):
        return self.token

    def weight(self, name, after=()):
        if name not in self.ready:
            (land,) = _exchange_wait([self.gathers[name]], False, "gather_wait_" + name, list(after))
            self.ready[name] = _WHOLE[name](land)
        return self.ready[name]

    def send_grads(self, grads):
        names = list(grads)
        blocks = [_BY_BLOCK[n](grads[n]) for n in names]
        lands = [_with_own_block(lax.dynamic_index_in_dim(b, self.me, 0, keepdims=False), self.me) for b in blocks]
        states, self.token = _exchange_start(blocks, lands, True, "scatter_start_" + names[0], self.token)
        self.sent.append((names, states))
        return self.token


def kernel(x, p, mix_norm_g, w_in, ssd_conv_w, ssd_conv_b, dt_bias, A_log, D_skip, ssd_norm_g, conf_dw_w, conf_dw_b, conf_ln_g, conf_ln_b, w_out, mlp_norm_g, w_up, w_down, ple_gate_norm_g, w_ple_gate, b_ple_gate, w_ple, ple_norm_g, final_norm_g, loss_target, m_mix_norm_g, m_w_in, m_ssd_conv_w, m_ssd_conv_b, m_dt_bias, m_A_log, m_D_skip, m_ssd_norm_g, m_conf_dw_w, m_conf_dw_b, m_conf_ln_g, m_conf_ln_b, m_w_out, m_mlp_norm_g, m_w_up, m_w_down, m_ple_gate_norm_g, m_w_ple_gate, m_b_ple_gate, m_w_ple, m_ple_norm_g, m_final_norm_g, v_mix_norm_g, v_w_in, v_ssd_conv_w, v_ssd_conv_b, v_dt_bias, v_A_log, v_D_skip, v_ssd_norm_g, v_conf_dw_w, v_conf_dw_b, v_conf_ln_g, v_conf_ln_b, v_w_out, v_mlp_norm_g, v_w_up, v_w_down, v_ple_gate_norm_g, v_w_ple_gate, v_b_ple_gate, v_w_ple, v_ple_norm_g, v_final_norm_g):
    wts = dict(mix_norm_g=mix_norm_g, w_in=w_in, ssd_conv_w=ssd_conv_w, ssd_conv_b=ssd_conv_b, dt_bias=dt_bias, A_log=A_log,
               D_skip=D_skip, ssd_norm_g=ssd_norm_g, conf_dw_w=conf_dw_w, conf_dw_b=conf_dw_b, conf_ln_g=conf_ln_g,
               conf_ln_b=conf_ln_b, w_out=w_out, mlp_norm_g=mlp_norm_g, w_up=w_up, w_down=w_down,
               ple_gate_norm_g=ple_gate_norm_g, w_ple_gate=w_ple_gate, b_ple_gate=b_ple_gate, w_ple=w_ple,
               ple_norm_g=ple_norm_g, final_norm_g=final_norm_g)
    mom1 = dict(mix_norm_g=m_mix_norm_g, w_in=m_w_in, ssd_conv_w=m_ssd_conv_w, ssd_conv_b=m_ssd_conv_b, dt_bias=m_dt_bias,
                A_log=m_A_log, D_skip=m_D_skip, ssd_norm_g=m_ssd_norm_g, conf_dw_w=m_conf_dw_w, conf_dw_b=m_conf_dw_b,
                conf_ln_g=m_conf_ln_g, conf_ln_b=m_conf_ln_b, w_out=m_w_out, mlp_norm_g=m_mlp_norm_g, w_up=m_w_up,
                w_down=m_w_down, ple_gate_norm_g=m_ple_gate_norm_g, w_ple_gate=m_w_ple_gate, b_ple_gate=m_b_ple_gate,
                w_ple=m_w_ple, ple_norm_g=m_ple_norm_g, final_norm_g=m_final_norm_g)
    mom2 = dict(mix_norm_g=v_mix_norm_g, w_in=v_w_in, ssd_conv_w=v_ssd_conv_w, ssd_conv_b=v_ssd_conv_b, dt_bias=v_dt_bias,
                A_log=v_A_log, D_skip=v_D_skip, ssd_norm_g=v_ssd_norm_g, conf_dw_w=v_conf_dw_w, conf_dw_b=v_conf_dw_b,
                conf_ln_g=v_conf_ln_g, conf_ln_b=v_conf_ln_b, w_out=v_w_out, mlp_norm_g=v_mlp_norm_g, w_up=v_w_up,
                w_down=v_w_down, ple_gate_norm_g=v_ple_gate_norm_g, w_ple_gate=v_w_ple_gate, b_ple_gate=v_b_ple_gate,
                w_ple=v_w_ple, ple_norm_g=v_ple_norm_g, final_norm_g=v_final_norm_g)
    x_pos, y_pos, c_pos = _mesh_pos()
    me = 4 * x_pos + 2 * y_pos + c_pos

    first = _all_gather([wts["w_in"][0].T.astype(bf16), wts["ssd_conv_w"][0], wts["conf_dw_w"][0]], "gather_first")
    ready = {
        "w_in": first[0].reshape(IN_WIDTH, D_MODEL),
        "ssd_conv_w": jnp.pad(_columns_of_blocks(first[1]), ((0, 8 - SSD_CONV), (0, 0))),
        "conf_dw_w": jnp.pad(_columns_of_blocks(first[2]), ((0, 32 - CONF_KERNEL), (0, 0))),
    }
    shards = [wts[n][0].astype(bf16) for n in _LATER]
    states, token = _exchange_start(shards, [_with_own_block(sh, me) for sh in shards], False, "gather_start", first[1])
    comm = _StepComm(me, ready, dict(zip(_LATER, states)), token)
    small = {n: wts[n].reshape(1, -1) for n in _REPLICATED}

    grad_x, acc = _local_step(x[0], p[0, 0], loss_target[0], small, comm)

    packed = _pack_small(acc)
    (small_state,), _ = _exchange_start([packed], [_with_own_block(packed, me)], False, "small_start", comm.token)
    grads, delta, new_m, new_v = {}, {}, {}, {}

    def adamw_big(names, sent, after):
        lands = _exchange_wait(sent, True, "scatter_wait_" + names[0], after)
        for n, land in zip(names, lands):
            view = (lambda a: a[0].T) if n == "w_in" else (lambda a: a[0])
            back = (lambda a: a.T[None]) if n == "w_in" else (lambda a: a[None])
            out = _adamw_big(land, view(wts[n]), view(mom1[n]), view(mom2[n]), "adamw_" + n)
            grads[n], delta[n], new_m[n], new_v[n] = [back(a) for a in out]

    for names, sent in comm.sent[:-1]:
        adamw_big(names, sent, [grad_x])
    (all_small,) = _exchange_wait([small_state], False, "small_wait", [delta[n] for n in _LATER])
    as_row = lambda d: {n: d[n].reshape(1, -1) for n in _REPLICATED}
    loss, per_name, conv_w_sum, conf_w_sum = _small_update(all_small, as_row(wts), as_row(mom1), as_row(mom2))
    for n in _REPLICATED:
        grads[n], delta[n], new_m[n], new_v[n] = [a.reshape(wts[n].shape) for a in per_name[n]]
    filt_g = [lax.dynamic_slice_in_dim(conv_w_sum, me * 192, 192, axis=1),
              lax.dynamic_slice_in_dim(conf_w_sum, me * 128, 128, axis=1)]
    filt = _adamw_filters(filt_g, [wts[n][0] for n in _CONV_WEIGHTS], [mom1[n][0] for n in _CONV_WEIGHTS],
                          [mom2[n][0] for n in _CONV_WEIGHTS])
    for n, g, (d, mn, vn) in zip(_CONV_WEIGHTS, filt_g, filt):
        grads[n], delta[n], new_m[n], new_v[n] = g[None], d[None], mn[None], vn[None]
    adamw_big(*comm.sent[-1], [delta[n] for n in _LATER])

    return (loss.reshape(()), grad_x[None], *[grads[n] for n in _WEIGHT_ORDER], *[delta[n] for n in _WEIGHT_ORDER],
            *[new_m[n] for n in _WEIGHT_ORDER], *[new_v[n] for n in _WEIGHT_ORDER])
```

```python
import functools

import jax
import jax.numpy as jnp
from jax import lax
from jax.experimental import pallas as pl
from jax.experimental.pallas import tpu as pltpu

f32 = jnp.float32
bf16 = jnp.bfloat16

EPS = 1e-6
D_MODEL = 1024
SSD_HEADS = 16
SSD_HEAD_DIM = 64
SSD_GROUPS = 2
SSD_STATE = 128
SSD_CONV = 4
CHUNK = 128
XBC_WIDTH = 1536
CONF_KERNEL = 31
D_FF = 4096
PLE_DIM = 256
IN_WIDTH = 4624
N_DEV = 8
SEG_Z = (0, 1024)
SEG_XBC = (1024, 2560)
SEG_DT = (2560, 2576)
SEG_CV = (2576, 3600)
SEG_CG = (3600, 4624)
IN_SEGS = (SEG_Z, SEG_XBC, SEG_DT, SEG_CV, SEG_CG)

ADAM_LR = 0.001
ADAM_B1 = 0.9
ADAM_B2 = 0.999
ADAM_EPS = 1e-08
ADAM_WD = 0.01
ADAM_STEP = 10

VMEM_LIMIT_BYTES = 56 * 1024 * 1024
TOKEN_TILE = 256
SMALL_ROWS = 16

HBM_SPEC = pl.BlockSpec(memory_space=pltpu.HBM)
MESH_ID = pl.DeviceIdType.MESH


def _params(*semantics):
    return pltpu.CompilerParams(dimension_semantics=semantics, vmem_limit_bytes=VMEM_LIMIT_BYTES)


def _nn(a, b):
    return lax.dot_general(a, b, (((1,), (0,)), ((), ())), preferred_element_type=f32)


def _nt(a, b):
    return lax.dot_general(a, b, (((1,), (1,)), ((), ())), preferred_element_type=f32)


def _tn(a, b):
    return lax.dot_general(a, b, (((0,), (0,)), ((), ())), preferred_element_type=f32)


@jax.custom_vjp
def bnn(a, b):
    return _nn(a.astype(bf16), b.astype(bf16))


def _bnn_fwd(a, b):
    ab, bb = a.astype(bf16), b.astype(bf16)
    return _nn(ab, bb), (ab, bb)


def _bnn_bwd(res, g):
    ab, bb = res
    gb = g.astype(bf16)
    return _nt(gb, bb), _tn(ab, gb)


bnn.defvjp(_bnn_fwd, _bnn_bwd)


@jax.custom_vjp
def bnt(a, b):
    return _nt(a.astype(bf16), b.astype(bf16))


def _bnt_fwd(a, b):
    ab, bb = a.astype(bf16), b.astype(bf16)
    return _nt(ab, bb), (ab, bb)


def _bnt_bwd(res, g):
    ab, bb = res
    gb = g.astype(bf16)
    return _nn(gb, bb), _tn(gb, ab)


bnt.defvjp(_bnt_fwd, _bnt_bwd)


@jax.custom_vjp
def btn(a, b):
    return _tn(a.astype(bf16), b.astype(bf16))


def _btn_fwd(a, b):
    ab, bb = a.astype(bf16), b.astype(bf16)
    return _tn(ab, bb), (ab, bb)


def _btn_bwd(res, g):
    ab, bb = res
    gb = g.astype(bf16)
    return _nt(bb, gb), _nn(ab, gb)


btn.defvjp(_btn_fwd, _btn_bwd)

CUMSUM_PASSES = 3
EXPAND_PASSES = 2


def _bf16_pieces(a, passes):
    pieces, rest = [], a
    for _ in range(passes):
        piece = rest.astype(bf16)
        pieces.append(piece)
        rest = rest - piece.astype(f32)
    return pieces


def _make_dot01(form, passes):
    fwd, bwd = {"right": (lambda b, p: _nn(p, b), lambda b, g: _nt(g, b)),
                "left": (lambda b, p: _nn(b, p), lambda b, g: _tn(b, g)),
                "tn": (lambda b, p: _tn(p, b), lambda b, g: _nt(b, g))}[form]

    def run(op, b, v):
        out = None
        for piece in _bf16_pieces(v, passes):
            term = op(b, piece)
            out = term if out is None else out + term
        return out

    @jax.custom_vjp
    def product(b, v):
        return run(fwd, b, v)

    product.defvjp(lambda b, v: (run(fwd, b, v), b), lambda b, g: (jnp.zeros_like(b), run(bwd, b, g)))
    return product


_DOT01 = {(form, passes): _make_dot01(form, passes)
          for form, passes in (("left", CUMSUM_PASSES), ("tn", CUMSUM_PASSES), ("right", EXPAND_PASSES))}


def _dot01(b01, a, form, passes):
    return _DOT01[form, passes](b01.astype(bf16), a)


def _rms(x, g):
    return x * lax.rsqrt(jnp.mean(x * x, axis=-1, keepdims=True) + EPS) * g


def _gated_norm(y, z, g):
    v = y * jax.nn.silu(z)
    half = v.shape[-1] // SSD_GROUPS
    parts = []
    for k in range(SSD_GROUPS):
        vk = v[:, k * half:(k + 1) * half]
        parts.append(vk * lax.rsqrt(jnp.mean(vk * vk, axis=-1, keepdims=True) + EPS))
    return jnp.concatenate(parts, axis=-1) * g


def _ln_silu(v, g, b):
    mu = jnp.mean(v, axis=-1, keepdims=True)
    xc = v - mu
    y = xc * lax.rsqrt(jnp.mean(xc * xc, axis=-1, keepdims=True) + EPS) * g + b
    return jax.nn.silu(y)


def _acc_init(i, *refs):
    @pl.when(i == 0)
    def _():
        for r in refs:
            r[...] = jnp.zeros_like(r)


def _after_token(body, token):
    if token is None:
        return body, [], []

    def body_after(tok_ref, *refs):
        del tok_ref
        body(*refs)

    return body_after, [token], [pl.BlockSpec(memory_space=pl.ANY)]


def _row_spec(tm, n):
    return pl.BlockSpec((tm, n), lambda i: (i, 0))


def _const_spec(shape):
    nd = len(shape)
    return pl.BlockSpec(shape, lambda i: (0,) * nd)


def _prev_halo_spec(tm, halo, n):
    return pl.BlockSpec((halo, n), lambda i: (jnp.maximum(i * (tm // halo) - 1, 0), 0))


def _next_halo_spec(tm, halo, n, t):
    return pl.BlockSpec((halo, n), lambda i: (jnp.minimum((i + 1) * (tm // halo), t // halo - 1), 0))


def _in_proj(x, g1, w_in_p, token=None):
    t = x.shape[0]
    tm = min(TOKEN_TILE, t)

    def body(x_ref, g_ref, w_ref, u_ref, z_ref, xbc_ref, dt_ref, cv_ref, cg_ref):
        u = _rms(x_ref[...], g_ref[...]).astype(bf16)
        u_ref[...] = u
        for (lo, hi), o_ref in zip(IN_SEGS, (z_ref, xbc_ref, dt_ref, cv_ref, cg_ref)):
            o_ref[...] = _nt(u, w_ref[lo:hi, :])

    widths = [hi - lo for lo, hi in IN_SEGS]
    outs = [jax.ShapeDtypeStruct((t, D_MODEL), bf16)] + [jax.ShapeDtypeStruct((t, n), f32) for n in widths]
    body, tok, tok_spec = _after_token(body, token)
    return pl.pallas_call(
        body, name="in_proj", grid=(t // tm,),
        in_specs=tok_spec + [_row_spec(tm, D_MODEL), _const_spec((1, D_MODEL)), _const_spec((IN_WIDTH, D_MODEL))],
        out_specs=[_row_spec(tm, D_MODEL)] + [_row_spec(tm, n) for n in widths],
        out_shape=outs, compiler_params=_params("arbitrary"),
    )(*tok, x, g1, w_in_p)


ROW_BLOCK = 16
LANE_CHUNK = 512


def _blocks(rows, cols):
    return [(slice(r, r + ROW_BLOCK), slice(c, c + LANE_CHUNK))
            for c in range(0, cols, LANE_CHUNK) for r in range(0, rows, ROW_BLOCK)]


def _conv4_block(ext_ref, w_ref, b_ref, rs, ls):
    acc = b_ref[:, ls] + w_ref[0:1, ls] * ext_ref[rs.start + 8 - 3:rs.stop + 8 - 3, ls]
    for k in range(1, SSD_CONV):
        acc = acc + w_ref[k:k + 1, ls] * ext_ref[rs.start + 8 - 3 + k:rs.stop + 8 - 3 + k, ls]
    return acc


def _ssd_pre(xbc_raw, dt_raw, cw, cb, dt_bias):
    t = xbc_raw.shape[0]
    tm = min(TOKEN_TILE, t)

    def body(cur_ref, halo_ref, dt_ref, w_ref, b_ref, dtb_ref, act_ref, dto_ref, ext_ref):
        i = pl.program_id(0)
        ext_ref[0:8, :] = jnp.where(i == 0, 0.0, halo_ref[...])
        ext_ref[8:, :] = cur_ref[...]
        for rs, ls in _blocks(tm, XBC_WIDTH):
            act_ref[rs, ls] = jax.nn.silu(_conv4_block(ext_ref, w_ref, b_ref, rs, ls))
        dto_ref[...] = jax.nn.softplus(dt_ref[...] + dtb_ref[...])

    return pl.pallas_call(
        body, name="ssd_pre", grid=(t // tm,),
        in_specs=[_row_spec(tm, XBC_WIDTH), _prev_halo_spec(tm, 8, XBC_WIDTH), _row_spec(tm, SSD_HEADS),
                  _const_spec((8, XBC_WIDTH)), _const_spec((1, XBC_WIDTH)), _const_spec((1, SSD_HEADS))],
        out_specs=[_row_spec(tm, XBC_WIDTH), _row_spec(tm, SSD_HEADS)],
        out_shape=(jax.ShapeDtypeStruct((t, XBC_WIDTH), f32), jax.ShapeDtypeStruct((t, SSD_HEADS), f32)),
        scratch_shapes=[pltpu.VMEM((tm + 8, XBC_WIDTH), f32)],
        compiler_params=_params("arbitrary"),
    )(xbc_raw, xbc_raw, dt_raw, cw, cb, dt_bias)


def _ssd_consts():
    r = lax.broadcasted_iota(jnp.int32, (CHUNK, CHUNK), 0)
    c = lax.broadcasted_iota(jnp.int32, (CHUNK, CHUNK), 1)
    causal = r >= c
    tril = causal.astype(f32)
    triu = (r <= c).astype(f32)
    hh = lax.broadcasted_iota(jnp.int32, (SSD_HEADS, D_MODEL), 0)
    jj = lax.broadcasted_iota(jnp.int32, (SSD_HEADS, D_MODEL), 1)
    expand = (lax.shift_right_logical(jj, 6) == hh).astype(f32)
    lane = lax.broadcasted_iota(jnp.int32, (1, 2 * SSD_HEAD_DIM), 1)
    m0 = (lane < SSD_HEAD_DIM).astype(f32)
    return causal, tril, triu, expand, m0, 1.0 - m0


def _ssd_chunk(xs, bm, cm, dt, s_prev, a_log, d_skip, consts):
    causal, tril, triu, expand, m0, m1 = consts
    a = dt * (-jnp.exp(a_log))
    cs = _dot01(tril, a, "left", CUMSUM_PASSES)
    cs_t = _dot01(triu, a, "tn", CUMSUM_PASSES)
    cs_last = cs[CHUNK - 1:CHUNK, :]
    per_head = jnp.concatenate([dt, jnp.exp(cs_last - cs), jnp.exp(cs), jnp.broadcast_to(jnp.exp(cs_last), (8, SSD_HEADS)),
                                jnp.broadcast_to(d_skip, (8, SSD_HEADS))], axis=0)
    per_channel = _dot01(expand, per_head, "right", EXPAND_PASSES)
    dt_e, dec_end, dec_start = (per_channel[i * CHUNK:(i + 1) * CHUNK] for i in range(3))
    chunk_dec = per_channel[3 * CHUNK:3 * CHUNK + 1]
    d_e = per_channel[3 * CHUNK + 8:3 * CHUNK + 9]
    xc = xs * dt_e
    x_dec = xc * dec_end
    gw = D_MODEL // SSD_GROUPS
    ys, states = [], []
    for g in range(SSD_GROUPS):
        bg = bm[:, g * SSD_STATE:(g + 1) * SSD_STATE]
        cg = cm[:, g * SSD_STATE:(g + 1) * SSD_STATE]
        sp = s_prev[:, g * gw:(g + 1) * gw]
        states.append(sp * chunk_dec[:, g * gw:(g + 1) * gw] + btn(bg, x_dec[:, g * gw:(g + 1) * gw]))
        y_off = bnn(cg, sp) * dec_start[:, g * gw:(g + 1) * gw]
        scores = bnt(cg, bg)
        pieces = []
        for pr in range(gw // (2 * SSD_HEAD_DIM)):
            lo = g * gw + pr * 2 * SSD_HEAD_DIM
            xp = xc[:, lo:lo + 2 * SSD_HEAD_DIM]
            acc = None
            for h, mk in ((lo // SSD_HEAD_DIM, m0), (lo // SSD_HEAD_DIM + 1, m1)):
                seg = cs[:, h:h + 1] - cs_t[h:h + 1, :]
                mat = scores * jnp.exp(jnp.where(causal, seg, -jnp.inf))
                term = bnn(mat, xp * mk)
                acc = term if acc is None else acc + term
            pieces.append(acc)
        ys.append(jnp.concatenate(pieces, axis=-1) + y_off)
    y = jnp.concatenate(ys, axis=-1) + xs * d_e
    return y, jnp.concatenate(states, axis=-1)


def _ssd_scan(xbc_act, dt, a_log, d_skip):
    t = xbc_act.shape[0]
    nc = t // CHUNK

    def body(xbc_ref, dt_ref, al_ref, dk_ref, y_ref, sp_ref, state_ref):
        c = pl.program_id(0)

        @pl.when(c == 0)
        def _():
            state_ref[...] = jnp.zeros_like(state_ref)

        s_prev = state_ref[...]
        sp_ref[...] = s_prev
        y, s_next = _ssd_chunk(xbc_ref[:, 0:1024], xbc_ref[:, 1024:1280], xbc_ref[:, 1280:1536], dt_ref[...],
                               s_prev, al_ref[...], dk_ref[...], _ssd_consts())
        y_ref[...] = y
        state_ref[...] = s_next

    return pl.pallas_call(
        body, name="ssd_scan", grid=(nc,),
        in_specs=[_row_spec(CHUNK, XBC_WIDTH), _row_spec(CHUNK, SSD_HEADS), _const_spec((1, SSD_HEADS)),
                  _const_spec((1, SSD_HEADS))],
        out_specs=[_row_spec(CHUNK, D_MODEL), pl.BlockSpec((None, SSD_STATE, D_MODEL), lambda c: (c, 0, 0))],
        out_shape=(jax.ShapeDtypeStruct((t, D_MODEL), f32), jax.ShapeDtypeStruct((nc, SSD_STATE, D_MODEL), f32)),
        scratch_shapes=[pltpu.VMEM((SSD_STATE, D_MODEL), f32)],
        compiler_params=_params("arbitrary"),
    )(xbc_act, dt, a_log, d_skip)


CONV_HALO = 32
CONV_BLOCK = 64
LANE_TILE = 128
SUBLANES = 8


def _fill_glu_ext(ext_ref, sg_ref, cv_ref, cg_ref, hcv_ref, hcg_ref, i):
    ext_ref[0:CONV_HALO, :] = jnp.where(i == 0, 0.0, hcv_ref[...] * jax.nn.sigmoid(hcg_ref[...]))
    sg = jax.nn.sigmoid(cg_ref[...])
    if sg_ref is not None:
        sg_ref[...] = sg
    ext_ref[CONV_HALO:, :] = cv_ref[...] * sg


def _shifted_copies(src_ref, dst_ref, rows):
    for s in range(1, SUBLANES):
        dst_ref[s, 0:rows, :] = src_ref[pl.ds(s, rows), :]


def _tap_source(src_ref, shifted_ref, offset):
    s = offset % SUBLANES
    return (src_ref if s == 0 else shifted_ref.at[s]), offset - s


def _conv_rows(src_ref, shifted_ref, taps, offsets, lanes, r0, init):
    accs = [init] * (CONV_BLOCK // SUBLANES)
    for k, off in enumerate(offsets):
        ref, base = _tap_source(src_ref, shifted_ref, off)
        for j in range(len(accs)):
            accs[j] = accs[j] + taps[k] * ref[pl.ds(r0 + base + SUBLANES * j, SUBLANES), lanes]
    return accs


def _conf_conv(cv, cg, w, b):
    t = cv.shape[0]
    tm = min(TOKEN_TILE, t)
    ext_rows = tm + CONV_HALO
    offsets = [CONV_HALO - (CONF_KERNEL - 1) + k for k in range(CONF_KERNEL)]

    def body(cv_ref, cg_ref, hcv_ref, hcg_ref, w_ref, b_ref, o_ref, ext_ref, shifted_ref):
        _fill_glu_ext(ext_ref, None, cv_ref, cg_ref, hcv_ref, hcg_ref, pl.program_id(0))
        _shifted_copies(ext_ref, shifted_ref, ext_rows - SUBLANES)
        for lb in range(1024 // LANE_TILE):
            lanes = slice(lb * LANE_TILE, (lb + 1) * LANE_TILE)
            taps = [jnp.broadcast_to(w_ref[k:k + 1, lanes], (SUBLANES, LANE_TILE)) for k in range(CONF_KERNEL)]
            bias = jnp.broadcast_to(b_ref[:, lanes], (SUBLANES, LANE_TILE))

            def rows(rb, carry, lanes=lanes, taps=taps, bias=bias):
                r0 = pl.multiple_of(rb * CONV_BLOCK, CONV_BLOCK)
                accs = _conv_rows(ext_ref, shifted_ref, taps, offsets, lanes, r0, bias)
                for j, a in enumerate(accs):
                    o_ref[pl.ds(r0 + SUBLANES * j, SUBLANES), lanes] = a
                return carry

            lax.fori_loop(0, tm // CONV_BLOCK, rows, 0)

    return pl.pallas_call(
        body, name="conf_conv", grid=(t // tm,),
        in_specs=[_row_spec(tm, 1024), _row_spec(tm, 1024), _prev_halo_spec(tm, CONV_HALO, 1024),
                  _prev_halo_spec(tm, CONV_HALO, 1024), _const_spec((32, 1024)), _const_spec((1, 1024))],
        out_specs=_row_spec(tm, 1024), out_shape=jax.ShapeDtypeStruct((t, 1024), f32),
        scratch_shapes=[pltpu.VMEM((ext_rows, 1024), f32), pltpu.VMEM((SUBLANES, ext_rows, 1024), f32)],
        compiler_params=_params("arbitrary"),
    )(cv, cg, cv, cg, w, b)


def _out_proj(y, z, v2, x, g_ssd, ln_g, ln_b, w_out, g2):
    t = x.shape[0]
    tm = min(TOKEN_TILE, t)

    def body(y_ref, z_ref, v_ref, x_ref, gs_ref, lg_ref, lb_ref, w_ref, g2_ref, ys_ref, yc_ref, h1_ref, u2_ref):
        ys = _gated_norm(y_ref[...], z_ref[...], gs_ref[...]).astype(bf16)
        yc = _ln_silu(v_ref[...], lg_ref[...], lb_ref[...]).astype(bf16)
        ys_ref[...] = ys
        yc_ref[...] = yc
        h1 = x_ref[...] + _nn(ys, w_ref[0:1024, :]) + _nn(yc, w_ref[1024:2048, :])
        h1_ref[...] = h1
        u2_ref[...] = _rms(h1, g2_ref[...]).astype(bf16)

    vec = _const_spec((1, 1024))
    row = _row_spec(tm, 1024)
    return pl.pallas_call(
        body, name="out_proj", grid=(t // tm,),
        in_specs=[row, row, row, row, vec, vec, vec, _const_spec((2048, 1024)), vec],
        out_specs=[row, row, row, row],
        out_shape=(jax.ShapeDtypeStruct((t, 1024), bf16), jax.ShapeDtypeStruct((t, 1024), bf16),
                   jax.ShapeDtypeStruct((t, 1024), f32), jax.ShapeDtypeStruct((t, 1024), bf16)),
        compiler_params=_params("arbitrary"),
    )(y, z, v2, x, g_ssd, ln_g, ln_b, w_out, g2)


def _mlp_up(u2, w_up):
    t = u2.shape[0]
    tm = min(TOKEN_TILE, t)
    blk = D_FF // N_DEV

    def body(u_ref, w_ref, pre_ref, hs_ref):
        u = u_ref[...]
        for k in range(N_DEV):
            pre = _nn(u, w_ref[k])
            pre_ref[:, k * blk:(k + 1) * blk] = pre.astype(bf16)
            r = jnp.maximum(pre, 0.0)
            hs_ref[:, k * blk:(k + 1) * blk] = (r * r).astype(bf16)

    return pl.pallas_call(
        body, name="mlp_up", grid=(t // tm,),
        in_specs=[_row_spec(tm, D_MODEL), _const_spec((N_DEV, D_MODEL, D_FF // N_DEV))],
        out_specs=[_row_spec(tm, D_FF), _row_spec(tm, D_FF)],
        out_shape=(jax.ShapeDtypeStruct((t, D_FF), bf16), jax.ShapeDtypeStruct((t, D_FF), bf16)),
        compiler_params=_params("arbitrary"),
    )(u2, w_up)


def _mlp_down(h1, hs, w_down):
    t = h1.shape[0]
    tm = min(TOKEN_TILE, t)

    def body(h_ref, hs_ref, w_ref, o_ref):
        o_ref[...] = h_ref[...] + _nn(hs_ref[...], w_ref[...])

    return pl.pallas_call(
        body, name="mlp_down", grid=(t // tm,),
        in_specs=[_row_spec(tm, D_MODEL), _row_spec(tm, D_FF), _const_spec((D_FF, D_MODEL))],
        out_specs=_row_spec(tm, D_MODEL), out_shape=jax.ShapeDtypeStruct((t, D_MODEL), f32),
        compiler_params=_params("arbitrary"),
    )(h1, hs, w_down)


TAIL_LOSS, TAIL_FINAL_G, TAIL_PLE_G, TAIL_GATE_B, TAIL_GATE_NORM_G = 0, 1, 2, 3, 4


def _tail(h2, p, target, g3, w_gate, b_gate, w_ple, ple_g, fin_g):
    t = h2.shape[0]
    tm = min(TOKEN_TILE, t)

    def body(h_ref, p_ref, t_ref, g3_ref, wg_ref, bg_ref, wp_ref, pg_ref, fg_ref,
             dh_ref, dhb_ref, dwg_ref, dwp_ref, acc_ref):
        i = pl.program_id(0)
        _acc_init(i, dwg_ref, dwp_ref, acc_ref)
        h2v = h_ref[...]
        tgt = t_ref[...]
        u3, vjp_u3 = jax.vjp(_rms, h2v, g3_ref[...])
        u3b = u3.astype(bf16)
        pb = p_ref[...].astype(bf16)
        gate_pre = _nn(u3b, wg_ref[...]) + bg_ref[...]
        emb_pre = _nn(pb, wp_ref[...])

        def tail_fn(hv, gp, ep, pg, fg):
            h3 = hv + jax.nn.sigmoid(gp) * _rms(ep, pg)
            err = _rms(h3, fg) - tgt
            return 0.5 * jnp.mean(err * err, axis=-1, keepdims=True)

        loss_tok, vjp_tail = jax.vjp(tail_fn, h2v, gate_pre, emb_pre, pg_ref[...], fg_ref[...])
        dh_a, dgp, dep, dpg, dfg = vjp_tail(jnp.ones_like(loss_tok))
        dgpb = dgp.astype(bf16)
        dh_b, dg3 = vjp_u3(_nt(dgpb, wg_ref[...]))
        dh = dh_a + dh_b
        dh_ref[...] = dh
        dhb_ref[...] = dh.astype(bf16)
        dwg_ref[...] += _tn(u3b, dgpb)
        dwp_ref[...] += _tn(pb, dep.astype(bf16))
        acc_ref[TAIL_LOSS:TAIL_LOSS + 1, :] += jnp.broadcast_to(jnp.sum(loss_tok, axis=0, keepdims=True), (1, 1024))
        acc_ref[TAIL_FINAL_G:TAIL_FINAL_G + 1, :] += dfg
        acc_ref[TAIL_PLE_G:TAIL_PLE_G + 1, :] += dpg
        acc_ref[TAIL_GATE_B:TAIL_GATE_B + 1, :] += jnp.sum(dgp, axis=0, keepdims=True)
        acc_ref[TAIL_GATE_NORM_G:TAIL_GATE_NORM_G + 1, :] += dg3

    vec = _const_spec((1, 1024))
    row = _row_spec(tm, 1024)
    return pl.pallas_call(
        body, name="tail", grid=(t // tm,),
        in_specs=[row, _row_spec(tm, PLE_DIM), row, vec, _const_spec((1024, 1024)), vec, _const_spec((PLE_DIM, 1024)), vec, vec],
        out_specs=[row, row, _const_spec((1024, 1024)), _const_spec((PLE_DIM, 1024)), _const_spec((8, 1024))],
        out_shape=(jax.ShapeDtypeStruct((t, 1024), f32), jax.ShapeDtypeStruct((t, 1024), bf16),
                   jax.ShapeDtypeStruct((1024, 1024), f32), jax.ShapeDtypeStruct((PLE_DIM, 1024), f32),
                   jax.ShapeDtypeStruct((8, 1024), f32)),
        compiler_params=_params("arbitrary"),
    )(h2, p, target, g3, w_gate, b_gate, w_ple, ple_g, fin_g)


def _mlp_bwd(dh2, dh2b, pre, h1, g2, w_down, w_up, token=None):
    t = dh2.shape[0]
    tm = min(TOKEN_TILE, t)
    blk = D_FF // N_DEV

    def body(dh_ref, dhb_ref, pre_ref, h1_ref, g_ref, wd_hbm, wu_hbm, dpre_ref, dh1_ref, dh1b_ref, acc_ref,
             wd_ref, wu_ref):
        i = pl.program_id(0)
        _acc_init(i, acc_ref)

        @pl.when(i == 0)
        def _():
            pltpu.sync_copy(wd_hbm, wd_ref)
            pltpu.sync_copy(wu_hbm, wu_ref)

        dhb = dhb_ref[...]
        du2 = jnp.zeros((tm, D_MODEL), f32)
        for k in range(D_FF // blk):
            dhs = _nt(dhb, wd_ref[k * blk:(k + 1) * blk, :])
            dpre = (dhs * (2.0 * jnp.maximum(pre_ref[:, k * blk:(k + 1) * blk].astype(f32), 0.0))).astype(bf16)
            dpre_ref[:, k * blk:(k + 1) * blk] = dpre
            du2 = du2 + _nt(dpre, wu_ref[k])
        _, vjp_u2 = jax.vjp(_rms, h1_ref[...], g_ref[...])
        d, dg = vjp_u2(du2)
        dh1 = dh_ref[...] + d
        dh1_ref[...] = dh1
        dh1b_ref[...] = dh1.astype(bf16)
        acc_ref[0:1, :] += dg

    row = _row_spec(tm, 1024)
    body, tok, tok_spec = _after_token(body, token)
    return pl.pallas_call(
        body, name="mlp_bwd", grid=(t // tm,),
        in_specs=tok_spec + [row, row, _row_spec(tm, D_FF), row, _const_spec((1, 1024)), HBM_SPEC, HBM_SPEC],
        out_specs=[_row_spec(tm, D_FF), row, row, _const_spec((8, 1024))],
        out_shape=(jax.ShapeDtypeStruct((t, D_FF), bf16), jax.ShapeDtypeStruct((t, 1024), f32),
                   jax.ShapeDtypeStruct((t, 1024), bf16), jax.ShapeDtypeStruct((8, 1024), f32)),
        scratch_shapes=[pltpu.VMEM((D_FF, D_MODEL), bf16), pltpu.VMEM((N_DEV, D_MODEL, D_FF // N_DEV), bf16)],
        compiler_params=_params("arbitrary"),
    )(*tok, dh2, dh2b, pre, h1, g2, w_down, w_up)


OPB_SSD_G, OPB_LN_G, OPB_LN_B, OPB_CONV_B = 0, 1, 2, 3


def _out_proj_bwd(dh1b, y, z, v2, g_ssd, ln_g, ln_b, w_out, token=None):
    t = y.shape[0]
    tm = min(TOKEN_TILE, t)

    def body(dh_ref, y_ref, z_ref, v_ref, gs_ref, lg_ref, lb_ref, w_ref, dy_ref, dz_ref, dv_ref, acc_ref):
        i = pl.program_id(0)
        _acc_init(i, acc_ref)
        dhb = dh_ref[...]
        dys = _nt(dhb, w_ref[0:1024, :])
        dyc = _nt(dhb, w_ref[1024:2048, :])
        _, vjp_g = jax.vjp(_gated_norm, y_ref[...], z_ref[...], gs_ref[...])
        dy, dz, dgs = vjp_g(dys)
        _, vjp_l = jax.vjp(_ln_silu, v_ref[...], lg_ref[...], lb_ref[...])
        dv, dlg, dlb = vjp_l(dyc)
        dy_ref[...] = dy
        dz_ref[...] = dz.astype(bf16)
        dv_ref[...] = dv
        acc_ref[OPB_SSD_G:OPB_SSD_G + 1, :] += dgs
        acc_ref[OPB_LN_G:OPB_LN_G + 1, :] += dlg
        acc_ref[OPB_LN_B:OPB_LN_B + 1, :] += dlb
        acc_ref[OPB_CONV_B:OPB_CONV_B + 1, :] += jnp.sum(dv, axis=0, keepdims=True)

    vec = _const_spec((1, 1024))
    row = _row_spec(tm, 1024)
    body, tok, tok_spec = _after_token(body, token)
    return pl.pallas_call(
        body, name="out_proj_bwd", grid=(t // tm,),
        in_specs=tok_spec + [row, row, row, row, vec, vec, vec, _const_spec((2048, 1024))],
        out_specs=[row, row, row, _const_spec((8, 1024))],
        out_shape=(jax.ShapeDtypeStruct((t, 1024), f32), jax.ShapeDtypeStruct((t, 1024), bf16),
                   jax.ShapeDtypeStruct((t, 1024), f32), jax.ShapeDtypeStruct((8, 1024), f32)),
        compiler_params=_params("arbitrary"),
    )(*tok, dh1b, y, z, v2, g_ssd, ln_g, ln_b, w_out)


def _conf_conv_bwd(dv2, cv, cg, w):
    t = cv.shape[0]
    tm = min(TOKEN_TILE, t)
    ext_rows = tm + CONV_HALO
    offsets = [CONF_KERNEL - 1 - k for k in range(CONF_KERNEL)]

    def body(dv_ref, dvn_ref, cv_ref, cg_ref, w_ref, dcv_ref, dcg_ref, dw_ref, glu_ref, dext_ref, sg_ref, shifted_ref):
        i = pl.program_id(0)
        _acc_init(i, dw_ref)
        sg = jax.nn.sigmoid(cg_ref[...])
        sg_ref[...] = sg
        glu_ref[...] = cv_ref[...] * sg
        dext_ref[0:tm, :] = dv_ref[...]
        dext_ref[tm:, :] = jnp.where(i == pl.num_programs(0) - 1, 0.0, dvn_ref[...])
        _shifted_copies(dext_ref, shifted_ref, ext_rows - SUBLANES)

        for lb in range(1024 // LANE_TILE):
            lanes = slice(lb * LANE_TILE, (lb + 1) * LANE_TILE)

            def rows_w(rb, accs, lanes=lanes):
                r0 = pl.multiple_of(rb * CONV_BLOCK, CONV_BLOCK)
                accs = list(accs)
                for j in range(CONV_BLOCK // SUBLANES):
                    x = glu_ref[pl.ds(r0 + SUBLANES * j, SUBLANES), lanes]
                    for k, off in enumerate(offsets):
                        ref, base = _tap_source(dext_ref, shifted_ref, off)
                        accs[k] = accs[k] + x * ref[pl.ds(r0 + base + SUBLANES * j, SUBLANES), lanes]
                return tuple(accs)

            zero = jnp.zeros((SUBLANES, LANE_TILE), f32)
            accs = lax.fori_loop(0, tm // CONV_BLOCK, rows_w, (zero,) * CONF_KERNEL)
            for k in range(CONF_KERNEL):
                dw_ref[k:k + 1, lanes] += jnp.sum(accs[k], axis=0, keepdims=True)

        for lb in range(1024 // LANE_TILE):
            lanes = slice(lb * LANE_TILE, (lb + 1) * LANE_TILE)
            taps = [jnp.broadcast_to(w_ref[k:k + 1, lanes], (SUBLANES, LANE_TILE)) for k in range(CONF_KERNEL)]

            def rows_x(rb, carry, lanes=lanes, taps=taps):
                r0 = pl.multiple_of(rb * CONV_BLOCK, CONV_BLOCK)
                zero = jnp.zeros((SUBLANES, LANE_TILE), f32)
                dglu = jnp.concatenate(_conv_rows(dext_ref, shifted_ref, taps, offsets, lanes, r0, zero), axis=0)
                sg = sg_ref[pl.ds(r0, CONV_BLOCK), lanes]
                cvv = cv_ref[pl.ds(r0, CONV_BLOCK), lanes]
                dcv_ref[pl.ds(r0, CONV_BLOCK), lanes] = (dglu * sg).astype(bf16)
                dcg_ref[pl.ds(r0, CONV_BLOCK), lanes] = (dglu * cvv * sg * (1.0 - sg)).astype(bf16)
                return carry

            lax.fori_loop(0, tm // CONV_BLOCK, rows_x, 0)

    row = _row_spec(tm, 1024)
    return pl.pallas_call(
        body, name="conf_conv_bwd", grid=(t // tm,),
        in_specs=[row, _next_halo_spec(tm, CONV_HALO, 1024, t), row, row, _const_spec((32, 1024))],
        out_specs=[row, row, _const_spec((32, 1024))],
        out_shape=(jax.ShapeDtypeStruct((t, 1024), bf16), jax.ShapeDtypeStruct((t, 1024), bf16),
                   jax.ShapeDtypeStruct((32, 1024), f32)),
        scratch_shapes=[pltpu.VMEM((tm, 1024), f32), pltpu.VMEM((ext_rows, 1024), f32), pltpu.VMEM((tm, 1024), f32),
                        pltpu.VMEM((SUBLANES, ext_rows, 1024), f32)],
        compiler_params=_params("arbitrary"),
    )(dv2, dv2, cv, cg, w)


def _ssd_scan_bwd(xbc_act, dt, s_prev, dy, a_log, d_skip):
    t = xbc_act.shape[0]
    nc = t // CHUNK

    def body(xbc_ref, dt_ref, sp_ref, dy_ref, al_ref, dk_ref, dxbc_ref, ddt_ref, dal_ref, ddk_ref, ds_ref):
        i = pl.program_id(0)
        _acc_init(i, ds_ref, dal_ref, ddk_ref)
        consts = _ssd_consts()
        _, vjp_c = jax.vjp(
            functools.partial(_ssd_chunk, consts=consts),
            xbc_ref[:, 0:1024], xbc_ref[:, 1024:1280], xbc_ref[:, 1280:1536], dt_ref[...], sp_ref[...],
            al_ref[...], dk_ref[...])
        dxs, dbm, dcm, ddt, dsp, dal, ddk = vjp_c((dy_ref[...], ds_ref[...]))
        dxbc_ref[:, 0:1024] = dxs
        dxbc_ref[:, 1024:1280] = dbm
        dxbc_ref[:, 1280:1536] = dcm
        ddt_ref[...] = ddt
        ds_ref[...] = dsp
        dal_ref[...] += dal
        ddk_ref[...] += ddk

    rev = lambda i: (nc - 1 - i, 0)
    return pl.pallas_call(
        body, name="ssd_scan_bwd", grid=(nc,),
        in_specs=[pl.BlockSpec((CHUNK, XBC_WIDTH), rev), pl.BlockSpec((CHUNK, SSD_HEADS), rev),
                  pl.BlockSpec((None, SSD_STATE, D_MODEL), lambda i: (nc - 1 - i, 0, 0)),
                  pl.BlockSpec((CHUNK, D_MODEL), rev), _const_spec((1, SSD_HEADS)), _const_spec((1, SSD_HEADS))],
        out_specs=[pl.BlockSpec((CHUNK, XBC_WIDTH), rev), pl.BlockSpec((CHUNK, SSD_HEADS), rev),
                   _const_spec((1, SSD_HEADS)), _const_spec((1, SSD_HEADS))],
        out_shape=(jax.ShapeDtypeStruct((t, XBC_WIDTH), f32), jax.ShapeDtypeStruct((t, SSD_HEADS), f32),
                   jax.ShapeDtypeStruct((1, SSD_HEADS), f32), jax.ShapeDtypeStruct((1, SSD_HEADS), f32)),
        scratch_shapes=[pltpu.VMEM((SSD_STATE, D_MODEL), f32)],
        compiler_params=_params("arbitrary"),
    )(xbc_act, dt, s_prev, dy, a_log, d_skip)


def _ssd_pre_bwd(xbc_raw, dxbc_act, ddt, dt_raw, cw, cb, dt_bias):
    t = xbc_raw.shape[0]
    tm = min(TOKEN_TILE, t)

    def body(cur_ref, halo_ref, dact_ref, ddt_ref, dtr_ref, w_ref, b_ref, dtb_ref,
             dco_ref, ddtr_ref, dw_ref, db_ref, ddtb_ref, ext_ref):
        i = pl.program_id(0)
        _acc_init(i, dw_ref, db_ref, ddtb_ref)
        ext_ref[0:8, :] = jnp.where(i == 0, 0.0, halo_ref[...])
        ext_ref[8:, :] = cur_ref[...]
        fold = lambda a: a[0:8] + a[8:ROW_BLOCK]
        for c in range(0, XBC_WIDTH, LANE_CHUNK):
            ls = slice(c, c + LANE_CHUNK)
            part_b = jnp.zeros((8, LANE_CHUNK), f32)
            part_w = [jnp.zeros((8, LANE_CHUNK), f32)] * SSD_CONV
            for r in range(0, tm, ROW_BLOCK):
                rs = slice(r, r + ROW_BLOCK)
                co = _conv4_block(ext_ref, w_ref, b_ref, rs, ls)
                sg = jax.nn.sigmoid(co)
                dco = dact_ref[rs, ls] * sg * (1.0 + co * (1.0 - sg))
                dco_ref[rs, ls] = dco
                part_b = part_b + fold(dco)
                part_w = [pw + fold(dco * ext_ref[r + 8 - 3 + k:r + ROW_BLOCK + 8 - 3 + k, ls])
                          for k, pw in enumerate(part_w)]
            db_ref[:, ls] += jnp.sum(part_b, axis=0, keepdims=True)
            for k in range(SSD_CONV):
                dw_ref[k:k + 1, ls] += jnp.sum(part_w[k], axis=0, keepdims=True)
        ddtr = ddt_ref[...] * jax.nn.sigmoid(dtr_ref[...] + dtb_ref[...])
        ddtb_ref[...] += jnp.sum(ddtr, axis=0, keepdims=True)
        ddtr_ref[...] = ddtr.astype(bf16)

    return pl.pallas_call(
        body, name="ssd_pre_bwd", grid=(t // tm,),
        in_specs=[_row_spec(tm, XBC_WIDTH), _prev_halo_spec(tm, 8, XBC_WIDTH), _row_spec(tm, XBC_WIDTH),
                  _row_spec(tm, SSD_HEADS), _row_spec(tm, SSD_HEADS), _const_spec((8, XBC_WIDTH)),
                  _const_spec((1, XBC_WIDTH)), _const_spec((1, SSD_HEADS))],
        out_specs=[_row_spec(tm, XBC_WIDTH), _row_spec(tm, SSD_HEADS), _const_spec((8, XBC_WIDTH)),
                   _const_spec((1, XBC_WIDTH)), _const_spec((1, SSD_HEADS))],
        out_shape=(jax.ShapeDtypeStruct((t, XBC_WIDTH), f32), jax.ShapeDtypeStruct((t, SSD_HEADS), bf16),
                   jax.ShapeDtypeStruct((8, XBC_WIDTH), f32), jax.ShapeDtypeStruct((1, XBC_WIDTH), f32),
                   jax.ShapeDtypeStruct((1, SSD_HEADS), f32)),
        scratch_shapes=[pltpu.VMEM((tm + 8, XBC_WIDTH), f32)],
        compiler_params=_params("arbitrary"),
    )(xbc_raw, xbc_raw, dxbc_act, ddt, dt_raw, cw, cb, dt_bias)


def _conv4_bwd_data(dco, cw):
    t = dco.shape[0]
    tm = min(TOKEN_TILE, t)

    def body(cur_ref, nxt_ref, w_ref, o_ref, ext_ref):
        i = pl.program_id(0)
        ext_ref[0:tm, :] = cur_ref[...]
        ext_ref[tm:, :] = jnp.where(i == pl.num_programs(0) - 1, 0.0, nxt_ref[...])
        for rs, ls in _blocks(tm, XBC_WIDTH):
            acc = w_ref[0:1, ls] * ext_ref[rs.start + SSD_CONV - 1:rs.stop + SSD_CONV - 1, ls]
            for k in range(1, SSD_CONV):
                acc = acc + w_ref[k:k + 1, ls] * ext_ref[rs.start + SSD_CONV - 1 - k:rs.stop + SSD_CONV - 1 - k, ls]
            o_ref[rs, ls] = acc.astype(bf16)

    return pl.pallas_call(
        body, name="conv4_bwd_data", grid=(t // tm,),
        in_specs=[_row_spec(tm, XBC_WIDTH), _next_halo_spec(tm, 8, XBC_WIDTH, t), _const_spec((8, XBC_WIDTH))],
        out_specs=_row_spec(tm, XBC_WIDTH), out_shape=jax.ShapeDtypeStruct((t, XBC_WIDTH), bf16),
        scratch_shapes=[pltpu.VMEM((tm + 8, XBC_WIDTH), f32)],
        compiler_params=_params("arbitrary"),
    )(dco, dco, cw)


def _in_proj_bwd(dproj, x, dh1, g1, w_in_p, token=None):
    t = x.shape[0]
    tm = min(TOKEN_TILE, t)

    def body(dz_ref, dxbc_ref, ddt_ref, dcv_ref, dcg_ref, x_ref, dh_ref, g_ref, w_ref, dx_ref, acc_ref):
        i = pl.program_id(0)
        _acc_init(i, acc_ref)
        du = jnp.zeros((tm, D_MODEL), f32)
        for (lo, hi), r in zip(IN_SEGS, (dz_ref, dxbc_ref, ddt_ref, dcv_ref, dcg_ref)):
            du = du + _nn(r[...], w_ref[lo:hi, :])
        _, vjp_u = jax.vjp(_rms, x_ref[...], g_ref[...])
        d, dg = vjp_u(du)
        dx_ref[...] = dh_ref[...] + d
        acc_ref[0:1, :] += dg

    row = _row_spec(tm, 1024)
    body, tok, tok_spec = _after_token(body, token)
    return pl.pallas_call(
        body, name="in_proj_bwd", grid=(t // tm,),
        in_specs=tok_spec + [_row_spec(tm, hi - lo) for lo, hi in IN_SEGS] + [row, row, _const_spec((1, 1024)),
                                                                               _const_spec((IN_WIDTH, D_MODEL))],
        out_specs=[row, _const_spec((8, 1024))],
        out_shape=(jax.ShapeDtypeStruct((t, 1024), f32), jax.ShapeDtypeStruct((8, 1024), f32)),
        compiler_params=_params("arbitrary"),
    )(*tok, *dproj, x, dh1, g1, w_in_p)


def _mm_tn(a, b, tk, tn, name, column_blocks=False):
    t, kk = a.shape
    n = b.shape[1]
    tk, tn = min(tk, kk), min(tn, n)

    def body(a_ref, b_ref, o_ref):
        o_ref[...] = _tn(a_ref[...], b_ref[...]).astype(bf16)

    if column_blocks:
        assert tk == kk
        out_spec = pl.BlockSpec((None, tk, tn), lambda i, j: (j, 0, 0))
        out_shape = jax.ShapeDtypeStruct((n // tn, kk, tn), bf16)
    else:
        out_spec = pl.BlockSpec((tk, tn), lambda i, j: (i, j))
        out_shape = jax.ShapeDtypeStruct((kk, n), bf16)
    return pl.pallas_call(
        body, name=name, grid=(kk // tk, n // tn),
        in_specs=[pl.BlockSpec((t, tk), lambda i, j: (0, i)), pl.BlockSpec((t, tn), lambda i, j: (0, j))],
        out_specs=out_spec, out_shape=out_shape,
        compiler_params=_params("arbitrary", "arbitrary"),
    )(a, b)


class _LocalWeights:
    def __init__(self, w):
        self.w = w
        self.sent = {}

    def start_token(self):
        return None

    def weight(self, name, after=()):
        del after
        return self.w[name]

    def send_grads(self, grads):
        self.sent.update(grads)
        return None


def _local_step(x, p, target, s, comm):
    conv_w, conf_w = comm.weight("ssd_conv_w"), comm.weight("conf_dw_w")
    w_in = comm.weight("w_in")
    u, z, xbc_raw, dt_raw, cv, cg = _in_proj(x, s["mix_norm_g"], w_in, token=comm.start_token())
    xbc_act, dt = _ssd_pre(xbc_raw, dt_raw, conv_w, s["ssd_conv_b"], s["dt_bias"])
    y, s_prev = _ssd_scan(xbc_act, dt, s["A_log"], s["D_skip"])
    v2 = _conf_conv(cv, cg, conf_w, s["conf_dw_b"])
    w_out = comm.weight("w_out", after=(y, v2))
    ys, yc, h1, u2 = _out_proj(y, z, v2, x, s["ssd_norm_g"], s["conf_ln_g"], s["conf_ln_b"], w_out, s["mlp_norm_g"])
    w_up = comm.weight("w_up", after=(u2,))
    pre, hs = _mlp_up(u2, w_up)
    w_down = comm.weight("w_down", after=(hs,))
    h2 = _mlp_down(h1, hs, w_down)
    w_gate, w_ple = comm.weight("w_ple_gate", after=(h2,)), comm.weight("w_ple", after=(h2,))
    dh2, dh2b, dwg, dwp, tail_acc = _tail(h2, p, target, s["ple_gate_norm_g"], w_gate, s["b_ple_gate"], w_ple,
                                          s["ple_norm_g"], s["final_norm_g"])
    token = comm.send_grads({"w_ple_gate": dwg, "w_ple": dwp})
    dpre, dh1, dh1b, mlp_acc = _mlp_bwd(dh2, dh2b, pre, h1, s["mlp_norm_g"], w_down, w_up, token=token)
    token = comm.send_grads({
        "w_down": _mm_tn(hs, dh2b, 512, 1024, "dw_down"),
        "w_up": _mm_tn(u2, dpre, 1024, D_FF // N_DEV, "dw_up", column_blocks=True),
        "w_out": [_mm_tn(ys, dh1b, 1024, 512, "dw_out_ssd"), _mm_tn(yc, dh1b, 1024, 512, "dw_out_conf")],
    })
    dy, dz, dv2, opb_acc = _out_proj_bwd(dh1b, y, z, v2, s["ssd_norm_g"], s["conf_ln_g"], s["conf_ln_b"], w_out,
                                         token=token)
    dcv, dcg, dconf_w = _conf_conv_bwd(dv2, cv, cg, conf_w)
    dxbc_act, ddt, d_alog, d_dskip = _ssd_scan_bwd(xbc_act, dt, s_prev, dy, s["A_log"], s["D_skip"])
    dco, ddt_raw, dconv_w, dconv_b, d_dtb = _ssd_pre_bwd(xbc_raw, dxbc_act, ddt, dt_raw, conv_w, s["ssd_conv_b"],
                                                        s["dt_bias"])
    dxbc_raw = _conv4_bwd_data(dco, conv_w)
    dproj = (dz, dxbc_raw, ddt_raw, dcv, dcg)
    token = comm.send_grads({"w_in": [_mm_tn(d, u, 512, 1024, "dw_in_" + n)
                                      for n, d in zip(("z", "xbc", "dt", "cv", "cg"), dproj)]})
    grad_x, inp_acc = _in_proj_bwd(dproj, x, dh1, s["mix_norm_g"], w_in, token=token)
    acc = {"in_proj": inp_acc, "out_proj": opb_acc, "mlp": mlp_acc, "tail": tail_acc, "ssd_conv_b": dconv_b,
           "dt_bias": d_dtb, "A_log": d_alog, "D_skip": d_dskip, "ssd_conv_w": dconv_w, "conf_dw_w": dconf_w}
    return grad_x, acc


def _mesh_pos():
    return lax.axis_index("x"), lax.axis_index("y"), lax.axis_index("c")


def _other_chips(x, y):
    return [(1 - x, y), (x, 1 - y), (1 - x, 1 - y)]


def _all_gather(arrs, name):
    n = len(arrs)

    def body(*refs):
        ins, outs = refs[:n], refs[n:2 * n]
        send_sems, recv_sems, local_sems = refs[2 * n:]
        x, y, c = _mesh_pos()
        me = 4 * x + 2 * y + c
        sibling = (x, y, 1 - c)
        chips = _other_chips(x, y)

        def copy(a, k, block, to, src=None):
            dst = outs[a].at[block]
            return pltpu.make_async_remote_copy(
                src_ref=dst if src is None else src, dst_ref=dst, send_sem=send_sems.at[a, k],
                recv_sem=recv_sems.at[a, k], device_id=to, device_id_type=MESH_ID)

        mine = [pltpu.make_async_copy(ins[a], outs[a].at[me], local_sems.at[a]) for a in range(n)]
        for cp in mine:
            cp.start()
        first = []
        for a in range(n):
            first.append(copy(a, 0, me, sibling, src=ins[a]))
            first += [copy(a, 1 + j, me, (px, py, c), src=ins[a]) for j, (px, py) in enumerate(chips)]
        for cp in first:
            cp.start()
        passed = []
        for j, (px, py) in enumerate(chips):
            for a in range(n):
                blk = 4 * px + 2 * py + c
                copy(a, 1 + j, blk, (x, y, c)).wait_recv()
                cp = copy(a, 4 + j, blk, sibling)
                cp.start()
                passed.append(cp)
        for a in range(n):
            copy(a, 0, 4 * x + 2 * y + (1 - c), (x, y, c)).wait_recv()
        for j, (px, py) in enumerate(chips):
            for a in range(n):
                copy(a, 4 + j, 4 * px + 2 * py + (1 - c), (x, y, c)).wait_recv()
        for cp in first + passed:
            cp.wait_send()
        for cp in mine:
            cp.wait()

    return pl.pallas_call(
        body, name=name,
        in_specs=[HBM_SPEC] * n, out_specs=[HBM_SPEC] * n,
        out_shape=[jax.ShapeDtypeStruct((N_DEV,) + a.shape, a.dtype) for a in arrs],
        scratch_shapes=[pltpu.SemaphoreType.DMA((n, 7)), pltpu.SemaphoreType.DMA((n, 7)), pltpu.SemaphoreType.DMA((n,))],
    )(*arrs)


_PEER_FLIPS = ((0, 0, 1), (1, 0, 0), (0, 1, 0), (1, 1, 0), (1, 0, 1), (0, 1, 1), (1, 1, 1))
SEM_SPEC = pl.BlockSpec(memory_space=pltpu.SEMAPHORE)
ANY_SPEC = pl.BlockSpec(memory_space=pl.ANY)


def _flip(v, d):
    return 1 - v if d else v


def _peers(x, y, c):
    out = []
    for dx, dy, dc in _PEER_FLIPS:
        px, py, pc = _flip(x, dx), _flip(y, dy), _flip(c, dc)
        out.append(((px, py, pc), 4 * px + 2 * py + pc))
    return out


def _exchange_copy(src_ref, land_ref, send_sems, recv_sems, k, peer, peer_block, my_block, by_block, outgoing):
    src = src_ref.at[peer_block] if by_block else src_ref
    dst = land_ref.at[my_block if outgoing else peer_block]
    return pltpu.make_async_remote_copy(src_ref=src, dst_ref=dst, send_sem=send_sems.at[k], recv_sem=recv_sems.at[k],
                                        device_id=peer, device_id_type=MESH_ID)


def _exchange_start(srcs, lands, by_block, name, after):
    n = len(srcs)

    def body(*refs):
        src_refs, land_refs = refs[1:1 + n], refs[1 + n:1 + 2 * n]
        outs = refs[1 + 2 * n:]
        send, recv, token = outs[:n], outs[n:2 * n], outs[4 * n]
        x, y, c = _mesh_pos()
        me = 4 * x + 2 * y + c
        for a in range(n):
            for k, (peer, blk) in enumerate(_peers(x, y, c)):
                _exchange_copy(src_refs[a], land_refs[a], send[a], recv[a], k, peer, blk, me, by_block, True).start()
        token[...] = jnp.zeros_like(token)

    sems = [pltpu.SemaphoreType.DMA((N_DEV - 1,))] * (2 * n)
    thru = [pltpu.HBM(a.shape, a.dtype) for a in list(srcs) + list(lands)]
    res = pl.pallas_call(
        body, name=name,
        in_specs=[ANY_SPEC] + [HBM_SPEC] * (2 * n),
        out_specs=[SEM_SPEC] * (2 * n) + [HBM_SPEC] * (2 * n) + [pl.BlockSpec(memory_space=pltpu.VMEM)],
        out_shape=sems + thru + [jax.ShapeDtypeStruct((8, 128), f32)],
        input_output_aliases={1 + i: 2 * n + i for i in range(2 * n)},
        compiler_params=pltpu.CompilerParams(has_side_effects=pltpu.SideEffectType.DATAFLOW_SIDE_EFFECTING),
    )(after, *[pltpu.with_memory_space_constraint(a, pltpu.HBM) for a in list(srcs) + list(lands)])
    states = [(res[2 * n + a], res[3 * n + a], res[a], res[n + a]) for a in range(n)]
    return states, res[4 * n]


def _exchange_wait(states, by_block, name, after):
    n, na = len(states), len(after)

    def body(*refs):
        src_refs, land_refs = refs[:n], refs[n:2 * n]
        send, recv = refs[2 * n:3 * n], refs[3 * n:4 * n]
        x, y, c = _mesh_pos()
        me = 4 * x + 2 * y + c
        for a in range(n):
            for k, (peer, blk) in enumerate(_peers(x, y, c)):
                _exchange_copy(src_refs[a], land_refs[a], send[a], recv[a], k, peer, blk, me, by_block, True).wait_send()
                _exchange_copy(src_refs[a], land_refs[a], send[a], recv[a], k, peer, blk, me, by_block, False).wait_recv()

    srcs, lands = [s[0] for s in states], [s[1] for s in states]
    res = pl.pallas_call(
        body, name=name,
        in_specs=[HBM_SPEC] * (2 * n) + [SEM_SPEC] * (2 * n) + [ANY_SPEC] * na,
        out_specs=[HBM_SPEC] * (2 * n),
        out_shape=[pltpu.HBM(a.shape, a.dtype) for a in srcs + lands],
        input_output_aliases={i: i for i in range(2 * n)},
        compiler_params=pltpu.CompilerParams(has_side_effects=pltpu.SideEffectType.DATAFLOW_SIDE_EFFECTING),
    )(*srcs, *lands, *[s[2] for s in states], *[s[3] for s in states], *after)
    return list(res[n:2 * n])


def _with_own_block(own, me):
    buf = lax.empty((N_DEV,) + own.shape, own.dtype)
    return lax.dynamic_update_slice(buf, own[None], (me,) + (0,) * own.ndim)


def _adamw_math(w, g, m, v):
    m = ADAM_B1 * m + (1.0 - ADAM_B1) * g
    v = ADAM_B2 * v + (1.0 - ADAM_B2) * (g * g)
    m_hat = m / (1.0 - ADAM_B1 ** ADAM_STEP)
    v_hat = v / (1.0 - ADAM_B2 ** ADAM_STEP)
    delta = -ADAM_LR * (m_hat / (jnp.sqrt(v_hat) + ADAM_EPS) + ADAM_WD * w)
    return delta, m, v


def _adamw_big(parts, w, m, v, name):
    rows, cols = w.shape
    tr, tc = (256, cols) if rows % 256 == 0 else (rows, 256)
    nparts = parts.shape[0]

    def body(p_ref, w_ref, m_ref, v_ref, g_ref, d_ref, mo_ref, vo_ref):
        g = p_ref[0].astype(f32)
        for j in range(1, nparts):
            g = g + p_ref[j].astype(f32)
        d, mn, vn = _adamw_math(w_ref[...], g, m_ref[...], v_ref[...])
        g_ref[...] = g
        d_ref[...] = d
        mo_ref[...] = mn
        vo_ref[...] = vn

    tile = pl.BlockSpec((tr, tc), lambda i, j: (i, j))
    shp = jax.ShapeDtypeStruct((rows, cols), f32)
    return pl.pallas_call(
        body, name=name, grid=(rows // tr, cols // tc),
        in_specs=[pl.BlockSpec((nparts, tr, tc), lambda i, j: (0, i, j)), tile, tile, tile],
        out_specs=[tile, tile, tile, tile], out_shape=(shp, shp, shp, shp),
        compiler_params=_params("arbitrary", "arbitrary"),
    )(parts, w, m, v)


PACK_ROWS = 56
_PACK_AT = {
    "mix_norm_g": (0, 0, 1024), "ssd_norm_g": (1, 0, 1024), "conf_ln_g": (2, 0, 1024), "conf_ln_b": (3, 0, 1024),
    "conf_dw_b": (4, 0, 1024), "mlp_norm_g": (5, 0, 1024), "final_norm_g": (7, 0, 1024), "ple_norm_g": (8, 0, 1024),
    "b_ple_gate": (9, 0, 1024), "ple_gate_norm_g": (10, 0, 1024), "dt_bias": (13, 0, 16), "A_log": (13, 128, 16),
    "D_skip": (13, 256, 16),
}
PACK_LOSS_ROW = 6
PACK_CONV_B_ROW = 11
PACK_CONV_W_ROW = 14
PACK_CONF_W_ROW = 24


def _pack_small(acc):
    def body(inp, opb, mlp, tail, cb, dtb, alog, dskip, cw, fw, o_ref):
        o_ref[...] = jnp.zeros_like(o_ref)
        rows = {"mix_norm_g": inp[0:1, :], "mlp_norm_g": mlp[0:1, :],
                "ssd_norm_g": opb[OPB_SSD_G:OPB_SSD_G + 1, :], "conf_ln_g": opb[OPB_LN_G:OPB_LN_G + 1, :],
                "conf_ln_b": opb[OPB_LN_B:OPB_LN_B + 1, :], "conf_dw_b": opb[OPB_CONV_B:OPB_CONV_B + 1, :],
                "final_norm_g": tail[TAIL_FINAL_G:TAIL_FINAL_G + 1, :], "ple_norm_g": tail[TAIL_PLE_G:TAIL_PLE_G + 1, :],
                "b_ple_gate": tail[TAIL_GATE_B:TAIL_GATE_B + 1, :],
                "ple_gate_norm_g": tail[TAIL_GATE_NORM_G:TAIL_GATE_NORM_G + 1, :],
                "dt_bias": dtb[...], "A_log": alog[...], "D_skip": dskip[...]}
        for name, val in rows.items():
            r, lo, width = _PACK_AT[name]
            o_ref[r:r + 1, lo:lo + width] = val
        o_ref[PACK_LOSS_ROW:PACK_LOSS_ROW + 1, :] = tail[TAIL_LOSS:TAIL_LOSS + 1, :]
        o_ref[PACK_CONV_B_ROW:PACK_CONV_B_ROW + 1, :] = cb[:, 0:1024]
        o_ref[PACK_CONV_B_ROW + 1:PACK_CONV_B_ROW + 2, 0:512] = cb[:, 1024:XBC_WIDTH]
        for k in range(SSD_CONV):
            o_ref[PACK_CONV_W_ROW + k:PACK_CONV_W_ROW + k + 1, :] = cw[k:k + 1, 0:1024]
            o_ref[PACK_CONV_W_ROW + SSD_CONV + k:PACK_CONV_W_ROW + SSD_CONV + k + 1, 0:512] = cw[k:k + 1, 1024:XBC_WIDTH]
        o_ref[PACK_CONF_W_ROW:PACK_CONF_W_ROW + 32, :] = fw[...]

    return pl.pallas_call(body, name="pack_small", out_shape=jax.ShapeDtypeStruct((PACK_ROWS, 1024), f32))(
        acc["in_proj"], acc["out_proj"], acc["mlp"], acc["tail"], acc["ssd_conv_b"], acc["dt_bias"], acc["A_log"],
        acc["D_skip"], acc["ssd_conv_w"], acc["conf_dw_w"])


def _small_update(all_small, w, m, v):
    names = _REPLICATED

    def body(all_ref, *refs):
        ins, outs = refs[:3 * len(names)], refs[3 * len(names):]
        s = all_ref[0]
        for j in range(1, N_DEV):
            s = s + all_ref[j]
        outs[0][...] = s[PACK_LOSS_ROW:PACK_LOSS_ROW + 1, 0:1]
        outs[1][...] = jnp.concatenate([s[PACK_CONV_W_ROW:PACK_CONV_W_ROW + SSD_CONV, :],
                                        s[PACK_CONV_W_ROW + SSD_CONV:PACK_CONV_W_ROW + 2 * SSD_CONV, 0:512]], axis=1)
        outs[2][...] = s[PACK_CONF_W_ROW:PACK_CONF_W_ROW + CONF_KERNEL, :]
        for i, name in enumerate(names):
            if name == "ssd_conv_b":
                g = jnp.concatenate([s[PACK_CONV_B_ROW:PACK_CONV_B_ROW + 1, :],
                                     s[PACK_CONV_B_ROW + 1:PACK_CONV_B_ROW + 2, 0:512]], axis=1)
            else:
                r, lo, width = _PACK_AT[name]
                g = s[r:r + 1, lo:lo + width]
            d, mn, vn = _adamw_math(ins[3 * i][...], g, ins[3 * i + 1][...], ins[3 * i + 2][...])
            for o_ref, val in zip(outs[3 + 4 * i:7 + 4 * i], (g, d, mn, vn)):
                o_ref[...] = val

    shapes = [jax.ShapeDtypeStruct((1, 1), f32), jax.ShapeDtypeStruct((SSD_CONV, XBC_WIDTH), f32),
              jax.ShapeDtypeStruct((CONF_KERNEL, D_MODEL), f32)]
    operands = []
    for name in names:
        operands += [w[name], m[name], v[name]]
        shapes += [jax.ShapeDtypeStruct(w[name].shape, f32)] * 4
    res = pl.pallas_call(body, name="small_update", out_shape=shapes)(all_small, *operands)
    per_name = {name: tuple(res[3 + 4 * i:7 + 4 * i]) for i, name in enumerate(names)}
    return res[0], per_name, res[1], res[2]


def _adamw_filters(g, w, m, v):
    n = len(g)

    def body(*refs):
        ins, outs = refs[:4 * n], refs[4 * n:]
        for i in range(n):
            d, mn, vn = _adamw_math(ins[4 * i + 1][...], ins[4 * i][...], ins[4 * i + 2][...], ins[4 * i + 3][...])
            for o_ref, val in zip(outs[3 * i:3 * i + 3], (d, mn, vn)):
                o_ref[...] = val

    operands, shapes = [], []
    for i in range(n):
        operands += [g[i], w[i], m[i], v[i]]
        shapes += [jax.ShapeDtypeStruct(w[i].shape, f32)] * 3
    res = pl.pallas_call(body, name="adamw_filters", out_shape=shapes)(*operands)
    return [tuple(res[3 * i:3 * i + 3]) for i in range(n)]


_REPLICATED = ("mix_norm_g", "ssd_conv_b", "dt_bias", "A_log", "D_skip", "ssd_norm_g", "conf_dw_b", "conf_ln_g",
               "conf_ln_b", "mlp_norm_g", "ple_gate_norm_g", "b_ple_gate", "ple_norm_g", "final_norm_g")
_CONV_WEIGHTS = ("ssd_conv_w", "conf_dw_w")
_BIG = ("w_in", "w_out", "w_up", "w_down", "w_ple_gate", "w_ple")
_WEIGHT_ORDER = ("mix_norm_g", "w_in", "ssd_conv_w", "ssd_conv_b", "dt_bias", "A_log", "D_skip", "ssd_norm_g", "conf_dw_w",
                 "conf_dw_b", "conf_ln_g", "conf_ln_b", "w_out", "mlp_norm_g", "w_up", "w_down", "ple_gate_norm_g",
                 "w_ple_gate", "b_ple_gate", "w_ple", "ple_norm_g", "final_norm_g")


def _blocks_of_columns(a):
    r, c8 = a.shape
    return jnp.transpose(a.reshape(r, N_DEV, c8 // N_DEV), (1, 0, 2))


def _columns_of_blocks(a):
    _, r, c = a.shape
    return jnp.transpose(a, (1, 0, 2)).reshape(r, N_DEV * c)


_LATER = ("w_out", "w_up", "w_down", "w_ple_gate", "w_ple")
_WHOLE = {
    "w_out": lambda a: a.reshape(2048, D_MODEL),
    "w_up": lambda a: a,
    "w_down": lambda a: a.reshape(D_FF, D_MODEL),
    "w_ple_gate": lambda a: a.reshape(D_MODEL, D_MODEL),
    "w_ple": _columns_of_blocks,
}
_BY_BLOCK = {
    "w_in": lambda g: jnp.concatenate(g, axis=0).reshape(N_DEV, IN_WIDTH // N_DEV, D_MODEL),
    "w_out": lambda g: jnp.concatenate(g, axis=0).reshape(N_DEV, 256, D_MODEL),
    "w_up": lambda g: g,
    "w_down": lambda g: g.reshape(N_DEV, 512, D_MODEL),
    "w_ple_gate": lambda g: g.astype(bf16).reshape(N_DEV, 128, D_MODEL),
    "w_ple": lambda g: _blocks_of_columns(g.astype(bf16)),
}


class _StepComm:
    def __init__(self, me, ready, gathers, token):
        self.me, self.ready, self.gathers, self.token = me, ready, gathers, token
        self.sent = []

    def start_token(self):
        return self.token

    def weight(self, name, after=()):
        if name not in self.ready:
            (land,) = _exchange_wait([self.gathers[name]], False, "gather_wait_" + name, list(after))
            self.ready[name] = _WHOLE[name](land)
        return self.ready[name]

    def send_grads(self, grads):
        names = list(grads)
        blocks = [_BY_BLOCK[n](grads[n]) for n in names]
        lands = [_with_own_block(lax.dynamic_index_in_dim(b, self.me, 0, keepdims=False), self.me) for b in blocks]
        states, self.token = _exchange_start(blocks, lands, True, "scatter_start_" + names[0], self.token)
        self.sent.append((names, states))
        return self.token


def kernel(x, p, mix_norm_g, w_in, ssd_conv_w, ssd_conv_b, dt_bias, A_log, D_skip, ssd_norm_g, conf_dw_w, conf_dw_b, conf_ln_g, conf_ln_b, w_out, mlp_norm_g, w_up, w_down, ple_gate_norm_g, w_ple_gate, b_ple_gate, w_ple, ple_norm_g, final_norm_g, loss_target, m_mix_norm_g, m_w_in, m_ssd_conv_w, m_ssd_conv_b, m_dt_bias, m_A_log, m_D_skip, m_ssd_norm_g, m_conf_dw_w, m_conf_dw_b, m_conf_ln_g, m_conf_ln_b, m_w_out, m_mlp_norm_g, m_w_up, m_w_down, m_ple_gate_norm_g, m_w_ple_gate, m_b_ple_gate, m_w_ple, m_ple_norm_g, m_final_norm_g, v_mix_norm_g, v_w_in, v_ssd_conv_w, v_ssd_conv_b, v_dt_bias, v_A_log, v_D_skip, v_ssd_norm_g, v_conf_dw_w, v_conf_dw_b, v_conf_ln_g, v_conf_ln_b, v_w_out, v_mlp_norm_g, v_w_up, v_w_down, v_ple_gate_norm_g, v_w_ple_gate, v_b_ple_gate, v_w_ple, v_ple_norm_g, v_final_norm_g):
    wts = dict(mix_norm_g=mix_norm_g, w_in=w_in, ssd_conv_w=ssd_conv_w, ssd_conv_b=ssd_conv_b, dt_bias=dt_bias, A_log=A_log,
               D_skip=D_skip, ssd_norm_g=ssd_norm_g, conf_dw_w=conf_dw_w, conf_dw_b=conf_dw_b, conf_ln_g=conf_ln_g,
               conf_ln_b=conf_ln_b, w_out=w_out, mlp_norm_g=mlp_norm_g, w_up=w_up, w_down=w_down,
               ple_gate_norm_g=ple_gate_norm_g, w_ple_gate=w_ple_gate, b_ple_gate=b_ple_gate, w_ple=w_ple,
               ple_norm_g=ple_norm_g, final_norm_g=final_norm_g)
    mom1 = dict(mix_norm_g=m_mix_norm_g, w_in=m_w_in, ssd_conv_w=m_ssd_conv_w, ssd_conv_b=m_ssd_conv_b, dt_bias=m_dt_bias,
                A_log=m_A_log, D_skip=m_D_skip, ssd_norm_g=m_ssd_norm_g, conf_dw_w=m_conf_dw_w, conf_dw_b=m_conf_dw_b,
                conf_ln_g=m_conf_ln_g, conf_ln_b=m_conf_ln_b, w_out=m_w_out, mlp_norm_g=m_mlp_norm_g, w_up=m_w_up,
                w_down=m_w_down, ple_gate_norm_g=m_ple_gate_norm_g, w_ple_gate=m_w_ple_gate, b_ple_gate=m_b_ple_gate,
                w_ple=m_w_ple, ple_norm_g=m_ple_norm_g, final_norm_g=m_final_norm_g)
    mom2 = dict(mix_norm_g=v_mix_norm_g, w_in=v_w_in, ssd_conv_w=v_ssd_conv_w, ssd_conv_b=v_ssd_conv_b, dt_bias=v_dt_bias,
                A_log=v_A_log, D_skip=v_D_skip, ssd_norm_g=v_ssd_norm_g, conf_dw_w=v_conf_dw_w, conf_dw_b=v_conf_dw_b,
                conf_ln_g=v_conf_ln_g, conf_ln_b=v_conf_ln_b, w_out=v_w_out, mlp_norm_g=v_mlp_norm_g, w_up=v_w_up,
                w_down=v_w_down, ple_gate_norm_g=v_ple_gate_norm_g, w_ple_gate=v_w_ple_gate, b_ple_gate=v_b_ple_gate,
                w_ple=v_w_ple, ple_norm_g=v_ple_norm_g, final_norm_g=v_final_norm_g)
    x_pos, y_pos, c_pos = _mesh_pos()
    me = 4 * x_pos + 2 * y_pos + c_pos

    first = _all_gather([wts["w_in"][0].T.astype(bf16), wts["ssd_conv_w"][0], wts["conf_dw_w"][0]], "gather_first")
    ready = {
        "w_in": first[0].reshape(IN_WIDTH, D_MODEL),
        "ssd_conv_w": jnp.pad(_columns_of_blocks(first[1]), ((0, 8 - SSD_CONV), (0, 0))),
        "conf_dw_w": jnp.pad(_columns_of_blocks(first[2]), ((0, 32 - CONF_KERNEL), (0, 0))),
    }
    shards = [wts[n][0].astype(bf16) for n in _LATER]
    states, token = _exchange_start(shards, [_with_own_block(sh, me) for sh in shards], False, "gather_start", first[1])
    comm = _StepComm(me, ready, dict(zip(_LATER, states)), token)
    small = {n: wts[n].reshape(1, -1) for n in _REPLICATED}

    grad_x, acc = _local_step(x[0], p[0, 0], loss_target[0], small, comm)

    packed = _pack_small(acc)
    (small_state,), _ = _exchange_start([packed], [_with_own_block(packed, me)], False, "small_start", comm.token)
    grads, delta, new_m, new_v = {}, {}, {}, {}

    def adamw_big(names, sent, after):
        lands = _exchange_wait(sent, True, "scatter_wait_" + names[0], after)
        for n, land in zip(names, lands):
            view = (lambda a: a[0].T) if n == "w_in" else (lambda a: a[0])
            back = (lambda a: a.T[None]) if n == "w_in" else (lambda a: a[None])
            out = _adamw_big(land, view(wts[n]), view(mom1[n]), view(mom2[n]), "adamw_" + n)
            grads[n], delta[n], new_m[n], new_v[n] = [back(a) for a in out]

    for names, sent in comm.sent[:-1]:
        adamw_big(names, sent, [grad_x])
    (all_small,) = _exchange_wait([small_state], False, "small_wait", [delta[n] for n in _LATER])
    as_row = lambda d: {n: d[n].reshape(1, -1) for n in _REPLICATED}
    loss, per_name, conv_w_sum, conf_w_sum = _small_update(all_small, as_row(wts), as_row(mom1), as_row(mom2))
    for n in _REPLICATED:
        grads[n], delta[n], new_m[n], new_v[n] = [a.reshape(wts[n].shape) for a in per_name[n]]
    filt_g = [lax.dynamic_slice_in_dim(conv_w_sum, me * 192, 192, axis=1),
              lax.dynamic_slice_in_dim(conf_w_sum, me * 128, 128, axis=1)]
    filt = _adamw_filters(filt_g, [wts[n][0] for n in _CONV_WEIGHTS], [mom1[n][0] for n in _CONV_WEIGHTS],
                          [mom2[n][0] for n in _CONV_WEIGHTS])
    for n, g, (d, mn, vn) in zip(_CONV_WEIGHTS, filt_g, filt):
        grads[n], delta[n], new_m[n], new_v[n] = g[None], d[None], mn[None], vn[None]
    adamw_big(*comm.sent[-1], [delta[n] for n in _LATER])

    return (loss.reshape(()), grad_x[None], *[grads[n] for n in _WEIGHT_ORDER], *[delta[n] for n in _WEIGHT_ORDER],
            *[new_m[n] for n in _WEIGHT_ORDER], *[new_v[n] for n in _WEIGHT_ORDER])
```

```python
import functools

import jax
import jax.numpy as jnp
from jax import lax
from jax.experimental import pallas as pl
from jax.experimental.pallas import tpu as pltpu

f32 = jnp.float32
bf16 = jnp.bfloat16

EPS = 1e-6
D_MODEL = 1024
SSD_HEADS = 16
SSD_HEAD_DIM = 64
SSD_GROUPS = 2
SSD_STATE = 128
SSD_CONV = 4
CHUNK = 128
XBC_WIDTH = 1536
CONF_KERNEL = 31
D_FF = 4096
PLE_DIM = 256
IN_WIDTH = 4624
N_DEV = 8
SEG_Z = (0, 1024)
SEG_XBC = (1024, 2560)
SEG_DT = (2560, 2576)
SEG_CV = (2576, 3600)
SEG_CG = (3600, 4624)
IN_SEGS = (SEG_Z, SEG_XBC, SEG_DT, SEG_CV, SEG_CG)

ADAM_LR = 0.001
ADAM_B1 = 0.9
ADAM_B2 = 0.999
ADAM_EPS = 1e-08
ADAM_WD = 0.01
ADAM_STEP = 10

VMEM_LIMIT_BYTES = 56 * 1024 * 1024
TOKEN_TILE = 512
SMALL_ROWS = 16

HBM_SPEC = pl.BlockSpec(memory_space=pltpu.HBM)
MESH_ID = pl.DeviceIdType.MESH


def _params(*semantics):
    return pltpu.CompilerParams(dimension_semantics=semantics, vmem_limit_bytes=VMEM_LIMIT_BYTES)


def _nn(a, b):
    return lax.dot_general(a, b, (((1,), (0,)), ((), ())), preferred_element_type=f32)


def _nt(a, b):
    return lax.dot_general(a, b, (((1,), (1,)), ((), ())), preferred_element_type=f32)


def _tn(a, b):
    return lax.dot_general(a, b, (((0,), (0,)), ((), ())), preferred_element_type=f32)


@jax.custom_vjp
def bnn(a, b):
    return _nn(a.astype(bf16), b.astype(bf16))


def _bnn_fwd(a, b):
    ab, bb = a.astype(bf16), b.astype(bf16)
    return _nn(ab, bb), (ab, bb)


def _bnn_bwd(res, g):
    ab, bb = res
    gb = g.astype(bf16)
    return _nt(gb, bb), _tn(ab, gb)


bnn.defvjp(_bnn_fwd, _bnn_bwd)


@jax.custom_vjp
def bnt(a, b):
    return _nt(a.astype(bf16), b.astype(bf16))


def _bnt_fwd(a, b):
    ab, bb = a.astype(bf16), b.astype(bf16)
    return _nt(ab, bb), (ab, bb)


def _bnt_bwd(res, g):
    ab, bb = res
    gb = g.astype(bf16)
    return _nn(gb, bb), _tn(gb, ab)


bnt.defvjp(_bnt_fwd, _bnt_bwd)


@jax.custom_vjp
def btn(a, b):
    return _tn(a.astype(bf16), b.astype(bf16))


def _btn_fwd(a, b):
    ab, bb = a.astype(bf16), b.astype(bf16)
    return _tn(ab, bb), (ab, bb)


def _btn_bwd(res, g):
    ab, bb = res
    gb = g.astype(bf16)
    return _nt(bb, gb), _nn(ab, gb)


btn.defvjp(_btn_fwd, _btn_bwd)

CUMSUM_PASSES = 3
EXPAND_PASSES = 2


def _bf16_pieces(a, passes):
    pieces, rest = [], a
    for _ in range(passes):
        piece = rest.astype(bf16)
        pieces.append(piece)
        rest = rest - piece.astype(f32)
    return pieces


def _make_dot01(form, passes):
    fwd, bwd = {"right": (lambda b, p: _nn(p, b), lambda b, g: _nt(g, b)),
                "left": (lambda b, p: _nn(b, p), lambda b, g: _tn(b, g)),
                "tn": (lambda b, p: _tn(p, b), lambda b, g: _nt(b, g))}[form]

    def run(op, b, v):
        out = None
        for piece in _bf16_pieces(v, passes):
            term = op(b, piece)
            out = term if out is None else out + term
        return out

    @jax.custom_vjp
    def product(b, v):
        return run(fwd, b, v)

    product.defvjp(lambda b, v: (run(fwd, b, v), b), lambda b, g: (jnp.zeros_like(b), run(bwd, b, g)))
    return product


_DOT01 = {(form, passes): _make_dot01(form, passes)
          for form, passes in (("left", CUMSUM_PASSES), ("tn", CUMSUM_PASSES), ("right", EXPAND_PASSES))}


def _dot01(b01, a, form, passes):
    return _DOT01[form, passes](b01.astype(bf16), a)


def _rms(x, g):
    return x * lax.rsqrt(jnp.mean(x * x, axis=-1, keepdims=True) + EPS) * g


def _gated_norm(y, z, g):
    v = y * jax.nn.silu(z)
    half = v.shape[-1] // SSD_GROUPS
    parts = []
    for k in range(SSD_GROUPS):
        vk = v[:, k * half:(k + 1) * half]
        parts.append(vk * lax.rsqrt(jnp.mean(vk * vk, axis=-1, keepdims=True) + EPS))
    return jnp.concatenate(parts, axis=-1) * g


def _ln_silu(v, g, b):
    mu = jnp.mean(v, axis=-1, keepdims=True)
    xc = v - mu
    y = xc * lax.rsqrt(jnp.mean(xc * xc, axis=-1, keepdims=True) + EPS) * g + b
    return jax.nn.silu(y)


def _acc_init(i, *refs):
    @pl.when(i == 0)
    def _():
        for r in refs:
            r[...] = jnp.zeros_like(r)


def _after_token(body, token):
    if token is None:
        return body, [], []

    def body_after(tok_ref, *refs):
        del tok_ref
        body(*refs)

    return body_after, [token], [pl.BlockSpec(memory_space=pl.ANY)]


def _row_spec(tm, n):
    return pl.BlockSpec((tm, n), lambda i: (i, 0))


def _const_spec(shape):
    nd = len(shape)
    return pl.BlockSpec(shape, lambda i: (0,) * nd)


def _prev_halo_spec(tm, halo, n):
    return pl.BlockSpec((halo, n), lambda i: (jnp.maximum(i * (tm // halo) - 1, 0), 0))


def _next_halo_spec(tm, halo, n, t):
    return pl.BlockSpec((halo, n), lambda i: (jnp.minimum((i + 1) * (tm // halo), t // halo - 1), 0))


def _in_proj(x, g1, w_in_p, token=None):
    t = x.shape[0]
    tm = min(TOKEN_TILE, t)

    def body(x_ref, g_ref, w_ref, u_ref, z_ref, xbc_ref, dt_ref, cv_ref, cg_ref):
        u = _rms(x_ref[...], g_ref[...]).astype(bf16)
        u_ref[...] = u
        for (lo, hi), o_ref in zip(IN_SEGS, (z_ref, xbc_ref, dt_ref, cv_ref, cg_ref)):
            o_ref[...] = _nt(u, w_ref[lo:hi, :])

    widths = [hi - lo for lo, hi in IN_SEGS]
    outs = [jax.ShapeDtypeStruct((t, D_MODEL), bf16)] + [jax.ShapeDtypeStruct((t, n), f32) for n in widths]
    body, tok, tok_spec = _after_token(body, token)
    return pl.pallas_call(
        body, name="in_proj", grid=(t // tm,),
        in_specs=tok_spec + [_row_spec(tm, D_MODEL), _const_spec((1, D_MODEL)), _const_spec((IN_WIDTH, D_MODEL))],
        out_specs=[_row_spec(tm, D_MODEL)] + [_row_spec(tm, n) for n in widths],
        out_shape=outs, compiler_params=_params("arbitrary"),
    )(*tok, x, g1, w_in_p)


ROW_BLOCK = 16
LANE_CHUNK = 512


def _blocks(rows, cols):
    return [(slice(r, r + ROW_BLOCK), slice(c, c + LANE_CHUNK))
            for c in range(0, cols, LANE_CHUNK) for r in range(0, rows, ROW_BLOCK)]


def _conv4_block(ext_ref, w_ref, b_ref, rs, ls):
    acc = b_ref[:, ls] + w_ref[0:1, ls] * ext_ref[rs.start + 8 - 3:rs.stop + 8 - 3, ls]
    for k in range(1, SSD_CONV):
        acc = acc + w_ref[k:k + 1, ls] * ext_ref[rs.start + 8 - 3 + k:rs.stop + 8 - 3 + k, ls]
    return acc


def _ssd_pre(xbc_raw, dt_raw, cw, cb, dt_bias):
    t = xbc_raw.shape[0]
    tm = min(TOKEN_TILE, t)

    def body(cur_ref, halo_ref, dt_ref, w_ref, b_ref, dtb_ref, act_ref, dto_ref, ext_ref):
        i = pl.program_id(0)
        ext_ref[0:8, :] = jnp.where(i == 0, 0.0, halo_ref[...])
        ext_ref[8:, :] = cur_ref[...]
        for rs, ls in _blocks(tm, XBC_WIDTH):
            act_ref[rs, ls] = jax.nn.silu(_conv4_block(ext_ref, w_ref, b_ref, rs, ls))
        dto_ref[...] = jax.nn.softplus(dt_ref[...] + dtb_ref[...])

    return pl.pallas_call(
        body, name="ssd_pre", grid=(t // tm,),
        in_specs=[_row_spec(tm, XBC_WIDTH), _prev_halo_spec(tm, 8, XBC_WIDTH), _row_spec(tm, SSD_HEADS),
                  _const_spec((8, XBC_WIDTH)), _const_spec((1, XBC_WIDTH)), _const_spec((1, SSD_HEADS))],
        out_specs=[_row_spec(tm, XBC_WIDTH), _row_spec(tm, SSD_HEADS)],
        out_shape=(jax.ShapeDtypeStruct((t, XBC_WIDTH), f32), jax.ShapeDtypeStruct((t, SSD_HEADS), f32)),
        scratch_shapes=[pltpu.VMEM((tm + 8, XBC_WIDTH), f32)],
        compiler_params=_params("arbitrary"),
    )(xbc_raw, xbc_raw, dt_raw, cw, cb, dt_bias)


def _ssd_consts():
    r = lax.broadcasted_iota(jnp.int32, (CHUNK, CHUNK), 0)
    c = lax.broadcasted_iota(jnp.int32, (CHUNK, CHUNK), 1)
    causal = r >= c
    tril = causal.astype(f32)
    triu = (r <= c).astype(f32)
    hh = lax.broadcasted_iota(jnp.int32, (SSD_HEADS, D_MODEL), 0)
    jj = lax.broadcasted_iota(jnp.int32, (SSD_HEADS, D_MODEL), 1)
    expand = (lax.shift_right_logical(jj, 6) == hh).astype(f32)
    lane = lax.broadcasted_iota(jnp.int32, (1, 2 * SSD_HEAD_DIM), 1)
    m0 = (lane < SSD_HEAD_DIM).astype(f32)
    return causal, tril, triu, expand, m0, 1.0 - m0


def _ssd_chunk(xs, bm, cm, dt, s_prev, a_log, d_skip, consts):
    causal, tril, triu, expand, m0, m1 = consts
    a = dt * (-jnp.exp(a_log))
    cs = _dot01(tril, a, "left", CUMSUM_PASSES)
    cs_t = _dot01(triu, a, "tn", CUMSUM_PASSES)
    cs_last = cs[CHUNK - 1:CHUNK, :]
    per_head = jnp.concatenate([dt, jnp.exp(cs_last - cs), jnp.exp(cs), jnp.broadcast_to(jnp.exp(cs_last), (8, SSD_HEADS)),
                                jnp.broadcast_to(d_skip, (8, SSD_HEADS))], axis=0)
    per_channel = _dot01(expand, per_head, "right", EXPAND_PASSES)
    dt_e, dec_end, dec_start = (per_channel[i * CHUNK:(i + 1) * CHUNK] for i in range(3))
    chunk_dec = per_channel[3 * CHUNK:3 * CHUNK + 1]
    d_e = per_channel[3 * CHUNK + 8:3 * CHUNK + 9]
    xc = xs * dt_e
    x_dec = xc * dec_end
    gw = D_MODEL // SSD_GROUPS
    ys, states = [], []
    for g in range(SSD_GROUPS):
        bg = bm[:, g * SSD_STATE:(g + 1) * SSD_STATE]
        cg = cm[:, g * SSD_STATE:(g + 1) * SSD_STATE]
        sp = s_prev[:, g * gw:(g + 1) * gw]
        states.append(sp * chunk_dec[:, g * gw:(g + 1) * gw] + btn(bg, x_dec[:, g * gw:(g + 1) * gw]))
        y_off = bnn(cg, sp) * dec_start[:, g * gw:(g + 1) * gw]
        scores = bnt(cg, bg)
        pieces = []
        for pr in range(gw // (2 * SSD_HEAD_DIM)):
            lo = g * gw + pr * 2 * SSD_HEAD_DIM
            xp = xc[:, lo:lo + 2 * SSD_HEAD_DIM]
            acc = None
            for h, mk in ((lo // SSD_HEAD_DIM, m0), (lo // SSD_HEAD_DIM + 1, m1)):
                seg = cs[:, h:h + 1] - cs_t[h:h + 1, :]
                mat = scores * jnp.exp(jnp.where(causal, seg, -jnp.inf))
                term = bnn(mat, xp * mk)
                acc = term if acc is None else acc + term
            pieces.append(acc)
        ys.append(jnp.concatenate(pieces, axis=-1) + y_off)
    y = jnp.concatenate(ys, axis=-1) + xs * d_e
    return y, jnp.concatenate(states, axis=-1)


def _ssd_scan(xbc_act, dt, a_log, d_skip):
    t = xbc_act.shape[0]
    nc = t // CHUNK

    def body(xbc_ref, dt_ref, al_ref, dk_ref, y_ref, sp_ref, state_ref):
        c = pl.program_id(0)

        @pl.when(c == 0)
        def _():
            state_ref[...] = jnp.zeros_like(state_ref)

        s_prev = state_ref[...]
        sp_ref[...] = s_prev
        y, s_next = _ssd_chunk(xbc_ref[:, 0:1024], xbc_ref[:, 1024:1280], xbc_ref[:, 1280:1536], dt_ref[...],
                               s_prev, al_ref[...], dk_ref[...], _ssd_consts())
        y_ref[...] = y
        state_ref[...] = s_next

    return pl.pallas_call(
        body, name="ssd_scan", grid=(nc,),
        in_specs=[_row_spec(CHUNK, XBC_WIDTH), _row_spec(CHUNK, SSD_HEADS), _const_spec((1, SSD_HEADS)),
                  _const_spec((1, SSD_HEADS))],
        out_specs=[_row_spec(CHUNK, D_MODEL), pl.BlockSpec((None, SSD_STATE, D_MODEL), lambda c: (c, 0, 0))],
        out_shape=(jax.ShapeDtypeStruct((t, D_MODEL), f32), jax.ShapeDtypeStruct((nc, SSD_STATE, D_MODEL), f32)),
        scratch_shapes=[pltpu.VMEM((SSD_STATE, D_MODEL), f32)],
        compiler_params=_params("arbitrary"),
    )(xbc_act, dt, a_log, d_skip)


CONV_HALO = 32
CONV_BLOCK = 64
LANE_TILE = 128
SUBLANES = 8


def _fill_glu_ext(ext_ref, sg_ref, cv_ref, cg_ref, hcv_ref, hcg_ref, i):
    ext_ref[0:CONV_HALO, :] = jnp.where(i == 0, 0.0, hcv_ref[...] * jax.nn.sigmoid(hcg_ref[...]))
    sg = jax.nn.sigmoid(cg_ref[...])
    if sg_ref is not None:
        sg_ref[...] = sg
    ext_ref[CONV_HALO:, :] = cv_ref[...] * sg


def _shifted_copies(src_ref, dst_ref, rows):
    for s in range(1, SUBLANES):
        dst_ref[s, 0:rows, :] = src_ref[pl.ds(s, rows), :]


def _tap_source(src_ref, shifted_ref, offset):
    s = offset % SUBLANES
    return (src_ref if s == 0 else shifted_ref.at[s]), offset - s


def _conv_rows(src_ref, shifted_ref, taps, offsets, lanes, r0, init):
    accs = [init] * (CONV_BLOCK // SUBLANES)
    for k, off in enumerate(offsets):
        ref, base = _tap_source(src_ref, shifted_ref, off)
        for j in range(len(accs)):
            accs[j] = accs[j] + taps[k] * ref[pl.ds(r0 + base + SUBLANES * j, SUBLANES), lanes]
    return accs


def _conf_conv(cv, cg, w, b):
    t = cv.shape[0]
    tm = min(TOKEN_TILE, t)
    ext_rows = tm + CONV_HALO
    offsets = [CONV_HALO - (CONF_KERNEL - 1) + k for k in range(CONF_KERNEL)]

    def body(cv_ref, cg_ref, hcv_ref, hcg_ref, w_ref, b_ref, o_ref, ext_ref, shifted_ref):
        _fill_glu_ext(ext_ref, None, cv_ref, cg_ref, hcv_ref, hcg_ref, pl.program_id(0))
        _shifted_copies(ext_ref, shifted_ref, ext_rows - SUBLANES)
        for lb in range(1024 // LANE_TILE):
            lanes = slice(lb * LANE_TILE, (lb + 1) * LANE_TILE)
            taps = [jnp.broadcast_to(w_ref[k:k + 1, lanes], (SUBLANES, LANE_TILE)) for k in range(CONF_KERNEL)]
            bias = jnp.broadcast_to(b_ref[:, lanes], (SUBLANES, LANE_TILE))

            def rows(rb, carry, lanes=lanes, taps=taps, bias=bias):
                r0 = pl.multiple_of(rb * CONV_BLOCK, CONV_BLOCK)
                accs = _conv_rows(ext_ref, shifted_ref, taps, offsets, lanes, r0, bias)
                for j, a in enumerate(accs):
                    o_ref[pl.ds(r0 + SUBLANES * j, SUBLANES), lanes] = a
                return carry

            lax.fori_loop(0, tm // CONV_BLOCK, rows, 0)

    return pl.pallas_call(
        body, name="conf_conv", grid=(t // tm,),
        in_specs=[_row_spec(tm, 1024), _row_spec(tm, 1024), _prev_halo_spec(tm, CONV_HALO, 1024),
                  _prev_halo_spec(tm, CONV_HALO, 1024), _const_spec((32, 1024)), _const_spec((1, 1024))],
        out_specs=_row_spec(tm, 1024), out_shape=jax.ShapeDtypeStruct((t, 1024), f32),
        scratch_shapes=[pltpu.VMEM((ext_rows, 1024), f32), pltpu.VMEM((SUBLANES, ext_rows, 1024), f32)],
        compiler_params=_params("arbitrary"),
    )(cv, cg, cv, cg, w, b)


def _out_proj(y, z, v2, x, g_ssd, ln_g, ln_b, w_out, g2):
    t = x.shape[0]
    tm = min(TOKEN_TILE, t)

    def body(y_ref, z_ref, v_ref, x_ref, gs_ref, lg_ref, lb_ref, w_ref, g2_ref, ys_ref, yc_ref, h1_ref, u2_ref):
        ys = _gated_norm(y_ref[...], z_ref[...], gs_ref[...]).astype(bf16)
        yc = _ln_silu(v_ref[...], lg_ref[...], lb_ref[...]).astype(bf16)
        ys_ref[...] = ys
        yc_ref[...] = yc
        h1 = x_ref[...] + _nn(ys, w_ref[0:1024, :]) + _nn(yc, w_ref[1024:2048, :])
        h1_ref[...] = h1
        u2_ref[...] = _rms(h1, g2_ref[...]).astype(bf16)

    vec = _const_spec((1, 1024))
    row = _row_spec(tm, 1024)
    return pl.pallas_call(
        body, name="out_proj", grid=(t // tm,),
        in_specs=[row, row, row, row, vec, vec, vec, _const_spec((2048, 1024)), vec],
        out_specs=[row, row, row, row],
        out_shape=(jax.ShapeDtypeStruct((t, 1024), bf16), jax.ShapeDtypeStruct((t, 1024), bf16),
                   jax.ShapeDtypeStruct((t, 1024), f32), jax.ShapeDtypeStruct((t, 1024), bf16)),
        compiler_params=_params("arbitrary"),
    )(y, z, v2, x, g_ssd, ln_g, ln_b, w_out, g2)


def _mlp_up(u2, w_up):
    t = u2.shape[0]
    tm = min(TOKEN_TILE, t)
    blk = D_FF // N_DEV

    def body(u_ref, w_ref, pre_ref, hs_ref):
        u = u_ref[...]
        for k in range(N_DEV):
            pre = _nn(u, w_ref[k])
            pre_ref[:, k * blk:(k + 1) * blk] = pre.astype(bf16)
            r = jnp.maximum(pre, 0.0)
            hs_ref[:, k * blk:(k + 1) * blk] = (r * r).astype(bf16)

    return pl.pallas_call(
        body, name="mlp_up", grid=(t // tm,),
        in_specs=[_row_spec(tm, D_MODEL), _const_spec((N_DEV, D_MODEL, D_FF // N_DEV))],
        out_specs=[_row_spec(tm, D_FF), _row_spec(tm, D_FF)],
        out_shape=(jax.ShapeDtypeStruct((t, D_FF), bf16), jax.ShapeDtypeStruct((t, D_FF), bf16)),
        compiler_params=_params("arbitrary"),
    )(u2, w_up)


def _mlp_down(h1, hs, w_down):
    t = h1.shape[0]
    tm = min(TOKEN_TILE, t)

    def body(h_ref, hs_ref, w_ref, o_ref):
        o_ref[...] = h_ref[...] + _nn(hs_ref[...], w_ref[...])

    return pl.pallas_call(
        body, name="mlp_down", grid=(t // tm,),
        in_specs=[_row_spec(tm, D_MODEL), _row_spec(tm, D_FF), _const_spec((D_FF, D_MODEL))],
        out_specs=_row_spec(tm, D_MODEL), out_shape=jax.ShapeDtypeStruct((t, D_MODEL), f32),
        compiler_params=_params("arbitrary"),
    )(h1, hs, w_down)


TAIL_LOSS, TAIL_FINAL_G, TAIL_PLE_G, TAIL_GATE_B, TAIL_GATE_NORM_G = 0, 1, 2, 3, 4


def _tail(h2, p, target, g3, w_gate, b_gate, w_ple, ple_g, fin_g):
    t = h2.shape[0]
    tm = min(TOKEN_TILE, t)

    def body(h_ref, p_ref, t_ref, g3_ref, wg_ref, bg_ref, wp_ref, pg_ref, fg_ref,
             dh_ref, dhb_ref, dwg_ref, dwp_ref, acc_ref):
        i = pl.program_id(0)
        _acc_init(i, dwg_ref, dwp_ref, acc_ref)
        h2v = h_ref[...]
        tgt = t_ref[...]
        u3, vjp_u3 = jax.vjp(_rms, h2v, g3_ref[...])
        u3b = u3.astype(bf16)
        pb = p_ref[...].astype(bf16)
        gate_pre = _nn(u3b, wg_ref[...]) + bg_ref[...]
        emb_pre = _nn(pb, wp_ref[...])

        def tail_fn(hv, gp, ep, pg, fg):
            h3 = hv + jax.nn.sigmoid(gp) * _rms(ep, pg)
            err = _rms(h3, fg) - tgt
            return 0.5 * jnp.mean(err * err, axis=-1, keepdims=True)

        loss_tok, vjp_tail = jax.vjp(tail_fn, h2v, gate_pre, emb_pre, pg_ref[...], fg_ref[...])
        dh_a, dgp, dep, dpg, dfg = vjp_tail(jnp.ones_like(loss_tok))
        dgpb = dgp.astype(bf16)
        dh_b, dg3 = vjp_u3(_nt(dgpb, wg_ref[...]))
        dh = dh_a + dh_b
        dh_ref[...] = dh
        dhb_ref[...] = dh.astype(bf16)
        dwg_ref[...] += _tn(u3b, dgpb)
        dwp_ref[...] += _tn(pb, dep.astype(bf16))
        acc_ref[TAIL_LOSS:TAIL_LOSS + 1, :] += jnp.broadcast_to(jnp.sum(loss_tok, axis=0, keepdims=True), (1, 1024))
        acc_ref[TAIL_FINAL_G:TAIL_FINAL_G + 1, :] += dfg
        acc_ref[TAIL_PLE_G:TAIL_PLE_G + 1, :] += dpg
        acc_ref[TAIL_GATE_B:TAIL_GATE_B + 1, :] += jnp.sum(dgp, axis=0, keepdims=True)
        acc_ref[TAIL_GATE_NORM_G:TAIL_GATE_NORM_G + 1, :] += dg3

    vec = _const_spec((1, 1024))
    row = _row_spec(tm, 1024)
    return pl.pallas_call(
        body, name="tail", grid=(t // tm,),
        in_specs=[row, _row_spec(tm, PLE_DIM), row, vec, _const_spec((1024, 1024)), vec, _const_spec((PLE_DIM, 1024)), vec, vec],
        out_specs=[row, row, _const_spec((1024, 1024)), _const_spec((PLE_DIM, 1024)), _const_spec((8, 1024))],
        out_shape=(jax.ShapeDtypeStruct((t, 1024), f32), jax.ShapeDtypeStruct((t, 1024), bf16),
                   jax.ShapeDtypeStruct((1024, 1024), f32), jax.ShapeDtypeStruct((PLE_DIM, 1024), f32),
                   jax.ShapeDtypeStruct((8, 1024), f32)),
        compiler_params=_params("arbitrary"),
    )(h2, p, target, g3, w_gate, b_gate, w_ple, ple_g, fin_g)


def _mlp_bwd(dh2, dh2b, pre, h1, g2, w_down, w_up, token=None):
    t = dh2.shape[0]
    tm = min(TOKEN_TILE, t)
    blk = D_FF // N_DEV

    def body(dh_ref, dhb_ref, pre_ref, h1_ref, g_ref, wd_hbm, wu_hbm, dpre_ref, dh1_ref, dh1b_ref, acc_ref,
             wd_ref, wu_ref):
        i = pl.program_id(0)
        _acc_init(i, acc_ref)

        @pl.when(i == 0)
        def _():
            pltpu.sync_copy(wd_hbm, wd_ref)
            pltpu.sync_copy(wu_hbm, wu_ref)

        dhb = dhb_ref[...]
        du2 = jnp.zeros((tm, D_MODEL), f32)
        for k in range(D_FF // blk):
            dhs = _nt(dhb, wd_ref[k * blk:(k + 1) * blk, :])
            dpre = (dhs * (2.0 * jnp.maximum(pre_ref[:, k * blk:(k + 1) * blk].astype(f32), 0.0))).astype(bf16)
            dpre_ref[:, k * blk:(k + 1) * blk] = dpre
            du2 = du2 + _nt(dpre, wu_ref[k])
        _, vjp_u2 = jax.vjp(_rms, h1_ref[...], g_ref[...])
        d, dg = vjp_u2(du2)
        dh1 = dh_ref[...] + d
        dh1_ref[...] = dh1
        dh1b_ref[...] = dh1.astype(bf16)
        acc_ref[0:1, :] += dg

    row = _row_spec(tm, 1024)
    body, tok, tok_spec = _after_token(body, token)
    return pl.pallas_call(
        body, name="mlp_bwd", grid=(t // tm,),
        in_specs=tok_spec + [row, row, _row_spec(tm, D_FF), row, _const_spec((1, 1024)), HBM_SPEC, HBM_SPEC],
        out_specs=[_row_spec(tm, D_FF), row, row, _const_spec((8, 1024))],
        out_shape=(jax.ShapeDtypeStruct((t, D_FF), bf16), jax.ShapeDtypeStruct((t, 1024), f32),
                   jax.ShapeDtypeStruct((t, 1024), bf16), jax.ShapeDtypeStruct((8, 1024), f32)),
        scratch_shapes=[pltpu.VMEM((D_FF, D_MODEL), bf16), pltpu.VMEM((N_DEV, D_MODEL, D_FF // N_DEV), bf16)],
        compiler_params=_params("arbitrary"),
    )(*tok, dh2, dh2b, pre, h1, g2, w_down, w_up)


OPB_SSD_G, OPB_LN_G, OPB_LN_B, OPB_CONV_B = 0, 1, 2, 3


def _out_proj_bwd(dh1b, y, z, v2, g_ssd, ln_g, ln_b, w_out, token=None):
    t = y.shape[0]
    tm = min(TOKEN_TILE, t)

    def body(dh_ref, y_ref, z_ref, v_ref, gs_ref, lg_ref, lb_ref, w_ref, dy_ref, dz_ref, dv_ref, acc_ref):
        i = pl.program_id(0)
        _acc_init(i, acc_ref)
        dhb = dh_ref[...]
        dys = _nt(dhb, w_ref[0:1024, :])
        dyc = _nt(dhb, w_ref[1024:2048, :])
        _, vjp_g = jax.vjp(_gated_norm, y_ref[...], z_ref[...], gs_ref[...])
        dy, dz, dgs = vjp_g(dys)
        _, vjp_l = jax.vjp(_ln_silu, v_ref[...], lg_ref[...], lb_ref[...])
        dv, dlg, dlb = vjp_l(dyc)
        dy_ref[...] = dy
        dz_ref[...] = dz.astype(bf16)
        dv_ref[...] = dv
        acc_ref[OPB_SSD_G:OPB_SSD_G + 1, :] += dgs
        acc_ref[OPB_LN_G:OPB_LN_G + 1, :] += dlg
        acc_ref[OPB_LN_B:OPB_LN_B + 1, :] += dlb
        acc_ref[OPB_CONV_B:OPB_CONV_B + 1, :] += jnp.sum(dv, axis=0, keepdims=True)

    vec = _const_spec((1, 1024))
    row = _row_spec(tm, 1024)
    body, tok, tok_spec = _after_token(body, token)
    return pl.pallas_call(
        body, name="out_proj_bwd", grid=(t // tm,),
        in_specs=tok_spec + [row, row, row, row, vec, vec, vec, _const_spec((2048, 1024))],
        out_specs=[row, row, row, _const_spec((8, 1024))],
        out_shape=(jax.ShapeDtypeStruct((t, 1024), f32), jax.ShapeDtypeStruct((t, 1024), bf16),
                   jax.ShapeDtypeStruct((t, 1024), f32), jax.ShapeDtypeStruct((8, 1024), f32)),
        compiler_params=_params("arbitrary"),
    )(*tok, dh1b, y, z, v2, g_ssd, ln_g, ln_b, w_out)


def _conf_conv_bwd(dv2, cv, cg, w):
    t = cv.shape[0]
    tm = min(TOKEN_TILE, t)
    ext_rows = tm + CONV_HALO
    offsets = [CONF_KERNEL - 1 - k for k in range(CONF_KERNEL)]

    def body(dv_ref, dvn_ref, cv_ref, cg_ref, w_ref, dcv_ref, dcg_ref, dw_ref, glu_ref, dext_ref, sg_ref, shifted_ref):
        i = pl.program_id(0)
        _acc_init(i, dw_ref)
        sg = jax.nn.sigmoid(cg_ref[...])
        sg_ref[...] = sg
        glu_ref[...] = cv_ref[...] * sg
        dext_ref[0:tm, :] = dv_ref[...]
        dext_ref[tm:, :] = jnp.where(i == pl.num_programs(0) - 1, 0.0, dvn_ref[...])
        _shifted_copies(dext_ref, shifted_ref, ext_rows - SUBLANES)

        for lb in range(1024 // LANE_TILE):
            lanes = slice(lb * LANE_TILE, (lb + 1) * LANE_TILE)

            def rows_w(rb, accs, lanes=lanes):
                r0 = pl.multiple_of(rb * CONV_BLOCK, CONV_BLOCK)
                accs = list(accs)
                for j in range(CONV_BLOCK // SUBLANES):
                    x = glu_ref[pl.ds(r0 + SUBLANES * j, SUBLANES), lanes]
                    for k, off in enumerate(offsets):
                        ref, base = _tap_source(dext_ref, shifted_ref, off)
                        accs[k] = accs[k] + x * ref[pl.ds(r0 + base + SUBLANES * j, SUBLANES), lanes]
                return tuple(accs)

            zero = jnp.zeros((SUBLANES, LANE_TILE), f32)
            accs = lax.fori_loop(0, tm // CONV_BLOCK, rows_w, (zero,) * CONF_KERNEL)
            for k in range(CONF_KERNEL):
                dw_ref[k:k + 1, lanes] += jnp.sum(accs[k], axis=0, keepdims=True)

        for lb in range(1024 // LANE_TILE):
            lanes = slice(lb * LANE_TILE, (lb + 1) * LANE_TILE)
            taps = [jnp.broadcast_to(w_ref[k:k + 1, lanes], (SUBLANES, LANE_TILE)) for k in range(CONF_KERNEL)]

            def rows_x(rb, carry, lanes=lanes, taps=taps):
                r0 = pl.multiple_of(rb * CONV_BLOCK, CONV_BLOCK)
                zero = jnp.zeros((SUBLANES, LANE_TILE), f32)
                dglu = jnp.concatenate(_conv_rows(dext_ref, shifted_ref, taps, offsets, lanes, r0, zero), axis=0)
                sg = sg_ref[pl.ds(r0, CONV_BLOCK), lanes]
                cvv = cv_ref[pl.ds(r0, CONV_BLOCK), lanes]
                dcv_ref[pl.ds(r0, CONV_BLOCK), lanes] = (dglu * sg).astype(bf16)
                dcg_ref[pl.ds(r0, CONV_BLOCK), lanes] = (dglu * cvv * sg * (1.0 - sg)).astype(bf16)
                return carry

            lax.fori_loop(0, tm // CONV_BLOCK, rows_x, 0)

    row = _row_spec(tm, 1024)
    return pl.pallas_call(
        body, name="conf_conv_bwd", grid=(t // tm,),
        in_specs=[row, _next_halo_spec(tm, CONV_HALO, 1024, t), row, row, _const_spec((32, 1024))],
        out_specs=[row, row, _const_spec((32, 1024))],
        out_shape=(jax.ShapeDtypeStruct((t, 1024), bf16), jax.ShapeDtypeStruct((t, 1024), bf16),
                   jax.ShapeDtypeStruct((32, 1024), f32)),
        scratch_shapes=[pltpu.VMEM((tm, 1024), f32), pltpu.VMEM((ext_rows, 1024), f32), pltpu.VMEM((tm, 1024), f32),
                        pltpu.VMEM((SUBLANES, ext_rows, 1024), f32)],
        compiler_params=_params("arbitrary"),
    )(dv2, dv2, cv, cg, w)


def _ssd_scan_bwd(xbc_act, dt, s_prev, dy, a_log, d_skip):
    t = xbc_act.shape[0]
    nc = t // CHUNK

    def body(xbc_ref, dt_ref, sp_ref, dy_ref, al_ref, dk_ref, dxbc_ref, ddt_ref, dal_ref, ddk_ref, ds_ref):
        i = pl.program_id(0)
        _acc_init(i, ds_ref, dal_ref, ddk_ref)
        consts = _ssd_consts()
        _, vjp_c = jax.vjp(
            functools.partial(_ssd_chunk, consts=consts),
            xbc_ref[:, 0:1024], xbc_ref[:, 1024:1280], xbc_ref[:, 1280:1536], dt_ref[...], sp_ref[...],
            al_ref[...], dk_ref[...])
        dxs, dbm, dcm, ddt, dsp, dal, ddk = vjp_c((dy_ref[...], ds_ref[...]))
        dxbc_ref[:, 0:1024] = dxs
        dxbc_ref[:, 1024:1280] = dbm
        dxbc_ref[:, 1280:1536] = dcm
        ddt_ref[...] = ddt
        ds_ref[...] = dsp
        dal_ref[...] += dal
        ddk_ref[...] += ddk

    rev = lambda i: (nc - 1 - i, 0)
    return pl.pallas_call(
        body, name="ssd_scan_bwd", grid=(nc,),
        in_specs=[pl.BlockSpec((CHUNK, XBC_WIDTH), rev), pl.BlockSpec((CHUNK, SSD_HEADS), rev),
                  pl.BlockSpec((None, SSD_STATE, D_MODEL), lambda i: (nc - 1 - i, 0, 0)),
                  pl.BlockSpec((CHUNK, D_MODEL), rev), _const_spec((1, SSD_HEADS)), _const_spec((1, SSD_HEADS))],
        out_specs=[pl.BlockSpec((CHUNK, XBC_WIDTH), rev), pl.BlockSpec((CHUNK, SSD_HEADS), rev),
                   _const_spec((1, SSD_HEADS)), _const_spec((1, SSD_HEADS))],
        out_shape=(jax.ShapeDtypeStruct((t, XBC_WIDTH), f32), jax.ShapeDtypeStruct((t, SSD_HEADS), f32),
                   jax.ShapeDtypeStruct((1, SSD_HEADS), f32), jax.ShapeDtypeStruct((1, SSD_HEADS), f32)),
        scratch_shapes=[pltpu.VMEM((SSD_STATE, D_MODEL), f32)],
        compiler_params=_params("arbitrary"),
    )(xbc_act, dt, s_prev, dy, a_log, d_skip)


def _ssd_pre_bwd(xbc_raw, dxbc_act, ddt, dt_raw, cw, cb, dt_bias):
    t = xbc_raw.shape[0]
    tm = min(TOKEN_TILE, t)

    def body(cur_ref, halo_ref, dact_ref, ddt_ref, dtr_ref, w_ref, b_ref, dtb_ref,
             dco_ref, ddtr_ref, dw_ref, db_ref, ddtb_ref, ext_ref):
        i = pl.program_id(0)
        _acc_init(i, dw_ref, db_ref, ddtb_ref)
        ext_ref[0:8, :] = jnp.where(i == 0, 0.0, halo_ref[...])
        ext_ref[8:, :] = cur_ref[...]
        fold = lambda a: a[0:8] + a[8:ROW_BLOCK]
        for c in range(0, XBC_WIDTH, LANE_CHUNK):
            ls = slice(c, c + LANE_CHUNK)
            part_b = jnp.zeros((8, LANE_CHUNK), f32)
            part_w = [jnp.zeros((8, LANE_CHUNK), f32)] * SSD_CONV
            for r in range(0, tm, ROW_BLOCK):
                rs = slice(r, r + ROW_BLOCK)
                co = _conv4_block(ext_ref, w_ref, b_ref, rs, ls)
                sg = jax.nn.sigmoid(co)
                dco = dact_ref[rs, ls] * sg * (1.0 + co * (1.0 - sg))
                dco_ref[rs, ls] = dco
                part_b = part_b + fold(dco)
                part_w = [pw + fold(dco * ext_ref[r + 8 - 3 + k:r + ROW_BLOCK + 8 - 3 + k, ls])
                          for k, pw in enumerate(part_w)]
            db_ref[:, ls] += jnp.sum(part_b, axis=0, keepdims=True)
            for k in range(SSD_CONV):
                dw_ref[k:k + 1, ls] += jnp.sum(part_w[k], axis=0, keepdims=True)
        ddtr = ddt_ref[...] * jax.nn.sigmoid(dtr_ref[...] + dtb_ref[...])
        ddtb_ref[...] += jnp.sum(ddtr, axis=0, keepdims=True)
        ddtr_ref[...] = ddtr.astype(bf16)

    return pl.pallas_call(
        body, name="ssd_pre_bwd", grid=(t // tm,),
        in_specs=[_row_spec(tm, XBC_WIDTH), _prev_halo_spec(tm, 8, XBC_WIDTH), _row_spec(tm, XBC_WIDTH),
                  _row_spec(tm, SSD_HEADS), _row_spec(tm, SSD_HEADS), _const_spec((8, XBC_WIDTH)),
                  _const_spec((1, XBC_WIDTH)), _const_spec((1, SSD_HEADS))],
        out_specs=[_row_spec(tm, XBC_WIDTH), _row_spec(tm, SSD_HEADS), _const_spec((8, XBC_WIDTH)),
                   _const_spec((1, XBC_WIDTH)), _const_spec((1, SSD_HEADS))],
        out_shape=(jax.ShapeDtypeStruct((t, XBC_WIDTH), f32), jax.ShapeDtypeStruct((t, SSD_HEADS), bf16),
                   jax.ShapeDtypeStruct((8, XBC_WIDTH), f32), jax.ShapeDtypeStruct((1, XBC_WIDTH), f32),
                   jax.ShapeDtypeStruct((1, SSD_HEADS), f32)),
        scratch_shapes=[pltpu.VMEM((tm + 8, XBC_WIDTH), f32)],
        compiler_params=_params("arbitrary"),
    )(xbc_raw, xbc_raw, dxbc_act, ddt, dt_raw, cw, cb, dt_bias)


def _conv4_bwd_data(dco, cw):
    t = dco.shape[0]
    tm = min(TOKEN_TILE, t)

    def body(cur_ref, nxt_ref, w_ref, o_ref, ext_ref):
        i = pl.program_id(0)
        ext_ref[0:tm, :] = cur_ref[...]
        ext_ref[tm:, :] = jnp.where(i == pl.num_programs(0) - 1, 0.0, nxt_ref[...])
        for rs, ls in _blocks(tm, XBC_WIDTH):
            acc = w_ref[0:1, ls] * ext_ref[rs.start + SSD_CONV - 1:rs.stop + SSD_CONV - 1, ls]
            for k in range(1, SSD_CONV):
                acc = acc + w_ref[k:k + 1, ls] * ext_ref[rs.start + SSD_CONV - 1 - k:rs.stop + SSD_CONV - 1 - k, ls]
            o_ref[rs, ls] = acc.astype(bf16)

    return pl.pallas_call(
        body, name="conv4_bwd_data", grid=(t // tm,),
        in_specs=[_row_spec(tm, XBC_WIDTH), _next_halo_spec(tm, 8, XBC_WIDTH, t), _const_spec((8, XBC_WIDTH))],
        out_specs=_row_spec(tm, XBC_WIDTH), out_shape=jax.ShapeDtypeStruct((t, XBC_WIDTH), bf16),
        scratch_shapes=[pltpu.VMEM((tm + 8, XBC_WIDTH), f32)],
        compiler_params=_params("arbitrary"),
    )(dco, dco, cw)


def _in_proj_bwd(dproj, x, dh1, g1, w_in_p, token=None):
    t = x.shape[0]
    tm = min(TOKEN_TILE, t)

    def body(dz_ref, dxbc_ref, ddt_ref, dcv_ref, dcg_ref, x_ref, dh_ref, g_ref, w_ref, dx_ref, acc_ref):
        i = pl.program_id(0)
        _acc_init(i, acc_ref)
        du = jnp.zeros((tm, D_MODEL), f32)
        for (lo, hi), r in zip(IN_SEGS, (dz_ref, dxbc_ref, ddt_ref, dcv_ref, dcg_ref)):
            du = du + _nn(r[...], w_ref[lo:hi, :])
        _, vjp_u = jax.vjp(_rms, x_ref[...], g_ref[...])
        d, dg = vjp_u(du)
        dx_ref[...] = dh_ref[...] + d
        acc_ref[0:1, :] += dg

    row = _row_spec(tm, 1024)
    body, tok, tok_spec = _after_token(body, token)
    return pl.pallas_call(
        body, name="in_proj_bwd", grid=(t // tm,),
        in_specs=tok_spec + [_row_spec(tm, hi - lo) for lo, hi in IN_SEGS] + [row, row, _const_spec((1, 1024)),
                                                                               _const_spec((IN_WIDTH, D_MODEL))],
        out_specs=[row, _const_spec((8, 1024))],
        out_shape=(jax.ShapeDtypeStruct((t, 1024), f32), jax.ShapeDtypeStruct((8, 1024), f32)),
        compiler_params=_params("arbitrary"),
    )(*tok, *dproj, x, dh1, g1, w_in_p)


def _mm_tn(a, b, tk, tn, name, column_blocks=False):
    t, kk = a.shape
    n = b.shape[1]
    tk, tn = min(tk, kk), min(tn, n)

    def body(a_ref, b_ref, o_ref):
        o_ref[...] = _tn(a_ref[...], b_ref[...]).astype(bf16)

    if column_blocks:
        assert tk == kk
        out_spec = pl.BlockSpec((None, tk, tn), lambda i, j: (j, 0, 0))
        out_shape = jax.ShapeDtypeStruct((n // tn, kk, tn), bf16)
    else:
        out_spec = pl.BlockSpec((tk, tn), lambda i, j: (i, j))
        out_shape = jax.ShapeDtypeStruct((kk, n), bf16)
    return pl.pallas_call(
        body, name=name, grid=(kk // tk, n // tn),
        in_specs=[pl.BlockSpec((t, tk), lambda i, j: (0, i)), pl.BlockSpec((t, tn), lambda i, j: (0, j))],
        out_specs=out_spec, out_shape=out_shape,
        compiler_params=_params("arbitrary", "arbitrary"),
    )(a, b)


class _LocalWeights:
    def __init__(self, w):
        self.w = w
        self.sent = {}

    def start_token(self):
        return None

    def weight(self, name, after=()):
        del after
        return self.w[name]

    def send_grads(self, grads):
        self.sent.update(grads)
        return None


def _local_step(x, p, target, s, comm):
    conv_w, conf_w = comm.weight("ssd_conv_w"), comm.weight("conf_dw_w")
    w_in = comm.weight("w_in")
    u, z, xbc_raw, dt_raw, cv, cg = _in_proj(x, s["mix_norm_g"], w_in, token=comm.start_token())
    xbc_act, dt = _ssd_pre(xbc_raw, dt_raw, conv_w, s["ssd_conv_b"], s["dt_bias"])
    y, s_prev = _ssd_scan(xbc_act, dt, s["A_log"], s["D_skip"])
    v2 = _conf_conv(cv, cg, conf_w, s["conf_dw_b"])
    w_out = comm.weight("w_out", after=(y, v2))
    ys, yc, h1, u2 = _out_proj(y, z, v2, x, s["ssd_norm_g"], s["conf_ln_g"], s["conf_ln_b"], w_out, s["mlp_norm_g"])
    w_up = comm.weight("w_up", after=(u2,))
    pre, hs = _mlp_up(u2, w_up)
    w_down = comm.weight("w_down", after=(hs,))
    h2 = _mlp_down(h1, hs, w_down)
    w_gate, w_ple = comm.weight("w_ple_gate", after=(h2,)), comm.weight("w_ple", after=(h2,))
    dh2, dh2b, dwg, dwp, tail_acc = _tail(h2, p, target, s["ple_gate_norm_g"], w_gate, s["b_ple_gate"], w_ple,
                                          s["ple_norm_g"], s["final_norm_g"])
    token = comm.send_grads({"w_ple_gate": dwg, "w_ple": dwp})
    dpre, dh1, dh1b, mlp_acc = _mlp_bwd(dh2, dh2b, pre, h1, s["mlp_norm_g"], w_down, w_up, token=token)
    token = comm.send_grads({
        "w_down": _mm_tn(hs, dh2b, 512, 1024, "dw_down"),
        "w_up": _mm_tn(u2, dpre, 1024, D_FF // N_DEV, "dw_up", column_blocks=True),
        "w_out": [_mm_tn(ys, dh1b, 1024, 512, "dw_out_ssd"), _mm_tn(yc, dh1b, 1024, 512, "dw_out_conf")],
    })
    dy, dz, dv2, opb_acc = _out_proj_bwd(dh1b, y, z, v2, s["ssd_norm_g"], s["conf_ln_g"], s["conf_ln_b"], w_out,
                                         token=token)
    dcv, dcg, dconf_w = _conf_conv_bwd(dv2, cv, cg, conf_w)
    dxbc_act, ddt, d_alog, d_dskip = _ssd_scan_bwd(xbc_act, dt, s_prev, dy, s["A_log"], s["D_skip"])
    dco, ddt_raw, dconv_w, dconv_b, d_dtb = _ssd_pre_bwd(xbc_raw, dxbc_act, ddt, dt_raw, conv_w, s["ssd_conv_b"],
                                                        s["dt_bias"])
    dxbc_raw = _conv4_bwd_data(dco, conv_w)
    dproj = (dz, dxbc_raw, ddt_raw, dcv, dcg)
    token = comm.send_grads({"w_in": [_mm_tn(d, u, 512, 1024, "dw_in_" + n)
                                      for n, d in zip(("z", "xbc", "dt", "cv", "cg"), dproj)]})
    grad_x, inp_acc = _in_proj_bwd(dproj, x, dh1, s["mix_norm_g"], w_in, token=token)
    acc = {"in_proj": inp_acc, "out_proj": opb_acc, "mlp": mlp_acc, "tail": tail_acc, "ssd_conv_b": dconv_b,
           "dt_bias": d_dtb, "A_log": d_alog, "D_skip": d_dskip, "ssd_conv_w": dconv_w, "conf_dw_w": dconf_w}
    return grad_x, acc


def _mesh_pos():
    return lax.axis_index("x"), lax.axis_index("y"), lax.axis_index("c")


def _other_chips(x, y):
    return [(1 - x, y), (x, 1 - y), (1 - x, 1 - y)]


def _all_gather(arrs, name):
    n = len(arrs)

    def body(*refs):
        ins, outs = refs[:n], refs[n:2 * n]
        send_sems, recv_sems, local_sems = refs[2 * n:]
        x, y, c = _mesh_pos()
        me = 4 * x + 2 * y + c
        sibling = (x, y, 1 - c)
        chips = _other_chips(x, y)

        def copy(a, k, block, to, src=None):
            dst = outs[a].at[block]
            return pltpu.make_async_remote_copy(
                src_ref=dst if src is None else src, dst_ref=dst, send_sem=send_sems.at[a, k],
                recv_sem=recv_sems.at[a, k], device_id=to, device_id_type=MESH_ID)

        mine = [pltpu.make_async_copy(ins[a], outs[a].at[me], local_sems.at[a]) for a in range(n)]
        for cp in mine:
            cp.start()
        first = []
        for a in range(n):
            first.append(copy(a, 0, me, sibling, src=ins[a]))
            first += [copy(a, 1 + j, me, (px, py, c), src=ins[a]) for j, (px, py) in enumerate(chips)]
        for cp in first:
            cp.start()
        passed = []
        for j, (px, py) in enumerate(chips):
            for a in range(n):
                blk = 4 * px + 2 * py + c
                copy(a, 1 + j, blk, (x, y, c)).wait_recv()
                cp = copy(a, 4 + j, blk, sibling)
                cp.start()
                passed.append(cp)
        for a in range(n):
            copy(a, 0, 4 * x + 2 * y + (1 - c), (x, y, c)).wait_recv()
        for j, (px, py) in enumerate(chips):
            for a in range(n):
                copy(a, 4 + j, 4 * px + 2 * py + (1 - c), (x, y, c)).wait_recv()
        for cp in first + passed:
            cp.wait_send()
        for cp in mine:
            cp.wait()

    return pl.pallas_call(
        body, name=name,
        in_specs=[HBM_SPEC] * n, out_specs=[HBM_SPEC] * n,
        out_shape=[jax.ShapeDtypeStruct((N_DEV,) + a.shape, a.dtype) for a in arrs],
        scratch_shapes=[pltpu.SemaphoreType.DMA((n, 7)), pltpu.SemaphoreType.DMA((n, 7)), pltpu.SemaphoreType.DMA((n,))],
    )(*arrs)


_PEER_FLIPS = ((0, 0, 1), (1, 0, 0), (0, 1, 0), (1, 1, 0), (1, 0, 1), (0, 1, 1), (1, 1, 1))
SEM_SPEC = pl.BlockSpec(memory_space=pltpu.SEMAPHORE)
ANY_SPEC = pl.BlockSpec(memory_space=pl.ANY)


def _flip(v, d):
    return 1 - v if d else v


def _peers(x, y, c):
    out = []
    for dx, dy, dc in _PEER_FLIPS:
        px, py, pc = _flip(x, dx), _flip(y, dy), _flip(c, dc)
        out.append(((px, py, pc), 4 * px + 2 * py + pc))
    return out


def _exchange_copy(src_ref, land_ref, send_sems, recv_sems, k, peer, peer_block, my_block, by_block, outgoing):
    src = src_ref.at[peer_block] if by_block else src_ref
    dst = land_ref.at[my_block if outgoing else peer_block]
    return pltpu.make_async_remote_copy(src_ref=src, dst_ref=dst, send_sem=send_sems.at[k], recv_sem=recv_sems.at[k],
                                        device_id=peer, device_id_type=MESH_ID)


def _exchange_start(srcs, by_block, name, after):
    n = len(srcs)
    lands = [lax.empty(a.shape if by_block else (N_DEV,) + a.shape, a.dtype) for a in srcs]

    def body(*refs):
        src_refs, land_refs = refs[1:1 + n], refs[1 + n:1 + 2 * n]
        outs, own_sems = refs[1 + 2 * n:-1], refs[-1]
        send, recv, token = outs[:n], outs[n:2 * n], outs[4 * n]
        x, y, c = _mesh_pos()
        me = 4 * x + 2 * y + c
        own = [pltpu.make_async_copy(src_refs[a].at[me] if by_block else src_refs[a], land_refs[a].at[me], own_sems.at[a])
               for a in range(n)]
        for a in range(n):
            for k, (peer, blk) in enumerate(_peers(x, y, c)):
                _exchange_copy(src_refs[a], land_refs[a], send[a], recv[a], k, peer, blk, me, by_block, True).start()
        for cp in own:
            cp.start()
        token[...] = jnp.zeros_like(token)
        for cp in own:
            cp.wait()

    sems = [pltpu.SemaphoreType.DMA((N_DEV - 1,))] * (2 * n)
    thru = [pltpu.HBM(a.shape, a.dtype) for a in list(srcs) + list(lands)]
    res = pl.pallas_call(
        body, name=name,
        in_specs=[ANY_SPEC] + [HBM_SPEC] * (2 * n),
        out_specs=[SEM_SPEC] * (2 * n) + [HBM_SPEC] * (2 * n) + [pl.BlockSpec(memory_space=pltpu.VMEM)],
        out_shape=sems + thru + [jax.ShapeDtypeStruct((8, 128), f32)],
        scratch_shapes=[pltpu.SemaphoreType.DMA((n,))],
        input_output_aliases={1 + i: 2 * n + i for i in range(2 * n)},
        compiler_params=pltpu.CompilerParams(has_side_effects=pltpu.SideEffectType.DATAFLOW_SIDE_EFFECTING),
    )(after, *[pltpu.with_memory_space_constraint(a, pltpu.HBM) for a in list(srcs) + list(lands)])
    states = [(res[2 * n + a], res[3 * n + a], res[a], res[n + a]) for a in range(n)]
    return states, res[4 * n]


def _exchange_wait(states, by_block, name, after):
    n, na = len(states), len(after)

    def body(*refs):
        src_refs, land_refs = refs[:n], refs[n:2 * n]
        send, recv = refs[2 * n:3 * n], refs[3 * n:4 * n]
        x, y, c = _mesh_pos()
        me = 4 * x + 2 * y + c
        for a in range(n):
            for k, (peer, blk) in enumerate(_peers(x, y, c)):
                _exchange_copy(src_refs[a], land_refs[a], send[a], recv[a], k, peer, blk, me, by_block, True).wait_send()
                _exchange_copy(src_refs[a], land_refs[a], send[a], recv[a], k, peer, blk, me, by_block, False).wait_recv()

    srcs, lands = [s[0] for s in states], [s[1] for s in states]
    res = pl.pallas_call(
        body, name=name,
        in_specs=[HBM_SPEC] * (2 * n) + [SEM_SPEC] * (2 * n) + [ANY_SPEC] * na,
        out_specs=[HBM_SPEC] * (2 * n),
        out_shape=[pltpu.HBM(a.shape, a.dtype) for a in srcs + lands],
        input_output_aliases={i: i for i in range(2 * n)},
        compiler_params=pltpu.CompilerParams(has_side_effects=pltpu.SideEffectType.DATAFLOW_SIDE_EFFECTING),
    )(*srcs, *lands, *[s[2] for s in states], *[s[3] for s in states], *after)
    return list(res[n:2 * n])


def _adamw_math(w, g, m, v):
    m = ADAM_B1 * m + (1.0 - ADAM_B1) * g
    v = ADAM_B2 * v + (1.0 - ADAM_B2) * (g * g)
    m_hat = m / (1.0 - ADAM_B1 ** ADAM_STEP)
    v_hat = v / (1.0 - ADAM_B2 ** ADAM_STEP)
    delta = -ADAM_LR * (m_hat / (jnp.sqrt(v_hat) + ADAM_EPS) + ADAM_WD * w)
    return delta, m, v


def _adamw_big(parts, w, m, v, name):
    rows, cols = w.shape
    tr, tc = (256, cols) if rows % 256 == 0 else (rows, 256)
    nparts = parts.shape[0]

    def body(p_ref, w_ref, m_ref, v_ref, g_ref, d_ref, mo_ref, vo_ref):
        g = p_ref[0].astype(f32)
        for j in range(1, nparts):
            g = g + p_ref[j].astype(f32)
        d, mn, vn = _adamw_math(w_ref[...], g, m_ref[...], v_ref[...])
        g_ref[...] = g
        d_ref[...] = d
        mo_ref[...] = mn
        vo_ref[...] = vn

    tile = pl.BlockSpec((tr, tc), lambda i, j: (i, j))
    shp = jax.ShapeDtypeStruct((rows, cols), f32)
    return pl.pallas_call(
        body, name=name, grid=(rows // tr, cols // tc),
        in_specs=[pl.BlockSpec((nparts, tr, tc), lambda i, j: (0, i, j)), tile, tile, tile],
        out_specs=[tile, tile, tile, tile], out_shape=(shp, shp, shp, shp),
        compiler_params=_params("arbitrary", "arbitrary"),
    )(parts, w, m, v)


PACK_ROWS = 56
_PACK_AT = {
    "mix_norm_g": (0, 0, 1024), "ssd_norm_g": (1, 0, 1024), "conf_ln_g": (2, 0, 1024), "conf_ln_b": (3, 0, 1024),
    "conf_dw_b": (4, 0, 1024), "mlp_norm_g": (5, 0, 1024), "final_norm_g": (7, 0, 1024), "ple_norm_g": (8, 0, 1024),
    "b_ple_gate": (9, 0, 1024), "ple_gate_norm_g": (10, 0, 1024), "dt_bias": (13, 0, 16), "A_log": (13, 128, 16),
    "D_skip": (13, 256, 16),
}
PACK_LOSS_ROW = 6
PACK_CONV_B_ROW = 11
PACK_CONV_W_ROW = 14
PACK_CONF_W_ROW = 24


def _pack_small(acc):
    def body(inp, opb, mlp, tail, cb, dtb, alog, dskip, cw, fw, o_ref):
        o_ref[...] = jnp.zeros_like(o_ref)
        rows = {"mix_norm_g": inp[0:1, :], "mlp_norm_g": mlp[0:1, :],
                "ssd_norm_g": opb[OPB_SSD_G:OPB_SSD_G + 1, :], "conf_ln_g": opb[OPB_LN_G:OPB_LN_G + 1, :],
                "conf_ln_b": opb[OPB_LN_B:OPB_LN_B + 1, :], "conf_dw_b": opb[OPB_CONV_B:OPB_CONV_B + 1, :],
                "final_norm_g": tail[TAIL_FINAL_G:TAIL_FINAL_G + 1, :], "ple_norm_g": tail[TAIL_PLE_G:TAIL_PLE_G + 1, :],
                "b_ple_gate": tail[TAIL_GATE_B:TAIL_GATE_B + 1, :],
                "ple_gate_norm_g": tail[TAIL_GATE_NORM_G:TAIL_GATE_NORM_G + 1, :],
                "dt_bias": dtb[...], "A_log": alog[...], "D_skip": dskip[...]}
        for name, val in rows.items():
            r, lo, width = _PACK_AT[name]
            o_ref[r:r + 1, lo:lo + width] = val
        o_ref[PACK_LOSS_ROW:PACK_LOSS_ROW + 1, :] = tail[TAIL_LOSS:TAIL_LOSS + 1, :]
        o_ref[PACK_CONV_B_ROW:PACK_CONV_B_ROW + 1, :] = cb[:, 0:1024]
        o_ref[PACK_CONV_B_ROW + 1:PACK_CONV_B_ROW + 2, 0:512] = cb[:, 1024:XBC_WIDTH]
        for k in range(SSD_CONV):
            o_ref[PACK_CONV_W_ROW + k:PACK_CONV_W_ROW + k + 1, :] = cw[k:k + 1, 0:1024]
            o_ref[PACK_CONV_W_ROW + SSD_CONV + k:PACK_CONV_W_ROW + SSD_CONV + k + 1, 0:512] = cw[k:k + 1, 1024:XBC_WIDTH]
        o_ref[PACK_CONF_W_ROW:PACK_CONF_W_ROW + 32, :] = fw[...]

    return pl.pallas_call(body, name="pack_small", out_shape=jax.ShapeDtypeStruct((PACK_ROWS, 1024), f32))(
        acc["in_proj"], acc["out_proj"], acc["mlp"], acc["tail"], acc["ssd_conv_b"], acc["dt_bias"], acc["A_log"],
        acc["D_skip"], acc["ssd_conv_w"], acc["conf_dw_w"])


def _small_update(all_small, w, m, v):
    names = _REPLICATED

    def body(all_ref, *refs):
        ins, outs = refs[:3 * len(names)], refs[3 * len(names):]
        s = all_ref[0]
        for j in range(1, N_DEV):
            s = s + all_ref[j]
        outs[0][...] = s[PACK_LOSS_ROW:PACK_LOSS_ROW + 1, 0:1]
        outs[1][...] = jnp.concatenate([s[PACK_CONV_W_ROW:PACK_CONV_W_ROW + SSD_CONV, :],
                                        s[PACK_CONV_W_ROW + SSD_CONV:PACK_CONV_W_ROW + 2 * SSD_CONV, 0:512]], axis=1)
        outs[2][...] = s[PACK_CONF_W_ROW:PACK_CONF_W_ROW + CONF_KERNEL, :]
        for i, name in enumerate(names):
            if name == "ssd_conv_b":
                g = jnp.concatenate([s[PACK_CONV_B_ROW:PACK_CONV_B_ROW + 1, :],
                                     s[PACK_CONV_B_ROW + 1:PACK_CONV_B_ROW + 2, 0:512]], axis=1)
            else:
                r, lo, width = _PACK_AT[name]
                g = s[r:r + 1, lo:lo + width]
            d, mn, vn = _adamw_math(ins[3 * i][...], g, ins[3 * i + 1][...], ins[3 * i + 2][...])
            for o_ref, val in zip(outs[3 + 4 * i:7 + 4 * i], (g, d, mn, vn)):
                o_ref[...] = val

    shapes = [jax.ShapeDtypeStruct((1, 1), f32), jax.ShapeDtypeStruct((SSD_CONV, XBC_WIDTH), f32),
              jax.ShapeDtypeStruct((CONF_KERNEL, D_MODEL), f32)]
    operands = []
    for name in names:
        operands += [w[name], m[name], v[name]]
        shapes += [jax.ShapeDtypeStruct(w[name].shape, f32)] * 4
    res = pl.pallas_call(body, name="small_update", out_shape=shapes)(all_small, *operands)
    per_name = {name: tuple(res[3 + 4 * i:7 + 4 * i]) for i, name in enumerate(names)}
    return res[0], per_name, res[1], res[2]


def _adamw_filters(g, w, m, v):
    n = len(g)

    def body(*refs):
        ins, outs = refs[:4 * n], refs[4 * n:]
        for i in range(n):
            d, mn, vn = _adamw_math(ins[4 * i + 1][...], ins[4 * i][...], ins[4 * i + 2][...], ins[4 * i + 3][...])
            for o_ref, val in zip(outs[3 * i:3 * i + 3], (d, mn, vn)):
                o_ref[...] = val

    operands, shapes = [], []
    for i in range(n):
        operands += [g[i], w[i], m[i], v[i]]
        shapes += [jax.ShapeDtypeStruct(w[i].shape, f32)] * 3
    res = pl.pallas_call(body, name="adamw_filters", out_shape=shapes)(*operands)
    return [tuple(res[3 * i:3 * i + 3]) for i in range(n)]


_REPLICATED = ("mix_norm_g", "ssd_conv_b", "dt_bias", "A_log", "D_skip", "ssd_norm_g", "conf_dw_b", "conf_ln_g",
               "conf_ln_b", "mlp_norm_g", "ple_gate_norm_g", "b_ple_gate", "ple_norm_g", "final_norm_g")
_CONV_WEIGHTS = ("ssd_conv_w", "conf_dw_w")
_BIG = ("w_in", "w_out", "w_up", "w_down", "w_ple_gate", "w_ple")
_WEIGHT_ORDER = ("mix_norm_g", "w_in", "ssd_conv_w", "ssd_conv_b", "dt_bias", "A_log", "D_skip", "ssd_norm_g", "conf_dw_w",
                 "conf_dw_b", "conf_ln_g", "conf_ln_b", "w_out", "mlp_norm_g", "w_up", "w_down", "ple_gate_norm_g",
                 "w_ple_gate", "b_ple_gate", "w_ple", "ple_norm_g", "final_norm_g")


def _blocks_of_columns(a):
    r, c8 = a.shape
    return jnp.transpose(a.reshape(r, N_DEV, c8 // N_DEV), (1, 0, 2))


def _columns_of_blocks(a):
    _, r, c = a.shape
    return jnp.transpose(a, (1, 0, 2)).reshape(r, N_DEV * c)


_LATER = ("w_out", "w_up", "w_down", "w_ple_gate", "w_ple")
_WHOLE = {
    "w_out": lambda a: a.reshape(2048, D_MODEL),
    "w_up": lambda a: a,
    "w_down": lambda a: a.reshape(D_FF, D_MODEL),
    "w_ple_gate": lambda a: a.reshape(D_MODEL, D_MODEL),
    "w_ple": _columns_of_blocks,
}
_BY_BLOCK = {
    "w_in": lambda g: jnp.concatenate(g, axis=0).reshape(N_DEV, IN_WIDTH // N_DEV, D_MODEL),
    "w_out": lambda g: jnp.concatenate(g, axis=0).reshape(N_DEV, 256, D_MODEL),
    "w_up": lambda g: g,
    "w_down": lambda g: g.reshape(N_DEV, 512, D_MODEL),
    "w_ple_gate": lambda g: g.astype(bf16).reshape(N_DEV, 128, D_MODEL),
    "w_ple": lambda g: _blocks_of_columns(g.astype(bf16)),
}


class _StepComm:
    def __init__(self, me, ready, gathers, token):
        self.me, self.ready, self.gathers, self.token = me, ready, gathers, token
        self.sent = []

    def start_token(self):
        return self.token

    def weight(self, name, after=()):
        if name not in self.ready:
            (land,) = _exchange_wait([self.gathers[name]], False, "gather_wait_" + name, list(after))
            self.ready[name] = _WHOLE[name](land)
        return self.ready[name]

    def send_grads(self, grads):
        names = list(grads)
        blocks = [_BY_BLOCK[n](grads[n]) for n in names]
        states, self.token = _exchange_start(blocks, True, "scatter_start_" + names[0], self.token)
        self.sent.append((names, states))
        return self.token


def kernel(x, p, mix_norm_g, w_in, ssd_conv_w, ssd_conv_b, dt_bias, A_log, D_skip, ssd_norm_g, conf_dw_w, conf_dw_b, conf_ln_g, conf_ln_b, w_out, mlp_norm_g, w_up, w_down, ple_gate_norm_g, w_ple_gate, b_ple_gate, w_ple, ple_norm_g, final_norm_g, loss_target, m_mix_norm_g, m_w_in, m_ssd_conv_w, m_ssd_conv_b, m_dt_bias, m_A_log, m_D_skip, m_ssd_norm_g, m_conf_dw_w, m_conf_dw_b, m_conf_ln_g, m_conf_ln_b, m_w_out, m_mlp_norm_g, m_w_up, m_w_down, m_ple_gate_norm_g, m_w_ple_gate, m_b_ple_gate, m_w_ple, m_ple_norm_g, m_final_norm_g, v_mix_norm_g, v_w_in, v_ssd_conv_w, v_ssd_conv_b, v_dt_bias, v_A_log, v_D_skip, v_ssd_norm_g, v_conf_dw_w, v_conf_dw_b, v_conf_ln_g, v_conf_ln_b, v_w_out, v_mlp_norm_g, v_w_up, v_w_down, v_ple_gate_norm_g, v_w_ple_gate, v_b_ple_gate, v_w_ple, v_ple_norm_g, v_final_norm_g):
    wts = dict(mix_norm_g=mix_norm_g, w_in=w_in, ssd_conv_w=ssd_conv_w, ssd_conv_b=ssd_conv_b, dt_bias=dt_bias, A_log=A_log,
               D_skip=D_skip, ssd_norm_g=ssd_norm_g, conf_dw_w=conf_dw_w, conf_dw_b=conf_dw_b, conf_ln_g=conf_ln_g,
               conf_ln_b=conf_ln_b, w_out=w_out, mlp_norm_g=mlp_norm_g, w_up=w_up, w_down=w_down,
               ple_gate_norm_g=ple_gate_norm_g, w_ple_gate=w_ple_gate, b_ple_gate=b_ple_gate, w_ple=w_ple,
               ple_norm_g=ple_norm_g, final_norm_g=final_norm_g)
    mom1 = dict(mix_norm_g=m_mix_norm_g, w_in=m_w_in, ssd_conv_w=m_ssd_conv_w, ssd_conv_b=m_ssd_conv_b, dt_bias=m_dt_bias,
                A_log=m_A_log, D_skip=m_D_skip, ssd_norm_g=m_ssd_norm_g, conf_dw_w=m_conf_dw_w, conf_dw_b=m_conf_dw_b,
                conf_ln_g=m_conf_ln_g, conf_ln_b=m_conf_ln_b, w_out=m_w_out, mlp_norm_g=m_mlp_norm_g, w_up=m_w_up,
                w_down=m_w_down, ple_gate_norm_g=m_ple_gate_norm_g, w_ple_gate=m_w_ple_gate, b_ple_gate=m_b_ple_gate,
                w_ple=m_w_ple, ple_norm_g=m_ple_norm_g, final_norm_g=m_final_norm_g)
    mom2 = dict(mix_norm_g=v_mix_norm_g, w_in=v_w_in, ssd_conv_w=v_ssd_conv_w, ssd_conv_b=v_ssd_conv_b, dt_bias=v_dt_bias,
                A_log=v_A_log, D_skip=v_D_skip, ssd_norm_g=v_ssd_norm_g, conf_dw_w=v_conf_dw_w, conf_dw_b=v_conf_dw_b,
                conf_ln_g=v_conf_ln_g, conf_ln_b=v_conf_ln_b, w_out=v_w_out, mlp_norm_g=v_mlp_norm_g, w_up=v_w_up,
                w_down=v_w_down, ple_gate_norm_g=v_ple_gate_norm_g, w_ple_gate=v_w_ple_gate, b_ple_gate=v_b_ple_gate,
                w_ple=v_w_ple, ple_norm_g=v_ple_norm_g, final_norm_g=v_final_norm_g)
    x_pos, y_pos, c_pos = _mesh_pos()
    me = 4 * x_pos + 2 * y_pos + c_pos

    first = _all_gather([wts["w_in"][0].T.astype(bf16), wts["ssd_conv_w"][0], wts["conf_dw_w"][0]], "gather_first")
    ready = {
        "w_in": first[0].reshape(IN_WIDTH, D_MODEL),
        "ssd_conv_w": jnp.pad(_columns_of_blocks(first[1]), ((0, 8 - SSD_CONV), (0, 0))),
        "conf_dw_w": jnp.pad(_columns_of_blocks(first[2]), ((0, 32 - CONF_KERNEL), (0, 0))),
    }
    shards = [wts[n][0].astype(bf16) for n in _LATER]
    states, token = _exchange_start(shards, False, "gather_start", first[1])
    comm = _StepComm(me, ready, dict(zip(_LATER, states)), token)
    small = {n: wts[n].reshape(1, -1) for n in _REPLICATED}

    grad_x, acc = _local_step(x[0], p[0, 0], loss_target[0], small, comm)

    packed = _pack_small(acc)
    (small_state,), _ = _exchange_start([packed], False, "small_start", comm.token)
    grads, delta, new_m, new_v = {}, {}, {}, {}

    def adamw_big(names, sent, after):
        lands = _exchange_wait(sent, True, "scatter_wait_" + names[0], after)
        for n, land in zip(names, lands):
            view = (lambda a: a[0].T) if n == "w_in" else (lambda a: a[0])
            back = (lambda a: a.T[None]) if n == "w_in" else (lambda a: a[None])
            out = _adamw_big(land, view(wts[n]), view(mom1[n]), view(mom2[n]), "adamw_" + n)
            grads[n], delta[n], new_m[n], new_v[n] = [back(a) for a in out]

    for names, sent in comm.sent[:-1]:
        adamw_big(names, sent, [grad_x])
    (all_small,) = _exchange_wait([small_state], False, "small_wait", [delta[n] for n in _LATER])
    as_row = lambda d: {n: d[n].reshape(1, -1) for n in _REPLICATED}
    loss, per_name, conv_w_sum, conf_w_sum = _small_update(all_small, as_row(wts), as_row(mom1), as_row(mom2))
    for n in _REPLICATED:
        grads[n], delta[n], new_m[n], new_v[n] = [a.reshape(wts[n].shape) for a in per_name[n]]
    filt_g = [lax.dynamic_slice_in_dim(conv_w_sum, me * 192, 192, axis=1),
              lax.dynamic_slice_in_dim(conf_w_sum, me * 128, 128, axis=1)]
    filt = _adamw_filters(filt_g, [wts[n][0] for n in _CONV_WEIGHTS], [mom1[n][0] for n in _CONV_WEIGHTS],
                          [mom2[n][0] for n in _CONV_WEIGHTS])
    for n, g, (d, mn, vn) in zip(_CONV_WEIGHTS, filt_g, filt):
        grads[n], delta[n], new_m[n], new_v[n] = g[None], d[None], mn[None], vn[None]
    adamw_big(*comm.sent[-1], [delta[n] for n in _LATER])

    return (loss.reshape(()), grad_x[None], *[grads[n] for n in _WEIGHT_ORDER], *[delta[n] for n in _WEIGHT_ORDER],
            *[new_m[n] for n in _WEIGHT_ORDER], *[new_v[n] for n in _WEIGHT_ORDER])
```

```python
import functools

import jax
import jax.numpy as jnp
from jax import lax
from jax.experimental import pallas as pl
from jax.experimental.pallas import tpu as pltpu

f32 = jnp.float32
bf16 = jnp.bfloat16

EPS = 1e-6
D_MODEL = 1024
SSD_HEADS = 16
SSD_HEAD_DIM = 64
SSD_GROUPS = 2
SSD_STATE = 128
SSD_CONV = 4
CHUNK = 128
XBC_WIDTH = 1536
CONF_KERNEL = 31
D_FF = 4096
PLE_DIM = 256
IN_WIDTH = 4624
N_DEV = 8
SEG_Z = (0, 1024)
SEG_XBC = (1024, 2560)
SEG_DT = (2560, 2576)
SEG_CV = (2576, 3600)
SEG_CG = (3600, 4624)
IN_SEGS = (SEG_Z, SEG_XBC, SEG_DT, SEG_CV, SEG_CG)

ADAM_LR = 0.001
ADAM_B1 = 0.9
ADAM_B2 = 0.999
ADAM_EPS = 1e-08
ADAM_WD = 0.01
ADAM_STEP = 10

VMEM_LIMIT_BYTES = 56 * 1024 * 1024
TOKEN_TILE = 512
SMALL_ROWS = 16

HBM_SPEC = pl.BlockSpec(memory_space=pltpu.HBM)
MESH_ID = pl.DeviceIdType.MESH


def _params(*semantics):
    return pltpu.CompilerParams(dimension_semantics=semantics, vmem_limit_bytes=VMEM_LIMIT_BYTES)


def _nn(a, b):
    return lax.dot_general(a, b, (((1,), (0,)), ((), ())), preferred_element_type=f32)


def _nt(a, b):
    return lax.dot_general(a, b, (((1,), (1,)), ((), ())), preferred_element_type=f32)


def _tn(a, b):
    return lax.dot_general(a, b, (((0,), (0,)), ((), ())), preferred_element_type=f32)


@jax.custom_vjp
def bnn(a, b):
    return _nn(a.astype(bf16), b.astype(bf16))


def _bnn_fwd(a, b):
    ab, bb = a.astype(bf16), b.astype(bf16)
    return _nn(ab, bb), (ab, bb)


def _bnn_bwd(res, g):
    ab, bb = res
    gb = g.astype(bf16)
    return _nt(gb, bb), _tn(ab, gb)


bnn.defvjp(_bnn_fwd, _bnn_bwd)


@jax.custom_vjp
def bnt(a, b):
    return _nt(a.astype(bf16), b.astype(bf16))


def _bnt_fwd(a, b):
    ab, bb = a.astype(bf16), b.astype(bf16)
    return _nt(ab, bb), (ab, bb)


def _bnt_bwd(res, g):
    ab, bb = res
    gb = g.astype(bf16)
    return _nn(gb, bb), _tn(gb, ab)


bnt.defvjp(_bnt_fwd, _bnt_bwd)


@jax.custom_vjp
def btn(a, b):
    return _tn(a.astype(bf16), b.astype(bf16))


def _btn_fwd(a, b):
    ab, bb = a.astype(bf16), b.astype(bf16)
    return _tn(ab, bb), (ab, bb)


def _btn_bwd(res, g):
    ab, bb = res
    gb = g.astype(bf16)
    return _nt(bb, gb), _nn(ab, gb)


btn.defvjp(_btn_fwd, _btn_bwd)

CUMSUM_PASSES = 3
EXPAND_PASSES = 2


def _bf16_pieces(a, passes):
    pieces, rest = [], a
    for _ in range(passes):
        piece = rest.astype(bf16)
        pieces.append(piece)
        rest = rest - piece.astype(f32)
    return pieces


def _make_dot01(form, passes):
    fwd, bwd = {"right": (lambda b, p: _nn(p, b), lambda b, g: _nt(g, b)),
                "left": (lambda b, p: _nn(b, p), lambda b, g: _tn(b, g)),
                "tn": (lambda b, p: _tn(p, b), lambda b, g: _nt(b, g))}[form]

    def run(op, b, v):
        out = None
        for piece in _bf16_pieces(v, passes):
            term = op(b, piece)
            out = term if out is None else out + term
        return out

    @jax.custom_vjp
    def product(b, v):
        return run(fwd, b, v)

    product.defvjp(lambda b, v: (run(fwd, b, v), b), lambda b, g: (jnp.zeros_like(b), run(bwd, b, g)))
    return product


_DOT01 = {(form, passes): _make_dot01(form, passes)
          for form, passes in (("left", CUMSUM_PASSES), ("tn", CUMSUM_PASSES), ("right", EXPAND_PASSES))}


def _dot01(b01, a, form, passes):
    return _DOT01[form, passes](b01.astype(bf16), a)


def _rms(x, g):
    return x * lax.rsqrt(jnp.mean(x * x, axis=-1, keepdims=True) + EPS) * g


def _gated_norm(y, z, g):
    v = y * jax.nn.silu(z)
    half = v.shape[-1] // SSD_GROUPS
    parts = []
    for k in range(SSD_GROUPS):
        vk = v[:, k * half:(k + 1) * half]
        parts.append(vk * lax.rsqrt(jnp.mean(vk * vk, axis=-1, keepdims=True) + EPS))
    return jnp.concatenate(parts, axis=-1) * g


def _ln_silu(v, g, b):
    mu = jnp.mean(v, axis=-1, keepdims=True)
    xc = v - mu
    y = xc * lax.rsqrt(jnp.mean(xc * xc, axis=-1, keepdims=True) + EPS) * g + b
    return jax.nn.silu(y)


def _acc_init(i, *refs):
    @pl.when(i == 0)
    def _():
        for r in refs:
            r[...] = jnp.zeros_like(r)


def _after_token(body, token):
    if token is None:
        return body, [], []

    def body_after(tok_ref, *refs):
        del tok_ref
        body(*refs)

    return body_after, [token], [pl.BlockSpec(memory_space=pl.ANY)]


def _row_spec(tm, n):
    return pl.BlockSpec((tm, n), lambda i: (i, 0))


def _const_spec(shape):
    nd = len(shape)
    return pl.BlockSpec(shape, lambda i: (0,) * nd)


def _prev_halo_spec(tm, halo, n):
    return pl.BlockSpec((halo, n), lambda i: (jnp.maximum(i * (tm // halo) - 1, 0), 0))


def _next_halo_spec(tm, halo, n, t):
    return pl.BlockSpec((halo, n), lambda i: (jnp.minimum((i + 1) * (tm // halo), t // halo - 1), 0))


def _in_proj(x, g1, w_in_p, token=None):
    t = x.shape[0]
    tm = min(TOKEN_TILE, t)

    def body(x_ref, g_ref, w_ref, u_ref, z_ref, xbc_ref, dt_ref, cv_ref, cg_ref):
        u = _rms(x_ref[...], g_ref[...]).astype(bf16)
        u_ref[...] = u
        for (lo, hi), o_ref in zip(IN_SEGS, (z_ref, xbc_ref, dt_ref, cv_ref, cg_ref)):
            o_ref[...] = _nt(u, w_ref[lo:hi, :])

    widths = [hi - lo for lo, hi in IN_SEGS]
    outs = [jax.ShapeDtypeStruct((t, D_MODEL), bf16)] + [jax.ShapeDtypeStruct((t, n), f32) for n in widths]
    body, tok, tok_spec = _after_token(body, token)
    return pl.pallas_call(
        body, name="in_proj", grid=(t // tm,),
        in_specs=tok_spec + [_row_spec(tm, D_MODEL), _const_spec((1, D_MODEL)), _const_spec((IN_WIDTH, D_MODEL))],
        out_specs=[_row_spec(tm, D_MODEL)] + [_row_spec(tm, n) for n in widths],
        out_shape=outs, compiler_params=_params("arbitrary"),
    )(*tok, x, g1, w_in_p)


ROW_BLOCK = 16
LANE_CHUNK = 512


def _blocks(rows, cols):
    return [(slice(r, r + ROW_BLOCK), slice(c, c + LANE_CHUNK))
            for c in range(0, cols, LANE_CHUNK) for r in range(0, rows, ROW_BLOCK)]


def _conv4_block(ext_ref, w_ref, b_ref, rs, ls):
    acc = b_ref[:, ls] + w_ref[0:1, ls] * ext_ref[rs.start + 8 - 3:rs.stop + 8 - 3, ls]
    for k in range(1, SSD_CONV):
        acc = acc + w_ref[k:k + 1, ls] * ext_ref[rs.start + 8 - 3 + k:rs.stop + 8 - 3 + k, ls]
    return acc


def _ssd_pre(xbc_raw, dt_raw, cw, cb, dt_bias):
    t = xbc_raw.shape[0]
    tm = min(TOKEN_TILE, t)

    def body(cur_ref, halo_ref, dt_ref, w_ref, b_ref, dtb_ref, act_ref, dto_ref, ext_ref):
        i = pl.program_id(0)
        ext_ref[0:8, :] = jnp.where(i == 0, 0.0, halo_ref[...])
        ext_ref[8:, :] = cur_ref[...]
        for rs, ls in _blocks(tm, XBC_WIDTH):
            act_ref[rs, ls] = jax.nn.silu(_conv4_block(ext_ref, w_ref, b_ref, rs, ls))
        dto_ref[...] = jax.nn.softplus(dt_ref[...] + dtb_ref[...])

    return pl.pallas_call(
        body, name="ssd_pre", grid=(t // tm,),
        in_specs=[_row_spec(tm, XBC_WIDTH), _prev_halo_spec(tm, 8, XBC_WIDTH), _row_spec(tm, SSD_HEADS),
                  _const_spec((8, XBC_WIDTH)), _const_spec((1, XBC_WIDTH)), _const_spec((1, SSD_HEADS))],
        out_specs=[_row_spec(tm, XBC_WIDTH), _row_spec(tm, SSD_HEADS)],
        out_shape=(jax.ShapeDtypeStruct((t, XBC_WIDTH), f32), jax.ShapeDtypeStruct((t, SSD_HEADS), f32)),
        scratch_shapes=[pltpu.VMEM((tm + 8, XBC_WIDTH), f32)],
        compiler_params=_params("arbitrary"),
    )(xbc_raw, xbc_raw, dt_raw, cw, cb, dt_bias)


def _ssd_consts():
    r = lax.broadcasted_iota(jnp.int32, (CHUNK, CHUNK), 0)
    c = lax.broadcasted_iota(jnp.int32, (CHUNK, CHUNK), 1)
    causal = r >= c
    tril = causal.astype(f32)
    triu = (r <= c).astype(f32)
    hh = lax.broadcasted_iota(jnp.int32, (SSD_HEADS, D_MODEL), 0)
    jj = lax.broadcasted_iota(jnp.int32, (SSD_HEADS, D_MODEL), 1)
    expand = (lax.shift_right_logical(jj, 6) == hh).astype(f32)
    lane = lax.broadcasted_iota(jnp.int32, (1, 2 * SSD_HEAD_DIM), 1)
    m0 = (lane < SSD_HEAD_DIM).astype(f32)
    return causal, tril, triu, expand, m0, 1.0 - m0


def _ssd_chunk(xs, bm, cm, dt, s_prev, a_log, d_skip, consts):
    causal, tril, triu, expand, m0, m1 = consts
    a = dt * (-jnp.exp(a_log))
    cs = _dot01(tril, a, "left", CUMSUM_PASSES)
    cs_t = _dot01(triu, a, "tn", CUMSUM_PASSES)
    cs_last = cs[CHUNK - 1:CHUNK, :]
    per_head = jnp.concatenate([dt, jnp.exp(cs_last - cs), jnp.exp(cs), jnp.broadcast_to(jnp.exp(cs_last), (8, SSD_HEADS)),
                                jnp.broadcast_to(d_skip, (8, SSD_HEADS))], axis=0)
    per_channel = _dot01(expand, per_head, "right", EXPAND_PASSES)
    dt_e, dec_end, dec_start = (per_channel[i * CHUNK:(i + 1) * CHUNK] for i in range(3))
    chunk_dec = per_channel[3 * CHUNK:3 * CHUNK + 1]
    d_e = per_channel[3 * CHUNK + 8:3 * CHUNK + 9]
    xc = xs * dt_e
    x_dec = xc * dec_end
    gw = D_MODEL // SSD_GROUPS
    ys, states = [], []
    for g in range(SSD_GROUPS):
        bg = bm[:, g * SSD_STATE:(g + 1) * SSD_STATE]
        cg = cm[:, g * SSD_STATE:(g + 1) * SSD_STATE]
        sp = s_prev[:, g * gw:(g + 1) * gw]
        states.append(sp * chunk_dec[:, g * gw:(g + 1) * gw] + btn(bg, x_dec[:, g * gw:(g + 1) * gw]))
        y_off = bnn(cg, sp) * dec_start[:, g * gw:(g + 1) * gw]
        scores = bnt(cg, bg)
        pieces = []
        for pr in range(gw // (2 * SSD_HEAD_DIM)):
            lo = g * gw + pr * 2 * SSD_HEAD_DIM
            xp = xc[:, lo:lo + 2 * SSD_HEAD_DIM]
            acc = None
            for h, mk in ((lo // SSD_HEAD_DIM, m0), (lo // SSD_HEAD_DIM + 1, m1)):
                seg = cs[:, h:h + 1] - cs_t[h:h + 1, :]
                mat = scores * jnp.exp(jnp.where(causal, seg, -jnp.inf))
                term = bnn(mat, xp * mk)
                acc = term if acc is None else acc + term
            pieces.append(acc)
        ys.append(jnp.concatenate(pieces, axis=-1) + y_off)
    y = jnp.concatenate(ys, axis=-1) + xs * d_e
    return y, jnp.concatenate(states, axis=-1)


def _ssd_scan(xbc_act, dt, a_log, d_skip):
    t = xbc_act.shape[0]
    nc = t // CHUNK

    def body(xbc_ref, dt_ref, al_ref, dk_ref, y_ref, sp_ref, state_ref):
        c = pl.program_id(0)

        @pl.when(c == 0)
        def _():
            state_ref[...] = jnp.zeros_like(state_ref)

        s_prev = state_ref[...]
        sp_ref[...] = s_prev
        y, s_next = _ssd_chunk(xbc_ref[:, 0:1024], xbc_ref[:, 1024:1280], xbc_ref[:, 1280:1536], dt_ref[...],
                               s_prev, al_ref[...], dk_ref[...], _ssd_consts())
        y_ref[...] = y
        state_ref[...] = s_next

    return pl.pallas_call(
        body, name="ssd_scan", grid=(nc,),
        in_specs=[_row_spec(CHUNK, XBC_WIDTH), _row_spec(CHUNK, SSD_HEADS), _const_spec((1, SSD_HEADS)),
                  _const_spec((1, SSD_HEADS))],
        out_specs=[_row_spec(CHUNK, D_MODEL), pl.BlockSpec((None, SSD_STATE, D_MODEL), lambda c: (c, 0, 0))],
        out_shape=(jax.ShapeDtypeStruct((t, D_MODEL), f32), jax.ShapeDtypeStruct((nc, SSD_STATE, D_MODEL), f32)),
        scratch_shapes=[pltpu.VMEM((SSD_STATE, D_MODEL), f32)],
        compiler_params=_params("arbitrary"),
    )(xbc_act, dt, a_log, d_skip)


CONV_HALO = 32
CONV_BLOCK = 64
LANE_TILE = 128
SUBLANES = 8


def _fill_glu_ext(ext_ref, sg_ref, cv_ref, cg_ref, hcv_ref, hcg_ref, i):
    ext_ref[0:CONV_HALO, :] = jnp.where(i == 0, 0.0, hcv_ref[...] * jax.nn.sigmoid(hcg_ref[...]))
    sg = jax.nn.sigmoid(cg_ref[...])
    if sg_ref is not None:
        sg_ref[...] = sg
    ext_ref[CONV_HALO:, :] = cv_ref[...] * sg


def _shifted_copies(src_ref, dst_ref, rows):
    for s in range(1, SUBLANES):
        dst_ref[s, 0:rows, :] = src_ref[pl.ds(s, rows), :]


def _tap_source(src_ref, shifted_ref, offset):
    s = offset % SUBLANES
    return (src_ref if s == 0 else shifted_ref.at[s]), offset - s


def _conv_rows(src_ref, shifted_ref, taps, offsets, lanes, r0, init):
    accs = [init] * (CONV_BLOCK // SUBLANES)
    for k, off in enumerate(offsets):
        ref, base = _tap_source(src_ref, shifted_ref, off)
        for j in range(len(accs)):
            accs[j] = accs[j] + taps[k] * ref[pl.ds(r0 + base + SUBLANES * j, SUBLANES), lanes]
    return accs


def _conf_conv(cv, cg, w, b):
    t = cv.shape[0]
    tm = min(TOKEN_TILE, t)
    ext_rows = tm + CONV_HALO
    offsets = [CONV_HALO - (CONF_KERNEL - 1) + k for k in range(CONF_KERNEL)]

    def body(cv_ref, cg_ref, hcv_ref, hcg_ref, w_ref, b_ref, o_ref, ext_ref, shifted_ref):
        _fill_glu_ext(ext_ref, None, cv_ref, cg_ref, hcv_ref, hcg_ref, pl.program_id(0))
        _shifted_copies(ext_ref, shifted_ref, ext_rows - SUBLANES)
        for lb in range(1024 // LANE_TILE):
            lanes = slice(lb * LANE_TILE, (lb + 1) * LANE_TILE)
            taps = [jnp.broadcast_to(w_ref[k:k + 1, lanes], (SUBLANES, LANE_TILE)) for k in range(CONF_KERNEL)]
            bias = jnp.broadcast_to(b_ref[:, lanes], (SUBLANES, LANE_TILE))

            def rows(rb, carry, lanes=lanes, taps=taps, bias=bias):
                r0 = pl.multiple_of(rb * CONV_BLOCK, CONV_BLOCK)
                accs = _conv_rows(ext_ref, shifted_ref, taps, offsets, lanes, r0, bias)
                for j, a in enumerate(accs):
                    o_ref[pl.ds(r0 + SUBLANES * j, SUBLANES), lanes] = a
                return carry

            lax.fori_loop(0, tm // CONV_BLOCK, rows, 0)

    return pl.pallas_call(
        body, name="conf_conv", grid=(t // tm,),
        in_specs=[_row_spec(tm, 1024), _row_spec(tm, 1024), _prev_halo_spec(tm, CONV_HALO, 1024),
                  _prev_halo_spec(tm, CONV_HALO, 1024), _const_spec((32, 1024)), _const_spec((1, 1024))],
        out_specs=_row_spec(tm, 1024), out_shape=jax.ShapeDtypeStruct((t, 1024), f32),
        scratch_shapes=[pltpu.VMEM((ext_rows, 1024), f32), pltpu.VMEM((SUBLANES, ext_rows, 1024), f32)],
        compiler_params=_params("arbitrary"),
    )(cv, cg, cv, cg, w, b)


def _out_proj(y, z, v2, x, g_ssd, ln_g, ln_b, w_out, g2):
    t = x.shape[0]
    tm = min(TOKEN_TILE, t)

    def body(y_ref, z_ref, v_ref, x_ref, gs_ref, lg_ref, lb_ref, w_ref, g2_ref, ys_ref, yc_ref, h1_ref, u2_ref):
        ys = _gated_norm(y_ref[...], z_ref[...], gs_ref[...]).astype(bf16)
        yc = _ln_silu(v_ref[...], lg_ref[...], lb_ref[...]).astype(bf16)
        ys_ref[...] = ys
        yc_ref[...] = yc
        h1 = x_ref[...] + _nn(ys, w_ref[0:1024, :]) + _nn(yc, w_ref[1024:2048, :])
        h1_ref[...] = h1
        u2_ref[...] = _rms(h1, g2_ref[...]).astype(bf16)

    vec = _const_spec((1, 1024))
    row = _row_spec(tm, 1024)
    return pl.pallas_call(
        body, name="out_proj", grid=(t // tm,),
        in_specs=[row, row, row, row, vec, vec, vec, _const_spec((2048, 1024)), vec],
        out_specs=[row, row, row, row],
        out_shape=(jax.ShapeDtypeStruct((t, 1024), bf16), jax.ShapeDtypeStruct((t, 1024), bf16),
                   jax.ShapeDtypeStruct((t, 1024), f32), jax.ShapeDtypeStruct((t, 1024), bf16)),
        compiler_params=_params("arbitrary"),
    )(y, z, v2, x, g_ssd, ln_g, ln_b, w_out, g2)


def _mlp_up(u2, w_up):
    t = u2.shape[0]
    tm = min(TOKEN_TILE, t)
    blk = D_FF // N_DEV

    def body(u_ref, w_ref, pre_ref, hs_ref):
        u = u_ref[...]
        for k in range(N_DEV):
            pre = _nn(u, w_ref[k])
            pre_ref[:, k * blk:(k + 1) * blk] = pre.astype(bf16)
            r = jnp.maximum(pre, 0.0)
            hs_ref[:, k * blk:(k + 1) * blk] = (r * r).astype(bf16)

    return pl.pallas_call(
        body, name="mlp_up", grid=(t // tm,),
        in_specs=[_row_spec(tm, D_MODEL), _const_spec((N_DEV, D_MODEL, D_FF // N_DEV))],
        out_specs=[_row_spec(tm, D_FF), _row_spec(tm, D_FF)],
        out_shape=(jax.ShapeDtypeStruct((t, D_FF), bf16), jax.ShapeDtypeStruct((t, D_FF), bf16)),
        compiler_params=_params("arbitrary"),
    )(u2, w_up)


def _mlp_down(h1, hs, w_down):
    t = h1.shape[0]
    tm = min(TOKEN_TILE, t)

    def body(h_ref, hs_ref, w_ref, o_ref):
        o_ref[...] = h_ref[...] + _nn(hs_ref[...], w_ref[...])

    return pl.pallas_call(
        body, name="mlp_down", grid=(t // tm,),
        in_specs=[_row_spec(tm, D_MODEL), _row_spec(tm, D_FF), _const_spec((D_FF, D_MODEL))],
        out_specs=_row_spec(tm, D_MODEL), out_shape=jax.ShapeDtypeStruct((t, D_MODEL), f32),
        compiler_params=_params("arbitrary"),
    )(h1, hs, w_down)


TAIL_LOSS, TAIL_FINAL_G, TAIL_PLE_G, TAIL_GATE_B, TAIL_GATE_NORM_G = 0, 1, 2, 3, 4


def _tail(h2, p, target, g3, w_gate, b_gate, w_ple, ple_g, fin_g):
    t = h2.shape[0]
    tm = min(TOKEN_TILE, t)

    def body(h_ref, p_ref, t_ref, g3_ref, wg_ref, bg_ref, wp_ref, pg_ref, fg_ref,
             dh_ref, dhb_ref, dwg_ref, dwp_ref, acc_ref):
        i = pl.program_id(0)
        _acc_init(i, dwg_ref, dwp_ref, acc_ref)
        h2v = h_ref[...]
        tgt = t_ref[...]
        u3, vjp_u3 = jax.vjp(_rms, h2v, g3_ref[...])
        u3b = u3.astype(bf16)
        pb = p_ref[...].astype(bf16)
        gate_pre = _nn(u3b, wg_ref[...]) + bg_ref[...]
        emb_pre = _nn(pb, wp_ref[...])

        def tail_fn(hv, gp, ep, pg, fg):
            h3 = hv + jax.nn.sigmoid(gp) * _rms(ep, pg)
            err = _rms(h3, fg) - tgt
            return 0.5 * jnp.mean(err * err, axis=-1, keepdims=True)

        loss_tok, vjp_tail = jax.vjp(tail_fn, h2v, gate_pre, emb_pre, pg_ref[...], fg_ref[...])
        dh_a, dgp, dep, dpg, dfg = vjp_tail(jnp.ones_like(loss_tok))
        dgpb = dgp.astype(bf16)
        dh_b, dg3 = vjp_u3(_nt(dgpb, wg_ref[...]))
        dh = dh_a + dh_b
        dh_ref[...] = dh
        dhb_ref[...] = dh.astype(bf16)
        dwg_ref[...] += _tn(u3b, dgpb)
        dwp_ref[...] += _tn(pb, dep.astype(bf16))
        acc_ref[TAIL_LOSS:TAIL_LOSS + 1, :] += jnp.broadcast_to(jnp.sum(loss_tok, axis=0, keepdims=True), (1, 1024))
        acc_ref[TAIL_FINAL_G:TAIL_FINAL_G + 1, :] += dfg
        acc_ref[TAIL_PLE_G:TAIL_PLE_G + 1, :] += dpg
        acc_ref[TAIL_GATE_B:TAIL_GATE_B + 1, :] += jnp.sum(dgp, axis=0, keepdims=True)
        acc_ref[TAIL_GATE_NORM_G:TAIL_GATE_NORM_G + 1, :] += dg3

    vec = _const_spec((1, 1024))
    row = _row_spec(tm, 1024)
    return pl.pallas_call(
        body, name="tail", grid=(t // tm,),
        in_specs=[row, _row_spec(tm, PLE_DIM), row, vec, _const_spec((1024, 1024)), vec, _const_spec((PLE_DIM, 1024)), vec, vec],
        out_specs=[row, row, _const_spec((1024, 1024)), _const_spec((PLE_DIM, 1024)), _const_spec((8, 1024))],
        out_shape=(jax.ShapeDtypeStruct((t, 1024), f32), jax.ShapeDtypeStruct((t, 1024), bf16),
                   jax.ShapeDtypeStruct((1024, 1024), f32), jax.ShapeDtypeStruct((PLE_DIM, 1024), f32),
                   jax.ShapeDtypeStruct((8, 1024), f32)),
        compiler_params=_params("arbitrary"),
    )(h2, p, target, g3, w_gate, b_gate, w_ple, ple_g, fin_g)


def _mlp_bwd(dh2, dh2b, pre, h1, g2, w_down, w_up, token=None):
    t = dh2.shape[0]
    tm = min(TOKEN_TILE, t)
    blk = D_FF // N_DEV

    def body(dh_ref, dhb_ref, pre_ref, h1_ref, g_ref, wd_hbm, wu_hbm, dpre_ref, dh1_ref, dh1b_ref, acc_ref,
             wd_ref, wu_ref):
        i = pl.program_id(0)
        _acc_init(i, acc_ref)

        @pl.when(i == 0)
        def _():
            pltpu.sync_copy(wd_hbm, wd_ref)
            pltpu.sync_copy(wu_hbm, wu_ref)

        dhb = dhb_ref[...]
        du2 = jnp.zeros((tm, D_MODEL), f32)
        for k in range(D_FF // blk):
            dhs = _nt(dhb, wd_ref[k * blk:(k + 1) * blk, :])
            dpre = (dhs * (2.0 * jnp.maximum(pre_ref[:, k * blk:(k + 1) * blk].astype(f32), 0.0))).astype(bf16)
            dpre_ref[:, k * blk:(k + 1) * blk] = dpre
            du2 = du2 + _nt(dpre, wu_ref[k])
        _, vjp_u2 = jax.vjp(_rms, h1_ref[...], g_ref[...])
        d, dg = vjp_u2(du2)
        dh1 = dh_ref[...] + d
        dh1_ref[...] = dh1
        dh1b_ref[...] = dh1.astype(bf16)
        acc_ref[0:1, :] += dg

    row = _row_spec(tm, 1024)
    body, tok, tok_spec = _after_token(body, token)
    return pl.pallas_call(
        body, name="mlp_bwd", grid=(t // tm,),
        in_specs=tok_spec + [row, row, _row_spec(tm, D_FF), row, _const_spec((1, 1024)), HBM_SPEC, HBM_SPEC],
        out_specs=[_row_spec(tm, D_FF), row, row, _const_spec((8, 1024))],
        out_shape=(jax.ShapeDtypeStruct((t, D_FF), bf16), jax.ShapeDtypeStruct((t, 1024), f32),
                   jax.ShapeDtypeStruct((t, 1024), bf16), jax.ShapeDtypeStruct((8, 1024), f32)),
        scratch_shapes=[pltpu.VMEM((D_FF, D_MODEL), bf16), pltpu.VMEM((N_DEV, D_MODEL, D_FF // N_DEV), bf16)],
        compiler_params=_params("arbitrary"),
    )(*tok, dh2, dh2b, pre, h1, g2, w_down, w_up)


OPB_SSD_G, OPB_LN_G, OPB_LN_B, OPB_CONV_B = 0, 1, 2, 3


def _out_proj_bwd(dh1b, y, z, v2, g_ssd, ln_g, ln_b, w_out, token=None):
    t = y.shape[0]
    tm = min(TOKEN_TILE, t)

    def body(dh_ref, y_ref, z_ref, v_ref, gs_ref, lg_ref, lb_ref, w_ref, dy_ref, dz_ref, dv_ref, acc_ref):
        i = pl.program_id(0)
        _acc_init(i, acc_ref)
        dhb = dh_ref[...]
        dys = _nt(dhb, w_ref[0:1024, :])
        dyc = _nt(dhb, w_ref[1024:2048, :])
        _, vjp_g = jax.vjp(_gated_norm, y_ref[...], z_ref[...], gs_ref[...])
        dy, dz, dgs = vjp_g(dys)
        _, vjp_l = jax.vjp(_ln_silu, v_ref[...], lg_ref[...], lb_ref[...])
        dv, dlg, dlb = vjp_l(dyc)
        dy_ref[...] = dy
        dz_ref[...] = dz.astype(bf16)
        dv_ref[...] = dv
        acc_ref[OPB_SSD_G:OPB_SSD_G + 1, :] += dgs
        acc_ref[OPB_LN_G:OPB_LN_G + 1, :] += dlg
        acc_ref[OPB_LN_B:OPB_LN_B + 1, :] += dlb
        acc_ref[OPB_CONV_B:OPB_CONV_B + 1, :] += jnp.sum(dv, axis=0, keepdims=True)

    vec = _const_spec((1, 1024))
    row = _row_spec(tm, 1024)
    body, tok, tok_spec = _after_token(body, token)
    return pl.pallas_call(
        body, name="out_proj_bwd", grid=(t // tm,),
        in_specs=tok_spec + [row, row, row, row, vec, vec, vec, _const_spec((2048, 1024))],
        out_specs=[row, row, row, _const_spec((8, 1024))],
        out_shape=(jax.ShapeDtypeStruct((t, 1024), f32), jax.ShapeDtypeStruct((t, 1024), bf16),
                   jax.ShapeDtypeStruct((t, 1024), f32), jax.ShapeDtypeStruct((8, 1024), f32)),
        compiler_params=_params("arbitrary"),
    )(*tok, dh1b, y, z, v2, g_ssd, ln_g, ln_b, w_out)


def _conf_conv_bwd(dv2, cv, cg, w):
    t = cv.shape[0]
    tm = min(TOKEN_TILE, t)
    ext_rows = tm + CONV_HALO
    offsets = [CONF_KERNEL - 1 - k for k in range(CONF_KERNEL)]

    def body(dv_ref, dvn_ref, cv_ref, cg_ref, w_ref, dcv_ref, dcg_ref, dw_ref, glu_ref, dext_ref, sg_ref, shifted_ref):
        i = pl.program_id(0)
        _acc_init(i, dw_ref)
        sg = jax.nn.sigmoid(cg_ref[...])
        sg_ref[...] = sg
        glu_ref[...] = cv_ref[...] * sg
        dext_ref[0:tm, :] = dv_ref[...]
        dext_ref[tm:, :] = jnp.where(i == pl.num_programs(0) - 1, 0.0, dvn_ref[...])
        _shifted_copies(dext_ref, shifted_ref, ext_rows - SUBLANES)

        for lb in range(1024 // LANE_TILE):
            lanes = slice(lb * LANE_TILE, (lb + 1) * LANE_TILE)

            def rows_w(rb, accs, lanes=lanes):
                r0 = pl.multiple_of(rb * CONV_BLOCK, CONV_BLOCK)
                accs = list(accs)
                for j in range(CONV_BLOCK // SUBLANES):
                    x = glu_ref[pl.ds(r0 + SUBLANES * j, SUBLANES), lanes]
                    for k, off in enumerate(offsets):
                        ref, base = _tap_source(dext_ref, shifted_ref, off)
                        accs[k] = accs[k] + x * ref[pl.ds(r0 + base + SUBLANES * j, SUBLANES), lanes]
                return tuple(accs)

            zero = jnp.zeros((SUBLANES, LANE_TILE), f32)
            accs = lax.fori_loop(0, tm // CONV_BLOCK, rows_w, (zero,) * CONF_KERNEL)
            for k in range(CONF_KERNEL):
                dw_ref[k:k + 1, lanes] += jnp.sum(accs[k], axis=0, keepdims=True)

        for lb in range(1024 // LANE_TILE):
            lanes = slice(lb * LANE_TILE, (lb + 1) * LANE_TILE)
            taps = [jnp.broadcast_to(w_ref[k:k + 1, lanes], (SUBLANES, LANE_TILE)) for k in range(CONF_KERNEL)]

            def rows_x(rb, carry, lanes=lanes, taps=taps):
                r0 = pl.multiple_of(rb * CONV_BLOCK, CONV_BLOCK)
                zero = jnp.zeros((SUBLANES, LANE_TILE), f32)
                dglu = jnp.concatenate(_conv_rows(dext_ref, shifted_ref, taps, offsets, lanes, r0, zero), axis=0)
                sg = sg_ref[pl.ds(r0, CONV_BLOCK), lanes]
                cvv = cv_ref[pl.ds(r0, CONV_BLOCK), lanes]
                dcv_ref[pl.ds(r0, CONV_BLOCK), lanes] = (dglu * sg).astype(bf16)
                dcg_ref[pl.ds(r0, CONV_BLOCK), lanes] = (dglu * cvv * sg * (1.0 - sg)).astype(bf16)
                return carry

            lax.fori_loop(0, tm // CONV_BLOCK, rows_x, 0)

    row = _row_spec(tm, 1024)
    return pl.pallas_call(
        body, name="conf_conv_bwd", grid=(t // tm,),
        in_specs=[row, _next_halo_spec(tm, CONV_HALO, 1024, t), row, row, _const_spec((32, 1024))],
        out_specs=[row, row, _const_spec((32, 1024))],
        out_shape=(jax.ShapeDtypeStruct((t, 1024), bf16), jax.ShapeDtypeStruct((t, 1024), bf16),
                   jax.ShapeDtypeStruct((32, 1024), f32)),
        scratch_shapes=[pltpu.VMEM((tm, 1024), f32), pltpu.VMEM((ext_rows, 1024), f32), pltpu.VMEM((tm, 1024), f32),
                        pltpu.VMEM((SUBLANES, ext_rows, 1024), f32)],
        compiler_params=_params("arbitrary"),
    )(dv2, dv2, cv, cg, w)


def _ssd_scan_bwd(xbc_act, dt, s_prev, dy, a_log, d_skip):
    t = xbc_act.shape[0]
    nc = t // CHUNK

    def body(xbc_ref, dt_ref, sp_ref, dy_ref, al_ref, dk_ref, dxbc_ref, ddt_ref, dal_ref, ddk_ref, ds_ref):
        i = pl.program_id(0)
        _acc_init(i, ds_ref, dal_ref, ddk_ref)
        consts = _ssd_consts()
        _, vjp_c = jax.vjp(
            functools.partial(_ssd_chunk, consts=consts),
            xbc_ref[:, 0:1024], xbc_ref[:, 1024:1280], xbc_ref[:, 1280:1536], dt_ref[...], sp_ref[...],
            al_ref[...], dk_ref[...])
        dxs, dbm, dcm, ddt, dsp, dal, ddk = vjp_c((dy_ref[...], ds_ref[...]))
        dxbc_ref[:, 0:1024] = dxs
        dxbc_ref[:, 1024:1280] = dbm
        dxbc_ref[:, 1280:1536] = dcm
        ddt_ref[...] = ddt
        ds_ref[...] = dsp
        dal_ref[...] += dal
        ddk_ref[...] += ddk

    rev = lambda i: (nc - 1 - i, 0)
    return pl.pallas_call(
        body, name="ssd_scan_bwd", grid=(nc,),
        in_specs=[pl.BlockSpec((CHUNK, XBC_WIDTH), rev), pl.BlockSpec((CHUNK, SSD_HEADS), rev),
                  pl.BlockSpec((None, SSD_STATE, D_MODEL), lambda i: (nc - 1 - i, 0, 0)),
                  pl.BlockSpec((CHUNK, D_MODEL), rev), _const_spec((1, SSD_HEADS)), _const_spec((1, SSD_HEADS))],
        out_specs=[pl.BlockSpec((CHUNK, XBC_WIDTH), rev), pl.BlockSpec((CHUNK, SSD_HEADS), rev),
                   _const_spec((1, SSD_HEADS)), _const_spec((1, SSD_HEADS))],
        out_shape=(jax.ShapeDtypeStruct((t, XBC_WIDTH), f32), jax.ShapeDtypeStruct((t, SSD_HEADS), f32),
                   jax.ShapeDtypeStruct((1, SSD_HEADS), f32), jax.ShapeDtypeStruct((1, SSD_HEADS), f32)),
        scratch_shapes=[pltpu.VMEM((SSD_STATE, D_MODEL), f32)],
        compiler_params=_params("arbitrary"),
    )(xbc_act, dt, s_prev, dy, a_log, d_skip)


def _ssd_pre_bwd(xbc_raw, dxbc_act, ddt, dt_raw, cw, cb, dt_bias):
    t = xbc_raw.shape[0]
    tm = min(TOKEN_TILE, t)

    def body(cur_ref, halo_ref, dact_ref, ddt_ref, dtr_ref, w_ref, b_ref, dtb_ref,
             dco_ref, ddtr_ref, dw_ref, db_ref, ddtb_ref, ext_ref):
        i = pl.program_id(0)
        _acc_init(i, dw_ref, db_ref, ddtb_ref)
        ext_ref[0:8, :] = jnp.where(i == 0, 0.0, halo_ref[...])
        ext_ref[8:, :] = cur_ref[...]
        fold = lambda a: a[0:8] + a[8:ROW_BLOCK]
        for c in range(0, XBC_WIDTH, LANE_CHUNK):
            ls = slice(c, c + LANE_CHUNK)
            part_b = jnp.zeros((8, LANE_CHUNK), f32)
            part_w = [jnp.zeros((8, LANE_CHUNK), f32)] * SSD_CONV
            for r in range(0, tm, ROW_BLOCK):
                rs = slice(r, r + ROW_BLOCK)
                co = _conv4_block(ext_ref, w_ref, b_ref, rs, ls)
                sg = jax.nn.sigmoid(co)
                dco = dact_ref[rs, ls] * sg * (1.0 + co * (1.0 - sg))
                dco_ref[rs, ls] = dco
                part_b = part_b + fold(dco)
                part_w = [pw + fold(dco * ext_ref[r + 8 - 3 + k:r + ROW_BLOCK + 8 - 3 + k, ls])
                          for k, pw in enumerate(part_w)]
            db_ref[:, ls] += jnp.sum(part_b, axis=0, keepdims=True)
            for k in range(SSD_CONV):
                dw_ref[k:k + 1, ls] += jnp.sum(part_w[k], axis=0, keepdims=True)
        ddtr = ddt_ref[...] * jax.nn.sigmoid(dtr_ref[...] + dtb_ref[...])
        ddtb_ref[...] += jnp.sum(ddtr, axis=0, keepdims=True)
        ddtr_ref[...] = ddtr.astype(bf16)

    return pl.pallas_call(
        body, name="ssd_pre_bwd", grid=(t // tm,),
        in_specs=[_row_spec(tm, XBC_WIDTH), _prev_halo_spec(tm, 8, XBC_WIDTH), _row_spec(tm, XBC_WIDTH),
                  _row_spec(tm, SSD_HEADS), _row_spec(tm, SSD_HEADS), _const_spec((8, XBC_WIDTH)),
                  _const_spec((1, XBC_WIDTH)), _const_spec((1, SSD_HEADS))],
        out_specs=[_row_spec(tm, XBC_WIDTH), _row_spec(tm, SSD_HEADS), _const_spec((8, XBC_WIDTH)),
                   _const_spec((1, XBC_WIDTH)), _const_spec((1, SSD_HEADS))],
        out_shape=(jax.ShapeDtypeStruct((t, XBC_WIDTH), f32), jax.ShapeDtypeStruct((t, SSD_HEADS), bf16),
                   jax.ShapeDtypeStruct((8, XBC_WIDTH), f32), jax.ShapeDtypeStruct((1, XBC_WIDTH), f32),
                   jax.ShapeDtypeStruct((1, SSD_HEADS), f32)),
        scratch_shapes=[pltpu.VMEM((tm + 8, XBC_WIDTH), f32)],
        compiler_params=_params("arbitrary"),
    )(xbc_raw, xbc_raw, dxbc_act, ddt, dt_raw, cw, cb, dt_bias)


def _conv4_bwd_data(dco, cw):
    t = dco.shape[0]
    tm = min(TOKEN_TILE, t)

    def body(cur_ref, nxt_ref, w_ref, o_ref, ext_ref):
        i = pl.program_id(0)
        ext_ref[0:tm, :] = cur_ref[...]
        ext_ref[tm:, :] = jnp.where(i == pl.num_programs(0) - 1, 0.0, nxt_ref[...])
        for rs, ls in _blocks(tm, XBC_WIDTH):
            acc = w_ref[0:1, ls] * ext_ref[rs.start + SSD_CONV - 1:rs.stop + SSD_CONV - 1, ls]
            for k in range(1, SSD_CONV):
                acc = acc + w_ref[k:k + 1, ls] * ext_ref[rs.start + SSD_CONV - 1 - k:rs.stop + SSD_CONV - 1 - k, ls]
            o_ref[rs, ls] = acc.astype(bf16)

    return pl.pallas_call(
        body, name="conv4_bwd_data", grid=(t // tm,),
        in_specs=[_row_spec(tm, XBC_WIDTH), _next_halo_spec(tm, 8, XBC_WIDTH, t), _const_spec((8, XBC_WIDTH))],
        out_specs=_row_spec(tm, XBC_WIDTH), out_shape=jax.ShapeDtypeStruct((t, XBC_WIDTH), bf16),
        scratch_shapes=[pltpu.VMEM((tm + 8, XBC_WIDTH), f32)],
        compiler_params=_params("arbitrary"),
    )(dco, dco, cw)


def _in_proj_bwd(dproj, x, dh1, g1, w_in_p, token=None):
    t = x.shape[0]
    tm = min(TOKEN_TILE, t)

    def body(dz_ref, dxbc_ref, ddt_ref, dcv_ref, dcg_ref, x_ref, dh_ref, g_ref, w_ref, dx_ref, acc_ref):
        i = pl.program_id(0)
        _acc_init(i, acc_ref)
        du = jnp.zeros((tm, D_MODEL), f32)
        for (lo, hi), r in zip(IN_SEGS, (dz_ref, dxbc_ref, ddt_ref, dcv_ref, dcg_ref)):
            du = du + _nn(r[...], w_ref[lo:hi, :])
        _, vjp_u = jax.vjp(_rms, x_ref[...], g_ref[...])
        d, dg = vjp_u(du)
        dx_ref[...] = dh_ref[...] + d
        acc_ref[0:1, :] += dg

    row = _row_spec(tm, 1024)
    body, tok, tok_spec = _after_token(body, token)
    return pl.pallas_call(
        body, name="in_proj_bwd", grid=(t // tm,),
        in_specs=tok_spec + [_row_spec(tm, hi - lo) for lo, hi in IN_SEGS] + [row, row, _const_spec((1, 1024)),
                                                                               _const_spec((IN_WIDTH, D_MODEL))],
        out_specs=[row, _const_spec((8, 1024))],
        out_shape=(jax.ShapeDtypeStruct((t, 1024), f32), jax.ShapeDtypeStruct((8, 1024), f32)),
        compiler_params=_params("arbitrary"),
    )(*tok, *dproj, x, dh1, g1, w_in_p)


def _mm_tn(a, b, tk, tn, name, column_blocks=False):
    t, kk = a.shape
    n = b.shape[1]
    tk, tn = min(tk, kk), min(tn, n)

    def body(a_ref, b_ref, o_ref):
        o_ref[...] = _tn(a_ref[...], b_ref[...]).astype(bf16)

    if column_blocks:
        assert tk == kk
        out_spec = pl.BlockSpec((None, tk, tn), lambda i, j: (j, 0, 0))
        out_shape = jax.ShapeDtypeStruct((n // tn, kk, tn), bf16)
    else:
        out_spec = pl.BlockSpec((tk, tn), lambda i, j: (i, j))
        out_shape = jax.ShapeDtypeStruct((kk, n), bf16)
    return pl.pallas_call(
        body, name=name, grid=(kk // tk, n // tn),
        in_specs=[pl.BlockSpec((t, tk), lambda i, j: (0, i)), pl.BlockSpec((t, tn), lambda i, j: (0, j))],
        out_specs=out_spec, out_shape=out_shape,
        compiler_params=_params("arbitrary", "arbitrary"),
    )(a, b)


class _LocalWeights:
    def __init__(self, w):
        self.w = w
        self.sent = {}

    def start_token(self):
        return None

    def weight(self, name, after=()):
        del after
        return self.w[name]

    def send_grads(self, grads):
        self.sent.update(grads)
        return None


def _local_step(x, p, target, s, comm):
    conv_w, conf_w = comm.weight("ssd_conv_w"), comm.weight("conf_dw_w")
    w_in = comm.weight("w_in")
    u, z, xbc_raw, dt_raw, cv, cg = _in_proj(x, s["mix_norm_g"], w_in, token=comm.start_token())
    xbc_act, dt = _ssd_pre(xbc_raw, dt_raw, conv_w, s["ssd_conv_b"], s["dt_bias"])
    y, s_prev = _ssd_scan(xbc_act, dt, s["A_log"], s["D_skip"])
    v2 = _conf_conv(cv, cg, conf_w, s["conf_dw_b"])
    w_out = comm.weight("w_out", after=(y, v2))
    ys, yc, h1, u2 = _out_proj(y, z, v2, x, s["ssd_norm_g"], s["conf_ln_g"], s["conf_ln_b"], w_out, s["mlp_norm_g"])
    w_up = comm.weight("w_up", after=(u2,))
    pre, hs = _mlp_up(u2, w_up)
    w_down = comm.weight("w_down", after=(hs,))
    h2 = _mlp_down(h1, hs, w_down)
    w_gate, w_ple = comm.weight("w_ple_gate", after=(h2,)), comm.weight("w_ple", after=(h2,))
    dh2, dh2b, dwg, dwp, tail_acc = _tail(h2, p, target, s["ple_gate_norm_g"], w_gate, s["b_ple_gate"], w_ple,
                                          s["ple_norm_g"], s["final_norm_g"])
    token = comm.send_grads({"w_ple_gate": dwg, "w_ple": dwp})
    dpre, dh1, dh1b, mlp_acc = _mlp_bwd(dh2, dh2b, pre, h1, s["mlp_norm_g"], w_down, w_up, token=token)
    token = comm.send_grads({
        "w_down": _mm_tn(hs, dh2b, 512, 1024, "dw_down"),
        "w_up": _mm_tn(u2, dpre, 1024, D_FF // N_DEV, "dw_up", column_blocks=True),
        "w_out": [_mm_tn(ys, dh1b, 1024, 512, "dw_out_ssd"), _mm_tn(yc, dh1b, 1024, 512, "dw_out_conf")],
    })
    dy, dz, dv2, opb_acc = _out_proj_bwd(dh1b, y, z, v2, s["ssd_norm_g"], s["conf_ln_g"], s["conf_ln_b"], w_out,
                                         token=token)
    dcv, dcg, dconf_w = _conf_conv_bwd(dv2, cv, cg, conf_w)
    dxbc_act, ddt, d_alog, d_dskip = _ssd_scan_bwd(xbc_act, dt, s_prev, dy, s["A_log"], s["D_skip"])
    dco, ddt_raw, dconv_w, dconv_b, d_dtb = _ssd_pre_bwd(xbc_raw, dxbc_act, ddt, dt_raw, conv_w, s["ssd_conv_b"],
                                                        s["dt_bias"])
    dxbc_raw = _conv4_bwd_data(dco, conv_w)
    dproj = (dz, dxbc_raw, ddt_raw, dcv, dcg)
    token = comm.send_grads({"w_in": [_mm_tn(d, u, 512, 1024, "dw_in_" + n)
                                      for n, d in zip(("z", "xbc", "dt", "cv", "cg"), dproj)]})
    grad_x, inp_acc = _in_proj_bwd(dproj, x, dh1, s["mix_norm_g"], w_in, token=token)
    acc = {"in_proj": inp_acc, "out_proj": opb_acc, "mlp": mlp_acc, "tail": tail_acc, "ssd_conv_b": dconv_b,
           "dt_bias": d_dtb, "A_log": d_alog, "D_skip": d_dskip, "ssd_conv_w": dconv_w, "conf_dw_w": dconf_w}
    return grad_x, acc


def _mesh_pos():
    return lax.axis_index("x"), lax.axis_index("y"), lax.axis_index("c")


def _other_chips(x, y):
    return [(1 - x, y), (x, 1 - y), (1 - x, 1 - y)]


def _all_gather(arrs, name):
    n = len(arrs)

    def body(*refs):
        ins, outs = refs[:n], refs[n:2 * n]
        send_sems, recv_sems, local_sems = refs[2 * n:]
        x, y, c = _mesh_pos()
        me = 4 * x + 2 * y + c
        sibling = (x, y, 1 - c)
        chips = _other_chips(x, y)

        def copy(a, k, block, to, src=None):
            dst = outs[a].at[block]
            return pltpu.make_async_remote_copy(
                src_ref=dst if src is None else src, dst_ref=dst, send_sem=send_sems.at[a, k],
                recv_sem=recv_sems.at[a, k], device_id=to, device_id_type=MESH_ID)

        mine = [pltpu.make_async_copy(ins[a], outs[a].at[me], local_sems.at[a]) for a in range(n)]
        for cp in mine:
            cp.start()
        first = []
        for a in range(n):
            first.append(copy(a, 0, me, sibling, src=ins[a]))
            first += [copy(a, 1 + j, me, (px, py, c), src=ins[a]) for j, (px, py) in enumerate(chips)]
        for cp in first:
            cp.start()
        passed = []
        for j, (px, py) in enumerate(chips):
            for a in range(n):
                blk = 4 * px + 2 * py + c
                copy(a, 1 + j, blk, (x, y, c)).wait_recv()
                cp = copy(a, 4 + j, blk, sibling)
                cp.start()
                passed.append(cp)
        for a in range(n):
            copy(a, 0, 4 * x + 2 * y + (1 - c), (x, y, c)).wait_recv()
        for j, (px, py) in enumerate(chips):
            for a in range(n):
                copy(a, 4 + j, 4 * px + 2 * py + (1 - c), (x, y, c)).wait_recv()
        for cp in first + passed:
            cp.wait_send()
        for cp in mine:
            cp.wait()

    return pl.pallas_call(
        body, name=name,
        in_specs=[HBM_SPEC] * n, out_specs=[HBM_SPEC] * n,
        out_shape=[jax.ShapeDtypeStruct((N_DEV,) + a.shape, a.dtype) for a in arrs],
        scratch_shapes=[pltpu.SemaphoreType.DMA((n, 7)), pltpu.SemaphoreType.DMA((n, 7)), pltpu.SemaphoreType.DMA((n,))],
    )(*arrs)


_PEER_FLIPS = ((0, 0, 1), (1, 0, 0), (0, 1, 0), (1, 1, 0), (1, 0, 1), (0, 1, 1), (1, 1, 1))
SEM_SPEC = pl.BlockSpec(memory_space=pltpu.SEMAPHORE)
ANY_SPEC = pl.BlockSpec(memory_space=pl.ANY)


def _flip(v, d):
    return 1 - v if d else v


def _peers(x, y, c):
    out = []
    for dx, dy, dc in _PEER_FLIPS:
        px, py, pc = _flip(x, dx), _flip(y, dy), _flip(c, dc)
        out.append(((px, py, pc), 4 * px + 2 * py + pc))
    return out


def _exchange_copy(src_ref, land_ref, send_sems, recv_sems, k, peer, peer_block, my_block, by_block, outgoing):
    src = src_ref.at[peer_block] if by_block else src_ref
    dst = land_ref.at[my_block if outgoing else peer_block]
    return pltpu.make_async_remote_copy(src_ref=src, dst_ref=dst, send_sem=send_sems.at[k], recv_sem=recv_sems.at[k],
                                        device_id=peer, device_id_type=MESH_ID)


def _exchange_start(srcs, by_block, name, after, lands=None):
    n = len(srcs)
    copy_own = lands is None
    if copy_own:
        lands = [lax.empty(a.shape if by_block else (N_DEV,) + a.shape, a.dtype) for a in srcs]

    def body(*refs):
        src_refs, land_refs = refs[1:1 + n], refs[1 + n:1 + 2 * n]
        outs, own_sems = refs[1 + 2 * n:-1], refs[-1]
        send, recv, token = outs[:n], outs[n:2 * n], outs[4 * n]
        x, y, c = _mesh_pos()
        me = 4 * x + 2 * y + c
        own = [pltpu.make_async_copy(src_refs[a].at[me] if by_block else src_refs[a], land_refs[a].at[me], own_sems.at[a])
               for a in range(n)] if copy_own else []
        for cp in own:
            cp.start()
        token[...] = jnp.zeros_like(token)
        for cp in own:
            cp.wait()
        for a in range(n):
            for k, (peer, blk) in enumerate(_peers(x, y, c)):
                _exchange_copy(src_refs[a], land_refs[a], send[a], recv[a], k, peer, blk, me, by_block, True).start()

    sems = [pltpu.SemaphoreType.DMA((N_DEV - 1,))] * (2 * n)
    thru = [pltpu.HBM(a.shape, a.dtype) for a in list(srcs) + list(lands)]
    res = pl.pallas_call(
        body, name=name,
        in_specs=[ANY_SPEC] + [HBM_SPEC] * (2 * n),
        out_specs=[SEM_SPEC] * (2 * n) + [HBM_SPEC] * (2 * n) + [pl.BlockSpec(memory_space=pltpu.VMEM)],
        out_shape=sems + thru + [jax.ShapeDtypeStruct((8, 128), f32)],
        scratch_shapes=[pltpu.SemaphoreType.DMA((n,))],
        input_output_aliases={1 + i: 2 * n + i for i in range(2 * n)},
        compiler_params=pltpu.CompilerParams(has_side_effects=pltpu.SideEffectType.DATAFLOW_SIDE_EFFECTING),
    )(after, *[pltpu.with_memory_space_constraint(a, pltpu.HBM) for a in list(srcs) + list(lands)])
    states = [(res[2 * n + a], res[3 * n + a], res[a], res[n + a]) for a in range(n)]
    return states, res[4 * n]


def _exchange_wait(states, by_block, name, after):
    n, na = len(states), len(after)

    def body(*refs):
        src_refs, land_refs = refs[:n], refs[n:2 * n]
        send, recv = refs[2 * n:3 * n], refs[3 * n:4 * n]
        x, y, c = _mesh_pos()
        me = 4 * x + 2 * y + c
        for a in range(n):
            for k, (peer, blk) in enumerate(_peers(x, y, c)):
                _exchange_copy(src_refs[a], land_refs[a], send[a], recv[a], k, peer, blk, me, by_block, True).wait_send()
                _exchange_copy(src_refs[a], land_refs[a], send[a], recv[a], k, peer, blk, me, by_block, False).wait_recv()

    srcs, lands = [s[0] for s in states], [s[1] for s in states]
    res = pl.pallas_call(
        body, name=name,
        in_specs=[HBM_SPEC] * (2 * n) + [SEM_SPEC] * (2 * n) + [ANY_SPEC] * na,
        out_specs=[HBM_SPEC] * (2 * n),
        out_shape=[pltpu.HBM(a.shape, a.dtype) for a in srcs + lands],
        input_output_aliases={i: i for i in range(2 * n)},
        compiler_params=pltpu.CompilerParams(has_side_effects=pltpu.SideEffectType.DATAFLOW_SIDE_EFFECTING),
    )(*srcs, *lands, *[s[2] for s in states], *[s[3] for s in states], *after)
    return list(res[n:2 * n])


def _adamw_math(w, g, m, v):
    m = ADAM_B1 * m + (1.0 - ADAM_B1) * g
    v = ADAM_B2 * v + (1.0 - ADAM_B2) * (g * g)
    m_hat = m / (1.0 - ADAM_B1 ** ADAM_STEP)
    v_hat = v / (1.0 - ADAM_B2 ** ADAM_STEP)
    delta = -ADAM_LR * (m_hat / (jnp.sqrt(v_hat) + ADAM_EPS) + ADAM_WD * w)
    return delta, m, v


def _adamw_big(parts, w, m, v, name):
    rows, cols = w.shape
    tr, tc = (256, cols) if rows % 256 == 0 else (rows, 256)
    nparts = parts.shape[0]

    def body(p_ref, w_ref, m_ref, v_ref, g_ref, d_ref, mo_ref, vo_ref):
        g = p_ref[0].astype(f32)
        for j in range(1, nparts):
            g = g + p_ref[j].astype(f32)
        d, mn, vn = _adamw_math(w_ref[...], g, m_ref[...], v_ref[...])
        g_ref[...] = g
        d_ref[...] = d
        mo_ref[...] = mn
        vo_ref[...] = vn

    tile = pl.BlockSpec((tr, tc), lambda i, j: (i, j))
    shp = jax.ShapeDtypeStruct((rows, cols), f32)
    return pl.pallas_call(
        body, name=name, grid=(rows // tr, cols // tc),
        in_specs=[pl.BlockSpec((nparts, tr, tc), lambda i, j: (0, i, j)), tile, tile, tile],
        out_specs=[tile, tile, tile, tile], out_shape=(shp, shp, shp, shp),
        compiler_params=_params("arbitrary", "arbitrary"),
    )(parts, w, m, v)


PACK_ROWS = 56
_PACK_AT = {
    "mix_norm_g": (0, 0, 1024), "ssd_norm_g": (1, 0, 1024), "conf_ln_g": (2, 0, 1024), "conf_ln_b": (3, 0, 1024),
    "conf_dw_b": (4, 0, 1024), "mlp_norm_g": (5, 0, 1024), "final_norm_g": (7, 0, 1024), "ple_norm_g": (8, 0, 1024),
    "b_ple_gate": (9, 0, 1024), "ple_gate_norm_g": (10, 0, 1024), "dt_bias": (13, 0, 16), "A_log": (13, 128, 16),
    "D_skip": (13, 256, 16),
}
PACK_LOSS_ROW = 6
PACK_CONV_B_ROW = 11
PACK_CONV_W_ROW = 14
PACK_CONF_W_ROW = 24


def _pack_small(acc):
    def body(inp, opb, mlp, tail, cb, dtb, alog, dskip, cw, fw, o_ref):
        o_ref[...] = jnp.zeros_like(o_ref)
        rows = {"mix_norm_g": inp[0:1, :], "mlp_norm_g": mlp[0:1, :],
                "ssd_norm_g": opb[OPB_SSD_G:OPB_SSD_G + 1, :], "conf_ln_g": opb[OPB_LN_G:OPB_LN_G + 1, :],
                "conf_ln_b": opb[OPB_LN_B:OPB_LN_B + 1, :], "conf_dw_b": opb[OPB_CONV_B:OPB_CONV_B + 1, :],
                "final_norm_g": tail[TAIL_FINAL_G:TAIL_FINAL_G + 1, :], "ple_norm_g": tail[TAIL_PLE_G:TAIL_PLE_G + 1, :],
                "b_ple_gate": tail[TAIL_GATE_B:TAIL_GATE_B + 1, :],
                "ple_gate_norm_g": tail[TAIL_GATE_NORM_G:TAIL_GATE_NORM_G + 1, :],
                "dt_bias": dtb[...], "A_log": alog[...], "D_skip": dskip[...]}
        for name, val in rows.items():
            r, lo, width = _PACK_AT[name]
            o_ref[r:r + 1, lo:lo + width] = val
        o_ref[PACK_LOSS_ROW:PACK_LOSS_ROW + 1, :] = tail[TAIL_LOSS:TAIL_LOSS + 1, :]
        o_ref[PACK_CONV_B_ROW:PACK_CONV_B_ROW + 1, :] = cb[:, 0:1024]
        o_ref[PACK_CONV_B_ROW + 1:PACK_CONV_B_ROW + 2, 0:512] = cb[:, 1024:XBC_WIDTH]
        for k in range(SSD_CONV):
            o_ref[PACK_CONV_W_ROW + k:PACK_CONV_W_ROW + k + 1, :] = cw[k:k + 1, 0:1024]
            o_ref[PACK_CONV_W_ROW + SSD_CONV + k:PACK_CONV_W_ROW + SSD_CONV + k + 1, 0:512] = cw[k:k + 1, 1024:XBC_WIDTH]
        o_ref[PACK_CONF_W_ROW:PACK_CONF_W_ROW + 32, :] = fw[...]

    return pl.pallas_call(body, name="pack_small", out_shape=jax.ShapeDtypeStruct((PACK_ROWS, 1024), f32))(
        acc["in_proj"], acc["out_proj"], acc["mlp"], acc["tail"], acc["ssd_conv_b"], acc["dt_bias"], acc["A_log"],
        acc["D_skip"], acc["ssd_conv_w"], acc["conf_dw_w"])


def _small_update(all_small, w, m, v):
    names = _REPLICATED

    def body(all_ref, *refs):
        ins, outs = refs[:3 * len(names)], refs[3 * len(names):]
        s = all_ref[0]
        for j in range(1, N_DEV):
            s = s + all_ref[j]
        outs[0][...] = s[PACK_LOSS_ROW:PACK_LOSS_ROW + 1, 0:1]
        outs[1][...] = jnp.concatenate([s[PACK_CONV_W_ROW:PACK_CONV_W_ROW + SSD_CONV, :],
                                        s[PACK_CONV_W_ROW + SSD_CONV:PACK_CONV_W_ROW + 2 * SSD_CONV, 0:512]], axis=1)
        outs[2][...] = s[PACK_CONF_W_ROW:PACK_CONF_W_ROW + CONF_KERNEL, :]
        for i, name in enumerate(names):
            if name == "ssd_conv_b":
                g = jnp.concatenate([s[PACK_CONV_B_ROW:PACK_CONV_B_ROW + 1, :],
                                     s[PACK_CONV_B_ROW + 1:PACK_CONV_B_ROW + 2, 0:512]], axis=1)
            else:
                r, lo, width = _PACK_AT[name]
                g = s[r:r + 1, lo:lo + width]
            d, mn, vn = _adamw_math(ins[3 * i][...], g, ins[3 * i + 1][...], ins[3 * i + 2][...])
            for o_ref, val in zip(outs[3 + 4 * i:7 + 4 * i], (g, d, mn, vn)):
                o_ref[...] = val

    shapes = [jax.ShapeDtypeStruct((1, 1), f32), jax.ShapeDtypeStruct((SSD_CONV, XBC_WIDTH), f32),
              jax.ShapeDtypeStruct((CONF_KERNEL, D_MODEL), f32)]
    operands = []
    for name in names:
        operands += [w[name], m[name], v[name]]
        shapes += [jax.ShapeDtypeStruct(w[name].shape, f32)] * 4
    res = pl.pallas_call(body, name="small_update", out_shape=shapes)(all_small, *operands)
    per_name = {name: tuple(res[3 + 4 * i:7 + 4 * i]) for i, name in enumerate(names)}
    return res[0], per_name, res[1], res[2]


def _adamw_filters(g, w, m, v):
    n = len(g)

    def body(*refs):
        ins, outs = refs[:4 * n], refs[4 * n:]
        for i in range(n):
            d, mn, vn = _adamw_math(ins[4 * i + 1][...], ins[4 * i][...], ins[4 * i + 2][...], ins[4 * i + 3][...])
            for o_ref, val in zip(outs[3 * i:3 * i + 3], (d, mn, vn)):
                o_ref[...] = val

    operands, shapes = [], []
    for i in range(n):
        operands += [g[i], w[i], m[i], v[i]]
        shapes += [jax.ShapeDtypeStruct(w[i].shape, f32)] * 3
    res = pl.pallas_call(body, name="adamw_filters", out_shape=shapes)(*operands)
    return [tuple(res[3 * i:3 * i + 3]) for i in range(n)]


_REPLICATED = ("mix_norm_g", "ssd_conv_b", "dt_bias", "A_log", "D_skip", "ssd_norm_g", "conf_dw_b", "conf_ln_g",
               "conf_ln_b", "mlp_norm_g", "ple_gate_norm_g", "b_ple_gate", "ple_norm_g", "final_norm_g")
_CONV_WEIGHTS = ("ssd_conv_w", "conf_dw_w")
_BIG = ("w_in", "w_out", "w_up", "w_down", "w_ple_gate", "w_ple")
_WEIGHT_ORDER = ("mix_norm_g", "w_in", "ssd_conv_w", "ssd_conv_b", "dt_bias", "A_log", "D_skip", "ssd_norm_g", "conf_dw_w",
                 "conf_dw_b", "conf_ln_g", "conf_ln_b", "w_out", "mlp_norm_g", "w_up", "w_down", "ple_gate_norm_g",
                 "w_ple_gate", "b_ple_gate", "w_ple", "ple_norm_g", "final_norm_g")


def _blocks_of_columns(a):
    r, c8 = a.shape
    return jnp.transpose(a.reshape(r, N_DEV, c8 // N_DEV), (1, 0, 2))


def _columns_of_blocks(a):
    _, r, c = a.shape
    return jnp.transpose(a, (1, 0, 2)).reshape(r, N_DEV * c)


_LATER = ("w_out", "w_up", "w_down", "w_ple_gate", "w_ple")
_WHOLE = {
    "w_out": lambda a: a.reshape(2048, D_MODEL),
    "w_up": lambda a: a,
    "w_down": lambda a: a.reshape(D_FF, D_MODEL),
    "w_ple_gate": lambda a: a.reshape(D_MODEL, D_MODEL),
    "w_ple": _columns_of_blocks,
}
_BY_BLOCK = {
    "w_in": lambda g: jnp.concatenate(g, axis=0).reshape(N_DEV, IN_WIDTH // N_DEV, D_MODEL),
    "w_out": lambda g: jnp.concatenate(g, axis=0).reshape(N_DEV, 256, D_MODEL),
    "w_up": lambda g: g,
    "w_down": lambda g: g.reshape(N_DEV, 512, D_MODEL),
    "w_ple_gate": lambda g: g.astype(bf16).reshape(N_DEV, 128, D_MODEL),
    "w_ple": lambda g: _blocks_of_columns(g.astype(bf16)),
}


class _StepComm:
    def __init__(self, me, ready, gathers, token):
        self.me, self.ready, self.gathers, self.token = me, ready, gathers, token
        self.sent = []

    def start_token(self):
        return self.token

    def weight(self, name, after=()):
        if name not in self.ready:
            (land,) = _exchange_wait([self.gathers[name]], False, "gather_wait_" + name, list(after))
            self.ready[name] = _WHOLE[name](land)
        return self.ready[name]

    def send_grads(self, grads):
        names = list(grads)
        blocks = [_BY_BLOCK[n](grads[n]) for n in names]
        states, self.token = _exchange_start(blocks, True, "scatter_start_" + names[0], self.token)
        self.sent.append((names, states))
        return self.token


def kernel(x, p, mix_norm_g, w_in, ssd_conv_w, ssd_conv_b, dt_bias, A_log, D_skip, ssd_norm_g, conf_dw_w, conf_dw_b, conf_ln_g, conf_ln_b, w_out, mlp_norm_g, w_up, w_down, ple_gate_norm_g, w_ple_gate, b_ple_gate, w_ple, ple_norm_g, final_norm_g, loss_target, m_mix_norm_g, m_w_in, m_ssd_conv_w, m_ssd_conv_b, m_dt_bias, m_A_log, m_D_skip, m_ssd_norm_g, m_conf_dw_w, m_conf_dw_b, m_conf_ln_g, m_conf_ln_b, m_w_out, m_mlp_norm_g, m_w_up, m_w_down, m_ple_gate_norm_g, m_w_ple_gate, m_b_ple_gate, m_w_ple, m_ple_norm_g, m_final_norm_g, v_mix_norm_g, v_w_in, v_ssd_conv_w, v_ssd_conv_b, v_dt_bias, v_A_log, v_D_skip, v_ssd_norm_g, v_conf_dw_w, v_conf_dw_b, v_conf_ln_g, v_conf_ln_b, v_w_out, v_mlp_norm_g, v_w_up, v_w_down, v_ple_gate_norm_g, v_w_ple_gate, v_b_ple_gate, v_w_ple, v_ple_norm_g, v_final_norm_g):
    wts = dict(mix_norm_g=mix_norm_g, w_in=w_in, ssd_conv_w=ssd_conv_w, ssd_conv_b=ssd_conv_b, dt_bias=dt_bias, A_log=A_log,
               D_skip=D_skip, ssd_norm_g=ssd_norm_g, conf_dw_w=conf_dw_w, conf_dw_b=conf_dw_b, conf_ln_g=conf_ln_g,
               conf_ln_b=conf_ln_b, w_out=w_out, mlp_norm_g=mlp_norm_g, w_up=w_up, w_down=w_down,
               ple_gate_norm_g=ple_gate_norm_g, w_ple_gate=w_ple_gate, b_ple_gate=b_ple_gate, w_ple=w_ple,
               ple_norm_g=ple_norm_g, final_norm_g=final_norm_g)
    mom1 = dict(mix_norm_g=m_mix_norm_g, w_in=m_w_in, ssd_conv_w=m_ssd_conv_w, ssd_conv_b=m_ssd_conv_b, dt_bias=m_dt_bias,
                A_log=m_A_log, D_skip=m_D_skip, ssd_norm_g=m_ssd_norm_g, conf_dw_w=m_conf_dw_w, conf_dw_b=m_conf_dw_b,
                conf_ln_g=m_conf_ln_g, conf_ln_b=m_conf_ln_b, w_out=m_w_out, mlp_norm_g=m_mlp_norm_g, w_up=m_w_up,
                w_down=m_w_down, ple_gate_norm_g=m_ple_gate_norm_g, w_ple_gate=m_w_ple_gate, b_ple_gate=m_b_ple_gate,
                w_ple=m_w_ple, ple_norm_g=m_ple_norm_g, final_norm_g=m_final_norm_g)
    mom2 = dict(mix_norm_g=v_mix_norm_g, w_in=v_w_in, ssd_conv_w=v_ssd_conv_w, ssd_conv_b=v_ssd_conv_b, dt_bias=v_dt_bias,
                A_log=v_A_log, D_skip=v_D_skip, ssd_norm_g=v_ssd_norm_g, conf_dw_w=v_conf_dw_w, conf_dw_b=v_conf_dw_b,
                conf_ln_g=v_conf_ln_g, conf_ln_b=v_conf_ln_b, w_out=v_w_out, mlp_norm_g=v_mlp_norm_g, w_up=v_w_up,
                w_down=v_w_down, ple_gate_norm_g=v_ple_gate_norm_g, w_ple_gate=v_w_ple_gate, b_ple_gate=v_b_ple_gate,
                w_ple=v_w_ple, ple_norm_g=v_ple_norm_g, final_norm_g=v_final_norm_g)
    x_pos, y_pos, c_pos = _mesh_pos()
    me = 4 * x_pos + 2 * y_pos + c_pos

    first = _all_gather([wts["w_in"][0].T.astype(bf16), wts["ssd_conv_w"][0], wts["conf_dw_w"][0]], "gather_first")
    ready = {
        "w_in": first[0].reshape(IN_WIDTH, D_MODEL),
        "ssd_conv_w": jnp.pad(_columns_of_blocks(first[1]), ((0, 8 - SSD_CONV), (0, 0))),
        "conf_dw_w": jnp.pad(_columns_of_blocks(first[2]), ((0, 32 - CONF_KERNEL), (0, 0))),
    }
    shards = [wts[n][0].astype(bf16) for n in _LATER]
    states, token = _exchange_start(shards, False, "gather_start", first[1])
    comm = _StepComm(me, ready, dict(zip(_LATER, states)), token)
    small = {n: wts[n].reshape(1, -1) for n in _REPLICATED}

    grad_x, acc = _local_step(x[0], p[0, 0], loss_target[0], small, comm)

    packed = _pack_small(acc)
    own_slot = lax.dynamic_update_slice(lax.empty((N_DEV,) + packed.shape, f32), packed[None], (me, 0, 0))
    (small_state,), _ = _exchange_start([packed], False, "small_start", comm.token, lands=[own_slot])
    grads, delta, new_m, new_v = {}, {}, {}, {}

    def adamw_big(names, sent, after):
        lands = _exchange_wait(sent, True, "scatter_wait_" + names[0], after)
        for n, land in zip(names, lands):
            view = (lambda a: a[0].T) if n == "w_in" else (lambda a: a[0])
            back = (lambda a: a.T[None]) if n == "w_in" else (lambda a: a[None])
            out = _adamw_big(land, view(wts[n]), view(mom1[n]), view(mom2[n]), "adamw_" + n)
            grads[n], delta[n], new_m[n], new_v[n] = [back(a) for a in out]

    for names, sent in comm.sent[:-1]:
        adamw_big(names, sent, [grad_x])
    (all_small,) = _exchange_wait([small_state], False, "small_wait", [delta[n] for n in _LATER])
    as_row = lambda d: {n: d[n].reshape(1, -1) for n in _REPLICATED}
    loss, per_name, conv_w_sum, conf_w_sum = _small_update(all_small, as_row(wts), as_row(mom1), as_row(mom2))
    for n in _REPLICATED:
        grads[n], delta[n], new_m[n], new_v[n] = [a.reshape(wts[n].shape) for a in per_name[n]]
    filt_g = [lax.dynamic_slice_in_dim(conv_w_sum, me * 192, 192, axis=1),
              lax.dynamic_slice_in_dim(conf_w_sum, me * 128, 128, axis=1)]
    filt = _adamw_filters(filt_g, [wts[n][0] for n in _CONV_WEIGHTS], [mom1[n][0] for n in _CONV_WEIGHTS],
                          [mom2[n][0] for n in _CONV_WEIGHTS])
    for n, g, (d, mn, vn) in zip(_CONV_WEIGHTS, filt_g, filt):
        grads[n], delta[n], new_m[n], new_v[n] = g[None], d[None], mn[None], vn[None]
    adamw_big(*comm.sent[-1], [delta[n] for n in _LATER])

    return (loss.reshape(()), grad_x[None], *[grads[n] for n in _WEIGHT_ORDER], *[delta[n] for n in _WEIGHT_ORDER],
            *[new_m[n] for n in _WEIGHT_ORDER], *[new_v[n] for n in _WEIGHT_ORDER])
```

```python
import functools

import jax
import jax.numpy as jnp
from jax import lax
from jax.experimental import pallas as pl
from jax.experimental.pallas import tpu as pltpu

f32 = jnp.float32
bf16 = jnp.bfloat16

EPS = 1e-6
D_MODEL = 1024
SSD_HEADS = 16
SSD_HEAD_DIM = 64
SSD_GROUPS = 2
SSD_STATE = 128
SSD_CONV = 4
CHUNK = 128
XBC_WIDTH = 1536
CONF_KERNEL = 31
D_FF = 4096
PLE_DIM = 256
IN_WIDTH = 4624
N_DEV = 8
SEG_Z = (0, 1024)
SEG_XBC = (1024, 2560)
SEG_DT = (2560, 2576)
SEG_CV = (2576, 3600)
SEG_CG = (3600, 4624)
IN_SEGS = (SEG_Z, SEG_XBC, SEG_DT, SEG_CV, SEG_CG)

ADAM_LR = 0.001
ADAM_B1 = 0.9
ADAM_B2 = 0.999
ADAM_EPS = 1e-08
ADAM_WD = 0.01
ADAM_STEP = 10

VMEM_LIMIT_BYTES = 56 * 1024 * 1024
TOKEN_TILE = 512
SMALL_ROWS = 16

HBM_SPEC = pl.BlockSpec(memory_space=pltpu.HBM)
MESH_ID = pl.DeviceIdType.MESH


def _params(*semantics):
    return pltpu.CompilerParams(dimension_semantics=semantics, vmem_limit_bytes=VMEM_LIMIT_BYTES)


def _nn(a, b):
    return lax.dot_general(a, b, (((1,), (0,)), ((), ())), preferred_element_type=f32)


def _nt(a, b):
    return lax.dot_general(a, b, (((1,), (1,)), ((), ())), preferred_element_type=f32)


def _tn(a, b):
    return lax.dot_general(a, b, (((0,), (0,)), ((), ())), preferred_element_type=f32)


@jax.custom_vjp
def bnn(a, b):
    return _nn(a.astype(bf16), b.astype(bf16))


def _bnn_fwd(a, b):
    ab, bb = a.astype(bf16), b.astype(bf16)
    return _nn(ab, bb), (ab, bb)


def _bnn_bwd(res, g):
    ab, bb = res
    gb = g.astype(bf16)
    return _nt(gb, bb), _tn(ab, gb)


bnn.defvjp(_bnn_fwd, _bnn_bwd)


@jax.custom_vjp
def bnt(a, b):
    return _nt(a.astype(bf16), b.astype(bf16))


def _bnt_fwd(a, b):
    ab, bb = a.astype(bf16), b.astype(bf16)
    return _nt(ab, bb), (ab, bb)


def _bnt_bwd(res, g):
    ab, bb = res
    gb = g.astype(bf16)
    return _nn(gb, bb), _tn(gb, ab)


bnt.defvjp(_bnt_fwd, _bnt_bwd)


@jax.custom_vjp
def btn(a, b):
    return _tn(a.astype(bf16), b.astype(bf16))


def _btn_fwd(a, b):
    ab, bb = a.astype(bf16), b.astype(bf16)
    return _tn(ab, bb), (ab, bb)


def _btn_bwd(res, g):
    ab, bb = res
    gb = g.astype(bf16)
    return _nt(bb, gb), _nn(ab, gb)


btn.defvjp(_btn_fwd, _btn_bwd)

CUMSUM_PASSES = 3
EXPAND_PASSES = 2


def _bf16_pieces(a, passes):
    pieces, rest = [], a
    for _ in range(passes):
        piece = rest.astype(bf16)
        pieces.append(piece)
        rest = rest - piece.astype(f32)
    return pieces


def _make_dot01(form, passes):
    fwd, bwd = {"right": (lambda b, p: _nn(p, b), lambda b, g: _nt(g, b)),
                "left": (lambda b, p: _nn(b, p), lambda b, g: _tn(b, g)),
                "tn": (lambda b, p: _tn(p, b), lambda b, g: _nt(b, g))}[form]

    def run(op, b, v):
        out = None
        for piece in _bf16_pieces(v, passes):
            term = op(b, piece)
            out = term if out is None else out + term
        return out

    @jax.custom_vjp
    def product(b, v):
        return run(fwd, b, v)

    product.defvjp(lambda b, v: (run(fwd, b, v), b), lambda b, g: (jnp.zeros_like(b), run(bwd, b, g)))
    return product


_DOT01 = {(form, passes): _make_dot01(form, passes)
          for form, passes in (("left", CUMSUM_PASSES), ("tn", CUMSUM_PASSES), ("right", EXPAND_PASSES))}


def _dot01(b01, a, form, passes):
    return _DOT01[form, passes](b01.astype(bf16), a)


def _rms(x, g):
    return x * lax.rsqrt(jnp.mean(x * x, axis=-1, keepdims=True) + EPS) * g


def _gated_norm(y, z, g):
    v = y * jax.nn.silu(z)
    half = v.shape[-1] // SSD_GROUPS
    parts = []
    for k in range(SSD_GROUPS):
        vk = v[:, k * half:(k + 1) * half]
        parts.append(vk * lax.rsqrt(jnp.mean(vk * vk, axis=-1, keepdims=True) + EPS))
    return jnp.concatenate(parts, axis=-1) * g


def _ln_silu(v, g, b):
    mu = jnp.mean(v, axis=-1, keepdims=True)
    xc = v - mu
    y = xc * lax.rsqrt(jnp.mean(xc * xc, axis=-1, keepdims=True) + EPS) * g + b
    return jax.nn.silu(y)


def _acc_init(i, *refs):
    @pl.when(i == 0)
    def _():
        for r in refs:
            r[...] = jnp.zeros_like(r)


def _after_token(body, token):
    if token is None:
        return body, [], []

    def body_after(tok_ref, *refs):
        del tok_ref
        body(*refs)

    return body_after, [token], [pl.BlockSpec(memory_space=pl.ANY)]


def _row_spec(tm, n):
    return pl.BlockSpec((tm, n), lambda i: (i, 0))


def _const_spec(shape):
    nd = len(shape)
    return pl.BlockSpec(shape, lambda i: (0,) * nd)


def _prev_halo_spec(tm, halo, n):
    return pl.BlockSpec((halo, n), lambda i: (jnp.maximum(i * (tm // halo) - 1, 0), 0))


def _next_halo_spec(tm, halo, n, t):
    return pl.BlockSpec((halo, n), lambda i: (jnp.minimum((i + 1) * (tm // halo), t // halo - 1), 0))


def _in_proj(x, g1, w_in_p, token=None):
    t = x.shape[0]
    tm = min(TOKEN_TILE, t)

    def body(x_ref, g_ref, w_ref, u_ref, z_ref, xbc_ref, dt_ref, cv_ref, cg_ref):
        u = _rms(x_ref[...], g_ref[...]).astype(bf16)
        u_ref[...] = u
        for (lo, hi), o_ref in zip(IN_SEGS, (z_ref, xbc_ref, dt_ref, cv_ref, cg_ref)):
            o_ref[...] = _nt(u, w_ref[lo:hi, :])

    widths = [hi - lo for lo, hi in IN_SEGS]
    outs = [jax.ShapeDtypeStruct((t, D_MODEL), bf16)] + [jax.ShapeDtypeStruct((t, n), f32) for n in widths]
    body, tok, tok_spec = _after_token(body, token)
    return pl.pallas_call(
        body, name="in_proj", grid=(t // tm,),
        in_specs=tok_spec + [_row_spec(tm, D_MODEL), _const_spec((1, D_MODEL)), _const_spec((IN_WIDTH, D_MODEL))],
        out_specs=[_row_spec(tm, D_MODEL)] + [_row_spec(tm, n) for n in widths],
        out_shape=outs, compiler_params=_params("arbitrary"),
    )(*tok, x, g1, w_in_p)


ROW_BLOCK = 16
LANE_CHUNK = 512


def _blocks(rows, cols):
    return [(slice(r, r + ROW_BLOCK), slice(c, c + LANE_CHUNK))
            for c in range(0, cols, LANE_CHUNK) for r in range(0, rows, ROW_BLOCK)]


def _conv4_block(ext_ref, w_ref, b_ref, rs, ls):
    acc = b_ref[:, ls] + w_ref[0:1, ls] * ext_ref[rs.start + 8 - 3:rs.stop + 8 - 3, ls]
    for k in range(1, SSD_CONV):
        acc = acc + w_ref[k:k + 1, ls] * ext_ref[rs.start + 8 - 3 + k:rs.stop + 8 - 3 + k, ls]
    return acc


def _ssd_pre(xbc_raw, dt_raw, cw, cb, dt_bias):
    t = xbc_raw.shape[0]
    tm = min(TOKEN_TILE, t)

    def body(cur_ref, halo_ref, dt_ref, w_ref, b_ref, dtb_ref, act_ref, dto_ref, ext_ref):
        i = pl.program_id(0)
        ext_ref[0:8, :] = jnp.where(i == 0, 0.0, halo_ref[...])
        ext_ref[8:, :] = cur_ref[...]
        for rs, ls in _blocks(tm, XBC_WIDTH):
            act_ref[rs, ls] = jax.nn.silu(_conv4_block(ext_ref, w_ref, b_ref, rs, ls))
        dto_ref[...] = jax.nn.softplus(dt_ref[...] + dtb_ref[...])

    return pl.pallas_call(
        body, name="ssd_pre", grid=(t // tm,),
        in_specs=[_row_spec(tm, XBC_WIDTH), _prev_halo_spec(tm, 8, XBC_WIDTH), _row_spec(tm, SSD_HEADS),
                  _const_spec((8, XBC_WIDTH)), _const_spec((1, XBC_WIDTH)), _const_spec((1, SSD_HEADS))],
        out_specs=[_row_spec(tm, XBC_WIDTH), _row_spec(tm, SSD_HEADS)],
        out_shape=(jax.ShapeDtypeStruct((t, XBC_WIDTH), f32), jax.ShapeDtypeStruct((t, SSD_HEADS), f32)),
        scratch_shapes=[pltpu.VMEM((tm + 8, XBC_WIDTH), f32)],
        compiler_params=_params("arbitrary"),
    )(xbc_raw, xbc_raw, dt_raw, cw, cb, dt_bias)


def _ssd_consts():
    r = lax.broadcasted_iota(jnp.int32, (CHUNK, CHUNK), 0)
    c = lax.broadcasted_iota(jnp.int32, (CHUNK, CHUNK), 1)
    causal = r >= c
    tril = causal.astype(f32)
    triu = (r <= c).astype(f32)
    hh = lax.broadcasted_iota(jnp.int32, (SSD_HEADS, D_MODEL), 0)
    jj = lax.broadcasted_iota(jnp.int32, (SSD_HEADS, D_MODEL), 1)
    expand = (lax.shift_right_logical(jj, 6) == hh).astype(f32)
    lane = lax.broadcasted_iota(jnp.int32, (1, 2 * SSD_HEAD_DIM), 1)
    m0 = (lane < SSD_HEAD_DIM).astype(f32)
    return causal, tril, triu, expand, m0, 1.0 - m0


def _ssd_chunk(xs, bm, cm, dt, s_prev, a_log, d_skip, consts):
    causal, tril, triu, expand, m0, m1 = consts
    a = dt * (-jnp.exp(a_log))
    cs = _dot01(tril, a, "left", CUMSUM_PASSES)
    cs_t = _dot01(triu, a, "tn", CUMSUM_PASSES)
    cs_last = cs[CHUNK - 1:CHUNK, :]
    per_head = jnp.concatenate([dt, jnp.exp(cs_last - cs), jnp.exp(cs), jnp.broadcast_to(jnp.exp(cs_last), (8, SSD_HEADS)),
                                jnp.broadcast_to(d_skip, (8, SSD_HEADS))], axis=0)
    per_channel = _dot01(expand, per_head, "right", EXPAND_PASSES)
    dt_e, dec_end, dec_start = (per_channel[i * CHUNK:(i + 1) * CHUNK] for i in range(3))
    chunk_dec = per_channel[3 * CHUNK:3 * CHUNK + 1]
    d_e = per_channel[3 * CHUNK + 8:3 * CHUNK + 9]
    xc = xs * dt_e
    x_dec = xc * dec_end
    gw = D_MODEL // SSD_GROUPS
    ys, states = [], []
    for g in range(SSD_GROUPS):
        bg = bm[:, g * SSD_STATE:(g + 1) * SSD_STATE]
        cg = cm[:, g * SSD_STATE:(g + 1) * SSD_STATE]
        sp = s_prev[:, g * gw:(g + 1) * gw]
        states.append(sp * chunk_dec[:, g * gw:(g + 1) * gw] + btn(bg, x_dec[:, g * gw:(g + 1) * gw]))
        y_off = bnn(cg, sp) * dec_start[:, g * gw:(g + 1) * gw]
        scores = bnt(cg, bg)
        pieces = []
        for pr in range(gw // (2 * SSD_HEAD_DIM)):
            lo = g * gw + pr * 2 * SSD_HEAD_DIM
            xp = xc[:, lo:lo + 2 * SSD_HEAD_DIM]
            acc = None
            for h, mk in ((lo // SSD_HEAD_DIM, m0), (lo // SSD_HEAD_DIM + 1, m1)):
                seg = cs[:, h:h + 1] - cs_t[h:h + 1, :]
                mat = scores * jnp.exp(jnp.where(causal, seg, -jnp.inf))
                term = bnn(mat, xp * mk)
                acc = term if acc is None else acc + term
            pieces.append(acc)
        ys.append(jnp.concatenate(pieces, axis=-1) + y_off)
    y = jnp.concatenate(ys, axis=-1) + xs * d_e
    return y, jnp.concatenate(states, axis=-1)


def _ssd_scan(xbc_act, dt, a_log, d_skip):
    t = xbc_act.shape[0]
    nc = t // CHUNK

    def body(xbc_ref, dt_ref, al_ref, dk_ref, y_ref, sp_ref, state_ref):
        c = pl.program_id(0)

        @pl.when(c == 0)
        def _():
            state_ref[...] = jnp.zeros_like(state_ref)

        s_prev = state_ref[...]
        sp_ref[...] = s_prev
        y, s_next = _ssd_chunk(xbc_ref[:, 0:1024], xbc_ref[:, 1024:1280], xbc_ref[:, 1280:1536], dt_ref[...],
                               s_prev, al_ref[...], dk_ref[...], _ssd_consts())
        y_ref[...] = y
        state_ref[...] = s_next

    return pl.pallas_call(
        body, name="ssd_scan", grid=(nc,),
        in_specs=[_row_spec(CHUNK, XBC_WIDTH), _row_spec(CHUNK, SSD_HEADS), _const_spec((1, SSD_HEADS)),
                  _const_spec((1, SSD_HEADS))],
        out_specs=[_row_spec(CHUNK, D_MODEL), pl.BlockSpec((None, SSD_STATE, D_MODEL), lambda c: (c, 0, 0))],
        out_shape=(jax.ShapeDtypeStruct((t, D_MODEL), f32), jax.ShapeDtypeStruct((nc, SSD_STATE, D_MODEL), f32)),
        scratch_shapes=[pltpu.VMEM((SSD_STATE, D_MODEL), f32)],
        compiler_params=_params("arbitrary"),
    )(xbc_act, dt, a_log, d_skip)


CONV_HALO = 32
CONV_BLOCK = 64
LANE_TILE = 128
SUBLANES = 8


def _fill_glu_ext(ext_ref, sg_ref, cv_ref, cg_ref, hcv_ref, hcg_ref, i):
    ext_ref[0:CONV_HALO, :] = jnp.where(i == 0, 0.0, hcv_ref[...] * jax.nn.sigmoid(hcg_ref[...]))
    sg = jax.nn.sigmoid(cg_ref[...])
    if sg_ref is not None:
        sg_ref[...] = sg
    ext_ref[CONV_HALO:, :] = cv_ref[...] * sg


def _shifted_copies(src_ref, dst_ref, rows):
    for s in range(1, SUBLANES):
        dst_ref[s, 0:rows, :] = src_ref[pl.ds(s, rows), :]


def _tap_source(src_ref, shifted_ref, offset):
    s = offset % SUBLANES
    return (src_ref if s == 0 else shifted_ref.at[s]), offset - s


def _conv_rows(src_ref, shifted_ref, taps, offsets, lanes, r0, init):
    accs = [init] * (CONV_BLOCK // SUBLANES)
    for k, off in enumerate(offsets):
        ref, base = _tap_source(src_ref, shifted_ref, off)
        for j in range(len(accs)):
            accs[j] = accs[j] + taps[k] * ref[pl.ds(r0 + base + SUBLANES * j, SUBLANES), lanes]
    return accs


def _conf_conv(cv, cg, w, b):
    t = cv.shape[0]
    tm = min(TOKEN_TILE, t)
    ext_rows = tm + CONV_HALO
    offsets = [CONV_HALO - (CONF_KERNEL - 1) + k for k in range(CONF_KERNEL)]

    def body(cv_ref, cg_ref, hcv_ref, hcg_ref, w_ref, b_ref, o_ref, ext_ref, shifted_ref):
        _fill_glu_ext(ext_ref, None, cv_ref, cg_ref, hcv_ref, hcg_ref, pl.program_id(0))
        _shifted_copies(ext_ref, shifted_ref, ext_rows - SUBLANES)
        for lb in range(1024 // LANE_TILE):
            lanes = slice(lb * LANE_TILE, (lb + 1) * LANE_TILE)
            taps = [jnp.broadcast_to(w_ref[k:k + 1, lanes], (SUBLANES, LANE_TILE)) for k in range(CONF_KERNEL)]
            bias = jnp.broadcast_to(b_ref[:, lanes], (SUBLANES, LANE_TILE))

            def rows(rb, carry, lanes=lanes, taps=taps, bias=bias):
                r0 = pl.multiple_of(rb * CONV_BLOCK, CONV_BLOCK)
                accs = _conv_rows(ext_ref, shifted_ref, taps, offsets, lanes, r0, bias)
                for j, a in enumerate(accs):
                    o_ref[pl.ds(r0 + SUBLANES * j, SUBLANES), lanes] = a
                return carry

            lax.fori_loop(0, tm // CONV_BLOCK, rows, 0)

    return pl.pallas_call(
        body, name="conf_conv", grid=(t // tm,),
        in_specs=[_row_spec(tm, 1024), _row_spec(tm, 1024), _prev_halo_spec(tm, CONV_HALO, 1024),
                  _prev_halo_spec(tm, CONV_HALO, 1024), _const_spec((32, 1024)), _const_spec((1, 1024))],
        out_specs=_row_spec(tm, 1024), out_shape=jax.ShapeDtypeStruct((t, 1024), f32),
        scratch_shapes=[pltpu.VMEM((ext_rows, 1024), f32), pltpu.VMEM((SUBLANES, ext_rows, 1024), f32)],
        compiler_params=_params("arbitrary"),
    )(cv, cg, cv, cg, w, b)


def _out_proj(y, z, v2, x, g_ssd, ln_g, ln_b, w_out, g2):
    t = x.shape[0]
    tm = min(TOKEN_TILE, t)

    def body(y_ref, z_ref, v_ref, x_ref, gs_ref, lg_ref, lb_ref, w_ref, g2_ref, ys_ref, yc_ref, h1_ref, u2_ref):
        ys = _gated_norm(y_ref[...], z_ref[...], gs_ref[...]).astype(bf16)
        yc = _ln_silu(v_ref[...], lg_ref[...], lb_ref[...]).astype(bf16)
        ys_ref[...] = ys
        yc_ref[...] = yc
        h1 = x_ref[...] + _nn(ys, w_ref[0:1024, :]) + _nn(yc, w_ref[1024:2048, :])
        h1_ref[...] = h1
        u2_ref[...] = _rms(h1, g2_ref[...]).astype(bf16)

    vec = _const_spec((1, 1024))
    row = _row_spec(tm, 1024)
    return pl.pallas_call(
        body, name="out_proj", grid=(t // tm,),
        in_specs=[row, row, row, row, vec, vec, vec, _const_spec((2048, 1024)), vec],
        out_specs=[row, row, row, row],
        out_shape=(jax.ShapeDtypeStruct((t, 1024), bf16), jax.ShapeDtypeStruct((t, 1024), bf16),
                   jax.ShapeDtypeStruct((t, 1024), f32), jax.ShapeDtypeStruct((t, 1024), bf16)),
        compiler_params=_params("arbitrary"),
    )(y, z, v2, x, g_ssd, ln_g, ln_b, w_out, g2)


def _mlp_up(u2, w_up):
    t = u2.shape[0]
    tm = min(TOKEN_TILE, t)
    blk = D_FF // N_DEV

    def body(u_ref, w_ref, pre_ref, hs_ref):
        u = u_ref[...]
        for k in range(N_DEV):
            pre = _nn(u, w_ref[k])
            pre_ref[:, k * blk:(k + 1) * blk] = pre.astype(bf16)
            r = jnp.maximum(pre, 0.0)
            hs_ref[:, k * blk:(k + 1) * blk] = (r * r).astype(bf16)

    return pl.pallas_call(
        body, name="mlp_up", grid=(t // tm,),
        in_specs=[_row_spec(tm, D_MODEL), _const_spec((N_DEV, D_MODEL, D_FF // N_DEV))],
        out_specs=[_row_spec(tm, D_FF), _row_spec(tm, D_FF)],
        out_shape=(jax.ShapeDtypeStruct((t, D_FF), bf16), jax.ShapeDtypeStruct((t, D_FF), bf16)),
        compiler_params=_params("arbitrary"),
    )(u2, w_up)


def _mlp_down(h1, hs, w_down):
    t = h1.shape[0]
    tm = min(TOKEN_TILE, t)

    def body(h_ref, hs_ref, w_ref, o_ref):
        o_ref[...] = h_ref[...] + _nn(hs_ref[...], w_ref[...])

    return pl.pallas_call(
        body, name="mlp_down", grid=(t // tm,),
        in_specs=[_row_spec(tm, D_MODEL), _row_spec(tm, D_FF), _const_spec((D_FF, D_MODEL))],
        out_specs=_row_spec(tm, D_MODEL), out_shape=jax.ShapeDtypeStruct((t, D_MODEL), f32),
        compiler_params=_params("arbitrary"),
    )(h1, hs, w_down)


TAIL_LOSS, TAIL_FINAL_G, TAIL_PLE_G, TAIL_GATE_B, TAIL_GATE_NORM_G = 0, 1, 2, 3, 4


def _tail(h2, p, target, g3, w_gate, b_gate, w_ple, ple_g, fin_g):
    t = h2.shape[0]
    tm = min(TOKEN_TILE, t)

    def body(h_ref, p_ref, t_ref, g3_ref, wg_ref, bg_ref, wp_ref, pg_ref, fg_ref,
             dh_ref, dhb_ref, dwg_ref, dwp_ref, acc_ref):
        i = pl.program_id(0)
        _acc_init(i, dwg_ref, dwp_ref, acc_ref)
        h2v = h_ref[...]
        tgt = t_ref[...]
        u3, vjp_u3 = jax.vjp(_rms, h2v, g3_ref[...])
        u3b = u3.astype(bf16)
        pb = p_ref[...].astype(bf16)
        gate_pre = _nn(u3b, wg_ref[...]) + bg_ref[...]
        emb_pre = _nn(pb, wp_ref[...])

        def tail_fn(hv, gp, ep, pg, fg):
            h3 = hv + jax.nn.sigmoid(gp) * _rms(ep, pg)
            err = _rms(h3, fg) - tgt
            return 0.5 * jnp.mean(err * err, axis=-1, keepdims=True)

        loss_tok, vjp_tail = jax.vjp(tail_fn, h2v, gate_pre, emb_pre, pg_ref[...], fg_ref[...])
        dh_a, dgp, dep, dpg, dfg = vjp_tail(jnp.ones_like(loss_tok))
        dgpb = dgp.astype(bf16)
        dh_b, dg3 = vjp_u3(_nt(dgpb, wg_ref[...]))
        dh = dh_a + dh_b
        dh_ref[...] = dh
        dhb_ref[...] = dh.astype(bf16)
        dwg_ref[...] += _tn(u3b, dgpb)
        dwp_ref[...] += _tn(pb, dep.astype(bf16))
        acc_ref[TAIL_LOSS:TAIL_LOSS + 1, :] += jnp.broadcast_to(jnp.sum(loss_tok, axis=0, keepdims=True), (1, 1024))
        acc_ref[TAIL_FINAL_G:TAIL_FINAL_G + 1, :] += dfg
        acc_ref[TAIL_PLE_G:TAIL_PLE_G + 1, :] += dpg
        acc_ref[TAIL_GATE_B:TAIL_GATE_B + 1, :] += jnp.sum(dgp, axis=0, keepdims=True)
        acc_ref[TAIL_GATE_NORM_G:TAIL_GATE_NORM_G + 1, :] += dg3

    vec = _const_spec((1, 1024))
    row = _row_spec(tm, 1024)
    return pl.pallas_call(
        body, name="tail", grid=(t // tm,),
        in_specs=[row, _row_spec(tm, PLE_DIM), row, vec, _const_spec((1024, 1024)), vec, _const_spec((PLE_DIM, 1024)), vec, vec],
        out_specs=[row, row, _const_spec((1024, 1024)), _const_spec((PLE_DIM, 1024)), _const_spec((8, 1024))],
        out_shape=(jax.ShapeDtypeStruct((t, 1024), f32), jax.ShapeDtypeStruct((t, 1024), bf16),
                   jax.ShapeDtypeStruct((1024, 1024), f32), jax.ShapeDtypeStruct((PLE_DIM, 1024), f32),
                   jax.ShapeDtypeStruct((8, 1024), f32)),
        compiler_params=_params("arbitrary"),
    )(h2, p, target, g3, w_gate, b_gate, w_ple, ple_g, fin_g)


def _mlp_bwd(dh2, dh2b, pre, h1, g2, w_down, w_up, token=None):
    t = dh2.shape[0]
    tm = min(TOKEN_TILE, t)
    blk = D_FF // N_DEV

    def body(dh_ref, dhb_ref, pre_ref, h1_ref, g_ref, wd_hbm, wu_hbm, dpre_ref, dh1_ref, dh1b_ref, acc_ref,
             wd_ref, wu_ref):
        i = pl.program_id(0)
        _acc_init(i, acc_ref)

        @pl.when(i == 0)
        def _():
            pltpu.sync_copy(wd_hbm, wd_ref)
            pltpu.sync_copy(wu_hbm, wu_ref)

        dhb = dhb_ref[...]
        du2 = jnp.zeros((tm, D_MODEL), f32)
        for k in range(D_FF // blk):
            dhs = _nt(dhb, wd_ref[k * blk:(k + 1) * blk, :])
            dpre = (dhs * (2.0 * jnp.maximum(pre_ref[:, k * blk:(k + 1) * blk].astype(f32), 0.0))).astype(bf16)
            dpre_ref[:, k * blk:(k + 1) * blk] = dpre
            du2 = du2 + _nt(dpre, wu_ref[k])
        _, vjp_u2 = jax.vjp(_rms, h1_ref[...], g_ref[...])
        d, dg = vjp_u2(du2)
        dh1 = dh_ref[...] + d
        dh1_ref[...] = dh1
        dh1b_ref[...] = dh1.astype(bf16)
        acc_ref[0:1, :] += dg

    row = _row_spec(tm, 1024)
    body, tok, tok_spec = _after_token(body, token)
    return pl.pallas_call(
        body, name="mlp_bwd", grid=(t // tm,),
        in_specs=tok_spec + [row, row, _row_spec(tm, D_FF), row, _const_spec((1, 1024)), HBM_SPEC, HBM_SPEC],
        out_specs=[_row_spec(tm, D_FF), row, row, _const_spec((8, 1024))],
        out_shape=(jax.ShapeDtypeStruct((t, D_FF), bf16), jax.ShapeDtypeStruct((t, 1024), f32),
                   jax.ShapeDtypeStruct((t, 1024), bf16), jax.ShapeDtypeStruct((8, 1024), f32)),
        scratch_shapes=[pltpu.VMEM((D_FF, D_MODEL), bf16), pltpu.VMEM((N_DEV, D_MODEL, D_FF // N_DEV), bf16)],
        compiler_params=_params("arbitrary"),
    )(*tok, dh2, dh2b, pre, h1, g2, w_down, w_up)


OPB_SSD_G, OPB_LN_G, OPB_LN_B, OPB_CONV_B = 0, 1, 2, 3


def _out_proj_bwd(dh1b, y, z, v2, g_ssd, ln_g, ln_b, w_out, token=None):
    t = y.shape[0]
    tm = min(TOKEN_TILE, t)

    def body(dh_ref, y_ref, z_ref, v_ref, gs_ref, lg_ref, lb_ref, w_ref, dy_ref, dz_ref, dv_ref, acc_ref):
        i = pl.program_id(0)
        _acc_init(i, acc_ref)
        dhb = dh_ref[...]
        dys = _nt(dhb, w_ref[0:1024, :])
        dyc = _nt(dhb, w_ref[1024:2048, :])
        _, vjp_g = jax.vjp(_gated_norm, y_ref[...], z_ref[...], gs_ref[...])
        dy, dz, dgs = vjp_g(dys)
        _, vjp_l = jax.vjp(_ln_silu, v_ref[...], lg_ref[...], lb_ref[...])
        dv, dlg, dlb = vjp_l(dyc)
        dy_ref[...] = dy
        dz_ref[...] = dz.astype(bf16)
        dv_ref[...] = dv
        acc_ref[OPB_SSD_G:OPB_SSD_G + 1, :] += dgs
        acc_ref[OPB_LN_G:OPB_LN_G + 1, :] += dlg
        acc_ref[OPB_LN_B:OPB_LN_B + 1, :] += dlb
        acc_ref[OPB_CONV_B:OPB_CONV_B + 1, :] += jnp.sum(dv, axis=0, keepdims=True)

    vec = _const_spec((1, 1024))
    row = _row_spec(tm, 1024)
    body, tok, tok_spec = _after_token(body, token)
    return pl.pallas_call(
        body, name="out_proj_bwd", grid=(t // tm,),
        in_specs=tok_spec + [row, row, row, row, vec, vec, vec, _const_spec((2048, 1024))],
        out_specs=[row, row, row, _const_spec((8, 1024))],
        out_shape=(jax.ShapeDtypeStruct((t, 1024), f32), jax.ShapeDtypeStruct((t, 1024), bf16),
                   jax.ShapeDtypeStruct((t, 1024), f32), jax.ShapeDtypeStruct((8, 1024), f32)),
        compiler_params=_params("arbitrary"),
    )(*tok, dh1b, y, z, v2, g_ssd, ln_g, ln_b, w_out)


def _conf_conv_bwd(dv2, cv, cg, w):
    t = cv.shape[0]
    tm = min(TOKEN_TILE, t)
    ext_rows = tm + CONV_HALO
    offsets = [CONF_KERNEL - 1 - k for k in range(CONF_KERNEL)]

    def body(dv_ref, dvn_ref, cv_ref, cg_ref, w_ref, dcv_ref, dcg_ref, dw_ref, glu_ref, dext_ref, sg_ref, shifted_ref):
        i = pl.program_id(0)
        _acc_init(i, dw_ref)
        sg = jax.nn.sigmoid(cg_ref[...])
        sg_ref[...] = sg
        glu_ref[...] = cv_ref[...] * sg
        dext_ref[0:tm, :] = dv_ref[...]
        dext_ref[tm:, :] = jnp.where(i == pl.num_programs(0) - 1, 0.0, dvn_ref[...])
        _shifted_copies(dext_ref, shifted_ref, ext_rows - SUBLANES)

        for lb in range(1024 // LANE_TILE):
            lanes = slice(lb * LANE_TILE, (lb + 1) * LANE_TILE)

            def rows_w(rb, accs, lanes=lanes):
                r0 = pl.multiple_of(rb * CONV_BLOCK, CONV_BLOCK)
                accs = list(accs)
                for j in range(CONV_BLOCK // SUBLANES):
                    x = glu_ref[pl.ds(r0 + SUBLANES * j, SUBLANES), lanes]
                    for k, off in enumerate(offsets):
                        ref, base = _tap_source(dext_ref, shifted_ref, off)
                        accs[k] = accs[k] + x * ref[pl.ds(r0 + base + SUBLANES * j, SUBLANES), lanes]
                return tuple(accs)

            zero = jnp.zeros((SUBLANES, LANE_TILE), f32)
            accs = lax.fori_loop(0, tm // CONV_BLOCK, rows_w, (zero,) * CONF_KERNEL)
            for k in range(CONF_KERNEL):
                dw_ref[k:k + 1, lanes] += jnp.sum(accs[k], axis=0, keepdims=True)

        for lb in range(1024 // LANE_TILE):
            lanes = slice(lb * LANE_TILE, (lb + 1) * LANE_TILE)
            taps = [jnp.broadcast_to(w_ref[k:k + 1, lanes], (SUBLANES, LANE_TILE)) for k in range(CONF_KERNEL)]

            def rows_x(rb, carry, lanes=lanes, taps=taps):
                r0 = pl.multiple_of(rb * CONV_BLOCK, CONV_BLOCK)
                zero = jnp.zeros((SUBLANES, LANE_TILE), f32)
                dglu = jnp.concatenate(_conv_rows(dext_ref, shifted_ref, taps, offsets, lanes, r0, zero), axis=0)
                sg = sg_ref[pl.ds(r0, CONV_BLOCK), lanes]
                cvv = cv_ref[pl.ds(r0, CONV_BLOCK), lanes]
                dcv_ref[pl.ds(r0, CONV_BLOCK), lanes] = (dglu * sg).astype(bf16)
                dcg_ref[pl.ds(r0, CONV_BLOCK), lanes] = (dglu * cvv * sg * (1.0 - sg)).astype(bf16)
                return carry

            lax.fori_loop(0, tm // CONV_BLOCK, rows_x, 0)

    row = _row_spec(tm, 1024)
    return pl.pallas_call(
        body, name="conf_conv_bwd", grid=(t // tm,),
        in_specs=[row, _next_halo_spec(tm, CONV_HALO, 1024, t), row, row, _const_spec((32, 1024))],
        out_specs=[row, row, _const_spec((32, 1024))],
        out_shape=(jax.ShapeDtypeStruct((t, 1024), bf16), jax.ShapeDtypeStruct((t, 1024), bf16),
                   jax.ShapeDtypeStruct((32, 1024), f32)),
        scratch_shapes=[pltpu.VMEM((tm, 1024), f32), pltpu.VMEM((ext_rows, 1024), f32), pltpu.VMEM((tm, 1024), f32),
                        pltpu.VMEM((SUBLANES, ext_rows, 1024), f32)],
        compiler_params=_params("arbitrary"),
    )(dv2, dv2, cv, cg, w)


def _ssd_scan_bwd(xbc_act, dt, s_prev, dy, a_log, d_skip):
    t = xbc_act.shape[0]
    nc = t // CHUNK

    def body(xbc_ref, dt_ref, sp_ref, dy_ref, al_ref, dk_ref, dxbc_ref, ddt_ref, dal_ref, ddk_ref, ds_ref):
        i = pl.program_id(0)
        _acc_init(i, ds_ref, dal_ref, ddk_ref)
        consts = _ssd_consts()
        _, vjp_c = jax.vjp(
            functools.partial(_ssd_chunk, consts=consts),
            xbc_ref[:, 0:1024], xbc_ref[:, 1024:1280], xbc_ref[:, 1280:1536], dt_ref[...], sp_ref[...],
            al_ref[...], dk_ref[...])
        dxs, dbm, dcm, ddt, dsp, dal, ddk = vjp_c((dy_ref[...], ds_ref[...]))
        dxbc_ref[:, 0:1024] = dxs
        dxbc_ref[:, 1024:1280] = dbm
        dxbc_ref[:, 1280:1536] = dcm
        ddt_ref[...] = ddt
        ds_ref[...] = dsp
        dal_ref[...] += dal
        ddk_ref[...] += ddk

    rev = lambda i: (nc - 1 - i, 0)
    return pl.pallas_call(
        body, name="ssd_scan_bwd", grid=(nc,),
        in_specs=[pl.BlockSpec((CHUNK, XBC_WIDTH), rev), pl.BlockSpec((CHUNK, SSD_HEADS), rev),
                  pl.BlockSpec((None, SSD_STATE, D_MODEL), lambda i: (nc - 1 - i, 0, 0)),
                  pl.BlockSpec((CHUNK, D_MODEL), rev), _const_spec((1, SSD_HEADS)), _const_spec((1, SSD_HEADS))],
        out_specs=[pl.BlockSpec((CHUNK, XBC_WIDTH), rev), pl.BlockSpec((CHUNK, SSD_HEADS), rev),
                   _const_spec((1, SSD_HEADS)), _const_spec((1, SSD_HEADS))],
        out_shape=(jax.ShapeDtypeStruct((t, XBC_WIDTH), f32), jax.ShapeDtypeStruct((t, SSD_HEADS), f32),
                   jax.ShapeDtypeStruct((1, SSD_HEADS), f32), jax.ShapeDtypeStruct((1, SSD_HEADS), f32)),
        scratch_shapes=[pltpu.VMEM((SSD_STATE, D_MODEL), f32)],
        compiler_params=_params("arbitrary"),
    )(xbc_act, dt, s_prev, dy, a_log, d_skip)


def _ssd_pre_bwd(xbc_raw, dxbc_act, ddt, dt_raw, cw, cb, dt_bias):
    t = xbc_raw.shape[0]
    tm = min(TOKEN_TILE, t)

    def body(cur_ref, halo_ref, dact_ref, ddt_ref, dtr_ref, w_ref, b_ref, dtb_ref,
             dco_ref, ddtr_ref, dw_ref, db_ref, ddtb_ref, ext_ref):
        i = pl.program_id(0)
        _acc_init(i, dw_ref, db_ref, ddtb_ref)
        ext_ref[0:8, :] = jnp.where(i == 0, 0.0, halo_ref[...])
        ext_ref[8:, :] = cur_ref[...]
        fold = lambda a: a[0:8] + a[8:ROW_BLOCK]
        for c in range(0, XBC_WIDTH, LANE_CHUNK):
            ls = slice(c, c + LANE_CHUNK)
            part_b = jnp.zeros((8, LANE_CHUNK), f32)
            part_w = [jnp.zeros((8, LANE_CHUNK), f32)] * SSD_CONV
            for r in range(0, tm, ROW_BLOCK):
                rs = slice(r, r + ROW_BLOCK)
                co = _conv4_block(ext_ref, w_ref, b_ref, rs, ls)
                sg = jax.nn.sigmoid(co)
                dco = dact_ref[rs, ls] * sg * (1.0 + co * (1.0 - sg))
                dco_ref[rs, ls] = dco
                part_b = part_b + fold(dco)
                part_w = [pw + fold(dco * ext_ref[r + 8 - 3 + k:r + ROW_BLOCK + 8 - 3 + k, ls])
                          for k, pw in enumerate(part_w)]
            db_ref[:, ls] += jnp.sum(part_b, axis=0, keepdims=True)
            for k in range(SSD_CONV):
                dw_ref[k:k + 1, ls] += jnp.sum(part_w[k], axis=0, keepdims=True)
        ddtr = ddt_ref[...] * jax.nn.sigmoid(dtr_ref[...] + dtb_ref[...])
        ddtb_ref[...] += jnp.sum(ddtr, axis=0, keepdims=True)
        ddtr_ref[...] = ddtr.astype(bf16)

    return pl.pallas_call(
        body, name="ssd_pre_bwd", grid=(t // tm,),
        in_specs=[_row_spec(tm, XBC_WIDTH), _prev_halo_spec(tm, 8, XBC_WIDTH), _row_spec(tm, XBC_WIDTH),
                  _row_spec(tm, SSD_HEADS), _row_spec(tm, SSD_HEADS), _const_spec((8, XBC_WIDTH)),
                  _const_spec((1, XBC_WIDTH)), _const_spec((1, SSD_HEADS))],
        out_specs=[_row_spec(tm, XBC_WIDTH), _row_spec(tm, SSD_HEADS), _const_spec((8, XBC_WIDTH)),
                   _const_spec((1, XBC_WIDTH)), _const_spec((1, SSD_HEADS))],
        out_shape=(jax.ShapeDtypeStruct((t, XBC_WIDTH), f32), jax.ShapeDtypeStruct((t, SSD_HEADS), bf16),
                   jax.ShapeDtypeStruct((8, XBC_WIDTH), f32), jax.ShapeDtypeStruct((1, XBC_WIDTH), f32),
                   jax.ShapeDtypeStruct((1, SSD_HEADS), f32)),
        scratch_shapes=[pltpu.VMEM((tm + 8, XBC_WIDTH), f32)],
        compiler_params=_params("arbitrary"),
    )(xbc_raw, xbc_raw, dxbc_act, ddt, dt_raw, cw, cb, dt_bias)


def _conv4_bwd_data(dco, cw):
    t = dco.shape[0]
    tm = min(TOKEN_TILE, t)

    def body(cur_ref, nxt_ref, w_ref, o_ref, ext_ref):
        i = pl.program_id(0)
        ext_ref[0:tm, :] = cur_ref[...]
        ext_ref[tm:, :] = jnp.where(i == pl.num_programs(0) - 1, 0.0, nxt_ref[...])
        for rs, ls in _blocks(tm, XBC_WIDTH):
            acc = w_ref[0:1, ls] * ext_ref[rs.start + SSD_CONV - 1:rs.stop + SSD_CONV - 1, ls]
            for k in range(1, SSD_CONV):
                acc = acc + w_ref[k:k + 1, ls] * ext_ref[rs.start + SSD_CONV - 1 - k:rs.stop + SSD_CONV - 1 - k, ls]
            o_ref[rs, ls] = acc.astype(bf16)

    return pl.pallas_call(
        body, name="conv4_bwd_data", grid=(t // tm,),
        in_specs=[_row_spec(tm, XBC_WIDTH), _next_halo_spec(tm, 8, XBC_WIDTH, t), _const_spec((8, XBC_WIDTH))],
        out_specs=_row_spec(tm, XBC_WIDTH), out_shape=jax.ShapeDtypeStruct((t, XBC_WIDTH), bf16),
        scratch_shapes=[pltpu.VMEM((tm + 8, XBC_WIDTH), f32)],
        compiler_params=_params("arbitrary"),
    )(dco, dco, cw)


def _in_proj_bwd(dproj, x, dh1, g1, w_in_p, token=None):
    t = x.shape[0]
    tm = min(TOKEN_TILE, t)

    def body(dz_ref, dxbc_ref, ddt_ref, dcv_ref, dcg_ref, x_ref, dh_ref, g_ref, w_ref, dx_ref, acc_ref):
        i = pl.program_id(0)
        _acc_init(i, acc_ref)
        du = jnp.zeros((tm, D_MODEL), f32)
        for (lo, hi), r in zip(IN_SEGS, (dz_ref, dxbc_ref, ddt_ref, dcv_ref, dcg_ref)):
            du = du + _nn(r[...], w_ref[lo:hi, :])
        _, vjp_u = jax.vjp(_rms, x_ref[...], g_ref[...])
        d, dg = vjp_u(du)
        dx_ref[...] = dh_ref[...] + d
        acc_ref[0:1, :] += dg

    row = _row_spec(tm, 1024)
    body, tok, tok_spec = _after_token(body, token)
    return pl.pallas_call(
        body, name="in_proj_bwd", grid=(t // tm,),
        in_specs=tok_spec + [_row_spec(tm, hi - lo) for lo, hi in IN_SEGS] + [row, row, _const_spec((1, 1024)),
                                                                               _const_spec((IN_WIDTH, D_MODEL))],
        out_specs=[row, _const_spec((8, 1024))],
        out_shape=(jax.ShapeDtypeStruct((t, 1024), f32), jax.ShapeDtypeStruct((8, 1024), f32)),
        compiler_params=_params("arbitrary"),
    )(*tok, *dproj, x, dh1, g1, w_in_p)


def _mm_tn(a, b, tk, tn, name, column_blocks=False):
    t, kk = a.shape
    n = b.shape[1]
    tk, tn = min(tk, kk), min(tn, n)

    def body(a_ref, b_ref, o_ref):
        o_ref[...] = _tn(a_ref[...], b_ref[...]).astype(bf16)

    if column_blocks:
        assert tk == kk
        out_spec = pl.BlockSpec((None, tk, tn), lambda i, j: (j, 0, 0))
        out_shape = jax.ShapeDtypeStruct((n // tn, kk, tn), bf16)
    else:
        out_spec = pl.BlockSpec((tk, tn), lambda i, j: (i, j))
        out_shape = jax.ShapeDtypeStruct((kk, n), bf16)
    return pl.pallas_call(
        body, name=name, grid=(kk // tk, n // tn),
        in_specs=[pl.BlockSpec((t, tk), lambda i, j: (0, i)), pl.BlockSpec((t, tn), lambda i, j: (0, j))],
        out_specs=out_spec, out_shape=out_shape,
        compiler_params=_params("arbitrary", "arbitrary"),
    )(a, b)


class _LocalWeights:
    def __init__(self, w):
        self.w = w
        self.sent = {}

    def start_token(self):
        return None

    def weight(self, name, after=()):
        del after
        return self.w[name]

    def send_grads(self, grads):
        self.sent.update(grads)
        return None


def _local_step(x, p, target, s, comm):
    conv_w, conf_w = comm.weight("ssd_conv_w"), comm.weight("conf_dw_w")
    w_in = comm.weight("w_in")
    u, z, xbc_raw, dt_raw, cv, cg = _in_proj(x, s["mix_norm_g"], w_in, token=comm.start_token())
    xbc_act, dt = _ssd_pre(xbc_raw, dt_raw, conv_w, s["ssd_conv_b"], s["dt_bias"])
    y, s_prev = _ssd_scan(xbc_act, dt, s["A_log"], s["D_skip"])
    v2 = _conf_conv(cv, cg, conf_w, s["conf_dw_b"])
    w_out = comm.weight("w_out", after=(y, v2))
    ys, yc, h1, u2 = _out_proj(y, z, v2, x, s["ssd_norm_g"], s["conf_ln_g"], s["conf_ln_b"], w_out, s["mlp_norm_g"])
    w_up = comm.weight("w_up", after=(u2,))
    pre, hs = _mlp_up(u2, w_up)
    w_down = comm.weight("w_down", after=(hs,))
    h2 = _mlp_down(h1, hs, w_down)
    w_gate, w_ple = comm.weight("w_ple_gate", after=(h2,)), comm.weight("w_ple", after=(h2,))
    dh2, dh2b, dwg, dwp, tail_acc = _tail(h2, p, target, s["ple_gate_norm_g"], w_gate, s["b_ple_gate"], w_ple,
                                          s["ple_norm_g"], s["final_norm_g"])
    token = comm.send_grads({"w_ple_gate": dwg, "w_ple": dwp})
    dpre, dh1, dh1b, mlp_acc = _mlp_bwd(dh2, dh2b, pre, h1, s["mlp_norm_g"], w_down, w_up, token=token)
    token = comm.send_grads({
        "w_down": _mm_tn(hs, dh2b, 512, 1024, "dw_down"),
        "w_up": _mm_tn(u2, dpre, 1024, D_FF // N_DEV, "dw_up", column_blocks=True),
        "w_out": [_mm_tn(ys, dh1b, 1024, 512, "dw_out_ssd"), _mm_tn(yc, dh1b, 1024, 512, "dw_out_conf")],
    })
    dy, dz, dv2, opb_acc = _out_proj_bwd(dh1b, y, z, v2, s["ssd_norm_g"], s["conf_ln_g"], s["conf_ln_b"], w_out,
                                         token=token)
    dcv, dcg, dconf_w = _conf_conv_bwd(dv2, cv, cg, conf_w)
    dxbc_act, ddt, d_alog, d_dskip = _ssd_scan_bwd(xbc_act, dt, s_prev, dy, s["A_log"], s["D_skip"])
    dco, ddt_raw, dconv_w, dconv_b, d_dtb = _ssd_pre_bwd(xbc_raw, dxbc_act, ddt, dt_raw, conv_w, s["ssd_conv_b"],
                                                        s["dt_bias"])
    dxbc_raw = _conv4_bwd_data(dco, conv_w)
    dproj = (dz, dxbc_raw, ddt_raw, dcv, dcg)
    token = comm.send_grads({"w_in": [_mm_tn(d, u, 512, 1024, "dw_in_" + n)
                                      for n, d in zip(("z", "xbc", "dt", "cv", "cg"), dproj)]})
    grad_x, inp_acc = _in_proj_bwd(dproj, x, dh1, s["mix_norm_g"], w_in, token=token)
    acc = {"in_proj": inp_acc, "out_proj": opb_acc, "mlp": mlp_acc, "tail": tail_acc, "ssd_conv_b": dconv_b,
           "dt_bias": d_dtb, "A_log": d_alog, "D_skip": d_dskip, "ssd_conv_w": dconv_w, "conf_dw_w": dconf_w}
    return grad_x, acc


def _mesh_pos():
    return lax.axis_index("x"), lax.axis_index("y"), lax.axis_index("c")


def _other_chips(x, y):
    return [(1 - x, y), (x, 1 - y), (1 - x, 1 - y)]


def _all_gather(arrs, name):
    n = len(arrs)

    def body(*refs):
        ins, outs = refs[:n], refs[n:2 * n]
        send_sems, recv_sems, local_sems = refs[2 * n:]
        x, y, c = _mesh_pos()
        me = 4 * x + 2 * y + c
        sibling = (x, y, 1 - c)
        chips = _other_chips(x, y)

        def copy(a, k, block, to, src=None):
            dst = outs[a].at[block]
            return pltpu.make_async_remote_copy(
                src_ref=dst if src is None else src, dst_ref=dst, send_sem=send_sems.at[a, k],
                recv_sem=recv_sems.at[a, k], device_id=to, device_id_type=MESH_ID)

        mine = [pltpu.make_async_copy(ins[a], outs[a].at[me], local_sems.at[a]) for a in range(n)]
        for cp in mine:
            cp.start()
        first = []
        for a in range(n):
            first.append(copy(a, 0, me, sibling, src=ins[a]))
            first += [copy(a, 1 + j, me, (px, py, c), src=ins[a]) for j, (px, py) in enumerate(chips)]
        for cp in first:
            cp.start()
        passed = []
        for j, (px, py) in enumerate(chips):
            for a in range(n):
                blk = 4 * px + 2 * py + c
                copy(a, 1 + j, blk, (x, y, c)).wait_recv()
                cp = copy(a, 4 + j, blk, sibling)
                cp.start()
                passed.append(cp)
        for a in range(n):
            copy(a, 0, 4 * x + 2 * y + (1 - c), (x, y, c)).wait_recv()
        for j, (px, py) in enumerate(chips):
            for a in range(n):
                copy(a, 4 + j, 4 * px + 2 * py + (1 - c), (x, y, c)).wait_recv()
        for cp in first + passed:
            cp.wait_send()
        for cp in mine:
            cp.wait()

    return pl.pallas_call(
        body, name=name,
        in_specs=[HBM_SPEC] * n, out_specs=[HBM_SPEC] * n,
        out_shape=[jax.ShapeDtypeStruct((N_DEV,) + a.shape, a.dtype) for a in arrs],
        scratch_shapes=[pltpu.SemaphoreType.DMA((n, 7)), pltpu.SemaphoreType.DMA((n, 7)), pltpu.SemaphoreType.DMA((n,))],
    )(*arrs)


_PEER_FLIPS = ((0, 0, 1), (1, 0, 0), (0, 1, 0), (1, 1, 0), (1, 0, 1), (0, 1, 1), (1, 1, 1))
SEM_SPEC = pl.BlockSpec(memory_space=pltpu.SEMAPHORE)
ANY_SPEC = pl.BlockSpec(memory_space=pl.ANY)


def _flip(v, d):
    return 1 - v if d else v


def _peers(x, y, c):
    out = []
    for dx, dy, dc in _PEER_FLIPS:
        px, py, pc = _flip(x, dx), _flip(y, dy), _flip(c, dc)
        out.append(((px, py, pc), 4 * px + 2 * py + pc))
    return out


def _exchange_copy(src_ref, land_ref, send_sems, recv_sems, k, peer, peer_block, my_block, by_block, outgoing):
    src = src_ref.at[peer_block] if by_block else src_ref
    dst = land_ref.at[my_block if outgoing else peer_block]
    return pltpu.make_async_remote_copy(src_ref=src, dst_ref=dst, send_sem=send_sems.at[k], recv_sem=recv_sems.at[k],
                                        device_id=peer, device_id_type=MESH_ID)


def _exchange_start(srcs, by_block, name, after):
    n = len(srcs)
    x_pos, y_pos, c_pos = _mesh_pos()
    me_at = 4 * x_pos + 2 * y_pos + c_pos
    lands = []
    for a in srcs:
        own = lax.dynamic_index_in_dim(a, me_at, 0, keepdims=False) if by_block else a
        lands.append(lax.dynamic_update_slice(lax.empty((N_DEV,) + own.shape, own.dtype), own[None],
                                              (me_at,) + (0,) * own.ndim))

    def body(*refs):
        src_refs, land_refs = refs[1:1 + n], refs[1 + n:1 + 2 * n]
        outs = refs[1 + 2 * n:]
        send, recv, token = outs[:n], outs[n:2 * n], outs[4 * n]
        x, y, c = _mesh_pos()
        me = 4 * x + 2 * y + c
        for a in range(n):
            for k, (peer, blk) in enumerate(_peers(x, y, c)):
                _exchange_copy(src_refs[a], land_refs[a], send[a], recv[a], k, peer, blk, me, by_block, True).start()
        token[...] = jnp.zeros_like(token)

    sems = [pltpu.SemaphoreType.DMA((N_DEV - 1,))] * (2 * n)
    thru = [pltpu.HBM(a.shape, a.dtype) for a in list(srcs) + list(lands)]
    res = pl.pallas_call(
        body, name=name,
        in_specs=[ANY_SPEC] + [HBM_SPEC] * (2 * n),
        out_specs=[SEM_SPEC] * (2 * n) + [HBM_SPEC] * (2 * n) + [pl.BlockSpec(memory_space=pltpu.VMEM)],
        out_shape=sems + thru + [jax.ShapeDtypeStruct((8, 128), f32)],
        input_output_aliases={1 + i: 2 * n + i for i in range(2 * n)},
        compiler_params=pltpu.CompilerParams(has_side_effects=pltpu.SideEffectType.DATAFLOW_SIDE_EFFECTING),
    )(after, *[pltpu.with_memory_space_constraint(a, pltpu.HBM) for a in list(srcs) + list(lands)])
    states = [(res[2 * n + a], res[3 * n + a], res[a], res[n + a]) for a in range(n)]
    return states, res[4 * n]


def _exchange_wait(states, by_block, name, after):
    n, na = len(states), len(after)

    def body(*refs):
        src_refs, land_refs = refs[:n], refs[n:2 * n]
        send, recv = refs[2 * n:3 * n], refs[3 * n:4 * n]
        x, y, c = _mesh_pos()
        me = 4 * x + 2 * y + c
        for a in range(n):
            for k, (peer, blk) in enumerate(_peers(x, y, c)):
                _exchange_copy(src_refs[a], land_refs[a], send[a], recv[a], k, peer, blk, me, by_block, True).wait_send()
                _exchange_copy(src_refs[a], land_refs[a], send[a], recv[a], k, peer, blk, me, by_block, False).wait_recv()

    srcs, lands = [s[0] for s in states], [s[1] for s in states]
    res = pl.pallas_call(
        body, name=name,
        in_specs=[HBM_SPEC] * (2 * n) + [SEM_SPEC] * (2 * n) + [ANY_SPEC] * na,
        out_specs=[HBM_SPEC] * (2 * n),
        out_shape=[pltpu.HBM(a.shape, a.dtype) for a in srcs + lands],
        input_output_aliases={i: i for i in range(2 * n)},
        compiler_params=pltpu.CompilerParams(has_side_effects=pltpu.SideEffectType.DATAFLOW_SIDE_EFFECTING),
    )(*srcs, *lands, *[s[2] for s in states], *[s[3] for s in states], *after)
    return list(res[n:2 * n])


def _adamw_math(w, g, m, v):
    m = ADAM_B1 * m + (1.0 - ADAM_B1) * g
    v = ADAM_B2 * v + (1.0 - ADAM_B2) * (g * g)
    m_hat = m / (1.0 - ADAM_B1 ** ADAM_STEP)
    v_hat = v / (1.0 - ADAM_B2 ** ADAM_STEP)
    delta = -ADAM_LR * (m_hat / (jnp.sqrt(v_hat) + ADAM_EPS) + ADAM_WD * w)
    return delta, m, v


def _adamw_big(parts, w, m, v, name):
    rows, cols = w.shape
    tr, tc = (256, cols) if rows % 256 == 0 else (rows, 256)
    nparts = parts.shape[0]

    def body(p_ref, w_ref, m_ref, v_ref, g_ref, d_ref, mo_ref, vo_ref):
        g = p_ref[0].astype(f32)
        for j in range(1, nparts):
            g = g + p_ref[j].astype(f32)
        d, mn, vn = _adamw_math(w_ref[...], g, m_ref[...], v_ref[...])
        g_ref[...] = g
        d_ref[...] = d
        mo_ref[...] = mn
        vo_ref[...] = vn

    tile = pl.BlockSpec((tr, tc), lambda i, j: (i, j))
    shp = jax.ShapeDtypeStruct((rows, cols), f32)
    return pl.pallas_call(
        body, name=name, grid=(rows // tr, cols // tc),
        in_specs=[pl.BlockSpec((nparts, tr, tc), lambda i, j: (0, i, j)), tile, tile, tile],
        out_specs=[tile, tile, tile, tile], out_shape=(shp, shp, shp, shp),
        compiler_params=_params("arbitrary", "arbitrary"),
    )(parts, w, m, v)


PACK_ROWS = 56
_PACK_AT = {
    "mix_norm_g": (0, 0, 1024), "ssd_norm_g": (1, 0, 1024), "conf_ln_g": (2, 0, 1024), "conf_ln_b": (3, 0, 1024),
    "conf_dw_b": (4, 0, 1024), "mlp_norm_g": (5, 0, 1024), "final_norm_g": (7, 0, 1024), "ple_norm_g": (8, 0, 1024),
    "b_ple_gate": (9, 0, 1024), "ple_gate_norm_g": (10, 0, 1024), "dt_bias": (13, 0, 16), "A_log": (13, 128, 16),
    "D_skip": (13, 256, 16),
}
PACK_LOSS_ROW = 6
PACK_CONV_B_ROW = 11
PACK_CONV_W_ROW = 14
PACK_CONF_W_ROW = 24


def _pack_small(acc):
    def body(inp, opb, mlp, tail, cb, dtb, alog, dskip, cw, fw, o_ref):
        o_ref[...] = jnp.zeros_like(o_ref)
        rows = {"mix_norm_g": inp[0:1, :], "mlp_norm_g": mlp[0:1, :],
                "ssd_norm_g": opb[OPB_SSD_G:OPB_SSD_G + 1, :], "conf_ln_g": opb[OPB_LN_G:OPB_LN_G + 1, :],
                "conf_ln_b": opb[OPB_LN_B:OPB_LN_B + 1, :], "conf_dw_b": opb[OPB_CONV_B:OPB_CONV_B + 1, :],
                "final_norm_g": tail[TAIL_FINAL_G:TAIL_FINAL_G + 1, :], "ple_norm_g": tail[TAIL_PLE_G:TAIL_PLE_G + 1, :],
                "b_ple_gate": tail[TAIL_GATE_B:TAIL_GATE_B + 1, :],
                "ple_gate_norm_g": tail[TAIL_GATE_NORM_G:TAIL_GATE_NORM_G + 1, :],
                "dt_bias": dtb[...], "A_log": alog[...], "D_skip": dskip[...]}
        for name, val in rows.items():
            r, lo, width = _PACK_AT[name]
            o_ref[r:r + 1, lo:lo + width] = val
        o_ref[PACK_LOSS_ROW:PACK_LOSS_ROW + 1, :] = tail[TAIL_LOSS:TAIL_LOSS + 1, :]
        o_ref[PACK_CONV_B_ROW:PACK_CONV_B_ROW + 1, :] = cb[:, 0:1024]
        o_ref[PACK_CONV_B_ROW + 1:PACK_CONV_B_ROW + 2, 0:512] = cb[:, 1024:XBC_WIDTH]
        for k in range(SSD_CONV):
            o_ref[PACK_CONV_W_ROW + k:PACK_CONV_W_ROW + k + 1, :] = cw[k:k + 1, 0:1024]
            o_ref[PACK_CONV_W_ROW + SSD_CONV + k:PACK_CONV_W_ROW + SSD_CONV + k + 1, 0:512] = cw[k:k + 1, 1024:XBC_WIDTH]
        o_ref[PACK_CONF_W_ROW:PACK_CONF_W_ROW + 32, :] = fw[...]

    return pl.pallas_call(body, name="pack_small", out_shape=jax.ShapeDtypeStruct((PACK_ROWS, 1024), f32))(
        acc["in_proj"], acc["out_proj"], acc["mlp"], acc["tail"], acc["ssd_conv_b"], acc["dt_bias"], acc["A_log"],
        acc["D_skip"], acc["ssd_conv_w"], acc["conf_dw_w"])


def _small_update(all_small, w, m, v):
    names = _REPLICATED

    def body(all_ref, *refs):
        ins, outs = refs[:3 * len(names)], refs[3 * len(names):]
        s = all_ref[0]
        for j in range(1, N_DEV):
            s = s + all_ref[j]
        outs[0][...] = s[PACK_LOSS_ROW:PACK_LOSS_ROW + 1, 0:1]
        outs[1][...] = jnp.concatenate([s[PACK_CONV_W_ROW:PACK_CONV_W_ROW + SSD_CONV, :],
                                        s[PACK_CONV_W_ROW + SSD_CONV:PACK_CONV_W_ROW + 2 * SSD_CONV, 0:512]], axis=1)
        outs[2][...] = s[PACK_CONF_W_ROW:PACK_CONF_W_ROW + CONF_KERNEL, :]
        for i, name in enumerate(names):
            if name == "ssd_conv_b":
                g = jnp.concatenate([s[PACK_CONV_B_ROW:PACK_CONV_B_ROW + 1, :],
                                     s[PACK_CONV_B_ROW + 1:PACK_CONV_B_ROW + 2, 0:512]], axis=1)
            else:
                r, lo, width = _PACK_AT[name]
                g = s[r:r + 1, lo:lo + width]
            d, mn, vn = _adamw_math(ins[3 * i][...], g, ins[3 * i + 1][...], ins[3 * i + 2][...])
            for o_ref, val in zip(outs[3 + 4 * i:7 + 4 * i], (g, d, mn, vn)):
                o_ref[...] = val

    shapes = [jax.ShapeDtypeStruct((1, 1), f32), jax.ShapeDtypeStruct((SSD_CONV, XBC_WIDTH), f32),
              jax.ShapeDtypeStruct((CONF_KERNEL, D_MODEL), f32)]
    operands = []
    for name in names:
        operands += [w[name], m[name], v[name]]
        shapes += [jax.ShapeDtypeStruct(w[name].shape, f32)] * 4
    res = pl.pallas_call(body, name="small_update", out_shape=shapes)(all_small, *operands)
    per_name = {name: tuple(res[3 + 4 * i:7 + 4 * i]) for i, name in enumerate(names)}
    return res[0], per_name, res[1], res[2]


def _adamw_filters(g, w, m, v):
    n = len(g)

    def body(*refs):
        ins, outs = refs[:4 * n], refs[4 * n:]
        for i in range(n):
            d, mn, vn = _adamw_math(ins[4 * i + 1][...], ins[4 * i][...], ins[4 * i + 2][...], ins[4 * i + 3][...])
            for o_ref, val in zip(outs[3 * i:3 * i + 3], (d, mn, vn)):
                o_ref[...] = val

    operands, shapes = [], []
    for i in range(n):
        operands += [g[i], w[i], m[i], v[i]]
        shapes += [jax.ShapeDtypeStruct(w[i].shape, f32)] * 3
    res = pl.pallas_call(body, name="adamw_filters", out_shape=shapes)(*operands)
    return [tuple(res[3 * i:3 * i + 3]) for i in range(n)]


_REPLICATED = ("mix_norm_g", "ssd_conv_b", "dt_bias", "A_log", "D_skip", "ssd_norm_g", "conf_dw_b", "conf_ln_g",
               "conf_ln_b", "mlp_norm_g", "ple_gate_norm_g", "b_ple_gate", "ple_norm_g", "final_norm_g")
_CONV_WEIGHTS = ("ssd_conv_w", "conf_dw_w")
_BIG = ("w_in", "w_out", "w_up", "w_down", "w_ple_gate", "w_ple")
_WEIGHT_ORDER = ("mix_norm_g", "w_in", "ssd_conv_w", "ssd_conv_b", "dt_bias", "A_log", "D_skip", "ssd_norm_g", "conf_dw_w",
                 "conf_dw_b", "conf_ln_g", "conf_ln_b", "w_out", "mlp_norm_g", "w_up", "w_down", "ple_gate_norm_g",
                 "w_ple_gate", "b_ple_gate", "w_ple", "ple_norm_g", "final_norm_g")


def _blocks_of_columns(a):
    r, c8 = a.shape
    return jnp.transpose(a.reshape(r, N_DEV, c8 // N_DEV), (1, 0, 2))


def _columns_of_blocks(a):
    _, r, c = a.shape
    return jnp.transpose(a, (1, 0, 2)).reshape(r, N_DEV * c)


_LATER = ("w_out", "w_up", "w_down", "w_ple_gate", "w_ple")
_WHOLE = {
    "w_out": lambda a: a.reshape(2048, D_MODEL),
    "w_up": lambda a: a,
    "w_down": lambda a: a.reshape(D_FF, D_MODEL),
    "w_ple_gate": lambda a: a.reshape(D_MODEL, D_MODEL),
    "w_ple": _columns_of_blocks,
}
_BY_BLOCK = {
    "w_in": lambda g: jnp.concatenate(g, axis=0).reshape(N_DEV, IN_WIDTH // N_DEV, D_MODEL),
    "w_out": lambda g: jnp.concatenate(g, axis=0).reshape(N_DEV, 256, D_MODEL),
    "w_up": lambda g: g,
    "w_down": lambda g: g.reshape(N_DEV, 512, D_MODEL),
    "w_ple_gate": lambda g: g.astype(bf16).reshape(N_DEV, 128, D_MODEL),
    "w_ple": lambda g: _blocks_of_columns(g.astype(bf16)),
}


class _StepComm:
    def __init__(self, me, ready, gathers, token):
        self.me, self.ready, self.gathers, self.token = me, ready, gathers, token
        self.sent = []

    def start_token(self):
        return self.token

    def weight(self, name, after=()):
        if name not in self.ready:
            (land,) = _exchange_wait([self.gathers[name]], False, "gather_wait_" + name, list(after))
            self.ready[name] = _WHOLE[name](land)
        return self.ready[name]

    def send_grads(self, grads):
        names = list(grads)
        blocks = [_BY_BLOCK[n](grads[n]) for n in names]
        states, self.token = _exchange_start(blocks, True, "scatter_start_" + names[0], self.token)
        self.sent.append((names, states))
        return self.token


def kernel(x, p, mix_norm_g, w_in, ssd_conv_w, ssd_conv_b, dt_bias, A_log, D_skip, ssd_norm_g, conf_dw_w, conf_dw_b, conf_ln_g, conf_ln_b, w_out, mlp_norm_g, w_up, w_down, ple_gate_norm_g, w_ple_gate, b_ple_gate, w_ple, ple_norm_g, final_norm_g, loss_target, m_mix_norm_g, m_w_in, m_ssd_conv_w, m_ssd_conv_b, m_dt_bias, m_A_log, m_D_skip, m_ssd_norm_g, m_conf_dw_w, m_conf_dw_b, m_conf_ln_g, m_conf_ln_b, m_w_out, m_mlp_norm_g, m_w_up, m_w_down, m_ple_gate_norm_g, m_w_ple_gate, m_b_ple_gate, m_w_ple, m_ple_norm_g, m_final_norm_g, v_mix_norm_g, v_w_in, v_ssd_conv_w, v_ssd_conv_b, v_dt_bias, v_A_log, v_D_skip, v_ssd_norm_g, v_conf_dw_w, v_conf_dw_b, v_conf_ln_g, v_conf_ln_b, v_w_out, v_mlp_norm_g, v_w_up, v_w_down, v_ple_gate_norm_g, v_w_ple_gate, v_b_ple_gate, v_w_ple, v_ple_norm_g, v_final_norm_g):
    wts = dict(mix_norm_g=mix_norm_g, w_in=w_in, ssd_conv_w=ssd_conv_w, ssd_conv_b=ssd_conv_b, dt_bias=dt_bias, A_log=A_log,
               D_skip=D_skip, ssd_norm_g=ssd_norm_g, conf_dw_w=conf_dw_w, conf_dw_b=conf_dw_b, conf_ln_g=conf_ln_g,
               conf_ln_b=conf_ln_b, w_out=w_out, mlp_norm_g=mlp_norm_g, w_up=w_up, w_down=w_down,
               ple_gate_norm_g=ple_gate_norm_g, w_ple_gate=w_ple_gate, b_ple_gate=b_ple_gate, w_ple=w_ple,
               ple_norm_g=ple_norm_g, final_norm_g=final_norm_g)
    mom1 = dict(mix_norm_g=m_mix_norm_g, w_in=m_w_in, ssd_conv_w=m_ssd_conv_w, ssd_conv_b=m_ssd_conv_b, dt_bias=m_dt_bias,
                A_log=m_A_log, D_skip=m_D_skip, ssd_norm_g=m_ssd_norm_g, conf_dw_w=m_conf_dw_w, conf_dw_b=m_conf_dw_b,
                conf_ln_g=m_conf_ln_g, conf_ln_b=m_conf_ln_b, w_out=m_w_out, mlp_norm_g=m_mlp_norm_g, w_up=m_w_up,
                w_down=m_w_down, ple_gate_norm_g=m_ple_gate_norm_g, w_ple_gate=m_w_ple_gate, b_ple_gate=m_b_ple_gate,
                w_ple=m_w_ple, ple_norm_g=m_ple_norm_g, final_norm_g=m_final_norm_g)
    mom2 = dict(mix_norm_g=v_mix_norm_g, w_in=v_w_in, ssd_conv_w=v_ssd_conv_w, ssd_conv_b=v_ssd_conv_b, dt_bias=v_dt_bias,
                A_log=v_A_log, D_skip=v_D_skip, ssd_norm_g=v_ssd_norm_g, conf_dw_w=v_conf_dw_w, conf_dw_b=v_conf_dw_b,
                conf_ln_g=v_conf_ln_g, conf_ln_b=v_conf_ln_b, w_out=v_w_out, mlp_norm_g=v_mlp_norm_g, w_up=v_w_up,
                w_down=v_w_down, ple_gate_norm_g=v_ple_gate_norm_g, w_ple_gate=v_w_ple_gate, b_ple_gate=v_b_ple_gate,
                w_ple=v_w_ple, ple_norm_g=v_ple_norm_g, final_norm_g=v_final_norm_g)
    x_pos, y_pos, c_pos = _mesh_pos()
    me = 4 * x_pos + 2 * y_pos + c_pos

    first = _all_gather([wts["w_in"][0].T.astype(bf16), wts["ssd_conv_w"][0], wts["conf_dw_w"][0]], "gather_first")
    ready = {
        "w_in": first[0].reshape(IN_WIDTH, D_MODEL),
        "ssd_conv_w": jnp.pad(_columns_of_blocks(first[1]), ((0, 8 - SSD_CONV), (0, 0))),
        "conf_dw_w": jnp.pad(_columns_of_blocks(first[2]), ((0, 32 - CONF_KERNEL), (0, 0))),
    }
    shards = [wts[n][0].astype(bf16) for n in _LATER]
    states, token = _exchange_start(shards, False, "gather_start", first[1])
    comm = _StepComm(me, ready, dict(zip(_LATER, states)), token)
    small = {n: wts[n].reshape(1, -1) for n in _REPLICATED}

    grad_x, acc = _local_step(x[0], p[0, 0], loss_target[0], small, comm)

    packed = _pack_small(acc)
    (small_state,), _ = _exchange_start([packed], False, "small_start", comm.token)
    grads, delta, new_m, new_v = {}, {}, {}, {}

    def adamw_big(names, sent, after):
        lands = _exchange_wait(sent, True, "scatter_wait_" + names[0], after)
        for n, land in zip(names, lands):
            view = (lambda a: a[0].T) if n == "w_in" else (lambda a: a[0])
            back = (lambda a: a.T[None]) if n == "w_in" else (lambda a: a[None])
            out = _adamw_big(land, view(wts[n]), view(mom1[n]), view(mom2[n]), "adamw_" + n)
            grads[n], delta[n], new_m[n], new_v[n] = [back(a) for a in out]

    for names, sent in comm.sent[:-1]:
        adamw_big(names, sent, [grad_x])
    (all_small,) = _exchange_wait([small_state], False, "small_wait", [delta[n] for n in _LATER])
    as_row = lambda d: {n: d[n].reshape(1, -1) for n in _REPLICATED}
    loss, per_name, conv_w_sum, conf_w_sum = _small_update(all_small, as_row(wts), as_row(mom1), as_row(mom2))
    for n in _REPLICATED:
        grads[n], delta[n], new_m[n], new_v[n] = [a.reshape(wts[n].shape) for a in per_name[n]]
    filt_g = [lax.dynamic_slice_in_dim(conv_w_sum, me * 192, 192, axis=1),
              lax.dynamic_slice_in_dim(conf_w_sum, me * 128, 128, axis=1)]
    filt = _adamw_filters(filt_g, [wts[n][0] for n in _CONV_WEIGHTS], [mom1[n][0] for n in _CONV_WEIGHTS],
                          [mom2[n][0] for n in _CONV_WEIGHTS])
    for n, g, (d, mn, vn) in zip(_CONV_WEIGHTS, filt_g, filt):
        grads[n], delta[n], new_m[n], new_v[n] = g[None], d[None], mn[None], vn[None]
    adamw_big(*comm.sent[-1], [delta[n] for n in _LATER])

    return (loss.reshape(()), grad_x[None], *[grads[n] for n in _WEIGHT_ORDER], *[delta[n] for n in _WEIGHT_ORDER],
            *[new_m[n] for n in _WEIGHT_ORDER], *[new_v[n] for n in _WEIGHT_ORDER])
```

```python
import functools

import jax
import jax.numpy as jnp
from jax import lax
from jax.experimental import pallas as pl
from jax.experimental.pallas import tpu as pltpu

f32 = jnp.float32
bf16 = jnp.bfloat16

EPS = 1e-6
D_MODEL = 1024
SSD_HEADS = 16
SSD_HEAD_DIM = 64
SSD_GROUPS = 2
SSD_STATE = 128
SSD_CONV = 4
CHUNK = 128
XBC_WIDTH = 1536
CONF_KERNEL = 31
D_FF = 4096
PLE_DIM = 256
IN_WIDTH = 4624
N_DEV = 8
SEG_Z = (0, 1024)
SEG_XBC = (1024, 2560)
SEG_DT = (2560, 2576)
SEG_CV = (2576, 3600)
SEG_CG = (3600, 4624)
IN_SEGS = (SEG_Z, SEG_XBC, SEG_DT, SEG_CV, SEG_CG)

ADAM_LR = 0.001
ADAM_B1 = 0.9
ADAM_B2 = 0.999
ADAM_EPS = 1e-08
ADAM_WD = 0.01
ADAM_STEP = 10

VMEM_LIMIT_BYTES = 56 * 1024 * 1024
TOKEN_TILE = 512
SMALL_ROWS = 16

HBM_SPEC = pl.BlockSpec(memory_space=pltpu.HBM)
MESH_ID = pl.DeviceIdType.MESH


def _params(*semantics):
    return pltpu.CompilerParams(dimension_semantics=semantics, vmem_limit_bytes=VMEM_LIMIT_BYTES)


def _nn(a, b):
    return lax.dot_general(a, b, (((1,), (0,)), ((), ())), preferred_element_type=f32)


def _nt(a, b):
    return lax.dot_general(a, b, (((1,), (1,)), ((), ())), preferred_element_type=f32)


def _tn(a, b):
    return lax.dot_general(a, b, (((0,), (0,)), ((), ())), preferred_element_type=f32)


@jax.custom_vjp
def bnn(a, b):
    return _nn(a.astype(bf16), b.astype(bf16))


def _bnn_fwd(a, b):
    ab, bb = a.astype(bf16), b.astype(bf16)
    return _nn(ab, bb), (ab, bb)


def _bnn_bwd(res, g):
    ab, bb = res
    gb = g.astype(bf16)
    return _nt(gb, bb), _tn(ab, gb)


bnn.defvjp(_bnn_fwd, _bnn_bwd)


@jax.custom_vjp
def bnt(a, b):
    return _nt(a.astype(bf16), b.astype(bf16))


def _bnt_fwd(a, b):
    ab, bb = a.astype(bf16), b.astype(bf16)
    return _nt(ab, bb), (ab, bb)


def _bnt_bwd(res, g):
    ab, bb = res
    gb = g.astype(bf16)
    return _nn(gb, bb), _tn(gb, ab)


bnt.defvjp(_bnt_fwd, _bnt_bwd)


@jax.custom_vjp
def btn(a, b):
    return _tn(a.astype(bf16), b.astype(bf16))


def _btn_fwd(a, b):
    ab, bb = a.astype(bf16), b.astype(bf16)
    return _tn(ab, bb), (ab, bb)


def _btn_bwd(res, g):
    ab, bb = res
    gb = g.astype(bf16)
    return _nt(bb, gb), _nn(ab, gb)


btn.defvjp(_btn_fwd, _btn_bwd)

SCAN_CHUNKS = 2
CUMSUM_PASSES = 3
EXPAND_PASSES = 2


def _bf16_pieces(a, passes):
    pieces, rest = [], a
    for _ in range(passes):
        piece = rest.astype(bf16)
        pieces.append(piece)
        rest = rest - piece.astype(f32)
    return pieces


def _make_dot01(form, passes):
    fwd, bwd = {"right": (lambda b, p: _nn(p, b), lambda b, g: _nt(g, b)),
                "left": (lambda b, p: _nn(b, p), lambda b, g: _tn(b, g)),
                "tn": (lambda b, p: _tn(p, b), lambda b, g: _nt(b, g))}[form]

    def run(op, b, v):
        out = None
        for piece in _bf16_pieces(v, passes):
            term = op(b, piece)
            out = term if out is None else out + term
        return out

    @jax.custom_vjp
    def product(b, v):
        return run(fwd, b, v)

    product.defvjp(lambda b, v: (run(fwd, b, v), b), lambda b, g: (jnp.zeros_like(b), run(bwd, b, g)))
    return product


_DOT01 = {(form, passes): _make_dot01(form, passes)
          for form, passes in (("left", CUMSUM_PASSES), ("tn", CUMSUM_PASSES), ("right", EXPAND_PASSES))}


def _dot01(b01, a, form, passes):
    return _DOT01[form, passes](b01.astype(bf16), a)


def _rms(x, g):
    return x * lax.rsqrt(jnp.mean(x * x, axis=-1, keepdims=True) + EPS) * g


def _gated_norm(y, z, g):
    v = y * jax.nn.silu(z)
    half = v.shape[-1] // SSD_GROUPS
    parts = []
    for k in range(SSD_GROUPS):
        vk = v[:, k * half:(k + 1) * half]
        parts.append(vk * lax.rsqrt(jnp.mean(vk * vk, axis=-1, keepdims=True) + EPS))
    return jnp.concatenate(parts, axis=-1) * g


def _ln_silu(v, g, b):
    mu = jnp.mean(v, axis=-1, keepdims=True)
    xc = v - mu
    y = xc * lax.rsqrt(jnp.mean(xc * xc, axis=-1, keepdims=True) + EPS) * g + b
    return jax.nn.silu(y)


def _acc_init(i, *refs):
    @pl.when(i == 0)
    def _():
        for r in refs:
            r[...] = jnp.zeros_like(r)


def _after_token(body, token):
    if token is None:
        return body, [], []

    def body_after(tok_ref, *refs):
        del tok_ref
        body(*refs)

    return body_after, [token], [pl.BlockSpec(memory_space=pl.ANY)]


def _row_spec(tm, n):
    return pl.BlockSpec((tm, n), lambda i: (i, 0))


def _const_spec(shape):
    nd = len(shape)
    return pl.BlockSpec(shape, lambda i: (0,) * nd)


def _prev_halo_spec(tm, halo, n):
    return pl.BlockSpec((halo, n), lambda i: (jnp.maximum(i * (tm // halo) - 1, 0), 0))


def _next_halo_spec(tm, halo, n, t):
    return pl.BlockSpec((halo, n), lambda i: (jnp.minimum((i + 1) * (tm // halo), t // halo - 1), 0))


def _in_proj(x, g1, w_in_p, token=None):
    t = x.shape[0]
    tm = min(TOKEN_TILE, t)

    def body(x_ref, g_ref, w_ref, u_ref, z_ref, xbc_ref, dt_ref, cv_ref, cg_ref):
        u = _rms(x_ref[...], g_ref[...]).astype(bf16)
        u_ref[...] = u
        for (lo, hi), o_ref in zip(IN_SEGS, (z_ref, xbc_ref, dt_ref, cv_ref, cg_ref)):
            o_ref[...] = _nt(u, w_ref[lo:hi, :])

    widths = [hi - lo for lo, hi in IN_SEGS]
    outs = [jax.ShapeDtypeStruct((t, D_MODEL), bf16)] + [jax.ShapeDtypeStruct((t, n), f32) for n in widths]
    body, tok, tok_spec = _after_token(body, token)
    return pl.pallas_call(
        body, name="in_proj", grid=(t // tm,),
        in_specs=tok_spec + [_row_spec(tm, D_MODEL), _const_spec((1, D_MODEL)), _const_spec((IN_WIDTH, D_MODEL))],
        out_specs=[_row_spec(tm, D_MODEL)] + [_row_spec(tm, n) for n in widths],
        out_shape=outs, compiler_params=_params("arbitrary"),
    )(*tok, x, g1, w_in_p)


ROW_BLOCK = 16
LANE_CHUNK = 512


def _blocks(rows, cols):
    return [(slice(r, r + ROW_BLOCK), slice(c, c + LANE_CHUNK))
            for c in range(0, cols, LANE_CHUNK) for r in range(0, rows, ROW_BLOCK)]


def _conv4_block(ext_ref, w_ref, b_ref, rs, ls):
    acc = b_ref[:, ls] + w_ref[0:1, ls] * ext_ref[rs.start + 8 - 3:rs.stop + 8 - 3, ls]
    for k in range(1, SSD_CONV):
        acc = acc + w_ref[k:k + 1, ls] * ext_ref[rs.start + 8 - 3 + k:rs.stop + 8 - 3 + k, ls]
    return acc


def _ssd_pre(xbc_raw, dt_raw, cw, cb, dt_bias):
    t = xbc_raw.shape[0]
    tm = min(TOKEN_TILE, t)

    def body(cur_ref, halo_ref, dt_ref, w_ref, b_ref, dtb_ref, act_ref, dto_ref, ext_ref):
        i = pl.program_id(0)
        ext_ref[0:8, :] = jnp.where(i == 0, 0.0, halo_ref[...])
        ext_ref[8:, :] = cur_ref[...]
        for rs, ls in _blocks(tm, XBC_WIDTH):
            act_ref[rs, ls] = jax.nn.silu(_conv4_block(ext_ref, w_ref, b_ref, rs, ls))
        dto_ref[...] = jax.nn.softplus(dt_ref[...] + dtb_ref[...])

    return pl.pallas_call(
        body, name="ssd_pre", grid=(t // tm,),
        in_specs=[_row_spec(tm, XBC_WIDTH), _prev_halo_spec(tm, 8, XBC_WIDTH), _row_spec(tm, SSD_HEADS),
                  _const_spec((8, XBC_WIDTH)), _const_spec((1, XBC_WIDTH)), _const_spec((1, SSD_HEADS))],
        out_specs=[_row_spec(tm, XBC_WIDTH), _row_spec(tm, SSD_HEADS)],
        out_shape=(jax.ShapeDtypeStruct((t, XBC_WIDTH), f32), jax.ShapeDtypeStruct((t, SSD_HEADS), f32)),
        scratch_shapes=[pltpu.VMEM((tm + 8, XBC_WIDTH), f32)],
        compiler_params=_params("arbitrary"),
    )(xbc_raw, xbc_raw, dt_raw, cw, cb, dt_bias)


def _ssd_consts():
    r = lax.broadcasted_iota(jnp.int32, (CHUNK, CHUNK), 0)
    c = lax.broadcasted_iota(jnp.int32, (CHUNK, CHUNK), 1)
    causal = r >= c
    tril = causal.astype(f32)
    triu = (r <= c).astype(f32)
    hh = lax.broadcasted_iota(jnp.int32, (SSD_HEADS, D_MODEL), 0)
    jj = lax.broadcasted_iota(jnp.int32, (SSD_HEADS, D_MODEL), 1)
    expand = (lax.shift_right_logical(jj, 6) == hh).astype(f32)
    lane = lax.broadcasted_iota(jnp.int32, (1, 2 * SSD_HEAD_DIM), 1)
    m0 = (lane < SSD_HEAD_DIM).astype(f32)
    return causal, tril, triu, expand, m0, 1.0 - m0


def _ssd_chunk(xs, bm, cm, dt, s_prev, a_log, d_skip, consts):
    causal, tril, triu, expand, m0, m1 = consts
    a = dt * (-jnp.exp(a_log))
    cs = _dot01(tril, a, "left", CUMSUM_PASSES)
    cs_t = _dot01(triu, a, "tn", CUMSUM_PASSES)
    cs_last = cs[CHUNK - 1:CHUNK, :]
    per_head = jnp.concatenate([dt, jnp.exp(cs_last - cs), jnp.exp(cs), jnp.broadcast_to(jnp.exp(cs_last), (8, SSD_HEADS)),
                                jnp.broadcast_to(d_skip, (8, SSD_HEADS))], axis=0)
    per_channel = _dot01(expand, per_head, "right", EXPAND_PASSES)
    dt_e, dec_end, dec_start = (per_channel[i * CHUNK:(i + 1) * CHUNK] for i in range(3))
    chunk_dec = per_channel[3 * CHUNK:3 * CHUNK + 1]
    d_e = per_channel[3 * CHUNK + 8:3 * CHUNK + 9]
    xc = xs * dt_e
    x_dec = xc * dec_end
    gw = D_MODEL // SSD_GROUPS
    ys, states = [], []
    for g in range(SSD_GROUPS):
        bg = bm[:, g * SSD_STATE:(g + 1) * SSD_STATE]
        cg = cm[:, g * SSD_STATE:(g + 1) * SSD_STATE]
        sp = s_prev[:, g * gw:(g + 1) * gw]
        states.append(sp * chunk_dec[:, g * gw:(g + 1) * gw] + btn(bg, x_dec[:, g * gw:(g + 1) * gw]))
        y_off = bnn(cg, sp) * dec_start[:, g * gw:(g + 1) * gw]
        scores = bnt(cg, bg)
        pieces = []
        for pr in range(gw // (2 * SSD_HEAD_DIM)):
            lo = g * gw + pr * 2 * SSD_HEAD_DIM
            xp = xc[:, lo:lo + 2 * SSD_HEAD_DIM]
            mats = []
            for h in (lo // SSD_HEAD_DIM, lo // SSD_HEAD_DIM + 1):
                seg = cs[:, h:h + 1] - cs_t[h:h + 1, :]
                mats.append(scores * jnp.exp(jnp.where(causal, seg, -jnp.inf)))
            pieces.append(bnn(jnp.concatenate(mats, axis=1), jnp.concatenate([xp * m0, xp * m1], axis=0)))
        ys.append(jnp.concatenate(pieces, axis=-1) + y_off)
    y = jnp.concatenate(ys, axis=-1) + xs * d_e
    return y, jnp.concatenate(states, axis=-1)


def _ssd_scan(xbc_act, dt, a_log, d_skip):
    t = xbc_act.shape[0]
    nc = t // CHUNK
    rows = SCAN_CHUNKS * CHUNK

    def body(xbc_ref, dt_ref, al_ref, dk_ref, y_ref, sp_ref, state_ref):
        c = pl.program_id(0)

        @pl.when(c == 0)
        def _():
            state_ref[...] = jnp.zeros_like(state_ref)

        state = state_ref[...]
        consts = _ssd_consts()
        for j in range(SCAN_CHUNKS):
            rs = slice(j * CHUNK, (j + 1) * CHUNK)
            sp_ref[j] = state
            y_ref[rs, :], state = _ssd_chunk(xbc_ref[rs, 0:1024], xbc_ref[rs, 1024:1280], xbc_ref[rs, 1280:1536],
                                             dt_ref[rs, :], state, al_ref[...], dk_ref[...], consts)
        state_ref[...] = state

    return pl.pallas_call(
        body, name="ssd_scan", grid=(nc // SCAN_CHUNKS,),
        in_specs=[_row_spec(rows, XBC_WIDTH), _row_spec(rows, SSD_HEADS), _const_spec((1, SSD_HEADS)),
                  _const_spec((1, SSD_HEADS))],
        out_specs=[_row_spec(rows, D_MODEL), pl.BlockSpec((SCAN_CHUNKS, SSD_STATE, D_MODEL), lambda c: (c, 0, 0))],
        out_shape=(jax.ShapeDtypeStruct((t, D_MODEL), f32), jax.ShapeDtypeStruct((nc, SSD_STATE, D_MODEL), f32)),
        scratch_shapes=[pltpu.VMEM((SSD_STATE, D_MODEL), f32)],
        compiler_params=_params("arbitrary"),
    )(xbc_act, dt, a_log, d_skip)


CONV_HALO = 32
CONV_BLOCK = 64
LANE_TILE = 128
SUBLANES = 8


def _fill_glu_ext(ext_ref, sg_ref, cv_ref, cg_ref, hcv_ref, hcg_ref, i):
    ext_ref[0:CONV_HALO, :] = jnp.where(i == 0, 0.0, hcv_ref[...] * jax.nn.sigmoid(hcg_ref[...]))
    sg = jax.nn.sigmoid(cg_ref[...])
    if sg_ref is not None:
        sg_ref[...] = sg
    ext_ref[CONV_HALO:, :] = cv_ref[...] * sg


def _shifted_copies(src_ref, dst_ref, rows):
    for s in range(1, SUBLANES):
        dst_ref[s, 0:rows, :] = src_ref[pl.ds(s, rows), :]


def _tap_source(src_ref, shifted_ref, offset):
    s = offset % SUBLANES
    return (src_ref if s == 0 else shifted_ref.at[s]), offset - s


def _conv_rows(src_ref, shifted_ref, taps, offsets, lanes, r0, init):
    accs = [init] * (CONV_BLOCK // SUBLANES)
    for k, off in enumerate(offsets):
        ref, base = _tap_source(src_ref, shifted_ref, off)
        for j in range(len(accs)):
            accs[j] = accs[j] + taps[k] * ref[pl.ds(r0 + base + SUBLANES * j, SUBLANES), lanes]
    return accs


def _conf_conv(cv, cg, w, b):
    t = cv.shape[0]
    tm = min(TOKEN_TILE, t)
    ext_rows = tm + CONV_HALO
    offsets = [CONV_HALO - (CONF_KERNEL - 1) + k for k in range(CONF_KERNEL)]

    def body(cv_ref, cg_ref, hcv_ref, hcg_ref, w_ref, b_ref, o_ref, ext_ref, shifted_ref):
        _fill_glu_ext(ext_ref, None, cv_ref, cg_ref, hcv_ref, hcg_ref, pl.program_id(0))
        _shifted_copies(ext_ref, shifted_ref, ext_rows - SUBLANES)
        for lb in range(1024 // LANE_TILE):
            lanes = slice(lb * LANE_TILE, (lb + 1) * LANE_TILE)
            taps = [jnp.broadcast_to(w_ref[k:k + 1, lanes], (SUBLANES, LANE_TILE)) for k in range(CONF_KERNEL)]
            bias = jnp.broadcast_to(b_ref[:, lanes], (SUBLANES, LANE_TILE))

            def rows(rb, carry, lanes=lanes, taps=taps, bias=bias):
                r0 = pl.multiple_of(rb * CONV_BLOCK, CONV_BLOCK)
                accs = _conv_rows(ext_ref, shifted_ref, taps, offsets, lanes, r0, bias)
                for j, a in enumerate(accs):
                    o_ref[pl.ds(r0 + SUBLANES * j, SUBLANES), lanes] = a
                return carry

            lax.fori_loop(0, tm // CONV_BLOCK, rows, 0)

    return pl.pallas_call(
        body, name="conf_conv", grid=(t // tm,),
        in_specs=[_row_spec(tm, 1024), _row_spec(tm, 1024), _prev_halo_spec(tm, CONV_HALO, 1024),
                  _prev_halo_spec(tm, CONV_HALO, 1024), _const_spec((32, 1024)), _const_spec((1, 1024))],
        out_specs=_row_spec(tm, 1024), out_shape=jax.ShapeDtypeStruct((t, 1024), f32),
        scratch_shapes=[pltpu.VMEM((ext_rows, 1024), f32), pltpu.VMEM((SUBLANES, ext_rows, 1024), f32)],
        compiler_params=_params("arbitrary"),
    )(cv, cg, cv, cg, w, b)


def _out_proj(y, z, v2, x, g_ssd, ln_g, ln_b, w_out, g2):
    t = x.shape[0]
    tm = min(TOKEN_TILE, t)

    def body(y_ref, z_ref, v_ref, x_ref, gs_ref, lg_ref, lb_ref, w_ref, g2_ref, ys_ref, yc_ref, h1_ref, u2_ref):
        ys = _gated_norm(y_ref[...], z_ref[...], gs_ref[...]).astype(bf16)
        yc = _ln_silu(v_ref[...], lg_ref[...], lb_ref[...]).astype(bf16)
        ys_ref[...] = ys
        yc_ref[...] = yc
        h1 = x_ref[...] + _nn(ys, w_ref[0:1024, :]) + _nn(yc, w_ref[1024:2048, :])
        h1_ref[...] = h1
        u2_ref[...] = _rms(h1, g2_ref[...]).astype(bf16)

    vec = _const_spec((1, 1024))
    row = _row_spec(tm, 1024)
    return pl.pallas_call(
        body, name="out_proj", grid=(t // tm,),
        in_specs=[row, row, row, row, vec, vec, vec, _const_spec((2048, 1024)), vec],
        out_specs=[row, row, row, row],
        out_shape=(jax.ShapeDtypeStruct((t, 1024), bf16), jax.ShapeDtypeStruct((t, 1024), bf16),
                   jax.ShapeDtypeStruct((t, 1024), f32), jax.ShapeDtypeStruct((t, 1024), bf16)),
        compiler_params=_params("arbitrary"),
    )(y, z, v2, x, g_ssd, ln_g, ln_b, w_out, g2)


def _mlp_up(u2, w_up):
    t = u2.shape[0]
    tm = min(TOKEN_TILE, t)
    blk = D_FF // N_DEV

    def body(u_ref, w_ref, pre_ref, hs_ref):
        u = u_ref[...]
        for k in range(N_DEV):
            pre = _nn(u, w_ref[k])
            pre_ref[:, k * blk:(k + 1) * blk] = pre.astype(bf16)
            r = jnp.maximum(pre, 0.0)
            hs_ref[:, k * blk:(k + 1) * blk] = (r * r).astype(bf16)

    return pl.pallas_call(
        body, name="mlp_up", grid=(t // tm,),
        in_specs=[_row_spec(tm, D_MODEL), _const_spec((N_DEV, D_MODEL, D_FF // N_DEV))],
        out_specs=[_row_spec(tm, D_FF), _row_spec(tm, D_FF)],
        out_shape=(jax.ShapeDtypeStruct((t, D_FF), bf16), jax.ShapeDtypeStruct((t, D_FF), bf16)),
        compiler_params=_params("arbitrary"),
    )(u2, w_up)


def _mlp_down(h1, hs, w_down):
    t = h1.shape[0]
    tm = min(TOKEN_TILE, t)

    def body(h_ref, hs_ref, w_ref, o_ref):
        o_ref[...] = h_ref[...] + _nn(hs_ref[...], w_ref[...])

    return pl.pallas_call(
        body, name="mlp_down", grid=(t // tm,),
        in_specs=[_row_spec(tm, D_MODEL), _row_spec(tm, D_FF), _const_spec((D_FF, D_MODEL))],
        out_specs=_row_spec(tm, D_MODEL), out_shape=jax.ShapeDtypeStruct((t, D_MODEL), f32),
        compiler_params=_params("arbitrary"),
    )(h1, hs, w_down)


TAIL_LOSS, TAIL_FINAL_G, TAIL_PLE_G, TAIL_GATE_B, TAIL_GATE_NORM_G = 0, 1, 2, 3, 4


def _tail(h2, p, target, g3, w_gate, b_gate, w_ple, ple_g, fin_g):
    t = h2.shape[0]
    tm = min(TOKEN_TILE, t)

    def body(h_ref, p_ref, t_ref, g3_ref, wg_ref, bg_ref, wp_ref, pg_ref, fg_ref,
             dh_ref, dhb_ref, dwg_ref, dwp_ref, acc_ref):
        i = pl.program_id(0)
        _acc_init(i, dwg_ref, dwp_ref, acc_ref)
        h2v = h_ref[...]
        tgt = t_ref[...]
        u3, vjp_u3 = jax.vjp(_rms, h2v, g3_ref[...])
        u3b = u3.astype(bf16)
        pb = p_ref[...].astype(bf16)
        gate_pre = _nn(u3b, wg_ref[...]) + bg_ref[...]
        emb_pre = _nn(pb, wp_ref[...])

        def tail_fn(hv, gp, ep, pg, fg):
            h3 = hv + jax.nn.sigmoid(gp) * _rms(ep, pg)
            err = _rms(h3, fg) - tgt
            return 0.5 * jnp.mean(err * err, axis=-1, keepdims=True)

        loss_tok, vjp_tail = jax.vjp(tail_fn, h2v, gate_pre, emb_pre, pg_ref[...], fg_ref[...])
        dh_a, dgp, dep, dpg, dfg = vjp_tail(jnp.ones_like(loss_tok))
        dgpb = dgp.astype(bf16)
        dh_b, dg3 = vjp_u3(_nt(dgpb, wg_ref[...]))
        dh = dh_a + dh_b
        dh_ref[...] = dh
        dhb_ref[...] = dh.astype(bf16)
        dwg_ref[...] += _tn(u3b, dgpb)
        dwp_ref[...] += _tn(pb, dep.astype(bf16))
        acc_ref[TAIL_LOSS:TAIL_LOSS + 1, :] += jnp.broadcast_to(jnp.sum(loss_tok, axis=0, keepdims=True), (1, 1024))
        acc_ref[TAIL_FINAL_G:TAIL_FINAL_G + 1, :] += dfg
        acc_ref[TAIL_PLE_G:TAIL_PLE_G + 1, :] += dpg
        acc_ref[TAIL_GATE_B:TAIL_GATE_B + 1, :] += jnp.sum(dgp, axis=0, keepdims=True)
        acc_ref[TAIL_GATE_NORM_G:TAIL_GATE_NORM_G + 1, :] += dg3

    vec = _const_spec((1, 1024))
    row = _row_spec(tm, 1024)
    return pl.pallas_call(
        body, name="tail", grid=(t // tm,),
        in_specs=[row, _row_spec(tm, PLE_DIM), row, vec, _const_spec((1024, 1024)), vec, _const_spec((PLE_DIM, 1024)), vec, vec],
        out_specs=[row, row, _const_spec((1024, 1024)), _const_spec((PLE_DIM, 1024)), _const_spec((8, 1024))],
        out_shape=(jax.ShapeDtypeStruct((t, 1024), f32), jax.ShapeDtypeStruct((t, 1024), bf16),
                   jax.ShapeDtypeStruct((1024, 1024), f32), jax.ShapeDtypeStruct((PLE_DIM, 1024), f32),
                   jax.ShapeDtypeStruct((8, 1024), f32)),
        compiler_params=_params("arbitrary"),
    )(h2, p, target, g3, w_gate, b_gate, w_ple, ple_g, fin_g)


def _mlp_bwd(dh2, dh2b, pre, h1, g2, w_down, w_up, token=None):
    t = dh2.shape[0]
    tm = min(TOKEN_TILE, t)
    blk = D_FF // N_DEV

    def body(dh_ref, dhb_ref, pre_ref, h1_ref, g_ref, wd_hbm, wu_hbm, dpre_ref, dh1_ref, dh1b_ref, acc_ref,
             wd_ref, wu_ref):
        i = pl.program_id(0)
        _acc_init(i, acc_ref)

        @pl.when(i == 0)
        def _():
            pltpu.sync_copy(wd_hbm, wd_ref)
            pltpu.sync_copy(wu_hbm, wu_ref)

        dhb = dhb_ref[...]
        du2 = jnp.zeros((tm, D_MODEL), f32)
        for k in range(D_FF // blk):
            dhs = _nt(dhb, wd_ref[k * blk:(k + 1) * blk, :])
            dpre = (dhs * (2.0 * jnp.maximum(pre_ref[:, k * blk:(k + 1) * blk].astype(f32), 0.0))).astype(bf16)
            dpre_ref[:, k * blk:(k + 1) * blk] = dpre
            du2 = du2 + _nt(dpre, wu_ref[k])
        _, vjp_u2 = jax.vjp(_rms, h1_ref[...], g_ref[...])
        d, dg = vjp_u2(du2)
        dh1 = dh_ref[...] + d
        dh1_ref[...] = dh1
        dh1b_ref[...] = dh1.astype(bf16)
        acc_ref[0:1, :] += dg

    row = _row_spec(tm, 1024)
    body, tok, tok_spec = _after_token(body, token)
    return pl.pallas_call(
        body, name="mlp_bwd", grid=(t // tm,),
        in_specs=tok_spec + [row, row, _row_spec(tm, D_FF), row, _const_spec((1, 1024)), HBM_SPEC, HBM_SPEC],
        out_specs=[_row_spec(tm, D_FF), row, row, _const_spec((8, 1024))],
        out_shape=(jax.ShapeDtypeStruct((t, D_FF), bf16), jax.ShapeDtypeStruct((t, 1024), f32),
                   jax.ShapeDtypeStruct((t, 1024), bf16), jax.ShapeDtypeStruct((8, 1024), f32)),
        scratch_shapes=[pltpu.VMEM((D_FF, D_MODEL), bf16), pltpu.VMEM((N_DEV, D_MODEL, D_FF // N_DEV), bf16)],
        compiler_params=_params("arbitrary"),
    )(*tok, dh2, dh2b, pre, h1, g2, w_down, w_up)


OPB_SSD_G, OPB_LN_G, OPB_LN_B, OPB_CONV_B = 0, 1, 2, 3


def _out_proj_bwd(dh1b, y, z, v2, g_ssd, ln_g, ln_b, w_out, token=None):
    t = y.shape[0]
    tm = min(TOKEN_TILE, t)

    def body(dh_ref, y_ref, z_ref, v_ref, gs_ref, lg_ref, lb_ref, w_ref, dy_ref, dz_ref, dv_ref, acc_ref):
        i = pl.program_id(0)
        _acc_init(i, acc_ref)
        dhb = dh_ref[...]
        dys = _nt(dhb, w_ref[0:1024, :])
        dyc = _nt(dhb, w_ref[1024:2048, :])
        _, vjp_g = jax.vjp(_gated_norm, y_ref[...], z_ref[...], gs_ref[...])
        dy, dz, dgs = vjp_g(dys)
        _, vjp_l = jax.vjp(_ln_silu, v_ref[...], lg_ref[...], lb_ref[...])
        dv, dlg, dlb = vjp_l(dyc)
        dy_ref[...] = dy
        dz_ref[...] = dz.astype(bf16)
        dv_ref[...] = dv
        acc_ref[OPB_SSD_G:OPB_SSD_G + 1, :] += dgs
        acc_ref[OPB_LN_G:OPB_LN_G + 1, :] += dlg
        acc_ref[OPB_LN_B:OPB_LN_B + 1, :] += dlb
        acc_ref[OPB_CONV_B:OPB_CONV_B + 1, :] += jnp.sum(dv, axis=0, keepdims=True)

    vec = _const_spec((1, 1024))
    row = _row_spec(tm, 1024)
    body, tok, tok_spec = _after_token(body, token)
    return pl.pallas_call(
        body, name="out_proj_bwd", grid=(t // tm,),
        in_specs=tok_spec + [row, row, row, row, vec, vec, vec, _const_spec((2048, 1024))],
        out_specs=[row, row, row, _const_spec((8, 1024))],
        out_shape=(jax.ShapeDtypeStruct((t, 1024), f32), jax.ShapeDtypeStruct((t, 1024), bf16),
                   jax.ShapeDtypeStruct((t, 1024), f32), jax.ShapeDtypeStruct((8, 1024), f32)),
        compiler_params=_params("arbitrary"),
    )(*tok, dh1b, y, z, v2, g_ssd, ln_g, ln_b, w_out)


def _conf_conv_bwd(dv2, cv, cg, w):
    t = cv.shape[0]
    tm = min(TOKEN_TILE, t)
    ext_rows = tm + CONV_HALO
    offsets = [CONF_KERNEL - 1 - k for k in range(CONF_KERNEL)]

    def body(dv_ref, dvn_ref, cv_ref, cg_ref, w_ref, dcv_ref, dcg_ref, dw_ref, glu_ref, dext_ref, sg_ref, shifted_ref):
        i = pl.program_id(0)
        _acc_init(i, dw_ref)
        sg = jax.nn.sigmoid(cg_ref[...])
        sg_ref[...] = sg
        glu_ref[...] = cv_ref[...] * sg
        dext_ref[0:tm, :] = dv_ref[...]
        dext_ref[tm:, :] = jnp.where(i == pl.num_programs(0) - 1, 0.0, dvn_ref[...])
        _shifted_copies(dext_ref, shifted_ref, ext_rows - SUBLANES)

        for lb in range(1024 // LANE_TILE):
            lanes = slice(lb * LANE_TILE, (lb + 1) * LANE_TILE)

            def rows_w(rb, accs, lanes=lanes):
                r0 = pl.multiple_of(rb * CONV_BLOCK, CONV_BLOCK)
                accs = list(accs)
                for j in range(CONV_BLOCK // SUBLANES):
                    x = glu_ref[pl.ds(r0 + SUBLANES * j, SUBLANES), lanes]
                    for k, off in enumerate(offsets):
                        ref, base = _tap_source(dext_ref, shifted_ref, off)
                        accs[k] = accs[k] + x * ref[pl.ds(r0 + base + SUBLANES * j, SUBLANES), lanes]
                return tuple(accs)

            zero = jnp.zeros((SUBLANES, LANE_TILE), f32)
            accs = lax.fori_loop(0, tm // CONV_BLOCK, rows_w, (zero,) * CONF_KERNEL)
            for k in range(CONF_KERNEL):
                dw_ref[k:k + 1, lanes] += jnp.sum(accs[k], axis=0, keepdims=True)

        for lb in range(1024 // LANE_TILE):
            lanes = slice(lb * LANE_TILE, (lb + 1) * LANE_TILE)
            taps = [jnp.broadcast_to(w_ref[k:k + 1, lanes], (SUBLANES, LANE_TILE)) for k in range(CONF_KERNEL)]

            def rows_x(rb, carry, lanes=lanes, taps=taps):
                r0 = pl.multiple_of(rb * CONV_BLOCK, CONV_BLOCK)
                zero = jnp.zeros((SUBLANES, LANE_TILE), f32)
                dglu = jnp.concatenate(_conv_rows(dext_ref, shifted_ref, taps, offsets, lanes, r0, zero), axis=0)
                sg = sg_ref[pl.ds(r0, CONV_BLOCK), lanes]
                cvv = cv_ref[pl.ds(r0, CONV_BLOCK), lanes]
                dcv_ref[pl.ds(r0, CONV_BLOCK), lanes] = (dglu * sg).astype(bf16)
                dcg_ref[pl.ds(r0, CONV_BLOCK), lanes] = (dglu * cvv * sg * (1.0 - sg)).astype(bf16)
                return carry

            lax.fori_loop(0, tm // CONV_BLOCK, rows_x, 0)

    row = _row_spec(tm, 1024)
    return pl.pallas_call(
        body, name="conf_conv_bwd", grid=(t // tm,),
        in_specs=[row, _next_halo_spec(tm, CONV_HALO, 1024, t), row, row, _const_spec((32, 1024))],
        out_specs=[row, row, _const_spec((32, 1024))],
        out_shape=(jax.ShapeDtypeStruct((t, 1024), bf16), jax.ShapeDtypeStruct((t, 1024), bf16),
                   jax.ShapeDtypeStruct((32, 1024), f32)),
        scratch_shapes=[pltpu.VMEM((tm, 1024), f32), pltpu.VMEM((ext_rows, 1024), f32), pltpu.VMEM((tm, 1024), f32),
                        pltpu.VMEM((SUBLANES, ext_rows, 1024), f32)],
        compiler_params=_params("arbitrary"),
    )(dv2, dv2, cv, cg, w)


def _ssd_scan_bwd(xbc_act, dt, s_prev, dy, a_log, d_skip):
    t = xbc_act.shape[0]
    steps = t // CHUNK // SCAN_CHUNKS
    rows = SCAN_CHUNKS * CHUNK

    def body(xbc_ref, dt_ref, sp_ref, dy_ref, al_ref, dk_ref, dxbc_ref, ddt_ref, dal_ref, ddk_ref, ds_ref):
        i = pl.program_id(0)
        _acc_init(i, ds_ref, dal_ref, ddk_ref)
        consts = _ssd_consts()
        ds = ds_ref[...]
        dal_sum, ddk_sum = dal_ref[...], ddk_ref[...]
        for j in reversed(range(SCAN_CHUNKS)):
            rs = slice(j * CHUNK, (j + 1) * CHUNK)
            _, vjp_c = jax.vjp(
                functools.partial(_ssd_chunk, consts=consts),
                xbc_ref[rs, 0:1024], xbc_ref[rs, 1024:1280], xbc_ref[rs, 1280:1536], dt_ref[rs, :], sp_ref[j],
                al_ref[...], dk_ref[...])
            dxs, dbm, dcm, ddt, ds, dal, ddk = vjp_c((dy_ref[rs, :], ds))
            dxbc_ref[rs, 0:1024] = dxs
            dxbc_ref[rs, 1024:1280] = dbm
            dxbc_ref[rs, 1280:1536] = dcm
            ddt_ref[rs, :] = ddt
            dal_sum, ddk_sum = dal_sum + dal, ddk_sum + ddk
        ds_ref[...] = ds
        dal_ref[...] = dal_sum
        ddk_ref[...] = ddk_sum

    rev = lambda i: (steps - 1 - i, 0)
    return pl.pallas_call(
        body, name="ssd_scan_bwd", grid=(steps,),
        in_specs=[pl.BlockSpec((rows, XBC_WIDTH), rev), pl.BlockSpec((rows, SSD_HEADS), rev),
                  pl.BlockSpec((SCAN_CHUNKS, SSD_STATE, D_MODEL), lambda i: (steps - 1 - i, 0, 0)),
                  pl.BlockSpec((rows, D_MODEL), rev), _const_spec((1, SSD_HEADS)), _const_spec((1, SSD_HEADS))],
        out_specs=[pl.BlockSpec((rows, XBC_WIDTH), rev), pl.BlockSpec((rows, SSD_HEADS), rev),
                   _const_spec((1, SSD_HEADS)), _const_spec((1, SSD_HEADS))],
        out_shape=(jax.ShapeDtypeStruct((t, XBC_WIDTH), f32), jax.ShapeDtypeStruct((t, SSD_HEADS), f32),
                   jax.ShapeDtypeStruct((1, SSD_HEADS), f32), jax.ShapeDtypeStruct((1, SSD_HEADS), f32)),
        scratch_shapes=[pltpu.VMEM((SSD_STATE, D_MODEL), f32)],
        compiler_params=_params("arbitrary"),
    )(xbc_act, dt, s_prev, dy, a_log, d_skip)


def _ssd_pre_bwd(xbc_raw, dxbc_act, ddt, dt_raw, cw, cb, dt_bias):
    t = xbc_raw.shape[0]
    tm = min(TOKEN_TILE, t)

    def body(cur_ref, halo_ref, dact_ref, ddt_ref, dtr_ref, w_ref, b_ref, dtb_ref,
             dco_ref, ddtr_ref, dw_ref, db_ref, ddtb_ref, ext_ref):
        i = pl.program_id(0)
        _acc_init(i, dw_ref, db_ref, ddtb_ref)
        ext_ref[0:8, :] = jnp.where(i == 0, 0.0, halo_ref[...])
        ext_ref[8:, :] = cur_ref[...]
        fold = lambda a: a[0:8] + a[8:ROW_BLOCK]
        for c in range(0, XBC_WIDTH, LANE_CHUNK):
            ls = slice(c, c + LANE_CHUNK)
            part_b = jnp.zeros((8, LANE_CHUNK), f32)
            part_w = [jnp.zeros((8, LANE_CHUNK), f32)] * SSD_CONV
            for r in range(0, tm, ROW_BLOCK):
                rs = slice(r, r + ROW_BLOCK)
                co = _conv4_block(ext_ref, w_ref, b_ref, rs, ls)
                sg = jax.nn.sigmoid(co)
                dco = dact_ref[rs, ls] * sg * (1.0 + co * (1.0 - sg))
                dco_ref[rs, ls] = dco
                part_b = part_b + fold(dco)
                part_w = [pw + fold(dco * ext_ref[r + 8 - 3 + k:r + ROW_BLOCK + 8 - 3 + k, ls])
                          for k, pw in enumerate(part_w)]
            db_ref[:, ls] += jnp.sum(part_b, axis=0, keepdims=True)
            for k in range(SSD_CONV):
                dw_ref[k:k + 1, ls] += jnp.sum(part_w[k], axis=0, keepdims=True)
        ddtr = ddt_ref[...] * jax.nn.sigmoid(dtr_ref[...] + dtb_ref[...])
        ddtb_ref[...] += jnp.sum(ddtr, axis=0, keepdims=True)
        ddtr_ref[...] = ddtr.astype(bf16)

    return pl.pallas_call(
        body, name="ssd_pre_bwd", grid=(t // tm,),
        in_specs=[_row_spec(tm, XBC_WIDTH), _prev_halo_spec(tm, 8, XBC_WIDTH), _row_spec(tm, XBC_WIDTH),
                  _row_spec(tm, SSD_HEADS), _row_spec(tm, SSD_HEADS), _const_spec((8, XBC_WIDTH)),
                  _const_spec((1, XBC_WIDTH)), _const_spec((1, SSD_HEADS))],
        out_specs=[_row_spec(tm, XBC_WIDTH), _row_spec(tm, SSD_HEADS), _const_spec((8, XBC_WIDTH)),
                   _const_spec((1, XBC_WIDTH)), _const_spec((1, SSD_HEADS))],
        out_shape=(jax.ShapeDtypeStruct((t, XBC_WIDTH), f32), jax.ShapeDtypeStruct((t, SSD_HEADS), bf16),
                   jax.ShapeDtypeStruct((8, XBC_WIDTH), f32), jax.ShapeDtypeStruct((1, XBC_WIDTH), f32),
                   jax.ShapeDtypeStruct((1, SSD_HEADS), f32)),
        scratch_shapes=[pltpu.VMEM((tm + 8, XBC_WIDTH), f32)],
        compiler_params=_params("arbitrary"),
    )(xbc_raw, xbc_raw, dxbc_act, ddt, dt_raw, cw, cb, dt_bias)


def _conv4_bwd_data(dco, cw):
    t = dco.shape[0]
    tm = min(TOKEN_TILE, t)

    def body(cur_ref, nxt_ref, w_ref, o_ref, ext_ref):
        i = pl.program_id(0)
        ext_ref[0:tm, :] = cur_ref[...]
        ext_ref[tm:, :] = jnp.where(i == pl.num_programs(0) - 1, 0.0, nxt_ref[...])
        for rs, ls in _blocks(tm, XBC_WIDTH):
            acc = w_ref[0:1, ls] * ext_ref[rs.start + SSD_CONV - 1:rs.stop + SSD_CONV - 1, ls]
            for k in range(1, SSD_CONV):
                acc = acc + w_ref[k:k + 1, ls] * ext_ref[rs.start + SSD_CONV - 1 - k:rs.stop + SSD_CONV - 1 - k, ls]
            o_ref[rs, ls] = acc.astype(bf16)

    return pl.pallas_call(
        body, name="conv4_bwd_data", grid=(t // tm,),
        in_specs=[_row_spec(tm, XBC_WIDTH), _next_halo_spec(tm, 8, XBC_WIDTH, t), _const_spec((8, XBC_WIDTH))],
        out_specs=_row_spec(tm, XBC_WIDTH), out_shape=jax.ShapeDtypeStruct((t, XBC_WIDTH), bf16),
        scratch_shapes=[pltpu.VMEM((tm + 8, XBC_WIDTH), f32)],
        compiler_params=_params("arbitrary"),
    )(dco, dco, cw)


def _in_proj_bwd(dproj, x, dh1, g1, w_in_p, token=None):
    t = x.shape[0]
    tm = min(TOKEN_TILE, t)

    def body(dz_ref, dxbc_ref, ddt_ref, dcv_ref, dcg_ref, x_ref, dh_ref, g_ref, w_ref, dx_ref, acc_ref):
        i = pl.program_id(0)
        _acc_init(i, acc_ref)
        du = jnp.zeros((tm, D_MODEL), f32)
        for (lo, hi), r in zip(IN_SEGS, (dz_ref, dxbc_ref, ddt_ref, dcv_ref, dcg_ref)):
            du = du + _nn(r[...], w_ref[lo:hi, :])
        _, vjp_u = jax.vjp(_rms, x_ref[...], g_ref[...])
        d, dg = vjp_u(du)
        dx_ref[...] = dh_ref[...] + d
        acc_ref[0:1, :] += dg

    row = _row_spec(tm, 1024)
    body, tok, tok_spec = _after_token(body, token)
    return pl.pallas_call(
        body, name="in_proj_bwd", grid=(t // tm,),
        in_specs=tok_spec + [_row_spec(tm, hi - lo) for lo, hi in IN_SEGS] + [row, row, _const_spec((1, 1024)),
                                                                               _const_spec((IN_WIDTH, D_MODEL))],
        out_specs=[row, _const_spec((8, 1024))],
        out_shape=(jax.ShapeDtypeStruct((t, 1024), f32), jax.ShapeDtypeStruct((8, 1024), f32)),
        compiler_params=_params("arbitrary"),
    )(*tok, *dproj, x, dh1, g1, w_in_p)


def _mm_tn(a, b, tk, tn, name, column_blocks=False):
    t, kk = a.shape
    n = b.shape[1]
    tk, tn = min(tk, kk), min(tn, n)

    def body(a_ref, b_ref, o_ref):
        o_ref[...] = _tn(a_ref[...], b_ref[...]).astype(bf16)

    if column_blocks:
        assert tk == kk
        out_spec = pl.BlockSpec((None, tk, tn), lambda i, j: (j, 0, 0))
        out_shape = jax.ShapeDtypeStruct((n // tn, kk, tn), bf16)
    else:
        out_spec = pl.BlockSpec((tk, tn), lambda i, j: (i, j))
        out_shape = jax.ShapeDtypeStruct((kk, n), bf16)
    return pl.pallas_call(
        body, name=name, grid=(kk // tk, n // tn),
        in_specs=[pl.BlockSpec((t, tk), lambda i, j: (0, i)), pl.BlockSpec((t, tn), lambda i, j: (0, j))],
        out_specs=out_spec, out_shape=out_shape,
        compiler_params=_params("arbitrary", "arbitrary"),
    )(a, b)


class _LocalWeights:
    def __init__(self, w):
        self.w = w
        self.sent = {}

    def start_token(self):
        return None

    def weight(self, name, after=()):
        del after
        return self.w[name]

    def send_grads(self, grads):
        self.sent.update(grads)
        return None


def _local_step(x, p, target, s, comm):
    conv_w, conf_w = comm.weight("ssd_conv_w"), comm.weight("conf_dw_w")
    w_in = comm.weight("w_in")
    u, z, xbc_raw, dt_raw, cv, cg = _in_proj(x, s["mix_norm_g"], w_in, token=comm.start_token())
    xbc_act, dt = _ssd_pre(xbc_raw, dt_raw, conv_w, s["ssd_conv_b"], s["dt_bias"])
    y, s_prev = _ssd_scan(xbc_act, dt, s["A_log"], s["D_skip"])
    v2 = _conf_conv(cv, cg, conf_w, s["conf_dw_b"])
    w_out = comm.weight("w_out", after=(y, v2))
    ys, yc, h1, u2 = _out_proj(y, z, v2, x, s["ssd_norm_g"], s["conf_ln_g"], s["conf_ln_b"], w_out, s["mlp_norm_g"])
    w_up = comm.weight("w_up", after=(u2,))
    pre, hs = _mlp_up(u2, w_up)
    w_down = comm.weight("w_down", after=(hs,))
    h2 = _mlp_down(h1, hs, w_down)
    w_gate, w_ple = comm.weight("w_ple_gate", after=(h2,)), comm.weight("w_ple", after=(h2,))
    dh2, dh2b, dwg, dwp, tail_acc = _tail(h2, p, target, s["ple_gate_norm_g"], w_gate, s["b_ple_gate"], w_ple,
                                          s["ple_norm_g"], s["final_norm_g"])
    token = comm.send_grads({"w_ple_gate": dwg, "w_ple": dwp})
    dpre, dh1, dh1b, mlp_acc = _mlp_bwd(dh2, dh2b, pre, h1, s["mlp_norm_g"], w_down, w_up, token=token)
    token = comm.send_grads({
        "w_down": _mm_tn(hs, dh2b, 512, 1024, "dw_down"),
        "w_up": _mm_tn(u2, dpre, 1024, D_FF // N_DEV, "dw_up", column_blocks=True),
        "w_out": [_mm_tn(ys, dh1b, 1024, 512, "dw_out_ssd"), _mm_tn(yc, dh1b, 1024, 512, "dw_out_conf")],
    })
    dy, dz, dv2, opb_acc = _out_proj_bwd(dh1b, y, z, v2, s["ssd_norm_g"], s["conf_ln_g"], s["conf_ln_b"], w_out,
                                         token=token)
    dcv, dcg, dconf_w = _conf_conv_bwd(dv2, cv, cg, conf_w)
    dxbc_act, ddt, d_alog, d_dskip = _ssd_scan_bwd(xbc_act, dt, s_prev, dy, s["A_log"], s["D_skip"])
    dco, ddt_raw, dconv_w, dconv_b, d_dtb = _ssd_pre_bwd(xbc_raw, dxbc_act, ddt, dt_raw, conv_w, s["ssd_conv_b"],
                                                        s["dt_bias"])
    dxbc_raw = _conv4_bwd_data(dco, conv_w)
    dproj = (dz, dxbc_raw, ddt_raw, dcv, dcg)
    token = comm.send_grads({"w_in": [_mm_tn(d, u, 512, 1024, "dw_in_" + n)
                                      for n, d in zip(("z", "xbc", "dt", "cv", "cg"), dproj)]})
    grad_x, inp_acc = _in_proj_bwd(dproj, x, dh1, s["mix_norm_g"], w_in, token=token)
    acc = {"in_proj": inp_acc, "out_proj": opb_acc, "mlp": mlp_acc, "tail": tail_acc, "ssd_conv_b": dconv_b,
           "dt_bias": d_dtb, "A_log": d_alog, "D_skip": d_dskip, "ssd_conv_w": dconv_w, "conf_dw_w": dconf_w}
    return grad_x, acc


def _mesh_pos():
    return lax.axis_index("x"), lax.axis_index("y"), lax.axis_index("c")


def _other_chips(x, y):
    return [(1 - x, y), (x, 1 - y), (1 - x, 1 - y)]


def _all_gather(arrs, name):
    n = len(arrs)

    def body(*refs):
        ins, outs = refs[:n], refs[n:2 * n]
        send_sems, recv_sems, local_sems = refs[2 * n:]
        x, y, c = _mesh_pos()
        me = 4 * x + 2 * y + c
        sibling = (x, y, 1 - c)
        chips = _other_chips(x, y)

        def copy(a, k, block, to, src=None):
            dst = outs[a].at[block]
            return pltpu.make_async_remote_copy(
                src_ref=dst if src is None else src, dst_ref=dst, send_sem=send_sems.at[a, k],
                recv_sem=recv_sems.at[a, k], device_id=to, device_id_type=MESH_ID)

        mine = [pltpu.make_async_copy(ins[a], outs[a].at[me], local_sems.at[a]) for a in range(n)]
        for cp in mine:
            cp.start()
        first = []
        for a in range(n):
            first.append(copy(a, 0, me, sibling, src=ins[a]))
            first += [copy(a, 1 + j, me, (px, py, c), src=ins[a]) for j, (px, py) in enumerate(chips)]
        for cp in first:
            cp.start()
        passed = []
        for j, (px, py) in enumerate(chips):
            for a in range(n):
                blk = 4 * px + 2 * py + c
                copy(a, 1 + j, blk, (x, y, c)).wait_recv()
                cp = copy(a, 4 + j, blk, sibling)
                cp.start()
                passed.append(cp)
        for a in range(n):
            copy(a, 0, 4 * x + 2 * y + (1 - c), (x, y, c)).wait_recv()
        for j, (px, py) in enumerate(chips):
            for a in range(n):
                copy(a, 4 + j, 4 * px + 2 * py + (1 - c), (x, y, c)).wait_recv()
        for cp in first + passed:
            cp.wait_send()
        for cp in mine:
            cp.wait()

    return pl.pallas_call(
        body, name=name,
        in_specs=[HBM_SPEC] * n, out_specs=[HBM_SPEC] * n,
        out_shape=[jax.ShapeDtypeStruct((N_DEV,) + a.shape, a.dtype) for a in arrs],
        scratch_shapes=[pltpu.SemaphoreType.DMA((n, 7)), pltpu.SemaphoreType.DMA((n, 7)), pltpu.SemaphoreType.DMA((n,))],
    )(*arrs)


_PEER_FLIPS = ((0, 0, 1), (1, 0, 0), (0, 1, 0), (1, 1, 0), (1, 0, 1), (0, 1, 1), (1, 1, 1))
SEM_SPEC = pl.BlockSpec(memory_space=pltpu.SEMAPHORE)
ANY_SPEC = pl.BlockSpec(memory_space=pl.ANY)


def _flip(v, d):
    return 1 - v if d else v


def _peers(x, y, c):
    out = []
    for dx, dy, dc in _PEER_FLIPS:
        px, py, pc = _flip(x, dx), _flip(y, dy), _flip(c, dc)
        out.append(((px, py, pc), 4 * px + 2 * py + pc))
    return out


def _exchange_copy(src_ref, land_ref, send_sems, recv_sems, k, peer, peer_block, my_block, by_block, outgoing):
    src = src_ref.at[peer_block] if by_block else src_ref
    dst = land_ref.at[my_block if outgoing else peer_block]
    return pltpu.make_async_remote_copy(src_ref=src, dst_ref=dst, send_sem=send_sems.at[k], recv_sem=recv_sems.at[k],
                                        device_id=peer, device_id_type=MESH_ID)


def _exchange_start(srcs, by_block, name, after):
    n = len(srcs)
    x_pos, y_pos, c_pos = _mesh_pos()
    me_at = 4 * x_pos + 2 * y_pos + c_pos
    lands = []
    for a in srcs:
        own = lax.dynamic_index_in_dim(a, me_at, 0, keepdims=False) if by_block else a
        lands.append(lax.dynamic_update_slice(lax.empty((N_DEV,) + own.shape, own.dtype), own[None],
                                              (me_at,) + (0,) * own.ndim))

    def body(*refs):
        src_refs, land_refs = refs[1:1 + n], refs[1 + n:1 + 2 * n]
        outs = refs[1 + 2 * n:]
        send, recv, token = outs[:n], outs[n:2 * n], outs[4 * n]
        x, y, c = _mesh_pos()
        me = 4 * x + 2 * y + c
        for a in range(n):
            for k, (peer, blk) in enumerate(_peers(x, y, c)):
                _exchange_copy(src_refs[a], land_refs[a], send[a], recv[a], k, peer, blk, me, by_block, True).start()
        token[...] = jnp.zeros_like(token)

    sems = [pltpu.SemaphoreType.DMA((N_DEV - 1,))] * (2 * n)
    thru = [pltpu.HBM(a.shape, a.dtype) for a in list(srcs) + list(lands)]
    res = pl.pallas_call(
        body, name=name,
        in_specs=[ANY_SPEC] + [HBM_SPEC] * (2 * n),
        out_specs=[SEM_SPEC] * (2 * n) + [HBM_SPEC] * (2 * n) + [pl.BlockSpec(memory_space=pltpu.VMEM)],
        out_shape=sems + thru + [jax.ShapeDtypeStruct((8, 128), f32)],
        input_output_aliases={1 + i: 2 * n + i for i in range(2 * n)},
        compiler_params=pltpu.CompilerParams(has_side_effects=pltpu.SideEffectType.DATAFLOW_SIDE_EFFECTING),
    )(after, *[pltpu.with_memory_space_constraint(a, pltpu.HBM) for a in list(srcs) + list(lands)])
    states = [(res[2 * n + a], res[3 * n + a], res[a], res[n + a]) for a in range(n)]
    return states, res[4 * n]


def _exchange_wait(states, by_block, name, after):
    n, na = len(states), len(after)

    def body(*refs):
        src_refs, land_refs = refs[:n], refs[n:2 * n]
        send, recv = refs[2 * n:3 * n], refs[3 * n:4 * n]
        x, y, c = _mesh_pos()
        me = 4 * x + 2 * y + c
        for a in range(n):
            for k, (peer, blk) in enumerate(_peers(x, y, c)):
                _exchange_copy(src_refs[a], land_refs[a], send[a], recv[a], k, peer, blk, me, by_block, True).wait_send()
                _exchange_copy(src_refs[a], land_refs[a], send[a], recv[a], k, peer, blk, me, by_block, False).wait_recv()

    srcs, lands = [s[0] for s in states], [s[1] for s in states]
    res = pl.pallas_call(
        body, name=name,
        in_specs=[HBM_SPEC] * (2 * n) + [SEM_SPEC] * (2 * n) + [ANY_SPEC] * na,
        out_specs=[HBM_SPEC] * (2 * n),
        out_shape=[pltpu.HBM(a.shape, a.dtype) for a in srcs + lands],
        input_output_aliases={i: i for i in range(2 * n)},
        compiler_params=pltpu.CompilerParams(has_side_effects=pltpu.SideEffectType.DATAFLOW_SIDE_EFFECTING),
    )(*srcs, *lands, *[s[2] for s in states], *[s[3] for s in states], *after)
    return list(res[n:2 * n])


def _adamw_math(w, g, m, v):
    m = ADAM_B1 * m + (1.0 - ADAM_B1) * g
    v = ADAM_B2 * v + (1.0 - ADAM_B2) * (g * g)
    m_hat = m / (1.0 - ADAM_B1 ** ADAM_STEP)
    v_hat = v / (1.0 - ADAM_B2 ** ADAM_STEP)
    delta = -ADAM_LR * (m_hat / (jnp.sqrt(v_hat) + ADAM_EPS) + ADAM_WD * w)
    return delta, m, v


def _adamw_big(parts, w, m, v, name):
    rows, cols = w.shape
    tr, tc = (256, cols) if rows % 256 == 0 else (rows, 256)
    nparts = parts.shape[0]

    def body(p_ref, w_ref, m_ref, v_ref, g_ref, d_ref, mo_ref, vo_ref):
        g = p_ref[0].astype(f32)
        for j in range(1, nparts):
            g = g + p_ref[j].astype(f32)
        d, mn, vn = _adamw_math(w_ref[...], g, m_ref[...], v_ref[...])
        g_ref[...] = g
        d_ref[...] = d
        mo_ref[...] = mn
        vo_ref[...] = vn

    tile = pl.BlockSpec((tr, tc), lambda i, j: (i, j))
    shp = jax.ShapeDtypeStruct((rows, cols), f32)
    return pl.pallas_call(
        body, name=name, grid=(rows // tr, cols // tc),
        in_specs=[pl.BlockSpec((nparts, tr, tc), lambda i, j: (0, i, j)), tile, tile, tile],
        out_specs=[tile, tile, tile, tile], out_shape=(shp, shp, shp, shp),
        compiler_params=_params("arbitrary", "arbitrary"),
    )(parts, w, m, v)


PACK_ROWS = 56
_PACK_AT = {
    "mix_norm_g": (0, 0, 1024), "ssd_norm_g": (1, 0, 1024), "conf_ln_g": (2, 0, 1024), "conf_ln_b": (3, 0, 1024),
    "conf_dw_b": (4, 0, 1024), "mlp_norm_g": (5, 0, 1024), "final_norm_g": (7, 0, 1024), "ple_norm_g": (8, 0, 1024),
    "b_ple_gate": (9, 0, 1024), "ple_gate_norm_g": (10, 0, 1024), "dt_bias": (13, 0, 16), "A_log": (13, 128, 16),
    "D_skip": (13, 256, 16),
}
PACK_LOSS_ROW = 6
PACK_CONV_B_ROW = 11
PACK_CONV_W_ROW = 14
PACK_CONF_W_ROW = 24


def _pack_small(acc):
    def body(inp, opb, mlp, tail, cb, dtb, alog, dskip, cw, fw, o_ref):
        o_ref[...] = jnp.zeros_like(o_ref)
        rows = {"mix_norm_g": inp[0:1, :], "mlp_norm_g": mlp[0:1, :],
                "ssd_norm_g": opb[OPB_SSD_G:OPB_SSD_G + 1, :], "conf_ln_g": opb[OPB_LN_G:OPB_LN_G + 1, :],
                "conf_ln_b": opb[OPB_LN_B:OPB_LN_B + 1, :], "conf_dw_b": opb[OPB_CONV_B:OPB_CONV_B + 1, :],
                "final_norm_g": tail[TAIL_FINAL_G:TAIL_FINAL_G + 1, :], "ple_norm_g": tail[TAIL_PLE_G:TAIL_PLE_G + 1, :],
                "b_ple_gate": tail[TAIL_GATE_B:TAIL_GATE_B + 1, :],
                "ple_gate_norm_g": tail[TAIL_GATE_NORM_G:TAIL_GATE_NORM_G + 1, :],
                "dt_bias": dtb[...], "A_log": alog[...], "D_skip": dskip[...]}
        for name, val in rows.items():
            r, lo, width = _PACK_AT[name]
            o_ref[r:r + 1, lo:lo + width] = val
        o_ref[PACK_LOSS_ROW:PACK_LOSS_ROW + 1, :] = tail[TAIL_LOSS:TAIL_LOSS + 1, :]
        o_ref[PACK_CONV_B_ROW:PACK_CONV_B_ROW + 1, :] = cb[:, 0:1024]
        o_ref[PACK_CONV_B_ROW + 1:PACK_CONV_B_ROW + 2, 0:512] = cb[:, 1024:XBC_WIDTH]
        for k in range(SSD_CONV):
            o_ref[PACK_CONV_W_ROW + k:PACK_CONV_W_ROW + k + 1, :] = cw[k:k + 1, 0:1024]
            o_ref[PACK_CONV_W_ROW + SSD_CONV + k:PACK_CONV_W_ROW + SSD_CONV + k + 1, 0:512] = cw[k:k + 1, 1024:XBC_WIDTH]
        o_ref[PACK_CONF_W_ROW:PACK_CONF_W_ROW + 32, :] = fw[...]

    return pl.pallas_call(body, name="pack_small", out_shape=jax.ShapeDtypeStruct((PACK_ROWS, 1024), f32))(
        acc["in_proj"], acc["out_proj"], acc["mlp"], acc["tail"], acc["ssd_conv_b"], acc["dt_bias"], acc["A_log"],
        acc["D_skip"], acc["ssd_conv_w"], acc["conf_dw_w"])


def _small_update(all_small, w, m, v):
    names = _REPLICATED

    def body(all_ref, *refs):
        ins, outs = refs[:3 * len(names)], refs[3 * len(names):]
        s = all_ref[0]
        for j in range(1, N_DEV):
            s = s + all_ref[j]
        outs[0][...] = s[PACK_LOSS_ROW:PACK_LOSS_ROW + 1, 0:1]
        outs[1][...] = jnp.concatenate([s[PACK_CONV_W_ROW:PACK_CONV_W_ROW + SSD_CONV, :],
                                        s[PACK_CONV_W_ROW + SSD_CONV:PACK_CONV_W_ROW + 2 * SSD_CONV, 0:512]], axis=1)
        outs[2][...] = s[PACK_CONF_W_ROW:PACK_CONF_W_ROW + CONF_KERNEL, :]
        for i, name in enumerate(names):
            if name == "ssd_conv_b":
                g = jnp.concatenate([s[PACK_CONV_B_ROW:PACK_CONV_B_ROW + 1, :],
                                     s[PACK_CONV_B_ROW + 1:PACK_CONV_B_ROW + 2, 0:512]], axis=1)
            else:
                r, lo, width = _PACK_AT[name]
                g = s[r:r + 1, lo:lo + width]
            d, mn, vn = _adamw_math(ins[3 * i][...], g, ins[3 * i + 1][...], ins[3 * i + 2][...])
            for o_ref, val in zip(outs[3 + 4 * i:7 + 4 * i], (g, d, mn, vn)):
                o_ref[...] = val

    shapes = [jax.ShapeDtypeStruct((1, 1), f32), jax.ShapeDtypeStruct((SSD_CONV, XBC_WIDTH), f32),
              jax.ShapeDtypeStruct((CONF_KERNEL, D_MODEL), f32)]
    operands = []
    for name in names:
        operands += [w[name], m[name], v[name]]
        shapes += [jax.ShapeDtypeStruct(w[name].shape, f32)] * 4
    res = pl.pallas_call(body, name="small_update", out_shape=shapes)(all_small, *operands)
    per_name = {name: tuple(res[3 + 4 * i:7 + 4 * i]) for i, name in enumerate(names)}
    return res[0], per_name, res[1], res[2]


def _adamw_filters(g, w, m, v):
    n = len(g)

    def body(*refs):
        ins, outs = refs[:4 * n], refs[4 * n:]
        for i in range(n):
            d, mn, vn = _adamw_math(ins[4 * i + 1][...], ins[4 * i][...], ins[4 * i + 2][...], ins[4 * i + 3][...])
            for o_ref, val in zip(outs[3 * i:3 * i + 3], (d, mn, vn)):
                o_ref[...] = val

    operands, shapes = [], []
    for i in range(n):
        operands += [g[i], w[i], m[i], v[i]]
        shapes += [jax.ShapeDtypeStruct(w[i].shape, f32)] * 3
    res = pl.pallas_call(body, name="adamw_filters", out_shape=shapes)(*operands)
    return [tuple(res[3 * i:3 * i + 3]) for i in range(n)]


_REPLICATED = ("mix_norm_g", "ssd_conv_b", "dt_bias", "A_log", "D_skip", "ssd_norm_g", "conf_dw_b", "conf_ln_g",
               "conf_ln_b", "mlp_norm_g", "ple_gate_norm_g", "b_ple_gate", "ple_norm_g", "final_norm_g")
_CONV_WEIGHTS = ("ssd_conv_w", "conf_dw_w")
_BIG = ("w_in", "w_out", "w_up", "w_down", "w_ple_gate", "w_ple")
_WEIGHT_ORDER = ("mix_norm_g", "w_in", "ssd_conv_w", "ssd_conv_b", "dt_bias", "A_log", "D_skip", "ssd_norm_g", "conf_dw_w",
                 "conf_dw_b", "conf_ln_g", "conf_ln_b", "w_out", "mlp_norm_g", "w_up", "w_down", "ple_gate_norm_g",
                 "w_ple_gate", "b_ple_gate", "w_ple", "ple_norm_g", "final_norm_g")


def _blocks_of_columns(a):
    r, c8 = a.shape
    return jnp.transpose(a.reshape(r, N_DEV, c8 // N_DEV), (1, 0, 2))


def _columns_of_blocks(a):
    _, r, c = a.shape
    return jnp.transpose(a, (1, 0, 2)).reshape(r, N_DEV * c)


_LATER = ("w_out", "w_up", "w_down", "w_ple_gate", "w_ple")
_WHOLE = {
    "w_out": lambda a: a.reshape(2048, D_MODEL),
    "w_up": lambda a: a,
    "w_down": lambda a: a.reshape(D_FF, D_MODEL),
    "w_ple_gate": lambda a: a.reshape(D_MODEL, D_MODEL),
    "w_ple": _columns_of_blocks,
}
_BY_BLOCK = {
    "w_in": lambda g: jnp.concatenate(g, axis=0).reshape(N_DEV, IN_WIDTH // N_DEV, D_MODEL),
    "w_out": lambda g: jnp.concatenate(g, axis=0).reshape(N_DEV, 256, D_MODEL),
    "w_up": lambda g: g,
    "w_down": lambda g: g.reshape(N_DEV, 512, D_MODEL),
    "w_ple_gate": lambda g: g.astype(bf16).reshape(N_DEV, 128, D_MODEL),
    "w_ple": lambda g: _blocks_of_columns(g.astype(bf16)),
}


class _StepComm:
    def __init__(self, me, ready, gathers, token):
        self.me, self.ready, self.gathers, self.token = me, ready, gathers, token
        self.sent = []

    def start_token(self):
        return self.token

    def weight(self, name, after=()):
        if name not in self.ready:
            (land,) = _exchange_wait([self.gathers[name]], False, "gather_wait_" + name, list(after))
            self.ready[name] = _WHOLE[name](land)
        return self.ready[name]

    def send_grads(self, grads):
        names = list(grads)
        blocks = [_BY_BLOCK[n](grads[n]) for n in names]
        states, self.token = _exchange_start(blocks, True, "scatter_start_" + names[0], self.token)
        self.sent.append((names, states))
        return self.token


def kernel(x, p, mix_norm_g, w_in, ssd_conv_w, ssd_conv_b, dt_bias, A_log, D_skip, ssd_norm_g, conf_dw_w, conf_dw_b, conf_ln_g, conf_ln_b, w_out, mlp_norm_g, w_up, w_down, ple_gate_norm_g, w_ple_gate, b_ple_gate, w_ple, ple_norm_g, final_norm_g, loss_target, m_mix_norm_g, m_w_in, m_ssd_conv_w, m_ssd_conv_b, m_dt_bias, m_A_log, m_D_skip, m_ssd_norm_g, m_conf_dw_w, m_conf_dw_b, m_conf_ln_g, m_conf_ln_b, m_w_out, m_mlp_norm_g, m_w_up, m_w_down, m_ple_gate_norm_g, m_w_ple_gate, m_b_ple_gate, m_w_ple, m_ple_norm_g, m_final_norm_g, v_mix_norm_g, v_w_in, v_ssd_conv_w, v_ssd_conv_b, v_dt_bias, v_A_log, v_D_skip, v_ssd_norm_g, v_conf_dw_w, v_conf_dw_b, v_conf_ln_g, v_conf_ln_b, v_w_out, v_mlp_norm_g, v_w_up, v_w_down, v_ple_gate_norm_g, v_w_ple_gate, v_b_ple_gate, v_w_ple, v_ple_norm_g, v_final_norm_g):
    wts = dict(mix_norm_g=mix_norm_g, w_in=w_in, ssd_conv_w=ssd_conv_w, ssd_conv_b=ssd_conv_b, dt_bias=dt_bias, A_log=A_log,
               D_skip=D_skip, ssd_norm_g=ssd_norm_g, conf_dw_w=conf_dw_w, conf_dw_b=conf_dw_b, conf_ln_g=conf_ln_g,
               conf_ln_b=conf_ln_b, w_out=w_out, mlp_norm_g=mlp_norm_g, w_up=w_up, w_down=w_down,
               ple_gate_norm_g=ple_gate_norm_g, w_ple_gate=w_ple_gate, b_ple_gate=b_ple_gate, w_ple=w_ple,
               ple_norm_g=ple_norm_g, final_norm_g=final_norm_g)
    mom1 = dict(mix_norm_g=m_mix_norm_g, w_in=m_w_in, ssd_conv_w=m_ssd_conv_w, ssd_conv_b=m_ssd_conv_b, dt_bias=m_dt_bias,
                A_log=m_A_log, D_skip=m_D_skip, ssd_norm_g=m_ssd_norm_g, conf_dw_w=m_conf_dw_w, conf_dw_b=m_conf_dw_b,
                conf_ln_g=m_conf_ln_g, conf_ln_b=m_conf_ln_b, w_out=m_w_out, mlp_norm_g=m_mlp_norm_g, w_up=m_w_up,
                w_down=m_w_down, ple_gate_norm_g=m_ple_gate_norm_g, w_ple_gate=m_w_ple_gate, b_ple_gate=m_b_ple_gate,
                w_ple=m_w_ple, ple_norm_g=m_ple_norm_g, final_norm_g=m_final_norm_g)
    mom2 = dict(mix_norm_g=v_mix_norm_g, w_in=v_w_in, ssd_conv_w=v_ssd_conv_w, ssd_conv_b=v_ssd_conv_b, dt_bias=v_dt_bias,
                A_log=v_A_log, D_skip=v_D_skip, ssd_norm_g=v_ssd_norm_g, conf_dw_w=v_conf_dw_w, conf_dw_b=v_conf_dw_b,
                conf_ln_g=v_conf_ln_g, conf_ln_b=v_conf_ln_b, w_out=v_w_out, mlp_norm_g=v_mlp_norm_g, w_up=v_w_up,
                w_down=v_w_down, ple_gate_norm_g=v_ple_gate_norm_g, w_ple_gate=v_w_ple_gate, b_ple_gate=v_b_ple_gate,
                w_ple=v_w_ple, ple_norm_g=v_ple_norm_g, final_norm_g=v_final_norm_g)
    x_pos, y_pos, c_pos = _mesh_pos()
    me = 4 * x_pos + 2 * y_pos + c_pos

    first = _all_gather([wts["w_in"][0].T.astype(bf16), wts["ssd_conv_w"][0], wts["conf_dw_w"][0]], "gather_first")
    ready = {
        "w_in": first[0].reshape(IN_WIDTH, D_MODEL),
        "ssd_conv_w": jnp.pad(_columns_of_blocks(first[1]), ((0, 8 - SSD_CONV), (0, 0))),
        "conf_dw_w": jnp.pad(_columns_of_blocks(first[2]), ((0, 32 - CONF_KERNEL), (0, 0))),
    }
    shards = [wts[n][0].astype(bf16) for n in _LATER]
    states, token = _exchange_start(shards, False, "gather_start", first[1])
    comm = _StepComm(me, ready, dict(zip(_LATER, states)), token)
    small = {n: wts[n].reshape(1, -1) for n in _REPLICATED}

    grad_x, acc = _local_step(x[0], p[0, 0], loss_target[0], small, comm)

    packed = _pack_small(acc)
    (small_state,), _ = _exchange_start([packed], False, "small_start", comm.token)
    grads, delta, new_m, new_v = {}, {}, {}, {}

    def adamw_big(names, sent, after):
        lands = _exchange_wait(sent, True, "scatter_wait_" + names[0], after)
        for n, land in zip(names, lands):
            view = (lambda a: a[0].T) if n == "w_in" else (lambda a: a[0])
            back = (lambda a: a.T[None]) if n == "w_in" else (lambda a: a[None])
            out = _adamw_big(land, view(wts[n]), view(mom1[n]), view(mom2[n]), "adamw_" + n)
            grads[n], delta[n], new_m[n], new_v[n] = [back(a) for a in out]

    for names, sent in comm.sent[:-1]:
        adamw_big(names, sent, [grad_x])
    adamw_big(*comm.sent[-1], [delta[n] for n in _LATER])
    (all_small,) = _exchange_wait([small_state], False, "small_wait", [delta["w_in"]])
    as_row = lambda d: {n: d[n].reshape(1, -1) for n in _REPLICATED}
    loss, per_name, conv_w_sum, conf_w_sum = _small_update(all_small, as_row(wts), as_row(mom1), as_row(mom2))
    for n in _REPLICATED:
        grads[n], delta[n], new_m[n], new_v[n] = [a.reshape(wts[n].shape) for a in per_name[n]]
    filt_g = [lax.dynamic_slice_in_dim(conv_w_sum, me * 192, 192, axis=1),
              lax.dynamic_slice_in_dim(conf_w_sum, me * 128, 128, axis=1)]
    filt = _adamw_filters(filt_g, [wts[n][0] for n in _CONV_WEIGHTS], [mom1[n][0] for n in _CONV_WEIGHTS],
                          [mom2[n][0] for n in _CONV_WEIGHTS])
    for n, g, (d, mn, vn) in zip(_CONV_WEIGHTS, filt_g, filt):
        grads[n], delta[n], new_m[n], new_v[n] = g[None], d[None], mn[None], vn[None]

    return (loss.reshape(()), grad_x[None], *[grads[n] for n in _WEIGHT_ORDER], *[delta[n] for n in _WEIGHT_ORDER],
            *[new_m[n] for n in _WEIGHT_ORDER], *[new_v[n] for n in _WEIGHT_ORDER])
```

```python
import functools

import jax
import jax.numpy as jnp
from jax import lax
from jax.experimental import pallas as pl
from jax.experimental.pallas import tpu as pltpu

f32 = jnp.float32
bf16 = jnp.bfloat16

EPS = 1e-6
D_MODEL = 1024
SSD_HEADS = 16
SSD_HEAD_DIM = 64
SSD_GROUPS = 2
SSD_STATE = 128
SSD_CONV = 4
CHUNK = 128
XBC_WIDTH = 1536
CONF_KERNEL = 31
D_FF = 4096
PLE_DIM = 256
IN_WIDTH = 4624
N_DEV = 8
SEG_Z = (0, 1024)
SEG_XBC = (1024, 2560)
SEG_DT = (2560, 2576)
SEG_CV = (2576, 3600)
SEG_CG = (3600, 4624)
IN_SEGS = (SEG_Z, SEG_XBC, SEG_DT, SEG_CV, SEG_CG)

ADAM_LR = 0.001
ADAM_B1 = 0.9
ADAM_B2 = 0.999
ADAM_EPS = 1e-08
ADAM_WD = 0.01
ADAM_STEP = 10

VMEM_LIMIT_BYTES = 56 * 1024 * 1024
TOKEN_TILE = 512
SMALL_ROWS = 16

HBM_SPEC = pl.BlockSpec(memory_space=pltpu.HBM)
MESH_ID = pl.DeviceIdType.MESH


def _params(*semantics):
    return pltpu.CompilerParams(dimension_semantics=semantics, vmem_limit_bytes=VMEM_LIMIT_BYTES)


def _nn(a, b):
    return lax.dot_general(a, b, (((1,), (0,)), ((), ())), preferred_element_type=f32)


def _nt(a, b):
    return lax.dot_general(a, b, (((1,), (1,)), ((), ())), preferred_element_type=f32)


def _tn(a, b):
    return lax.dot_general(a, b, (((0,), (0,)), ((), ())), preferred_element_type=f32)


@jax.custom_vjp
def bnn(a, b):
    return _nn(a.astype(bf16), b.astype(bf16))


def _bnn_fwd(a, b):
    ab, bb = a.astype(bf16), b.astype(bf16)
    return _nn(ab, bb), (ab, bb)


def _bnn_bwd(res, g):
    ab, bb = res
    gb = g.astype(bf16)
    return _nt(gb, bb), _tn(ab, gb)


bnn.defvjp(_bnn_fwd, _bnn_bwd)


@jax.custom_vjp
def bnt(a, b):
    return _nt(a.astype(bf16), b.astype(bf16))


def _bnt_fwd(a, b):
    ab, bb = a.astype(bf16), b.astype(bf16)
    return _nt(ab, bb), (ab, bb)


def _bnt_bwd(res, g):
    ab, bb = res
    gb = g.astype(bf16)
    return _nn(gb, bb), _tn(gb, ab)


bnt.defvjp(_bnt_fwd, _bnt_bwd)


@jax.custom_vjp
def btn(a, b):
    return _tn(a.astype(bf16), b.astype(bf16))


def _btn_fwd(a, b):
    ab, bb = a.astype(bf16), b.astype(bf16)
    return _tn(ab, bb), (ab, bb)


def _btn_bwd(res, g):
    ab, bb = res
    gb = g.astype(bf16)
    return _nt(bb, gb), _nn(ab, gb)


btn.defvjp(_btn_fwd, _btn_bwd)

SCAN_CHUNKS = 4
CUMSUM_PASSES = 3
EXPAND_PASSES = 2


def _bf16_pieces(a, passes):
    pieces, rest = [], a
    for _ in range(passes):
        piece = rest.astype(bf16)
        pieces.append(piece)
        rest = rest - piece.astype(f32)
    return pieces


def _make_dot01(form, passes):
    fwd, bwd = {"right": (lambda b, p: _nn(p, b), lambda b, g: _nt(g, b)),
                "left": (lambda b, p: _nn(b, p), lambda b, g: _tn(b, g)),
                "tn": (lambda b, p: _tn(p, b), lambda b, g: _nt(b, g))}[form]

    def run(op, b, v):
        out = None
        for piece in _bf16_pieces(v, passes):
            term = op(b, piece)
            out = term if out is None else out + term
        return out

    @jax.custom_vjp
    def product(b, v):
        return run(fwd, b, v)

    product.defvjp(lambda b, v: (run(fwd, b, v), b), lambda b, g: (jnp.zeros_like(b), run(bwd, b, g)))
    return product


_DOT01 = {(form, passes): _make_dot01(form, passes)
          for form, passes in (("left", CUMSUM_PASSES), ("tn", CUMSUM_PASSES), ("right", EXPAND_PASSES))}


def _dot01(b01, a, form, passes):
    return _DOT01[form, passes](b01.astype(bf16), a)


def _rms(x, g):
    return x * lax.rsqrt(jnp.mean(x * x, axis=-1, keepdims=True) + EPS) * g


def _gated_norm(y, z, g):
    v = y * jax.nn.silu(z)
    half = v.shape[-1] // SSD_GROUPS
    parts = []
    for k in range(SSD_GROUPS):
        vk = v[:, k * half:(k + 1) * half]
        parts.append(vk * lax.rsqrt(jnp.mean(vk * vk, axis=-1, keepdims=True) + EPS))
    return jnp.concatenate(parts, axis=-1) * g


def _ln_silu(v, g, b):
    mu = jnp.mean(v, axis=-1, keepdims=True)
    xc = v - mu
    y = xc * lax.rsqrt(jnp.mean(xc * xc, axis=-1, keepdims=True) + EPS) * g + b
    return jax.nn.silu(y)


def _acc_init(i, *refs):
    @pl.when(i == 0)
    def _():
        for r in refs:
            r[...] = jnp.zeros_like(r)


def _after_token(body, token):
    if token is None:
        return body, [], []

    def body_after(tok_ref, *refs):
        del tok_ref
        body(*refs)

    return body_after, [token], [pl.BlockSpec(memory_space=pl.ANY)]


def _row_spec(tm, n):
    return pl.BlockSpec((tm, n), lambda i: (i, 0))


def _const_spec(shape):
    nd = len(shape)
    return pl.BlockSpec(shape, lambda i: (0,) * nd)


def _prev_halo_spec(tm, halo, n):
    return pl.BlockSpec((halo, n), lambda i: (jnp.maximum(i * (tm // halo) - 1, 0), 0))


def _next_halo_spec(tm, halo, n, t):
    return pl.BlockSpec((halo, n), lambda i: (jnp.minimum((i + 1) * (tm // halo), t // halo - 1), 0))


def _in_proj(x, g1, w_in_p, token=None):
    t = x.shape[0]
    tm = min(TOKEN_TILE, t)

    def body(x_ref, g_ref, w_ref, u_ref, z_ref, xbc_ref, dt_ref, cv_ref, cg_ref):
        u = _rms(x_ref[...], g_ref[...]).astype(bf16)
        u_ref[...] = u
        for (lo, hi), o_ref in zip(IN_SEGS, (z_ref, xbc_ref, dt_ref, cv_ref, cg_ref)):
            o_ref[...] = _nt(u, w_ref[lo:hi, :])

    widths = [hi - lo for lo, hi in IN_SEGS]
    outs = [jax.ShapeDtypeStruct((t, D_MODEL), bf16)] + [jax.ShapeDtypeStruct((t, n), f32) for n in widths]
    body, tok, tok_spec = _after_token(body, token)
    return pl.pallas_call(
        body, name="in_proj", grid=(t // tm,),
        in_specs=tok_spec + [_row_spec(tm, D_MODEL), _const_spec((1, D_MODEL)), _const_spec((IN_WIDTH, D_MODEL))],
        out_specs=[_row_spec(tm, D_MODEL)] + [_row_spec(tm, n) for n in widths],
        out_shape=outs, compiler_params=_params("arbitrary"),
    )(*tok, x, g1, w_in_p)


ROW_BLOCK = 16
LANE_CHUNK = 512


def _blocks(rows, cols):
    return [(slice(r, r + ROW_BLOCK), slice(c, c + LANE_CHUNK))
            for c in range(0, cols, LANE_CHUNK) for r in range(0, rows, ROW_BLOCK)]


def _conv4_block(ext_ref, w_ref, b_ref, rs, ls):
    acc = b_ref[:, ls] + w_ref[0:1, ls] * ext_ref[rs.start + 8 - 3:rs.stop + 8 - 3, ls]
    for k in range(1, SSD_CONV):
        acc = acc + w_ref[k:k + 1, ls] * ext_ref[rs.start + 8 - 3 + k:rs.stop + 8 - 3 + k, ls]
    return acc


def _ssd_pre(xbc_raw, dt_raw, cw, cb, dt_bias):
    t = xbc_raw.shape[0]
    tm = min(TOKEN_TILE, t)

    def body(cur_ref, halo_ref, dt_ref, w_ref, b_ref, dtb_ref, act_ref, dto_ref, ext_ref):
        i = pl.program_id(0)
        ext_ref[0:8, :] = jnp.where(i == 0, 0.0, halo_ref[...])
        ext_ref[8:, :] = cur_ref[...]
        for rs, ls in _blocks(tm, XBC_WIDTH):
            act_ref[rs, ls] = jax.nn.silu(_conv4_block(ext_ref, w_ref, b_ref, rs, ls))
        dto_ref[...] = jax.nn.softplus(dt_ref[...] + dtb_ref[...])

    return pl.pallas_call(
        body, name="ssd_pre", grid=(t // tm,),
        in_specs=[_row_spec(tm, XBC_WIDTH), _prev_halo_spec(tm, 8, XBC_WIDTH), _row_spec(tm, SSD_HEADS),
                  _const_spec((8, XBC_WIDTH)), _const_spec((1, XBC_WIDTH)), _const_spec((1, SSD_HEADS))],
        out_specs=[_row_spec(tm, XBC_WIDTH), _row_spec(tm, SSD_HEADS)],
        out_shape=(jax.ShapeDtypeStruct((t, XBC_WIDTH), f32), jax.ShapeDtypeStruct((t, SSD_HEADS), f32)),
        scratch_shapes=[pltpu.VMEM((tm + 8, XBC_WIDTH), f32)],
        compiler_params=_params("arbitrary"),
    )(xbc_raw, xbc_raw, dt_raw, cw, cb, dt_bias)


def _ssd_consts():
    r = lax.broadcasted_iota(jnp.int32, (CHUNK, CHUNK), 0)
    c = lax.broadcasted_iota(jnp.int32, (CHUNK, CHUNK), 1)
    causal = r >= c
    tril = causal.astype(f32)
    triu = (r <= c).astype(f32)
    hh = lax.broadcasted_iota(jnp.int32, (SSD_HEADS, D_MODEL), 0)
    jj = lax.broadcasted_iota(jnp.int32, (SSD_HEADS, D_MODEL), 1)
    expand = (lax.shift_right_logical(jj, 6) == hh).astype(f32)
    lane = lax.broadcasted_iota(jnp.int32, (1, 2 * SSD_HEAD_DIM), 1)
    m0 = (lane < SSD_HEAD_DIM).astype(f32)
    return causal, tril, triu, expand, m0, 1.0 - m0


def _ssd_chunk(xs, bm, cm, dt, s_prev, a_log, d_skip, consts):
    causal, tril, triu, expand, m0, m1 = consts
    a = dt * (-jnp.exp(a_log))
    cs = _dot01(tril, a, "left", CUMSUM_PASSES)
    cs_t = _dot01(triu, a, "tn", CUMSUM_PASSES)
    cs_last = cs[CHUNK - 1:CHUNK, :]
    per_head = jnp.concatenate([dt, jnp.exp(cs_last - cs), jnp.exp(cs), jnp.broadcast_to(jnp.exp(cs_last), (8, SSD_HEADS)),
                                jnp.broadcast_to(d_skip, (8, SSD_HEADS))], axis=0)
    per_channel = _dot01(expand, per_head, "right", EXPAND_PASSES)
    dt_e, dec_end, dec_start = (per_channel[i * CHUNK:(i + 1) * CHUNK] for i in range(3))
    chunk_dec = per_channel[3 * CHUNK:3 * CHUNK + 1]
    d_e = per_channel[3 * CHUNK + 8:3 * CHUNK + 9]
    xc = xs * dt_e
    x_dec = xc * dec_end
    gw = D_MODEL // SSD_GROUPS
    ys, states = [], []
    for g in range(SSD_GROUPS):
        bg = bm[:, g * SSD_STATE:(g + 1) * SSD_STATE]
        cg = cm[:, g * SSD_STATE:(g + 1) * SSD_STATE]
        sp = s_prev[:, g * gw:(g + 1) * gw]
        states.append(sp * chunk_dec[:, g * gw:(g + 1) * gw] + btn(bg, x_dec[:, g * gw:(g + 1) * gw]))
        y_off = bnn(cg, sp) * dec_start[:, g * gw:(g + 1) * gw]
        scores = bnt(cg, bg)
        pieces = []
        for pr in range(gw // (2 * SSD_HEAD_DIM)):
            lo = g * gw + pr * 2 * SSD_HEAD_DIM
            xp = xc[:, lo:lo + 2 * SSD_HEAD_DIM]
            mats = []
            for h in (lo // SSD_HEAD_DIM, lo // SSD_HEAD_DIM + 1):
                seg = cs[:, h:h + 1] - cs_t[h:h + 1, :]
                mats.append(scores * jnp.exp(jnp.where(causal, seg, -jnp.inf)))
            pieces.append(bnn(jnp.concatenate(mats, axis=1), jnp.concatenate([xp * m0, xp * m1], axis=0)))
        ys.append(jnp.concatenate(pieces, axis=-1) + y_off)
    y = jnp.concatenate(ys, axis=-1) + xs * d_e
    return y, jnp.concatenate(states, axis=-1)


def _ssd_scan(xbc_act, dt, a_log, d_skip):
    t = xbc_act.shape[0]
    nc = t // CHUNK
    rows = SCAN_CHUNKS * CHUNK

    def body(xbc_ref, dt_ref, al_ref, dk_ref, y_ref, sp_ref, state_ref):
        c = pl.program_id(0)

        @pl.when(c == 0)
        def _():
            state_ref[...] = jnp.zeros_like(state_ref)

        state = state_ref[...]
        consts = _ssd_consts()
        for j in range(SCAN_CHUNKS):
            rs = slice(j * CHUNK, (j + 1) * CHUNK)
            sp_ref[j] = state
            y_ref[rs, :], state = _ssd_chunk(xbc_ref[rs, 0:1024], xbc_ref[rs, 1024:1280], xbc_ref[rs, 1280:1536],
                                             dt_ref[rs, :], state, al_ref[...], dk_ref[...], consts)
        state_ref[...] = state

    return pl.pallas_call(
        body, name="ssd_scan", grid=(nc // SCAN_CHUNKS,),
        in_specs=[_row_spec(rows, XBC_WIDTH), _row_spec(rows, SSD_HEADS), _const_spec((1, SSD_HEADS)),
                  _const_spec((1, SSD_HEADS))],
        out_specs=[_row_spec(rows, D_MODEL), pl.BlockSpec((SCAN_CHUNKS, SSD_STATE, D_MODEL), lambda c: (c, 0, 0))],
        out_shape=(jax.ShapeDtypeStruct((t, D_MODEL), f32), jax.ShapeDtypeStruct((nc, SSD_STATE, D_MODEL), f32)),
        scratch_shapes=[pltpu.VMEM((SSD_STATE, D_MODEL), f32)],
        compiler_params=_params("arbitrary"),
    )(xbc_act, dt, a_log, d_skip)


CONV_HALO = 32
CONV_BLOCK = 64
LANE_TILE = 128
SUBLANES = 8


def _fill_glu_ext(ext_ref, cv_ref, cg_ref, hcv_ref, hcg_ref, i):
    ext_ref[0:CONV_HALO, :] = jnp.where(i == 0, 0.0, hcv_ref[...] * jax.nn.sigmoid(hcg_ref[...]))
    ext_ref[CONV_HALO:, :] = cv_ref[...] * jax.nn.sigmoid(cg_ref[...])


def _shifted_copies(src_ref, dst_ref, rows):
    for s in range(1, SUBLANES):
        dst_ref[s, 0:rows, :] = src_ref[pl.ds(s, rows), :]


def _tap_source(src_ref, shifted_ref, offset):
    s = offset % SUBLANES
    return (src_ref if s == 0 else shifted_ref.at[s]), offset - s


def _conv_rows(src_ref, shifted_ref, taps, offsets, lanes, r0, init):
    accs = [init] * (CONV_BLOCK // SUBLANES)
    for k, off in enumerate(offsets):
        ref, base = _tap_source(src_ref, shifted_ref, off)
        for j in range(len(accs)):
            accs[j] = accs[j] + taps[k] * ref[pl.ds(r0 + base + SUBLANES * j, SUBLANES), lanes]
    return accs


def _conf_conv(cv, cg, w, b):
    t = cv.shape[0]
    tm = min(TOKEN_TILE, t)
    ext_rows = tm + CONV_HALO
    offsets = [CONV_HALO - (CONF_KERNEL - 1) + k for k in range(CONF_KERNEL)]

    def body(cv_ref, cg_ref, hcv_ref, hcg_ref, w_ref, b_ref, o_ref, ext_ref, shifted_ref):
        _fill_glu_ext(ext_ref, cv_ref, cg_ref, hcv_ref, hcg_ref, pl.program_id(0))
        _shifted_copies(ext_ref, shifted_ref, ext_rows - SUBLANES)
        for lb in range(1024 // LANE_TILE):
            lanes = slice(lb * LANE_TILE, (lb + 1) * LANE_TILE)
            taps = [jnp.broadcast_to(w_ref[k:k + 1, lanes], (SUBLANES, LANE_TILE)) for k in range(CONF_KERNEL)]
            bias = jnp.broadcast_to(b_ref[:, lanes], (SUBLANES, LANE_TILE))

            def rows(rb, carry, lanes=lanes, taps=taps, bias=bias):
                r0 = pl.multiple_of(rb * CONV_BLOCK, CONV_BLOCK)
                accs = _conv_rows(ext_ref, shifted_ref, taps, offsets, lanes, r0, bias)
                for j, a in enumerate(accs):
                    o_ref[pl.ds(r0 + SUBLANES * j, SUBLANES), lanes] = a
                return carry

            lax.fori_loop(0, tm // CONV_BLOCK, rows, 0)

    return pl.pallas_call(
        body, name="conf_conv", grid=(t // tm,),
        in_specs=[_row_spec(tm, 1024), _row_spec(tm, 1024), _prev_halo_spec(tm, CONV_HALO, 1024),
                  _prev_halo_spec(tm, CONV_HALO, 1024), _const_spec((32, 1024)), _const_spec((1, 1024))],
        out_specs=_row_spec(tm, 1024), out_shape=jax.ShapeDtypeStruct((t, 1024), f32),
        scratch_shapes=[pltpu.VMEM((ext_rows, 1024), f32), pltpu.VMEM((SUBLANES, ext_rows, 1024), f32)],
        compiler_params=_params("arbitrary"),
    )(cv, cg, cv, cg, w, b)


def _out_proj(y, z, v2, x, g_ssd, ln_g, ln_b, w_out, g2):
    t = x.shape[0]
    tm = min(TOKEN_TILE, t)

    def body(y_ref, z_ref, v_ref, x_ref, gs_ref, lg_ref, lb_ref, w_ref, g2_ref, ys_ref, yc_ref, h1_ref, u2_ref):
        ys = _gated_norm(y_ref[...], z_ref[...], gs_ref[...]).astype(bf16)
        yc = _ln_silu(v_ref[...], lg_ref[...], lb_ref[...]).astype(bf16)
        ys_ref[...] = ys
        yc_ref[...] = yc
        h1 = x_ref[...] + _nn(ys, w_ref[0:1024, :]) + _nn(yc, w_ref[1024:2048, :])
        h1_ref[...] = h1
        u2_ref[...] = _rms(h1, g2_ref[...]).astype(bf16)

    vec = _const_spec((1, 1024))
    row = _row_spec(tm, 1024)
    return pl.pallas_call(
        body, name="out_proj", grid=(t // tm,),
        in_specs=[row, row, row, row, vec, vec, vec, _const_spec((2048, 1024)), vec],
        out_specs=[row, row, row, row],
        out_shape=(jax.ShapeDtypeStruct((t, 1024), bf16), jax.ShapeDtypeStruct((t, 1024), bf16),
                   jax.ShapeDtypeStruct((t, 1024), f32), jax.ShapeDtypeStruct((t, 1024), bf16)),
        compiler_params=_params("arbitrary"),
    )(y, z, v2, x, g_ssd, ln_g, ln_b, w_out, g2)


def _mlp_up(u2, w_up):
    t = u2.shape[0]
    tm = min(TOKEN_TILE, t)
    blk = D_FF // N_DEV

    def body(u_ref, w_ref, pre_ref, hs_ref):
        u = u_ref[...]
        for k in range(N_DEV):
            pre = _nn(u, w_ref[k])
            pre_ref[:, k * blk:(k + 1) * blk] = pre.astype(bf16)
            r = jnp.maximum(pre, 0.0)
            hs_ref[:, k * blk:(k + 1) * blk] = (r * r).astype(bf16)

    return pl.pallas_call(
        body, name="mlp_up", grid=(t // tm,),
        in_specs=[_row_spec(tm, D_MODEL), _const_spec((N_DEV, D_MODEL, D_FF // N_DEV))],
        out_specs=[_row_spec(tm, D_FF), _row_spec(tm, D_FF)],
        out_shape=(jax.ShapeDtypeStruct((t, D_FF), bf16), jax.ShapeDtypeStruct((t, D_FF), bf16)),
        compiler_params=_params("arbitrary"),
    )(u2, w_up)


def _mlp_down(h1, hs, w_down):
    t = h1.shape[0]
    tm = min(TOKEN_TILE, t)

    def body(h_ref, hs_ref, w_ref, o_ref):
        o_ref[...] = h_ref[...] + _nn(hs_ref[...], w_ref[...])

    return pl.pallas_call(
        body, name="mlp_down", grid=(t // tm,),
        in_specs=[_row_spec(tm, D_MODEL), _row_spec(tm, D_FF), _const_spec((D_FF, D_MODEL))],
        out_specs=_row_spec(tm, D_MODEL), out_shape=jax.ShapeDtypeStruct((t, D_MODEL), f32),
        compiler_params=_params("arbitrary"),
    )(h1, hs, w_down)


TAIL_LOSS, TAIL_FINAL_G, TAIL_PLE_G, TAIL_GATE_B, TAIL_GATE_NORM_G = 0, 1, 2, 3, 4


def _tail(h2, p, target, g3, w_gate, b_gate, w_ple, ple_g, fin_g):
    t = h2.shape[0]
    tm = min(TOKEN_TILE, t)

    def body(h_ref, p_ref, t_ref, g3_ref, wg_ref, bg_ref, wp_ref, pg_ref, fg_ref,
             dh_ref, dhb_ref, dwg_ref, dwp_ref, acc_ref):
        i = pl.program_id(0)
        _acc_init(i, dwg_ref, dwp_ref, acc_ref)
        h2v = h_ref[...]
        tgt = t_ref[...]
        u3, vjp_u3 = jax.vjp(_rms, h2v, g3_ref[...])
        u3b = u3.astype(bf16)
        pb = p_ref[...].astype(bf16)
        gate_pre = _nn(u3b, wg_ref[...]) + bg_ref[...]
        emb_pre = _nn(pb, wp_ref[...])

        def tail_fn(hv, gp, ep, pg, fg):
            h3 = hv + jax.nn.sigmoid(gp) * _rms(ep, pg)
            err = _rms(h3, fg) - tgt
            return 0.5 * jnp.mean(err * err, axis=-1, keepdims=True)

        loss_tok, vjp_tail = jax.vjp(tail_fn, h2v, gate_pre, emb_pre, pg_ref[...], fg_ref[...])
        dh_a, dgp, dep, dpg, dfg = vjp_tail(jnp.ones_like(loss_tok))
        dgpb = dgp.astype(bf16)
        dh_b, dg3 = vjp_u3(_nt(dgpb, wg_ref[...]))
        dh = dh_a + dh_b
        dh_ref[...] = dh
        dhb_ref[...] = dh.astype(bf16)
        dwg_ref[...] += _tn(u3b, dgpb)
        dwp_ref[...] += _tn(pb, dep.astype(bf16))
        acc_ref[TAIL_LOSS:TAIL_LOSS + 1, :] += jnp.broadcast_to(jnp.sum(loss_tok, axis=0, keepdims=True), (1, 1024))
        acc_ref[TAIL_FINAL_G:TAIL_FINAL_G + 1, :] += dfg
        acc_ref[TAIL_PLE_G:TAIL_PLE_G + 1, :] += dpg
        acc_ref[TAIL_GATE_B:TAIL_GATE_B + 1, :] += jnp.sum(dgp, axis=0, keepdims=True)
        acc_ref[TAIL_GATE_NORM_G:TAIL_GATE_NORM_G + 1, :] += dg3

    vec = _const_spec((1, 1024))
    row = _row_spec(tm, 1024)
    return pl.pallas_call(
        body, name="tail", grid=(t // tm,),
        in_specs=[row, _row_spec(tm, PLE_DIM), row, vec, _const_spec((1024, 1024)), vec, _const_spec((PLE_DIM, 1024)), vec, vec],
        out_specs=[row, row, _const_spec((1024, 1024)), _const_spec((PLE_DIM, 1024)), _const_spec((8, 1024))],
        out_shape=(jax.ShapeDtypeStruct((t, 1024), f32), jax.ShapeDtypeStruct((t, 1024), bf16),
                   jax.ShapeDtypeStruct((1024, 1024), f32), jax.ShapeDtypeStruct((PLE_DIM, 1024), f32),
                   jax.ShapeDtypeStruct((8, 1024), f32)),
        compiler_params=_params("arbitrary"),
    )(h2, p, target, g3, w_gate, b_gate, w_ple, ple_g, fin_g)


def _mlp_bwd(dh2, dh2b, pre, h1, g2, w_down, w_up, token=None):
    t = dh2.shape[0]
    tm = min(TOKEN_TILE, t)
    blk = D_FF // N_DEV

    def body(dh_ref, dhb_ref, pre_ref, h1_ref, g_ref, wd_hbm, wu_hbm, dpre_ref, dh1_ref, dh1b_ref, acc_ref,
             wd_ref, wu_ref):
        i = pl.program_id(0)
        _acc_init(i, acc_ref)

        @pl.when(i == 0)
        def _():
            pltpu.sync_copy(wd_hbm, wd_ref)
            pltpu.sync_copy(wu_hbm, wu_ref)

        dhb = dhb_ref[...]
        du2 = jnp.zeros((tm, D_MODEL), f32)
        for k in range(D_FF // blk):
            dhs = _nt(dhb, wd_ref[k * blk:(k + 1) * blk, :])
            dpre = (dhs * (2.0 * jnp.maximum(pre_ref[:, k * blk:(k + 1) * blk].astype(f32), 0.0))).astype(bf16)
            dpre_ref[:, k * blk:(k + 1) * blk] = dpre
            du2 = du2 + _nt(dpre, wu_ref[k])
        _, vjp_u2 = jax.vjp(_rms, h1_ref[...], g_ref[...])
        d, dg = vjp_u2(du2)
        dh1 = dh_ref[...] + d
        dh1_ref[...] = dh1
        dh1b_ref[...] = dh1.astype(bf16)
        acc_ref[0:1, :] += dg

    row = _row_spec(tm, 1024)
    body, tok, tok_spec = _after_token(body, token)
    return pl.pallas_call(
        body, name="mlp_bwd", grid=(t // tm,),
        in_specs=tok_spec + [row, row, _row_spec(tm, D_FF), row, _const_spec((1, 1024)), HBM_SPEC, HBM_SPEC],
        out_specs=[_row_spec(tm, D_FF), row, row, _const_spec((8, 1024))],
        out_shape=(jax.ShapeDtypeStruct((t, D_FF), bf16), jax.ShapeDtypeStruct((t, 1024), f32),
                   jax.ShapeDtypeStruct((t, 1024), bf16), jax.ShapeDtypeStruct((8, 1024), f32)),
        scratch_shapes=[pltpu.VMEM((D_FF, D_MODEL), bf16), pltpu.VMEM((N_DEV, D_MODEL, D_FF // N_DEV), bf16)],
        compiler_params=_params("arbitrary"),
    )(*tok, dh2, dh2b, pre, h1, g2, w_down, w_up)


OPB_SSD_G, OPB_LN_G, OPB_LN_B, OPB_CONV_B = 0, 1, 2, 3


def _out_proj_bwd(dh1b, y, z, v2, g_ssd, ln_g, ln_b, w_out, token=None):
    t = y.shape[0]
    tm = min(TOKEN_TILE, t)

    def body(dh_ref, y_ref, z_ref, v_ref, gs_ref, lg_ref, lb_ref, w_ref, dy_ref, dz_ref, dv_ref, acc_ref):
        i = pl.program_id(0)
        _acc_init(i, acc_ref)
        dhb = dh_ref[...]
        dys = _nt(dhb, w_ref[0:1024, :])
        dyc = _nt(dhb, w_ref[1024:2048, :])
        _, vjp_g = jax.vjp(_gated_norm, y_ref[...], z_ref[...], gs_ref[...])
        dy, dz, dgs = vjp_g(dys)
        _, vjp_l = jax.vjp(_ln_silu, v_ref[...], lg_ref[...], lb_ref[...])
        dv, dlg, dlb = vjp_l(dyc)
        dy_ref[...] = dy
        dz_ref[...] = dz.astype(bf16)
        dv_ref[...] = dv
        acc_ref[OPB_SSD_G:OPB_SSD_G + 1, :] += dgs
        acc_ref[OPB_LN_G:OPB_LN_G + 1, :] += dlg
        acc_ref[OPB_LN_B:OPB_LN_B + 1, :] += dlb
        acc_ref[OPB_CONV_B:OPB_CONV_B + 1, :] += jnp.sum(dv, axis=0, keepdims=True)

    vec = _const_spec((1, 1024))
    row = _row_spec(tm, 1024)
    body, tok, tok_spec = _after_token(body, token)
    return pl.pallas_call(
        body, name="out_proj_bwd", grid=(t // tm,),
        in_specs=tok_spec + [row, row, row, row, vec, vec, vec, _const_spec((2048, 1024))],
        out_specs=[row, row, row, _const_spec((8, 1024))],
        out_shape=(jax.ShapeDtypeStruct((t, 1024), f32), jax.ShapeDtypeStruct((t, 1024), bf16),
                   jax.ShapeDtypeStruct((t, 1024), f32), jax.ShapeDtypeStruct((8, 1024), f32)),
        compiler_params=_params("arbitrary"),
    )(*tok, dh1b, y, z, v2, g_ssd, ln_g, ln_b, w_out)


def _conf_conv_bwd(dv2, cv, cg, w):
    t = cv.shape[0]
    tm = min(TOKEN_TILE, t)
    ext_rows = tm + CONV_HALO
    offsets = [CONF_KERNEL - 1 - k for k in range(CONF_KERNEL)]

    def body(dv_ref, dvn_ref, cv_ref, cg_ref, w_ref, dcv_ref, dcg_ref, dw_ref, glu_ref, dext_ref, sg_ref, shifted_ref):
        i = pl.program_id(0)
        _acc_init(i, dw_ref)
        sg = jax.nn.sigmoid(cg_ref[...])
        sg_ref[...] = sg
        glu_ref[...] = cv_ref[...] * sg
        dext_ref[0:tm, :] = dv_ref[...]
        dext_ref[tm:, :] = jnp.where(i == pl.num_programs(0) - 1, 0.0, dvn_ref[...])
        _shifted_copies(dext_ref, shifted_ref, ext_rows - SUBLANES)

        for lb in range(1024 // LANE_TILE):
            lanes = slice(lb * LANE_TILE, (lb + 1) * LANE_TILE)

            def rows_w(rb, accs, lanes=lanes):
                r0 = pl.multiple_of(rb * CONV_BLOCK, CONV_BLOCK)
                accs = list(accs)
                for j in range(CONV_BLOCK // SUBLANES):
                    x = glu_ref[pl.ds(r0 + SUBLANES * j, SUBLANES), lanes]
                    for k, off in enumerate(offsets):
                        ref, base = _tap_source(dext_ref, shifted_ref, off)
                        accs[k] = accs[k] + x * ref[pl.ds(r0 + base + SUBLANES * j, SUBLANES), lanes]
                return tuple(accs)

            zero = jnp.zeros((SUBLANES, LANE_TILE), f32)
            accs = lax.fori_loop(0, tm // CONV_BLOCK, rows_w, (zero,) * CONF_KERNEL)
            for k in range(CONF_KERNEL):
                dw_ref[k:k + 1, lanes] += jnp.sum(accs[k], axis=0, keepdims=True)

        for lb in range(1024 // LANE_TILE):
            lanes = slice(lb * LANE_TILE, (lb + 1) * LANE_TILE)
            taps = [jnp.broadcast_to(w_ref[k:k + 1, lanes], (SUBLANES, LANE_TILE)) for k in range(CONF_KERNEL)]

            def rows_x(rb, carry, lanes=lanes, taps=taps):
                r0 = pl.multiple_of(rb * CONV_BLOCK, CONV_BLOCK)
                zero = jnp.zeros((SUBLANES, LANE_TILE), f32)
                dglu = jnp.concatenate(_conv_rows(dext_ref, shifted_ref, taps, offsets, lanes, r0, zero), axis=0)
                sg = sg_ref[pl.ds(r0, CONV_BLOCK), lanes]
                cvv = cv_ref[pl.ds(r0, CONV_BLOCK), lanes]
                dcv_ref[pl.ds(r0, CONV_BLOCK), lanes] = (dglu * sg).astype(bf16)
                dcg_ref[pl.ds(r0, CONV_BLOCK), lanes] = (dglu * cvv * sg * (1.0 - sg)).astype(bf16)
                return carry

            lax.fori_loop(0, tm // CONV_BLOCK, rows_x, 0)

    row = _row_spec(tm, 1024)
    return pl.pallas_call(
        body, name="conf_conv_bwd", grid=(t // tm,),
        in_specs=[row, _next_halo_spec(tm, CONV_HALO, 1024, t), row, row, _const_spec((32, 1024))],
        out_specs=[row, row, _const_spec((32, 1024))],
        out_shape=(jax.ShapeDtypeStruct((t, 1024), bf16), jax.ShapeDtypeStruct((t, 1024), bf16),
                   jax.ShapeDtypeStruct((32, 1024), f32)),
        scratch_shapes=[pltpu.VMEM((tm, 1024), f32), pltpu.VMEM((ext_rows, 1024), f32), pltpu.VMEM((tm, 1024), f32),
                        pltpu.VMEM((SUBLANES, ext_rows, 1024), f32)],
        compiler_params=_params("arbitrary"),
    )(dv2, dv2, cv, cg, w)


def _ssd_scan_bwd(xbc_act, dt, s_prev, dy, a_log, d_skip):
    t = xbc_act.shape[0]
    steps = t // CHUNK // SCAN_CHUNKS
    rows = SCAN_CHUNKS * CHUNK

    def body(xbc_ref, dt_ref, sp_ref, dy_ref, al_ref, dk_ref, dxbc_ref, ddt_ref, dal_ref, ddk_ref, ds_ref):
        i = pl.program_id(0)
        _acc_init(i, ds_ref, dal_ref, ddk_ref)
        consts = _ssd_consts()
        ds = ds_ref[...]
        dal_sum, ddk_sum = dal_ref[...], ddk_ref[...]
        for j in reversed(range(SCAN_CHUNKS)):
            rs = slice(j * CHUNK, (j + 1) * CHUNK)
            _, vjp_c = jax.vjp(
                functools.partial(_ssd_chunk, consts=consts),
                xbc_ref[rs, 0:1024], xbc_ref[rs, 1024:1280], xbc_ref[rs, 1280:1536], dt_ref[rs, :], sp_ref[j],
                al_ref[...], dk_ref[...])
            dxs, dbm, dcm, ddt, ds, dal, ddk = vjp_c((dy_ref[rs, :], ds))
            dxbc_ref[rs, 0:1024] = dxs
            dxbc_ref[rs, 1024:1280] = dbm
            dxbc_ref[rs, 1280:1536] = dcm
            ddt_ref[rs, :] = ddt
            dal_sum, ddk_sum = dal_sum + dal, ddk_sum + ddk
        ds_ref[...] = ds
        dal_ref[...] = dal_sum
        ddk_ref[...] = ddk_sum

    rev = lambda i: (steps - 1 - i, 0)
    return pl.pallas_call(
        body, name="ssd_scan_bwd", grid=(steps,),
        in_specs=[pl.BlockSpec((rows, XBC_WIDTH), rev), pl.BlockSpec((rows, SSD_HEADS), rev),
                  pl.BlockSpec((SCAN_CHUNKS, SSD_STATE, D_MODEL), lambda i: (steps - 1 - i, 0, 0)),
                  pl.BlockSpec((rows, D_MODEL), rev), _const_spec((1, SSD_HEADS)), _const_spec((1, SSD_HEADS))],
        out_specs=[pl.BlockSpec((rows, XBC_WIDTH), rev), pl.BlockSpec((rows, SSD_HEADS), rev),
                   _const_spec((1, SSD_HEADS)), _const_spec((1, SSD_HEADS))],
        out_shape=(jax.ShapeDtypeStruct((t, XBC_WIDTH), f32), jax.ShapeDtypeStruct((t, SSD_HEADS), f32),
                   jax.ShapeDtypeStruct((1, SSD_HEADS), f32), jax.ShapeDtypeStruct((1, SSD_HEADS), f32)),
        scratch_shapes=[pltpu.VMEM((SSD_STATE, D_MODEL), f32)],
        compiler_params=_params("arbitrary"),
    )(xbc_act, dt, s_prev, dy, a_log, d_skip)


def _ssd_pre_bwd(xbc_raw, dxbc_act, ddt, dt_raw, cw, cb, dt_bias):
    t = xbc_raw.shape[0]
    tm = min(TOKEN_TILE, t)

    def body(cur_ref, halo_ref, dact_ref, ddt_ref, dtr_ref, w_ref, b_ref, dtb_ref,
             dco_ref, ddtr_ref, dw_ref, db_ref, ddtb_ref, ext_ref):
        i = pl.program_id(0)
        _acc_init(i, dw_ref, db_ref, ddtb_ref)
        ext_ref[0:8, :] = jnp.where(i == 0, 0.0, halo_ref[...])
        ext_ref[8:, :] = cur_ref[...]
        fold = lambda a: a[0:8] + a[8:ROW_BLOCK]
        for c in range(0, XBC_WIDTH, LANE_CHUNK):
            ls = slice(c, c + LANE_CHUNK)
            part_b = jnp.zeros((8, LANE_CHUNK), f32)
            part_w = [jnp.zeros((8, LANE_CHUNK), f32)] * SSD_CONV
            for r in range(0, tm, ROW_BLOCK):
                rs = slice(r, r + ROW_BLOCK)
                co = _conv4_block(ext_ref, w_ref, b_ref, rs, ls)
                sg = jax.nn.sigmoid(co)
                dco = dact_ref[rs, ls] * sg * (1.0 + co * (1.0 - sg))
                dco_ref[rs, ls] = dco
                part_b = part_b + fold(dco)
                part_w = [pw + fold(dco * ext_ref[r + 8 - 3 + k:r + ROW_BLOCK + 8 - 3 + k, ls])
                          for k, pw in enumerate(part_w)]
            db_ref[:, ls] += jnp.sum(part_b, axis=0, keepdims=True)
            for k in range(SSD_CONV):
                dw_ref[k:k + 1, ls] += jnp.sum(part_w[k], axis=0, keepdims=True)
        ddtr = ddt_ref[...] * jax.nn.sigmoid(dtr_ref[...] + dtb_ref[...])
        ddtb_ref[...] += jnp.sum(ddtr, axis=0, keepdims=True)
        ddtr_ref[...] = ddtr.astype(bf16)

    return pl.pallas_call(
        body, name="ssd_pre_bwd", grid=(t // tm,),
        in_specs=[_row_spec(tm, XBC_WIDTH), _prev_halo_spec(tm, 8, XBC_WIDTH), _row_spec(tm, XBC_WIDTH),
                  _row_spec(tm, SSD_HEADS), _row_spec(tm, SSD_HEADS), _const_spec((8, XBC_WIDTH)),
                  _const_spec((1, XBC_WIDTH)), _const_spec((1, SSD_HEADS))],
        out_specs=[_row_spec(tm, XBC_WIDTH), _row_spec(tm, SSD_HEADS), _const_spec((8, XBC_WIDTH)),
                   _const_spec((1, XBC_WIDTH)), _const_spec((1, SSD_HEADS))],
        out_shape=(jax.ShapeDtypeStruct((t, XBC_WIDTH), f32), jax.ShapeDtypeStruct((t, SSD_HEADS), bf16),
                   jax.ShapeDtypeStruct((8, XBC_WIDTH), f32), jax.ShapeDtypeStruct((1, XBC_WIDTH), f32),
                   jax.ShapeDtypeStruct((1, SSD_HEADS), f32)),
        scratch_shapes=[pltpu.VMEM((tm + 8, XBC_WIDTH), f32)],
        compiler_params=_params("arbitrary"),
    )(xbc_raw, xbc_raw, dxbc_act, ddt, dt_raw, cw, cb, dt_bias)


def _conv4_bwd_data(dco, cw):
    t = dco.shape[0]
    tm = min(TOKEN_TILE, t)

    def body(cur_ref, nxt_ref, w_ref, o_ref, ext_ref):
        i = pl.program_id(0)
        ext_ref[0:tm, :] = cur_ref[...]
        ext_ref[tm:, :] = jnp.where(i == pl.num_programs(0) - 1, 0.0, nxt_ref[...])
        for rs, ls in _blocks(tm, XBC_WIDTH):
            acc = w_ref[0:1, ls] * ext_ref[rs.start + SSD_CONV - 1:rs.stop + SSD_CONV - 1, ls]
            for k in range(1, SSD_CONV):
                acc = acc + w_ref[k:k + 1, ls] * ext_ref[rs.start + SSD_CONV - 1 - k:rs.stop + SSD_CONV - 1 - k, ls]
            o_ref[rs, ls] = acc.astype(bf16)

    return pl.pallas_call(
        body, name="conv4_bwd_data", grid=(t // tm,),
        in_specs=[_row_spec(tm, XBC_WIDTH), _next_halo_spec(tm, 8, XBC_WIDTH, t), _const_spec((8, XBC_WIDTH))],
        out_specs=_row_spec(tm, XBC_WIDTH), out_shape=jax.ShapeDtypeStruct((t, XBC_WIDTH), bf16),
        scratch_shapes=[pltpu.VMEM((tm + 8, XBC_WIDTH), f32)],
        compiler_params=_params("arbitrary"),
    )(dco, dco, cw)


def _in_proj_bwd(dproj, x, dh1, g1, w_in_p, token=None):
    t = x.shape[0]
    tm = min(TOKEN_TILE, t)

    def body(dz_ref, dxbc_ref, ddt_ref, dcv_ref, dcg_ref, x_ref, dh_ref, g_ref, w_ref, dx_ref, acc_ref):
        i = pl.program_id(0)
        _acc_init(i, acc_ref)
        du = jnp.zeros((tm, D_MODEL), f32)
        for (lo, hi), r in zip(IN_SEGS, (dz_ref, dxbc_ref, ddt_ref, dcv_ref, dcg_ref)):
            du = du + _nn(r[...], w_ref[lo:hi, :])
        _, vjp_u = jax.vjp(_rms, x_ref[...], g_ref[...])
        d, dg = vjp_u(du)
        dx_ref[...] = dh_ref[...] + d
        acc_ref[0:1, :] += dg

    row = _row_spec(tm, 1024)
    body, tok, tok_spec = _after_token(body, token)
    return pl.pallas_call(
        body, name="in_proj_bwd", grid=(t // tm,),
        in_specs=tok_spec + [_row_spec(tm, hi - lo) for lo, hi in IN_SEGS] + [row, row, _const_spec((1, 1024)),
                                                                               _const_spec((IN_WIDTH, D_MODEL))],
        out_specs=[row, _const_spec((8, 1024))],
        out_shape=(jax.ShapeDtypeStruct((t, 1024), f32), jax.ShapeDtypeStruct((8, 1024), f32)),
        compiler_params=_params("arbitrary"),
    )(*tok, *dproj, x, dh1, g1, w_in_p)


def _mm_tn(a, b, tk, tn, name, column_blocks=False):
    t, kk = a.shape
    n = b.shape[1]
    tk, tn = min(tk, kk), min(tn, n)

    def body(a_ref, b_ref, o_ref):
        o_ref[...] = _tn(a_ref[...], b_ref[...]).astype(bf16)

    if column_blocks:
        assert tk == kk
        out_spec = pl.BlockSpec((None, tk, tn), lambda i, j: (j, 0, 0))
        out_shape = jax.ShapeDtypeStruct((n // tn, kk, tn), bf16)
    else:
        out_spec = pl.BlockSpec((tk, tn), lambda i, j: (i, j))
        out_shape = jax.ShapeDtypeStruct((kk, n), bf16)
    return pl.pallas_call(
        body, name=name, grid=(kk // tk, n // tn),
        in_specs=[pl.BlockSpec((t, tk), lambda i, j: (0, i)), pl.BlockSpec((t, tn), lambda i, j: (0, j))],
        out_specs=out_spec, out_shape=out_shape,
        compiler_params=_params("arbitrary", "arbitrary"),
    )(a, b)


class _LocalWeights:
    def __init__(self, w):
        self.w = w
        self.sent = {}

    def start_token(self):
        return None

    def weight(self, name, after=()):
        del after
        return self.w[name]

    def send_grads(self, grads):
        self.sent.update(grads)
        return None


def _local_step(x, p, target, s, comm):
    conv_w, conf_w = comm.weight("ssd_conv_w"), comm.weight("conf_dw_w")
    w_in = comm.weight("w_in")
    u, z, xbc_raw, dt_raw, cv, cg = _in_proj(x, s["mix_norm_g"], w_in, token=comm.start_token())
    xbc_act, dt = _ssd_pre(xbc_raw, dt_raw, conv_w, s["ssd_conv_b"], s["dt_bias"])
    y, s_prev = _ssd_scan(xbc_act, dt, s["A_log"], s["D_skip"])
    v2 = _conf_conv(cv, cg, conf_w, s["conf_dw_b"])
    w_out = comm.weight("w_out", after=(y, v2))
    ys, yc, h1, u2 = _out_proj(y, z, v2, x, s["ssd_norm_g"], s["conf_ln_g"], s["conf_ln_b"], w_out, s["mlp_norm_g"])
    w_up = comm.weight("w_up", after=(u2,))
    pre, hs = _mlp_up(u2, w_up)
    w_down = comm.weight("w_down", after=(hs,))
    h2 = _mlp_down(h1, hs, w_down)
    w_gate, w_ple = comm.weight("w_ple_gate", after=(h2,)), comm.weight("w_ple", after=(h2,))
    dh2, dh2b, dwg, dwp, tail_acc = _tail(h2, p, target, s["ple_gate_norm_g"], w_gate, s["b_ple_gate"], w_ple,
                                          s["ple_norm_g"], s["final_norm_g"])
    token = comm.send_grads({"w_ple_gate": dwg, "w_ple": dwp})
    dpre, dh1, dh1b, mlp_acc = _mlp_bwd(dh2, dh2b, pre, h1, s["mlp_norm_g"], w_down, w_up, token=token)
    token = comm.send_grads({
        "w_down": _mm_tn(hs, dh2b, 512, 1024, "dw_down"),
        "w_up": _mm_tn(u2, dpre, 1024, D_FF // N_DEV, "dw_up", column_blocks=True),
        "w_out": [_mm_tn(ys, dh1b, 1024, 512, "dw_out_ssd"), _mm_tn(yc, dh1b, 1024, 512, "dw_out_conf")],
    })
    dy, dz, dv2, opb_acc = _out_proj_bwd(dh1b, y, z, v2, s["ssd_norm_g"], s["conf_ln_g"], s["conf_ln_b"], w_out,
                                         token=token)
    dcv, dcg, dconf_w = _conf_conv_bwd(dv2, cv, cg, conf_w)
    dxbc_act, ddt, d_alog, d_dskip = _ssd_scan_bwd(xbc_act, dt, s_prev, dy, s["A_log"], s["D_skip"])
    dco, ddt_raw, dconv_w, dconv_b, d_dtb = _ssd_pre_bwd(xbc_raw, dxbc_act, ddt, dt_raw, conv_w, s["ssd_conv_b"],
                                                        s["dt_bias"])
    dxbc_raw = _conv4_bwd_data(dco, conv_w)
    dproj = (dz, dxbc_raw, ddt_raw, dcv, dcg)
    token = comm.send_grads({"w_in": [_mm_tn(d, u, 512, 1024, "dw_in_" + n)
                                      for n, d in zip(("z", "xbc", "dt", "cv", "cg"), dproj)]})
    grad_x, inp_acc = _in_proj_bwd(dproj, x, dh1, s["mix_norm_g"], w_in, token=token)
    acc = {"in_proj": inp_acc, "out_proj": opb_acc, "mlp": mlp_acc, "tail": tail_acc, "ssd_conv_b": dconv_b,
           "dt_bias": d_dtb, "A_log": d_alog, "D_skip": d_dskip, "ssd_conv_w": dconv_w, "conf_dw_w": dconf_w}
    return grad_x, acc


def _mesh_pos():
    return lax.axis_index("x"), lax.axis_index("y"), lax.axis_index("c")


def _other_chips(x, y):
    return [(1 - x, y), (x, 1 - y), (1 - x, 1 - y)]


def _all_gather(arrs, name):
    n = len(arrs)

    def body(*refs):
        ins, outs = refs[:n], refs[n:2 * n]
        send_sems, recv_sems, local_sems = refs[2 * n:]
        x, y, c = _mesh_pos()
        me = 4 * x + 2 * y + c
        sibling = (x, y, 1 - c)
        chips = _other_chips(x, y)

        def copy(a, k, block, to, src=None):
            dst = outs[a].at[block]
            return pltpu.make_async_remote_copy(
                src_ref=dst if src is None else src, dst_ref=dst, send_sem=send_sems.at[a, k],
                recv_sem=recv_sems.at[a, k], device_id=to, device_id_type=MESH_ID)

        mine = [pltpu.make_async_copy(ins[a], outs[a].at[me], local_sems.at[a]) for a in range(n)]
        for cp in mine:
            cp.start()
        first = []
        for a in range(n):
            first.append(copy(a, 0, me, sibling, src=ins[a]))
            first += [copy(a, 1 + j, me, (px, py, c), src=ins[a]) for j, (px, py) in enumerate(chips)]
        for cp in first:
            cp.start()
        passed = []
        for j, (px, py) in enumerate(chips):
            for a in range(n):
                blk = 4 * px + 2 * py + c
                copy(a, 1 + j, blk, (x, y, c)).wait_recv()
                cp = copy(a, 4 + j, blk, sibling)
                cp.start()
                passed.append(cp)
        for a in range(n):
            copy(a, 0, 4 * x + 2 * y + (1 - c), (x, y, c)).wait_recv()
        for j, (px, py) in enumerate(chips):
            for a in range(n):
                copy(a, 4 + j, 4 * px + 2 * py + (1 - c), (x, y, c)).wait_recv()
        for cp in first + passed:
            cp.wait_send()
        for cp in mine:
            cp.wait()

    return pl.pallas_call(
        body, name=name,
        in_specs=[HBM_SPEC] * n, out_specs=[HBM_SPEC] * n,
        out_shape=[jax.ShapeDtypeStruct((N_DEV,) + a.shape, a.dtype) for a in arrs],
        scratch_shapes=[pltpu.SemaphoreType.DMA((n, 7)), pltpu.SemaphoreType.DMA((n, 7)), pltpu.SemaphoreType.DMA((n,))],
    )(*arrs)


_PEER_FLIPS = ((0, 0, 1), (1, 0, 0), (0, 1, 0), (1, 1, 0), (1, 0, 1), (0, 1, 1), (1, 1, 1))
SEM_SPEC = pl.BlockSpec(memory_space=pltpu.SEMAPHORE)
ANY_SPEC = pl.BlockSpec(memory_space=pl.ANY)


def _flip(v, d):
    return 1 - v if d else v


def _peers(x, y, c):
    out = []
    for dx, dy, dc in _PEER_FLIPS:
        px, py, pc = _flip(x, dx), _flip(y, dy), _flip(c, dc)
        out.append(((px, py, pc), 4 * px + 2 * py + pc))
    return out


def _exchange_copy(src_ref, land_ref, send_sems, recv_sems, k, peer, peer_block, my_block, by_block, outgoing):
    src = src_ref.at[peer_block] if by_block else src_ref
    dst = land_ref.at[my_block if outgoing else peer_block]
    return pltpu.make_async_remote_copy(src_ref=src, dst_ref=dst, send_sem=send_sems.at[k], recv_sem=recv_sems.at[k],
                                        device_id=peer, device_id_type=MESH_ID)


def _exchange_start(srcs, by_block, name, after):
    n = len(srcs)
    x_pos, y_pos, c_pos = _mesh_pos()
    me_at = 4 * x_pos + 2 * y_pos + c_pos
    lands = []
    for a in srcs:
        own = lax.dynamic_index_in_dim(a, me_at, 0, keepdims=False) if by_block else a
        lands.append(lax.dynamic_update_slice(lax.empty((N_DEV,) + own.shape, own.dtype), own[None],
                                              (me_at,) + (0,) * own.ndim))

    def body(*refs):
        src_refs, land_refs = refs[1:1 + n], refs[1 + n:1 + 2 * n]
        outs = refs[1 + 2 * n:]
        send, recv, token = outs[:n], outs[n:2 * n], outs[4 * n]
        x, y, c = _mesh_pos()
        me = 4 * x + 2 * y + c
        for a in range(n):
            for k, (peer, blk) in enumerate(_peers(x, y, c)):
                _exchange_copy(src_refs[a], land_refs[a], send[a], recv[a], k, peer, blk, me, by_block, True).start()
        token[...] = jnp.zeros_like(token)

    sems = [pltpu.SemaphoreType.DMA((N_DEV - 1,))] * (2 * n)
    thru = [pltpu.HBM(a.shape, a.dtype) for a in list(srcs) + list(lands)]
    res = pl.pallas_call(
        body, name=name,
        in_specs=[ANY_SPEC] + [HBM_SPEC] * (2 * n),
        out_specs=[SEM_SPEC] * (2 * n) + [HBM_SPEC] * (2 * n) + [pl.BlockSpec(memory_space=pltpu.VMEM)],
        out_shape=sems + thru + [jax.ShapeDtypeStruct((8, 128), f32)],
        input_output_aliases={1 + i: 2 * n + i for i in range(2 * n)},
        compiler_params=pltpu.CompilerParams(has_side_effects=pltpu.SideEffectType.DATAFLOW_SIDE_EFFECTING),
    )(after, *[pltpu.with_memory_space_constraint(a, pltpu.HBM) for a in list(srcs) + list(lands)])
    states = [(res[2 * n + a], res[3 * n + a], res[a], res[n + a]) for a in range(n)]
    return states, res[4 * n]


def _exchange_wait(states, by_block, name, after):
    n, na = len(states), len(after)

    def body(*refs):
        src_refs, land_refs = refs[:n], refs[n:2 * n]
        send, recv = refs[2 * n:3 * n], refs[3 * n:4 * n]
        x, y, c = _mesh_pos()
        me = 4 * x + 2 * y + c
        for a in range(n):
            for k, (peer, blk) in enumerate(_peers(x, y, c)):
                _exchange_copy(src_refs[a], land_refs[a], send[a], recv[a], k, peer, blk, me, by_block, True).wait_send()
                _exchange_copy(src_refs[a], land_refs[a], send[a], recv[a], k, peer, blk, me, by_block, False).wait_recv()

    srcs, lands = [s[0] for s in states], [s[1] for s in states]
    res = pl.pallas_call(
        body, name=name,
        in_specs=[HBM_SPEC] * (2 * n) + [SEM_SPEC] * (2 * n) + [ANY_SPEC] * na,
        out_specs=[HBM_SPEC] * (2 * n),
        out_shape=[pltpu.HBM(a.shape, a.dtype) for a in srcs + lands],
        input_output_aliases={i: i for i in range(2 * n)},
        compiler_params=pltpu.CompilerParams(has_side_effects=pltpu.SideEffectType.DATAFLOW_SIDE_EFFECTING),
    )(*srcs, *lands, *[s[2] for s in states], *[s[3] for s in states], *after)
    return list(res[n:2 * n])


def _adamw_math(w, g, m, v):
    m = ADAM_B1 * m + (1.0 - ADAM_B1) * g
    v = ADAM_B2 * v + (1.0 - ADAM_B2) * (g * g)
    m_hat = m / (1.0 - ADAM_B1 ** ADAM_STEP)
    v_hat = v / (1.0 - ADAM_B2 ** ADAM_STEP)
    delta = -ADAM_LR * (m_hat / (jnp.sqrt(v_hat) + ADAM_EPS) + ADAM_WD * w)
    return delta, m, v


def _adamw_big(parts, w, m, v, name):
    rows, cols = w.shape
    tr, tc = (256, cols) if rows % 256 == 0 else (rows, 256)
    nparts = parts.shape[0]

    def body(p_ref, w_ref, m_ref, v_ref, g_ref, d_ref, mo_ref, vo_ref):
        g = p_ref[0].astype(f32)
        for j in range(1, nparts):
            g = g + p_ref[j].astype(f32)
        d, mn, vn = _adamw_math(w_ref[...], g, m_ref[...], v_ref[...])
        g_ref[...] = g
        d_ref[...] = d
        mo_ref[...] = mn
        vo_ref[...] = vn

    tile = pl.BlockSpec((tr, tc), lambda i, j: (i, j))
    shp = jax.ShapeDtypeStruct((rows, cols), f32)
    return pl.pallas_call(
        body, name=name, grid=(rows // tr, cols // tc),
        in_specs=[pl.BlockSpec((nparts, tr, tc), lambda i, j: (0, i, j)), tile, tile, tile],
        out_specs=[tile, tile, tile, tile], out_shape=(shp, shp, shp, shp),
        compiler_params=_params("arbitrary", "arbitrary"),
    )(parts, w, m, v)


PACK_ROWS = 56
_PACK_AT = {
    "mix_norm_g": (0, 0, 1024), "ssd_norm_g": (1, 0, 1024), "conf_ln_g": (2, 0, 1024), "conf_ln_b": (3, 0, 1024),
    "conf_dw_b": (4, 0, 1024), "mlp_norm_g": (5, 0, 1024), "final_norm_g": (7, 0, 1024), "ple_norm_g": (8, 0, 1024),
    "b_ple_gate": (9, 0, 1024), "ple_gate_norm_g": (10, 0, 1024), "dt_bias": (13, 0, 16), "A_log": (13, 128, 16),
    "D_skip": (13, 256, 16),
}
PACK_LOSS_ROW = 6
PACK_CONV_B_ROW = 11
PACK_CONV_W_ROW = 14
PACK_CONF_W_ROW = 24


def _pack_small(acc):
    def body(inp, opb, mlp, tail, cb, dtb, alog, dskip, cw, fw, o_ref):
        o_ref[...] = jnp.zeros_like(o_ref)
        rows = {"mix_norm_g": inp[0:1, :], "mlp_norm_g": mlp[0:1, :],
                "ssd_norm_g": opb[OPB_SSD_G:OPB_SSD_G + 1, :], "conf_ln_g": opb[OPB_LN_G:OPB_LN_G + 1, :],
                "conf_ln_b": opb[OPB_LN_B:OPB_LN_B + 1, :], "conf_dw_b": opb[OPB_CONV_B:OPB_CONV_B + 1, :],
                "final_norm_g": tail[TAIL_FINAL_G:TAIL_FINAL_G + 1, :], "ple_norm_g": tail[TAIL_PLE_G:TAIL_PLE_G + 1, :],
                "b_ple_gate": tail[TAIL_GATE_B:TAIL_GATE_B + 1, :],
                "ple_gate_norm_g": tail[TAIL_GATE_NORM_G:TAIL_GATE_NORM_G + 1, :],
                "dt_bias": dtb[...], "A_log": alog[...], "D_skip": dskip[...]}
        for name, val in rows.items():
            r, lo, width = _PACK_AT[name]
            o_ref[r:r + 1, lo:lo + width] = val
        o_ref[PACK_LOSS_ROW:PACK_LOSS_ROW + 1, :] = tail[TAIL_LOSS:TAIL_LOSS + 1, :]
        o_ref[PACK_CONV_B_ROW:PACK_CONV_B_ROW + 1, :] = cb[:, 0:1024]
        o_ref[PACK_CONV_B_ROW + 1:PACK_CONV_B_ROW + 2, 0:512] = cb[:, 1024:XBC_WIDTH]
        for k in range(SSD_CONV):
            o_ref[PACK_CONV_W_ROW + k:PACK_CONV_W_ROW + k + 1, :] = cw[k:k + 1, 0:1024]
            o_ref[PACK_CONV_W_ROW + SSD_CONV + k:PACK_CONV_W_ROW + SSD_CONV + k + 1, 0:512] = cw[k:k + 1, 1024:XBC_WIDTH]
        o_ref[PACK_CONF_W_ROW:PACK_CONF_W_ROW + 32, :] = fw[...]

    return pl.pallas_call(body, name="pack_small", out_shape=jax.ShapeDtypeStruct((PACK_ROWS, 1024), f32))(
        acc["in_proj"], acc["out_proj"], acc["mlp"], acc["tail"], acc["ssd_conv_b"], acc["dt_bias"], acc["A_log"],
        acc["D_skip"], acc["ssd_conv_w"], acc["conf_dw_w"])


def _small_update(all_small, w, m, v):
    names = _REPLICATED

    def body(all_ref, *refs):
        ins, outs = refs[:3 * len(names)], refs[3 * len(names):]
        s = all_ref[0]
        for j in range(1, N_DEV):
            s = s + all_ref[j]
        outs[0][...] = s[PACK_LOSS_ROW:PACK_LOSS_ROW + 1, 0:1]
        outs[1][...] = jnp.concatenate([s[PACK_CONV_W_ROW:PACK_CONV_W_ROW + SSD_CONV, :],
                                        s[PACK_CONV_W_ROW + SSD_CONV:PACK_CONV_W_ROW + 2 * SSD_CONV, 0:512]], axis=1)
        outs[2][...] = s[PACK_CONF_W_ROW:PACK_CONF_W_ROW + CONF_KERNEL, :]
        for i, name in enumerate(names):
            if name == "ssd_conv_b":
                g = jnp.concatenate([s[PACK_CONV_B_ROW:PACK_CONV_B_ROW + 1, :],
                                     s[PACK_CONV_B_ROW + 1:PACK_CONV_B_ROW + 2, 0:512]], axis=1)
            else:
                r, lo, width = _PACK_AT[name]
                g = s[r:r + 1, lo:lo + width]
            d, mn, vn = _adamw_math(ins[3 * i][...], g, ins[3 * i + 1][...], ins[3 * i + 2][...])
            for o_ref, val in zip(outs[3 + 4 * i:7 + 4 * i], (g, d, mn, vn)):
                o_ref[...] = val

    shapes = [jax.ShapeDtypeStruct((1, 1), f32), jax.ShapeDtypeStruct((SSD_CONV, XBC_WIDTH), f32),
              jax.ShapeDtypeStruct((CONF_KERNEL, D_MODEL), f32)]
    operands = []
    for name in names:
        operands += [w[name], m[name], v[name]]
        shapes += [jax.ShapeDtypeStruct(w[name].shape, f32)] * 4
    res = pl.pallas_call(body, name="small_update", out_shape=shapes)(all_small, *operands)
    per_name = {name: tuple(res[3 + 4 * i:7 + 4 * i]) for i, name in enumerate(names)}
    return res[0], per_name, res[1], res[2]


def _adamw_filters(g, w, m, v):
    n = len(g)

    def body(*refs):
        ins, outs = refs[:4 * n], refs[4 * n:]
        for i in range(n):
            d, mn, vn = _adamw_math(ins[4 * i + 1][...], ins[4 * i][...], ins[4 * i + 2][...], ins[4 * i + 3][...])
            for o_ref, val in zip(outs[3 * i:3 * i + 3], (d, mn, vn)):
                o_ref[...] = val

    operands, shapes = [], []
    for i in range(n):
        operands += [g[i], w[i], m[i], v[i]]
        shapes += [jax.ShapeDtypeStruct(w[i].shape, f32)] * 3
    res = pl.pallas_call(body, name="adamw_filters", out_shape=shapes)(*operands)
    return [tuple(res[3 * i:3 * i + 3]) for i in range(n)]


_REPLICATED = ("mix_norm_g", "ssd_conv_b", "dt_bias", "A_log", "D_skip", "ssd_norm_g", "conf_dw_b", "conf_ln_g",
               "conf_ln_b", "mlp_norm_g", "ple_gate_norm_g", "b_ple_gate", "ple_norm_g", "final_norm_g")
_CONV_WEIGHTS = ("ssd_conv_w", "conf_dw_w")
_WEIGHT_ORDER = ("mix_norm_g", "w_in", "ssd_conv_w", "ssd_conv_b", "dt_bias", "A_log", "D_skip", "ssd_norm_g", "conf_dw_w",
                 "conf_dw_b", "conf_ln_g", "conf_ln_b", "w_out", "mlp_norm_g", "w_up", "w_down", "ple_gate_norm_g",
                 "w_ple_gate", "b_ple_gate", "w_ple", "ple_norm_g", "final_norm_g")


def _blocks_of_columns(a):
    r, c8 = a.shape
    return jnp.transpose(a.reshape(r, N_DEV, c8 // N_DEV), (1, 0, 2))


def _columns_of_blocks(a):
    _, r, c = a.shape
    return jnp.transpose(a, (1, 0, 2)).reshape(r, N_DEV * c)


_LATER = ("w_out", "w_up", "w_down", "w_ple_gate", "w_ple")
_WHOLE = {
    "w_out": lambda a: a.reshape(2048, D_MODEL),
    "w_up": lambda a: a,
    "w_down": lambda a: a.reshape(D_FF, D_MODEL),
    "w_ple_gate": lambda a: a.reshape(D_MODEL, D_MODEL),
    "w_ple": _columns_of_blocks,
}
_BY_BLOCK = {
    "w_in": lambda g: jnp.concatenate(g, axis=0).reshape(N_DEV, IN_WIDTH // N_DEV, D_MODEL),
    "w_out": lambda g: jnp.concatenate(g, axis=0).reshape(N_DEV, 256, D_MODEL),
    "w_up": lambda g: g,
    "w_down": lambda g: g.reshape(N_DEV, 512, D_MODEL),
    "w_ple_gate": lambda g: g.astype(bf16).reshape(N_DEV, 128, D_MODEL),
    "w_ple": lambda g: _blocks_of_columns(g.astype(bf16)),
}


class _StepComm:
    def __init__(self, me, ready, gathers, token):
        self.me, self.ready, self.gathers, self.token = me, ready, gathers, token
        self.sent = []

    def start_token(self):
        return self.token

    def weight(self, name, after=()):
        if name not in self.ready:
            (land,) = _exchange_wait([self.gathers[name]], False, "gather_wait_" + name, list(after))
            self.ready[name] = _WHOLE[name](land)
        return self.ready[name]

    def send_grads(self, grads):
        names = list(grads)
        blocks = [_BY_BLOCK[n](grads[n]) for n in names]
        states, self.token = _exchange_start(blocks, True, "scatter_start_" + names[0], self.token)
        self.sent.append((names, states))
        return self.token


def kernel(x, p, mix_norm_g, w_in, ssd_conv_w, ssd_conv_b, dt_bias, A_log, D_skip, ssd_norm_g, conf_dw_w, conf_dw_b, conf_ln_g, conf_ln_b, w_out, mlp_norm_g, w_up, w_down, ple_gate_norm_g, w_ple_gate, b_ple_gate, w_ple, ple_norm_g, final_norm_g, loss_target, m_mix_norm_g, m_w_in, m_ssd_conv_w, m_ssd_conv_b, m_dt_bias, m_A_log, m_D_skip, m_ssd_norm_g, m_conf_dw_w, m_conf_dw_b, m_conf_ln_g, m_conf_ln_b, m_w_out, m_mlp_norm_g, m_w_up, m_w_down, m_ple_gate_norm_g, m_w_ple_gate, m_b_ple_gate, m_w_ple, m_ple_norm_g, m_final_norm_g, v_mix_norm_g, v_w_in, v_ssd_conv_w, v_ssd_conv_b, v_dt_bias, v_A_log, v_D_skip, v_ssd_norm_g, v_conf_dw_w, v_conf_dw_b, v_conf_ln_g, v_conf_ln_b, v_w_out, v_mlp_norm_g, v_w_up, v_w_down, v_ple_gate_norm_g, v_w_ple_gate, v_b_ple_gate, v_w_ple, v_ple_norm_g, v_final_norm_g):
    wts = dict(mix_norm_g=mix_norm_g, w_in=w_in, ssd_conv_w=ssd_conv_w, ssd_conv_b=ssd_conv_b, dt_bias=dt_bias, A_log=A_log,
               D_skip=D_skip, ssd_norm_g=ssd_norm_g, conf_dw_w=conf_dw_w, conf_dw_b=conf_dw_b, conf_ln_g=conf_ln_g,
               conf_ln_b=conf_ln_b, w_out=w_out, mlp_norm_g=mlp_norm_g, w_up=w_up, w_down=w_down,
               ple_gate_norm_g=ple_gate_norm_g, w_ple_gate=w_ple_gate, b_ple_gate=b_ple_gate, w_ple=w_ple,
               ple_norm_g=ple_norm_g, final_norm_g=final_norm_g)
    mom1 = dict(mix_norm_g=m_mix_norm_g, w_in=m_w_in, ssd_conv_w=m_ssd_conv_w, ssd_conv_b=m_ssd_conv_b, dt_bias=m_dt_bias,
                A_log=m_A_log, D_skip=m_D_skip, ssd_norm_g=m_ssd_norm_g, conf_dw_w=m_conf_dw_w, conf_dw_b=m_conf_dw_b,
                conf_ln_g=m_conf_ln_g, conf_ln_b=m_conf_ln_b, w_out=m_w_out, mlp_norm_g=m_mlp_norm_g, w_up=m_w_up,
                w_down=m_w_down, ple_gate_norm_g=m_ple_gate_norm_g, w_ple_gate=m_w_ple_gate, b_ple_gate=m_b_ple_gate,
                w_ple=m_w_ple, ple_norm_g=m_ple_norm_g, final_norm_g=m_final_norm_g)
    mom2 = dict(mix_norm_g=v_mix_norm_g, w_in=v_w_in, ssd_conv_w=v_ssd_conv_w, ssd_conv_b=v_ssd_conv_b, dt_bias=v_dt_bias,
                A_log=v_A_log, D_skip=v_D_skip, ssd_norm_g=v_ssd_norm_g, conf_dw_w=v_conf_dw_w, conf_dw_b=v_conf_dw_b,
                conf_ln_g=v_conf_ln_g, conf_ln_b=v_conf_ln_b, w_out=v_w_out, mlp_norm_g=v_mlp_norm_g, w_up=v_w_up,
                w_down=v_w_down, ple_gate_norm_g=v_ple_gate_norm_g, w_ple_gate=v_w_ple_gate, b_ple_gate=v_b_ple_gate,
                w_ple=v_w_ple, ple_norm_g=v_ple_norm_g, final_norm_g=v_final_norm_g)
    x_pos, y_pos, c_pos = _mesh_pos()
    me = 4 * x_pos + 2 * y_pos + c_pos

    first = _all_gather([wts["w_in"][0].T.astype(bf16), wts["ssd_conv_w"][0], wts["conf_dw_w"][0]], "gather_first")
    ready = {
        "w_in": first[0].reshape(IN_WIDTH, D_MODEL),
        "ssd_conv_w": jnp.pad(_columns_of_blocks(first[1]), ((0, 8 - SSD_CONV), (0, 0))),
        "conf_dw_w": jnp.pad(_columns_of_blocks(first[2]), ((0, 32 - CONF_KERNEL), (0, 0))),
    }
    shards = [wts[n][0].astype(bf16) for n in _LATER]
    states, token = _exchange_start(shards, False, "gather_start", first[1])
    comm = _StepComm(me, ready, dict(zip(_LATER, states)), token)
    small = {n: wts[n].reshape(1, -1) for n in _REPLICATED}

    grad_x, acc = _local_step(x[0], p[0, 0], loss_target[0], small, comm)

    packed = _pack_small(acc)
    (small_state,), _ = _exchange_start([packed], False, "small_start", comm.token)
    grads, delta, new_m, new_v = {}, {}, {}, {}

    def adamw_big(names, sent, after):
        lands = _exchange_wait(sent, True, "scatter_wait_" + names[0], after)
        for n, land in zip(names, lands):
            view = (lambda a: a[0].T) if n == "w_in" else (lambda a: a[0])
            back = (lambda a: a.T[None]) if n == "w_in" else (lambda a: a[None])
            out = _adamw_big(land, view(wts[n]), view(mom1[n]), view(mom2[n]), "adamw_" + n)
            grads[n], delta[n], new_m[n], new_v[n] = [back(a) for a in out]

    for names, sent in comm.sent[:-1]:
        adamw_big(names, sent, [grad_x])
    adamw_big(*comm.sent[-1], [delta[n] for n in _LATER])
    (all_small,) = _exchange_wait([small_state], False, "small_wait", [delta["w_in"]])
    as_row = lambda d: {n: d[n].reshape(1, -1) for n in _REPLICATED}
    loss, per_name, conv_w_sum, conf_w_sum = _small_update(all_small, as_row(wts), as_row(mom1), as_row(mom2))
    for n in _REPLICATED:
        grads[n], delta[n], new_m[n], new_v[n] = [a.reshape(wts[n].shape) for a in per_name[n]]
    filt_g = [lax.dynamic_slice_in_dim(conv_w_sum, me * 192, 192, axis=1),
              lax.dynamic_slice_in_dim(conf_w_sum, me * 128, 128, axis=1)]
    filt = _adamw_filters(filt_g, [wts[n][0] for n in _CONV_WEIGHTS], [mom1[n][0] for n in _CONV_WEIGHTS],
                          [mom2[n][0] for n in _CONV_WEIGHTS])
    for n, g, (d, mn, vn) in zip(_CONV_WEIGHTS, filt_g, filt):
        grads[n], delta[n], new_m[n], new_v[n] = g[None], d[None], mn[None], vn[None]

    return (loss.reshape(()), grad_x[None], *[grads[n] for n in _WEIGHT_ORDER], *[delta[n] for n in _WEIGHT_ORDER],
            *[new_m[n] for n in _WEIGHT_ORDER], *[new_v[n] for n in _WEIGHT_ORDER])
```

```python
import functools

import jax
import jax.numpy as jnp
from jax import lax
from jax.experimental import pallas as pl
from jax.experimental.pallas import tpu as pltpu

f32 = jnp.float32
bf16 = jnp.bfloat16

EPS = 1e-6
D_MODEL = 1024
SSD_HEADS = 16
SSD_HEAD_DIM = 64
SSD_GROUPS = 2
SSD_STATE = 128
SSD_CONV = 4
CHUNK = 128
XBC_WIDTH = 1536
CONF_KERNEL = 31
D_FF = 4096
PLE_DIM = 256
IN_WIDTH = 4624
N_DEV = 8
SEG_Z = (0, 1024)
SEG_XBC = (1024, 2560)
SEG_DT = (2560, 2576)
SEG_CV = (2576, 3600)
SEG_CG = (3600, 4624)
IN_SEGS = (SEG_Z, SEG_XBC, SEG_DT, SEG_CV, SEG_CG)

ADAM_LR = 0.001
ADAM_B1 = 0.9
ADAM_B2 = 0.999
ADAM_EPS = 1e-08
ADAM_WD = 0.01
ADAM_STEP = 10

VMEM_LIMIT_BYTES = 56 * 1024 * 1024
TOKEN_TILE = 512
SMALL_ROWS = 16

HBM_SPEC = pl.BlockSpec(memory_space=pltpu.HBM)
MESH_ID = pl.DeviceIdType.MESH


def _params(*semantics):
    return pltpu.CompilerParams(dimension_semantics=semantics, vmem_limit_bytes=VMEM_LIMIT_BYTES)


def _nn(a, b):
    return lax.dot_general(a, b, (((1,), (0,)), ((), ())), preferred_element_type=f32)


def _nt(a, b):
    return lax.dot_general(a, b, (((1,), (1,)), ((), ())), preferred_element_type=f32)


def _tn(a, b):
    return lax.dot_general(a, b, (((0,), (0,)), ((), ())), preferred_element_type=f32)


@jax.custom_vjp
def bnn(a, b):
    return _nn(a.astype(bf16), b.astype(bf16))


def _bnn_fwd(a, b):
    ab, bb = a.astype(bf16), b.astype(bf16)
    return _nn(ab, bb), (ab, bb)


def _bnn_bwd(res, g):
    ab, bb = res
    gb = g.astype(bf16)
    return _nt(gb, bb), _tn(ab, gb)


bnn.defvjp(_bnn_fwd, _bnn_bwd)


@jax.custom_vjp
def bnt(a, b):
    return _nt(a.astype(bf16), b.astype(bf16))


def _bnt_fwd(a, b):
    ab, bb = a.astype(bf16), b.astype(bf16)
    return _nt(ab, bb), (ab, bb)


def _bnt_bwd(res, g):
    ab, bb = res
    gb = g.astype(bf16)
    return _nn(gb, bb), _tn(gb, ab)


bnt.defvjp(_bnt_fwd, _bnt_bwd)


@jax.custom_vjp
def btn(a, b):
    return _tn(a.astype(bf16), b.astype(bf16))


def _btn_fwd(a, b):
    ab, bb = a.astype(bf16), b.astype(bf16)
    return _tn(ab, bb), (ab, bb)


def _btn_bwd(res, g):
    ab, bb = res
    gb = g.astype(bf16)
    return _nt(bb, gb), _nn(ab, gb)


btn.defvjp(_btn_fwd, _btn_bwd)

SCAN_CHUNKS = 4
CUMSUM_PASSES = 3
EXPAND_PASSES = 2


def _bf16_pieces(a, passes):
    pieces, rest = [], a
    for _ in range(passes):
        piece = rest.astype(bf16)
        pieces.append(piece)
        rest = rest - piece.astype(f32)
    return pieces


def _make_dot01(form, passes):
    fwd, bwd = {"right": (lambda b, p: _nn(p, b), lambda b, g: _nt(g, b)),
                "left": (lambda b, p: _nn(b, p), lambda b, g: _tn(b, g)),
                "tn": (lambda b, p: _tn(p, b), lambda b, g: _nt(b, g))}[form]

    def run(op, b, v):
        out = None
        for piece in _bf16_pieces(v, passes):
            term = op(b, piece)
            out = term if out is None else out + term
        return out

    @jax.custom_vjp
    def product(b, v):
        return run(fwd, b, v)

    product.defvjp(lambda b, v: (run(fwd, b, v), b), lambda b, g: (jnp.zeros_like(b), run(bwd, b, g)))
    return product


_DOT01 = {(form, passes): _make_dot01(form, passes)
          for form, passes in (("left", CUMSUM_PASSES), ("tn", CUMSUM_PASSES), ("right", EXPAND_PASSES))}


def _dot01(b01, a, form, passes):
    return _DOT01[form, passes](b01.astype(bf16), a)


def _rms(x, g):
    return x * lax.rsqrt(jnp.mean(x * x, axis=-1, keepdims=True) + EPS) * g


def _gated_norm(y, z, g):
    v = y * jax.nn.silu(z)
    half = v.shape[-1] // SSD_GROUPS
    parts = []
    for k in range(SSD_GROUPS):
        vk = v[:, k * half:(k + 1) * half]
        parts.append(vk * lax.rsqrt(jnp.mean(vk * vk, axis=-1, keepdims=True) + EPS))
    return jnp.concatenate(parts, axis=-1) * g


def _ln_silu(v, g, b):
    mu = jnp.mean(v, axis=-1, keepdims=True)
    xc = v - mu
    y = xc * lax.rsqrt(jnp.mean(xc * xc, axis=-1, keepdims=True) + EPS) * g + b
    return jax.nn.silu(y)


def _acc_init(i, *refs):
    @pl.when(i == 0)
    def _():
        for r in refs:
            r[...] = jnp.zeros_like(r)


def _after_token(body, token):
    if token is None:
        return body, [], []

    def body_after(tok_ref, *refs):
        del tok_ref
        body(*refs)

    return body_after, [token], [pl.BlockSpec(memory_space=pl.ANY)]


def _row_spec(tm, n):
    return pl.BlockSpec((tm, n), lambda i: (i, 0))


def _const_spec(shape):
    nd = len(shape)
    return pl.BlockSpec(shape, lambda i: (0,) * nd)


def _prev_halo_spec(tm, halo, n):
    return pl.BlockSpec((halo, n), lambda i: (jnp.maximum(i * (tm // halo) - 1, 0), 0))


def _next_halo_spec(tm, halo, n, t):
    return pl.BlockSpec((halo, n), lambda i: (jnp.minimum((i + 1) * (tm // halo), t // halo - 1), 0))


def _in_proj(x, g1, w_in_p, token=None):
    t = x.shape[0]
    tm = min(TOKEN_TILE, t)

    def body(x_ref, g_ref, w_ref, u_ref, z_ref, xbc_ref, dt_ref, cv_ref, cg_ref):
        u = _rms(x_ref[...], g_ref[...]).astype(bf16)
        u_ref[...] = u
        for (lo, hi), o_ref in zip(IN_SEGS, (z_ref, xbc_ref, dt_ref, cv_ref, cg_ref)):
            o_ref[...] = _nt(u, w_ref[lo:hi, :])

    widths = [hi - lo for lo, hi in IN_SEGS]
    outs = [jax.ShapeDtypeStruct((t, D_MODEL), bf16)] + [jax.ShapeDtypeStruct((t, n), f32) for n in widths]
    body, tok, tok_spec = _after_token(body, token)
    return pl.pallas_call(
        body, name="in_proj", grid=(t // tm,),
        in_specs=tok_spec + [_row_spec(tm, D_MODEL), _const_spec((1, D_MODEL)), _const_spec((IN_WIDTH, D_MODEL))],
        out_specs=[_row_spec(tm, D_MODEL)] + [_row_spec(tm, n) for n in widths],
        out_shape=outs, compiler_params=_params("arbitrary"),
    )(*tok, x, g1, w_in_p)


ROW_BLOCK = 16
LANE_CHUNK = 512


def _blocks(rows, cols):
    return [(slice(r, r + ROW_BLOCK), slice(c, c + LANE_CHUNK))
            for c in range(0, cols, LANE_CHUNK) for r in range(0, rows, ROW_BLOCK)]


def _conv4_block(ext_ref, w_ref, b_ref, rs, ls):
    acc = b_ref[:, ls] + w_ref[0:1, ls] * ext_ref[rs.start + 8 - 3:rs.stop + 8 - 3, ls]
    for k in range(1, SSD_CONV):
        acc = acc + w_ref[k:k + 1, ls] * ext_ref[rs.start + 8 - 3 + k:rs.stop + 8 - 3 + k, ls]
    return acc


def _ssd_pre(xbc_raw, dt_raw, cw, cb, dt_bias):
    t = xbc_raw.shape[0]
    tm = min(TOKEN_TILE, t)

    def body(cur_ref, halo_ref, dt_ref, w_ref, b_ref, dtb_ref, act_ref, dto_ref, ext_ref):
        i = pl.program_id(0)
        ext_ref[0:8, :] = jnp.where(i == 0, 0.0, halo_ref[...])
        ext_ref[8:, :] = cur_ref[...]
        for rs, ls in _blocks(tm, XBC_WIDTH):
            act_ref[rs, ls] = jax.nn.silu(_conv4_block(ext_ref, w_ref, b_ref, rs, ls))
        dto_ref[...] = jax.nn.softplus(dt_ref[...] + dtb_ref[...])

    return pl.pallas_call(
        body, name="ssd_pre", grid=(t // tm,),
        in_specs=[_row_spec(tm, XBC_WIDTH), _prev_halo_spec(tm, 8, XBC_WIDTH), _row_spec(tm, SSD_HEADS),
                  _const_spec((8, XBC_WIDTH)), _const_spec((1, XBC_WIDTH)), _const_spec((1, SSD_HEADS))],
        out_specs=[_row_spec(tm, XBC_WIDTH), _row_spec(tm, SSD_HEADS)],
        out_shape=(jax.ShapeDtypeStruct((t, XBC_WIDTH), f32), jax.ShapeDtypeStruct((t, SSD_HEADS), f32)),
        scratch_shapes=[pltpu.VMEM((tm + 8, XBC_WIDTH), f32)],
        compiler_params=_params("arbitrary"),
    )(xbc_raw, xbc_raw, dt_raw, cw, cb, dt_bias)


def _ssd_consts():
    r = lax.broadcasted_iota(jnp.int32, (CHUNK, CHUNK), 0)
    c = lax.broadcasted_iota(jnp.int32, (CHUNK, CHUNK), 1)
    causal = r >= c
    tril = causal.astype(f32)
    triu = (r <= c).astype(f32)
    hh = lax.broadcasted_iota(jnp.int32, (SSD_HEADS, D_MODEL), 0)
    jj = lax.broadcasted_iota(jnp.int32, (SSD_HEADS, D_MODEL), 1)
    expand = (lax.shift_right_logical(jj, 6) == hh).astype(f32)
    lane = lax.broadcasted_iota(jnp.int32, (1, 2 * SSD_HEAD_DIM), 1)
    m0 = (lane < SSD_HEAD_DIM).astype(f32)
    return causal, tril, triu, expand, m0, 1.0 - m0


def _ssd_chunk(xs, bm, cm, dt, s_prev, a_log, d_skip, consts):
    causal, tril, triu, expand, m0, m1 = consts
    a = dt * (-jnp.exp(a_log))
    cs = _dot01(tril, a, "left", CUMSUM_PASSES)
    cs_t = _dot01(triu, a, "tn", CUMSUM_PASSES)
    cs_last = cs[CHUNK - 1:CHUNK, :]
    per_head = jnp.concatenate([dt, jnp.exp(cs_last - cs), jnp.exp(cs), jnp.broadcast_to(jnp.exp(cs_last), (8, SSD_HEADS)),
                                jnp.broadcast_to(d_skip, (8, SSD_HEADS))], axis=0)
    per_channel = _dot01(expand, per_head, "right", EXPAND_PASSES)
    dt_e, dec_end, dec_start = (per_channel[i * CHUNK:(i + 1) * CHUNK] for i in range(3))
    chunk_dec = per_channel[3 * CHUNK:3 * CHUNK + 1]
    d_e = per_channel[3 * CHUNK + 8:3 * CHUNK + 9]
    xc = xs * dt_e
    x_dec = xc * dec_end
    gw = D_MODEL // SSD_GROUPS
    ys, states = [], []
    for g in range(SSD_GROUPS):
        bg = bm[:, g * SSD_STATE:(g + 1) * SSD_STATE]
        cg = cm[:, g * SSD_STATE:(g + 1) * SSD_STATE]
        sp = s_prev[:, g * gw:(g + 1) * gw]
        states.append(sp * chunk_dec[:, g * gw:(g + 1) * gw] + btn(bg, x_dec[:, g * gw:(g + 1) * gw]))
        y_off = bnn(cg, sp) * dec_start[:, g * gw:(g + 1) * gw]
        scores = bnt(cg, bg)
        pieces = []
        for pr in range(gw // (2 * SSD_HEAD_DIM)):
            lo = g * gw + pr * 2 * SSD_HEAD_DIM
            xp = xc[:, lo:lo + 2 * SSD_HEAD_DIM]
            mats = []
            for h in (lo // SSD_HEAD_DIM, lo // SSD_HEAD_DIM + 1):
                seg = cs[:, h:h + 1] - cs_t[h:h + 1, :]
                mats.append(scores * jnp.exp(jnp.where(causal, seg, -jnp.inf)))
            pieces.append(bnn(jnp.concatenate(mats, axis=1), jnp.concatenate([xp * m0, xp * m1], axis=0)))
        ys.append(jnp.concatenate(pieces, axis=-1) + y_off)
    y = jnp.concatenate(ys, axis=-1) + xs * d_e
    return y, jnp.concatenate(states, axis=-1)


def _ssd_scan(xbc_act, dt, a_log, d_skip):
    t = xbc_act.shape[0]
    nc = t // CHUNK
    rows = SCAN_CHUNKS * CHUNK

    def body(xbc_ref, dt_ref, al_ref, dk_ref, y_ref, sp_ref, state_ref):
        c = pl.program_id(0)

        @pl.when(c == 0)
        def _():
            state_ref[...] = jnp.zeros_like(state_ref)

        state = state_ref[...]
        consts = _ssd_consts()
        for j in range(SCAN_CHUNKS):
            rs = slice(j * CHUNK, (j + 1) * CHUNK)
            sp_ref[j] = state
            y_ref[rs, :], state = _ssd_chunk(xbc_ref[rs, 0:1024], xbc_ref[rs, 1024:1280], xbc_ref[rs, 1280:1536],
                                             dt_ref[rs, :], state, al_ref[...], dk_ref[...], consts)
        state_ref[...] = state

    return pl.pallas_call(
        body, name="ssd_scan", grid=(nc // SCAN_CHUNKS,),
        in_specs=[_row_spec(rows, XBC_WIDTH), _row_spec(rows, SSD_HEADS), _const_spec((1, SSD_HEADS)),
                  _const_spec((1, SSD_HEADS))],
        out_specs=[_row_spec(rows, D_MODEL), pl.BlockSpec((SCAN_CHUNKS, SSD_STATE, D_MODEL), lambda c: (c, 0, 0))],
        out_shape=(jax.ShapeDtypeStruct((t, D_MODEL), f32), jax.ShapeDtypeStruct((nc, SSD_STATE, D_MODEL), f32)),
        scratch_shapes=[pltpu.VMEM((SSD_STATE, D_MODEL), f32)],
        compiler_params=_params("arbitrary"),
    )(xbc_act, dt, a_log, d_skip)


CONV_HALO = 32
CONV_BLOCK = 64
LANE_TILE = 128
SUBLANES = 8


def _fill_glu_ext(ext_ref, cv_ref, cg_ref, hcv_ref, hcg_ref, i):
    ext_ref[0:CONV_HALO, :] = jnp.where(i == 0, 0.0, hcv_ref[...] * jax.nn.sigmoid(hcg_ref[...]))
    ext_ref[CONV_HALO:, :] = cv_ref[...] * jax.nn.sigmoid(cg_ref[...])


def _shifted_copies(src_ref, dst_ref, rows):
    for s in range(1, SUBLANES):
        dst_ref[s, 0:rows, :] = src_ref[pl.ds(s, rows), :]


def _tap_source(src_ref, shifted_ref, offset):
    s = offset % SUBLANES
    return (src_ref if s == 0 else shifted_ref.at[s]), offset - s


def _conv_rows(src_ref, shifted_ref, taps, offsets, lanes, r0, init):
    accs = [init] * (CONV_BLOCK // SUBLANES)
    for k, off in enumerate(offsets):
        ref, base = _tap_source(src_ref, shifted_ref, off)
        for j in range(len(accs)):
            accs[j] = accs[j] + taps[k] * ref[pl.ds(r0 + base + SUBLANES * j, SUBLANES), lanes]
    return accs


def _conf_conv(cv, cg, w, b):
    t = cv.shape[0]
    tm = min(TOKEN_TILE, t)
    ext_rows = tm + CONV_HALO
    offsets = [CONV_HALO - (CONF_KERNEL - 1) + k for k in range(CONF_KERNEL)]

    def body(cv_ref, cg_ref, hcv_ref, hcg_ref, w_ref, b_ref, o_ref, ext_ref, shifted_ref):
        _fill_glu_ext(ext_ref, cv_ref, cg_ref, hcv_ref, hcg_ref, pl.program_id(0))
        _shifted_copies(ext_ref, shifted_ref, ext_rows - SUBLANES)
        for lb in range(1024 // LANE_TILE):
            lanes = slice(lb * LANE_TILE, (lb + 1) * LANE_TILE)
            taps = [jnp.broadcast_to(w_ref[k:k + 1, lanes], (SUBLANES, LANE_TILE)) for k in range(CONF_KERNEL)]
            bias = jnp.broadcast_to(b_ref[:, lanes], (SUBLANES, LANE_TILE))

            def rows(rb, carry, lanes=lanes, taps=taps, bias=bias):
                r0 = pl.multiple_of(rb * CONV_BLOCK, CONV_BLOCK)
                accs = _conv_rows(ext_ref, shifted_ref, taps, offsets, lanes, r0, bias)
                for j, a in enumerate(accs):
                    o_ref[pl.ds(r0 + SUBLANES * j, SUBLANES), lanes] = a
                return carry

            lax.fori_loop(0, tm // CONV_BLOCK, rows, 0)

    return pl.pallas_call(
        body, name="conf_conv", grid=(t // tm,),
        in_specs=[_row_spec(tm, 1024), _row_spec(tm, 1024), _prev_halo_spec(tm, CONV_HALO, 1024),
                  _prev_halo_spec(tm, CONV_HALO, 1024), _const_spec((32, 1024)), _const_spec((1, 1024))],
        out_specs=_row_spec(tm, 1024), out_shape=jax.ShapeDtypeStruct((t, 1024), f32),
        scratch_shapes=[pltpu.VMEM((ext_rows, 1024), f32), pltpu.VMEM((SUBLANES, ext_rows, 1024), f32)],
        compiler_params=_params("arbitrary"),
    )(cv, cg, cv, cg, w, b)


def _out_proj(y, z, v2, x, g_ssd, ln_g, ln_b, w_out, g2):
    t = x.shape[0]
    tm = min(TOKEN_TILE, t)

    def body(y_ref, z_ref, v_ref, x_ref, gs_ref, lg_ref, lb_ref, w_ref, g2_ref, ys_ref, yc_ref, h1_ref, u2_ref):
        ys = _gated_norm(y_ref[...], z_ref[...], gs_ref[...]).astype(bf16)
        yc = _ln_silu(v_ref[...], lg_ref[...], lb_ref[...]).astype(bf16)
        ys_ref[...] = ys
        yc_ref[...] = yc
        h1 = x_ref[...] + _nn(ys, w_ref[0:1024, :]) + _nn(yc, w_ref[1024:2048, :])
        h1_ref[...] = h1
        u2_ref[...] = _rms(h1, g2_ref[...]).astype(bf16)

    vec = _const_spec((1, 1024))
    row = _row_spec(tm, 1024)
    return pl.pallas_call(
        body, name="out_proj", grid=(t // tm,),
        in_specs=[row, row, row, row, vec, vec, vec, _const_spec((2048, 1024)), vec],
        out_specs=[row, row, row, row],
        out_shape=(jax.ShapeDtypeStruct((t, 1024), bf16), jax.ShapeDtypeStruct((t, 1024), bf16),
                   jax.ShapeDtypeStruct((t, 1024), f32), jax.ShapeDtypeStruct((t, 1024), bf16)),
        compiler_params=_params("arbitrary"),
    )(y, z, v2, x, g_ssd, ln_g, ln_b, w_out, g2)


def _mlp_up(u2, w_up):
    t = u2.shape[0]
    tm = min(TOKEN_TILE, t)
    blk = D_FF // N_DEV

    def body(u_ref, w_ref, pre_ref, hs_ref):
        u = u_ref[...]
        for k in range(N_DEV):
            pre = _nn(u, w_ref[k])
            pre_ref[:, k * blk:(k + 1) * blk] = pre.astype(bf16)
            r = jnp.maximum(pre, 0.0)
            hs_ref[:, k * blk:(k + 1) * blk] = (r * r).astype(bf16)

    return pl.pallas_call(
        body, name="mlp_up", grid=(t // tm,),
        in_specs=[_row_spec(tm, D_MODEL), _const_spec((N_DEV, D_MODEL, D_FF // N_DEV))],
        out_specs=[_row_spec(tm, D_FF), _row_spec(tm, D_FF)],
        out_shape=(jax.ShapeDtypeStruct((t, D_FF), bf16), jax.ShapeDtypeStruct((t, D_FF), bf16)),
        compiler_params=_params("arbitrary"),
    )(u2, w_up)


def _mlp_down(h1, hs, w_down):
    t = h1.shape[0]
    tm = min(TOKEN_TILE, t)

    def body(h_ref, hs_ref, w_ref, o_ref):
        o_ref[...] = h_ref[...] + _nn(hs_ref[...], w_ref[...])

    return pl.pallas_call(
        body, name="mlp_down", grid=(t // tm,),
        in_specs=[_row_spec(tm, D_MODEL), _row_spec(tm, D_FF), _const_spec((D_FF, D_MODEL))],
        out_specs=_row_spec(tm, D_MODEL), out_shape=jax.ShapeDtypeStruct((t, D_MODEL), f32),
        compiler_params=_params("arbitrary"),
    )(h1, hs, w_down)


TAIL_LOSS, TAIL_FINAL_G, TAIL_PLE_G, TAIL_GATE_B, TAIL_GATE_NORM_G = 0, 1, 2, 3, 4


def _tail(h2, p, target, g3, w_gate, b_gate, w_ple, ple_g, fin_g):
    t = h2.shape[0]
    tm = min(TOKEN_TILE, t)

    def body(h_ref, p_ref, t_ref, g3_ref, wg_ref, bg_ref, wp_ref, pg_ref, fg_ref,
             dh_ref, dhb_ref, dwg_ref, dwp_ref, acc_ref):
        i = pl.program_id(0)
        _acc_init(i, dwg_ref, dwp_ref, acc_ref)
        h2v = h_ref[...]
        tgt = t_ref[...]
        u3, vjp_u3 = jax.vjp(_rms, h2v, g3_ref[...])
        u3b = u3.astype(bf16)
        pb = p_ref[...].astype(bf16)
        gate_pre = _nn(u3b, wg_ref[...]) + bg_ref[...]
        emb_pre = _nn(pb, wp_ref[...])

        def tail_fn(hv, gp, ep, pg, fg):
            h3 = hv + jax.nn.sigmoid(gp) * _rms(ep, pg)
            err = _rms(h3, fg) - tgt
            return 0.5 * jnp.mean(err * err, axis=-1, keepdims=True)

        loss_tok, vjp_tail = jax.vjp(tail_fn, h2v, gate_pre, emb_pre, pg_ref[...], fg_ref[...])
        dh_a, dgp, dep, dpg, dfg = vjp_tail(jnp.ones_like(loss_tok))
        dgpb = dgp.astype(bf16)
        dh_b, dg3 = vjp_u3(_nt(dgpb, wg_ref[...]))
        dh = dh_a + dh_b
        dh_ref[...] = dh
        dhb_ref[...] = dh.astype(bf16)
        dwg_ref[...] += _tn(u3b, dgpb)
        dwp_ref[...] += _tn(pb, dep.astype(bf16))
        acc_ref[TAIL_LOSS:TAIL_LOSS + 1, :] += jnp.broadcast_to(jnp.sum(loss_tok, axis=0, keepdims=True), (1, 1024))
        acc_ref[TAIL_FINAL_G:TAIL_FINAL_G + 1, :] += dfg
        acc_ref[TAIL_PLE_G:TAIL_PLE_G + 1, :] += dpg
        acc_ref[TAIL_GATE_B:TAIL_GATE_B + 1, :] += jnp.sum(dgp, axis=0, keepdims=True)
        acc_ref[TAIL_GATE_NORM_G:TAIL_GATE_NORM_G + 1, :] += dg3

    vec = _const_spec((1, 1024))
    row = _row_spec(tm, 1024)
    return pl.pallas_call(
        body, name="tail", grid=(t // tm,),
        in_specs=[row, _row_spec(tm, PLE_DIM), row, vec, _const_spec((1024, 1024)), vec, _const_spec((PLE_DIM, 1024)), vec, vec],
        out_specs=[row, row, _const_spec((1024, 1024)), _const_spec((PLE_DIM, 1024)), _const_spec((8, 1024))],
        out_shape=(jax.ShapeDtypeStruct((t, 1024), f32), jax.ShapeDtypeStruct((t, 1024), bf16),
                   jax.ShapeDtypeStruct((1024, 1024), f32), jax.ShapeDtypeStruct((PLE_DIM, 1024), f32),
                   jax.ShapeDtypeStruct((8, 1024), f32)),
        compiler_params=_params("arbitrary"),
    )(h2, p, target, g3, w_gate, b_gate, w_ple, ple_g, fin_g)


def _mlp_bwd(dh2, dh2b, pre, h1, g2, w_down, w_up, token=None):
    t = dh2.shape[0]
    tm = min(TOKEN_TILE, t)
    blk = D_FF // N_DEV

    def body(dh_ref, dhb_ref, pre_ref, h1_ref, g_ref, wd_hbm, wu_hbm, dpre_ref, dh1_ref, dh1b_ref, acc_ref,
             wd_ref, wu_ref):
        i = pl.program_id(0)
        _acc_init(i, acc_ref)

        @pl.when(i == 0)
        def _():
            pltpu.sync_copy(wd_hbm, wd_ref)
            pltpu.sync_copy(wu_hbm, wu_ref)

        dhb = dhb_ref[...]
        du2 = jnp.zeros((tm, D_MODEL), f32)
        for k in range(D_FF // blk):
            dhs = _nt(dhb, wd_ref[k * blk:(k + 1) * blk, :])
            dpre = (dhs * (2.0 * jnp.maximum(pre_ref[:, k * blk:(k + 1) * blk].astype(f32), 0.0))).astype(bf16)
            dpre_ref[:, k * blk:(k + 1) * blk] = dpre
            du2 = du2 + _nt(dpre, wu_ref[k])
        _, vjp_u2 = jax.vjp(_rms, h1_ref[...], g_ref[...])
        d, dg = vjp_u2(du2)
        dh1 = dh_ref[...] + d
        dh1_ref[...] = dh1
        dh1b_ref[...] = dh1.astype(bf16)
        acc_ref[0:1, :] += dg

    row = _row_spec(tm, 1024)
    body, tok, tok_spec = _after_token(body, token)
    return pl.pallas_call(
        body, name="mlp_bwd", grid=(t // tm,),
        in_specs=tok_spec + [row, row, _row_spec(tm, D_FF), row, _const_spec((1, 1024)), HBM_SPEC, HBM_SPEC],
        out_specs=[_row_spec(tm, D_FF), row, row, _const_spec((8, 1024))],
        out_shape=(jax.ShapeDtypeStruct((t, D_FF), bf16), jax.ShapeDtypeStruct((t, 1024), f32),
                   jax.ShapeDtypeStruct((t, 1024), bf16), jax.ShapeDtypeStruct((8, 1024), f32)),
        scratch_shapes=[pltpu.VMEM((D_FF, D_MODEL), bf16), pltpu.VMEM((N_DEV, D_MODEL, D_FF // N_DEV), bf16)],
        compiler_params=_params("arbitrary"),
    )(*tok, dh2, dh2b, pre, h1, g2, w_down, w_up)


OPB_SSD_G, OPB_LN_G, OPB_LN_B, OPB_CONV_B = 0, 1, 2, 3


def _out_proj_bwd(dh1b, y, z, v2, g_ssd, ln_g, ln_b, w_out, token=None):
    t = y.shape[0]
    tm = min(TOKEN_TILE, t)

    def body(dh_ref, y_ref, z_ref, v_ref, gs_ref, lg_ref, lb_ref, w_ref, dy_ref, dz_ref, dv_ref, acc_ref):
        i = pl.program_id(0)
        _acc_init(i, acc_ref)
        dhb = dh_ref[...]
        dys = _nt(dhb, w_ref[0:1024, :])
        dyc = _nt(dhb, w_ref[1024:2048, :])
        _, vjp_g = jax.vjp(_gated_norm, y_ref[...], z_ref[...], gs_ref[...])
        dy, dz, dgs = vjp_g(dys)
        _, vjp_l = jax.vjp(_ln_silu, v_ref[...], lg_ref[...], lb_ref[...])
        dv, dlg, dlb = vjp_l(dyc)
        dy_ref[...] = dy
        dz_ref[...] = dz.astype(bf16)
        dv_ref[...] = dv
        acc_ref[OPB_SSD_G:OPB_SSD_G + 1, :] += dgs
        acc_ref[OPB_LN_G:OPB_LN_G + 1, :] += dlg
        acc_ref[OPB_LN_B:OPB_LN_B + 1, :] += dlb
        acc_ref[OPB_CONV_B:OPB_CONV_B + 1, :] += jnp.sum(dv, axis=0, keepdims=True)

    vec = _const_spec((1, 1024))
    row = _row_spec(tm, 1024)
    body, tok, tok_spec = _after_token(body, token)
    return pl.pallas_call(
        body, name="out_proj_bwd", grid=(t // tm,),
        in_specs=tok_spec + [row, row, row, row, vec, vec, vec, _const_spec((2048, 1024))],
        out_specs=[row, row, row, _const_spec((8, 1024))],
        out_shape=(jax.ShapeDtypeStruct((t, 1024), f32), jax.ShapeDtypeStruct((t, 1024), bf16),
                   jax.ShapeDtypeStruct((t, 1024), f32), jax.ShapeDtypeStruct((8, 1024), f32)),
        compiler_params=_params("arbitrary"),
    )(*tok, dh1b, y, z, v2, g_ssd, ln_g, ln_b, w_out)


def _conf_conv_bwd(dv2, cv, cg, w):
    t = cv.shape[0]
    tm = min(TOKEN_TILE, t)
    ext_rows = tm + CONV_HALO
    offsets = [CONF_KERNEL - 1 - k for k in range(CONF_KERNEL)]

    def body(dv_ref, dvn_ref, cv_ref, cg_ref, w_ref, dcv_ref, dcg_ref, dw_ref, glu_ref, dext_ref, sg_ref, shifted_ref):
        i = pl.program_id(0)
        _acc_init(i, dw_ref)
        sg = jax.nn.sigmoid(cg_ref[...])
        sg_ref[...] = sg
        glu_ref[...] = cv_ref[...] * sg
        dext_ref[0:tm, :] = dv_ref[...]
        dext_ref[tm:, :] = jnp.where(i == pl.num_programs(0) - 1, 0.0, dvn_ref[...])
        _shifted_copies(dext_ref, shifted_ref, ext_rows - SUBLANES)

        for lb in range(1024 // LANE_TILE):
            lanes = slice(lb * LANE_TILE, (lb + 1) * LANE_TILE)

            def rows_w(rb, accs, lanes=lanes):
                r0 = pl.multiple_of(rb * CONV_BLOCK, CONV_BLOCK)
                accs = list(accs)
                for j in range(CONV_BLOCK // SUBLANES):
                    x = glu_ref[pl.ds(r0 + SUBLANES * j, SUBLANES), lanes]
                    for k, off in enumerate(offsets):
                        ref, base = _tap_source(dext_ref, shifted_ref, off)
                        accs[k] = accs[k] + x * ref[pl.ds(r0 + base + SUBLANES * j, SUBLANES), lanes]
                return tuple(accs)

            zero = jnp.zeros((SUBLANES, LANE_TILE), f32)
            accs = lax.fori_loop(0, tm // CONV_BLOCK, rows_w, (zero,) * CONF_KERNEL)
            for k in range(CONF_KERNEL):
                dw_ref[k:k + 1, lanes] += jnp.sum(accs[k], axis=0, keepdims=True)

        for lb in range(1024 // LANE_TILE):
            lanes = slice(lb * LANE_TILE, (lb + 1) * LANE_TILE)
            taps = [jnp.broadcast_to(w_ref[k:k + 1, lanes], (SUBLANES, LANE_TILE)) for k in range(CONF_KERNEL)]

            def rows_x(rb, carry, lanes=lanes, taps=taps):
                r0 = pl.multiple_of(rb * CONV_BLOCK, CONV_BLOCK)
                zero = jnp.zeros((SUBLANES, LANE_TILE), f32)
                dglu = jnp.concatenate(_conv_rows(dext_ref, shifted_ref, taps, offsets, lanes, r0, zero), axis=0)
                sg = sg_ref[pl.ds(r0, CONV_BLOCK), lanes]
                cvv = cv_ref[pl.ds(r0, CONV_BLOCK), lanes]
                dcv_ref[pl.ds(r0, CONV_BLOCK), lanes] = (dglu * sg).astype(bf16)
                dcg_ref[pl.ds(r0, CONV_BLOCK), lanes] = (dglu * cvv * sg * (1.0 - sg)).astype(bf16)
                return carry

            lax.fori_loop(0, tm // CONV_BLOCK, rows_x, 0)

    row = _row_spec(tm, 1024)
    return pl.pallas_call(
        body, name="conf_conv_bwd", grid=(t // tm,),
        in_specs=[row, _next_halo_spec(tm, CONV_HALO, 1024, t), row, row, _const_spec((32, 1024))],
        out_specs=[row, row, _const_spec((32, 1024))],
        out_shape=(jax.ShapeDtypeStruct((t, 1024), bf16), jax.ShapeDtypeStruct((t, 1024), bf16),
                   jax.ShapeDtypeStruct((32, 1024), f32)),
        scratch_shapes=[pltpu.VMEM((tm, 1024), f32), pltpu.VMEM((ext_rows, 1024), f32), pltpu.VMEM((tm, 1024), f32),
                        pltpu.VMEM((SUBLANES, ext_rows, 1024), f32)],
        compiler_params=_params("arbitrary"),
    )(dv2, dv2, cv, cg, w)


def _ssd_scan_bwd(xbc_act, dt, s_prev, dy, a_log, d_skip):
    t = xbc_act.shape[0]
    steps = t // CHUNK // SCAN_CHUNKS
    rows = SCAN_CHUNKS * CHUNK

    def body(xbc_ref, dt_ref, sp_ref, dy_ref, al_ref, dk_ref, dxbc_ref, ddt_ref, dal_ref, ddk_ref, ds_ref):
        i = pl.program_id(0)
        _acc_init(i, ds_ref, dal_ref, ddk_ref)
        consts = _ssd_consts()
        ds = ds_ref[...]
        dal_sum, ddk_sum = dal_ref[...], ddk_ref[...]
        for j in reversed(range(SCAN_CHUNKS)):
            rs = slice(j * CHUNK, (j + 1) * CHUNK)
            _, vjp_c = jax.vjp(
                functools.partial(_ssd_chunk, consts=consts),
                xbc_ref[rs, 0:1024], xbc_ref[rs, 1024:1280], xbc_ref[rs, 1280:1536], dt_ref[rs, :], sp_ref[j],
                al_ref[...], dk_ref[...])
            dxs, dbm, dcm, ddt, ds, dal, ddk = vjp_c((dy_ref[rs, :], ds))
            dxbc_ref[rs, 0:1024] = dxs
            dxbc_ref[rs, 1024:1280] = dbm
            dxbc_ref[rs, 1280:1536] = dcm
            ddt_ref[rs, :] = ddt
            dal_sum, ddk_sum = dal_sum + dal, ddk_sum + ddk
        ds_ref[...] = ds
        dal_ref[...] = dal_sum
        ddk_ref[...] = ddk_sum

    rev = lambda i: (steps - 1 - i, 0)
    return pl.pallas_call(
        body, name="ssd_scan_bwd", grid=(steps,),
        in_specs=[pl.BlockSpec((rows, XBC_WIDTH), rev), pl.BlockSpec((rows, SSD_HEADS), rev),
                  pl.BlockSpec((SCAN_CHUNKS, SSD_STATE, D_MODEL), lambda i: (steps - 1 - i, 0, 0)),
                  pl.BlockSpec((rows, D_MODEL), rev), _const_spec((1, SSD_HEADS)), _const_spec((1, SSD_HEADS))],
        out_specs=[pl.BlockSpec((rows, XBC_WIDTH), rev), pl.BlockSpec((rows, SSD_HEADS), rev),
                   _const_spec((1, SSD_HEADS)), _const_spec((1, SSD_HEADS))],
        out_shape=(jax.ShapeDtypeStruct((t, XBC_WIDTH), f32), jax.ShapeDtypeStruct((t, SSD_HEADS), f32),
                   jax.ShapeDtypeStruct((1, SSD_HEADS), f32), jax.ShapeDtypeStruct((1, SSD_HEADS), f32)),
        scratch_shapes=[pltpu.VMEM((SSD_STATE, D_MODEL), f32)],
        compiler_params=_params("arbitrary"),
    )(xbc_act, dt, s_prev, dy, a_log, d_skip)


def _ssd_pre_bwd(xbc_raw, dxbc_act, ddt, dt_raw, cw, cb, dt_bias):
    t = xbc_raw.shape[0]
    tm = min(TOKEN_TILE, t)

    def body(cur_ref, halo_ref, dact_ref, ddt_ref, dtr_ref, w_ref, b_ref, dtb_ref,
             dco_ref, ddtr_ref, dw_ref, db_ref, ddtb_ref, ext_ref):
        i = pl.program_id(0)
        _acc_init(i, dw_ref, db_ref, ddtb_ref)
        ext_ref[0:8, :] = jnp.where(i == 0, 0.0, halo_ref[...])
        ext_ref[8:, :] = cur_ref[...]
        fold = lambda a: a[0:8] + a[8:ROW_BLOCK]
        for c in range(0, XBC_WIDTH, LANE_CHUNK):
            ls = slice(c, c + LANE_CHUNK)
            part_b = jnp.zeros((8, LANE_CHUNK), f32)
            part_w = [jnp.zeros((8, LANE_CHUNK), f32)] * SSD_CONV
            for r in range(0, tm, ROW_BLOCK):
                rs = slice(r, r + ROW_BLOCK)
                co = _conv4_block(ext_ref, w_ref, b_ref, rs, ls)
                sg = jax.nn.sigmoid(co)
                dco = dact_ref[rs, ls] * sg * (1.0 + co * (1.0 - sg))
                dco_ref[rs, ls] = dco
                part_b = part_b + fold(dco)
                part_w = [pw + fold(dco * ext_ref[r + 8 - 3 + k:r + ROW_BLOCK + 8 - 3 + k, ls])
                          for k, pw in enumerate(part_w)]
            db_ref[:, ls] += jnp.sum(part_b, axis=0, keepdims=True)
            for k in range(SSD_CONV):
                dw_ref[k:k + 1, ls] += jnp.sum(part_w[k], axis=0, keepdims=True)
        ddtr = ddt_ref[...] * jax.nn.sigmoid(dtr_ref[...] + dtb_ref[...])
        ddtb_ref[...] += jnp.sum(ddtr, axis=0, keepdims=True)
        ddtr_ref[...] = ddtr.astype(bf16)

    return pl.pallas_call(
        body, name="ssd_pre_bwd", grid=(t // tm,),
        in_specs=[_row_spec(tm, XBC_WIDTH), _prev_halo_spec(tm, 8, XBC_WIDTH), _row_spec(tm, XBC_WIDTH),
                  _row_spec(tm, SSD_HEADS), _row_spec(tm, SSD_HEADS), _const_spec((8, XBC_WIDTH)),
                  _const_spec((1, XBC_WIDTH)), _const_spec((1, SSD_HEADS))],
        out_specs=[_row_spec(tm, XBC_WIDTH), _row_spec(tm, SSD_HEADS), _const_spec((8, XBC_WIDTH)),
                   _const_spec((1, XBC_WIDTH)), _const_spec((1, SSD_HEADS))],
        out_shape=(jax.ShapeDtypeStruct((t, XBC_WIDTH), f32), jax.ShapeDtypeStruct((t, SSD_HEADS), bf16),
                   jax.ShapeDtypeStruct((8, XBC_WIDTH), f32), jax.ShapeDtypeStruct((1, XBC_WIDTH), f32),
                   jax.ShapeDtypeStruct((1, SSD_HEADS), f32)),
        scratch_shapes=[pltpu.VMEM((tm + 8, XBC_WIDTH), f32)],
        compiler_params=_params("arbitrary"),
    )(xbc_raw, xbc_raw, dxbc_act, ddt, dt_raw, cw, cb, dt_bias)


def _conv4_bwd_data(dco, cw):
    t = dco.shape[0]
    tm = min(TOKEN_TILE, t)

    def body(cur_ref, nxt_ref, w_ref, o_ref, ext_ref):
        i = pl.program_id(0)
        ext_ref[0:tm, :] = cur_ref[...]
        ext_ref[tm:, :] = jnp.where(i == pl.num_programs(0) - 1, 0.0, nxt_ref[...])
        for rs, ls in _blocks(tm, XBC_WIDTH):
            acc = w_ref[0:1, ls] * ext_ref[rs.start + SSD_CONV - 1:rs.stop + SSD_CONV - 1, ls]
            for k in range(1, SSD_CONV):
                acc = acc + w_ref[k:k + 1, ls] * ext_ref[rs.start + SSD_CONV - 1 - k:rs.stop + SSD_CONV - 1 - k, ls]
            o_ref[rs, ls] = acc.astype(bf16)

    return pl.pallas_call(
        body, name="conv4_bwd_data", grid=(t // tm,),
        in_specs=[_row_spec(tm, XBC_WIDTH), _next_halo_spec(tm, 8, XBC_WIDTH, t), _const_spec((8, XBC_WIDTH))],
        out_specs=_row_spec(tm, XBC_WIDTH), out_shape=jax.ShapeDtypeStruct((t, XBC_WIDTH), bf16),
        scratch_shapes=[pltpu.VMEM((tm + 8, XBC_WIDTH), f32)],
        compiler_params=_params("arbitrary"),
    )(dco, dco, cw)


def _in_proj_bwd(dproj, x, dh1, g1, w_in_p, token=None):
    t = x.shape[0]
    tm = min(TOKEN_TILE, t)

    def body(dz_ref, dxbc_ref, ddt_ref, dcv_ref, dcg_ref, x_ref, dh_ref, g_ref, w_ref, dx_ref, acc_ref):
        i = pl.program_id(0)
        _acc_init(i, acc_ref)
        du = jnp.zeros((tm, D_MODEL), f32)
        for (lo, hi), r in zip(IN_SEGS, (dz_ref, dxbc_ref, ddt_ref, dcv_ref, dcg_ref)):
            du = du + _nn(r[...], w_ref[lo:hi, :])
        _, vjp_u = jax.vjp(_rms, x_ref[...], g_ref[...])
        d, dg = vjp_u(du)
        dx_ref[...] = dh_ref[...] + d
        acc_ref[0:1, :] += dg

    row = _row_spec(tm, 1024)
    body, tok, tok_spec = _after_token(body, token)
    return pl.pallas_call(
        body, name="in_proj_bwd", grid=(t // tm,),
        in_specs=tok_spec + [_row_spec(tm, hi - lo) for lo, hi in IN_SEGS] + [row, row, _const_spec((1, 1024)),
                                                                               _const_spec((IN_WIDTH, D_MODEL))],
        out_specs=[row, _const_spec((8, 1024))],
        out_shape=(jax.ShapeDtypeStruct((t, 1024), f32), jax.ShapeDtypeStruct((8, 1024), f32)),
        compiler_params=_params("arbitrary"),
    )(*tok, *dproj, x, dh1, g1, w_in_p)


def _mm_tn(a, b, tk, tn, name, column_blocks=False):
    t, kk = a.shape
    n = b.shape[1]
    tk, tn = min(tk, kk), min(tn, n)

    def body(a_ref, b_ref, o_ref):
        o_ref[...] = _tn(a_ref[...], b_ref[...]).astype(bf16)

    if column_blocks:
        assert tk == kk
        out_spec = pl.BlockSpec((None, tk, tn), lambda i, j: (j, 0, 0))
        out_shape = jax.ShapeDtypeStruct((n // tn, kk, tn), bf16)
    else:
        out_spec = pl.BlockSpec((tk, tn), lambda i, j: (i, j))
        out_shape = jax.ShapeDtypeStruct((kk, n), bf16)
    return pl.pallas_call(
        body, name=name, grid=(kk // tk, n // tn),
        in_specs=[pl.BlockSpec((t, tk), lambda i, j: (0, i)), pl.BlockSpec((t, tn), lambda i, j: (0, j))],
        out_specs=out_spec, out_shape=out_shape,
        compiler_params=_params("arbitrary", "arbitrary"),
    )(a, b)


class _LocalWeights:
    def __init__(self, w):
        self.w = w
        self.sent = {}

    def start_token(self):
        return None

    def weight(self, name, after=()):
        del after
        return self.w[name]

    def send_grads(self, grads):
        self.sent.update(grads)
        return None


def _local_step(x, p, target, s, comm):
    conv_w, conf_w = comm.weight("ssd_conv_w"), comm.weight("conf_dw_w")
    w_in = comm.weight("w_in")
    u, z, xbc_raw, dt_raw, cv, cg = _in_proj(x, s["mix_norm_g"], w_in, token=comm.start_token())
    xbc_act, dt = _ssd_pre(xbc_raw, dt_raw, conv_w, s["ssd_conv_b"], s["dt_bias"])
    y, s_prev = _ssd_scan(xbc_act, dt, s["A_log"], s["D_skip"])
    v2 = _conf_conv(cv, cg, conf_w, s["conf_dw_b"])
    w_out = comm.weight("w_out", after=(y, v2))
    ys, yc, h1, u2 = _out_proj(y, z, v2, x, s["ssd_norm_g"], s["conf_ln_g"], s["conf_ln_b"], w_out, s["mlp_norm_g"])
    w_up = comm.weight("w_up", after=(u2,))
    pre, hs = _mlp_up(u2, w_up)
    w_down = comm.weight("w_down", after=(hs,))
    h2 = _mlp_down(h1, hs, w_down)
    w_gate, w_ple = comm.weight("w_ple_gate", after=(h2,)), comm.weight("w_ple", after=(h2,))
    dh2, dh2b, dwg, dwp, tail_acc = _tail(h2, p, target, s["ple_gate_norm_g"], w_gate, s["b_ple_gate"], w_ple,
                                          s["ple_norm_g"], s["final_norm_g"])
    token = comm.send_grads({"w_ple_gate": dwg, "w_ple": dwp})
    dpre, dh1, dh1b, mlp_acc = _mlp_bwd(dh2, dh2b, pre, h1, s["mlp_norm_g"], w_down, w_up, token=token)
    token = comm.send_grads({
        "w_down": _mm_tn(hs, dh2b, 512, 1024, "dw_down"),
        "w_up": _mm_tn(u2, dpre, 1024, D_FF // N_DEV, "dw_up", column_blocks=True),
        "w_out": [_mm_tn(ys, dh1b, 1024, 512, "dw_out_ssd"), _mm_tn(yc, dh1b, 1024, 512, "dw_out_conf")],
    })
    dy, dz, dv2, opb_acc = _out_proj_bwd(dh1b, y, z, v2, s["ssd_norm_g"], s["conf_ln_g"], s["conf_ln_b"], w_out,
                                         token=token)
    dcv, dcg, dconf_w = _conf_conv_bwd(dv2, cv, cg, conf_w)
    dxbc_act, ddt, d_alog, d_dskip = _ssd_scan_bwd(xbc_act, dt, s_prev, dy, s["A_log"], s["D_skip"])
    dco, ddt_raw, dconv_w, dconv_b, d_dtb = _ssd_pre_bwd(xbc_raw, dxbc_act, ddt, dt_raw, conv_w, s["ssd_conv_b"],
                                                        s["dt_bias"])
    dxbc_raw = _conv4_bwd_data(dco, conv_w)
    dproj = (dz, dxbc_raw, ddt_raw, dcv, dcg)
    token = comm.send_grads({"w_in": [_mm_tn(d, u, 512, 1024, "dw_in_" + n)
                                      for n, d in zip(("z", "xbc", "dt", "cv", "cg"), dproj)]})
    grad_x, inp_acc = _in_proj_bwd(dproj, x, dh1, s["mix_norm_g"], w_in, token=token)
    acc = {"in_proj": inp_acc, "out_proj": opb_acc, "mlp": mlp_acc, "tail": tail_acc, "ssd_conv_b": dconv_b,
           "dt_bias": d_dtb, "A_log": d_alog, "D_skip": d_dskip, "ssd_conv_w": dconv_w, "conf_dw_w": dconf_w}
    return grad_x, acc


def _mesh_pos():
    return lax.axis_index("x"), lax.axis_index("y"), lax.axis_index("c")


def _other_chips(x, y):
    return [(1 - x, y), (x, 1 - y), (1 - x, 1 - y)]


def _all_gather(arrs, name):
    n = len(arrs)

    def body(*refs):
        ins, outs = refs[:n], refs[n:2 * n]
        send_sems, recv_sems, local_sems = refs[2 * n:]
        x, y, c = _mesh_pos()
        me = 4 * x + 2 * y + c
        sibling = (x, y, 1 - c)
        chips = _other_chips(x, y)

        def copy(a, k, block, to, src=None):
            dst = outs[a].at[block]
            return pltpu.make_async_remote_copy(
                src_ref=dst if src is None else src, dst_ref=dst, send_sem=send_sems.at[a, k],
                recv_sem=recv_sems.at[a, k], device_id=to, device_id_type=MESH_ID)

        mine = [pltpu.make_async_copy(ins[a], outs[a].at[me], local_sems.at[a]) for a in range(n)]
        for cp in mine:
            cp.start()
        first = []
        for a in range(n):
            first.append(copy(a, 0, me, sibling, src=ins[a]))
            first += [copy(a, 1 + j, me, (px, py, c), src=ins[a]) for j, (px, py) in enumerate(chips)]
        for cp in first:
            cp.start()
        passed = []
        for j, (px, py) in enumerate(chips):
            for a in range(n):
                blk = 4 * px + 2 * py + c
                copy(a, 1 + j, blk, (x, y, c)).wait_recv()
                cp = copy(a, 4 + j, blk, sibling)
                cp.start()
                passed.append(cp)
        for a in range(n):
            copy(a, 0, 4 * x + 2 * y + (1 - c), (x, y, c)).wait_recv()
        for j, (px, py) in enumerate(chips):
            for a in range(n):
                copy(a, 4 + j, 4 * px + 2 * py + (1 - c), (x, y, c)).wait_recv()
        for cp in first + passed:
            cp.wait_send()
        for cp in mine:
            cp.wait()

    return pl.pallas_call(
        body, name=name,
        in_specs=[HBM_SPEC] * n, out_specs=[HBM_SPEC] * n,
        out_shape=[jax.ShapeDtypeStruct((N_DEV,) + a.shape, a.dtype) for a in arrs],
        scratch_shapes=[pltpu.SemaphoreType.DMA((n, 7)), pltpu.SemaphoreType.DMA((n, 7)), pltpu.SemaphoreType.DMA((n,))],
    )(*arrs)


_PEER_FLIPS = ((0, 0, 1), (1, 0, 0), (0, 1, 0), (1, 1, 0), (1, 0, 1), (0, 1, 1), (1, 1, 1))
SEM_SPEC = pl.BlockSpec(memory_space=pltpu.SEMAPHORE)
ANY_SPEC = pl.BlockSpec(memory_space=pl.ANY)


def _flip(v, d):
    return 1 - v if d else v


def _peers(x, y, c):
    out = []
    for dx, dy, dc in _PEER_FLIPS:
        px, py, pc = _flip(x, dx), _flip(y, dy), _flip(c, dc)
        out.append(((px, py, pc), 4 * px + 2 * py + pc))
    return out


def _exchange_copy(src_ref, land_ref, send_sems, recv_sems, k, peer, peer_block, my_block, by_block, outgoing):
    src = src_ref.at[peer_block] if by_block else src_ref
    dst = land_ref.at[my_block if outgoing else peer_block]
    return pltpu.make_async_remote_copy(src_ref=src, dst_ref=dst, send_sem=send_sems.at[k], recv_sem=recv_sems.at[k],
                                        device_id=peer, device_id_type=MESH_ID)


def _exchange_start(srcs, by_block, name, after):
    n = len(srcs)
    x_pos, y_pos, c_pos = _mesh_pos()
    me_at = 4 * x_pos + 2 * y_pos + c_pos
    if by_block:
        lands = [lax.empty(a.shape, a.dtype) for a in srcs]
    else:
        lands = [lax.dynamic_update_slice(lax.empty((N_DEV,) + a.shape, a.dtype), a[None], (me_at,) + (0,) * a.ndim)
                 for a in srcs]

    def body(*refs):
        src_refs, land_refs = refs[1:1 + n], refs[1 + n:1 + 2 * n]
        outs = refs[1 + 2 * n:]
        send, recv, token = outs[:n], outs[n:2 * n], outs[4 * n]
        x, y, c = _mesh_pos()
        me = 4 * x + 2 * y + c
        for a in range(n):
            for k, (peer, blk) in enumerate(_peers(x, y, c)):
                _exchange_copy(src_refs[a], land_refs[a], send[a], recv[a], k, peer, blk, me, by_block, True).start()
        token[...] = jnp.zeros_like(token)

    sems = [pltpu.SemaphoreType.DMA((N_DEV - 1,))] * (2 * n)
    thru = [pltpu.HBM(a.shape, a.dtype) for a in list(srcs) + list(lands)]
    res = pl.pallas_call(
        body, name=name,
        in_specs=[ANY_SPEC] + [HBM_SPEC] * (2 * n),
        out_specs=[SEM_SPEC] * (2 * n) + [HBM_SPEC] * (2 * n) + [pl.BlockSpec(memory_space=pltpu.VMEM)],
        out_shape=sems + thru + [jax.ShapeDtypeStruct((8, 128), f32)],
        input_output_aliases={1 + i: 2 * n + i for i in range(2 * n)},
        compiler_params=pltpu.CompilerParams(has_side_effects=pltpu.SideEffectType.DATAFLOW_SIDE_EFFECTING),
    )(after, *[pltpu.with_memory_space_constraint(a, pltpu.HBM) for a in list(srcs) + list(lands)])
    states = [(res[2 * n + a], res[3 * n + a], res[a], res[n + a]) for a in range(n)]
    return states, res[4 * n]


def _exchange_wait(states, by_block, name, after):
    n, na = len(states), len(after)

    def body(*refs):
        src_refs, land_refs = refs[:n], refs[n:2 * n]
        send, recv = refs[2 * n:3 * n], refs[3 * n:4 * n]
        x, y, c = _mesh_pos()
        me = 4 * x + 2 * y + c
        for a in range(n):
            for k, (peer, blk) in enumerate(_peers(x, y, c)):
                _exchange_copy(src_refs[a], land_refs[a], send[a], recv[a], k, peer, blk, me, by_block, True).wait_send()
                _exchange_copy(src_refs[a], land_refs[a], send[a], recv[a], k, peer, blk, me, by_block, False).wait_recv()

    srcs, lands = [s[0] for s in states], [s[1] for s in states]
    res = pl.pallas_call(
        body, name=name,
        in_specs=[HBM_SPEC] * (2 * n) + [SEM_SPEC] * (2 * n) + [ANY_SPEC] * na,
        out_specs=[HBM_SPEC] * (2 * n),
        out_shape=[pltpu.HBM(a.shape, a.dtype) for a in srcs + lands],
        input_output_aliases={i: i for i in range(2 * n)},
        compiler_params=pltpu.CompilerParams(has_side_effects=pltpu.SideEffectType.DATAFLOW_SIDE_EFFECTING),
    )(*srcs, *lands, *[s[2] for s in states], *[s[3] for s in states], *after)
    return list(res[n:2 * n]), list(res[:n])


def _adamw_math(w, g, m, v):
    m = ADAM_B1 * m + (1.0 - ADAM_B1) * g
    v = ADAM_B2 * v + (1.0 - ADAM_B2) * (g * g)
    m_hat = m / (1.0 - ADAM_B1 ** ADAM_STEP)
    v_hat = v / (1.0 - ADAM_B2 ** ADAM_STEP)
    delta = -ADAM_LR * (m_hat / (jnp.sqrt(v_hat) + ADAM_EPS) + ADAM_WD * w)
    return delta, m, v


def _adamw_big(parts, mine, me, w, m, v, name):
    rows, cols = w.shape
    tr, tc = (256, cols) if rows % 256 == 0 else (rows, 256)

    def body(me_ref, p_ref, own_ref, w_ref, m_ref, v_ref, g_ref, d_ref, mo_ref, vo_ref):
        own = own_ref[...].astype(f32)
        g = None
        for j in range(N_DEV):
            part = jnp.where(me_ref[0] == j, own, p_ref[j].astype(f32))
            g = part if g is None else g + part
        d, mn, vn = _adamw_math(w_ref[...], g, m_ref[...], v_ref[...])
        g_ref[...] = g
        d_ref[...] = d
        mo_ref[...] = mn
        vo_ref[...] = vn

    tile = pl.BlockSpec((tr, tc), lambda i, j, me_ref: (i, j))
    shp = jax.ShapeDtypeStruct((rows, cols), f32)
    grid_spec = pltpu.PrefetchScalarGridSpec(
        num_scalar_prefetch=1, grid=(rows // tr, cols // tc),
        in_specs=[pl.BlockSpec((N_DEV, tr, tc), lambda i, j, me_ref: (0, i, j)),
                  pl.BlockSpec((None, tr, tc), lambda i, j, me_ref: (me_ref[0], i, j)), tile, tile, tile],
        out_specs=[tile, tile, tile, tile])
    return pl.pallas_call(
        body, name=name, grid_spec=grid_spec, out_shape=(shp, shp, shp, shp),
        compiler_params=_params("arbitrary", "arbitrary"),
    )(me, parts, mine, w, m, v)


PACK_ROWS = 56
_PACK_AT = {
    "mix_norm_g": (0, 0, 1024), "ssd_norm_g": (1, 0, 1024), "conf_ln_g": (2, 0, 1024), "conf_ln_b": (3, 0, 1024),
    "conf_dw_b": (4, 0, 1024), "mlp_norm_g": (5, 0, 1024), "final_norm_g": (7, 0, 1024), "ple_norm_g": (8, 0, 1024),
    "b_ple_gate": (9, 0, 1024), "ple_gate_norm_g": (10, 0, 1024), "dt_bias": (13, 0, 16), "A_log": (13, 128, 16),
    "D_skip": (13, 256, 16),
}
PACK_LOSS_ROW = 6
PACK_CONV_B_ROW = 11
PACK_CONV_W_ROW = 14
PACK_CONF_W_ROW = 24


def _pack_small(acc):
    def body(inp, opb, mlp, tail, cb, dtb, alog, dskip, cw, fw, o_ref):
        o_ref[...] = jnp.zeros_like(o_ref)
        rows = {"mix_norm_g": inp[0:1, :], "mlp_norm_g": mlp[0:1, :],
                "ssd_norm_g": opb[OPB_SSD_G:OPB_SSD_G + 1, :], "conf_ln_g": opb[OPB_LN_G:OPB_LN_G + 1, :],
                "conf_ln_b": opb[OPB_LN_B:OPB_LN_B + 1, :], "conf_dw_b": opb[OPB_CONV_B:OPB_CONV_B + 1, :],
                "final_norm_g": tail[TAIL_FINAL_G:TAIL_FINAL_G + 1, :], "ple_norm_g": tail[TAIL_PLE_G:TAIL_PLE_G + 1, :],
                "b_ple_gate": tail[TAIL_GATE_B:TAIL_GATE_B + 1, :],
                "ple_gate_norm_g": tail[TAIL_GATE_NORM_G:TAIL_GATE_NORM_G + 1, :],
                "dt_bias": dtb[...], "A_log": alog[...], "D_skip": dskip[...]}
        for name, val in rows.items():
            r, lo, width = _PACK_AT[name]
            o_ref[r:r + 1, lo:lo + width] = val
        o_ref[PACK_LOSS_ROW:PACK_LOSS_ROW + 1, :] = tail[TAIL_LOSS:TAIL_LOSS + 1, :]
        o_ref[PACK_CONV_B_ROW:PACK_CONV_B_ROW + 1, :] = cb[:, 0:1024]
        o_ref[PACK_CONV_B_ROW + 1:PACK_CONV_B_ROW + 2, 0:512] = cb[:, 1024:XBC_WIDTH]
        for k in range(SSD_CONV):
            o_ref[PACK_CONV_W_ROW + k:PACK_CONV_W_ROW + k + 1, :] = cw[k:k + 1, 0:1024]
            o_ref[PACK_CONV_W_ROW + SSD_CONV + k:PACK_CONV_W_ROW + SSD_CONV + k + 1, 0:512] = cw[k:k + 1, 1024:XBC_WIDTH]
        o_ref[PACK_CONF_W_ROW:PACK_CONF_W_ROW + 32, :] = fw[...]

    return pl.pallas_call(body, name="pack_small", out_shape=jax.ShapeDtypeStruct((PACK_ROWS, 1024), f32))(
        acc["in_proj"], acc["out_proj"], acc["mlp"], acc["tail"], acc["ssd_conv_b"], acc["dt_bias"], acc["A_log"],
        acc["D_skip"], acc["ssd_conv_w"], acc["conf_dw_w"])


def _small_update(all_small, w, m, v):
    names = _REPLICATED

    def body(all_ref, *refs):
        ins, outs = refs[:3 * len(names)], refs[3 * len(names):]
        s = all_ref[0]
        for j in range(1, N_DEV):
            s = s + all_ref[j]
        outs[0][...] = s[PACK_LOSS_ROW:PACK_LOSS_ROW + 1, 0:1]
        outs[1][...] = jnp.concatenate([s[PACK_CONV_W_ROW:PACK_CONV_W_ROW + SSD_CONV, :],
                                        s[PACK_CONV_W_ROW + SSD_CONV:PACK_CONV_W_ROW + 2 * SSD_CONV, 0:512]], axis=1)
        outs[2][...] = s[PACK_CONF_W_ROW:PACK_CONF_W_ROW + CONF_KERNEL, :]
        for i, name in enumerate(names):
            if name == "ssd_conv_b":
                g = jnp.concatenate([s[PACK_CONV_B_ROW:PACK_CONV_B_ROW + 1, :],
                                     s[PACK_CONV_B_ROW + 1:PACK_CONV_B_ROW + 2, 0:512]], axis=1)
            else:
                r, lo, width = _PACK_AT[name]
                g = s[r:r + 1, lo:lo + width]
            d, mn, vn = _adamw_math(ins[3 * i][...], g, ins[3 * i + 1][...], ins[3 * i + 2][...])
            for o_ref, val in zip(outs[3 + 4 * i:7 + 4 * i], (g, d, mn, vn)):
                o_ref[...] = val

    shapes = [jax.ShapeDtypeStruct((1, 1), f32), jax.ShapeDtypeStruct((SSD_CONV, XBC_WIDTH), f32),
              jax.ShapeDtypeStruct((CONF_KERNEL, D_MODEL), f32)]
    operands = []
    for name in names:
        operands += [w[name], m[name], v[name]]
        shapes += [jax.ShapeDtypeStruct(w[name].shape, f32)] * 4
    res = pl.pallas_call(body, name="small_update", out_shape=shapes)(all_small, *operands)
    per_name = {name: tuple(res[3 + 4 * i:7 + 4 * i]) for i, name in enumerate(names)}
    return res[0], per_name, res[1], res[2]


def _adamw_filters(g, w, m, v):
    n = len(g)

    def body(*refs):
        ins, outs = refs[:4 * n], refs[4 * n:]
        for i in range(n):
            d, mn, vn = _adamw_math(ins[4 * i + 1][...], ins[4 * i][...], ins[4 * i + 2][...], ins[4 * i + 3][...])
            for o_ref, val in zip(outs[3 * i:3 * i + 3], (d, mn, vn)):
                o_ref[...] = val

    operands, shapes = [], []
    for i in range(n):
        operands += [g[i], w[i], m[i], v[i]]
        shapes += [jax.ShapeDtypeStruct(w[i].shape, f32)] * 3
    res = pl.pallas_call(body, name="adamw_filters", out_shape=shapes)(*operands)
    return [tuple(res[3 * i:3 * i + 3]) for i in range(n)]


_REPLICATED = ("mix_norm_g", "ssd_conv_b", "dt_bias", "A_log", "D_skip", "ssd_norm_g", "conf_dw_b", "conf_ln_g",
               "conf_ln_b", "mlp_norm_g", "ple_gate_norm_g", "b_ple_gate", "ple_norm_g", "final_norm_g")
_CONV_WEIGHTS = ("ssd_conv_w", "conf_dw_w")
_WEIGHT_ORDER = ("mix_norm_g", "w_in", "ssd_conv_w", "ssd_conv_b", "dt_bias", "A_log", "D_skip", "ssd_norm_g", "conf_dw_w",
                 "conf_dw_b", "conf_ln_g", "conf_ln_b", "w_out", "mlp_norm_g", "w_up", "w_down", "ple_gate_norm_g",
                 "w_ple_gate", "b_ple_gate", "w_ple", "ple_norm_g", "final_norm_g")


def _blocks_of_columns(a):
    r, c8 = a.shape
    return jnp.transpose(a.reshape(r, N_DEV, c8 // N_DEV), (1, 0, 2))


def _columns_of_blocks(a):
    _, r, c = a.shape
    return jnp.transpose(a, (1, 0, 2)).reshape(r, N_DEV * c)


_LATER = ("w_out", "w_up", "w_down", "w_ple_gate", "w_ple")
_WHOLE = {
    "w_out": lambda a: a.reshape(2048, D_MODEL),
    "w_up": lambda a: a,
    "w_down": lambda a: a.reshape(D_FF, D_MODEL),
    "w_ple_gate": lambda a: a.reshape(D_MODEL, D_MODEL),
    "w_ple": _columns_of_blocks,
}
_BY_BLOCK = {
    "w_in": lambda g: jnp.concatenate(g, axis=0).reshape(N_DEV, IN_WIDTH // N_DEV, D_MODEL),
    "w_out": lambda g: jnp.concatenate(g, axis=0).reshape(N_DEV, 256, D_MODEL),
    "w_up": lambda g: g,
    "w_down": lambda g: g.reshape(N_DEV, 512, D_MODEL),
    "w_ple_gate": lambda g: g.astype(bf16).reshape(N_DEV, 128, D_MODEL),
    "w_ple": lambda g: _blocks_of_columns(g.astype(bf16)),
}


class _StepComm:
    def __init__(self, me, ready, gathers, token):
        self.me, self.ready, self.gathers, self.token = me, ready, gathers, token
        self.sent = []

    def start_token(self):
        return self.token

    def weight(self, name, after=()):
        if name not in self.ready:
            (land,), _ = _exchange_wait([self.gathers[name]], False, "gather_wait_" + name, list(after))
            self.ready[name] = _WHOLE[name](land)
        return self.ready[name]

    def send_grads(self, grads):
        names = list(grads)
        blocks = [_BY_BLOCK[n](grads[n]) for n in names]
        states, self.token = _exchange_start(blocks, True, "scatter_start_" + names[0], self.token)
        self.sent.append((names, states))
        return self.token


def kernel(x, p, mix_norm_g, w_in, ssd_conv_w, ssd_conv_b, dt_bias, A_log, D_skip, ssd_norm_g, conf_dw_w, conf_dw_b, conf_ln_g, conf_ln_b, w_out, mlp_norm_g, w_up, w_down, ple_gate_norm_g, w_ple_gate, b_ple_gate, w_ple, ple_norm_g, final_norm_g, loss_target, m_mix_norm_g, m_w_in, m_ssd_conv_w, m_ssd_conv_b, m_dt_bias, m_A_log, m_D_skip, m_ssd_norm_g, m_conf_dw_w, m_conf_dw_b, m_conf_ln_g, m_conf_ln_b, m_w_out, m_mlp_norm_g, m_w_up, m_w_down, m_ple_gate_norm_g, m_w_ple_gate, m_b_ple_gate, m_w_ple, m_ple_norm_g, m_final_norm_g, v_mix_norm_g, v_w_in, v_ssd_conv_w, v_ssd_conv_b, v_dt_bias, v_A_log, v_D_skip, v_ssd_norm_g, v_conf_dw_w, v_conf_dw_b, v_conf_ln_g, v_conf_ln_b, v_w_out, v_mlp_norm_g, v_w_up, v_w_down, v_ple_gate_norm_g, v_w_ple_gate, v_b_ple_gate, v_w_ple, v_ple_norm_g, v_final_norm_g):
    wts = dict(mix_norm_g=mix_norm_g, w_in=w_in, ssd_conv_w=ssd_conv_w, ssd_conv_b=ssd_conv_b, dt_bias=dt_bias, A_log=A_log,
               D_skip=D_skip, ssd_norm_g=ssd_norm_g, conf_dw_w=conf_dw_w, conf_dw_b=conf_dw_b, conf_ln_g=conf_ln_g,
               conf_ln_b=conf_ln_b, w_out=w_out, mlp_norm_g=mlp_norm_g, w_up=w_up, w_down=w_down,
               ple_gate_norm_g=ple_gate_norm_g, w_ple_gate=w_ple_gate, b_ple_gate=b_ple_gate, w_ple=w_ple,
               ple_norm_g=ple_norm_g, final_norm_g=final_norm_g)
    mom1 = dict(mix_norm_g=m_mix_norm_g, w_in=m_w_in, ssd_conv_w=m_ssd_conv_w, ssd_conv_b=m_ssd_conv_b, dt_bias=m_dt_bias,
                A_log=m_A_log, D_skip=m_D_skip, ssd_norm_g=m_ssd_norm_g, conf_dw_w=m_conf_dw_w, conf_dw_b=m_conf_dw_b,
                conf_ln_g=m_conf_ln_g, conf_ln_b=m_conf_ln_b, w_out=m_w_out, mlp_norm_g=m_mlp_norm_g, w_up=m_w_up,
                w_down=m_w_down, ple_gate_norm_g=m_ple_gate_norm_g, w_ple_gate=m_w_ple_gate, b_ple_gate=m_b_ple_gate,
                w_ple=m_w_ple, ple_norm_g=m_ple_norm_g, final_norm_g=m_final_norm_g)
    mom2 = dict(mix_norm_g=v_mix_norm_g, w_in=v_w_in, ssd_conv_w=v_ssd_conv_w, ssd_conv_b=v_ssd_conv_b, dt_bias=v_dt_bias,
                A_log=v_A_log, D_skip=v_D_skip, ssd_norm_g=v_ssd_norm_g, conf_dw_w=v_conf_dw_w, conf_dw_b=v_conf_dw_b,
                conf_ln_g=v_conf_ln_g, conf_ln_b=v_conf_ln_b, w_out=v_w_out, mlp_norm_g=v_mlp_norm_g, w_up=v_w_up,
                w_down=v_w_down, ple_gate_norm_g=v_ple_gate_norm_g, w_ple_gate=v_w_ple_gate, b_ple_gate=v_b_ple_gate,
                w_ple=v_w_ple, ple_norm_g=v_ple_norm_g, final_norm_g=v_final_norm_g)
    x_pos, y_pos, c_pos = _mesh_pos()
    me = 4 * x_pos + 2 * y_pos + c_pos

    first = _all_gather([wts["w_in"][0].T.astype(bf16), wts["ssd_conv_w"][0], wts["conf_dw_w"][0]], "gather_first")
    ready = {
        "w_in": first[0].reshape(IN_WIDTH, D_MODEL),
        "ssd_conv_w": jnp.pad(_columns_of_blocks(first[1]), ((0, 8 - SSD_CONV), (0, 0))),
        "conf_dw_w": jnp.pad(_columns_of_blocks(first[2]), ((0, 32 - CONF_KERNEL), (0, 0))),
    }
    shards = [wts[n][0].astype(bf16) for n in _LATER]
    states, token = _exchange_start(shards, False, "gather_start", first[1])
    comm = _StepComm(me, ready, dict(zip(_LATER, states)), token)
    small = {n: wts[n].reshape(1, -1) for n in _REPLICATED}

    grad_x, acc = _local_step(x[0], p[0, 0], loss_target[0], small, comm)

    packed = _pack_small(acc)
    (small_state,), _ = _exchange_start([packed], False, "small_start", comm.token)
    grads, delta, new_m, new_v = {}, {}, {}, {}
    me_index = me.astype(jnp.int32).reshape(1)

    def adamw_big(names, sent, after):
        lands, mine = _exchange_wait(sent, True, "scatter_wait_" + names[0], after)
        for n, land, own in zip(names, lands, mine):
            view = (lambda a: a[0].T) if n == "w_in" else (lambda a: a[0])
            back = (lambda a: a.T[None]) if n == "w_in" else (lambda a: a[None])
            out = _adamw_big(land, own, me_index, view(wts[n]), view(mom1[n]), view(mom2[n]), "adamw_" + n)
            grads[n], delta[n], new_m[n], new_v[n] = [back(a) for a in out]

    for names, sent in comm.sent[:-1]:
        adamw_big(names, sent, [grad_x])
    adamw_big(*comm.sent[-1], [delta[n] for n in _LATER])
    (all_small,), _ = _exchange_wait([small_state], False, "small_wait", [delta["w_in"]])
    as_row = lambda d: {n: d[n].reshape(1, -1) for n in _REPLICATED}
    loss, per_name, conv_w_sum, conf_w_sum = _small_update(all_small, as_row(wts), as_row(mom1), as_row(mom2))
    for n in _REPLICATED:
        grads[n], delta[n], new_m[n], new_v[n] = [a.reshape(wts[n].shape) for a in per_name[n]]
    filt_g = [lax.dynamic_slice_in_dim(conv_w_sum, me * 192, 192, axis=1),
              lax.dynamic_slice_in_dim(conf_w_sum, me * 128, 128, axis=1)]
    filt = _adamw_filters(filt_g, [wts[n][0] for n in _CONV_WEIGHTS], [mom1[n][0] for n in _CONV_WEIGHTS],
                          [mom2[n][0] for n in _CONV_WEIGHTS])
    for n, g, (d, mn, vn) in zip(_CONV_WEIGHTS, filt_g, filt):
        grads[n], delta[n], new_m[n], new_v[n] = g[None], d[None], mn[None], vn[None]

    return (loss.reshape(()), grad_x[None], *[grads[n] for n in _WEIGHT_ORDER], *[delta[n] for n in _WEIGHT_ORDER],
            *[new_m[n] for n in _WEIGHT_ORDER], *[new_v[n] for n in _WEIGHT_ORDER])
```

```python
import functools

import jax
import jax.numpy as jnp
from jax import lax
from jax.experimental import pallas as pl
from jax.experimental.pallas import tpu as pltpu

f32 = jnp.float32
bf16 = jnp.bfloat16

EPS = 1e-6
D_MODEL = 1024
SSD_HEADS = 16
SSD_HEAD_DIM = 64
SSD_GROUPS = 2
SSD_STATE = 128
SSD_CONV = 4
CHUNK = 128
XBC_WIDTH = 1536
CONF_KERNEL = 31
D_FF = 4096
PLE_DIM = 256
IN_WIDTH = 4624
N_DEV = 8
SEG_Z = (0, 1024)
SEG_XBC = (1024, 2560)
SEG_DT = (2560, 2576)
SEG_CV = (2576, 3600)
SEG_CG = (3600, 4624)
IN_SEGS = (SEG_Z, SEG_XBC, SEG_DT, SEG_CV, SEG_CG)

ADAM_LR = 0.001
ADAM_B1 = 0.9
ADAM_B2 = 0.999
ADAM_EPS = 1e-08
ADAM_WD = 0.01
ADAM_STEP = 10

VMEM_LIMIT_BYTES = 56 * 1024 * 1024
TOKEN_TILE = 512
SMALL_ROWS = 16

HBM_SPEC = pl.BlockSpec(memory_space=pltpu.HBM)
MESH_ID = pl.DeviceIdType.MESH


def _params(*semantics):
    return pltpu.CompilerParams(dimension_semantics=semantics, vmem_limit_bytes=VMEM_LIMIT_BYTES)


def _nn(a, b):
    return lax.dot_general(a, b, (((1,), (0,)), ((), ())), preferred_element_type=f32)


def _nt(a, b):
    return lax.dot_general(a, b, (((1,), (1,)), ((), ())), preferred_element_type=f32)


def _tn(a, b):
    return lax.dot_general(a, b, (((0,), (0,)), ((), ())), preferred_element_type=f32)


@jax.custom_vjp
def bnn(a, b):
    return _nn(a.astype(bf16), b.astype(bf16))


def _bnn_fwd(a, b):
    ab, bb = a.astype(bf16), b.astype(bf16)
    return _nn(ab, bb), (ab, bb)


def _bnn_bwd(res, g):
    ab, bb = res
    gb = g.astype(bf16)
    return _nt(gb, bb), _tn(ab, gb)


bnn.defvjp(_bnn_fwd, _bnn_bwd)


@jax.custom_vjp
def bnt(a, b):
    return _nt(a.astype(bf16), b.astype(bf16))


def _bnt_fwd(a, b):
    ab, bb = a.astype(bf16), b.astype(bf16)
    return _nt(ab, bb), (ab, bb)


def _bnt_bwd(res, g):
    ab, bb = res
    gb = g.astype(bf16)
    return _nn(gb, bb), _tn(gb, ab)


bnt.defvjp(_bnt_fwd, _bnt_bwd)


@jax.custom_vjp
def btn(a, b):
    return _tn(a.astype(bf16), b.astype(bf16))


def _btn_fwd(a, b):
    ab, bb = a.astype(bf16), b.astype(bf16)
    return _tn(ab, bb), (ab, bb)


def _btn_bwd(res, g):
    ab, bb = res
    gb = g.astype(bf16)
    return _nt(bb, gb), _nn(ab, gb)


btn.defvjp(_btn_fwd, _btn_bwd)

SCAN_CHUNKS = 4
CUMSUM_PASSES = 3
EXPAND_PASSES = 2


def _bf16_pieces(a, passes):
    pieces, rest = [], a
    for _ in range(passes):
        piece = rest.astype(bf16)
        pieces.append(piece)
        rest = rest - piece.astype(f32)
    return pieces


def _make_dot01(form, passes):
    fwd, bwd = {"right": (lambda b, p: _nn(p, b), lambda b, g: _nt(g, b)),
                "left": (lambda b, p: _nn(b, p), lambda b, g: _tn(b, g)),
                "tn": (lambda b, p: _tn(p, b), lambda b, g: _nt(b, g))}[form]

    def run(op, b, v):
        out = None
        for piece in _bf16_pieces(v, passes):
            term = op(b, piece)
            out = term if out is None else out + term
        return out

    @jax.custom_vjp
    def product(b, v):
        return run(fwd, b, v)

    product.defvjp(lambda b, v: (run(fwd, b, v), b), lambda b, g: (jnp.zeros_like(b), run(bwd, b, g)))
    return product


_DOT01 = {(form, passes): _make_dot01(form, passes)
          for form, passes in (("left", CUMSUM_PASSES), ("tn", CUMSUM_PASSES), ("right", EXPAND_PASSES))}


def _dot01(b01, a, form, passes):
    return _DOT01[form, passes](b01.astype(bf16), a)


def _rms(x, g):
    return x * lax.rsqrt(jnp.mean(x * x, axis=-1, keepdims=True) + EPS) * g


def _gated_norm(y, z, g):
    v = y * jax.nn.silu(z)
    half = v.shape[-1] // SSD_GROUPS
    parts = []
    for k in range(SSD_GROUPS):
        vk = v[:, k * half:(k + 1) * half]
        parts.append(vk * lax.rsqrt(jnp.mean(vk * vk, axis=-1, keepdims=True) + EPS))
    return jnp.concatenate(parts, axis=-1) * g


def _ln_silu(v, g, b):
    mu = jnp.mean(v, axis=-1, keepdims=True)
    xc = v - mu
    y = xc * lax.rsqrt(jnp.mean(xc * xc, axis=-1, keepdims=True) + EPS) * g + b
    return jax.nn.silu(y)


def _acc_init(i, *refs):
    @pl.when(i == 0)
    def _():
        for r in refs:
            r[...] = jnp.zeros_like(r)


def _after_token(body, token):
    if token is None:
        return body, [], []

    def body_after(tok_ref, *refs):
        del tok_ref
        body(*refs)

    return body_after, [token], [pl.BlockSpec(memory_space=pl.ANY)]


def _row_spec(tm, n):
    return pl.BlockSpec((tm, n), lambda i: (i, 0))


def _const_spec(shape):
    nd = len(shape)
    return pl.BlockSpec(shape, lambda i: (0,) * nd)


def _prev_halo_spec(tm, halo, n):
    return pl.BlockSpec((halo, n), lambda i: (jnp.maximum(i * (tm // halo) - 1, 0), 0))


def _next_halo_spec(tm, halo, n, t):
    return pl.BlockSpec((halo, n), lambda i: (jnp.minimum((i + 1) * (tm // halo), t // halo - 1), 0))


def _in_proj(x, g1, w_in_p, token=None):
    t = x.shape[0]
    tm = min(TOKEN_TILE, t)

    def body(x_ref, g_ref, w_ref, u_ref, z_ref, xbc_ref, dt_ref, cv_ref, cg_ref):
        u = _rms(x_ref[...], g_ref[...]).astype(bf16)
        u_ref[...] = u
        for (lo, hi), o_ref in zip(IN_SEGS, (z_ref, xbc_ref, dt_ref, cv_ref, cg_ref)):
            o_ref[...] = _nt(u, w_ref[lo:hi, :])

    widths = [hi - lo for lo, hi in IN_SEGS]
    outs = [jax.ShapeDtypeStruct((t, D_MODEL), bf16)] + [jax.ShapeDtypeStruct((t, n), f32) for n in widths]
    body, tok, tok_spec = _after_token(body, token)
    return pl.pallas_call(
        body, name="in_proj", grid=(t // tm,),
        in_specs=tok_spec + [_row_spec(tm, D_MODEL), _const_spec((1, D_MODEL)), _const_spec((IN_WIDTH, D_MODEL))],
        out_specs=[_row_spec(tm, D_MODEL)] + [_row_spec(tm, n) for n in widths],
        out_shape=outs, compiler_params=_params("arbitrary"),
    )(*tok, x, g1, w_in_p)


ROW_BLOCK = 16
LANE_CHUNK = 512


def _blocks(rows, cols):
    return [(slice(r, r + ROW_BLOCK), slice(c, c + LANE_CHUNK))
            for c in range(0, cols, LANE_CHUNK) for r in range(0, rows, ROW_BLOCK)]


def _conv4_block(ext_ref, w_ref, b_ref, rs, ls):
    acc = b_ref[:, ls] + w_ref[0:1, ls] * ext_ref[rs.start + 8 - 3:rs.stop + 8 - 3, ls]
    for k in range(1, SSD_CONV):
        acc = acc + w_ref[k:k + 1, ls] * ext_ref[rs.start + 8 - 3 + k:rs.stop + 8 - 3 + k, ls]
    return acc


def _ssd_pre(xbc_raw, dt_raw, cw, cb, dt_bias):
    t = xbc_raw.shape[0]
    tm = min(TOKEN_TILE, t)

    def body(cur_ref, halo_ref, dt_ref, w_ref, b_ref, dtb_ref, act_ref, dto_ref, ext_ref):
        i = pl.program_id(0)
        ext_ref[0:8, :] = jnp.where(i == 0, 0.0, halo_ref[...])
        ext_ref[8:, :] = cur_ref[...]
        for rs, ls in _blocks(tm, XBC_WIDTH):
            act_ref[rs, ls] = jax.nn.silu(_conv4_block(ext_ref, w_ref, b_ref, rs, ls))
        dto_ref[...] = jax.nn.softplus(dt_ref[...] + dtb_ref[...])

    return pl.pallas_call(
        body, name="ssd_pre", grid=(t // tm,),
        in_specs=[_row_spec(tm, XBC_WIDTH), _prev_halo_spec(tm, 8, XBC_WIDTH), _row_spec(tm, SSD_HEADS),
                  _const_spec((8, XBC_WIDTH)), _const_spec((1, XBC_WIDTH)), _const_spec((1, SSD_HEADS))],
        out_specs=[_row_spec(tm, XBC_WIDTH), _row_spec(tm, SSD_HEADS)],
        out_shape=(jax.ShapeDtypeStruct((t, XBC_WIDTH), f32), jax.ShapeDtypeStruct((t, SSD_HEADS), f32)),
        scratch_shapes=[pltpu.VMEM((tm + 8, XBC_WIDTH), f32)],
        compiler_params=_params("arbitrary"),
    )(xbc_raw, xbc_raw, dt_raw, cw, cb, dt_bias)


def _ssd_consts():
    r = lax.broadcasted_iota(jnp.int32, (CHUNK, CHUNK), 0)
    c = lax.broadcasted_iota(jnp.int32, (CHUNK, CHUNK), 1)
    causal = r >= c
    tril = causal.astype(f32)
    triu = (r <= c).astype(f32)
    hh = lax.broadcasted_iota(jnp.int32, (SSD_HEADS, D_MODEL), 0)
    jj = lax.broadcasted_iota(jnp.int32, (SSD_HEADS, D_MODEL), 1)
    expand = (lax.shift_right_logical(jj, 6) == hh).astype(f32)
    lane = lax.broadcasted_iota(jnp.int32, (1, 2 * SSD_HEAD_DIM), 1)
    m0 = (lane < SSD_HEAD_DIM).astype(f32)
    return causal, tril, triu, expand, m0, 1.0 - m0


def _ssd_chunk(xs, bm, cm, dt, s_prev, a_log, d_skip, consts):
    causal, tril, triu, expand, m0, m1 = consts
    a = dt * (-jnp.exp(a_log))
    cs = _dot01(tril, a, "left", CUMSUM_PASSES)
    cs_t = _dot01(triu, a, "tn", CUMSUM_PASSES)
    cs_last = cs[CHUNK - 1:CHUNK, :]
    per_head = jnp.concatenate([dt, jnp.exp(cs_last - cs), jnp.exp(cs), jnp.broadcast_to(jnp.exp(cs_last), (8, SSD_HEADS)),
                                jnp.broadcast_to(d_skip, (8, SSD_HEADS))], axis=0)
    per_channel = _dot01(expand, per_head, "right", EXPAND_PASSES)
    dt_e, dec_end, dec_start = (per_channel[i * CHUNK:(i + 1) * CHUNK] for i in range(3))
    chunk_dec = per_channel[3 * CHUNK:3 * CHUNK + 1]
    d_e = per_channel[3 * CHUNK + 8:3 * CHUNK + 9]
    xc = xs * dt_e
    x_dec = xc * dec_end
    gw = D_MODEL // SSD_GROUPS
    ys, states = [], []
    for g in range(SSD_GROUPS):
        bg = bm[:, g * SSD_STATE:(g + 1) * SSD_STATE]
        cg = cm[:, g * SSD_STATE:(g + 1) * SSD_STATE]
        sp = s_prev[:, g * gw:(g + 1) * gw]
        states.append(sp * chunk_dec[:, g * gw:(g + 1) * gw] + btn(bg, x_dec[:, g * gw:(g + 1) * gw]))
        y_off = bnn(cg, sp) * dec_start[:, g * gw:(g + 1) * gw]
        scores = bnt(cg, bg)
        pieces = []
        for pr in range(gw // (2 * SSD_HEAD_DIM)):
            lo = g * gw + pr * 2 * SSD_HEAD_DIM
            xp = xc[:, lo:lo + 2 * SSD_HEAD_DIM]
            mats = []
            for h in (lo // SSD_HEAD_DIM, lo // SSD_HEAD_DIM + 1):
                seg = cs[:, h:h + 1] - cs_t[h:h + 1, :]
                mats.append(scores * jnp.exp(jnp.where(causal, seg, -jnp.inf)))
            pieces.append(bnn(jnp.concatenate(mats, axis=1), jnp.concatenate([xp * m0, xp * m1], axis=0)))
        ys.append(jnp.concatenate(pieces, axis=-1) + y_off)
    y = jnp.concatenate(ys, axis=-1) + xs * d_e
    return y, jnp.concatenate(states, axis=-1)


def _ssd_scan(xbc_act, dt, a_log, d_skip):
    t = xbc_act.shape[0]
    nc = t // CHUNK
    rows = SCAN_CHUNKS * CHUNK

    def body(xbc_ref, dt_ref, al_ref, dk_ref, y_ref, sp_ref, state_ref):
        c = pl.program_id(0)

        @pl.when(c == 0)
        def _():
            state_ref[...] = jnp.zeros_like(state_ref)

        state = state_ref[...]
        consts = _ssd_consts()
        for j in range(SCAN_CHUNKS):
            rs = slice(j * CHUNK, (j + 1) * CHUNK)
            sp_ref[j] = state
            y_ref[rs, :], state = _ssd_chunk(xbc_ref[rs, 0:1024], xbc_ref[rs, 1024:1280], xbc_ref[rs, 1280:1536],
                                             dt_ref[rs, :], state, al_ref[...], dk_ref[...], consts)
        state_ref[...] = state

    return pl.pallas_call(
        body, name="ssd_scan", grid=(nc // SCAN_CHUNKS,),
        in_specs=[_row_spec(rows, XBC_WIDTH), _row_spec(rows, SSD_HEADS), _const_spec((1, SSD_HEADS)),
                  _const_spec((1, SSD_HEADS))],
        out_specs=[_row_spec(rows, D_MODEL), pl.BlockSpec((SCAN_CHUNKS, SSD_STATE, D_MODEL), lambda c: (c, 0, 0))],
        out_shape=(jax.ShapeDtypeStruct((t, D_MODEL), f32), jax.ShapeDtypeStruct((nc, SSD_STATE, D_MODEL), f32)),
        scratch_shapes=[pltpu.VMEM((SSD_STATE, D_MODEL), f32)],
        compiler_params=_params("arbitrary"),
    )(xbc_act, dt, a_log, d_skip)


CONV_HALO = 32
CONV_BLOCK = 128
LANE_TILE = 128
SUBLANES = 8


def _fill_glu_ext(ext_ref, cv_ref, cg_ref, hcv_ref, hcg_ref, i):
    ext_ref[0:CONV_HALO, :] = jnp.where(i == 0, 0.0, hcv_ref[...] * jax.nn.sigmoid(hcg_ref[...]))
    ext_ref[CONV_HALO:, :] = cv_ref[...] * jax.nn.sigmoid(cg_ref[...])


def _shifted_copies(src_ref, dst_ref, rows):
    for s in range(1, SUBLANES):
        dst_ref[s, 0:rows, :] = src_ref[pl.ds(s, rows), :]


def _tap_source(src_ref, shifted_ref, offset):
    s = offset % SUBLANES
    return (src_ref if s == 0 else shifted_ref.at[s]), offset - s


def _conv_rows(src_ref, shifted_ref, taps, offsets, lanes, r0, init):
    accs = [init] * (CONV_BLOCK // SUBLANES)
    for k, off in enumerate(offsets):
        ref, base = _tap_source(src_ref, shifted_ref, off)
        for j in range(len(accs)):
            accs[j] = accs[j] + taps[k] * ref[pl.ds(r0 + base + SUBLANES * j, SUBLANES), lanes]
    return accs


def _conf_conv(cv, cg, w, b):
    t = cv.shape[0]
    tm = min(TOKEN_TILE, t)
    ext_rows = tm + CONV_HALO
    offsets = [CONV_HALO - (CONF_KERNEL - 1) + k for k in range(CONF_KERNEL)]

    def body(cv_ref, cg_ref, hcv_ref, hcg_ref, w_ref, b_ref, o_ref, ext_ref, shifted_ref):
        _fill_glu_ext(ext_ref, cv_ref, cg_ref, hcv_ref, hcg_ref, pl.program_id(0))
        _shifted_copies(ext_ref, shifted_ref, ext_rows - SUBLANES)
        for lb in range(1024 // LANE_TILE):
            lanes = slice(lb * LANE_TILE, (lb + 1) * LANE_TILE)
            taps = [jnp.broadcast_to(w_ref[k:k + 1, lanes], (SUBLANES, LANE_TILE)) for k in range(CONF_KERNEL)]
            bias = jnp.broadcast_to(b_ref[:, lanes], (SUBLANES, LANE_TILE))

            def rows(rb, carry, lanes=lanes, taps=taps, bias=bias):
                r0 = pl.multiple_of(rb * CONV_BLOCK, CONV_BLOCK)
                accs = _conv_rows(ext_ref, shifted_ref, taps, offsets, lanes, r0, bias)
                for j, a in enumerate(accs):
                    o_ref[pl.ds(r0 + SUBLANES * j, SUBLANES), lanes] = a
                return carry

            lax.fori_loop(0, tm // CONV_BLOCK, rows, 0)

    return pl.pallas_call(
        body, name="conf_conv", grid=(t // tm,),
        in_specs=[_row_spec(tm, 1024), _row_spec(tm, 1024), _prev_halo_spec(tm, CONV_HALO, 1024),
                  _prev_halo_spec(tm, CONV_HALO, 1024), _const_spec((32, 1024)), _const_spec((1, 1024))],
        out_specs=_row_spec(tm, 1024), out_shape=jax.ShapeDtypeStruct((t, 1024), f32),
        scratch_shapes=[pltpu.VMEM((ext_rows, 1024), f32), pltpu.VMEM((SUBLANES, ext_rows, 1024), f32)],
        compiler_params=_params("arbitrary"),
    )(cv, cg, cv, cg, w, b)


def _out_proj(y, z, v2, x, g_ssd, ln_g, ln_b, w_out, g2):
    t = x.shape[0]
    tm = min(TOKEN_TILE, t)

    def body(y_ref, z_ref, v_ref, x_ref, gs_ref, lg_ref, lb_ref, w_ref, g2_ref, ys_ref, yc_ref, h1_ref, u2_ref):
        ys = _gated_norm(y_ref[...], z_ref[...], gs_ref[...]).astype(bf16)
        yc = _ln_silu(v_ref[...], lg_ref[...], lb_ref[...]).astype(bf16)
        ys_ref[...] = ys
        yc_ref[...] = yc
        h1 = x_ref[...] + _nn(ys, w_ref[0:1024, :]) + _nn(yc, w_ref[1024:2048, :])
        h1_ref[...] = h1
        u2_ref[...] = _rms(h1, g2_ref[...]).astype(bf16)

    vec = _const_spec((1, 1024))
    row = _row_spec(tm, 1024)
    return pl.pallas_call(
        body, name="out_proj", grid=(t // tm,),
        in_specs=[row, row, row, row, vec, vec, vec, _const_spec((2048, 1024)), vec],
        out_specs=[row, row, row, row],
        out_shape=(jax.ShapeDtypeStruct((t, 1024), bf16), jax.ShapeDtypeStruct((t, 1024), bf16),
                   jax.ShapeDtypeStruct((t, 1024), f32), jax.ShapeDtypeStruct((t, 1024), bf16)),
        compiler_params=_params("arbitrary"),
    )(y, z, v2, x, g_ssd, ln_g, ln_b, w_out, g2)


def _mlp_up(u2, w_up):
    t = u2.shape[0]
    tm = min(TOKEN_TILE, t)
    blk = D_FF // N_DEV

    def body(u_ref, w_ref, pre_ref, hs_ref):
        u = u_ref[...]
        for k in range(N_DEV):
            pre = _nn(u, w_ref[k])
            pre_ref[:, k * blk:(k + 1) * blk] = pre.astype(bf16)
            r = jnp.maximum(pre, 0.0)
            hs_ref[:, k * blk:(k + 1) * blk] = (r * r).astype(bf16)

    return pl.pallas_call(
        body, name="mlp_up", grid=(t // tm,),
        in_specs=[_row_spec(tm, D_MODEL), _const_spec((N_DEV, D_MODEL, D_FF // N_DEV))],
        out_specs=[_row_spec(tm, D_FF), _row_spec(tm, D_FF)],
        out_shape=(jax.ShapeDtypeStruct((t, D_FF), bf16), jax.ShapeDtypeStruct((t, D_FF), bf16)),
        compiler_params=_params("arbitrary"),
    )(u2, w_up)


def _mlp_down(h1, hs, w_down):
    t = h1.shape[0]
    tm = min(TOKEN_TILE, t)

    def body(h_ref, hs_ref, w_ref, o_ref):
        o_ref[...] = h_ref[...] + _nn(hs_ref[...], w_ref[...])

    return pl.pallas_call(
        body, name="mlp_down", grid=(t // tm,),
        in_specs=[_row_spec(tm, D_MODEL), _row_spec(tm, D_FF), _const_spec((D_FF, D_MODEL))],
        out_specs=_row_spec(tm, D_MODEL), out_shape=jax.ShapeDtypeStruct((t, D_MODEL), f32),
        compiler_params=_params("arbitrary"),
    )(h1, hs, w_down)


TAIL_LOSS, TAIL_FINAL_G, TAIL_PLE_G, TAIL_GATE_B, TAIL_GATE_NORM_G = 0, 1, 2, 3, 4


def _tail(h2, p, target, g3, w_gate, b_gate, w_ple, ple_g, fin_g):
    t = h2.shape[0]
    tm = min(TOKEN_TILE, t)

    def body(h_ref, p_ref, t_ref, g3_ref, wg_ref, bg_ref, wp_ref, pg_ref, fg_ref,
             dh_ref, dhb_ref, dwg_ref, dwp_ref, acc_ref):
        i = pl.program_id(0)
        _acc_init(i, dwg_ref, dwp_ref, acc_ref)
        h2v = h_ref[...]
        tgt = t_ref[...]
        u3, vjp_u3 = jax.vjp(_rms, h2v, g3_ref[...])
        u3b = u3.astype(bf16)
        pb = p_ref[...].astype(bf16)
        gate_pre = _nn(u3b, wg_ref[...]) + bg_ref[...]
        emb_pre = _nn(pb, wp_ref[...])

        def tail_fn(hv, gp, ep, pg, fg):
            h3 = hv + jax.nn.sigmoid(gp) * _rms(ep, pg)
            err = _rms(h3, fg) - tgt
            return 0.5 * jnp.mean(err * err, axis=-1, keepdims=True)

        loss_tok, vjp_tail = jax.vjp(tail_fn, h2v, gate_pre, emb_pre, pg_ref[...], fg_ref[...])
        dh_a, dgp, dep, dpg, dfg = vjp_tail(jnp.ones_like(loss_tok))
        dgpb = dgp.astype(bf16)
        dh_b, dg3 = vjp_u3(_nt(dgpb, wg_ref[...]))
        dh = dh_a + dh_b
        dh_ref[...] = dh
        dhb_ref[...] = dh.astype(bf16)
        dwg_ref[...] += _tn(u3b, dgpb)
        dwp_ref[...] += _tn(pb, dep.astype(bf16))
        acc_ref[TAIL_LOSS:TAIL_LOSS + 1, :] += jnp.broadcast_to(jnp.sum(loss_tok, axis=0, keepdims=True), (1, 1024))
        acc_ref[TAIL_FINAL_G:TAIL_FINAL_G + 1, :] += dfg
        acc_ref[TAIL_PLE_G:TAIL_PLE_G + 1, :] += dpg
        acc_ref[TAIL_GATE_B:TAIL_GATE_B + 1, :] += jnp.sum(dgp, axis=0, keepdims=True)
        acc_ref[TAIL_GATE_NORM_G:TAIL_GATE_NORM_G + 1, :] += dg3

    vec = _const_spec((1, 1024))
    row = _row_spec(tm, 1024)
    return pl.pallas_call(
        body, name="tail", grid=(t // tm,),
        in_specs=[row, _row_spec(tm, PLE_DIM), row, vec, _const_spec((1024, 1024)), vec, _const_spec((PLE_DIM, 1024)), vec, vec],
        out_specs=[row, row, _const_spec((1024, 1024)), _const_spec((PLE_DIM, 1024)), _const_spec((8, 1024))],
        out_shape=(jax.ShapeDtypeStruct((t, 1024), f32), jax.ShapeDtypeStruct((t, 1024), bf16),
                   jax.ShapeDtypeStruct((1024, 1024), f32), jax.ShapeDtypeStruct((PLE_DIM, 1024), f32),
                   jax.ShapeDtypeStruct((8, 1024), f32)),
        compiler_params=_params("arbitrary"),
    )(h2, p, target, g3, w_gate, b_gate, w_ple, ple_g, fin_g)


def _mlp_bwd(dh2, dh2b, pre, h1, g2, w_down, w_up, token=None):
    t = dh2.shape[0]
    tm = min(TOKEN_TILE, t)
    blk = D_FF // N_DEV

    def body(dh_ref, dhb_ref, pre_ref, h1_ref, g_ref, wd_hbm, wu_hbm, dpre_ref, dh1_ref, dh1b_ref, acc_ref,
             wd_ref, wu_ref):
        i = pl.program_id(0)
        _acc_init(i, acc_ref)

        @pl.when(i == 0)
        def _():
            pltpu.sync_copy(wd_hbm, wd_ref)
            pltpu.sync_copy(wu_hbm, wu_ref)

        dhb = dhb_ref[...]
        du2 = jnp.zeros((tm, D_MODEL), f32)
        for k in range(D_FF // blk):
            dhs = _nt(dhb, wd_ref[k * blk:(k + 1) * blk, :])
            dpre = (dhs * (2.0 * jnp.maximum(pre_ref[:, k * blk:(k + 1) * blk].astype(f32), 0.0))).astype(bf16)
            dpre_ref[:, k * blk:(k + 1) * blk] = dpre
            du2 = du2 + _nt(dpre, wu_ref[k])
        _, vjp_u2 = jax.vjp(_rms, h1_ref[...], g_ref[...])
        d, dg = vjp_u2(du2)
        dh1 = dh_ref[...] + d
        dh1_ref[...] = dh1
        dh1b_ref[...] = dh1.astype(bf16)
        acc_ref[0:1, :] += dg

    row = _row_spec(tm, 1024)
    body, tok, tok_spec = _after_token(body, token)
    return pl.pallas_call(
        body, name="mlp_bwd", grid=(t // tm,),
        in_specs=tok_spec + [row, row, _row_spec(tm, D_FF), row, _const_spec((1, 1024)), HBM_SPEC, HBM_SPEC],
        out_specs=[_row_spec(tm, D_FF), row, row, _const_spec((8, 1024))],
        out_shape=(jax.ShapeDtypeStruct((t, D_FF), bf16), jax.ShapeDtypeStruct((t, 1024), f32),
                   jax.ShapeDtypeStruct((t, 1024), bf16), jax.ShapeDtypeStruct((8, 1024), f32)),
        scratch_shapes=[pltpu.VMEM((D_FF, D_MODEL), bf16), pltpu.VMEM((N_DEV, D_MODEL, D_FF // N_DEV), bf16)],
        compiler_params=_params("arbitrary"),
    )(*tok, dh2, dh2b, pre, h1, g2, w_down, w_up)


OPB_SSD_G, OPB_LN_G, OPB_LN_B, OPB_CONV_B = 0, 1, 2, 3


def _out_proj_bwd(dh1b, y, z, v2, g_ssd, ln_g, ln_b, w_out, token=None):
    t = y.shape[0]
    tm = min(TOKEN_TILE, t)

    def body(dh_ref, y_ref, z_ref, v_ref, gs_ref, lg_ref, lb_ref, w_ref, dy_ref, dz_ref, dv_ref, acc_ref):
        i = pl.program_id(0)
        _acc_init(i, acc_ref)
        dhb = dh_ref[...]
        dys = _nt(dhb, w_ref[0:1024, :])
        dyc = _nt(dhb, w_ref[1024:2048, :])
        _, vjp_g = jax.vjp(_gated_norm, y_ref[...], z_ref[...], gs_ref[...])
        dy, dz, dgs = vjp_g(dys)
        _, vjp_l = jax.vjp(_ln_silu, v_ref[...], lg_ref[...], lb_ref[...])
        dv, dlg, dlb = vjp_l(dyc)
        dy_ref[...] = dy
        dz_ref[...] = dz.astype(bf16)
        dv_ref[...] = dv
        acc_ref[OPB_SSD_G:OPB_SSD_G + 1, :] += dgs
        acc_ref[OPB_LN_G:OPB_LN_G + 1, :] += dlg
        acc_ref[OPB_LN_B:OPB_LN_B + 1, :] += dlb
        acc_ref[OPB_CONV_B:OPB_CONV_B + 1, :] += jnp.sum(dv, axis=0, keepdims=True)

    vec = _const_spec((1, 1024))
    row = _row_spec(tm, 1024)
    body, tok, tok_spec = _after_token(body, token)
    return pl.pallas_call(
        body, name="out_proj_bwd", grid=(t // tm,),
        in_specs=tok_spec + [row, row, row, row, vec, vec, vec, _const_spec((2048, 1024))],
        out_specs=[row, row, row, _const_spec((8, 1024))],
        out_shape=(jax.ShapeDtypeStruct((t, 1024), f32), jax.ShapeDtypeStruct((t, 1024), bf16),
                   jax.ShapeDtypeStruct((t, 1024), f32), jax.ShapeDtypeStruct((8, 1024), f32)),
        compiler_params=_params("arbitrary"),
    )(*tok, dh1b, y, z, v2, g_ssd, ln_g, ln_b, w_out)


def _conf_conv_bwd(dv2, cv, cg, w):
    t = cv.shape[0]
    tm = min(TOKEN_TILE, t)
    ext_rows = tm + CONV_HALO
    offsets = [CONF_KERNEL - 1 - k for k in range(CONF_KERNEL)]

    def body(dv_ref, dvn_ref, cv_ref, cg_ref, w_ref, dcv_ref, dcg_ref, dw_ref, glu_ref, dext_ref, sg_ref, shifted_ref):
        i = pl.program_id(0)
        _acc_init(i, dw_ref)
        sg = jax.nn.sigmoid(cg_ref[...])
        sg_ref[...] = sg
        glu_ref[...] = cv_ref[...] * sg
        dext_ref[0:tm, :] = dv_ref[...]
        dext_ref[tm:, :] = jnp.where(i == pl.num_programs(0) - 1, 0.0, dvn_ref[...])
        _shifted_copies(dext_ref, shifted_ref, ext_rows - SUBLANES)

        for lb in range(1024 // LANE_TILE):
            lanes = slice(lb * LANE_TILE, (lb + 1) * LANE_TILE)

            def rows_w(rb, accs, lanes=lanes):
                r0 = pl.multiple_of(rb * CONV_BLOCK, CONV_BLOCK)
                accs = list(accs)
                for j in range(CONV_BLOCK // SUBLANES):
                    x = glu_ref[pl.ds(r0 + SUBLANES * j, SUBLANES), lanes]
                    for k, off in enumerate(offsets):
                        ref, base = _tap_source(dext_ref, shifted_ref, off)
                        accs[k] = accs[k] + x * ref[pl.ds(r0 + base + SUBLANES * j, SUBLANES), lanes]
                return tuple(accs)

            zero = jnp.zeros((SUBLANES, LANE_TILE), f32)
            accs = lax.fori_loop(0, tm // CONV_BLOCK, rows_w, (zero,) * CONF_KERNEL)
            for k in range(CONF_KERNEL):
                dw_ref[k:k + 1, lanes] += jnp.sum(accs[k], axis=0, keepdims=True)

        for lb in range(1024 // LANE_TILE):
            lanes = slice(lb * LANE_TILE, (lb + 1) * LANE_TILE)
            taps = [jnp.broadcast_to(w_ref[k:k + 1, lanes], (SUBLANES, LANE_TILE)) for k in range(CONF_KERNEL)]

            def rows_x(rb, carry, lanes=lanes, taps=taps):
                r0 = pl.multiple_of(rb * CONV_BLOCK, CONV_BLOCK)
                zero = jnp.zeros((SUBLANES, LANE_TILE), f32)
                dglu = jnp.concatenate(_conv_rows(dext_ref, shifted_ref, taps, offsets, lanes, r0, zero), axis=0)
                sg = sg_ref[pl.ds(r0, CONV_BLOCK), lanes]
                cvv = cv_ref[pl.ds(r0, CONV_BLOCK), lanes]
                dcv_ref[pl.ds(r0, CONV_BLOCK), lanes] = (dglu * sg).astype(bf16)
                dcg_ref[pl.ds(r0, CONV_BLOCK), lanes] = (dglu * cvv * sg * (1.0 - sg)).astype(bf16)
                return carry

            lax.fori_loop(0, tm // CONV_BLOCK, rows_x, 0)

    row = _row_spec(tm, 1024)
    return pl.pallas_call(
        body, name="conf_conv_bwd", grid=(t // tm,),
        in_specs=[row, _next_halo_spec(tm, CONV_HALO, 1024, t), row, row, _const_spec((32, 1024))],
        out_specs=[row, row, _const_spec((32, 1024))],
        out_shape=(jax.ShapeDtypeStruct((t, 1024), bf16), jax.ShapeDtypeStruct((t, 1024), bf16),
                   jax.ShapeDtypeStruct((32, 1024), f32)),
        scratch_shapes=[pltpu.VMEM((tm, 1024), f32), pltpu.VMEM((ext_rows, 1024), f32), pltpu.VMEM((tm, 1024), f32),
                        pltpu.VMEM((SUBLANES, ext_rows, 1024), f32)],
        compiler_params=_params("arbitrary"),
    )(dv2, dv2, cv, cg, w)


def _ssd_scan_bwd(xbc_act, dt, s_prev, dy, a_log, d_skip):
    t = xbc_act.shape[0]
    steps = t // CHUNK // SCAN_CHUNKS
    rows = SCAN_CHUNKS * CHUNK

    def body(xbc_ref, dt_ref, sp_ref, dy_ref, al_ref, dk_ref, dxbc_ref, ddt_ref, dal_ref, ddk_ref, ds_ref):
        i = pl.program_id(0)
        _acc_init(i, ds_ref, dal_ref, ddk_ref)
        consts = _ssd_consts()
        ds = ds_ref[...]
        dal_sum, ddk_sum = dal_ref[...], ddk_ref[...]
        for j in reversed(range(SCAN_CHUNKS)):
            rs = slice(j * CHUNK, (j + 1) * CHUNK)
            _, vjp_c = jax.vjp(
                functools.partial(_ssd_chunk, consts=consts),
                xbc_ref[rs, 0:1024], xbc_ref[rs, 1024:1280], xbc_ref[rs, 1280:1536], dt_ref[rs, :], sp_ref[j],
                al_ref[...], dk_ref[...])
            dxs, dbm, dcm, ddt, ds, dal, ddk = vjp_c((dy_ref[rs, :], ds))
            dxbc_ref[rs, 0:1024] = dxs
            dxbc_ref[rs, 1024:1280] = dbm
            dxbc_ref[rs, 1280:1536] = dcm
            ddt_ref[rs, :] = ddt
            dal_sum, ddk_sum = dal_sum + dal, ddk_sum + ddk
        ds_ref[...] = ds
        dal_ref[...] = dal_sum
        ddk_ref[...] = ddk_sum

    rev = lambda i: (steps - 1 - i, 0)
    return pl.pallas_call(
        body, name="ssd_scan_bwd", grid=(steps,),
        in_specs=[pl.BlockSpec((rows, XBC_WIDTH), rev), pl.BlockSpec((rows, SSD_HEADS), rev),
                  pl.BlockSpec((SCAN_CHUNKS, SSD_STATE, D_MODEL), lambda i: (steps - 1 - i, 0, 0)),
                  pl.BlockSpec((rows, D_MODEL), rev), _const_spec((1, SSD_HEADS)), _const_spec((1, SSD_HEADS))],
        out_specs=[pl.BlockSpec((rows, XBC_WIDTH), rev), pl.BlockSpec((rows, SSD_HEADS), rev),
                   _const_spec((1, SSD_HEADS)), _const_spec((1, SSD_HEADS))],
        out_shape=(jax.ShapeDtypeStruct((t, XBC_WIDTH), f32), jax.ShapeDtypeStruct((t, SSD_HEADS), f32),
                   jax.ShapeDtypeStruct((1, SSD_HEADS), f32), jax.ShapeDtypeStruct((1, SSD_HEADS), f32)),
        scratch_shapes=[pltpu.VMEM((SSD_STATE, D_MODEL), f32)],
        compiler_params=_params("arbitrary"),
    )(xbc_act, dt, s_prev, dy, a_log, d_skip)


def _ssd_pre_bwd(xbc_raw, dxbc_act, ddt, dt_raw, cw, cb, dt_bias):
    t = xbc_raw.shape[0]
    tm = min(TOKEN_TILE, t)

    def body(cur_ref, halo_ref, dact_ref, ddt_ref, dtr_ref, w_ref, b_ref, dtb_ref,
             dco_ref, ddtr_ref, dw_ref, db_ref, ddtb_ref, ext_ref):
        i = pl.program_id(0)
        _acc_init(i, dw_ref, db_ref, ddtb_ref)
        ext_ref[0:8, :] = jnp.where(i == 0, 0.0, halo_ref[...])
        ext_ref[8:, :] = cur_ref[...]
        fold = lambda a: a[0:8] + a[8:ROW_BLOCK]
        for c in range(0, XBC_WIDTH, LANE_CHUNK):
            ls = slice(c, c + LANE_CHUNK)
            part_b = jnp.zeros((8, LANE_CHUNK), f32)
            part_w = [jnp.zeros((8, LANE_CHUNK), f32)] * SSD_CONV
            for r in range(0, tm, ROW_BLOCK):
                rs = slice(r, r + ROW_BLOCK)
                co = _conv4_block(ext_ref, w_ref, b_ref, rs, ls)
                sg = jax.nn.sigmoid(co)
                dco = dact_ref[rs, ls] * sg * (1.0 + co * (1.0 - sg))
                dco_ref[rs, ls] = dco
                part_b = part_b + fold(dco)
                part_w = [pw + fold(dco * ext_ref[r + 8 - 3 + k:r + ROW_BLOCK + 8 - 3 + k, ls])
                          for k, pw in enumerate(part_w)]
            db_ref[:, ls] += jnp.sum(part_b, axis=0, keepdims=True)
            for k in range(SSD_CONV):
                dw_ref[k:k + 1, ls] += jnp.sum(part_w[k], axis=0, keepdims=True)
        ddtr = ddt_ref[...] * jax.nn.sigmoid(dtr_ref[...] + dtb_ref[...])
        ddtb_ref[...] += jnp.sum(ddtr, axis=0, keepdims=True)
        ddtr_ref[...] = ddtr.astype(bf16)

    return pl.pallas_call(
        body, name="ssd_pre_bwd", grid=(t // tm,),
        in_specs=[_row_spec(tm, XBC_WIDTH), _prev_halo_spec(tm, 8, XBC_WIDTH), _row_spec(tm, XBC_WIDTH),
                  _row_spec(tm, SSD_HEADS), _row_spec(tm, SSD_HEADS), _const_spec((8, XBC_WIDTH)),
                  _const_spec((1, XBC_WIDTH)), _const_spec((1, SSD_HEADS))],
        out_specs=[_row_spec(tm, XBC_WIDTH), _row_spec(tm, SSD_HEADS), _const_spec((8, XBC_WIDTH)),
                   _const_spec((1, XBC_WIDTH)), _const_spec((1, SSD_HEADS))],
        out_shape=(jax.ShapeDtypeStruct((t, XBC_WIDTH), f32), jax.ShapeDtypeStruct((t, SSD_HEADS), bf16),
                   jax.ShapeDtypeStruct((8, XBC_WIDTH), f32), jax.ShapeDtypeStruct((1, XBC_WIDTH), f32),
                   jax.ShapeDtypeStruct((1, SSD_HEADS), f32)),
        scratch_shapes=[pltpu.VMEM((tm + 8, XBC_WIDTH), f32)],
        compiler_params=_params("arbitrary"),
    )(xbc_raw, xbc_raw, dxbc_act, ddt, dt_raw, cw, cb, dt_bias)


def _conv4_bwd_data(dco, cw):
    t = dco.shape[0]
    tm = min(TOKEN_TILE, t)

    def body(cur_ref, nxt_ref, w_ref, o_ref, ext_ref):
        i = pl.program_id(0)
        ext_ref[0:tm, :] = cur_ref[...]
        ext_ref[tm:, :] = jnp.where(i == pl.num_programs(0) - 1, 0.0, nxt_ref[...])
        for rs, ls in _blocks(tm, XBC_WIDTH):
            acc = w_ref[0:1, ls] * ext_ref[rs.start + SSD_CONV - 1:rs.stop + SSD_CONV - 1, ls]
            for k in range(1, SSD_CONV):
                acc = acc + w_ref[k:k + 1, ls] * ext_ref[rs.start + SSD_CONV - 1 - k:rs.stop + SSD_CONV - 1 - k, ls]
            o_ref[rs, ls] = acc.astype(bf16)

    return pl.pallas_call(
        body, name="conv4_bwd_data", grid=(t // tm,),
        in_specs=[_row_spec(tm, XBC_WIDTH), _next_halo_spec(tm, 8, XBC_WIDTH, t), _const_spec((8, XBC_WIDTH))],
        out_specs=_row_spec(tm, XBC_WIDTH), out_shape=jax.ShapeDtypeStruct((t, XBC_WIDTH), bf16),
        scratch_shapes=[pltpu.VMEM((tm + 8, XBC_WIDTH), f32)],
        compiler_params=_params("arbitrary"),
    )(dco, dco, cw)


def _in_proj_bwd(dproj, x, dh1, g1, w_in_p, token=None):
    t = x.shape[0]
    tm = min(TOKEN_TILE, t)

    def body(dz_ref, dxbc_ref, ddt_ref, dcv_ref, dcg_ref, x_ref, dh_ref, g_ref, w_ref, dx_ref, acc_ref):
        i = pl.program_id(0)
        _acc_init(i, acc_ref)
        du = jnp.zeros((tm, D_MODEL), f32)
        for (lo, hi), r in zip(IN_SEGS, (dz_ref, dxbc_ref, ddt_ref, dcv_ref, dcg_ref)):
            du = du + _nn(r[...], w_ref[lo:hi, :])
        _, vjp_u = jax.vjp(_rms, x_ref[...], g_ref[...])
        d, dg = vjp_u(du)
        dx_ref[...] = dh_ref[...] + d
        acc_ref[0:1, :] += dg

    row = _row_spec(tm, 1024)
    body, tok, tok_spec = _after_token(body, token)
    return pl.pallas_call(
        body, name="in_proj_bwd", grid=(t // tm,),
        in_specs=tok_spec + [_row_spec(tm, hi - lo) for lo, hi in IN_SEGS] + [row, row, _const_spec((1, 1024)),
                                                                               _const_spec((IN_WIDTH, D_MODEL))],
        out_specs=[row, _const_spec((8, 1024))],
        out_shape=(jax.ShapeDtypeStruct((t, 1024), f32), jax.ShapeDtypeStruct((8, 1024), f32)),
        compiler_params=_params("arbitrary"),
    )(*tok, *dproj, x, dh1, g1, w_in_p)


def _mm_tn(a, b, tk, tn, name, column_blocks=False):
    t, kk = a.shape
    n = b.shape[1]
    tk, tn = min(tk, kk), min(tn, n)

    def body(a_ref, b_ref, o_ref):
        o_ref[...] = _tn(a_ref[...], b_ref[...]).astype(bf16)

    if column_blocks:
        assert tk == kk
        out_spec = pl.BlockSpec((None, tk, tn), lambda i, j: (j, 0, 0))
        out_shape = jax.ShapeDtypeStruct((n // tn, kk, tn), bf16)
    else:
        out_spec = pl.BlockSpec((tk, tn), lambda i, j: (i, j))
        out_shape = jax.ShapeDtypeStruct((kk, n), bf16)
    return pl.pallas_call(
        body, name=name, grid=(kk // tk, n // tn),
        in_specs=[pl.BlockSpec((t, tk), lambda i, j: (0, i)), pl.BlockSpec((t, tn), lambda i, j: (0, j))],
        out_specs=out_spec, out_shape=out_shape,
        compiler_params=_params("arbitrary", "arbitrary"),
    )(a, b)


class _LocalWeights:
    def __init__(self, w):
        self.w = w
        self.sent = {}

    def start_token(self):
        return None

    def weight(self, name, after=()):
        del after
        return self.w[name]

    def send_grads(self, grads):
        self.sent.update(grads)
        return None


def _local_step(x, p, target, s, comm):
    conv_w, conf_w = comm.weight("ssd_conv_w"), comm.weight("conf_dw_w")
    w_in = comm.weight("w_in")
    u, z, xbc_raw, dt_raw, cv, cg = _in_proj(x, s["mix_norm_g"], w_in, token=comm.start_token())
    xbc_act, dt = _ssd_pre(xbc_raw, dt_raw, conv_w, s["ssd_conv_b"], s["dt_bias"])
    y, s_prev = _ssd_scan(xbc_act, dt, s["A_log"], s["D_skip"])
    v2 = _conf_conv(cv, cg, conf_w, s["conf_dw_b"])
    w_out = comm.weight("w_out", after=(y, v2))
    ys, yc, h1, u2 = _out_proj(y, z, v2, x, s["ssd_norm_g"], s["conf_ln_g"], s["conf_ln_b"], w_out, s["mlp_norm_g"])
    w_up = comm.weight("w_up", after=(u2,))
    pre, hs = _mlp_up(u2, w_up)
    w_down = comm.weight("w_down", after=(hs,))
    h2 = _mlp_down(h1, hs, w_down)
    w_gate, w_ple = comm.weight("w_ple_gate", after=(h2,)), comm.weight("w_ple", after=(h2,))
    dh2, dh2b, dwg, dwp, tail_acc = _tail(h2, p, target, s["ple_gate_norm_g"], w_gate, s["b_ple_gate"], w_ple,
                                          s["ple_norm_g"], s["final_norm_g"])
    token = comm.send_grads({"w_ple_gate": dwg, "w_ple": dwp})
    dpre, dh1, dh1b, mlp_acc = _mlp_bwd(dh2, dh2b, pre, h1, s["mlp_norm_g"], w_down, w_up, token=token)
    token = comm.send_grads({
        "w_down": _mm_tn(hs, dh2b, 512, 1024, "dw_down"),
        "w_up": _mm_tn(u2, dpre, 1024, D_FF // N_DEV, "dw_up", column_blocks=True),
        "w_out": [_mm_tn(ys, dh1b, 1024, 512, "dw_out_ssd"), _mm_tn(yc, dh1b, 1024, 512, "dw_out_conf")],
    })
    dy, dz, dv2, opb_acc = _out_proj_bwd(dh1b, y, z, v2, s["ssd_norm_g"], s["conf_ln_g"], s["conf_ln_b"], w_out,
                                         token=token)
    dcv, dcg, dconf_w = _conf_conv_bwd(dv2, cv, cg, conf_w)
    dxbc_act, ddt, d_alog, d_dskip = _ssd_scan_bwd(xbc_act, dt, s_prev, dy, s["A_log"], s["D_skip"])
    dco, ddt_raw, dconv_w, dconv_b, d_dtb = _ssd_pre_bwd(xbc_raw, dxbc_act, ddt, dt_raw, conv_w, s["ssd_conv_b"],
                                                        s["dt_bias"])
    dxbc_raw = _conv4_bwd_data(dco, conv_w)
    dproj = (dz, dxbc_raw, ddt_raw, dcv, dcg)
    token = comm.send_grads({"w_in": [_mm_tn(d, u, 512, 1024, "dw_in_" + n)
                                      for n, d in zip(("z", "xbc", "dt", "cv", "cg"), dproj)]})
    grad_x, inp_acc = _in_proj_bwd(dproj, x, dh1, s["mix_norm_g"], w_in, token=token)
    acc = {"in_proj": inp_acc, "out_proj": opb_acc, "mlp": mlp_acc, "tail": tail_acc, "ssd_conv_b": dconv_b,
           "dt_bias": d_dtb, "A_log": d_alog, "D_skip": d_dskip, "ssd_conv_w": dconv_w, "conf_dw_w": dconf_w}
    return grad_x, acc


def _mesh_pos():
    return lax.axis_index("x"), lax.axis_index("y"), lax.axis_index("c")


def _other_chips(x, y):
    return [(1 - x, y), (x, 1 - y), (1 - x, 1 - y)]


def _all_gather(arrs, name):
    n = len(arrs)

    def body(*refs):
        ins, outs = refs[:n], refs[n:2 * n]
        send_sems, recv_sems, local_sems = refs[2 * n:]
        x, y, c = _mesh_pos()
        me = 4 * x + 2 * y + c
        sibling = (x, y, 1 - c)
        chips = _other_chips(x, y)

        def copy(a, k, block, to, src=None):
            dst = outs[a].at[block]
            return pltpu.make_async_remote_copy(
                src_ref=dst if src is None else src, dst_ref=dst, send_sem=send_sems.at[a, k],
                recv_sem=recv_sems.at[a, k], device_id=to, device_id_type=MESH_ID)

        mine = [pltpu.make_async_copy(ins[a], outs[a].at[me], local_sems.at[a]) for a in range(n)]
        for cp in mine:
            cp.start()
        first = []
        for a in range(n):
            first.append(copy(a, 0, me, sibling, src=ins[a]))
            first += [copy(a, 1 + j, me, (px, py, c), src=ins[a]) for j, (px, py) in enumerate(chips)]
        for cp in first:
            cp.start()
        passed = []
        for j, (px, py) in enumerate(chips):
            for a in range(n):
                blk = 4 * px + 2 * py + c
                copy(a, 1 + j, blk, (x, y, c)).wait_recv()
                cp = copy(a, 4 + j, blk, sibling)
                cp.start()
                passed.append(cp)
        for a in range(n):
            copy(a, 0, 4 * x + 2 * y + (1 - c), (x, y, c)).wait_recv()
        for j, (px, py) in enumerate(chips):
            for a in range(n):
                copy(a, 4 + j, 4 * px + 2 * py + (1 - c), (x, y, c)).wait_recv()
        for cp in first + passed:
            cp.wait_send()
        for cp in mine:
            cp.wait()

    return pl.pallas_call(
        body, name=name,
        in_specs=[HBM_SPEC] * n, out_specs=[HBM_SPEC] * n,
        out_shape=[jax.ShapeDtypeStruct((N_DEV,) + a.shape, a.dtype) for a in arrs],
        scratch_shapes=[pltpu.SemaphoreType.DMA((n, 7)), pltpu.SemaphoreType.DMA((n, 7)), pltpu.SemaphoreType.DMA((n,))],
    )(*arrs)


_PEER_FLIPS = ((0, 0, 1), (1, 0, 0), (0, 1, 0), (1, 1, 0), (1, 0, 1), (0, 1, 1), (1, 1, 1))
SEM_SPEC = pl.BlockSpec(memory_space=pltpu.SEMAPHORE)
ANY_SPEC = pl.BlockSpec(memory_space=pl.ANY)


def _flip(v, d):
    return 1 - v if d else v


def _peers(x, y, c):
    out = []
    for dx, dy, dc in _PEER_FLIPS:
        px, py, pc = _flip(x, dx), _flip(y, dy), _flip(c, dc)
        out.append(((px, py, pc), 4 * px + 2 * py + pc))
    return out


def _exchange_copy(src_ref, land_ref, send_sems, recv_sems, k, peer, peer_block, my_block, by_block, outgoing):
    src = src_ref.at[peer_block] if by_block else src_ref
    dst = land_ref.at[my_block if outgoing else peer_block]
    return pltpu.make_async_remote_copy(src_ref=src, dst_ref=dst, send_sem=send_sems.at[k], recv_sem=recv_sems.at[k],
                                        device_id=peer, device_id_type=MESH_ID)


def _exchange_start(srcs, by_block, name, after):
    n = len(srcs)
    x_pos, y_pos, c_pos = _mesh_pos()
    me_at = 4 * x_pos + 2 * y_pos + c_pos
    if by_block:
        lands = [lax.empty(a.shape, a.dtype) for a in srcs]
    else:
        lands = [lax.dynamic_update_slice(lax.empty((N_DEV,) + a.shape, a.dtype), a[None], (me_at,) + (0,) * a.ndim)
                 for a in srcs]

    def body(*refs):
        src_refs, land_refs = refs[1:1 + n], refs[1 + n:1 + 2 * n]
        outs = refs[1 + 2 * n:]
        send, recv, token = outs[:n], outs[n:2 * n], outs[4 * n]
        x, y, c = _mesh_pos()
        me = 4 * x + 2 * y + c
        for a in range(n):
            for k, (peer, blk) in enumerate(_peers(x, y, c)):
                _exchange_copy(src_refs[a], land_refs[a], send[a], recv[a], k, peer, blk, me, by_block, True).start()
        token[...] = jnp.zeros_like(token)

    sems = [pltpu.SemaphoreType.DMA((N_DEV - 1,))] * (2 * n)
    thru = [pltpu.HBM(a.shape, a.dtype) for a in list(srcs) + list(lands)]
    res = pl.pallas_call(
        body, name=name,
        in_specs=[ANY_SPEC] + [HBM_SPEC] * (2 * n),
        out_specs=[SEM_SPEC] * (2 * n) + [HBM_SPEC] * (2 * n) + [pl.BlockSpec(memory_space=pltpu.VMEM)],
        out_shape=sems + thru + [jax.ShapeDtypeStruct((8, 128), f32)],
        input_output_aliases={1 + i: 2 * n + i for i in range(2 * n)},
        compiler_params=pltpu.CompilerParams(has_side_effects=pltpu.SideEffectType.DATAFLOW_SIDE_EFFECTING),
    )(after, *[pltpu.with_memory_space_constraint(a, pltpu.HBM) for a in list(srcs) + list(lands)])
    states = [(res[2 * n + a], res[3 * n + a], res[a], res[n + a]) for a in range(n)]
    return states, res[4 * n]


def _exchange_wait(states, by_block, name, after):
    n, na = len(states), len(after)

    def body(*refs):
        src_refs, land_refs = refs[:n], refs[n:2 * n]
        send, recv = refs[2 * n:3 * n], refs[3 * n:4 * n]
        x, y, c = _mesh_pos()
        me = 4 * x + 2 * y + c
        for a in range(n):
            for k, (peer, blk) in enumerate(_peers(x, y, c)):
                _exchange_copy(src_refs[a], land_refs[a], send[a], recv[a], k, peer, blk, me, by_block, True).wait_send()
                _exchange_copy(src_refs[a], land_refs[a], send[a], recv[a], k, peer, blk, me, by_block, False).wait_recv()

    srcs, lands = [s[0] for s in states], [s[1] for s in states]
    res = pl.pallas_call(
        body, name=name,
        in_specs=[HBM_SPEC] * (2 * n) + [SEM_SPEC] * (2 * n) + [ANY_SPEC] * na,
        out_specs=[HBM_SPEC] * (2 * n),
        out_shape=[pltpu.HBM(a.shape, a.dtype) for a in srcs + lands],
        input_output_aliases={i: i for i in range(2 * n)},
        compiler_params=pltpu.CompilerParams(has_side_effects=pltpu.SideEffectType.DATAFLOW_SIDE_EFFECTING),
    )(*srcs, *lands, *[s[2] for s in states], *[s[3] for s in states], *after)
    return list(res[n:2 * n]), list(res[:n])


def _adamw_math(w, g, m, v):
    m = ADAM_B1 * m + (1.0 - ADAM_B1) * g
    v = ADAM_B2 * v + (1.0 - ADAM_B2) * (g * g)
    m_hat = m / (1.0 - ADAM_B1 ** ADAM_STEP)
    v_hat = v / (1.0 - ADAM_B2 ** ADAM_STEP)
    delta = -ADAM_LR * (m_hat / (jnp.sqrt(v_hat) + ADAM_EPS) + ADAM_WD * w)
    return delta, m, v


def _adamw_big(parts, mine, me, w, m, v, name):
    rows, cols = w.shape
    tr, tc = (256, cols) if rows % 256 == 0 else (rows, 256)

    def body(me_ref, p_ref, own_ref, w_ref, m_ref, v_ref, g_ref, d_ref, mo_ref, vo_ref):
        own = own_ref[...].astype(f32)
        g = None
        for j in range(N_DEV):
            part = jnp.where(me_ref[0] == j, own, p_ref[j].astype(f32))
            g = part if g is None else g + part
        d, mn, vn = _adamw_math(w_ref[...], g, m_ref[...], v_ref[...])
        g_ref[...] = g
        d_ref[...] = d
        mo_ref[...] = mn
        vo_ref[...] = vn

    tile = pl.BlockSpec((tr, tc), lambda i, j, me_ref: (i, j))
    shp = jax.ShapeDtypeStruct((rows, cols), f32)
    grid_spec = pltpu.PrefetchScalarGridSpec(
        num_scalar_prefetch=1, grid=(rows // tr, cols // tc),
        in_specs=[pl.BlockSpec((N_DEV, tr, tc), lambda i, j, me_ref: (0, i, j)),
                  pl.BlockSpec((None, tr, tc), lambda i, j, me_ref: (me_ref[0], i, j)), tile, tile, tile],
        out_specs=[tile, tile, tile, tile])
    return pl.pallas_call(
        body, name=name, grid_spec=grid_spec, out_shape=(shp, shp, shp, shp),
        compiler_params=_params("arbitrary", "arbitrary"),
    )(me, parts, mine, w, m, v)


PACK_ROWS = 56
_PACK_AT = {
    "mix_norm_g": (0, 0, 1024), "ssd_norm_g": (1, 0, 1024), "conf_ln_g": (2, 0, 1024), "conf_ln_b": (3, 0, 1024),
    "conf_dw_b": (4, 0, 1024), "mlp_norm_g": (5, 0, 1024), "final_norm_g": (7, 0, 1024), "ple_norm_g": (8, 0, 1024),
    "b_ple_gate": (9, 0, 1024), "ple_gate_norm_g": (10, 0, 1024), "dt_bias": (13, 0, 16), "A_log": (13, 128, 16),
    "D_skip": (13, 256, 16),
}
PACK_LOSS_ROW = 6
PACK_CONV_B_ROW = 11
PACK_CONV_W_ROW = 14
PACK_CONF_W_ROW = 24


def _pack_small(acc):
    def body(inp, opb, mlp, tail, cb, dtb, alog, dskip, cw, fw, o_ref):
        o_ref[...] = jnp.zeros_like(o_ref)
        rows = {"mix_norm_g": inp[0:1, :], "mlp_norm_g": mlp[0:1, :],
                "ssd_norm_g": opb[OPB_SSD_G:OPB_SSD_G + 1, :], "conf_ln_g": opb[OPB_LN_G:OPB_LN_G + 1, :],
                "conf_ln_b": opb[OPB_LN_B:OPB_LN_B + 1, :], "conf_dw_b": opb[OPB_CONV_B:OPB_CONV_B + 1, :],
                "final_norm_g": tail[TAIL_FINAL_G:TAIL_FINAL_G + 1, :], "ple_norm_g": tail[TAIL_PLE_G:TAIL_PLE_G + 1, :],
                "b_ple_gate": tail[TAIL_GATE_B:TAIL_GATE_B + 1, :],
                "ple_gate_norm_g": tail[TAIL_GATE_NORM_G:TAIL_GATE_NORM_G + 1, :],
                "dt_bias": dtb[...], "A_log": alog[...], "D_skip": dskip[...]}
        for name, val in rows.items():
            r, lo, width = _PACK_AT[name]
            o_ref[r:r + 1, lo:lo + width] = val
        o_ref[PACK_LOSS_ROW:PACK_LOSS_ROW + 1, :] = tail[TAIL_LOSS:TAIL_LOSS + 1, :]
        o_ref[PACK_CONV_B_ROW:PACK_CONV_B_ROW + 1, :] = cb[:, 0:1024]
        o_ref[PACK_CONV_B_ROW + 1:PACK_CONV_B_ROW + 2, 0:512] = cb[:, 1024:XBC_WIDTH]
        for k in range(SSD_CONV):
            o_ref[PACK_CONV_W_ROW + k:PACK_CONV_W_ROW + k + 1, :] = cw[k:k + 1, 0:1024]
            o_ref[PACK_CONV_W_ROW + SSD_CONV + k:PACK_CONV_W_ROW + SSD_CONV + k + 1, 0:512] = cw[k:k + 1, 1024:XBC_WIDTH]
        o_ref[PACK_CONF_W_ROW:PACK_CONF_W_ROW + 32, :] = fw[...]

    return pl.pallas_call(body, name="pack_small", out_shape=jax.ShapeDtypeStruct((PACK_ROWS, 1024), f32))(
        acc["in_proj"], acc["out_proj"], acc["mlp"], acc["tail"], acc["ssd_conv_b"], acc["dt_bias"], acc["A_log"],
        acc["D_skip"], acc["ssd_conv_w"], acc["conf_dw_w"])


def _small_update(all_small, w, m, v):
    names = _REPLICATED

    def body(all_ref, *refs):
        ins, outs = refs[:3 * len(names)], refs[3 * len(names):]
        s = all_ref[0]
        for j in range(1, N_DEV):
            s = s + all_ref[j]
        outs[0][...] = s[PACK_LOSS_ROW:PACK_LOSS_ROW + 1, 0:1]
        outs[1][...] = jnp.concatenate([s[PACK_CONV_W_ROW:PACK_CONV_W_ROW + SSD_CONV, :],
                                        s[PACK_CONV_W_ROW + SSD_CONV:PACK_CONV_W_ROW + 2 * SSD_CONV, 0:512]], axis=1)
        outs[2][...] = s[PACK_CONF_W_ROW:PACK_CONF_W_ROW + CONF_KERNEL, :]
        for i, name in enumerate(names):
            if name == "ssd_conv_b":
                g = jnp.concatenate([s[PACK_CONV_B_ROW:PACK_CONV_B_ROW + 1, :],
                                     s[PACK_CONV_B_ROW + 1:PACK_CONV_B_ROW + 2, 0:512]], axis=1)
            else:
                r, lo, width = _PACK_AT[name]
                g = s[r:r + 1, lo:lo + width]
            d, mn, vn = _adamw_math(ins[3 * i][...], g, ins[3 * i + 1][...], ins[3 * i + 2][...])
            for o_ref, val in zip(outs[3 + 4 * i:7 + 4 * i], (g, d, mn, vn)):
                o_ref[...] = val

    shapes = [jax.ShapeDtypeStruct((1, 1), f32), jax.ShapeDtypeStruct((SSD_CONV, XBC_WIDTH), f32),
              jax.ShapeDtypeStruct((CONF_KERNEL, D_MODEL), f32)]
    operands = []
    for name in names:
        operands += [w[name], m[name], v[name]]
        shapes += [jax.ShapeDtypeStruct(w[name].shape, f32)] * 4
    res = pl.pallas_call(body, name="small_update", out_shape=shapes)(all_small, *operands)
    per_name = {name: tuple(res[3 + 4 * i:7 + 4 * i]) for i, name in enumerate(names)}
    return res[0], per_name, res[1], res[2]


def _adamw_filters(g, w, m, v):
    n = len(g)

    def body(*refs):
        ins, outs = refs[:4 * n], refs[4 * n:]
        for i in range(n):
            d, mn, vn = _adamw_math(ins[4 * i + 1][...], ins[4 * i][...], ins[4 * i + 2][...], ins[4 * i + 3][...])
            for o_ref, val in zip(outs[3 * i:3 * i + 3], (d, mn, vn)):
                o_ref[...] = val

    operands, shapes = [], []
    for i in range(n):
        operands += [g[i], w[i], m[i], v[i]]
        shapes += [jax.ShapeDtypeStruct(w[i].shape, f32)] * 3
    res = pl.pallas_call(body, name="adamw_filters", out_shape=shapes)(*operands)
    return [tuple(res[3 * i:3 * i + 3]) for i in range(n)]


_REPLICATED = ("mix_norm_g", "ssd_conv_b", "dt_bias", "A_log", "D_skip", "ssd_norm_g", "conf_dw_b", "conf_ln_g",
               "conf_ln_b", "mlp_norm_g", "ple_gate_norm_g", "b_ple_gate", "ple_norm_g", "final_norm_g")
_CONV_WEIGHTS = ("ssd_conv_w", "conf_dw_w")
_WEIGHT_ORDER = ("mix_norm_g", "w_in", "ssd_conv_w", "ssd_conv_b", "dt_bias", "A_log", "D_skip", "ssd_norm_g", "conf_dw_w",
                 "conf_dw_b", "conf_ln_g", "conf_ln_b", "w_out", "mlp_norm_g", "w_up", "w_down", "ple_gate_norm_g",
                 "w_ple_gate", "b_ple_gate", "w_ple", "ple_norm_g", "final_norm_g")


def _blocks_of_columns(a):
    r, c8 = a.shape
    return jnp.transpose(a.reshape(r, N_DEV, c8 // N_DEV), (1, 0, 2))


def _columns_of_blocks(a):
    _, r, c = a.shape
    return jnp.transpose(a, (1, 0, 2)).reshape(r, N_DEV * c)


_LATER = ("w_out", "w_up", "w_down", "w_ple_gate", "w_ple")
_WHOLE = {
    "w_out": lambda a: a.reshape(2048, D_MODEL),
    "w_up": lambda a: a,
    "w_down": lambda a: a.reshape(D_FF, D_MODEL),
    "w_ple_gate": lambda a: a.reshape(D_MODEL, D_MODEL),
    "w_ple": _columns_of_blocks,
}
_BY_BLOCK = {
    "w_in": lambda g: jnp.concatenate(g, axis=0).reshape(N_DEV, IN_WIDTH // N_DEV, D_MODEL),
    "w_out": lambda g: jnp.concatenate(g, axis=0).reshape(N_DEV, 256, D_MODEL),
    "w_up": lambda g: g,
    "w_down": lambda g: g.reshape(N_DEV, 512, D_MODEL),
    "w_ple_gate": lambda g: g.astype(bf16).reshape(N_DEV, 128, D_MODEL),
    "w_ple": lambda g: _blocks_of_columns(g.astype(bf16)),
}


class _StepComm:
    def __init__(self, me, ready, gathers, token):
        self.me, self.ready, self.gathers, self.token = me, ready, gathers, token
        self.sent = []

    def start_token(self):
        return self.token

    def weight(self, name, after=()):
        if name not in self.ready:
            (land,), _ = _exchange_wait([self.gathers[name]], False, "gather_wait_" + name, list(after))
            self.ready[name] = _WHOLE[name](land)
        return self.ready[name]

    def send_grads(self, grads):
        names = list(grads)
        blocks = [_BY_BLOCK[n](grads[n]) for n in names]
        states, self.token = _exchange_start(blocks, True, "scatter_start_" + names[0], self.token)
        self.sent.append((names, states))
        return self.token


def kernel(x, p, mix_norm_g, w_in, ssd_conv_w, ssd_conv_b, dt_bias, A_log, D_skip, ssd_norm_g, conf_dw_w, conf_dw_b, conf_ln_g, conf_ln_b, w_out, mlp_norm_g, w_up, w_down, ple_gate_norm_g, w_ple_gate, b_ple_gate, w_ple, ple_norm_g, final_norm_g, loss_target, m_mix_norm_g, m_w_in, m_ssd_conv_w, m_ssd_conv_b, m_dt_bias, m_A_log, m_D_skip, m_ssd_norm_g, m_conf_dw_w, m_conf_dw_b, m_conf_ln_g, m_conf_ln_b, m_w_out, m_mlp_norm_g, m_w_up, m_w_down, m_ple_gate_norm_g, m_w_ple_gate, m_b_ple_gate, m_w_ple, m_ple_norm_g, m_final_norm_g, v_mix_norm_g, v_w_in, v_ssd_conv_w, v_ssd_conv_b, v_dt_bias, v_A_log, v_D_skip, v_ssd_norm_g, v_conf_dw_w, v_conf_dw_b, v_conf_ln_g, v_conf_ln_b, v_w_out, v_mlp_norm_g, v_w_up, v_w_down, v_ple_gate_norm_g, v_w_ple_gate, v_b_ple_gate, v_w_ple, v_ple_norm_g, v_final_norm_g):
    wts = dict(mix_norm_g=mix_norm_g, w_in=w_in, ssd_conv_w=ssd_conv_w, ssd_conv_b=ssd_conv_b, dt_bias=dt_bias, A_log=A_log,
               D_skip=D_skip, ssd_norm_g=ssd_norm_g, conf_dw_w=conf_dw_w, conf_dw_b=conf_dw_b, conf_ln_g=conf_ln_g,
               conf_ln_b=conf_ln_b, w_out=w_out, mlp_norm_g=mlp_norm_g, w_up=w_up, w_down=w_down,
               ple_gate_norm_g=ple_gate_norm_g, w_ple_gate=w_ple_gate, b_ple_gate=b_ple_gate, w_ple=w_ple,
               ple_norm_g=ple_norm_g, final_norm_g=final_norm_g)
    mom1 = dict(mix_norm_g=m_mix_norm_g, w_in=m_w_in, ssd_conv_w=m_ssd_conv_w, ssd_conv_b=m_ssd_conv_b, dt_bias=m_dt_bias,
                A_log=m_A_log, D_skip=m_D_skip, ssd_norm_g=m_ssd_norm_g, conf_dw_w=m_conf_dw_w, conf_dw_b=m_conf_dw_b,
                conf_ln_g=m_conf_ln_g, conf_ln_b=m_conf_ln_b, w_out=m_w_out, mlp_norm_g=m_mlp_norm_g, w_up=m_w_up,
                w_down=m_w_down, ple_gate_norm_g=m_ple_gate_norm_g, w_ple_gate=m_w_ple_gate, b_ple_gate=m_b_ple_gate,
                w_ple=m_w_ple, ple_norm_g=m_ple_norm_g, final_norm_g=m_final_norm_g)
    mom2 = dict(mix_norm_g=v_mix_norm_g, w_in=v_w_in, ssd_conv_w=v_ssd_conv_w, ssd_conv_b=v_ssd_conv_b, dt_bias=v_dt_bias,
                A_log=v_A_log, D_skip=v_D_skip, ssd_norm_g=v_ssd_norm_g, conf_dw_w=v_conf_dw_w, conf_dw_b=v_conf_dw_b,
                conf_ln_g=v_conf_ln_g, conf_ln_b=v_conf_ln_b, w_out=v_w_out, mlp_norm_g=v_mlp_norm_g, w_up=v_w_up,
                w_down=v_w_down, ple_gate_norm_g=v_ple_gate_norm_g, w_ple_gate=v_w_ple_gate, b_ple_gate=v_b_ple_gate,
                w_ple=v_w_ple, ple_norm_g=v_ple_norm_g, final_norm_g=v_final_norm_g)
    x_pos, y_pos, c_pos = _mesh_pos()
    me = 4 * x_pos + 2 * y_pos + c_pos

    first = _all_gather([wts["w_in"][0].T.astype(bf16), wts["ssd_conv_w"][0], wts["conf_dw_w"][0]], "gather_first")
    ready = {
        "w_in": first[0].reshape(IN_WIDTH, D_MODEL),
        "ssd_conv_w": jnp.pad(_columns_of_blocks(first[1]), ((0, 8 - SSD_CONV), (0, 0))),
        "conf_dw_w": jnp.pad(_columns_of_blocks(first[2]), ((0, 32 - CONF_KERNEL), (0, 0))),
    }
    shards = [wts[n][0].astype(bf16) for n in _LATER]
    states, token = _exchange_start(shards, False, "gather_start", first[1])
    comm = _StepComm(me, ready, dict(zip(_LATER, states)), token)
    small = {n: wts[n].reshape(1, -1) for n in _REPLICATED}

    grad_x, acc = _local_step(x[0], p[0, 0], loss_target[0], small, comm)

    packed = _pack_small(acc)
    (small_state,), _ = _exchange_start([packed], False, "small_start", comm.token)
    grads, delta, new_m, new_v = {}, {}, {}, {}
    me_index = me.astype(jnp.int32).reshape(1)

    def adamw_big(names, sent, after):
        lands, mine = _exchange_wait(sent, True, "scatter_wait_" + names[0], after)
        for n, land, own in zip(names, lands, mine):
            view = (lambda a: a[0].T) if n == "w_in" else (lambda a: a[0])
            back = (lambda a: a.T[None]) if n == "w_in" else (lambda a: a[None])
            out = _adamw_big(land, own, me_index, view(wts[n]), view(mom1[n]), view(mom2[n]), "adamw_" + n)
            grads[n], delta[n], new_m[n], new_v[n] = [back(a) for a in out]

    for names, sent in comm.sent[:-1]:
        adamw_big(names, sent, [grad_x])
    adamw_big(*comm.sent[-1], [delta[n] for n in _LATER])
    (all_small,), _ = _exchange_wait([small_state], False, "small_wait", [delta["w_in"]])
    as_row = lambda d: {n: d[n].reshape(1, -1) for n in _REPLICATED}
    loss, per_name, conv_w_sum, conf_w_sum = _small_update(all_small, as_row(wts), as_row(mom1), as_row(mom2))
    for n in _REPLICATED:
        grads[n], delta[n], new_m[n], new_v[n] = [a.reshape(wts[n].shape) for a in per_name[n]]
    filt_g = [lax.dynamic_slice_in_dim(conv_w_sum, me * 192, 192, axis=1),
              lax.dynamic_slice_in_dim(conf_w_sum, me * 128, 128, axis=1)]
    filt = _adamw_filters(filt_g, [wts[n][0] for n in _CONV_WEIGHTS], [mom1[n][0] for n in _CONV_WEIGHTS],
                          [mom2[n][0] for n in _CONV_WEIGHTS])
    for n, g, (d, mn, vn) in zip(_CONV_WEIGHTS, filt_g, filt):
        grads[n], delta[n], new_m[n], new_v[n] = g[None], d[None], mn[None], vn[None]

    return (loss.reshape(()), grad_x[None], *[grads[n] for n in _WEIGHT_ORDER], *[delta[n] for n in _WEIGHT_ORDER],
            *[new_m[n] for n in _WEIGHT_ORDER], *[new_v[n] for n in _WEIGHT_ORDER])
```

```python
import functools

import jax
import jax.numpy as jnp
from jax import lax
from jax.experimental import pallas as pl
from jax.experimental.pallas import tpu as pltpu

f32 = jnp.float32
bf16 = jnp.bfloat16

EPS = 1e-6
D_MODEL = 1024
SSD_HEADS = 16
SSD_HEAD_DIM = 64
SSD_GROUPS = 2
SSD_STATE = 128
SSD_CONV = 4
CHUNK = 128
XBC_WIDTH = 1536
CONF_KERNEL = 31
D_FF = 4096
PLE_DIM = 256
IN_WIDTH = 4624
N_DEV = 8
SEG_Z = (0, 1024)
SEG_XBC = (1024, 2560)
SEG_DT = (2560, 2576)
SEG_CV = (2576, 3600)
SEG_CG = (3600, 4624)
IN_SEGS = (SEG_Z, SEG_XBC, SEG_DT, SEG_CV, SEG_CG)

ADAM_LR = 0.001
ADAM_B1 = 0.9
ADAM_B2 = 0.999
ADAM_EPS = 1e-08
ADAM_WD = 0.01
ADAM_STEP = 10

VMEM_LIMIT_BYTES = 56 * 1024 * 1024
TOKEN_TILE = 512
SMALL_ROWS = 16

HBM_SPEC = pl.BlockSpec(memory_space=pltpu.HBM)
MESH_ID = pl.DeviceIdType.MESH


def _params(*semantics):
    return pltpu.CompilerParams(dimension_semantics=semantics, vmem_limit_bytes=VMEM_LIMIT_BYTES)


def _nn(a, b):
    return lax.dot_general(a, b, (((1,), (0,)), ((), ())), preferred_element_type=f32)


def _nt(a, b):
    return lax.dot_general(a, b, (((1,), (1,)), ((), ())), preferred_element_type=f32)


def _tn(a, b):
    return lax.dot_general(a, b, (((0,), (0,)), ((), ())), preferred_element_type=f32)


@jax.custom_vjp
def bnn(a, b):
    return _nn(a.astype(bf16), b.astype(bf16))


def _bnn_fwd(a, b):
    ab, bb = a.astype(bf16), b.astype(bf16)
    return _nn(ab, bb), (ab, bb)


def _bnn_bwd(res, g):
    ab, bb = res
    gb = g.astype(bf16)
    return _nt(gb, bb), _tn(ab, gb)


bnn.defvjp(_bnn_fwd, _bnn_bwd)


@jax.custom_vjp
def bnt(a, b):
    return _nt(a.astype(bf16), b.astype(bf16))


def _bnt_fwd(a, b):
    ab, bb = a.astype(bf16), b.astype(bf16)
    return _nt(ab, bb), (ab, bb)


def _bnt_bwd(res, g):
    ab, bb = res
    gb = g.astype(bf16)
    return _nn(gb, bb), _tn(gb, ab)


bnt.defvjp(_bnt_fwd, _bnt_bwd)


@jax.custom_vjp
def btn(a, b):
    return _tn(a.astype(bf16), b.astype(bf16))


def _btn_fwd(a, b):
    ab, bb = a.astype(bf16), b.astype(bf16)
    return _tn(ab, bb), (ab, bb)


def _btn_bwd(res, g):
    ab, bb = res
    gb = g.astype(bf16)
    return _nt(bb, gb), _nn(ab, gb)


btn.defvjp(_btn_fwd, _btn_bwd)

SCAN_CHUNKS = 4
CUMSUM_PASSES = 3
EXPAND_PASSES = 2


def _bf16_pieces(a, passes):
    pieces, rest = [], a
    for _ in range(passes):
        piece = rest.astype(bf16)
        pieces.append(piece)
        rest = rest - piece.astype(f32)
    return pieces


def _make_dot01(form, passes):
    fwd, bwd = {"right": (lambda b, p: _nn(p, b), lambda b, g: _nt(g, b)),
                "left": (lambda b, p: _nn(b, p), lambda b, g: _tn(b, g)),
                "tn": (lambda b, p: _tn(p, b), lambda b, g: _nt(b, g))}[form]

    def run(op, b, v):
        out = None
        for piece in _bf16_pieces(v, passes):
            term = op(b, piece)
            out = term if out is None else out + term
        return out

    @jax.custom_vjp
    def product(b, v):
        return run(fwd, b, v)

    product.defvjp(lambda b, v: (run(fwd, b, v), b), lambda b, g: (jnp.zeros_like(b), run(bwd, b, g)))
    return product


_DOT01 = {(form, passes): _make_dot01(form, passes)
          for form, passes in (("left", CUMSUM_PASSES), ("tn", CUMSUM_PASSES), ("right", EXPAND_PASSES))}


def _dot01(b01, a, form, passes):
    return _DOT01[form, passes](b01.astype(bf16), a)


def _rms(x, g):
    return x * lax.rsqrt(jnp.mean(x * x, axis=-1, keepdims=True) + EPS) * g


def _gated_norm(y, z, g):
    v = y * jax.nn.silu(z)
    half = v.shape[-1] // SSD_GROUPS
    parts = []
    for k in range(SSD_GROUPS):
        vk = v[:, k * half:(k + 1) * half]
        parts.append(vk * lax.rsqrt(jnp.mean(vk * vk, axis=-1, keepdims=True) + EPS))
    return jnp.concatenate(parts, axis=-1) * g


def _ln_silu(v, g, b):
    mu = jnp.mean(v, axis=-1, keepdims=True)
    xc = v - mu
    y = xc * lax.rsqrt(jnp.mean(xc * xc, axis=-1, keepdims=True) + EPS) * g + b
    return jax.nn.silu(y)


def _acc_init(i, *refs):
    @pl.when(i == 0)
    def _():
        for r in refs:
            r[...] = jnp.zeros_like(r)


def _after_token(body, token):
    if token is None:
        return body, [], []

    def body_after(tok_ref, *refs):
        del tok_ref
        body(*refs)

    return body_after, [token], [pl.BlockSpec(memory_space=pl.ANY)]


def _row_spec(tm, n):
    return pl.BlockSpec((tm, n), lambda i: (i, 0))


def _const_spec(shape):
    nd = len(shape)
    return pl.BlockSpec(shape, lambda i: (0,) * nd)


def _prev_halo_spec(tm, halo, n):
    return pl.BlockSpec((halo, n), lambda i: (jnp.maximum(i * (tm // halo) - 1, 0), 0))


def _next_halo_spec(tm, halo, n, t):
    return pl.BlockSpec((halo, n), lambda i: (jnp.minimum((i + 1) * (tm // halo), t // halo - 1), 0))


def _in_proj(x, g1, w_in_p, token=None):
    t = x.shape[0]
    tm = min(TOKEN_TILE, t)

    def body(x_ref, g_ref, w_ref, u_ref, z_ref, xbc_ref, dt_ref, cv_ref, cg_ref):
        u = _rms(x_ref[...], g_ref[...]).astype(bf16)
        u_ref[...] = u
        for (lo, hi), o_ref in zip(IN_SEGS, (z_ref, xbc_ref, dt_ref, cv_ref, cg_ref)):
            o_ref[...] = _nt(u, w_ref[lo:hi, :])

    widths = [hi - lo for lo, hi in IN_SEGS]
    outs = [jax.ShapeDtypeStruct((t, D_MODEL), bf16)] + [jax.ShapeDtypeStruct((t, n), f32) for n in widths]
    body, tok, tok_spec = _after_token(body, token)
    return pl.pallas_call(
        body, name="in_proj", grid=(t // tm,),
        in_specs=tok_spec + [_row_spec(tm, D_MODEL), _const_spec((1, D_MODEL)), _const_spec((IN_WIDTH, D_MODEL))],
        out_specs=[_row_spec(tm, D_MODEL)] + [_row_spec(tm, n) for n in widths],
        out_shape=outs, compiler_params=_params("arbitrary"),
    )(*tok, x, g1, w_in_p)


ROW_BLOCK = 16
LANE_CHUNK = 512


def _blocks(rows, cols):
    return [(slice(r, r + ROW_BLOCK), slice(c, c + LANE_CHUNK))
            for c in range(0, cols, LANE_CHUNK) for r in range(0, rows, ROW_BLOCK)]


def _conv4_block(ext_ref, w_ref, b_ref, rs, ls):
    acc = b_ref[:, ls] + w_ref[0:1, ls] * ext_ref[rs.start + 8 - 3:rs.stop + 8 - 3, ls]
    for k in range(1, SSD_CONV):
        acc = acc + w_ref[k:k + 1, ls] * ext_ref[rs.start + 8 - 3 + k:rs.stop + 8 - 3 + k, ls]
    return acc


def _ssd_pre(xbc_raw, dt_raw, cw, cb, dt_bias):
    t = xbc_raw.shape[0]
    tm = min(TOKEN_TILE, t)

    def body(cur_ref, halo_ref, dt_ref, w_ref, b_ref, dtb_ref, act_ref, dto_ref, ext_ref):
        i = pl.program_id(0)
        ext_ref[0:8, :] = jnp.where(i == 0, 0.0, halo_ref[...])
        ext_ref[8:, :] = cur_ref[...]
        for rs, ls in _blocks(tm, XBC_WIDTH):
            act_ref[rs, ls] = jax.nn.silu(_conv4_block(ext_ref, w_ref, b_ref, rs, ls))
        dto_ref[...] = jax.nn.softplus(dt_ref[...] + dtb_ref[...])

    return pl.pallas_call(
        body, name="ssd_pre", grid=(t // tm,),
        in_specs=[_row_spec(tm, XBC_WIDTH), _prev_halo_spec(tm, 8, XBC_WIDTH), _row_spec(tm, SSD_HEADS),
                  _const_spec((8, XBC_WIDTH)), _const_spec((1, XBC_WIDTH)), _const_spec((1, SSD_HEADS))],
        out_specs=[_row_spec(tm, XBC_WIDTH), _row_spec(tm, SSD_HEADS)],
        out_shape=(jax.ShapeDtypeStruct((t, XBC_WIDTH), f32), jax.ShapeDtypeStruct((t, SSD_HEADS), f32)),
        scratch_shapes=[pltpu.VMEM((tm + 8, XBC_WIDTH), f32)],
        compiler_params=_params("arbitrary"),
    )(xbc_raw, xbc_raw, dt_raw, cw, cb, dt_bias)


def _ssd_consts():
    r = lax.broadcasted_iota(jnp.int32, (CHUNK, CHUNK), 0)
    c = lax.broadcasted_iota(jnp.int32, (CHUNK, CHUNK), 1)
    causal = r >= c
    tril = causal.astype(f32)
    triu = (r <= c).astype(f32)
    hh = lax.broadcasted_iota(jnp.int32, (SSD_HEADS, D_MODEL), 0)
    jj = lax.broadcasted_iota(jnp.int32, (SSD_HEADS, D_MODEL), 1)
    expand = (lax.shift_right_logical(jj, 6) == hh).astype(f32)
    lane = lax.broadcasted_iota(jnp.int32, (1, 2 * SSD_HEAD_DIM), 1)
    m0 = (lane < SSD_HEAD_DIM).astype(f32)
    return causal, tril, triu, expand, m0, 1.0 - m0


def _ssd_chunk(xs, bm, cm, dt, s_prev, a_log, d_skip, consts):
    causal, tril, triu, expand, m0, m1 = consts
    a = dt * (-jnp.exp(a_log))
    cs = _dot01(tril, a, "left", CUMSUM_PASSES)
    cs_t = _dot01(triu, a, "tn", CUMSUM_PASSES)
    cs_last = cs[CHUNK - 1:CHUNK, :]
    per_head = jnp.concatenate([dt, jnp.exp(cs_last - cs), jnp.exp(cs), jnp.broadcast_to(jnp.exp(cs_last), (8, SSD_HEADS)),
                                jnp.broadcast_to(d_skip, (8, SSD_HEADS))], axis=0)
    per_channel = _dot01(expand, per_head, "right", EXPAND_PASSES)
    dt_e, dec_end, dec_start = (per_channel[i * CHUNK:(i + 1) * CHUNK] for i in range(3))
    chunk_dec = per_channel[3 * CHUNK:3 * CHUNK + 1]
    d_e = per_channel[3 * CHUNK + 8:3 * CHUNK + 9]
    xc = xs * dt_e
    x_dec = xc * dec_end
    gw = D_MODEL // SSD_GROUPS
    ys, states = [], []
    for g in range(SSD_GROUPS):
        bg = bm[:, g * SSD_STATE:(g + 1) * SSD_STATE]
        cg = cm[:, g * SSD_STATE:(g + 1) * SSD_STATE]
        sp = s_prev[:, g * gw:(g + 1) * gw]
        states.append(sp * chunk_dec[:, g * gw:(g + 1) * gw] + btn(bg, x_dec[:, g * gw:(g + 1) * gw]))
        y_off = bnn(cg, sp) * dec_start[:, g * gw:(g + 1) * gw]
        scores = bnt(cg, bg)
        pieces = []
        for pr in range(gw // (2 * SSD_HEAD_DIM)):
            lo = g * gw + pr * 2 * SSD_HEAD_DIM
            xp = xc[:, lo:lo + 2 * SSD_HEAD_DIM]
            mats = []
            for h in (lo // SSD_HEAD_DIM, lo // SSD_HEAD_DIM + 1):
                seg = cs[:, h:h + 1] - cs_t[h:h + 1, :]
                mats.append(scores * jnp.exp(jnp.where(causal, seg, -jnp.inf)))
            pieces.append(bnn(jnp.concatenate(mats, axis=1), jnp.concatenate([xp * m0, xp * m1], axis=0)))
        ys.append(jnp.concatenate(pieces, axis=-1) + y_off)
    y = jnp.concatenate(ys, axis=-1) + xs * d_e
    return y, jnp.concatenate(states, axis=-1)


def _ssd_scan(xbc_act, dt, a_log, d_skip):
    t = xbc_act.shape[0]
    nc = t // CHUNK
    rows = SCAN_CHUNKS * CHUNK

    def body(xbc_ref, dt_ref, al_ref, dk_ref, y_ref, sp_ref, state_ref):
        c = pl.program_id(0)

        @pl.when(c == 0)
        def _():
            state_ref[...] = jnp.zeros_like(state_ref)

        state = state_ref[...]
        consts = _ssd_consts()
        for j in range(SCAN_CHUNKS):
            rs = slice(j * CHUNK, (j + 1) * CHUNK)
            sp_ref[j] = state
            y_ref[rs, :], state = _ssd_chunk(xbc_ref[rs, 0:1024], xbc_ref[rs, 1024:1280], xbc_ref[rs, 1280:1536],
                                             dt_ref[rs, :], state, al_ref[...], dk_ref[...], consts)
        state_ref[...] = state

    return pl.pallas_call(
        body, name="ssd_scan", grid=(nc // SCAN_CHUNKS,),
        in_specs=[_row_spec(rows, XBC_WIDTH), _row_spec(rows, SSD_HEADS), _const_spec((1, SSD_HEADS)),
                  _const_spec((1, SSD_HEADS))],
        out_specs=[_row_spec(rows, D_MODEL), pl.BlockSpec((SCAN_CHUNKS, SSD_STATE, D_MODEL), lambda c: (c, 0, 0))],
        out_shape=(jax.ShapeDtypeStruct((t, D_MODEL), f32), jax.ShapeDtypeStruct((nc, SSD_STATE, D_MODEL), f32)),
        scratch_shapes=[pltpu.VMEM((SSD_STATE, D_MODEL), f32)],
        compiler_params=_params("arbitrary"),
    )(xbc_act, dt, a_log, d_skip)


CONV_HALO = 32
CONV_BLOCK = 128
LANE_TILE = 128
SUBLANES = 8


def _fill_glu_ext(ext_ref, cv_ref, cg_ref, hcv_ref, hcg_ref, i):
    ext_ref[0:CONV_HALO, :] = jnp.where(i == 0, 0.0, hcv_ref[...] * jax.nn.sigmoid(hcg_ref[...]))
    ext_ref[CONV_HALO:, :] = cv_ref[...] * jax.nn.sigmoid(cg_ref[...])


def _shifted_copies(src_ref, dst_ref, rows):
    for s in range(1, SUBLANES):
        dst_ref[s, 0:rows, :] = src_ref[pl.ds(s, rows), :]


def _tap_source(src_ref, shifted_ref, offset):
    s = offset % SUBLANES
    return (src_ref if s == 0 else shifted_ref.at[s]), offset - s


def _conv_rows(src_ref, shifted_ref, taps, offsets, lanes, r0, init):
    accs = [init] * (CONV_BLOCK // SUBLANES)
    for k, off in enumerate(offsets):
        ref, base = _tap_source(src_ref, shifted_ref, off)
        for j in range(len(accs)):
            accs[j] = accs[j] + taps[k] * ref[pl.ds(r0 + base + SUBLANES * j, SUBLANES), lanes]
    return accs


def _conf_conv(cv, cg, w, b):
    t = cv.shape[0]
    tm = min(TOKEN_TILE, t)
    ext_rows = tm + CONV_HALO
    offsets = [CONV_HALO - (CONF_KERNEL - 1) + k for k in range(CONF_KERNEL)]

    def body(cv_ref, cg_ref, hcv_ref, hcg_ref, w_ref, b_ref, o_ref, ext_ref, shifted_ref):
        _fill_glu_ext(ext_ref, cv_ref, cg_ref, hcv_ref, hcg_ref, pl.program_id(0))
        _shifted_copies(ext_ref, shifted_ref, ext_rows - SUBLANES)
        for lb in range(1024 // LANE_TILE):
            lanes = slice(lb * LANE_TILE, (lb + 1) * LANE_TILE)
            taps = [jnp.broadcast_to(w_ref[k:k + 1, lanes], (SUBLANES, LANE_TILE)) for k in range(CONF_KERNEL)]
            bias = jnp.broadcast_to(b_ref[:, lanes], (SUBLANES, LANE_TILE))

            def rows(rb, carry, lanes=lanes, taps=taps, bias=bias):
                r0 = pl.multiple_of(rb * CONV_BLOCK, CONV_BLOCK)
                accs = _conv_rows(ext_ref, shifted_ref, taps, offsets, lanes, r0, bias)
                for j, a in enumerate(accs):
                    o_ref[pl.ds(r0 + SUBLANES * j, SUBLANES), lanes] = a
                return carry

            lax.fori_loop(0, tm // CONV_BLOCK, rows, 0)

    return pl.pallas_call(
        body, name="conf_conv", grid=(t // tm,),
        in_specs=[_row_spec(tm, 1024), _row_spec(tm, 1024), _prev_halo_spec(tm, CONV_HALO, 1024),
                  _prev_halo_spec(tm, CONV_HALO, 1024), _const_spec((32, 1024)), _const_spec((1, 1024))],
        out_specs=_row_spec(tm, 1024), out_shape=jax.ShapeDtypeStruct((t, 1024), f32),
        scratch_shapes=[pltpu.VMEM((ext_rows, 1024), f32), pltpu.VMEM((SUBLANES, ext_rows, 1024), f32)],
        compiler_params=_params("arbitrary"),
    )(cv, cg, cv, cg, w, b)


def _out_proj(y, z, v2, x, g_ssd, ln_g, ln_b, w_out, g2):
    t = x.shape[0]
    tm = min(TOKEN_TILE, t)

    def body(y_ref, z_ref, v_ref, x_ref, gs_ref, lg_ref, lb_ref, w_ref, g2_ref, mixed_ref, h1_ref, u2_ref):
        ys = _gated_norm(y_ref[...], z_ref[...], gs_ref[...]).astype(bf16)
        yc = _ln_silu(v_ref[...], lg_ref[...], lb_ref[...]).astype(bf16)
        mixed_ref[:, 0:1024] = ys
        mixed_ref[:, 1024:2048] = yc
        h1 = x_ref[...] + _nn(ys, w_ref[0:1024, :]) + _nn(yc, w_ref[1024:2048, :])
        h1_ref[...] = h1
        u2_ref[...] = _rms(h1, g2_ref[...]).astype(bf16)

    vec = _const_spec((1, 1024))
    row = _row_spec(tm, 1024)
    return pl.pallas_call(
        body, name="out_proj", grid=(t // tm,),
        in_specs=[row, row, row, row, vec, vec, vec, _const_spec((2048, 1024)), vec],
        out_specs=[_row_spec(tm, 2048), row, row],
        out_shape=(jax.ShapeDtypeStruct((t, 2048), bf16),
                   jax.ShapeDtypeStruct((t, 1024), f32), jax.ShapeDtypeStruct((t, 1024), bf16)),
        compiler_params=_params("arbitrary"),
    )(y, z, v2, x, g_ssd, ln_g, ln_b, w_out, g2)


def _mlp_up(u2, w_up):
    t = u2.shape[0]
    tm = min(TOKEN_TILE, t)
    blk = D_FF // N_DEV

    def body(u_ref, w_ref, pre_ref, hs_ref):
        u = u_ref[...]
        for k in range(N_DEV):
            pre = _nn(u, w_ref[k])
            pre_ref[:, k * blk:(k + 1) * blk] = pre.astype(bf16)
            r = jnp.maximum(pre, 0.0)
            hs_ref[:, k * blk:(k + 1) * blk] = (r * r).astype(bf16)

    return pl.pallas_call(
        body, name="mlp_up", grid=(t // tm,),
        in_specs=[_row_spec(tm, D_MODEL), _const_spec((N_DEV, D_MODEL, D_FF // N_DEV))],
        out_specs=[_row_spec(tm, D_FF), _row_spec(tm, D_FF)],
        out_shape=(jax.ShapeDtypeStruct((t, D_FF), bf16), jax.ShapeDtypeStruct((t, D_FF), bf16)),
        compiler_params=_params("arbitrary"),
    )(u2, w_up)


def _mlp_down(h1, hs, w_down):
    t = h1.shape[0]
    tm = min(TOKEN_TILE, t)

    def body(h_ref, hs_ref, w_ref, o_ref):
        o_ref[...] = h_ref[...] + _nn(hs_ref[...], w_ref[...])

    return pl.pallas_call(
        body, name="mlp_down", grid=(t // tm,),
        in_specs=[_row_spec(tm, D_MODEL), _row_spec(tm, D_FF), _const_spec((D_FF, D_MODEL))],
        out_specs=_row_spec(tm, D_MODEL), out_shape=jax.ShapeDtypeStruct((t, D_MODEL), f32),
        compiler_params=_params("arbitrary"),
    )(h1, hs, w_down)


TAIL_LOSS, TAIL_FINAL_G, TAIL_PLE_G, TAIL_GATE_B, TAIL_GATE_NORM_G = 0, 1, 2, 3, 4


def _tail(h2, p, target, g3, w_gate, b_gate, w_ple, ple_g, fin_g):
    t = h2.shape[0]
    tm = min(TOKEN_TILE, t)

    def body(h_ref, p_ref, t_ref, g3_ref, wg_ref, bg_ref, wp_ref, pg_ref, fg_ref,
             dh_ref, dhb_ref, dwg_ref, dwp_ref, acc_ref):
        i = pl.program_id(0)
        _acc_init(i, dwg_ref, dwp_ref, acc_ref)
        h2v = h_ref[...]
        tgt = t_ref[...]
        u3, vjp_u3 = jax.vjp(_rms, h2v, g3_ref[...])
        u3b = u3.astype(bf16)
        pb = p_ref[...].astype(bf16)
        gate_pre = _nn(u3b, wg_ref[...]) + bg_ref[...]
        emb_pre = _nn(pb, wp_ref[...])

        def tail_fn(hv, gp, ep, pg, fg):
            h3 = hv + jax.nn.sigmoid(gp) * _rms(ep, pg)
            err = _rms(h3, fg) - tgt
            return 0.5 * jnp.mean(err * err, axis=-1, keepdims=True)

        loss_tok, vjp_tail = jax.vjp(tail_fn, h2v, gate_pre, emb_pre, pg_ref[...], fg_ref[...])
        dh_a, dgp, dep, dpg, dfg = vjp_tail(jnp.ones_like(loss_tok))
        dgpb = dgp.astype(bf16)
        dh_b, dg3 = vjp_u3(_nt(dgpb, wg_ref[...]))
        dh = dh_a + dh_b
        dh_ref[...] = dh
        dhb_ref[...] = dh.astype(bf16)
        dwg_ref[...] += _tn(u3b, dgpb)
        dwp_ref[...] += _tn(pb, dep.astype(bf16))
        acc_ref[TAIL_LOSS:TAIL_LOSS + 1, :] += jnp.broadcast_to(jnp.sum(loss_tok, axis=0, keepdims=True), (1, 1024))
        acc_ref[TAIL_FINAL_G:TAIL_FINAL_G + 1, :] += dfg
        acc_ref[TAIL_PLE_G:TAIL_PLE_G + 1, :] += dpg
        acc_ref[TAIL_GATE_B:TAIL_GATE_B + 1, :] += jnp.sum(dgp, axis=0, keepdims=True)
        acc_ref[TAIL_GATE_NORM_G:TAIL_GATE_NORM_G + 1, :] += dg3

    vec = _const_spec((1, 1024))
    row = _row_spec(tm, 1024)
    return pl.pallas_call(
        body, name="tail", grid=(t // tm,),
        in_specs=[row, _row_spec(tm, PLE_DIM), row, vec, _const_spec((1024, 1024)), vec, _const_spec((PLE_DIM, 1024)), vec, vec],
        out_specs=[row, row, _const_spec((1024, 1024)), _const_spec((PLE_DIM, 1024)), _const_spec((8, 1024))],
        out_shape=(jax.ShapeDtypeStruct((t, 1024), f32), jax.ShapeDtypeStruct((t, 1024), bf16),
                   jax.ShapeDtypeStruct((1024, 1024), f32), jax.ShapeDtypeStruct((PLE_DIM, 1024), f32),
                   jax.ShapeDtypeStruct((8, 1024), f32)),
        compiler_params=_params("arbitrary"),
    )(h2, p, target, g3, w_gate, b_gate, w_ple, ple_g, fin_g)


def _mlp_bwd(dh2, dh2b, pre, h1, g2, w_down, w_up, token=None):
    t = dh2.shape[0]
    tm = min(TOKEN_TILE, t)
    blk = D_FF // N_DEV

    def body(dh_ref, dhb_ref, pre_ref, h1_ref, g_ref, wd_hbm, wu_hbm, dpre_ref, dh1_ref, dh1b_ref, acc_ref,
             wd_ref, wu_ref):
        i = pl.program_id(0)
        _acc_init(i, acc_ref)

        @pl.when(i == 0)
        def _():
            pltpu.sync_copy(wd_hbm, wd_ref)
            pltpu.sync_copy(wu_hbm, wu_ref)

        dhb = dhb_ref[...]
        du2 = jnp.zeros((tm, D_MODEL), f32)
        for k in range(D_FF // blk):
            dhs = _nt(dhb, wd_ref[k * blk:(k + 1) * blk, :])
            dpre = (dhs * (2.0 * jnp.maximum(pre_ref[:, k * blk:(k + 1) * blk].astype(f32), 0.0))).astype(bf16)
            dpre_ref[:, k * blk:(k + 1) * blk] = dpre
            du2 = du2 + _nt(dpre, wu_ref[k])
        _, vjp_u2 = jax.vjp(_rms, h1_ref[...], g_ref[...])
        d, dg = vjp_u2(du2)
        dh1 = dh_ref[...] + d
        dh1_ref[...] = dh1
        dh1b_ref[...] = dh1.astype(bf16)
        acc_ref[0:1, :] += dg

    row = _row_spec(tm, 1024)
    body, tok, tok_spec = _after_token(body, token)
    return pl.pallas_call(
        body, name="mlp_bwd", grid=(t // tm,),
        in_specs=tok_spec + [row, row, _row_spec(tm, D_FF), row, _const_spec((1, 1024)), HBM_SPEC, HBM_SPEC],
        out_specs=[_row_spec(tm, D_FF), row, row, _const_spec((8, 1024))],
        out_shape=(jax.ShapeDtypeStruct((t, D_FF), bf16), jax.ShapeDtypeStruct((t, 1024), f32),
                   jax.ShapeDtypeStruct((t, 1024), bf16), jax.ShapeDtypeStruct((8, 1024), f32)),
        scratch_shapes=[pltpu.VMEM((D_FF, D_MODEL), bf16), pltpu.VMEM((N_DEV, D_MODEL, D_FF // N_DEV), bf16)],
        compiler_params=_params("arbitrary"),
    )(*tok, dh2, dh2b, pre, h1, g2, w_down, w_up)


OPB_SSD_G, OPB_LN_G, OPB_LN_B, OPB_CONV_B = 0, 1, 2, 3


def _out_proj_bwd(dh1b, y, z, v2, g_ssd, ln_g, ln_b, w_out, token=None):
    t = y.shape[0]
    tm = min(TOKEN_TILE, t)

    def body(dh_ref, y_ref, z_ref, v_ref, gs_ref, lg_ref, lb_ref, w_ref, dy_ref, dz_ref, dv_ref, acc_ref):
        i = pl.program_id(0)
        _acc_init(i, acc_ref)
        dhb = dh_ref[...]
        dmixed = _nt(dhb, w_ref[...])
        dys, dyc = dmixed[:, 0:1024], dmixed[:, 1024:2048]
        _, vjp_g = jax.vjp(_gated_norm, y_ref[...], z_ref[...], gs_ref[...])
        dy, dz, dgs = vjp_g(dys)
        _, vjp_l = jax.vjp(_ln_silu, v_ref[...], lg_ref[...], lb_ref[...])
        dv, dlg, dlb = vjp_l(dyc)
        dy_ref[...] = dy
        dz_ref[...] = dz.astype(bf16)
        dv_ref[...] = dv
        acc_ref[OPB_SSD_G:OPB_SSD_G + 1, :] += dgs
        acc_ref[OPB_LN_G:OPB_LN_G + 1, :] += dlg
        acc_ref[OPB_LN_B:OPB_LN_B + 1, :] += dlb
        acc_ref[OPB_CONV_B:OPB_CONV_B + 1, :] += jnp.sum(dv, axis=0, keepdims=True)

    vec = _const_spec((1, 1024))
    row = _row_spec(tm, 1024)
    body, tok, tok_spec = _after_token(body, token)
    return pl.pallas_call(
        body, name="out_proj_bwd", grid=(t // tm,),
        in_specs=tok_spec + [row, row, row, row, vec, vec, vec, _const_spec((2048, 1024))],
        out_specs=[row, row, row, _const_spec((8, 1024))],
        out_shape=(jax.ShapeDtypeStruct((t, 1024), f32), jax.ShapeDtypeStruct((t, 1024), bf16),
                   jax.ShapeDtypeStruct((t, 1024), f32), jax.ShapeDtypeStruct((8, 1024), f32)),
        compiler_params=_params("arbitrary"),
    )(*tok, dh1b, y, z, v2, g_ssd, ln_g, ln_b, w_out)


def _conf_conv_bwd(dv2, cv, cg, w):
    t = cv.shape[0]
    tm = min(TOKEN_TILE, t)
    ext_rows = tm + CONV_HALO
    offsets = [CONF_KERNEL - 1 - k for k in range(CONF_KERNEL)]

    def body(dv_ref, dvn_ref, cv_ref, cg_ref, w_ref, dcv_ref, dcg_ref, dw_ref, glu_ref, dext_ref, sg_ref, shifted_ref):
        i = pl.program_id(0)
        _acc_init(i, dw_ref)
        sg = jax.nn.sigmoid(cg_ref[...])
        sg_ref[...] = sg
        glu_ref[...] = cv_ref[...] * sg
        dext_ref[0:tm, :] = dv_ref[...]
        dext_ref[tm:, :] = jnp.where(i == pl.num_programs(0) - 1, 0.0, dvn_ref[...])
        _shifted_copies(dext_ref, shifted_ref, ext_rows - SUBLANES)

        for lb in range(1024 // LANE_TILE):
            lanes = slice(lb * LANE_TILE, (lb + 1) * LANE_TILE)

            def rows_w(rb, accs, lanes=lanes):
                r0 = pl.multiple_of(rb * CONV_BLOCK, CONV_BLOCK)
                accs = list(accs)
                for j in range(CONV_BLOCK // SUBLANES):
                    x = glu_ref[pl.ds(r0 + SUBLANES * j, SUBLANES), lanes]
                    for k, off in enumerate(offsets):
                        ref, base = _tap_source(dext_ref, shifted_ref, off)
                        accs[k] = accs[k] + x * ref[pl.ds(r0 + base + SUBLANES * j, SUBLANES), lanes]
                return tuple(accs)

            zero = jnp.zeros((SUBLANES, LANE_TILE), f32)
            accs = lax.fori_loop(0, tm // CONV_BLOCK, rows_w, (zero,) * CONF_KERNEL)
            for k in range(CONF_KERNEL):
                dw_ref[k:k + 1, lanes] += jnp.sum(accs[k], axis=0, keepdims=True)

        for lb in range(1024 // LANE_TILE):
            lanes = slice(lb * LANE_TILE, (lb + 1) * LANE_TILE)
            taps = [jnp.broadcast_to(w_ref[k:k + 1, lanes], (SUBLANES, LANE_TILE)) for k in range(CONF_KERNEL)]

            def rows_x(rb, carry, lanes=lanes, taps=taps):
                r0 = pl.multiple_of(rb * CONV_BLOCK, CONV_BLOCK)
                zero = jnp.zeros((SUBLANES, LANE_TILE), f32)
                dglu = jnp.concatenate(_conv_rows(dext_ref, shifted_ref, taps, offsets, lanes, r0, zero), axis=0)
                sg = sg_ref[pl.ds(r0, CONV_BLOCK), lanes]
                cvv = cv_ref[pl.ds(r0, CONV_BLOCK), lanes]
                dcv_ref[pl.ds(r0, CONV_BLOCK), lanes] = (dglu * sg).astype(bf16)
                dcg_ref[pl.ds(r0, CONV_BLOCK), lanes] = (dglu * cvv * sg * (1.0 - sg)).astype(bf16)
                return carry

            lax.fori_loop(0, tm // CONV_BLOCK, rows_x, 0)

    row = _row_spec(tm, 1024)
    return pl.pallas_call(
        body, name="conf_conv_bwd", grid=(t // tm,),
        in_specs=[row, _next_halo_spec(tm, CONV_HALO, 1024, t), row, row, _const_spec((32, 1024))],
        out_specs=[row, row, _const_spec((32, 1024))],
        out_shape=(jax.ShapeDtypeStruct((t, 1024), bf16), jax.ShapeDtypeStruct((t, 1024), bf16),
                   jax.ShapeDtypeStruct((32, 1024), f32)),
        scratch_shapes=[pltpu.VMEM((tm, 1024), f32), pltpu.VMEM((ext_rows, 1024), f32), pltpu.VMEM((tm, 1024), f32),
                        pltpu.VMEM((SUBLANES, ext_rows, 1024), f32)],
        compiler_params=_params("arbitrary"),
    )(dv2, dv2, cv, cg, w)


def _ssd_scan_bwd(xbc_act, dt, s_prev, dy, a_log, d_skip):
    t = xbc_act.shape[0]
    steps = t // CHUNK // SCAN_CHUNKS
    rows = SCAN_CHUNKS * CHUNK

    def body(xbc_ref, dt_ref, sp_ref, dy_ref, al_ref, dk_ref, dxbc_ref, ddt_ref, dal_ref, ddk_ref, ds_ref):
        i = pl.program_id(0)
        _acc_init(i, ds_ref, dal_ref, ddk_ref)
        consts = _ssd_consts()
        ds = ds_ref[...]
        dal_sum, ddk_sum = dal_ref[...], ddk_ref[...]
        for j in reversed(range(SCAN_CHUNKS)):
            rs = slice(j * CHUNK, (j + 1) * CHUNK)
            _, vjp_c = jax.vjp(
                functools.partial(_ssd_chunk, consts=consts),
                xbc_ref[rs, 0:1024], xbc_ref[rs, 1024:1280], xbc_ref[rs, 1280:1536], dt_ref[rs, :], sp_ref[j],
                al_ref[...], dk_ref[...])
            dxs, dbm, dcm, ddt, ds, dal, ddk = vjp_c((dy_ref[rs, :], ds))
            dxbc_ref[rs, 0:1024] = dxs
            dxbc_ref[rs, 1024:1280] = dbm
            dxbc_ref[rs, 1280:1536] = dcm
            ddt_ref[rs, :] = ddt
            dal_sum, ddk_sum = dal_sum + dal, ddk_sum + ddk
        ds_ref[...] = ds
        dal_ref[...] = dal_sum
        ddk_ref[...] = ddk_sum

    rev = lambda i: (steps - 1 - i, 0)
    return pl.pallas_call(
        body, name="ssd_scan_bwd", grid=(steps,),
        in_specs=[pl.BlockSpec((rows, XBC_WIDTH), rev), pl.BlockSpec((rows, SSD_HEADS), rev),
                  pl.BlockSpec((SCAN_CHUNKS, SSD_STATE, D_MODEL), lambda i: (steps - 1 - i, 0, 0)),
                  pl.BlockSpec((rows, D_MODEL), rev), _const_spec((1, SSD_HEADS)), _const_spec((1, SSD_HEADS))],
        out_specs=[pl.BlockSpec((rows, XBC_WIDTH), rev), pl.BlockSpec((rows, SSD_HEADS), rev),
                   _const_spec((1, SSD_HEADS)), _const_spec((1, SSD_HEADS))],
        out_shape=(jax.ShapeDtypeStruct((t, XBC_WIDTH), f32), jax.ShapeDtypeStruct((t, SSD_HEADS), f32),
                   jax.ShapeDtypeStruct((1, SSD_HEADS), f32), jax.ShapeDtypeStruct((1, SSD_HEADS), f32)),
        scratch_shapes=[pltpu.VMEM((SSD_STATE, D_MODEL), f32)],
        compiler_params=_params("arbitrary"),
    )(xbc_act, dt, s_prev, dy, a_log, d_skip)


def _ssd_pre_bwd(xbc_raw, dxbc_act, ddt, dt_raw, cw, cb, dt_bias):
    t = xbc_raw.shape[0]
    tm = min(TOKEN_TILE, t)

    def body(cur_ref, halo_ref, dact_ref, ddt_ref, dtr_ref, w_ref, b_ref, dtb_ref,
             dco_ref, ddtr_ref, dw_ref, db_ref, ddtb_ref, ext_ref):
        i = pl.program_id(0)
        _acc_init(i, dw_ref, db_ref, ddtb_ref)
        ext_ref[0:8, :] = jnp.where(i == 0, 0.0, halo_ref[...])
        ext_ref[8:, :] = cur_ref[...]
        fold = lambda a: a[0:8] + a[8:ROW_BLOCK]
        for c in range(0, XBC_WIDTH, LANE_CHUNK):
            ls = slice(c, c + LANE_CHUNK)
            part_b = jnp.zeros((8, LANE_CHUNK), f32)
            part_w = [jnp.zeros((8, LANE_CHUNK), f32)] * SSD_CONV
            for r in range(0, tm, ROW_BLOCK):
                rs = slice(r, r + ROW_BLOCK)
                co = _conv4_block(ext_ref, w_ref, b_ref, rs, ls)
                sg = jax.nn.sigmoid(co)
                dco = dact_ref[rs, ls] * sg * (1.0 + co * (1.0 - sg))
                dco_ref[rs, ls] = dco
                part_b = part_b + fold(dco)
                part_w = [pw + fold(dco * ext_ref[r + 8 - 3 + k:r + ROW_BLOCK + 8 - 3 + k, ls])
                          for k, pw in enumerate(part_w)]
            db_ref[:, ls] += jnp.sum(part_b, axis=0, keepdims=True)
            for k in range(SSD_CONV):
                dw_ref[k:k + 1, ls] += jnp.sum(part_w[k], axis=0, keepdims=True)
        ddtr = ddt_ref[...] * jax.nn.sigmoid(dtr_ref[...] + dtb_ref[...])
        ddtb_ref[...] += jnp.sum(ddtr, axis=0, keepdims=True)
        ddtr_ref[...] = ddtr.astype(bf16)

    return pl.pallas_call(
        body, name="ssd_pre_bwd", grid=(t // tm,),
        in_specs=[_row_spec(tm, XBC_WIDTH), _prev_halo_spec(tm, 8, XBC_WIDTH), _row_spec(tm, XBC_WIDTH),
                  _row_spec(tm, SSD_HEADS), _row_spec(tm, SSD_HEADS), _const_spec((8, XBC_WIDTH)),
                  _const_spec((1, XBC_WIDTH)), _const_spec((1, SSD_HEADS))],
        out_specs=[_row_spec(tm, XBC_WIDTH), _row_spec(tm, SSD_HEADS), _const_spec((8, XBC_WIDTH)),
                   _const_spec((1, XBC_WIDTH)), _const_spec((1, SSD_HEADS))],
        out_shape=(jax.ShapeDtypeStruct((t, XBC_WIDTH), f32), jax.ShapeDtypeStruct((t, SSD_HEADS), bf16),
                   jax.ShapeDtypeStruct((8, XBC_WIDTH), f32), jax.ShapeDtypeStruct((1, XBC_WIDTH), f32),
                   jax.ShapeDtypeStruct((1, SSD_HEADS), f32)),
        scratch_shapes=[pltpu.VMEM((tm + 8, XBC_WIDTH), f32)],
        compiler_params=_params("arbitrary"),
    )(xbc_raw, xbc_raw, dxbc_act, ddt, dt_raw, cw, cb, dt_bias)


def _conv4_bwd_data(dco, cw):
    t = dco.shape[0]
    tm = min(TOKEN_TILE, t)

    def body(cur_ref, nxt_ref, w_ref, o_ref, ext_ref):
        i = pl.program_id(0)
        ext_ref[0:tm, :] = cur_ref[...]
        ext_ref[tm:, :] = jnp.where(i == pl.num_programs(0) - 1, 0.0, nxt_ref[...])
        for rs, ls in _blocks(tm, XBC_WIDTH):
            acc = w_ref[0:1, ls] * ext_ref[rs.start + SSD_CONV - 1:rs.stop + SSD_CONV - 1, ls]
            for k in range(1, SSD_CONV):
                acc = acc + w_ref[k:k + 1, ls] * ext_ref[rs.start + SSD_CONV - 1 - k:rs.stop + SSD_CONV - 1 - k, ls]
            o_ref[rs, ls] = acc.astype(bf16)

    return pl.pallas_call(
        body, name="conv4_bwd_data", grid=(t // tm,),
        in_specs=[_row_spec(tm, XBC_WIDTH), _next_halo_spec(tm, 8, XBC_WIDTH, t), _const_spec((8, XBC_WIDTH))],
        out_specs=_row_spec(tm, XBC_WIDTH), out_shape=jax.ShapeDtypeStruct((t, XBC_WIDTH), bf16),
        scratch_shapes=[pltpu.VMEM((tm + 8, XBC_WIDTH), f32)],
        compiler_params=_params("arbitrary"),
    )(dco, dco, cw)


def _in_proj_bwd(dproj, x, dh1, g1, w_in_p, token=None):
    t = x.shape[0]
    tm = min(TOKEN_TILE, t)

    def body(dz_ref, dxbc_ref, ddt_ref, dcv_ref, dcg_ref, x_ref, dh_ref, g_ref, w_ref, dx_ref, acc_ref):
        i = pl.program_id(0)
        _acc_init(i, acc_ref)
        du = jnp.zeros((tm, D_MODEL), f32)
        for (lo, hi), r in zip(IN_SEGS, (dz_ref, dxbc_ref, ddt_ref, dcv_ref, dcg_ref)):
            du = du + _nn(r[...], w_ref[lo:hi, :])
        _, vjp_u = jax.vjp(_rms, x_ref[...], g_ref[...])
        d, dg = vjp_u(du)
        dx_ref[...] = dh_ref[...] + d
        acc_ref[0:1, :] += dg

    row = _row_spec(tm, 1024)
    body, tok, tok_spec = _after_token(body, token)
    return pl.pallas_call(
        body, name="in_proj_bwd", grid=(t // tm,),
        in_specs=tok_spec + [_row_spec(tm, hi - lo) for lo, hi in IN_SEGS] + [row, row, _const_spec((1, 1024)),
                                                                               _const_spec((IN_WIDTH, D_MODEL))],
        out_specs=[row, _const_spec((8, 1024))],
        out_shape=(jax.ShapeDtypeStruct((t, 1024), f32), jax.ShapeDtypeStruct((8, 1024), f32)),
        compiler_params=_params("arbitrary"),
    )(*tok, *dproj, x, dh1, g1, w_in_p)


def _mm_tn(a, b, tk, tn, name, column_blocks=False):
    t, kk = a.shape
    n = b.shape[1]
    tk, tn = min(tk, kk), min(tn, n)

    def body(a_ref, b_ref, o_ref):
        o_ref[...] = _tn(a_ref[...], b_ref[...]).astype(bf16)

    if column_blocks:
        assert tk == kk
        out_spec = pl.BlockSpec((None, tk, tn), lambda i, j: (j, 0, 0))
        out_shape = jax.ShapeDtypeStruct((n // tn, kk, tn), bf16)
    else:
        out_spec = pl.BlockSpec((tk, tn), lambda i, j: (i, j))
        out_shape = jax.ShapeDtypeStruct((kk, n), bf16)
    return pl.pallas_call(
        body, name=name, grid=(kk // tk, n // tn),
        in_specs=[pl.BlockSpec((t, tk), lambda i, j: (0, i)), pl.BlockSpec((t, tn), lambda i, j: (0, j))],
        out_specs=out_spec, out_shape=out_shape,
        compiler_params=_params("arbitrary", "arbitrary"),
    )(a, b)


class _LocalWeights:
    def __init__(self, w):
        self.w = w
        self.sent = {}

    def start_token(self):
        return None

    def weight(self, name, after=()):
        del after
        return self.w[name]

    def send_grads(self, grads):
        self.sent.update(grads)
        return None


def _local_step(x, p, target, s, comm):
    conv_w, conf_w = comm.weight("ssd_conv_w"), comm.weight("conf_dw_w")
    w_in = comm.weight("w_in")
    u, z, xbc_raw, dt_raw, cv, cg = _in_proj(x, s["mix_norm_g"], w_in, token=comm.start_token())
    xbc_act, dt = _ssd_pre(xbc_raw, dt_raw, conv_w, s["ssd_conv_b"], s["dt_bias"])
    y, s_prev = _ssd_scan(xbc_act, dt, s["A_log"], s["D_skip"])
    v2 = _conf_conv(cv, cg, conf_w, s["conf_dw_b"])
    w_out = comm.weight("w_out", after=(y, v2))
    mixed, h1, u2 = _out_proj(y, z, v2, x, s["ssd_norm_g"], s["conf_ln_g"], s["conf_ln_b"], w_out, s["mlp_norm_g"])
    w_up = comm.weight("w_up", after=(u2,))
    pre, hs = _mlp_up(u2, w_up)
    w_down = comm.weight("w_down", after=(hs,))
    h2 = _mlp_down(h1, hs, w_down)
    w_gate, w_ple = comm.weight("w_ple_gate", after=(h2,)), comm.weight("w_ple", after=(h2,))
    dh2, dh2b, dwg, dwp, tail_acc = _tail(h2, p, target, s["ple_gate_norm_g"], w_gate, s["b_ple_gate"], w_ple,
                                          s["ple_norm_g"], s["final_norm_g"])
    token = comm.send_grads({"w_ple_gate": dwg, "w_ple": dwp})
    dpre, dh1, dh1b, mlp_acc = _mlp_bwd(dh2, dh2b, pre, h1, s["mlp_norm_g"], w_down, w_up, token=token)
    token = comm.send_grads({
        "w_down": _mm_tn(hs, dh2b, 512, 1024, "dw_down"),
        "w_up": _mm_tn(u2, dpre, 1024, D_FF // N_DEV, "dw_up", column_blocks=True),
        "w_out": _mm_tn(mixed, dh1b, 512, 1024, "dw_out"),
    })
    dy, dz, dv2, opb_acc = _out_proj_bwd(dh1b, y, z, v2, s["ssd_norm_g"], s["conf_ln_g"], s["conf_ln_b"], w_out,
                                         token=token)
    dcv, dcg, dconf_w = _conf_conv_bwd(dv2, cv, cg, conf_w)
    dxbc_act, ddt, d_alog, d_dskip = _ssd_scan_bwd(xbc_act, dt, s_prev, dy, s["A_log"], s["D_skip"])
    dco, ddt_raw, dconv_w, dconv_b, d_dtb = _ssd_pre_bwd(xbc_raw, dxbc_act, ddt, dt_raw, conv_w, s["ssd_conv_b"],
                                                        s["dt_bias"])
    dxbc_raw = _conv4_bwd_data(dco, conv_w)
    dproj = (dz, dxbc_raw, ddt_raw, dcv, dcg)
    token = comm.send_grads({"w_in": [_mm_tn(d, u, 512, 1024, "dw_in_" + n)
                                      for n, d in zip(("z", "xbc", "dt", "cv", "cg"), dproj)]})
    grad_x, inp_acc = _in_proj_bwd(dproj, x, dh1, s["mix_norm_g"], w_in, token=token)
    acc = {"in_proj": inp_acc, "out_proj": opb_acc, "mlp": mlp_acc, "tail": tail_acc, "ssd_conv_b": dconv_b,
           "dt_bias": d_dtb, "A_log": d_alog, "D_skip": d_dskip, "ssd_conv_w": dconv_w, "conf_dw_w": dconf_w}
    return grad_x, acc


def _mesh_pos():
    return lax.axis_index("x"), lax.axis_index("y"), lax.axis_index("c")


def _other_chips(x, y):
    return [(1 - x, y), (x, 1 - y), (1 - x, 1 - y)]


def _all_gather(arrs, name):
    n = len(arrs)

    def body(*refs):
        ins, outs = refs[:n], refs[n:2 * n]
        send_sems, recv_sems, local_sems = refs[2 * n:]
        x, y, c = _mesh_pos()
        me = 4 * x + 2 * y + c
        sibling = (x, y, 1 - c)
        chips = _other_chips(x, y)

        def copy(a, k, block, to, src=None):
            dst = outs[a].at[block]
            return pltpu.make_async_remote_copy(
                src_ref=dst if src is None else src, dst_ref=dst, send_sem=send_sems.at[a, k],
                recv_sem=recv_sems.at[a, k], device_id=to, device_id_type=MESH_ID)

        mine = [pltpu.make_async_copy(ins[a], outs[a].at[me], local_sems.at[a]) for a in range(n)]
        for cp in mine:
            cp.start()
        first = []
        for a in range(n):
            first.append(copy(a, 0, me, sibling, src=ins[a]))
            first += [copy(a, 1 + j, me, (px, py, c), src=ins[a]) for j, (px, py) in enumerate(chips)]
        for cp in first:
            cp.start()
        passed = []
        for j, (px, py) in enumerate(chips):
            for a in range(n):
                blk = 4 * px + 2 * py + c
                copy(a, 1 + j, blk, (x, y, c)).wait_recv()
                cp = copy(a, 4 + j, blk, sibling)
                cp.start()
                passed.append(cp)
        for a in range(n):
            copy(a, 0, 4 * x + 2 * y + (1 - c), (x, y, c)).wait_recv()
        for j, (px, py) in enumerate(chips):
            for a in range(n):
                copy(a, 4 + j, 4 * px + 2 * py + (1 - c), (x, y, c)).wait_recv()
        for cp in first + passed:
            cp.wait_send()
        for cp in mine:
            cp.wait()

    return pl.pallas_call(
        body, name=name,
        in_specs=[HBM_SPEC] * n, out_specs=[HBM_SPEC] * n,
        out_shape=[jax.ShapeDtypeStruct((N_DEV,) + a.shape, a.dtype) for a in arrs],
        scratch_shapes=[pltpu.SemaphoreType.DMA((n, 7)), pltpu.SemaphoreType.DMA((n, 7)), pltpu.SemaphoreType.DMA((n,))],
    )(*arrs)


_PEER_FLIPS = ((0, 0, 1), (1, 0, 0), (0, 1, 0), (1, 1, 0), (1, 0, 1), (0, 1, 1), (1, 1, 1))
SEM_SPEC = pl.BlockSpec(memory_space=pltpu.SEMAPHORE)
ANY_SPEC = pl.BlockSpec(memory_space=pl.ANY)


def _flip(v, d):
    return 1 - v if d else v


def _peers(x, y, c):
    out = []
    for dx, dy, dc in _PEER_FLIPS:
        px, py, pc = _flip(x, dx), _flip(y, dy), _flip(c, dc)
        out.append(((px, py, pc), 4 * px + 2 * py + pc))
    return out


def _exchange_copy(src_ref, land_ref, send_sems, recv_sems, k, peer, peer_block, my_block, by_block, outgoing):
    src = src_ref.at[peer_block] if by_block else src_ref
    dst = land_ref.at[my_block if outgoing else peer_block]
    return pltpu.make_async_remote_copy(src_ref=src, dst_ref=dst, send_sem=send_sems.at[k], recv_sem=recv_sems.at[k],
                                        device_id=peer, device_id_type=MESH_ID)


def _exchange_start(srcs, by_block, name, after):
    n = len(srcs)
    x_pos, y_pos, c_pos = _mesh_pos()
    me_at = 4 * x_pos + 2 * y_pos + c_pos
    if by_block:
        lands = [lax.empty(a.shape, a.dtype) for a in srcs]
    else:
        lands = [lax.dynamic_update_slice(lax.empty((N_DEV,) + a.shape, a.dtype), a[None], (me_at,) + (0,) * a.ndim)
                 for a in srcs]

    def body(*refs):
        src_refs, land_refs = refs[1:1 + n], refs[1 + n:1 + 2 * n]
        outs = refs[1 + 2 * n:]
        send, recv, token = outs[:n], outs[n:2 * n], outs[4 * n]
        x, y, c = _mesh_pos()
        me = 4 * x + 2 * y + c
        for a in range(n):
            for k, (peer, blk) in enumerate(_peers(x, y, c)):
                _exchange_copy(src_refs[a], land_refs[a], send[a], recv[a], k, peer, blk, me, by_block, True).start()
        token[...] = jnp.zeros_like(token)

    sems = [pltpu.SemaphoreType.DMA((N_DEV - 1,))] * (2 * n)
    thru = [pltpu.HBM(a.shape, a.dtype) for a in list(srcs) + list(lands)]
    res = pl.pallas_call(
        body, name=name,
        in_specs=[ANY_SPEC] + [HBM_SPEC] * (2 * n),
        out_specs=[SEM_SPEC] * (2 * n) + [HBM_SPEC] * (2 * n) + [pl.BlockSpec(memory_space=pltpu.VMEM)],
        out_shape=sems + thru + [jax.ShapeDtypeStruct((8, 128), f32)],
        input_output_aliases={1 + i: 2 * n + i for i in range(2 * n)},
        compiler_params=pltpu.CompilerParams(has_side_effects=pltpu.SideEffectType.DATAFLOW_SIDE_EFFECTING),
    )(after, *[pltpu.with_memory_space_constraint(a, pltpu.HBM) for a in list(srcs) + list(lands)])
    states = [(res[2 * n + a], res[3 * n + a], res[a], res[n + a]) for a in range(n)]
    return states, res[4 * n]


def _exchange_wait(states, by_block, name, after):
    n, na = len(states), len(after)

    def body(*refs):
        src_refs, land_refs = refs[:n], refs[n:2 * n]
        send, recv = refs[2 * n:3 * n], refs[3 * n:4 * n]
        x, y, c = _mesh_pos()
        me = 4 * x + 2 * y + c
        for a in range(n):
            for k, (peer, blk) in enumerate(_peers(x, y, c)):
                _exchange_copy(src_refs[a], land_refs[a], send[a], recv[a], k, peer, blk, me, by_block, True).wait_send()
                _exchange_copy(src_refs[a], land_refs[a], send[a], recv[a], k, peer, blk, me, by_block, False).wait_recv()

    srcs, lands = [s[0] for s in states], [s[1] for s in states]
    res = pl.pallas_call(
        body, name=name,
        in_specs=[HBM_SPEC] * (2 * n) + [SEM_SPEC] * (2 * n) + [ANY_SPEC] * na,
        out_specs=[HBM_SPEC] * (2 * n),
        out_shape=[pltpu.HBM(a.shape, a.dtype) for a in srcs + lands],
        input_output_aliases={i: i for i in range(2 * n)},
        compiler_params=pltpu.CompilerParams(has_side_effects=pltpu.SideEffectType.DATAFLOW_SIDE_EFFECTING),
    )(*srcs, *lands, *[s[2] for s in states], *[s[3] for s in states], *after)
    return list(res[n:2 * n]), list(res[:n])


def _adamw_math(w, g, m, v):
    m = ADAM_B1 * m + (1.0 - ADAM_B1) * g
    v = ADAM_B2 * v + (1.0 - ADAM_B2) * (g * g)
    m_hat = m / (1.0 - ADAM_B1 ** ADAM_STEP)
    v_hat = v / (1.0 - ADAM_B2 ** ADAM_STEP)
    delta = -ADAM_LR * (m_hat / (jnp.sqrt(v_hat) + ADAM_EPS) + ADAM_WD * w)
    return delta, m, v


def _adamw_big(parts, mine, me, w, m, v, name):
    rows, cols = w.shape
    tr, tc = (256, cols) if rows % 256 == 0 else (rows, 256)

    def body(me_ref, p_ref, own_ref, w_ref, m_ref, v_ref, g_ref, d_ref, mo_ref, vo_ref):
        own = own_ref[...].astype(f32)
        g = None
        for j in range(N_DEV):
            part = jnp.where(me_ref[0] == j, own, p_ref[j].astype(f32))
            g = part if g is None else g + part
        d, mn, vn = _adamw_math(w_ref[...], g, m_ref[...], v_ref[...])
        g_ref[...] = g
        d_ref[...] = d
        mo_ref[...] = mn
        vo_ref[...] = vn

    tile = pl.BlockSpec((tr, tc), lambda i, j, me_ref: (i, j))
    shp = jax.ShapeDtypeStruct((rows, cols), f32)
    grid_spec = pltpu.PrefetchScalarGridSpec(
        num_scalar_prefetch=1, grid=(rows // tr, cols // tc),
        in_specs=[pl.BlockSpec((N_DEV, tr, tc), lambda i, j, me_ref: (0, i, j)),
                  pl.BlockSpec((None, tr, tc), lambda i, j, me_ref: (me_ref[0], i, j)), tile, tile, tile],
        out_specs=[tile, tile, tile, tile])
    return pl.pallas_call(
        body, name=name, grid_spec=grid_spec, out_shape=(shp, shp, shp, shp),
        compiler_params=_params("arbitrary", "arbitrary"),
    )(me, parts, mine, w, m, v)


PACK_ROWS = 56
_PACK_AT = {
    "mix_norm_g": (0, 0, 1024), "ssd_norm_g": (1, 0, 1024), "conf_ln_g": (2, 0, 1024), "conf_ln_b": (3, 0, 1024),
    "conf_dw_b": (4, 0, 1024), "mlp_norm_g": (5, 0, 1024), "final_norm_g": (7, 0, 1024), "ple_norm_g": (8, 0, 1024),
    "b_ple_gate": (9, 0, 1024), "ple_gate_norm_g": (10, 0, 1024), "dt_bias": (13, 0, 16), "A_log": (13, 128, 16),
    "D_skip": (13, 256, 16),
}
PACK_LOSS_ROW = 6
PACK_CONV_B_ROW = 11
PACK_CONV_W_ROW = 14
PACK_CONF_W_ROW = 24


def _pack_small(acc):
    def body(inp, opb, mlp, tail, cb, dtb, alog, dskip, cw, fw, o_ref):
        o_ref[...] = jnp.zeros_like(o_ref)
        rows = {"mix_norm_g": inp[0:1, :], "mlp_norm_g": mlp[0:1, :],
                "ssd_norm_g": opb[OPB_SSD_G:OPB_SSD_G + 1, :], "conf_ln_g": opb[OPB_LN_G:OPB_LN_G + 1, :],
                "conf_ln_b": opb[OPB_LN_B:OPB_LN_B + 1, :], "conf_dw_b": opb[OPB_CONV_B:OPB_CONV_B + 1, :],
                "final_norm_g": tail[TAIL_FINAL_G:TAIL_FINAL_G + 1, :], "ple_norm_g": tail[TAIL_PLE_G:TAIL_PLE_G + 1, :],
                "b_ple_gate": tail[TAIL_GATE_B:TAIL_GATE_B + 1, :],
                "ple_gate_norm_g": tail[TAIL_GATE_NORM_G:TAIL_GATE_NORM_G + 1, :],
                "dt_bias": dtb[...], "A_log": alog[...], "D_skip": dskip[...]}
        for name, val in rows.items():
            r, lo, width = _PACK_AT[name]
            o_ref[r:r + 1, lo:lo + width] = val
        o_ref[PACK_LOSS_ROW:PACK_LOSS_ROW + 1, :] = tail[TAIL_LOSS:TAIL_LOSS + 1, :]
        o_ref[PACK_CONV_B_ROW:PACK_CONV_B_ROW + 1, :] = cb[:, 0:1024]
        o_ref[PACK_CONV_B_ROW + 1:PACK_CONV_B_ROW + 2, 0:512] = cb[:, 1024:XBC_WIDTH]
        for k in range(SSD_CONV):
            o_ref[PACK_CONV_W_ROW + k:PACK_CONV_W_ROW + k + 1, :] = cw[k:k + 1, 0:1024]
            o_ref[PACK_CONV_W_ROW + SSD_CONV + k:PACK_CONV_W_ROW + SSD_CONV + k + 1, 0:512] = cw[k:k + 1, 1024:XBC_WIDTH]
        o_ref[PACK_CONF_W_ROW:PACK_CONF_W_ROW + 32, :] = fw[...]

    return pl.pallas_call(body, name="pack_small", out_shape=jax.ShapeDtypeStruct((PACK_ROWS, 1024), f32))(
        acc["in_proj"], acc["out_proj"], acc["mlp"], acc["tail"], acc["ssd_conv_b"], acc["dt_bias"], acc["A_log"],
        acc["D_skip"], acc["ssd_conv_w"], acc["conf_dw_w"])


def _small_update(all_small, w, m, v):
    names = _REPLICATED

    def body(all_ref, *refs):
        ins, outs = refs[:3 * len(names)], refs[3 * len(names):]
        s = all_ref[0]
        for j in range(1, N_DEV):
            s = s + all_ref[j]
        outs[0][...] = s[PACK_LOSS_ROW:PACK_LOSS_ROW + 1, 0:1]
        outs[1][...] = jnp.concatenate([s[PACK_CONV_W_ROW:PACK_CONV_W_ROW + SSD_CONV, :],
                                        s[PACK_CONV_W_ROW + SSD_CONV:PACK_CONV_W_ROW + 2 * SSD_CONV, 0:512]], axis=1)
        outs[2][...] = s[PACK_CONF_W_ROW:PACK_CONF_W_ROW + CONF_KERNEL, :]
        for i, name in enumerate(names):
            if name == "ssd_conv_b":
                g = jnp.concatenate([s[PACK_CONV_B_ROW:PACK_CONV_B_ROW + 1, :],
                                     s[PACK_CONV_B_ROW + 1:PACK_CONV_B_ROW + 2, 0:512]], axis=1)
            else:
                r, lo, width = _PACK_AT[name]
                g = s[r:r + 1, lo:lo + width]
            d, mn, vn = _adamw_math(ins[3 * i][...], g, ins[3 * i + 1][...], ins[3 * i + 2][...])
            for o_ref, val in zip(outs[3 + 4 * i:7 + 4 * i], (g, d, mn, vn)):
                o_ref[...] = val

    shapes = [jax.ShapeDtypeStruct((1, 1), f32), jax.ShapeDtypeStruct((SSD_CONV, XBC_WIDTH), f32),
              jax.ShapeDtypeStruct((CONF_KERNEL, D_MODEL), f32)]
    operands = []
    for name in names:
        operands += [w[name], m[name], v[name]]
        shapes += [jax.ShapeDtypeStruct(w[name].shape, f32)] * 4
    res = pl.pallas_call(body, name="small_update", out_shape=shapes)(all_small, *operands)
    per_name = {name: tuple(res[3 + 4 * i:7 + 4 * i]) for i, name in enumerate(names)}
    return res[0], per_name, res[1], res[2]


def _adamw_filters(g, w, m, v):
    n = len(g)

    def body(*refs):
        ins, outs = refs[:4 * n], refs[4 * n:]
        for i in range(n):
            d, mn, vn = _adamw_math(ins[4 * i + 1][...], ins[4 * i][...], ins[4 * i + 2][...], ins[4 * i + 3][...])
            for o_ref, val in zip(outs[3 * i:3 * i + 3], (d, mn, vn)):
                o_ref[...] = val

    operands, shapes = [], []
    for i in range(n):
        operands += [g[i], w[i], m[i], v[i]]
        shapes += [jax.ShapeDtypeStruct(w[i].shape, f32)] * 3
    res = pl.pallas_call(body, name="adamw_filters", out_shape=shapes)(*operands)
    return [tuple(res[3 * i:3 * i + 3]) for i in range(n)]


_REPLICATED = ("mix_norm_g", "ssd_conv_b", "dt_bias", "A_log", "D_skip", "ssd_norm_g", "conf_dw_b", "conf_ln_g",
               "conf_ln_b", "mlp_norm_g", "ple_gate_norm_g", "b_ple_gate", "ple_norm_g", "final_norm_g")
_CONV_WEIGHTS = ("ssd_conv_w", "conf_dw_w")
_WEIGHT_ORDER = ("mix_norm_g", "w_in", "ssd_conv_w", "ssd_conv_b", "dt_bias", "A_log", "D_skip", "ssd_norm_g", "conf_dw_w",
                 "conf_dw_b", "conf_ln_g", "conf_ln_b", "w_out", "mlp_norm_g", "w_up", "w_down", "ple_gate_norm_g",
                 "w_ple_gate", "b_ple_gate", "w_ple", "ple_norm_g", "final_norm_g")


def _blocks_of_columns(a):
    r, c8 = a.shape
    return jnp.transpose(a.reshape(r, N_DEV, c8 // N_DEV), (1, 0, 2))


def _columns_of_blocks(a):
    _, r, c = a.shape
    return jnp.transpose(a, (1, 0, 2)).reshape(r, N_DEV * c)


_LATER = ("w_out", "w_up", "w_down", "w_ple_gate", "w_ple")
_WHOLE = {
    "w_out": lambda a: a.reshape(2048, D_MODEL),
    "w_up": lambda a: a,
    "w_down": lambda a: a.reshape(D_FF, D_MODEL),
    "w_ple_gate": lambda a: a.reshape(D_MODEL, D_MODEL),
    "w_ple": _columns_of_blocks,
}
_BY_BLOCK = {
    "w_in": lambda g: jnp.concatenate(g, axis=0).reshape(N_DEV, IN_WIDTH // N_DEV, D_MODEL),
    "w_out": lambda g: g.reshape(N_DEV, 256, D_MODEL),
    "w_up": lambda g: g,
    "w_down": lambda g: g.reshape(N_DEV, 512, D_MODEL),
    "w_ple_gate": lambda g: g.astype(bf16).reshape(N_DEV, 128, D_MODEL),
    "w_ple": lambda g: _blocks_of_columns(g.astype(bf16)),
}


class _StepComm:
    def __init__(self, me, ready, gathers, token):
        self.me, self.ready, self.gathers, self.token = me, ready, gathers, token
        self.sent = []

    def start_token(self):
        return self.token

    def weight(self, name, after=()):
        if name not in self.ready:
            (land,), _ = _exchange_wait([self.gathers[name]], False, "gather_wait_" + name, list(after))
            self.ready[name] = _WHOLE[name](land)
        return self.ready[name]

    def send_grads(self, grads):
        names = list(grads)
        blocks = [_BY_BLOCK[n](grads[n]) for n in names]
        states, self.token = _exchange_start(blocks, True, "scatter_start_" + names[0], self.token)
        self.sent.append((names, states))
        return self.token


def kernel(x, p, mix_norm_g, w_in, ssd_conv_w, ssd_conv_b, dt_bias, A_log, D_skip, ssd_norm_g, conf_dw_w, conf_dw_b, conf_ln_g, conf_ln_b, w_out, mlp_norm_g, w_up, w_down, ple_gate_norm_g, w_ple_gate, b_ple_gate, w_ple, ple_norm_g, final_norm_g, loss_target, m_mix_norm_g, m_w_in, m_ssd_conv_w, m_ssd_conv_b, m_dt_bias, m_A_log, m_D_skip, m_ssd_norm_g, m_conf_dw_w, m_conf_dw_b, m_conf_ln_g, m_conf_ln_b, m_w_out, m_mlp_norm_g, m_w_up, m_w_down, m_ple_gate_norm_g, m_w_ple_gate, m_b_ple_gate, m_w_ple, m_ple_norm_g, m_final_norm_g, v_mix_norm_g, v_w_in, v_ssd_conv_w, v_ssd_conv_b, v_dt_bias, v_A_log, v_D_skip, v_ssd_norm_g, v_conf_dw_w, v_conf_dw_b, v_conf_ln_g, v_conf_ln_b, v_w_out, v_mlp_norm_g, v_w_up, v_w_down, v_ple_gate_norm_g, v_w_ple_gate, v_b_ple_gate, v_w_ple, v_ple_norm_g, v_final_norm_g):
    wts = dict(mix_norm_g=mix_norm_g, w_in=w_in, ssd_conv_w=ssd_conv_w, ssd_conv_b=ssd_conv_b, dt_bias=dt_bias, A_log=A_log,
               D_skip=D_skip, ssd_norm_g=ssd_norm_g, conf_dw_w=conf_dw_w, conf_dw_b=conf_dw_b, conf_ln_g=conf_ln_g,
               conf_ln_b=conf_ln_b, w_out=w_out, mlp_norm_g=mlp_norm_g, w_up=w_up, w_down=w_down,
               ple_gate_norm_g=ple_gate_norm_g, w_ple_gate=w_ple_gate, b_ple_gate=b_ple_gate, w_ple=w_ple,
               ple_norm_g=ple_norm_g, final_norm_g=final_norm_g)
    mom1 = dict(mix_norm_g=m_mix_norm_g, w_in=m_w_in, ssd_conv_w=m_ssd_conv_w, ssd_conv_b=m_ssd_conv_b, dt_bias=m_dt_bias,
                A_log=m_A_log, D_skip=m_D_skip, ssd_norm_g=m_ssd_norm_g, conf_dw_w=m_conf_dw_w, conf_dw_b=m_conf_dw_b,
                conf_ln_g=m_conf_ln_g, conf_ln_b=m_conf_ln_b, w_out=m_w_out, mlp_norm_g=m_mlp_norm_g, w_up=m_w_up,
                w_down=m_w_down, ple_gate_norm_g=m_ple_gate_norm_g, w_ple_gate=m_w_ple_gate, b_ple_gate=m_b_ple_gate,
                w_ple=m_w_ple, ple_norm_g=m_ple_norm_g, final_norm_g=m_final_norm_g)
    mom2 = dict(mix_norm_g=v_mix_norm_g, w_in=v_w_in, ssd_conv_w=v_ssd_conv_w, ssd_conv_b=v_ssd_conv_b, dt_bias=v_dt_bias,
                A_log=v_A_log, D_skip=v_D_skip, ssd_norm_g=v_ssd_norm_g, conf_dw_w=v_conf_dw_w, conf_dw_b=v_conf_dw_b,
                conf_ln_g=v_conf_ln_g, conf_ln_b=v_conf_ln_b, w_out=v_w_out, mlp_norm_g=v_mlp_norm_g, w_up=v_w_up,
                w_down=v_w_down, ple_gate_norm_g=v_ple_gate_norm_g, w_ple_gate=v_w_ple_gate, b_ple_gate=v_b_ple_gate,
                w_ple=v_w_ple, ple_norm_g=v_ple_norm_g, final_norm_g=v_final_norm_g)
    x_pos, y_pos, c_pos = _mesh_pos()
    me = 4 * x_pos + 2 * y_pos + c_pos

    first = _all_gather([wts["w_in"][0].T.astype(bf16), wts["ssd_conv_w"][0], wts["conf_dw_w"][0]], "gather_first")
    ready = {
        "w_in": first[0].reshape(IN_WIDTH, D_MODEL),
        "ssd_conv_w": jnp.pad(_columns_of_blocks(first[1]), ((0, 8 - SSD_CONV), (0, 0))),
        "conf_dw_w": jnp.pad(_columns_of_blocks(first[2]), ((0, 32 - CONF_KERNEL), (0, 0))),
    }
    shards = [wts[n][0].astype(bf16) for n in _LATER]
    states, token = _exchange_start(shards, False, "gather_start", first[1])
    comm = _StepComm(me, ready, dict(zip(_LATER, states)), token)
    small = {n: wts[n].reshape(1, -1) for n in _REPLICATED}

    grad_x, acc = _local_step(x[0], p[0, 0], loss_target[0], small, comm)

    packed = _pack_small(acc)
    (small_state,), _ = _exchange_start([packed], False, "small_start", comm.token)
    grads, delta, new_m, new_v = {}, {}, {}, {}
    me_index = me.astype(jnp.int32).reshape(1)

    def adamw_big(names, sent, after):
        lands, mine = _exchange_wait(sent, True, "scatter_wait_" + names[0], after)
        for n, land, own in zip(names, lands, mine):
            view = (lambda a: a[0].T) if n == "w_in" else (lambda a: a[0])
            back = (lambda a: a.T[None]) if n == "w_in" else (lambda a: a[None])
            out = _adamw_big(land, own, me_index, view(wts[n]), view(mom1[n]), view(mom2[n]), "adamw_" + n)
            grads[n], delta[n], new_m[n], new_v[n] = [back(a) for a in out]

    for names, sent in comm.sent[:-1]:
        adamw_big(names, sent, [grad_x])
    adamw_big(*comm.sent[-1], [delta[n] for n in _LATER])
    (all_small,), _ = _exchange_wait([small_state], False, "small_wait", [delta["w_in"]])
    as_row = lambda d: {n: d[n].reshape(1, -1) for n in _REPLICATED}
    loss, per_name, conv_w_sum, conf_w_sum = _small_update(all_small, as_row(wts), as_row(mom1), as_row(mom2))
    for n in _REPLICATED:
        grads[n], delta[n], new_m[n], new_v[n] = [a.reshape(wts[n].shape) for a in per_name[n]]
    filt_g = [lax.dynamic_slice_in_dim(conv_w_sum, me * 192, 192, axis=1),
              lax.dynamic_slice_in_dim(conf_w_sum, me * 128, 128, axis=1)]
    filt = _adamw_filters(filt_g, [wts[n][0] for n in _CONV_WEIGHTS], [mom1[n][0] for n in _CONV_WEIGHTS],
                          [mom2[n][0] for n in _CONV_WEIGHTS])
    for n, g, (d, mn, vn) in zip(_CONV_WEIGHTS, filt_g, filt):
        grads[n], delta[n], new_m[n], new_v[n] = g[None], d[None], mn[None], vn[None]

    return (loss.reshape(()), grad_x[None], *[grads[n] for n in _WEIGHT_ORDER], *[delta[n] for n in _WEIGHT_ORDER],
            *[new_m[n] for n in _WEIGHT_ORDER], *[new_v[n] for n in _WEIGHT_ORDER])
```

```python
import functools

import jax
import jax.numpy as jnp
from jax import lax
from jax.experimental import pallas as pl
from jax.experimental.pallas import tpu as pltpu

f32 = jnp.float32
bf16 = jnp.bfloat16

EPS = 1e-6
D_MODEL = 1024
SSD_HEADS = 16
SSD_HEAD_DIM = 64
SSD_GROUPS = 2
SSD_STATE = 128
SSD_CONV = 4
CHUNK = 128
XBC_WIDTH = 1536
CONF_KERNEL = 31
D_FF = 4096
PLE_DIM = 256
IN_WIDTH = 4624
N_DEV = 8
SEG_Z = (0, 1024)
SEG_XBC = (1024, 2560)
SEG_DT = (2560, 2576)
SEG_CV = (2576, 3600)
SEG_CG = (3600, 4624)
IN_SEGS = (SEG_Z, SEG_XBC, SEG_DT, SEG_CV, SEG_CG)
BWD_SEGS = (SEG_Z, SEG_XBC, SEG_DT, (SEG_CV[0], SEG_CG[1]))

ADAM_LR = 0.001
ADAM_B1 = 0.9
ADAM_B2 = 0.999
ADAM_EPS = 1e-08
ADAM_WD = 0.01
ADAM_STEP = 10

VMEM_LIMIT_BYTES = 56 * 1024 * 1024
TOKEN_TILE = 512
SMALL_ROWS = 16

HBM_SPEC = pl.BlockSpec(memory_space=pltpu.HBM)
MESH_ID = pl.DeviceIdType.MESH


def _params(*semantics):
    return pltpu.CompilerParams(dimension_semantics=semantics, vmem_limit_bytes=VMEM_LIMIT_BYTES)


def _nn(a, b):
    return lax.dot_general(a, b, (((1,), (0,)), ((), ())), preferred_element_type=f32)


def _nt(a, b):
    return lax.dot_general(a, b, (((1,), (1,)), ((), ())), preferred_element_type=f32)


def _tn(a, b):
    return lax.dot_general(a, b, (((0,), (0,)), ((), ())), preferred_element_type=f32)


@jax.custom_vjp
def bnn(a, b):
    return _nn(a.astype(bf16), b.astype(bf16))


def _bnn_fwd(a, b):
    ab, bb = a.astype(bf16), b.astype(bf16)
    return _nn(ab, bb), (ab, bb)


def _bnn_bwd(res, g):
    ab, bb = res
    gb = g.astype(bf16)
    return _nt(gb, bb), _tn(ab, gb)


bnn.defvjp(_bnn_fwd, _bnn_bwd)


@jax.custom_vjp
def bnt(a, b):
    return _nt(a.astype(bf16), b.astype(bf16))


def _bnt_fwd(a, b):
    ab, bb = a.astype(bf16), b.astype(bf16)
    return _nt(ab, bb), (ab, bb)


def _bnt_bwd(res, g):
    ab, bb = res
    gb = g.astype(bf16)
    return _nn(gb, bb), _tn(gb, ab)


bnt.defvjp(_bnt_fwd, _bnt_bwd)


@jax.custom_vjp
def btn(a, b):
    return _tn(a.astype(bf16), b.astype(bf16))


def _btn_fwd(a, b):
    ab, bb = a.astype(bf16), b.astype(bf16)
    return _tn(ab, bb), (ab, bb)


def _btn_bwd(res, g):
    ab, bb = res
    gb = g.astype(bf16)
    return _nt(bb, gb), _nn(ab, gb)


btn.defvjp(_btn_fwd, _btn_bwd)

SCAN_CHUNKS = 4
CUMSUM_PASSES = 3
EXPAND_PASSES = 2


def _bf16_pieces(a, passes):
    pieces, rest = [], a
    for _ in range(passes):
        piece = rest.astype(bf16)
        pieces.append(piece)
        rest = rest - piece.astype(f32)
    return pieces


def _make_dot01(form, passes):
    fwd, bwd = {"right": (lambda b, p: _nn(p, b), lambda b, g: _nt(g, b)),
                "left": (lambda b, p: _nn(b, p), lambda b, g: _tn(b, g)),
                "tn": (lambda b, p: _tn(p, b), lambda b, g: _nt(b, g))}[form]

    def run(op, b, v):
        out = None
        for piece in _bf16_pieces(v, passes):
            term = op(b, piece)
            out = term if out is None else out + term
        return out

    @jax.custom_vjp
    def product(b, v):
        return run(fwd, b, v)

    product.defvjp(lambda b, v: (run(fwd, b, v), b), lambda b, g: (jnp.zeros_like(b), run(bwd, b, g)))
    return product


_DOT01 = {(form, passes): _make_dot01(form, passes)
          for form, passes in (("left", CUMSUM_PASSES), ("tn", CUMSUM_PASSES), ("right", EXPAND_PASSES))}


def _dot01(b01, a, form, passes):
    return _DOT01[form, passes](b01.astype(bf16), a)


def _rms(x, g):
    return x * lax.rsqrt(jnp.mean(x * x, axis=-1, keepdims=True) + EPS) * g


def _gated_norm(y, z, g):
    v = y * jax.nn.silu(z)
    half = v.shape[-1] // SSD_GROUPS
    parts = []
    for k in range(SSD_GROUPS):
        vk = v[:, k * half:(k + 1) * half]
        parts.append(vk * lax.rsqrt(jnp.mean(vk * vk, axis=-1, keepdims=True) + EPS))
    return jnp.concatenate(parts, axis=-1) * g


def _ln_silu(v, g, b):
    mu = jnp.mean(v, axis=-1, keepdims=True)
    xc = v - mu
    y = xc * lax.rsqrt(jnp.mean(xc * xc, axis=-1, keepdims=True) + EPS) * g + b
    return jax.nn.silu(y)


def _acc_init(i, *refs):
    @pl.when(i == 0)
    def _():
        for r in refs:
            r[...] = jnp.zeros_like(r)


def _after_token(body, token):
    if token is None:
        return body, [], []

    def body_after(tok_ref, *refs):
        del tok_ref
        body(*refs)

    return body_after, [token], [pl.BlockSpec(memory_space=pl.ANY)]


def _row_spec(tm, n):
    return pl.BlockSpec((tm, n), lambda i: (i, 0))


def _const_spec(shape):
    nd = len(shape)
    return pl.BlockSpec(shape, lambda i: (0,) * nd)


def _prev_halo_spec(tm, halo, n):
    return pl.BlockSpec((halo, n), lambda i: (jnp.maximum(i * (tm // halo) - 1, 0), 0))


def _next_halo_spec(tm, halo, n, t):
    return pl.BlockSpec((halo, n), lambda i: (jnp.minimum((i + 1) * (tm // halo), t // halo - 1), 0))


def _in_proj(x, g1, w_in_p, token=None):
    t = x.shape[0]
    tm = min(TOKEN_TILE, t)

    def body(x_ref, g_ref, w_ref, u_ref, z_ref, xbc_ref, dt_ref, cv_ref, cg_ref):
        u = _rms(x_ref[...], g_ref[...]).astype(bf16)
        u_ref[...] = u
        for (lo, hi), o_ref in zip(IN_SEGS, (z_ref, xbc_ref, dt_ref, cv_ref, cg_ref)):
            o_ref[...] = _nt(u, w_ref[lo:hi, :])

    widths = [hi - lo for lo, hi in IN_SEGS]
    outs = [jax.ShapeDtypeStruct((t, D_MODEL), bf16)] + [jax.ShapeDtypeStruct((t, n), f32) for n in widths]
    body, tok, tok_spec = _after_token(body, token)
    return pl.pallas_call(
        body, name="in_proj", grid=(t // tm,),
        in_specs=tok_spec + [_row_spec(tm, D_MODEL), _const_spec((1, D_MODEL)), _const_spec((IN_WIDTH, D_MODEL))],
        out_specs=[_row_spec(tm, D_MODEL)] + [_row_spec(tm, n) for n in widths],
        out_shape=outs, compiler_params=_params("arbitrary"),
    )(*tok, x, g1, w_in_p)


ROW_BLOCK = 16
LANE_CHUNK = 512


def _blocks(rows, cols):
    return [(slice(r, r + ROW_BLOCK), slice(c, c + LANE_CHUNK))
            for c in range(0, cols, LANE_CHUNK) for r in range(0, rows, ROW_BLOCK)]


def _conv4_block(ext_ref, w_ref, b_ref, rs, ls):
    acc = b_ref[:, ls] + w_ref[0:1, ls] * ext_ref[rs.start + 8 - 3:rs.stop + 8 - 3, ls]
    for k in range(1, SSD_CONV):
        acc = acc + w_ref[k:k + 1, ls] * ext_ref[rs.start + 8 - 3 + k:rs.stop + 8 - 3 + k, ls]
    return acc


def _ssd_pre(xbc_raw, dt_raw, cw, cb, dt_bias):
    t = xbc_raw.shape[0]
    tm = min(TOKEN_TILE, t)

    def body(cur_ref, halo_ref, dt_ref, w_ref, b_ref, dtb_ref, act_ref, dto_ref, ext_ref):
        i = pl.program_id(0)
        ext_ref[0:8, :] = jnp.where(i == 0, 0.0, halo_ref[...])
        ext_ref[8:, :] = cur_ref[...]
        for rs, ls in _blocks(tm, XBC_WIDTH):
            act_ref[rs, ls] = jax.nn.silu(_conv4_block(ext_ref, w_ref, b_ref, rs, ls))
        dto_ref[...] = jax.nn.softplus(dt_ref[...] + dtb_ref[...])

    return pl.pallas_call(
        body, name="ssd_pre", grid=(t // tm,),
        in_specs=[_row_spec(tm, XBC_WIDTH), _prev_halo_spec(tm, 8, XBC_WIDTH), _row_spec(tm, SSD_HEADS),
                  _const_spec((8, XBC_WIDTH)), _const_spec((1, XBC_WIDTH)), _const_spec((1, SSD_HEADS))],
        out_specs=[_row_spec(tm, XBC_WIDTH), _row_spec(tm, SSD_HEADS)],
        out_shape=(jax.ShapeDtypeStruct((t, XBC_WIDTH), f32), jax.ShapeDtypeStruct((t, SSD_HEADS), f32)),
        scratch_shapes=[pltpu.VMEM((tm + 8, XBC_WIDTH), f32)],
        compiler_params=_params("arbitrary"),
    )(xbc_raw, xbc_raw, dt_raw, cw, cb, dt_bias)


def _ssd_consts():
    r = lax.broadcasted_iota(jnp.int32, (CHUNK, CHUNK), 0)
    c = lax.broadcasted_iota(jnp.int32, (CHUNK, CHUNK), 1)
    causal = r >= c
    tril = causal.astype(f32)
    triu = (r <= c).astype(f32)
    hh = lax.broadcasted_iota(jnp.int32, (SSD_HEADS, D_MODEL), 0)
    jj = lax.broadcasted_iota(jnp.int32, (SSD_HEADS, D_MODEL), 1)
    expand = (lax.shift_right_logical(jj, 6) == hh).astype(f32)
    lane = lax.broadcasted_iota(jnp.int32, (1, 2 * SSD_HEAD_DIM), 1)
    m0 = (lane < SSD_HEAD_DIM).astype(f32)
    return causal, tril, triu, expand, m0, 1.0 - m0


def _ssd_chunk(xs, bm, cm, dt, s_prev, a_log, d_skip, consts):
    causal, tril, triu, expand, m0, m1 = consts
    a = dt * (-jnp.exp(a_log))
    cs = _dot01(tril, a, "left", CUMSUM_PASSES)
    cs_t = _dot01(triu, a, "tn", CUMSUM_PASSES)
    cs_last = cs[CHUNK - 1:CHUNK, :]
    per_head = jnp.concatenate([dt, jnp.exp(cs_last - cs), jnp.exp(cs), jnp.broadcast_to(jnp.exp(cs_last), (8, SSD_HEADS)),
                                jnp.broadcast_to(d_skip, (8, SSD_HEADS))], axis=0)
    per_channel = _dot01(expand, per_head, "right", EXPAND_PASSES)
    dt_e, dec_end, dec_start = (per_channel[i * CHUNK:(i + 1) * CHUNK] for i in range(3))
    chunk_dec = per_channel[3 * CHUNK:3 * CHUNK + 1]
    d_e = per_channel[3 * CHUNK + 8:3 * CHUNK + 9]
    xc = xs * dt_e
    x_dec = xc * dec_end
    gw = D_MODEL // SSD_GROUPS
    ys, states = [], []
    for g in range(SSD_GROUPS):
        bg = bm[:, g * SSD_STATE:(g + 1) * SSD_STATE]
        cg = cm[:, g * SSD_STATE:(g + 1) * SSD_STATE]
        sp = s_prev[:, g * gw:(g + 1) * gw]
        states.append(sp * chunk_dec[:, g * gw:(g + 1) * gw] + btn(bg, x_dec[:, g * gw:(g + 1) * gw]))
        y_off = bnn(cg, sp) * dec_start[:, g * gw:(g + 1) * gw]
        scores = bnt(cg, bg)
        pieces = []
        for pr in range(gw // (2 * SSD_HEAD_DIM)):
            lo = g * gw + pr * 2 * SSD_HEAD_DIM
            xp = xc[:, lo:lo + 2 * SSD_HEAD_DIM]
            mats = []
            for h in (lo // SSD_HEAD_DIM, lo // SSD_HEAD_DIM + 1):
                seg = cs[:, h:h + 1] - cs_t[h:h + 1, :]
                mats.append(scores * jnp.exp(jnp.where(causal, seg, -jnp.inf)))
            pieces.append(bnn(jnp.concatenate(mats, axis=1), jnp.concatenate([xp * m0, xp * m1], axis=0)))
        ys.append(jnp.concatenate(pieces, axis=-1) + y_off)
    y = jnp.concatenate(ys, axis=-1) + xs * d_e
    return y, jnp.concatenate(states, axis=-1)


def _ssd_scan(xbc_act, dt, a_log, d_skip):
    t = xbc_act.shape[0]
    nc = t // CHUNK
    rows = SCAN_CHUNKS * CHUNK

    def body(xbc_ref, dt_ref, al_ref, dk_ref, y_ref, sp_ref, state_ref):
        c = pl.program_id(0)

        @pl.when(c == 0)
        def _():
            state_ref[...] = jnp.zeros_like(state_ref)

        state = state_ref[...]
        consts = _ssd_consts()
        for j in range(SCAN_CHUNKS):
            rs = slice(j * CHUNK, (j + 1) * CHUNK)
            sp_ref[j] = state
            y_ref[rs, :], state = _ssd_chunk(xbc_ref[rs, 0:1024], xbc_ref[rs, 1024:1280], xbc_ref[rs, 1280:1536],
                                             dt_ref[rs, :], state, al_ref[...], dk_ref[...], consts)
        state_ref[...] = state

    return pl.pallas_call(
        body, name="ssd_scan", grid=(nc // SCAN_CHUNKS,),
        in_specs=[_row_spec(rows, XBC_WIDTH), _row_spec(rows, SSD_HEADS), _const_spec((1, SSD_HEADS)),
                  _const_spec((1, SSD_HEADS))],
        out_specs=[_row_spec(rows, D_MODEL), pl.BlockSpec((SCAN_CHUNKS, SSD_STATE, D_MODEL), lambda c: (c, 0, 0))],
        out_shape=(jax.ShapeDtypeStruct((t, D_MODEL), f32), jax.ShapeDtypeStruct((nc, SSD_STATE, D_MODEL), f32)),
        scratch_shapes=[pltpu.VMEM((SSD_STATE, D_MODEL), f32)],
        compiler_params=_params("arbitrary"),
    )(xbc_act, dt, a_log, d_skip)


CONV_HALO = 32
CONV_BLOCK = 128
LANE_TILE = 128
SUBLANES = 8


def _fill_glu_ext(ext_ref, cv_ref, cg_ref, hcv_ref, hcg_ref, i):
    ext_ref[0:CONV_HALO, :] = jnp.where(i == 0, 0.0, hcv_ref[...] * jax.nn.sigmoid(hcg_ref[...]))
    ext_ref[CONV_HALO:, :] = cv_ref[...] * jax.nn.sigmoid(cg_ref[...])


def _shifted_copies(src_ref, dst_ref, rows):
    for s in range(1, SUBLANES):
        dst_ref[s, 0:rows, :] = src_ref[pl.ds(s, rows), :]


def _tap_source(src_ref, shifted_ref, offset):
    s = offset % SUBLANES
    return (src_ref if s == 0 else shifted_ref.at[s]), offset - s


def _conv_rows(src_ref, shifted_ref, taps, offsets, lanes, r0, init):
    accs = [init] * (CONV_BLOCK // SUBLANES)
    for k, off in enumerate(offsets):
        ref, base = _tap_source(src_ref, shifted_ref, off)
        for j in range(len(accs)):
            accs[j] = accs[j] + taps[k] * ref[pl.ds(r0 + base + SUBLANES * j, SUBLANES), lanes]
    return accs


def _conf_conv(cv, cg, w, b):
    t = cv.shape[0]
    tm = min(TOKEN_TILE, t)
    ext_rows = tm + CONV_HALO
    offsets = [CONV_HALO - (CONF_KERNEL - 1) + k for k in range(CONF_KERNEL)]

    def body(cv_ref, cg_ref, hcv_ref, hcg_ref, w_ref, b_ref, o_ref, ext_ref, shifted_ref):
        _fill_glu_ext(ext_ref, cv_ref, cg_ref, hcv_ref, hcg_ref, pl.program_id(0))
        _shifted_copies(ext_ref, shifted_ref, ext_rows - SUBLANES)
        for lb in range(1024 // LANE_TILE):
            lanes = slice(lb * LANE_TILE, (lb + 1) * LANE_TILE)
            taps = [jnp.broadcast_to(w_ref[k:k + 1, lanes], (SUBLANES, LANE_TILE)) for k in range(CONF_KERNEL)]
            bias = jnp.broadcast_to(b_ref[:, lanes], (SUBLANES, LANE_TILE))

            def rows(rb, carry, lanes=lanes, taps=taps, bias=bias):
                r0 = pl.multiple_of(rb * CONV_BLOCK, CONV_BLOCK)
                accs = _conv_rows(ext_ref, shifted_ref, taps, offsets, lanes, r0, bias)
                for j, a in enumerate(accs):
                    o_ref[pl.ds(r0 + SUBLANES * j, SUBLANES), lanes] = a
                return carry

            lax.fori_loop(0, tm // CONV_BLOCK, rows, 0)

    return pl.pallas_call(
        body, name="conf_conv", grid=(t // tm,),
        in_specs=[_row_spec(tm, 1024), _row_spec(tm, 1024), _prev_halo_spec(tm, CONV_HALO, 1024),
                  _prev_halo_spec(tm, CONV_HALO, 1024), _const_spec((32, 1024)), _const_spec((1, 1024))],
        out_specs=_row_spec(tm, 1024), out_shape=jax.ShapeDtypeStruct((t, 1024), f32),
        scratch_shapes=[pltpu.VMEM((ext_rows, 1024), f32), pltpu.VMEM((SUBLANES, ext_rows, 1024), f32)],
        compiler_params=_params("arbitrary"),
    )(cv, cg, cv, cg, w, b)


def _out_proj(y, z, v2, x, g_ssd, ln_g, ln_b, w_out, g2):
    t = x.shape[0]
    tm = min(TOKEN_TILE, t)

    def body(y_ref, z_ref, v_ref, x_ref, gs_ref, lg_ref, lb_ref, w_ref, g2_ref, mixed_ref, h1_ref, u2_ref):
        ys = _gated_norm(y_ref[...], z_ref[...], gs_ref[...]).astype(bf16)
        yc = _ln_silu(v_ref[...], lg_ref[...], lb_ref[...]).astype(bf16)
        mixed_ref[:, 0:1024] = ys
        mixed_ref[:, 1024:2048] = yc
        h1 = x_ref[...] + _nn(ys, w_ref[0:1024, :]) + _nn(yc, w_ref[1024:2048, :])
        h1_ref[...] = h1
        u2_ref[...] = _rms(h1, g2_ref[...]).astype(bf16)

    vec = _const_spec((1, 1024))
    row = _row_spec(tm, 1024)
    return pl.pallas_call(
        body, name="out_proj", grid=(t // tm,),
        in_specs=[row, row, row, row, vec, vec, vec, _const_spec((2048, 1024)), vec],
        out_specs=[_row_spec(tm, 2048), row, row],
        out_shape=(jax.ShapeDtypeStruct((t, 2048), bf16),
                   jax.ShapeDtypeStruct((t, 1024), f32), jax.ShapeDtypeStruct((t, 1024), bf16)),
        compiler_params=_params("arbitrary"),
    )(y, z, v2, x, g_ssd, ln_g, ln_b, w_out, g2)


def _mlp_up(u2, w_up):
    t = u2.shape[0]
    tm = min(TOKEN_TILE, t)
    blk = D_FF // N_DEV

    def body(u_ref, w_ref, pre_ref, hs_ref):
        u = u_ref[...]
        for k in range(N_DEV):
            pre = _nn(u, w_ref[k])
            pre_ref[:, k * blk:(k + 1) * blk] = pre.astype(bf16)
            r = jnp.maximum(pre, 0.0)
            hs_ref[:, k * blk:(k + 1) * blk] = (r * r).astype(bf16)

    return pl.pallas_call(
        body, name="mlp_up", grid=(t // tm,),
        in_specs=[_row_spec(tm, D_MODEL), _const_spec((N_DEV, D_MODEL, D_FF // N_DEV))],
        out_specs=[_row_spec(tm, D_FF), _row_spec(tm, D_FF)],
        out_shape=(jax.ShapeDtypeStruct((t, D_FF), bf16), jax.ShapeDtypeStruct((t, D_FF), bf16)),
        compiler_params=_params("arbitrary"),
    )(u2, w_up)


def _mlp_down(h1, hs, w_down):
    t = h1.shape[0]
    tm = min(TOKEN_TILE, t)

    def body(h_ref, hs_ref, w_ref, o_ref):
        o_ref[...] = h_ref[...] + _nn(hs_ref[...], w_ref[...])

    return pl.pallas_call(
        body, name="mlp_down", grid=(t // tm,),
        in_specs=[_row_spec(tm, D_MODEL), _row_spec(tm, D_FF), _const_spec((D_FF, D_MODEL))],
        out_specs=_row_spec(tm, D_MODEL), out_shape=jax.ShapeDtypeStruct((t, D_MODEL), f32),
        compiler_params=_params("arbitrary"),
    )(h1, hs, w_down)


TAIL_LOSS, TAIL_FINAL_G, TAIL_PLE_G, TAIL_GATE_B, TAIL_GATE_NORM_G = 0, 1, 2, 3, 4


def _tail(h2, p, target, g3, w_gate, b_gate, w_ple, ple_g, fin_g):
    t = h2.shape[0]
    tm = min(TOKEN_TILE, t)

    def body(h_ref, p_ref, t_ref, g3_ref, wg_ref, bg_ref, wp_ref, pg_ref, fg_ref,
             dh_ref, dhb_ref, dwg_ref, dwp_ref, acc_ref):
        i = pl.program_id(0)
        _acc_init(i, dwg_ref, dwp_ref, acc_ref)
        h2v = h_ref[...]
        tgt = t_ref[...]
        u3, vjp_u3 = jax.vjp(_rms, h2v, g3_ref[...])
        u3b = u3.astype(bf16)
        pb = p_ref[...].astype(bf16)
        gate_pre = _nn(u3b, wg_ref[...]) + bg_ref[...]
        emb_pre = _nn(pb, wp_ref[...])

        def tail_fn(hv, gp, ep, pg, fg):
            h3 = hv + jax.nn.sigmoid(gp) * _rms(ep, pg)
            err = _rms(h3, fg) - tgt
            return 0.5 * jnp.mean(err * err, axis=-1, keepdims=True)

        loss_tok, vjp_tail = jax.vjp(tail_fn, h2v, gate_pre, emb_pre, pg_ref[...], fg_ref[...])
        dh_a, dgp, dep, dpg, dfg = vjp_tail(jnp.ones_like(loss_tok))
        dgpb = dgp.astype(bf16)
        dh_b, dg3 = vjp_u3(_nt(dgpb, wg_ref[...]))
        dh = dh_a + dh_b
        dh_ref[...] = dh
        dhb_ref[...] = dh.astype(bf16)
        dwg_ref[...] += _tn(u3b, dgpb)
        dwp_ref[...] += _tn(pb, dep.astype(bf16))
        acc_ref[TAIL_LOSS:TAIL_LOSS + 1, :] += jnp.broadcast_to(jnp.sum(loss_tok, axis=0, keepdims=True), (1, 1024))
        acc_ref[TAIL_FINAL_G:TAIL_FINAL_G + 1, :] += dfg
        acc_ref[TAIL_PLE_G:TAIL_PLE_G + 1, :] += dpg
        acc_ref[TAIL_GATE_B:TAIL_GATE_B + 1, :] += jnp.sum(dgp, axis=0, keepdims=True)
        acc_ref[TAIL_GATE_NORM_G:TAIL_GATE_NORM_G + 1, :] += dg3

    vec = _const_spec((1, 1024))
    row = _row_spec(tm, 1024)
    return pl.pallas_call(
        body, name="tail", grid=(t // tm,),
        in_specs=[row, _row_spec(tm, PLE_DIM), row, vec, _const_spec((1024, 1024)), vec, _const_spec((PLE_DIM, 1024)), vec, vec],
        out_specs=[row, row, _const_spec((1024, 1024)), _const_spec((PLE_DIM, 1024)), _const_spec((8, 1024))],
        out_shape=(jax.ShapeDtypeStruct((t, 1024), f32), jax.ShapeDtypeStruct((t, 1024), bf16),
                   jax.ShapeDtypeStruct((1024, 1024), f32), jax.ShapeDtypeStruct((PLE_DIM, 1024), f32),
                   jax.ShapeDtypeStruct((8, 1024), f32)),
        compiler_params=_params("arbitrary"),
    )(h2, p, target, g3, w_gate, b_gate, w_ple, ple_g, fin_g)


def _mlp_bwd(dh2, dh2b, pre, h1, g2, w_down, w_up, token=None):
    t = dh2.shape[0]
    tm = min(TOKEN_TILE, t)
    blk = D_FF // N_DEV

    def body(dh_ref, dhb_ref, pre_ref, h1_ref, g_ref, wd_hbm, wu_hbm, dpre_ref, dh1_ref, dh1b_ref, acc_ref,
             wd_ref, wu_ref):
        i = pl.program_id(0)
        _acc_init(i, acc_ref)

        @pl.when(i == 0)
        def _():
            pltpu.sync_copy(wd_hbm, wd_ref)
            pltpu.sync_copy(wu_hbm, wu_ref)

        dhb = dhb_ref[...]
        du2 = jnp.zeros((tm, D_MODEL), f32)
        for k in range(D_FF // blk):
            dhs = _nt(dhb, wd_ref[k * blk:(k + 1) * blk, :])
            dpre = (dhs * (2.0 * jnp.maximum(pre_ref[:, k * blk:(k + 1) * blk].astype(f32), 0.0))).astype(bf16)
            dpre_ref[:, k * blk:(k + 1) * blk] = dpre
            du2 = du2 + _nt(dpre, wu_ref[k])
        _, vjp_u2 = jax.vjp(_rms, h1_ref[...], g_ref[...])
        d, dg = vjp_u2(du2)
        dh1 = dh_ref[...] + d
        dh1_ref[...] = dh1
        dh1b_ref[...] = dh1.astype(bf16)
        acc_ref[0:1, :] += dg

    row = _row_spec(tm, 1024)
    body, tok, tok_spec = _after_token(body, token)
    return pl.pallas_call(
        body, name="mlp_bwd", grid=(t // tm,),
        in_specs=tok_spec + [row, row, _row_spec(tm, D_FF), row, _const_spec((1, 1024)), HBM_SPEC, HBM_SPEC],
        out_specs=[_row_spec(tm, D_FF), row, row, _const_spec((8, 1024))],
        out_shape=(jax.ShapeDtypeStruct((t, D_FF), bf16), jax.ShapeDtypeStruct((t, 1024), f32),
                   jax.ShapeDtypeStruct((t, 1024), bf16), jax.ShapeDtypeStruct((8, 1024), f32)),
        scratch_shapes=[pltpu.VMEM((D_FF, D_MODEL), bf16), pltpu.VMEM((N_DEV, D_MODEL, D_FF // N_DEV), bf16)],
        compiler_params=_params("arbitrary"),
    )(*tok, dh2, dh2b, pre, h1, g2, w_down, w_up)


OPB_SSD_G, OPB_LN_G, OPB_LN_B, OPB_CONV_B = 0, 1, 2, 3


def _out_proj_bwd(dh1b, y, z, v2, g_ssd, ln_g, ln_b, w_out, token=None):
    t = y.shape[0]
    tm = min(TOKEN_TILE, t)

    def body(dh_ref, y_ref, z_ref, v_ref, gs_ref, lg_ref, lb_ref, w_ref, dy_ref, dz_ref, dv_ref, acc_ref):
        i = pl.program_id(0)
        _acc_init(i, acc_ref)
        dhb = dh_ref[...]
        dmixed = _nt(dhb, w_ref[...])
        dys, dyc = dmixed[:, 0:1024], dmixed[:, 1024:2048]
        _, vjp_g = jax.vjp(_gated_norm, y_ref[...], z_ref[...], gs_ref[...])
        dy, dz, dgs = vjp_g(dys)
        _, vjp_l = jax.vjp(_ln_silu, v_ref[...], lg_ref[...], lb_ref[...])
        dv, dlg, dlb = vjp_l(dyc)
        dy_ref[...] = dy
        dz_ref[...] = dz.astype(bf16)
        dv_ref[...] = dv
        acc_ref[OPB_SSD_G:OPB_SSD_G + 1, :] += dgs
        acc_ref[OPB_LN_G:OPB_LN_G + 1, :] += dlg
        acc_ref[OPB_LN_B:OPB_LN_B + 1, :] += dlb
        acc_ref[OPB_CONV_B:OPB_CONV_B + 1, :] += jnp.sum(dv, axis=0, keepdims=True)

    vec = _const_spec((1, 1024))
    row = _row_spec(tm, 1024)
    body, tok, tok_spec = _after_token(body, token)
    return pl.pallas_call(
        body, name="out_proj_bwd", grid=(t // tm,),
        in_specs=tok_spec + [row, row, row, row, vec, vec, vec, _const_spec((2048, 1024))],
        out_specs=[row, row, row, _const_spec((8, 1024))],
        out_shape=(jax.ShapeDtypeStruct((t, 1024), f32), jax.ShapeDtypeStruct((t, 1024), bf16),
                   jax.ShapeDtypeStruct((t, 1024), f32), jax.ShapeDtypeStruct((8, 1024), f32)),
        compiler_params=_params("arbitrary"),
    )(*tok, dh1b, y, z, v2, g_ssd, ln_g, ln_b, w_out)


def _conf_conv_bwd(dv2, cv, cg, w):
    t = cv.shape[0]
    tm = min(TOKEN_TILE, t)
    ext_rows = tm + CONV_HALO
    offsets = [CONF_KERNEL - 1 - k for k in range(CONF_KERNEL)]

    def body(dv_ref, dvn_ref, cv_ref, cg_ref, w_ref, dcvg_ref, dw_ref, glu_ref, dext_ref, sg_ref, shifted_ref):
        i = pl.program_id(0)
        _acc_init(i, dw_ref)
        sg = jax.nn.sigmoid(cg_ref[...])
        sg_ref[...] = sg
        glu_ref[...] = cv_ref[...] * sg
        dext_ref[0:tm, :] = dv_ref[...]
        dext_ref[tm:, :] = jnp.where(i == pl.num_programs(0) - 1, 0.0, dvn_ref[...])
        _shifted_copies(dext_ref, shifted_ref, ext_rows - SUBLANES)

        for lb in range(1024 // LANE_TILE):
            lanes = slice(lb * LANE_TILE, (lb + 1) * LANE_TILE)

            def rows_w(rb, accs, lanes=lanes):
                r0 = pl.multiple_of(rb * CONV_BLOCK, CONV_BLOCK)
                accs = list(accs)
                for j in range(CONV_BLOCK // SUBLANES):
                    x = glu_ref[pl.ds(r0 + SUBLANES * j, SUBLANES), lanes]
                    for k, off in enumerate(offsets):
                        ref, base = _tap_source(dext_ref, shifted_ref, off)
                        accs[k] = accs[k] + x * ref[pl.ds(r0 + base + SUBLANES * j, SUBLANES), lanes]
                return tuple(accs)

            zero = jnp.zeros((SUBLANES, LANE_TILE), f32)
            accs = lax.fori_loop(0, tm // CONV_BLOCK, rows_w, (zero,) * CONF_KERNEL)
            for k in range(CONF_KERNEL):
                dw_ref[k:k + 1, lanes] += jnp.sum(accs[k], axis=0, keepdims=True)

        for lb in range(1024 // LANE_TILE):
            lanes = slice(lb * LANE_TILE, (lb + 1) * LANE_TILE)
            taps = [jnp.broadcast_to(w_ref[k:k + 1, lanes], (SUBLANES, LANE_TILE)) for k in range(CONF_KERNEL)]

            def rows_x(rb, carry, lanes=lanes, taps=taps):
                r0 = pl.multiple_of(rb * CONV_BLOCK, CONV_BLOCK)
                zero = jnp.zeros((SUBLANES, LANE_TILE), f32)
                dglu = jnp.concatenate(_conv_rows(dext_ref, shifted_ref, taps, offsets, lanes, r0, zero), axis=0)
                sg = sg_ref[pl.ds(r0, CONV_BLOCK), lanes]
                cvv = cv_ref[pl.ds(r0, CONV_BLOCK), lanes]
                gate_lanes = slice(lanes.start + 1024, lanes.stop + 1024)
                dcvg_ref[pl.ds(r0, CONV_BLOCK), lanes] = (dglu * sg).astype(bf16)
                dcvg_ref[pl.ds(r0, CONV_BLOCK), gate_lanes] = (dglu * cvv * sg * (1.0 - sg)).astype(bf16)
                return carry

            lax.fori_loop(0, tm // CONV_BLOCK, rows_x, 0)

    row = _row_spec(tm, 1024)
    return pl.pallas_call(
        body, name="conf_conv_bwd", grid=(t // tm,),
        in_specs=[row, _next_halo_spec(tm, CONV_HALO, 1024, t), row, row, _const_spec((32, 1024))],
        out_specs=[_row_spec(tm, 2048), _const_spec((32, 1024))],
        out_shape=(jax.ShapeDtypeStruct((t, 2048), bf16), jax.ShapeDtypeStruct((32, 1024), f32)),
        scratch_shapes=[pltpu.VMEM((tm, 1024), f32), pltpu.VMEM((ext_rows, 1024), f32), pltpu.VMEM((tm, 1024), f32),
                        pltpu.VMEM((SUBLANES, ext_rows, 1024), f32)],
        compiler_params=_params("arbitrary"),
    )(dv2, dv2, cv, cg, w)


def _ssd_scan_bwd(xbc_act, dt, s_prev, dy, a_log, d_skip):
    t = xbc_act.shape[0]
    steps = t // CHUNK // SCAN_CHUNKS
    rows = SCAN_CHUNKS * CHUNK

    def body(xbc_ref, dt_ref, sp_ref, dy_ref, al_ref, dk_ref, dxbc_ref, ddt_ref, dal_ref, ddk_ref, ds_ref):
        i = pl.program_id(0)
        _acc_init(i, ds_ref, dal_ref, ddk_ref)
        consts = _ssd_consts()
        ds = ds_ref[...]
        dal_sum, ddk_sum = dal_ref[...], ddk_ref[...]
        for j in reversed(range(SCAN_CHUNKS)):
            rs = slice(j * CHUNK, (j + 1) * CHUNK)
            _, vjp_c = jax.vjp(
                functools.partial(_ssd_chunk, consts=consts),
                xbc_ref[rs, 0:1024], xbc_ref[rs, 1024:1280], xbc_ref[rs, 1280:1536], dt_ref[rs, :], sp_ref[j],
                al_ref[...], dk_ref[...])
            dxs, dbm, dcm, ddt, ds, dal, ddk = vjp_c((dy_ref[rs, :], ds))
            dxbc_ref[rs, 0:1024] = dxs
            dxbc_ref[rs, 1024:1280] = dbm
            dxbc_ref[rs, 1280:1536] = dcm
            ddt_ref[rs, :] = ddt
            dal_sum, ddk_sum = dal_sum + dal, ddk_sum + ddk
        ds_ref[...] = ds
        dal_ref[...] = dal_sum
        ddk_ref[...] = ddk_sum

    rev = lambda i: (steps - 1 - i, 0)
    return pl.pallas_call(
        body, name="ssd_scan_bwd", grid=(steps,),
        in_specs=[pl.BlockSpec((rows, XBC_WIDTH), rev), pl.BlockSpec((rows, SSD_HEADS), rev),
                  pl.BlockSpec((SCAN_CHUNKS, SSD_STATE, D_MODEL), lambda i: (steps - 1 - i, 0, 0)),
                  pl.BlockSpec((rows, D_MODEL), rev), _const_spec((1, SSD_HEADS)), _const_spec((1, SSD_HEADS))],
        out_specs=[pl.BlockSpec((rows, XBC_WIDTH), rev), pl.BlockSpec((rows, SSD_HEADS), rev),
                   _const_spec((1, SSD_HEADS)), _const_spec((1, SSD_HEADS))],
        out_shape=(jax.ShapeDtypeStruct((t, XBC_WIDTH), f32), jax.ShapeDtypeStruct((t, SSD_HEADS), f32),
                   jax.ShapeDtypeStruct((1, SSD_HEADS), f32), jax.ShapeDtypeStruct((1, SSD_HEADS), f32)),
        scratch_shapes=[pltpu.VMEM((SSD_STATE, D_MODEL), f32)],
        compiler_params=_params("arbitrary"),
    )(xbc_act, dt, s_prev, dy, a_log, d_skip)


def _ssd_pre_bwd(xbc_raw, dxbc_act, ddt, dt_raw, cw, cb, dt_bias):
    t = xbc_raw.shape[0]
    tm = min(TOKEN_TILE, t)

    def body(cur_ref, halo_ref, dact_ref, ddt_ref, dtr_ref, w_ref, b_ref, dtb_ref,
             dco_ref, ddtr_ref, dw_ref, db_ref, ddtb_ref, ext_ref):
        i = pl.program_id(0)
        _acc_init(i, dw_ref, db_ref, ddtb_ref)
        ext_ref[0:8, :] = jnp.where(i == 0, 0.0, halo_ref[...])
        ext_ref[8:, :] = cur_ref[...]
        fold = lambda a: a[0:8] + a[8:ROW_BLOCK]
        for c in range(0, XBC_WIDTH, LANE_CHUNK):
            ls = slice(c, c + LANE_CHUNK)
            part_b = jnp.zeros((8, LANE_CHUNK), f32)
            part_w = [jnp.zeros((8, LANE_CHUNK), f32)] * SSD_CONV
            for r in range(0, tm, ROW_BLOCK):
                rs = slice(r, r + ROW_BLOCK)
                co = _conv4_block(ext_ref, w_ref, b_ref, rs, ls)
                sg = jax.nn.sigmoid(co)
                dco = dact_ref[rs, ls] * sg * (1.0 + co * (1.0 - sg))
                dco_ref[rs, ls] = dco
                part_b = part_b + fold(dco)
                part_w = [pw + fold(dco * ext_ref[r + 8 - 3 + k:r + ROW_BLOCK + 8 - 3 + k, ls])
                          for k, pw in enumerate(part_w)]
            db_ref[:, ls] += jnp.sum(part_b, axis=0, keepdims=True)
            for k in range(SSD_CONV):
                dw_ref[k:k + 1, ls] += jnp.sum(part_w[k], axis=0, keepdims=True)
        ddtr = ddt_ref[...] * jax.nn.sigmoid(dtr_ref[...] + dtb_ref[...])
        ddtb_ref[...] += jnp.sum(ddtr, axis=0, keepdims=True)
        ddtr_ref[...] = ddtr.astype(bf16)

    return pl.pallas_call(
        body, name="ssd_pre_bwd", grid=(t // tm,),
        in_specs=[_row_spec(tm, XBC_WIDTH), _prev_halo_spec(tm, 8, XBC_WIDTH), _row_spec(tm, XBC_WIDTH),
                  _row_spec(tm, SSD_HEADS), _row_spec(tm, SSD_HEADS), _const_spec((8, XBC_WIDTH)),
                  _const_spec((1, XBC_WIDTH)), _const_spec((1, SSD_HEADS))],
        out_specs=[_row_spec(tm, XBC_WIDTH), _row_spec(tm, SSD_HEADS), _const_spec((8, XBC_WIDTH)),
                   _const_spec((1, XBC_WIDTH)), _const_spec((1, SSD_HEADS))],
        out_shape=(jax.ShapeDtypeStruct((t, XBC_WIDTH), f32), jax.ShapeDtypeStruct((t, SSD_HEADS), bf16),
                   jax.ShapeDtypeStruct((8, XBC_WIDTH), f32), jax.ShapeDtypeStruct((1, XBC_WIDTH), f32),
                   jax.ShapeDtypeStruct((1, SSD_HEADS), f32)),
        scratch_shapes=[pltpu.VMEM((tm + 8, XBC_WIDTH), f32)],
        compiler_params=_params("arbitrary"),
    )(xbc_raw, xbc_raw, dxbc_act, ddt, dt_raw, cw, cb, dt_bias)


def _conv4_bwd_data(dco, cw):
    t = dco.shape[0]
    tm = min(TOKEN_TILE, t)

    def body(cur_ref, nxt_ref, w_ref, o_ref, ext_ref):
        i = pl.program_id(0)
        ext_ref[0:tm, :] = cur_ref[...]
        ext_ref[tm:, :] = jnp.where(i == pl.num_programs(0) - 1, 0.0, nxt_ref[...])
        for rs, ls in _blocks(tm, XBC_WIDTH):
            acc = w_ref[0:1, ls] * ext_ref[rs.start + SSD_CONV - 1:rs.stop + SSD_CONV - 1, ls]
            for k in range(1, SSD_CONV):
                acc = acc + w_ref[k:k + 1, ls] * ext_ref[rs.start + SSD_CONV - 1 - k:rs.stop + SSD_CONV - 1 - k, ls]
            o_ref[rs, ls] = acc.astype(bf16)

    return pl.pallas_call(
        body, name="conv4_bwd_data", grid=(t // tm,),
        in_specs=[_row_spec(tm, XBC_WIDTH), _next_halo_spec(tm, 8, XBC_WIDTH, t), _const_spec((8, XBC_WIDTH))],
        out_specs=_row_spec(tm, XBC_WIDTH), out_shape=jax.ShapeDtypeStruct((t, XBC_WIDTH), bf16),
        scratch_shapes=[pltpu.VMEM((tm + 8, XBC_WIDTH), f32)],
        compiler_params=_params("arbitrary"),
    )(dco, dco, cw)


def _in_proj_bwd(dproj, x, dh1, g1, w_in_p, token=None):
    t = x.shape[0]
    tm = min(TOKEN_TILE, t)

    def body(dz_ref, dxbc_ref, ddt_ref, dcvg_ref, x_ref, dh_ref, g_ref, w_ref, dx_ref, acc_ref):
        i = pl.program_id(0)
        _acc_init(i, acc_ref)
        du = jnp.zeros((tm, D_MODEL), f32)
        for (lo, hi), r in zip(BWD_SEGS, (dz_ref, dxbc_ref, ddt_ref, dcvg_ref)):
            du = du + _nn(r[...], w_ref[lo:hi, :])
        _, vjp_u = jax.vjp(_rms, x_ref[...], g_ref[...])
        d, dg = vjp_u(du)
        dx_ref[...] = dh_ref[...] + d
        acc_ref[0:1, :] += dg

    row = _row_spec(tm, 1024)
    body, tok, tok_spec = _after_token(body, token)
    return pl.pallas_call(
        body, name="in_proj_bwd", grid=(t // tm,),
        in_specs=tok_spec + [_row_spec(tm, hi - lo) for lo, hi in BWD_SEGS] + [row, row, _const_spec((1, 1024)),
                                                                               _const_spec((IN_WIDTH, D_MODEL))],
        out_specs=[row, _const_spec((8, 1024))],
        out_shape=(jax.ShapeDtypeStruct((t, 1024), f32), jax.ShapeDtypeStruct((8, 1024), f32)),
        compiler_params=_params("arbitrary"),
    )(*tok, *dproj, x, dh1, g1, w_in_p)


def _mm_tn(a, b, tk, tn, name, column_blocks=False):
    t, kk = a.shape
    n = b.shape[1]
    tk, tn = min(tk, kk), min(tn, n)

    def body(a_ref, b_ref, o_ref):
        o_ref[...] = _tn(a_ref[...], b_ref[...]).astype(bf16)

    if column_blocks:
        assert tk == kk
        out_spec = pl.BlockSpec((None, tk, tn), lambda i, j: (j, 0, 0))
        out_shape = jax.ShapeDtypeStruct((n // tn, kk, tn), bf16)
    else:
        out_spec = pl.BlockSpec((tk, tn), lambda i, j: (i, j))
        out_shape = jax.ShapeDtypeStruct((kk, n), bf16)
    return pl.pallas_call(
        body, name=name, grid=(kk // tk, n // tn),
        in_specs=[pl.BlockSpec((t, tk), lambda i, j: (0, i)), pl.BlockSpec((t, tn), lambda i, j: (0, j))],
        out_specs=out_spec, out_shape=out_shape,
        compiler_params=_params("arbitrary", "arbitrary"),
    )(a, b)


class _LocalWeights:
    def __init__(self, w):
        self.w = w
        self.sent = {}

    def start_token(self):
        return None

    def weight(self, name, after=()):
        del after
        return self.w[name]

    def send_grads(self, grads):
        self.sent.update(grads)
        return None


def _local_step(x, p, target, s, comm):
    conv_w, conf_w = comm.weight("ssd_conv_w"), comm.weight("conf_dw_w")
    w_in = comm.weight("w_in")
    u, z, xbc_raw, dt_raw, cv, cg = _in_proj(x, s["mix_norm_g"], w_in, token=comm.start_token())
    xbc_act, dt = _ssd_pre(xbc_raw, dt_raw, conv_w, s["ssd_conv_b"], s["dt_bias"])
    y, s_prev = _ssd_scan(xbc_act, dt, s["A_log"], s["D_skip"])
    v2 = _conf_conv(cv, cg, conf_w, s["conf_dw_b"])
    w_out = comm.weight("w_out", after=(y, v2))
    mixed, h1, u2 = _out_proj(y, z, v2, x, s["ssd_norm_g"], s["conf_ln_g"], s["conf_ln_b"], w_out, s["mlp_norm_g"])
    w_up = comm.weight("w_up", after=(u2,))
    pre, hs = _mlp_up(u2, w_up)
    w_down = comm.weight("w_down", after=(hs,))
    h2 = _mlp_down(h1, hs, w_down)
    w_gate, w_ple = comm.weight("w_ple_gate", after=(h2,)), comm.weight("w_ple", after=(h2,))
    dh2, dh2b, dwg, dwp, tail_acc = _tail(h2, p, target, s["ple_gate_norm_g"], w_gate, s["b_ple_gate"], w_ple,
                                          s["ple_norm_g"], s["final_norm_g"])
    token = comm.send_grads({"w_ple_gate": dwg, "w_ple": dwp})
    dpre, dh1, dh1b, mlp_acc = _mlp_bwd(dh2, dh2b, pre, h1, s["mlp_norm_g"], w_down, w_up, token=token)
    token = comm.send_grads({
        "w_down": _mm_tn(hs, dh2b, 512, 1024, "dw_down"),
        "w_up": _mm_tn(u2, dpre, 1024, D_FF // N_DEV, "dw_up", column_blocks=True),
        "w_out": _mm_tn(mixed, dh1b, 512, 1024, "dw_out"),
    })
    dy, dz, dv2, opb_acc = _out_proj_bwd(dh1b, y, z, v2, s["ssd_norm_g"], s["conf_ln_g"], s["conf_ln_b"], w_out,
                                         token=token)
    dcvg, dconf_w = _conf_conv_bwd(dv2, cv, cg, conf_w)
    dxbc_act, ddt, d_alog, d_dskip = _ssd_scan_bwd(xbc_act, dt, s_prev, dy, s["A_log"], s["D_skip"])
    dco, ddt_raw, dconv_w, dconv_b, d_dtb = _ssd_pre_bwd(xbc_raw, dxbc_act, ddt, dt_raw, conv_w, s["ssd_conv_b"],
                                                        s["dt_bias"])
    dxbc_raw = _conv4_bwd_data(dco, conv_w)
    dproj = (dz, dxbc_raw, ddt_raw, dcvg)
    token = comm.send_grads({"w_in": [_mm_tn(d, u, 512, 1024, "dw_in_" + n)
                                      for n, d in zip(("z", "xbc", "dt", "cvg"), dproj)]})
    grad_x, inp_acc = _in_proj_bwd(dproj, x, dh1, s["mix_norm_g"], w_in, token=token)
    acc = {"in_proj": inp_acc, "out_proj": opb_acc, "mlp": mlp_acc, "tail": tail_acc, "ssd_conv_b": dconv_b,
           "dt_bias": d_dtb, "A_log": d_alog, "D_skip": d_dskip, "ssd_conv_w": dconv_w, "conf_dw_w": dconf_w}
    return grad_x, acc


def _mesh_pos():
    return lax.axis_index("x"), lax.axis_index("y"), lax.axis_index("c")


def _other_chips(x, y):
    return [(1 - x, y), (x, 1 - y), (1 - x, 1 - y)]


def _all_gather(arrs, name):
    n = len(arrs)

    def body(*refs):
        ins, outs = refs[:n], refs[n:2 * n]
        send_sems, recv_sems, local_sems = refs[2 * n:]
        x, y, c = _mesh_pos()
        me = 4 * x + 2 * y + c
        sibling = (x, y, 1 - c)
        chips = _other_chips(x, y)

        def copy(a, k, block, to, src=None):
            dst = outs[a].at[block]
            return pltpu.make_async_remote_copy(
                src_ref=dst if src is None else src, dst_ref=dst, send_sem=send_sems.at[a, k],
                recv_sem=recv_sems.at[a, k], device_id=to, device_id_type=MESH_ID)

        mine = [pltpu.make_async_copy(ins[a], outs[a].at[me], local_sems.at[a]) for a in range(n)]
        for cp in mine:
            cp.start()
        first = []
        for a in range(n):
            first.append(copy(a, 0, me, sibling, src=ins[a]))
            first += [copy(a, 1 + j, me, (px, py, c), src=ins[a]) for j, (px, py) in enumerate(chips)]
        for cp in first:
            cp.start()
        passed = []
        for j, (px, py) in enumerate(chips):
            for a in range(n):
                blk = 4 * px + 2 * py + c
                copy(a, 1 + j, blk, (x, y, c)).wait_recv()
                cp = copy(a, 4 + j, blk, sibling)
                cp.start()
                passed.append(cp)
        for a in range(n):
            copy(a, 0, 4 * x + 2 * y + (1 - c), (x, y, c)).wait_recv()
        for j, (px, py) in enumerate(chips):
            for a in range(n):
                copy(a, 4 + j, 4 * px + 2 * py + (1 - c), (x, y, c)).wait_recv()
        for cp in first + passed:
            cp.wait_send()
        for cp in mine:
            cp.wait()

    return pl.pallas_call(
        body, name=name,
        in_specs=[HBM_SPEC] * n, out_specs=[HBM_SPEC] * n,
        out_shape=[jax.ShapeDtypeStruct((N_DEV,) + a.shape, a.dtype) for a in arrs],
        scratch_shapes=[pltpu.SemaphoreType.DMA((n, 7)), pltpu.SemaphoreType.DMA((n, 7)), pltpu.SemaphoreType.DMA((n,))],
    )(*arrs)


_PEER_FLIPS = ((0, 0, 1), (1, 0, 0), (0, 1, 0), (1, 1, 0), (1, 0, 1), (0, 1, 1), (1, 1, 1))
SEM_SPEC = pl.BlockSpec(memory_space=pltpu.SEMAPHORE)
ANY_SPEC = pl.BlockSpec(memory_space=pl.ANY)


def _flip(v, d):
    return 1 - v if d else v


def _peers(x, y, c):
    out = []
    for dx, dy, dc in _PEER_FLIPS:
        px, py, pc = _flip(x, dx), _flip(y, dy), _flip(c, dc)
        out.append(((px, py, pc), 4 * px + 2 * py + pc))
    return out


def _exchange_copy(src_ref, land_ref, send_sems, recv_sems, k, peer, peer_block, my_block, by_block, outgoing):
    src = src_ref.at[peer_block] if by_block else src_ref
    dst = land_ref.at[my_block if outgoing else peer_block]
    return pltpu.make_async_remote_copy(src_ref=src, dst_ref=dst, send_sem=send_sems.at[k], recv_sem=recv_sems.at[k],
                                        device_id=peer, device_id_type=MESH_ID)


def _exchange_start(srcs, by_block, name, after):
    n = len(srcs)
    x_pos, y_pos, c_pos = _mesh_pos()
    me_at = 4 * x_pos + 2 * y_pos + c_pos
    if by_block:
        lands = [lax.empty(a.shape, a.dtype) for a in srcs]
    else:
        lands = [lax.dynamic_update_slice(lax.empty((N_DEV,) + a.shape, a.dtype), a[None], (me_at,) + (0,) * a.ndim)
                 for a in srcs]

    def body(*refs):
        src_refs, land_refs = refs[1:1 + n], refs[1 + n:1 + 2 * n]
        outs = refs[1 + 2 * n:]
        send, recv, token = outs[:n], outs[n:2 * n], outs[4 * n]
        x, y, c = _mesh_pos()
        me = 4 * x + 2 * y + c
        for a in range(n):
            for k, (peer, blk) in enumerate(_peers(x, y, c)):
                _exchange_copy(src_refs[a], land_refs[a], send[a], recv[a], k, peer, blk, me, by_block, True).start()
        token[...] = jnp.zeros_like(token)

    sems = [pltpu.SemaphoreType.DMA((N_DEV - 1,))] * (2 * n)
    thru = [pltpu.HBM(a.shape, a.dtype) for a in list(srcs) + list(lands)]
    res = pl.pallas_call(
        body, name=name,
        in_specs=[ANY_SPEC] + [HBM_SPEC] * (2 * n),
        out_specs=[SEM_SPEC] * (2 * n) + [HBM_SPEC] * (2 * n) + [pl.BlockSpec(memory_space=pltpu.VMEM)],
        out_shape=sems + thru + [jax.ShapeDtypeStruct((8, 128), f32)],
        input_output_aliases={1 + i: 2 * n + i for i in range(2 * n)},
        compiler_params=pltpu.CompilerParams(has_side_effects=pltpu.SideEffectType.DATAFLOW_SIDE_EFFECTING),
    )(after, *[pltpu.with_memory_space_constraint(a, pltpu.HBM) for a in list(srcs) + list(lands)])
    states = [(res[2 * n + a], res[3 * n + a], res[a], res[n + a]) for a in range(n)]
    return states, res[4 * n]


def _exchange_wait(states, by_block, name, after):
    n, na = len(states), len(after)

    def body(*refs):
        src_refs, land_refs = refs[:n], refs[n:2 * n]
        send, recv = refs[2 * n:3 * n], refs[3 * n:4 * n]
        x, y, c = _mesh_pos()
        me = 4 * x + 2 * y + c
        for a in range(n):
            for k, (peer, blk) in enumerate(_peers(x, y, c)):
                _exchange_copy(src_refs[a], land_refs[a], send[a], recv[a], k, peer, blk, me, by_block, True).wait_send()
                _exchange_copy(src_refs[a], land_refs[a], send[a], recv[a], k, peer, blk, me, by_block, False).wait_recv()

    srcs, lands = [s[0] for s in states], [s[1] for s in states]
    res = pl.pallas_call(
        body, name=name,
        in_specs=[HBM_SPEC] * (2 * n) + [SEM_SPEC] * (2 * n) + [ANY_SPEC] * na,
        out_specs=[HBM_SPEC] * (2 * n),
        out_shape=[pltpu.HBM(a.shape, a.dtype) for a in srcs + lands],
        input_output_aliases={i: i for i in range(2 * n)},
        compiler_params=pltpu.CompilerParams(has_side_effects=pltpu.SideEffectType.DATAFLOW_SIDE_EFFECTING),
    )(*srcs, *lands, *[s[2] for s in states], *[s[3] for s in states], *after)
    return list(res[n:2 * n]), list(res[:n])


def _adamw_math(w, g, m, v):
    m = ADAM_B1 * m + (1.0 - ADAM_B1) * g
    v = ADAM_B2 * v + (1.0 - ADAM_B2) * (g * g)
    m_hat = m / (1.0 - ADAM_B1 ** ADAM_STEP)
    v_hat = v / (1.0 - ADAM_B2 ** ADAM_STEP)
    delta = -ADAM_LR * (m_hat / (jnp.sqrt(v_hat) + ADAM_EPS) + ADAM_WD * w)
    return delta, m, v


def _adamw_big(parts, mine, me, w, m, v, name):
    rows, cols = w.shape
    tr, tc = (256, cols) if rows % 256 == 0 else (rows, 256)

    def body(me_ref, p_ref, own_ref, w_ref, m_ref, v_ref, g_ref, d_ref, mo_ref, vo_ref):
        own = own_ref[...].astype(f32)
        g = None
        for j in range(N_DEV):
            part = jnp.where(me_ref[0] == j, own, p_ref[j].astype(f32))
            g = part if g is None else g + part
        d, mn, vn = _adamw_math(w_ref[...], g, m_ref[...], v_ref[...])
        g_ref[...] = g
        d_ref[...] = d
        mo_ref[...] = mn
        vo_ref[...] = vn

    tile = pl.BlockSpec((tr, tc), lambda i, j, me_ref: (i, j))
    shp = jax.ShapeDtypeStruct((rows, cols), f32)
    grid_spec = pltpu.PrefetchScalarGridSpec(
        num_scalar_prefetch=1, grid=(rows // tr, cols // tc),
        in_specs=[pl.BlockSpec((N_DEV, tr, tc), lambda i, j, me_ref: (0, i, j)),
                  pl.BlockSpec((None, tr, tc), lambda i, j, me_ref: (me_ref[0], i, j)), tile, tile, tile],
        out_specs=[tile, tile, tile, tile])
    return pl.pallas_call(
        body, name=name, grid_spec=grid_spec, out_shape=(shp, shp, shp, shp),
        compiler_params=_params("arbitrary", "arbitrary"),
    )(me, parts, mine, w, m, v)


PACK_ROWS = 56
_PACK_AT = {
    "mix_norm_g": (0, 0, 1024), "ssd_norm_g": (1, 0, 1024), "conf_ln_g": (2, 0, 1024), "conf_ln_b": (3, 0, 1024),
    "conf_dw_b": (4, 0, 1024), "mlp_norm_g": (5, 0, 1024), "final_norm_g": (7, 0, 1024), "ple_norm_g": (8, 0, 1024),
    "b_ple_gate": (9, 0, 1024), "ple_gate_norm_g": (10, 0, 1024), "dt_bias": (13, 0, 16), "A_log": (13, 128, 16),
    "D_skip": (13, 256, 16),
}
PACK_LOSS_ROW = 6
PACK_CONV_B_ROW = 11
PACK_CONV_W_ROW = 14
PACK_CONF_W_ROW = 24


def _pack_small(acc):
    def body(inp, opb, mlp, tail, cb, dtb, alog, dskip, cw, fw, o_ref):
        o_ref[...] = jnp.zeros_like(o_ref)
        rows = {"mix_norm_g": inp[0:1, :], "mlp_norm_g": mlp[0:1, :],
                "ssd_norm_g": opb[OPB_SSD_G:OPB_SSD_G + 1, :], "conf_ln_g": opb[OPB_LN_G:OPB_LN_G + 1, :],
                "conf_ln_b": opb[OPB_LN_B:OPB_LN_B + 1, :], "conf_dw_b": opb[OPB_CONV_B:OPB_CONV_B + 1, :],
                "final_norm_g": tail[TAIL_FINAL_G:TAIL_FINAL_G + 1, :], "ple_norm_g": tail[TAIL_PLE_G:TAIL_PLE_G + 1, :],
                "b_ple_gate": tail[TAIL_GATE_B:TAIL_GATE_B + 1, :],
                "ple_gate_norm_g": tail[TAIL_GATE_NORM_G:TAIL_GATE_NORM_G + 1, :],
                "dt_bias": dtb[...], "A_log": alog[...], "D_skip": dskip[...]}
        for name, val in rows.items():
            r, lo, width = _PACK_AT[name]
            o_ref[r:r + 1, lo:lo + width] = val
        o_ref[PACK_LOSS_ROW:PACK_LOSS_ROW + 1, :] = tail[TAIL_LOSS:TAIL_LOSS + 1, :]
        o_ref[PACK_CONV_B_ROW:PACK_CONV_B_ROW + 1, :] = cb[:, 0:1024]
        o_ref[PACK_CONV_B_ROW + 1:PACK_CONV_B_ROW + 2, 0:512] = cb[:, 1024:XBC_WIDTH]
        for k in range(SSD_CONV):
            o_ref[PACK_CONV_W_ROW + k:PACK_CONV_W_ROW + k + 1, :] = cw[k:k + 1, 0:1024]
            o_ref[PACK_CONV_W_ROW + SSD_CONV + k:PACK_CONV_W_ROW + SSD_CONV + k + 1, 0:512] = cw[k:k + 1, 1024:XBC_WIDTH]
        o_ref[PACK_CONF_W_ROW:PACK_CONF_W_ROW + 32, :] = fw[...]

    return pl.pallas_call(body, name="pack_small", out_shape=jax.ShapeDtypeStruct((PACK_ROWS, 1024), f32))(
        acc["in_proj"], acc["out_proj"], acc["mlp"], acc["tail"], acc["ssd_conv_b"], acc["dt_bias"], acc["A_log"],
        acc["D_skip"], acc["ssd_conv_w"], acc["conf_dw_w"])


def _small_update(all_small, w, m, v):
    names = _REPLICATED

    def body(all_ref, *refs):
        ins, outs = refs[:3 * len(names)], refs[3 * len(names):]
        s = all_ref[0]
        for j in range(1, N_DEV):
            s = s + all_ref[j]
        outs[0][...] = s[PACK_LOSS_ROW:PACK_LOSS_ROW + 1, 0:1]
        outs[1][...] = jnp.concatenate([s[PACK_CONV_W_ROW:PACK_CONV_W_ROW + SSD_CONV, :],
                                        s[PACK_CONV_W_ROW + SSD_CONV:PACK_CONV_W_ROW + 2 * SSD_CONV, 0:512]], axis=1)
        outs[2][...] = s[PACK_CONF_W_ROW:PACK_CONF_W_ROW + CONF_KERNEL, :]
        for i, name in enumerate(names):
            if name == "ssd_conv_b":
                g = jnp.concatenate([s[PACK_CONV_B_ROW:PACK_CONV_B_ROW + 1, :],
                                     s[PACK_CONV_B_ROW + 1:PACK_CONV_B_ROW + 2, 0:512]], axis=1)
            else:
                r, lo, width = _PACK_AT[name]
                g = s[r:r + 1, lo:lo + width]
            d, mn, vn = _adamw_math(ins[3 * i][...], g, ins[3 * i + 1][...], ins[3 * i + 2][...])
            for o_ref, val in zip(outs[3 + 4 * i:7 + 4 * i], (g, d, mn, vn)):
                o_ref[...] = val

    shapes = [jax.ShapeDtypeStruct((1, 1), f32), jax.ShapeDtypeStruct((SSD_CONV, XBC_WIDTH), f32),
              jax.ShapeDtypeStruct((CONF_KERNEL, D_MODEL), f32)]
    operands = []
    for name in names:
        operands += [w[name], m[name], v[name]]
        shapes += [jax.ShapeDtypeStruct(w[name].shape, f32)] * 4
    res = pl.pallas_call(body, name="small_update", out_shape=shapes)(all_small, *operands)
    per_name = {name: tuple(res[3 + 4 * i:7 + 4 * i]) for i, name in enumerate(names)}
    return res[0], per_name, res[1], res[2]


def _adamw_filters(g, w, m, v):
    n = len(g)

    def body(*refs):
        ins, outs = refs[:4 * n], refs[4 * n:]
        for i in range(n):
            d, mn, vn = _adamw_math(ins[4 * i + 1][...], ins[4 * i][...], ins[4 * i + 2][...], ins[4 * i + 3][...])
            for o_ref, val in zip(outs[3 * i:3 * i + 3], (d, mn, vn)):
                o_ref[...] = val

    operands, shapes = [], []
    for i in range(n):
        operands += [g[i], w[i], m[i], v[i]]
        shapes += [jax.ShapeDtypeStruct(w[i].shape, f32)] * 3
    res = pl.pallas_call(body, name="adamw_filters", out_shape=shapes)(*operands)
    return [tuple(res[3 * i:3 * i + 3]) for i in range(n)]


_REPLICATED = ("mix_norm_g", "ssd_conv_b", "dt_bias", "A_log", "D_skip", "ssd_norm_g", "conf_dw_b", "conf_ln_g",
               "conf_ln_b", "mlp_norm_g", "ple_gate_norm_g", "b_ple_gate", "ple_norm_g", "final_norm_g")
_CONV_WEIGHTS = ("ssd_conv_w", "conf_dw_w")
_WEIGHT_ORDER = ("mix_norm_g", "w_in", "ssd_conv_w", "ssd_conv_b", "dt_bias", "A_log", "D_skip", "ssd_norm_g", "conf_dw_w",
                 "conf_dw_b", "conf_ln_g", "conf_ln_b", "w_out", "mlp_norm_g", "w_up", "w_down", "ple_gate_norm_g",
                 "w_ple_gate", "b_ple_gate", "w_ple", "ple_norm_g", "final_norm_g")


def _blocks_of_columns(a):
    r, c8 = a.shape
    return jnp.transpose(a.reshape(r, N_DEV, c8 // N_DEV), (1, 0, 2))


def _columns_of_blocks(a):
    _, r, c = a.shape
    return jnp.transpose(a, (1, 0, 2)).reshape(r, N_DEV * c)


_LATER = ("w_out", "w_up", "w_down", "w_ple_gate", "w_ple")
_WHOLE = {
    "w_out": lambda a: a.reshape(2048, D_MODEL),
    "w_up": lambda a: a,
    "w_down": lambda a: a.reshape(D_FF, D_MODEL),
    "w_ple_gate": lambda a: a.reshape(D_MODEL, D_MODEL),
    "w_ple": _columns_of_blocks,
}
_BY_BLOCK = {
    "w_in": lambda g: jnp.concatenate(g, axis=0).reshape(N_DEV, IN_WIDTH // N_DEV, D_MODEL),
    "w_out": lambda g: g.reshape(N_DEV, 256, D_MODEL),
    "w_up": lambda g: g,
    "w_down": lambda g: g.reshape(N_DEV, 512, D_MODEL),
    "w_ple_gate": lambda g: g.astype(bf16).reshape(N_DEV, 128, D_MODEL),
    "w_ple": lambda g: _blocks_of_columns(g.astype(bf16)),
}


class _StepComm:
    def __init__(self, me, ready, gathers, token):
        self.me, self.ready, self.gathers, self.token = me, ready, gathers, token
        self.sent = []

    def start_token(self):
        return self.token

    def weight(self, name, after=()):
        if name not in self.ready:
            (land,), _ = _exchange_wait([self.gathers[name]], False, "gather_wait_" + name, list(after))
            self.ready[name] = _WHOLE[name](land)
        return self.ready[name]

    def send_grads(self, grads):
        names = list(grads)
        blocks = [_BY_BLOCK[n](grads[n]) for n in names]
        states, self.token = _exchange_start(blocks, True, "scatter_start_" + names[0], self.token)
        self.sent.append((names, states))
        return self.token


def kernel(x, p, mix_norm_g, w_in, ssd_conv_w, ssd_conv_b, dt_bias, A_log, D_skip, ssd_norm_g, conf_dw_w, conf_dw_b, conf_ln_g, conf_ln_b, w_out, mlp_norm_g, w_up, w_down, ple_gate_norm_g, w_ple_gate, b_ple_gate, w_ple, ple_norm_g, final_norm_g, loss_target, m_mix_norm_g, m_w_in, m_ssd_conv_w, m_ssd_conv_b, m_dt_bias, m_A_log, m_D_skip, m_ssd_norm_g, m_conf_dw_w, m_conf_dw_b, m_conf_ln_g, m_conf_ln_b, m_w_out, m_mlp_norm_g, m_w_up, m_w_down, m_ple_gate_norm_g, m_w_ple_gate, m_b_ple_gate, m_w_ple, m_ple_norm_g, m_final_norm_g, v_mix_norm_g, v_w_in, v_ssd_conv_w, v_ssd_conv_b, v_dt_bias, v_A_log, v_D_skip, v_ssd_norm_g, v_conf_dw_w, v_conf_dw_b, v_conf_ln_g, v_conf_ln_b, v_w_out, v_mlp_norm_g, v_w_up, v_w_down, v_ple_gate_norm_g, v_w_ple_gate, v_b_ple_gate, v_w_ple, v_ple_norm_g, v_final_norm_g):
    wts = dict(mix_norm_g=mix_norm_g, w_in=w_in, ssd_conv_w=ssd_conv_w, ssd_conv_b=ssd_conv_b, dt_bias=dt_bias, A_log=A_log,
               D_skip=D_skip, ssd_norm_g=ssd_norm_g, conf_dw_w=conf_dw_w, conf_dw_b=conf_dw_b, conf_ln_g=conf_ln_g,
               conf_ln_b=conf_ln_b, w_out=w_out, mlp_norm_g=mlp_norm_g, w_up=w_up, w_down=w_down,
               ple_gate_norm_g=ple_gate_norm_g, w_ple_gate=w_ple_gate, b_ple_gate=b_ple_gate, w_ple=w_ple,
               ple_norm_g=ple_norm_g, final_norm_g=final_norm_g)
    mom1 = dict(mix_norm_g=m_mix_norm_g, w_in=m_w_in, ssd_conv_w=m_ssd_conv_w, ssd_conv_b=m_ssd_conv_b, dt_bias=m_dt_bias,
                A_log=m_A_log, D_skip=m_D_skip, ssd_norm_g=m_ssd_norm_g, conf_dw_w=m_conf_dw_w, conf_dw_b=m_conf_dw_b,
                conf_ln_g=m_conf_ln_g, conf_ln_b=m_conf_ln_b, w_out=m_w_out, mlp_norm_g=m_mlp_norm_g, w_up=m_w_up,
                w_down=m_w_down, ple_gate_norm_g=m_ple_gate_norm_g, w_ple_gate=m_w_ple_gate, b_ple_gate=m_b_ple_gate,
                w_ple=m_w_ple, ple_norm_g=m_ple_norm_g, final_norm_g=m_final_norm_g)
    mom2 = dict(mix_norm_g=v_mix_norm_g, w_in=v_w_in, ssd_conv_w=v_ssd_conv_w, ssd_conv_b=v_ssd_conv_b, dt_bias=v_dt_bias,
                A_log=v_A_log, D_skip=v_D_skip, ssd_norm_g=v_ssd_norm_g, conf_dw_w=v_conf_dw_w, conf_dw_b=v_conf_dw_b,
                conf_ln_g=v_conf_ln_g, conf_ln_b=v_conf_ln_b, w_out=v_w_out, mlp_norm_g=v_mlp_norm_g, w_up=v_w_up,
                w_down=v_w_down, ple_gate_norm_g=v_ple_gate_norm_g, w_ple_gate=v_w_ple_gate, b_ple_gate=v_b_ple_gate,
                w_ple=v_w_ple, ple_norm_g=v_ple_norm_g, final_norm_g=v_final_norm_g)
    x_pos, y_pos, c_pos = _mesh_pos()
    me = 4 * x_pos + 2 * y_pos + c_pos

    first = _all_gather([wts["w_in"][0].T.astype(bf16), wts["ssd_conv_w"][0], wts["conf_dw_w"][0]], "gather_first")
    ready = {
        "w_in": first[0].reshape(IN_WIDTH, D_MODEL),
        "ssd_conv_w": jnp.pad(_columns_of_blocks(first[1]), ((0, 8 - SSD_CONV), (0, 0))),
        "conf_dw_w": jnp.pad(_columns_of_blocks(first[2]), ((0, 32 - CONF_KERNEL), (0, 0))),
    }
    shards = [wts[n][0].astype(bf16) for n in _LATER]
    states, token = _exchange_start(shards, False, "gather_start", first[1])
    comm = _StepComm(me, ready, dict(zip(_LATER, states)), token)
    small = {n: wts[n].reshape(1, -1) for n in _REPLICATED}

    grad_x, acc = _local_step(x[0], p[0, 0], loss_target[0], small, comm)

    packed = _pack_small(acc)
    (small_state,), _ = _exchange_start([packed], False, "small_start", comm.token)
    grads, delta, new_m, new_v = {}, {}, {}, {}
    me_index = me.astype(jnp.int32).reshape(1)

    def adamw_big(names, sent, after):
        lands, mine = _exchange_wait(sent, True, "scatter_wait_" + names[0], after)
        for n, land, own in zip(names, lands, mine):
            view = (lambda a: a[0].T) if n == "w_in" else (lambda a: a[0])
            back = (lambda a: a.T[None]) if n == "w_in" else (lambda a: a[None])
            out = _adamw_big(land, own, me_index, view(wts[n]), view(mom1[n]), view(mom2[n]), "adamw_" + n)
            grads[n], delta[n], new_m[n], new_v[n] = [back(a) for a in out]

    for names, sent in comm.sent[:-1]:
        adamw_big(names, sent, [grad_x])
    adamw_big(*comm.sent[-1], [delta[n] for n in _LATER])
    (all_small,), _ = _exchange_wait([small_state], False, "small_wait", [delta["w_in"]])
    as_row = lambda d: {n: d[n].reshape(1, -1) for n in _REPLICATED}
    loss, per_name, conv_w_sum, conf_w_sum = _small_update(all_small, as_row(wts), as_row(mom1), as_row(mom2))
    for n in _REPLICATED:
        grads[n], delta[n], new_m[n], new_v[n] = [a.reshape(wts[n].shape) for a in per_name[n]]
    filt_g = [lax.dynamic_slice_in_dim(conv_w_sum, me * 192, 192, axis=1),
              lax.dynamic_slice_in_dim(conf_w_sum, me * 128, 128, axis=1)]
    filt = _adamw_filters(filt_g, [wts[n][0] for n in _CONV_WEIGHTS], [mom1[n][0] for n in _CONV_WEIGHTS],
                          [mom2[n][0] for n in _CONV_WEIGHTS])
    for n, g, (d, mn, vn) in zip(_CONV_WEIGHTS, filt_g, filt):
        grads[n], delta[n], new_m[n], new_v[n] = g[None], d[None], mn[None], vn[None]

    return (loss.reshape(()), grad_x[None], *[grads[n] for n in _WEIGHT_ORDER], *[delta[n] for n in _WEIGHT_ORDER],
            *[new_m[n] for n in _WEIGHT_ORDER], *[new_v[n] for n in _WEIGHT_ORDER])
```

```python
import functools

import jax
import jax.numpy as jnp
from jax import lax
from jax.experimental import pallas as pl
from jax.experimental.pallas import tpu as pltpu

f32 = jnp.float32
bf16 = jnp.bfloat16

EPS = 1e-6
D_MODEL = 1024
SSD_HEADS = 16
SSD_HEAD_DIM = 64
SSD_GROUPS = 2
SSD_STATE = 128
SSD_CONV = 4
CHUNK = 128
XBC_WIDTH = 1536
CONF_KERNEL = 31
D_FF = 4096
PLE_DIM = 256
IN_WIDTH = 4624
N_DEV = 8
SEG_Z = (0, 1024)
SEG_XBC = (1024, 2560)
SEG_DT = (2560, 2576)
SEG_CV = (2576, 3600)
SEG_CG = (3600, 4624)
IN_SEGS = (SEG_Z, SEG_XBC, SEG_DT, SEG_CV, SEG_CG)
BWD_SEGS = (SEG_Z, SEG_XBC, SEG_DT, (SEG_CV[0], SEG_CG[1]))

ADAM_LR = 0.001
ADAM_B1 = 0.9
ADAM_B2 = 0.999
ADAM_EPS = 1e-08
ADAM_WD = 0.01
ADAM_STEP = 10

VMEM_LIMIT_BYTES = 56 * 1024 * 1024
TOKEN_TILE = 512
SMALL_ROWS = 16

HBM_SPEC = pl.BlockSpec(memory_space=pltpu.HBM)
MESH_ID = pl.DeviceIdType.MESH


def _params(*semantics):
    return pltpu.CompilerParams(dimension_semantics=semantics, vmem_limit_bytes=VMEM_LIMIT_BYTES)


def _nn(a, b):
    return lax.dot_general(a, b, (((1,), (0,)), ((), ())), preferred_element_type=f32)


def _nt(a, b):
    return lax.dot_general(a, b, (((1,), (1,)), ((), ())), preferred_element_type=f32)


def _tn(a, b):
    return lax.dot_general(a, b, (((0,), (0,)), ((), ())), preferred_element_type=f32)


@jax.custom_vjp
def bnn(a, b):
    return _nn(a.astype(bf16), b.astype(bf16))


def _bnn_fwd(a, b):
    ab, bb = a.astype(bf16), b.astype(bf16)
    return _nn(ab, bb), (ab, bb)


def _bnn_bwd(res, g):
    ab, bb = res
    gb = g.astype(bf16)
    return _nt(gb, bb), _tn(ab, gb)


bnn.defvjp(_bnn_fwd, _bnn_bwd)


@jax.custom_vjp
def bnt(a, b):
    return _nt(a.astype(bf16), b.astype(bf16))


def _bnt_fwd(a, b):
    ab, bb = a.astype(bf16), b.astype(bf16)
    return _nt(ab, bb), (ab, bb)


def _bnt_bwd(res, g):
    ab, bb = res
    gb = g.astype(bf16)
    return _nn(gb, bb), _tn(gb, ab)


bnt.defvjp(_bnt_fwd, _bnt_bwd)


@jax.custom_vjp
def btn(a, b):
    return _tn(a.astype(bf16), b.astype(bf16))


def _btn_fwd(a, b):
    ab, bb = a.astype(bf16), b.astype(bf16)
    return _tn(ab, bb), (ab, bb)


def _btn_bwd(res, g):
    ab, bb = res
    gb = g.astype(bf16)
    return _nt(bb, gb), _nn(ab, gb)


btn.defvjp(_btn_fwd, _btn_bwd)

SCAN_CHUNKS = 4
CUMSUM_PASSES = 3
EXPAND_PASSES = 2


def _bf16_pieces(a, passes):
    pieces, rest = [], a
    for _ in range(passes):
        piece = rest.astype(bf16)
        pieces.append(piece)
        rest = rest - piece.astype(f32)
    return pieces


def _make_dot01(form, passes):
    fwd, bwd = {"right": (lambda b, p: _nn(p, b), lambda b, g: _nt(g, b)),
                "left": (lambda b, p: _nn(b, p), lambda b, g: _tn(b, g)),
                "tn": (lambda b, p: _tn(p, b), lambda b, g: _nt(b, g))}[form]

    def run(op, b, v):
        out = None
        for piece in _bf16_pieces(v, passes):
            term = op(b, piece)
            out = term if out is None else out + term
        return out

    @jax.custom_vjp
    def product(b, v):
        return run(fwd, b, v)

    product.defvjp(lambda b, v: (run(fwd, b, v), b), lambda b, g: (jnp.zeros_like(b), run(bwd, b, g)))
    return product


_DOT01 = {(form, passes): _make_dot01(form, passes)
          for form, passes in (("left", CUMSUM_PASSES), ("tn", CUMSUM_PASSES), ("right", EXPAND_PASSES))}


def _dot01(b01, a, form, passes):
    return _DOT01[form, passes](b01.astype(bf16), a)


def _rms(x, g):
    return x * lax.rsqrt(jnp.mean(x * x, axis=-1, keepdims=True) + EPS) * g


def _gated_norm(y, z, g):
    v = y * jax.nn.silu(z)
    half = v.shape[-1] // SSD_GROUPS
    parts = []
    for k in range(SSD_GROUPS):
        vk = v[:, k * half:(k + 1) * half]
        parts.append(vk * lax.rsqrt(jnp.mean(vk * vk, axis=-1, keepdims=True) + EPS))
    return jnp.concatenate(parts, axis=-1) * g


def _ln_silu(v, g, b):
    mu = jnp.mean(v, axis=-1, keepdims=True)
    xc = v - mu
    y = xc * lax.rsqrt(jnp.mean(xc * xc, axis=-1, keepdims=True) + EPS) * g + b
    return jax.nn.silu(y)


def _acc_init(i, *refs):
    @pl.when(i == 0)
    def _():
        for r in refs:
            r[...] = jnp.zeros_like(r)


def _after_token(body, token):
    if token is None:
        return body, [], []

    def body_after(tok_ref, *refs):
        del tok_ref
        body(*refs)

    return body_after, [token], [pl.BlockSpec(memory_space=pl.ANY)]


def _row_spec(tm, n):
    return pl.BlockSpec((tm, n), lambda i: (i, 0))


def _const_spec(shape):
    nd = len(shape)
    return pl.BlockSpec(shape, lambda i: (0,) * nd)


def _prev_halo_spec(tm, halo, n):
    return pl.BlockSpec((halo, n), lambda i: (jnp.maximum(i * (tm // halo) - 1, 0), 0))


def _next_halo_spec(tm, halo, n, t):
    return pl.BlockSpec((halo, n), lambda i: (jnp.minimum((i + 1) * (tm // halo), t // halo - 1), 0))


def _in_proj(x, g1, w_in_p, token=None):
    t = x.shape[0]
    tm = min(TOKEN_TILE, t)

    def body(x_ref, g_ref, w_ref, u_ref, z_ref, xbc_ref, dt_ref, cv_ref, cg_ref):
        u = _rms(x_ref[...], g_ref[...]).astype(bf16)
        u_ref[...] = u
        for (lo, hi), o_ref in zip(IN_SEGS, (z_ref, xbc_ref, dt_ref, cv_ref, cg_ref)):
            o_ref[...] = _nt(u, w_ref[lo:hi, :])

    widths = [hi - lo for lo, hi in IN_SEGS]
    outs = [jax.ShapeDtypeStruct((t, D_MODEL), bf16)] + [jax.ShapeDtypeStruct((t, n), f32) for n in widths]
    body, tok, tok_spec = _after_token(body, token)
    return pl.pallas_call(
        body, name="in_proj", grid=(t // tm,),
        in_specs=tok_spec + [_row_spec(tm, D_MODEL), _const_spec((1, D_MODEL)), _const_spec((IN_WIDTH, D_MODEL))],
        out_specs=[_row_spec(tm, D_MODEL)] + [_row_spec(tm, n) for n in widths],
        out_shape=outs, compiler_params=_params("arbitrary"),
    )(*tok, x, g1, w_in_p)


ROW_BLOCK = 16
LANE_CHUNK = 512


def _blocks(rows, cols):
    return [(slice(r, r + ROW_BLOCK), slice(c, c + LANE_CHUNK))
            for c in range(0, cols, LANE_CHUNK) for r in range(0, rows, ROW_BLOCK)]


def _conv4_block(ext_ref, w_ref, b_ref, rs, ls):
    acc = b_ref[:, ls] + w_ref[0:1, ls] * ext_ref[rs.start + 8 - 3:rs.stop + 8 - 3, ls]
    for k in range(1, SSD_CONV):
        acc = acc + w_ref[k:k + 1, ls] * ext_ref[rs.start + 8 - 3 + k:rs.stop + 8 - 3 + k, ls]
    return acc


def _ssd_pre(xbc_raw, dt_raw, cw, cb, dt_bias):
    t = xbc_raw.shape[0]
    tm = min(TOKEN_TILE, t)

    def body(cur_ref, halo_ref, dt_ref, w_ref, b_ref, dtb_ref, act_ref, dto_ref, ext_ref):
        i = pl.program_id(0)
        ext_ref[0:8, :] = jnp.where(i == 0, 0.0, halo_ref[...])
        ext_ref[8:, :] = cur_ref[...]
        for rs, ls in _blocks(tm, XBC_WIDTH):
            act_ref[rs, ls] = jax.nn.silu(_conv4_block(ext_ref, w_ref, b_ref, rs, ls))
        dto_ref[...] = jax.nn.softplus(dt_ref[...] + dtb_ref[...])

    return pl.pallas_call(
        body, name="ssd_pre", grid=(t // tm,),
        in_specs=[_row_spec(tm, XBC_WIDTH), _prev_halo_spec(tm, 8, XBC_WIDTH), _row_spec(tm, SSD_HEADS),
                  _const_spec((8, XBC_WIDTH)), _const_spec((1, XBC_WIDTH)), _const_spec((1, SSD_HEADS))],
        out_specs=[_row_spec(tm, XBC_WIDTH), _row_spec(tm, SSD_HEADS)],
        out_shape=(jax.ShapeDtypeStruct((t, XBC_WIDTH), f32), jax.ShapeDtypeStruct((t, SSD_HEADS), f32)),
        scratch_shapes=[pltpu.VMEM((tm + 8, XBC_WIDTH), f32)],
        compiler_params=_params("arbitrary"),
    )(xbc_raw, xbc_raw, dt_raw, cw, cb, dt_bias)


def _ssd_consts():
    r = lax.broadcasted_iota(jnp.int32, (CHUNK, CHUNK), 0)
    c = lax.broadcasted_iota(jnp.int32, (CHUNK, CHUNK), 1)
    causal = r >= c
    tril = causal.astype(f32)
    triu = (r <= c).astype(f32)
    hh = lax.broadcasted_iota(jnp.int32, (SSD_HEADS, D_MODEL), 0)
    jj = lax.broadcasted_iota(jnp.int32, (SSD_HEADS, D_MODEL), 1)
    expand = (lax.shift_right_logical(jj, 6) == hh).astype(f32)
    lane = lax.broadcasted_iota(jnp.int32, (1, 2 * SSD_HEAD_DIM), 1)
    m0 = (lane < SSD_HEAD_DIM).astype(f32)
    return causal, tril, triu, expand, m0, 1.0 - m0


def _ssd_chunk(xs, bm, cm, dt, s_prev, a_log, d_skip, consts):
    causal, tril, triu, expand, m0, m1 = consts
    a = dt * (-jnp.exp(a_log))
    cs = _dot01(tril, a, "left", CUMSUM_PASSES)
    cs_t = _dot01(triu, a, "tn", CUMSUM_PASSES)
    cs_last = cs[CHUNK - 1:CHUNK, :]
    per_head = jnp.concatenate([dt, jnp.exp(cs_last - cs), jnp.exp(cs), jnp.broadcast_to(jnp.exp(cs_last), (8, SSD_HEADS)),
                                jnp.broadcast_to(d_skip, (8, SSD_HEADS))], axis=0)
    per_channel = _dot01(expand, per_head, "right", EXPAND_PASSES)
    dt_e, dec_end, dec_start = (per_channel[i * CHUNK:(i + 1) * CHUNK] for i in range(3))
    chunk_dec = per_channel[3 * CHUNK:3 * CHUNK + 1]
    d_e = per_channel[3 * CHUNK + 8:3 * CHUNK + 9]
    xc = xs * dt_e
    x_dec = xc * dec_end
    gw = D_MODEL // SSD_GROUPS
    ys, states = [], []
    for g in range(SSD_GROUPS):
        bg = bm[:, g * SSD_STATE:(g + 1) * SSD_STATE]
        cg = cm[:, g * SSD_STATE:(g + 1) * SSD_STATE]
        sp = s_prev[:, g * gw:(g + 1) * gw]
        states.append(sp * chunk_dec[:, g * gw:(g + 1) * gw] + btn(bg, x_dec[:, g * gw:(g + 1) * gw]))
        y_off = bnn(cg, sp) * dec_start[:, g * gw:(g + 1) * gw]
        scores = bnt(cg, bg)
        pieces = []
        for pr in range(gw // (2 * SSD_HEAD_DIM)):
            lo = g * gw + pr * 2 * SSD_HEAD_DIM
            xp = xc[:, lo:lo + 2 * SSD_HEAD_DIM]
            mats = []
            for h in (lo // SSD_HEAD_DIM, lo // SSD_HEAD_DIM + 1):
                seg = cs[:, h:h + 1] - cs_t[h:h + 1, :]
                mats.append(scores * jnp.exp(jnp.where(causal, seg, -jnp.inf)))
            pieces.append(bnn(jnp.concatenate(mats, axis=1), jnp.concatenate([xp * m0, xp * m1], axis=0)))
        ys.append(jnp.concatenate(pieces, axis=-1) + y_off)
    y = jnp.concatenate(ys, axis=-1) + xs * d_e
    return y, jnp.concatenate(states, axis=-1)


def _ssd_scan(xbc_act, dt, a_log, d_skip):
    t = xbc_act.shape[0]
    nc = t // CHUNK
    rows = SCAN_CHUNKS * CHUNK

    def body(xbc_ref, dt_ref, al_ref, dk_ref, y_ref, sp_ref, state_ref):
        c = pl.program_id(0)

        @pl.when(c == 0)
        def _():
            state_ref[...] = jnp.zeros_like(state_ref)

        state = state_ref[...]
        consts = _ssd_consts()
        for j in range(SCAN_CHUNKS):
            rs = slice(j * CHUNK, (j + 1) * CHUNK)
            sp_ref[j] = state
            y_ref[rs, :], state = _ssd_chunk(xbc_ref[rs, 0:1024], xbc_ref[rs, 1024:1280], xbc_ref[rs, 1280:1536],
                                             dt_ref[rs, :], state, al_ref[...], dk_ref[...], consts)
        state_ref[...] = state

    return pl.pallas_call(
        body, name="ssd_scan", grid=(nc // SCAN_CHUNKS,),
        in_specs=[_row_spec(rows, XBC_WIDTH), _row_spec(rows, SSD_HEADS), _const_spec((1, SSD_HEADS)),
                  _const_spec((1, SSD_HEADS))],
        out_specs=[_row_spec(rows, D_MODEL), pl.BlockSpec((SCAN_CHUNKS, SSD_STATE, D_MODEL), lambda c: (c, 0, 0))],
        out_shape=(jax.ShapeDtypeStruct((t, D_MODEL), f32), jax.ShapeDtypeStruct((nc, SSD_STATE, D_MODEL), f32)),
        scratch_shapes=[pltpu.VMEM((SSD_STATE, D_MODEL), f32)],
        compiler_params=_params("arbitrary"),
    )(xbc_act, dt, a_log, d_skip)


CONV_HALO = 32
CONV_BLOCK = 128
LANE_TILE = 128
SUBLANES = 8


def _fill_glu_ext(ext_ref, cv_ref, cg_ref, hcv_ref, hcg_ref, i):
    ext_ref[0:CONV_HALO, :] = jnp.where(i == 0, 0.0, hcv_ref[...] * jax.nn.sigmoid(hcg_ref[...]))
    ext_ref[CONV_HALO:, :] = cv_ref[...] * jax.nn.sigmoid(cg_ref[...])


def _shifted_copies(src_ref, dst_ref, rows):
    for s in range(1, SUBLANES):
        dst_ref[s, 0:rows, :] = src_ref[pl.ds(s, rows), :]


def _tap_source(src_ref, shifted_ref, offset):
    s = offset % SUBLANES
    return (src_ref if s == 0 else shifted_ref.at[s]), offset - s


def _conv_rows(src_ref, shifted_ref, taps, offsets, lanes, r0, init):
    accs = [init] * (CONV_BLOCK // SUBLANES)
    for k, off in enumerate(offsets):
        ref, base = _tap_source(src_ref, shifted_ref, off)
        for j in range(len(accs)):
            accs[j] = accs[j] + taps[k] * ref[pl.ds(r0 + base + SUBLANES * j, SUBLANES), lanes]
    return accs


def _conf_conv(cv, cg, w, b):
    t = cv.shape[0]
    tm = min(TOKEN_TILE, t)
    ext_rows = tm + CONV_HALO
    offsets = [CONV_HALO - (CONF_KERNEL - 1) + k for k in range(CONF_KERNEL)]

    def body(cv_ref, cg_ref, hcv_ref, hcg_ref, w_ref, b_ref, o_ref, ext_ref, shifted_ref):
        _fill_glu_ext(ext_ref, cv_ref, cg_ref, hcv_ref, hcg_ref, pl.program_id(0))
        _shifted_copies(ext_ref, shifted_ref, ext_rows - SUBLANES)
        for lb in range(1024 // LANE_TILE):
            lanes = slice(lb * LANE_TILE, (lb + 1) * LANE_TILE)
            taps = [jnp.broadcast_to(w_ref[k:k + 1, lanes], (SUBLANES, LANE_TILE)) for k in range(CONF_KERNEL)]
            bias = jnp.broadcast_to(b_ref[:, lanes], (SUBLANES, LANE_TILE))

            def rows(rb, carry, lanes=lanes, taps=taps, bias=bias):
                r0 = pl.multiple_of(rb * CONV_BLOCK, CONV_BLOCK)
                accs = _conv_rows(ext_ref, shifted_ref, taps, offsets, lanes, r0, bias)
                for j, a in enumerate(accs):
                    o_ref[pl.ds(r0 + SUBLANES * j, SUBLANES), lanes] = a
                return carry

            lax.fori_loop(0, tm // CONV_BLOCK, rows, 0)

    return pl.pallas_call(
        body, name="conf_conv", grid=(t // tm,),
        in_specs=[_row_spec(tm, 1024), _row_spec(tm, 1024), _prev_halo_spec(tm, CONV_HALO, 1024),
                  _prev_halo_spec(tm, CONV_HALO, 1024), _const_spec((32, 1024)), _const_spec((1, 1024))],
        out_specs=_row_spec(tm, 1024), out_shape=jax.ShapeDtypeStruct((t, 1024), f32),
        scratch_shapes=[pltpu.VMEM((ext_rows, 1024), f32), pltpu.VMEM((SUBLANES, ext_rows, 1024), f32)],
        compiler_params=_params("arbitrary"),
    )(cv, cg, cv, cg, w, b)


def _out_proj(y, z, v2, x, g_ssd, ln_g, ln_b, w_out, g2):
    t = x.shape[0]
    tm = min(TOKEN_TILE, t)

    def body(y_ref, z_ref, v_ref, x_ref, gs_ref, lg_ref, lb_ref, w_ref, g2_ref, mixed_ref, h1_ref, u2_ref):
        ys = _gated_norm(y_ref[...], z_ref[...], gs_ref[...]).astype(bf16)
        yc = _ln_silu(v_ref[...], lg_ref[...], lb_ref[...]).astype(bf16)
        mixed_ref[:, 0:1024] = ys
        mixed_ref[:, 1024:2048] = yc
        h1 = x_ref[...] + _nn(ys, w_ref[0:1024, :]) + _nn(yc, w_ref[1024:2048, :])
        h1_ref[...] = h1
        u2_ref[...] = _rms(h1, g2_ref[...]).astype(bf16)

    vec = _const_spec((1, 1024))
    row = _row_spec(tm, 1024)
    return pl.pallas_call(
        body, name="out_proj", grid=(t // tm,),
        in_specs=[row, row, row, row, vec, vec, vec, _const_spec((2048, 1024)), vec],
        out_specs=[_row_spec(tm, 2048), row, row],
        out_shape=(jax.ShapeDtypeStruct((t, 2048), bf16),
                   jax.ShapeDtypeStruct((t, 1024), f32), jax.ShapeDtypeStruct((t, 1024), bf16)),
        compiler_params=_params("arbitrary"),
    )(y, z, v2, x, g_ssd, ln_g, ln_b, w_out, g2)


def _mlp_up(u2, w_up):
    t = u2.shape[0]
    tm = min(TOKEN_TILE, t)
    blk = D_FF // N_DEV

    def body(u_ref, w_ref, pre_ref, hs_ref):
        u = u_ref[...]
        for k in range(N_DEV):
            pre = _nn(u, w_ref[k])
            pre_ref[:, k * blk:(k + 1) * blk] = pre.astype(bf16)
            r = jnp.maximum(pre, 0.0)
            hs_ref[:, k * blk:(k + 1) * blk] = (r * r).astype(bf16)

    return pl.pallas_call(
        body, name="mlp_up", grid=(t // tm,),
        in_specs=[_row_spec(tm, D_MODEL), _const_spec((N_DEV, D_MODEL, D_FF // N_DEV))],
        out_specs=[_row_spec(tm, D_FF), _row_spec(tm, D_FF)],
        out_shape=(jax.ShapeDtypeStruct((t, D_FF), bf16), jax.ShapeDtypeStruct((t, D_FF), bf16)),
        compiler_params=_params("arbitrary"),
    )(u2, w_up)


def _mlp_down(h1, hs, w_down):
    t = h1.shape[0]
    tm = min(TOKEN_TILE, t)

    def body(h_ref, hs_ref, w_ref, o_ref):
        o_ref[...] = h_ref[...] + _nn(hs_ref[...], w_ref[...])

    return pl.pallas_call(
        body, name="mlp_down", grid=(t // tm,),
        in_specs=[_row_spec(tm, D_MODEL), _row_spec(tm, D_FF), _const_spec((D_FF, D_MODEL))],
        out_specs=_row_spec(tm, D_MODEL), out_shape=jax.ShapeDtypeStruct((t, D_MODEL), f32),
        compiler_params=_params("arbitrary"),
    )(h1, hs, w_down)


TAIL_LOSS, TAIL_FINAL_G, TAIL_PLE_G, TAIL_GATE_B, TAIL_GATE_NORM_G = 0, 1, 2, 3, 4


def _tail(h2, p, target, g3, w_gate, b_gate, w_ple, ple_g, fin_g):
    t = h2.shape[0]
    tm = min(TOKEN_TILE, t)

    def body(h_ref, p_ref, t_ref, g3_ref, wg_ref, bg_ref, wp_ref, pg_ref, fg_ref,
             dh_ref, dhb_ref, dwg_ref, dwp_ref, acc_ref):
        i = pl.program_id(0)
        _acc_init(i, dwg_ref, dwp_ref, acc_ref)
        h2v = h_ref[...]
        tgt = t_ref[...]
        u3, vjp_u3 = jax.vjp(_rms, h2v, g3_ref[...])
        u3b = u3.astype(bf16)
        pb = p_ref[...].astype(bf16)
        gate_pre = _nn(u3b, wg_ref[...]) + bg_ref[...]
        emb_pre = _nn(pb, wp_ref[...])

        def tail_fn(hv, gp, ep, pg, fg):
            h3 = hv + jax.nn.sigmoid(gp) * _rms(ep, pg)
            err = _rms(h3, fg) - tgt
            return 0.5 * jnp.mean(err * err, axis=-1, keepdims=True)

        loss_tok, vjp_tail = jax.vjp(tail_fn, h2v, gate_pre, emb_pre, pg_ref[...], fg_ref[...])
        dh_a, dgp, dep, dpg, dfg = vjp_tail(jnp.ones_like(loss_tok))
        dgpb = dgp.astype(bf16)
        dh_b, dg3 = vjp_u3(_nt(dgpb, wg_ref[...]))
        dh = dh_a + dh_b
        dh_ref[...] = dh
        dhb_ref[...] = dh.astype(bf16)
        dwg_ref[...] += _tn(u3b, dgpb)
        dwp_ref[...] += _tn(pb, dep.astype(bf16))
        acc_ref[TAIL_LOSS:TAIL_LOSS + 1, :] += jnp.broadcast_to(jnp.sum(loss_tok, axis=0, keepdims=True), (1, 1024))
        acc_ref[TAIL_FINAL_G:TAIL_FINAL_G + 1, :] += dfg
        acc_ref[TAIL_PLE_G:TAIL_PLE_G + 1, :] += dpg
        acc_ref[TAIL_GATE_B:TAIL_GATE_B + 1, :] += jnp.sum(dgp, axis=0, keepdims=True)
        acc_ref[TAIL_GATE_NORM_G:TAIL_GATE_NORM_G + 1, :] += dg3

    vec = _const_spec((1, 1024))
    row = _row_spec(tm, 1024)
    return pl.pallas_call(
        body, name="tail", grid=(t // tm,),
        in_specs=[row, _row_spec(tm, PLE_DIM), row, vec, _const_spec((1024, 1024)), vec, _const_spec((PLE_DIM, 1024)), vec, vec],
        out_specs=[row, row, _const_spec((1024, 1024)), _const_spec((PLE_DIM, 1024)), _const_spec((8, 1024))],
        out_shape=(jax.ShapeDtypeStruct((t, 1024), f32), jax.ShapeDtypeStruct((t, 1024), bf16),
                   jax.ShapeDtypeStruct((1024, 1024), f32), jax.ShapeDtypeStruct((PLE_DIM, 1024), f32),
                   jax.ShapeDtypeStruct((8, 1024), f32)),
        compiler_params=_params("arbitrary"),
    )(h2, p, target, g3, w_gate, b_gate, w_ple, ple_g, fin_g)


def _mlp_bwd(dh2, dh2b, pre, h1, g2, w_down, w_up, token=None):
    t = dh2.shape[0]
    tm = min(TOKEN_TILE, t)
    blk = D_FF // N_DEV

    def body(dh_ref, dhb_ref, pre_ref, h1_ref, g_ref, wd_hbm, wu_hbm, dpre_ref, dh1_ref, dh1b_ref, acc_ref,
             wd_ref, wu_ref):
        i = pl.program_id(0)
        _acc_init(i, acc_ref)

        @pl.when(i == 0)
        def _():
            pltpu.sync_copy(wd_hbm, wd_ref)
            pltpu.sync_copy(wu_hbm, wu_ref)

        dhb = dhb_ref[...]
        du2 = jnp.zeros((tm, D_MODEL), f32)
        for k in range(D_FF // blk):
            dhs = _nt(dhb, wd_ref[k * blk:(k + 1) * blk, :])
            dpre = (dhs * (2.0 * jnp.maximum(pre_ref[:, k * blk:(k + 1) * blk].astype(f32), 0.0))).astype(bf16)
            dpre_ref[:, k * blk:(k + 1) * blk] = dpre
            du2 = du2 + _nt(dpre, wu_ref[k])
        _, vjp_u2 = jax.vjp(_rms, h1_ref[...], g_ref[...])
        d, dg = vjp_u2(du2)
        dh1 = dh_ref[...] + d
        dh1_ref[...] = dh1
        dh1b_ref[...] = dh1.astype(bf16)
        acc_ref[0:1, :] += dg

    row = _row_spec(tm, 1024)
    body, tok, tok_spec = _after_token(body, token)
    return pl.pallas_call(
        body, name="mlp_bwd", grid=(t // tm,),
        in_specs=tok_spec + [row, row, _row_spec(tm, D_FF), row, _const_spec((1, 1024)), HBM_SPEC, HBM_SPEC],
        out_specs=[_row_spec(tm, D_FF), row, row, _const_spec((8, 1024))],
        out_shape=(jax.ShapeDtypeStruct((t, D_FF), bf16), jax.ShapeDtypeStruct((t, 1024), f32),
                   jax.ShapeDtypeStruct((t, 1024), bf16), jax.ShapeDtypeStruct((8, 1024), f32)),
        scratch_shapes=[pltpu.VMEM((D_FF, D_MODEL), bf16), pltpu.VMEM((N_DEV, D_MODEL, D_FF // N_DEV), bf16)],
        compiler_params=_params("arbitrary"),
    )(*tok, dh2, dh2b, pre, h1, g2, w_down, w_up)


OPB_SSD_G, OPB_LN_G, OPB_LN_B, OPB_CONV_B = 0, 1, 2, 3


def _out_proj_bwd(dh1b, y, z, v2, g_ssd, ln_g, ln_b, w_out, token=None):
    t = y.shape[0]
    tm = min(TOKEN_TILE, t)

    def body(dh_ref, y_ref, z_ref, v_ref, gs_ref, lg_ref, lb_ref, w_ref, dy_ref, dz_ref, dv_ref, acc_ref):
        i = pl.program_id(0)
        _acc_init(i, acc_ref)
        dhb = dh_ref[...]
        dmixed = _nt(dhb, w_ref[...])
        dys, dyc = dmixed[:, 0:1024], dmixed[:, 1024:2048]
        _, vjp_g = jax.vjp(_gated_norm, y_ref[...], z_ref[...], gs_ref[...])
        dy, dz, dgs = vjp_g(dys)
        _, vjp_l = jax.vjp(_ln_silu, v_ref[...], lg_ref[...], lb_ref[...])
        dv, dlg, dlb = vjp_l(dyc)
        dy_ref[...] = dy
        dz_ref[...] = dz.astype(bf16)
        dv_ref[...] = dv
        acc_ref[OPB_SSD_G:OPB_SSD_G + 1, :] += dgs
        acc_ref[OPB_LN_G:OPB_LN_G + 1, :] += dlg
        acc_ref[OPB_LN_B:OPB_LN_B + 1, :] += dlb
        acc_ref[OPB_CONV_B:OPB_CONV_B + 1, :] += jnp.sum(dv, axis=0, keepdims=True)

    vec = _const_spec((1, 1024))
    row = _row_spec(tm, 1024)
    body, tok, tok_spec = _after_token(body, token)
    return pl.pallas_call(
        body, name="out_proj_bwd", grid=(t // tm,),
        in_specs=tok_spec + [row, row, row, row, vec, vec, vec, _const_spec((2048, 1024))],
        out_specs=[row, row, row, _const_spec((8, 1024))],
        out_shape=(jax.ShapeDtypeStruct((t, 1024), f32), jax.ShapeDtypeStruct((t, 1024), bf16),
                   jax.ShapeDtypeStruct((t, 1024), f32), jax.ShapeDtypeStruct((8, 1024), f32)),
        compiler_params=_params("arbitrary"),
    )(*tok, dh1b, y, z, v2, g_ssd, ln_g, ln_b, w_out)


def _conf_conv_bwd(dv2, cv, cg, w):
    t = cv.shape[0]
    tm = min(TOKEN_TILE, t)
    ext_rows = tm + CONV_HALO
    offsets = [CONF_KERNEL - 1 - k for k in range(CONF_KERNEL)]

    def body(dv_ref, dvn_ref, cv_ref, cg_ref, w_ref, dcvg_ref, dw_ref, glu_ref, dext_ref, sg_ref, shifted_ref):
        i = pl.program_id(0)
        _acc_init(i, dw_ref)
        sg = jax.nn.sigmoid(cg_ref[...])
        sg_ref[...] = sg
        glu_ref[...] = cv_ref[...] * sg
        dext_ref[0:tm, :] = dv_ref[...]
        dext_ref[tm:, :] = jnp.where(i == pl.num_programs(0) - 1, 0.0, dvn_ref[...])
        _shifted_copies(dext_ref, shifted_ref, ext_rows - SUBLANES)

        for lb in range(1024 // LANE_TILE):
            lanes = slice(lb * LANE_TILE, (lb + 1) * LANE_TILE)

            def rows_w(rb, accs, lanes=lanes):
                r0 = pl.multiple_of(rb * CONV_BLOCK, CONV_BLOCK)
                accs = list(accs)
                for j in range(CONV_BLOCK // SUBLANES):
                    x = glu_ref[pl.ds(r0 + SUBLANES * j, SUBLANES), lanes]
                    for k, off in enumerate(offsets):
                        ref, base = _tap_source(dext_ref, shifted_ref, off)
                        accs[k] = accs[k] + x * ref[pl.ds(r0 + base + SUBLANES * j, SUBLANES), lanes]
                return tuple(accs)

            zero = jnp.zeros((SUBLANES, LANE_TILE), f32)
            accs = lax.fori_loop(0, tm // CONV_BLOCK, rows_w, (zero,) * CONF_KERNEL)
            for k in range(CONF_KERNEL):
                dw_ref[k:k + 1, lanes] += jnp.sum(accs[k], axis=0, keepdims=True)

        for lb in range(1024 // LANE_TILE):
            lanes = slice(lb * LANE_TILE, (lb + 1) * LANE_TILE)
            taps = [jnp.broadcast_to(w_ref[k:k + 1, lanes], (SUBLANES, LANE_TILE)) for k in range(CONF_KERNEL)]

            def rows_x(rb, carry, lanes=lanes, taps=taps):
                r0 = pl.multiple_of(rb * CONV_BLOCK, CONV_BLOCK)
                zero = jnp.zeros((SUBLANES, LANE_TILE), f32)
                dglu = jnp.concatenate(_conv_rows(dext_ref, shifted_ref, taps, offsets, lanes, r0, zero), axis=0)
                sg = sg_ref[pl.ds(r0, CONV_BLOCK), lanes]
                cvv = cv_ref[pl.ds(r0, CONV_BLOCK), lanes]
                gate_lanes = slice(lanes.start + 1024, lanes.stop + 1024)
                dcvg_ref[pl.ds(r0, CONV_BLOCK), lanes] = (dglu * sg).astype(bf16)
                dcvg_ref[pl.ds(r0, CONV_BLOCK), gate_lanes] = (dglu * cvv * sg * (1.0 - sg)).astype(bf16)
                return carry

            lax.fori_loop(0, tm // CONV_BLOCK, rows_x, 0)

    row = _row_spec(tm, 1024)
    return pl.pallas_call(
        body, name="conf_conv_bwd", grid=(t // tm,),
        in_specs=[row, _next_halo_spec(tm, CONV_HALO, 1024, t), row, row, _const_spec((32, 1024))],
        out_specs=[_row_spec(tm, 2048), _const_spec((32, 1024))],
        out_shape=(jax.ShapeDtypeStruct((t, 2048), bf16), jax.ShapeDtypeStruct((32, 1024), f32)),
        scratch_shapes=[pltpu.VMEM((tm, 1024), f32), pltpu.VMEM((ext_rows, 1024), f32), pltpu.VMEM((tm, 1024), f32),
                        pltpu.VMEM((SUBLANES, ext_rows, 1024), f32)],
        compiler_params=_params("arbitrary"),
    )(dv2, dv2, cv, cg, w)


def _ssd_scan_bwd(xbc_act, dt, s_prev, dy, a_log, d_skip):
    t = xbc_act.shape[0]
    steps = t // CHUNK // SCAN_CHUNKS
    rows = SCAN_CHUNKS * CHUNK

    def body(xbc_ref, dt_ref, sp_ref, dy_ref, al_ref, dk_ref, dxbc_ref, ddt_ref, dal_ref, ddk_ref, ds_ref):
        i = pl.program_id(0)
        _acc_init(i, ds_ref, dal_ref, ddk_ref)
        consts = _ssd_consts()
        ds = ds_ref[...]
        dal_sum, ddk_sum = dal_ref[...], ddk_ref[...]
        for j in reversed(range(SCAN_CHUNKS)):
            rs = slice(j * CHUNK, (j + 1) * CHUNK)
            _, vjp_c = jax.vjp(
                functools.partial(_ssd_chunk, consts=consts),
                xbc_ref[rs, 0:1024], xbc_ref[rs, 1024:1280], xbc_ref[rs, 1280:1536], dt_ref[rs, :], sp_ref[j],
                al_ref[...], dk_ref[...])
            dxs, dbm, dcm, ddt, ds, dal, ddk = vjp_c((dy_ref[rs, :], ds))
            dxbc_ref[rs, 0:1024] = dxs
            dxbc_ref[rs, 1024:1280] = dbm
            dxbc_ref[rs, 1280:1536] = dcm
            ddt_ref[rs, :] = ddt
            dal_sum, ddk_sum = dal_sum + dal, ddk_sum + ddk
        ds_ref[...] = ds
        dal_ref[...] = dal_sum
        ddk_ref[...] = ddk_sum

    rev = lambda i: (steps - 1 - i, 0)
    return pl.pallas_call(
        body, name="ssd_scan_bwd", grid=(steps,),
        in_specs=[pl.BlockSpec((rows, XBC_WIDTH), rev), pl.BlockSpec((rows, SSD_HEADS), rev),
                  pl.BlockSpec((SCAN_CHUNKS, SSD_STATE, D_MODEL), lambda i: (steps - 1 - i, 0, 0)),
                  pl.BlockSpec((rows, D_MODEL), rev), _const_spec((1, SSD_HEADS)), _const_spec((1, SSD_HEADS))],
        out_specs=[pl.BlockSpec((rows, XBC_WIDTH), rev), pl.BlockSpec((rows, SSD_HEADS), rev),
                   _const_spec((1, SSD_HEADS)), _const_spec((1, SSD_HEADS))],
        out_shape=(jax.ShapeDtypeStruct((t, XBC_WIDTH), f32), jax.ShapeDtypeStruct((t, SSD_HEADS), f32),
                   jax.ShapeDtypeStruct((1, SSD_HEADS), f32), jax.ShapeDtypeStruct((1, SSD_HEADS), f32)),
        scratch_shapes=[pltpu.VMEM((SSD_STATE, D_MODEL), f32)],
        compiler_params=_params("arbitrary"),
    )(xbc_act, dt, s_prev, dy, a_log, d_skip)


def _ssd_pre_bwd(xbc_raw, dxbc_act, ddt, dt_raw, cw, cb, dt_bias, u):
    t = xbc_raw.shape[0]
    tm = min(TOKEN_TILE, t)

    def body(cur_ref, halo_ref, dact_ref, ddt_ref, dtr_ref, w_ref, b_ref, dtb_ref, u_ref,
             dco_ref, ddtr_ref, dw_ref, db_ref, ddtb_ref, dwdt_ref, ext_ref):
        i = pl.program_id(0)
        _acc_init(i, dw_ref, db_ref, ddtb_ref, dwdt_ref)
        ext_ref[0:8, :] = jnp.where(i == 0, 0.0, halo_ref[...])
        ext_ref[8:, :] = cur_ref[...]
        fold = lambda a: a[0:8] + a[8:ROW_BLOCK]
        for c in range(0, XBC_WIDTH, LANE_CHUNK):
            ls = slice(c, c + LANE_CHUNK)
            part_b = jnp.zeros((8, LANE_CHUNK), f32)
            part_w = [jnp.zeros((8, LANE_CHUNK), f32)] * SSD_CONV
            for r in range(0, tm, ROW_BLOCK):
                rs = slice(r, r + ROW_BLOCK)
                co = _conv4_block(ext_ref, w_ref, b_ref, rs, ls)
                sg = jax.nn.sigmoid(co)
                dco = dact_ref[rs, ls] * sg * (1.0 + co * (1.0 - sg))
                dco_ref[rs, ls] = dco
                part_b = part_b + fold(dco)
                part_w = [pw + fold(dco * ext_ref[r + 8 - 3 + k:r + ROW_BLOCK + 8 - 3 + k, ls])
                          for k, pw in enumerate(part_w)]
            db_ref[:, ls] += jnp.sum(part_b, axis=0, keepdims=True)
            for k in range(SSD_CONV):
                dw_ref[k:k + 1, ls] += jnp.sum(part_w[k], axis=0, keepdims=True)
        ddtr = ddt_ref[...] * jax.nn.sigmoid(dtr_ref[...] + dtb_ref[...])
        ddtb_ref[...] += jnp.sum(ddtr, axis=0, keepdims=True)
        ddtr_b = ddtr.astype(bf16)
        ddtr_ref[...] = ddtr_b
        dwdt_ref[...] += _tn(ddtr_b, u_ref[...])

    return pl.pallas_call(
        body, name="ssd_pre_bwd", grid=(t // tm,),
        in_specs=[_row_spec(tm, XBC_WIDTH), _prev_halo_spec(tm, 8, XBC_WIDTH), _row_spec(tm, XBC_WIDTH),
                  _row_spec(tm, SSD_HEADS), _row_spec(tm, SSD_HEADS), _const_spec((8, XBC_WIDTH)),
                  _const_spec((1, XBC_WIDTH)), _const_spec((1, SSD_HEADS)), _row_spec(tm, D_MODEL)],
        out_specs=[_row_spec(tm, XBC_WIDTH), _row_spec(tm, SSD_HEADS), _const_spec((8, XBC_WIDTH)),
                   _const_spec((1, XBC_WIDTH)), _const_spec((1, SSD_HEADS)), _const_spec((SSD_HEADS, D_MODEL))],
        out_shape=(jax.ShapeDtypeStruct((t, XBC_WIDTH), f32), jax.ShapeDtypeStruct((t, SSD_HEADS), bf16),
                   jax.ShapeDtypeStruct((8, XBC_WIDTH), f32), jax.ShapeDtypeStruct((1, XBC_WIDTH), f32),
                   jax.ShapeDtypeStruct((1, SSD_HEADS), f32), jax.ShapeDtypeStruct((SSD_HEADS, D_MODEL), f32)),
        scratch_shapes=[pltpu.VMEM((tm + 8, XBC_WIDTH), f32)],
        compiler_params=_params("arbitrary"),
    )(xbc_raw, xbc_raw, dxbc_act, ddt, dt_raw, cw, cb, dt_bias, u)


def _conv4_bwd_data(dco, cw):
    t = dco.shape[0]
    tm = min(TOKEN_TILE, t)

    def body(cur_ref, nxt_ref, w_ref, o_ref, ext_ref):
        i = pl.program_id(0)
        ext_ref[0:tm, :] = cur_ref[...]
        ext_ref[tm:, :] = jnp.where(i == pl.num_programs(0) - 1, 0.0, nxt_ref[...])
        for rs, ls in _blocks(tm, XBC_WIDTH):
            acc = w_ref[0:1, ls] * ext_ref[rs.start + SSD_CONV - 1:rs.stop + SSD_CONV - 1, ls]
            for k in range(1, SSD_CONV):
                acc = acc + w_ref[k:k + 1, ls] * ext_ref[rs.start + SSD_CONV - 1 - k:rs.stop + SSD_CONV - 1 - k, ls]
            o_ref[rs, ls] = acc.astype(bf16)

    return pl.pallas_call(
        body, name="conv4_bwd_data", grid=(t // tm,),
        in_specs=[_row_spec(tm, XBC_WIDTH), _next_halo_spec(tm, 8, XBC_WIDTH, t), _const_spec((8, XBC_WIDTH))],
        out_specs=_row_spec(tm, XBC_WIDTH), out_shape=jax.ShapeDtypeStruct((t, XBC_WIDTH), bf16),
        scratch_shapes=[pltpu.VMEM((tm + 8, XBC_WIDTH), f32)],
        compiler_params=_params("arbitrary"),
    )(dco, dco, cw)


def _in_proj_bwd(dproj, x, dh1, g1, w_in_p, token=None):
    t = x.shape[0]
    tm = min(TOKEN_TILE, t)

    def body(dz_ref, dxbc_ref, ddt_ref, dcvg_ref, x_ref, dh_ref, g_ref, w_ref, dx_ref, acc_ref):
        i = pl.program_id(0)
        _acc_init(i, acc_ref)
        du = jnp.zeros((tm, D_MODEL), f32)
        for (lo, hi), r in zip(BWD_SEGS, (dz_ref, dxbc_ref, ddt_ref, dcvg_ref)):
            du = du + _nn(r[...], w_ref[lo:hi, :])
        _, vjp_u = jax.vjp(_rms, x_ref[...], g_ref[...])
        d, dg = vjp_u(du)
        dx_ref[...] = dh_ref[...] + d
        acc_ref[0:1, :] += dg

    row = _row_spec(tm, 1024)
    body, tok, tok_spec = _after_token(body, token)
    return pl.pallas_call(
        body, name="in_proj_bwd", grid=(t // tm,),
        in_specs=tok_spec + [_row_spec(tm, hi - lo) for lo, hi in BWD_SEGS] + [row, row, _const_spec((1, 1024)),
                                                                               _const_spec((IN_WIDTH, D_MODEL))],
        out_specs=[row, _const_spec((8, 1024))],
        out_shape=(jax.ShapeDtypeStruct((t, 1024), f32), jax.ShapeDtypeStruct((8, 1024), f32)),
        compiler_params=_params("arbitrary"),
    )(*tok, *dproj, x, dh1, g1, w_in_p)


def _mm_tn(a, b, tk, tn, name, column_blocks=False):
    t, kk = a.shape
    n = b.shape[1]
    tk, tn = min(tk, kk), min(tn, n)

    def body(a_ref, b_ref, o_ref):
        o_ref[...] = _tn(a_ref[...], b_ref[...]).astype(bf16)

    if column_blocks:
        assert tk == kk
        out_spec = pl.BlockSpec((None, tk, tn), lambda i, j: (j, 0, 0))
        out_shape = jax.ShapeDtypeStruct((n // tn, kk, tn), bf16)
    else:
        out_spec = pl.BlockSpec((tk, tn), lambda i, j: (i, j))
        out_shape = jax.ShapeDtypeStruct((kk, n), bf16)
    return pl.pallas_call(
        body, name=name, grid=(kk // tk, n // tn),
        in_specs=[pl.BlockSpec((t, tk), lambda i, j: (0, i)), pl.BlockSpec((t, tn), lambda i, j: (0, j))],
        out_specs=out_spec, out_shape=out_shape,
        compiler_params=_params("arbitrary", "arbitrary"),
    )(a, b)


class _LocalWeights:
    def __init__(self, w):
        self.w = w
        self.sent = {}

    def start_token(self):
        return None

    def weight(self, name, after=()):
        del after
        return self.w[name]

    def send_grads(self, grads):
        self.sent.update(grads)
        return None


def _local_step(x, p, target, s, comm):
    conv_w, conf_w = comm.weight("ssd_conv_w"), comm.weight("conf_dw_w")
    w_in = comm.weight("w_in")
    u, z, xbc_raw, dt_raw, cv, cg = _in_proj(x, s["mix_norm_g"], w_in, token=comm.start_token())
    xbc_act, dt = _ssd_pre(xbc_raw, dt_raw, conv_w, s["ssd_conv_b"], s["dt_bias"])
    y, s_prev = _ssd_scan(xbc_act, dt, s["A_log"], s["D_skip"])
    v2 = _conf_conv(cv, cg, conf_w, s["conf_dw_b"])
    w_out = comm.weight("w_out", after=(y, v2))
    mixed, h1, u2 = _out_proj(y, z, v2, x, s["ssd_norm_g"], s["conf_ln_g"], s["conf_ln_b"], w_out, s["mlp_norm_g"])
    w_up = comm.weight("w_up", after=(u2,))
    pre, hs = _mlp_up(u2, w_up)
    w_down = comm.weight("w_down", after=(hs,))
    h2 = _mlp_down(h1, hs, w_down)
    w_gate, w_ple = comm.weight("w_ple_gate", after=(h2,)), comm.weight("w_ple", after=(h2,))
    dh2, dh2b, dwg, dwp, tail_acc = _tail(h2, p, target, s["ple_gate_norm_g"], w_gate, s["b_ple_gate"], w_ple,
                                          s["ple_norm_g"], s["final_norm_g"])
    token = comm.send_grads({"w_ple_gate": dwg, "w_ple": dwp})
    dpre, dh1, dh1b, mlp_acc = _mlp_bwd(dh2, dh2b, pre, h1, s["mlp_norm_g"], w_down, w_up, token=token)
    token = comm.send_grads({
        "w_down": _mm_tn(hs, dh2b, 512, 1024, "dw_down"),
        "w_up": _mm_tn(u2, dpre, 1024, D_FF // N_DEV, "dw_up", column_blocks=True),
        "w_out": _mm_tn(mixed, dh1b, 512, 1024, "dw_out"),
    })
    dy, dz, dv2, opb_acc = _out_proj_bwd(dh1b, y, z, v2, s["ssd_norm_g"], s["conf_ln_g"], s["conf_ln_b"], w_out,
                                         token=token)
    dcvg, dconf_w = _conf_conv_bwd(dv2, cv, cg, conf_w)
    dxbc_act, ddt, d_alog, d_dskip = _ssd_scan_bwd(xbc_act, dt, s_prev, dy, s["A_log"], s["D_skip"])
    dco, ddt_raw, dconv_w, dconv_b, d_dtb, dw_in_dt = _ssd_pre_bwd(xbc_raw, dxbc_act, ddt, dt_raw, conv_w,
                                                                  s["ssd_conv_b"], s["dt_bias"], u)
    dxbc_raw = _conv4_bwd_data(dco, conv_w)
    dproj = (dz, dxbc_raw, ddt_raw, dcvg)
    token = comm.send_grads({"w_in": [_mm_tn(dz, u, 512, 1024, "dw_in_z"), _mm_tn(dxbc_raw, u, 512, 1024, "dw_in_xbc"),
                                      dw_in_dt.astype(bf16), _mm_tn(dcvg, u, 512, 1024, "dw_in_cvg")]})
    grad_x, inp_acc = _in_proj_bwd(dproj, x, dh1, s["mix_norm_g"], w_in, token=token)
    acc = {"in_proj": inp_acc, "out_proj": opb_acc, "mlp": mlp_acc, "tail": tail_acc, "ssd_conv_b": dconv_b,
           "dt_bias": d_dtb, "A_log": d_alog, "D_skip": d_dskip, "ssd_conv_w": dconv_w, "conf_dw_w": dconf_w}
    return grad_x, acc


def _mesh_pos():
    return lax.axis_index("x"), lax.axis_index("y"), lax.axis_index("c")


def _other_chips(x, y):
    return [(1 - x, y), (x, 1 - y), (1 - x, 1 - y)]


def _all_gather(arrs, name):
    n = len(arrs)

    def body(*refs):
        ins, outs = refs[:n], refs[n:2 * n]
        send_sems, recv_sems, local_sems = refs[2 * n:]
        x, y, c = _mesh_pos()
        me = 4 * x + 2 * y + c
        sibling = (x, y, 1 - c)
        chips = _other_chips(x, y)

        def copy(a, k, block, to, src=None):
            dst = outs[a].at[block]
            return pltpu.make_async_remote_copy(
                src_ref=dst if src is None else src, dst_ref=dst, send_sem=send_sems.at[a, k],
                recv_sem=recv_sems.at[a, k], device_id=to, device_id_type=MESH_ID)

        mine = [pltpu.make_async_copy(ins[a], outs[a].at[me], local_sems.at[a]) for a in range(n)]
        for cp in mine:
            cp.start()
        first = []
        for a in range(n):
            first.append(copy(a, 0, me, sibling, src=ins[a]))
            first += [copy(a, 1 + j, me, (px, py, c), src=ins[a]) for j, (px, py) in enumerate(chips)]
        for cp in first:
            cp.start()
        passed = []
        for j, (px, py) in enumerate(chips):
            for a in range(n):
                blk = 4 * px + 2 * py + c
                copy(a, 1 + j, blk, (x, y, c)).wait_recv()
                cp = copy(a, 4 + j, blk, sibling)
                cp.start()
                passed.append(cp)
        for a in range(n):
            copy(a, 0, 4 * x + 2 * y + (1 - c), (x, y, c)).wait_recv()
        for j, (px, py) in enumerate(chips):
            for a in range(n):
                copy(a, 4 + j, 4 * px + 2 * py + (1 - c), (x, y, c)).wait_recv()
        for cp in first + passed:
            cp.wait_send()
        for cp in mine:
            cp.wait()

    return pl.pallas_call(
        body, name=name,
        in_specs=[HBM_SPEC] * n, out_specs=[HBM_SPEC] * n,
        out_shape=[jax.ShapeDtypeStruct((N_DEV,) + a.shape, a.dtype) for a in arrs],
        scratch_shapes=[pltpu.SemaphoreType.DMA((n, 7)), pltpu.SemaphoreType.DMA((n, 7)), pltpu.SemaphoreType.DMA((n,))],
    )(*arrs)


_PEER_FLIPS = ((0, 0, 1), (1, 0, 0), (0, 1, 0), (1, 1, 0), (1, 0, 1), (0, 1, 1), (1, 1, 1))
SEM_SPEC = pl.BlockSpec(memory_space=pltpu.SEMAPHORE)
ANY_SPEC = pl.BlockSpec(memory_space=pl.ANY)


def _flip(v, d):
    return 1 - v if d else v


def _peers(x, y, c):
    out = []
    for dx, dy, dc in _PEER_FLIPS:
        px, py, pc = _flip(x, dx), _flip(y, dy), _flip(c, dc)
        out.append(((px, py, pc), 4 * px + 2 * py + pc))
    return out


def _exchange_copy(src_ref, land_ref, send_sems, recv_sems, k, peer, peer_block, my_block, by_block, outgoing):
    src = src_ref.at[peer_block] if by_block else src_ref
    dst = land_ref.at[my_block if outgoing else peer_block]
    return pltpu.make_async_remote_copy(src_ref=src, dst_ref=dst, send_sem=send_sems.at[k], recv_sem=recv_sems.at[k],
                                        device_id=peer, device_id_type=MESH_ID)


def _exchange_start(srcs, by_block, name, after):
    n = len(srcs)
    x_pos, y_pos, c_pos = _mesh_pos()
    me_at = 4 * x_pos + 2 * y_pos + c_pos
    if by_block:
        lands = [lax.empty(a.shape, a.dtype) for a in srcs]
    else:
        lands = [lax.dynamic_update_slice(lax.empty((N_DEV,) + a.shape, a.dtype), a[None], (me_at,) + (0,) * a.ndim)
                 for a in srcs]

    def body(*refs):
        src_refs, land_refs = refs[1:1 + n], refs[1 + n:1 + 2 * n]
        outs = refs[1 + 2 * n:]
        send, recv, token = outs[:n], outs[n:2 * n], outs[4 * n]
        x, y, c = _mesh_pos()
        me = 4 * x + 2 * y + c
        for a in range(n):
            for k, (peer, blk) in enumerate(_peers(x, y, c)):
                _exchange_copy(src_refs[a], land_refs[a], send[a], recv[a], k, peer, blk, me, by_block, True).start()
        token[...] = jnp.zeros_like(token)

    sems = [pltpu.SemaphoreType.DMA((N_DEV - 1,))] * (2 * n)
    thru = [pltpu.HBM(a.shape, a.dtype) for a in list(srcs) + list(lands)]
    res = pl.pallas_call(
        body, name=name,
        in_specs=[ANY_SPEC] + [HBM_SPEC] * (2 * n),
        out_specs=[SEM_SPEC] * (2 * n) + [HBM_SPEC] * (2 * n) + [pl.BlockSpec(memory_space=pltpu.VMEM)],
        out_shape=sems + thru + [jax.ShapeDtypeStruct((8, 128), f32)],
        input_output_aliases={1 + i: 2 * n + i for i in range(2 * n)},
        compiler_params=pltpu.CompilerParams(has_side_effects=pltpu.SideEffectType.DATAFLOW_SIDE_EFFECTING),
    )(after, *[pltpu.with_memory_space_constraint(a, pltpu.HBM) for a in list(srcs) + list(lands)])
    states = [(res[2 * n + a], res[3 * n + a], res[a], res[n + a]) for a in range(n)]
    return states, res[4 * n]


def _exchange_wait(states, by_block, name, after):
    n, na = len(states), len(after)

    def body(*refs):
        src_refs, land_refs = refs[:n], refs[n:2 * n]
        send, recv = refs[2 * n:3 * n], refs[3 * n:4 * n]
        x, y, c = _mesh_pos()
        me = 4 * x + 2 * y + c
        for a in range(n):
            for k, (peer, blk) in enumerate(_peers(x, y, c)):
                _exchange_copy(src_refs[a], land_refs[a], send[a], recv[a], k, peer, blk, me, by_block, True).wait_send()
                _exchange_copy(src_refs[a], land_refs[a], send[a], recv[a], k, peer, blk, me, by_block, False).wait_recv()

    srcs, lands = [s[0] for s in states], [s[1] for s in states]
    res = pl.pallas_call(
        body, name=name,
        in_specs=[HBM_SPEC] * (2 * n) + [SEM_SPEC] * (2 * n) + [ANY_SPEC] * na,
        out_specs=[HBM_SPEC] * (2 * n),
        out_shape=[pltpu.HBM(a.shape, a.dtype) for a in srcs + lands],
        input_output_aliases={i: i for i in range(2 * n)},
        compiler_params=pltpu.CompilerParams(has_side_effects=pltpu.SideEffectType.DATAFLOW_SIDE_EFFECTING),
    )(*srcs, *lands, *[s[2] for s in states], *[s[3] for s in states], *after)
    return list(res[n:2 * n]), list(res[:n])


def _adamw_math(w, g, m, v):
    m = ADAM_B1 * m + (1.0 - ADAM_B1) * g
    v = ADAM_B2 * v + (1.0 - ADAM_B2) * (g * g)
    m_hat = m / (1.0 - ADAM_B1 ** ADAM_STEP)
    v_hat = v / (1.0 - ADAM_B2 ** ADAM_STEP)
    delta = -ADAM_LR * (m_hat / (jnp.sqrt(v_hat) + ADAM_EPS) + ADAM_WD * w)
    return delta, m, v


def _adamw_big(parts, mine, me, w, m, v, name):
    rows, cols = w.shape
    tr, tc = (256, cols) if rows % 256 == 0 else (rows, 256)

    def body(me_ref, p_ref, own_ref, w_ref, m_ref, v_ref, g_ref, d_ref, mo_ref, vo_ref):
        own = own_ref[...].astype(f32)
        g = None
        for j in range(N_DEV):
            part = jnp.where(me_ref[0] == j, own, p_ref[j].astype(f32))
            g = part if g is None else g + part
        d, mn, vn = _adamw_math(w_ref[...], g, m_ref[...], v_ref[...])
        g_ref[...] = g
        d_ref[...] = d
        mo_ref[...] = mn
        vo_ref[...] = vn

    tile = pl.BlockSpec((tr, tc), lambda i, j, me_ref: (i, j))
    shp = jax.ShapeDtypeStruct((rows, cols), f32)
    grid_spec = pltpu.PrefetchScalarGridSpec(
        num_scalar_prefetch=1, grid=(rows // tr, cols // tc),
        in_specs=[pl.BlockSpec((N_DEV, tr, tc), lambda i, j, me_ref: (0, i, j)),
                  pl.BlockSpec((None, tr, tc), lambda i, j, me_ref: (me_ref[0], i, j)), tile, tile, tile],
        out_specs=[tile, tile, tile, tile])
    return pl.pallas_call(
        body, name=name, grid_spec=grid_spec, out_shape=(shp, shp, shp, shp),
        compiler_params=_params("arbitrary", "arbitrary"),
    )(me, parts, mine, w, m, v)


PACK_ROWS = 56
_PACK_AT = {
    "mix_norm_g": (0, 0, 1024), "ssd_norm_g": (1, 0, 1024), "conf_ln_g": (2, 0, 1024), "conf_ln_b": (3, 0, 1024),
    "conf_dw_b": (4, 0, 1024), "mlp_norm_g": (5, 0, 1024), "final_norm_g": (7, 0, 1024), "ple_norm_g": (8, 0, 1024),
    "b_ple_gate": (9, 0, 1024), "ple_gate_norm_g": (10, 0, 1024), "dt_bias": (13, 0, 16), "A_log": (13, 128, 16),
    "D_skip": (13, 256, 16),
}
PACK_LOSS_ROW = 6
PACK_CONV_B_ROW = 11
PACK_CONV_W_ROW = 14
PACK_CONF_W_ROW = 24


def _pack_small(acc):
    def body(inp, opb, mlp, tail, cb, dtb, alog, dskip, cw, fw, o_ref):
        o_ref[...] = jnp.zeros_like(o_ref)
        rows = {"mix_norm_g": inp[0:1, :], "mlp_norm_g": mlp[0:1, :],
                "ssd_norm_g": opb[OPB_SSD_G:OPB_SSD_G + 1, :], "conf_ln_g": opb[OPB_LN_G:OPB_LN_G + 1, :],
                "conf_ln_b": opb[OPB_LN_B:OPB_LN_B + 1, :], "conf_dw_b": opb[OPB_CONV_B:OPB_CONV_B + 1, :],
                "final_norm_g": tail[TAIL_FINAL_G:TAIL_FINAL_G + 1, :], "ple_norm_g": tail[TAIL_PLE_G:TAIL_PLE_G + 1, :],
                "b_ple_gate": tail[TAIL_GATE_B:TAIL_GATE_B + 1, :],
                "ple_gate_norm_g": tail[TAIL_GATE_NORM_G:TAIL_GATE_NORM_G + 1, :],
                "dt_bias": dtb[...], "A_log": alog[...], "D_skip": dskip[...]}
        for name, val in rows.items():
            r, lo, width = _PACK_AT[name]
            o_ref[r:r + 1, lo:lo + width] = val
        o_ref[PACK_LOSS_ROW:PACK_LOSS_ROW + 1, :] = tail[TAIL_LOSS:TAIL_LOSS + 1, :]
        o_ref[PACK_CONV_B_ROW:PACK_CONV_B_ROW + 1, :] = cb[:, 0:1024]
        o_ref[PACK_CONV_B_ROW + 1:PACK_CONV_B_ROW + 2, 0:512] = cb[:, 1024:XBC_WIDTH]
        for k in range(SSD_CONV):
            o_ref[PACK_CONV_W_ROW + k:PACK_CONV_W_ROW + k + 1, :] = cw[k:k + 1, 0:1024]
            o_ref[PACK_CONV_W_ROW + SSD_CONV + k:PACK_CONV_W_ROW + SSD_CONV + k + 1, 0:512] = cw[k:k + 1, 1024:XBC_WIDTH]
        o_ref[PACK_CONF_W_ROW:PACK_CONF_W_ROW + 32, :] = fw[...]

    return pl.pallas_call(body, name="pack_small", out_shape=jax.ShapeDtypeStruct((PACK_ROWS, 1024), f32))(
        acc["in_proj"], acc["out_proj"], acc["mlp"], acc["tail"], acc["ssd_conv_b"], acc["dt_bias"], acc["A_log"],
        acc["D_skip"], acc["ssd_conv_w"], acc["conf_dw_w"])


def _small_update(all_small, w, m, v):
    names = _REPLICATED

    def body(all_ref, *refs):
        ins, outs = refs[:3 * len(names)], refs[3 * len(names):]
        s = all_ref[0]
        for j in range(1, N_DEV):
            s = s + all_ref[j]
        outs[0][...] = s[PACK_LOSS_ROW:PACK_LOSS_ROW + 1, 0:1]
        outs[1][...] = jnp.concatenate([s[PACK_CONV_W_ROW:PACK_CONV_W_ROW + SSD_CONV, :],
                                        s[PACK_CONV_W_ROW + SSD_CONV:PACK_CONV_W_ROW + 2 * SSD_CONV, 0:512]], axis=1)
        outs[2][...] = s[PACK_CONF_W_ROW:PACK_CONF_W_ROW + CONF_KERNEL, :]
        for i, name in enumerate(names):
            if name == "ssd_conv_b":
                g = jnp.concatenate([s[PACK_CONV_B_ROW:PACK_CONV_B_ROW + 1, :],
                                     s[PACK_CONV_B_ROW + 1:PACK_CONV_B_ROW + 2, 0:512]], axis=1)
            else:
                r, lo, width = _PACK_AT[name]
                g = s[r:r + 1, lo:lo + width]
            d, mn, vn = _adamw_math(ins[3 * i][...], g, ins[3 * i + 1][...], ins[3 * i + 2][...])
            for o_ref, val in zip(outs[3 + 4 * i:7 + 4 * i], (g, d, mn, vn)):
                o_ref[...] = val

    shapes = [jax.ShapeDtypeStruct((1, 1), f32), jax.ShapeDtypeStruct((SSD_CONV, XBC_WIDTH), f32),
              jax.ShapeDtypeStruct((CONF_KERNEL, D_MODEL), f32)]
    operands = []
    for name in names:
        operands += [w[name], m[name], v[name]]
        shapes += [jax.ShapeDtypeStruct(w[name].shape, f32)] * 4
    res = pl.pallas_call(body, name="small_update", out_shape=shapes)(all_small, *operands)
    per_name = {name: tuple(res[3 + 4 * i:7 + 4 * i]) for i, name in enumerate(names)}
    return res[0], per_name, res[1], res[2]


def _adamw_filters(g, w, m, v):
    n = len(g)

    def body(*refs):
        ins, outs = refs[:4 * n], refs[4 * n:]
        for i in range(n):
            d, mn, vn = _adamw_math(ins[4 * i + 1][...], ins[4 * i][...], ins[4 * i + 2][...], ins[4 * i + 3][...])
            for o_ref, val in zip(outs[3 * i:3 * i + 3], (d, mn, vn)):
                o_ref[...] = val

    operands, shapes = [], []
    for i in range(n):
        operands += [g[i], w[i], m[i], v[i]]
        shapes += [jax.ShapeDtypeStruct(w[i].shape, f32)] * 3
    res = pl.pallas_call(body, name="adamw_filters", out_shape=shapes)(*operands)
    return [tuple(res[3 * i:3 * i + 3]) for i in range(n)]


_REPLICATED = ("mix_norm_g", "ssd_conv_b", "dt_bias", "A_log", "D_skip", "ssd_norm_g", "conf_dw_b", "conf_ln_g",
               "conf_ln_b", "mlp_norm_g", "ple_gate_norm_g", "b_ple_gate", "ple_norm_g", "final_norm_g")
_CONV_WEIGHTS = ("ssd_conv_w", "conf_dw_w")
_WEIGHT_ORDER = ("mix_norm_g", "w_in", "ssd_conv_w", "ssd_conv_b", "dt_bias", "A_log", "D_skip", "ssd_norm_g", "conf_dw_w",
                 "conf_dw_b", "conf_ln_g", "conf_ln_b", "w_out", "mlp_norm_g", "w_up", "w_down", "ple_gate_norm_g",
                 "w_ple_gate", "b_ple_gate", "w_ple", "ple_norm_g", "final_norm_g")


def _blocks_of_columns(a):
    r, c8 = a.shape
    return jnp.transpose(a.reshape(r, N_DEV, c8 // N_DEV), (1, 0, 2))


def _columns_of_blocks(a):
    _, r, c = a.shape
    return jnp.transpose(a, (1, 0, 2)).reshape(r, N_DEV * c)


_LATER = ("w_out", "w_up", "w_down", "w_ple_gate", "w_ple")
_WHOLE = {
    "w_out": lambda a: a.reshape(2048, D_MODEL),
    "w_up": lambda a: a,
    "w_down": lambda a: a.reshape(D_FF, D_MODEL),
    "w_ple_gate": lambda a: a.reshape(D_MODEL, D_MODEL),
    "w_ple": _columns_of_blocks,
}
_BY_BLOCK = {
    "w_in": lambda g: jnp.concatenate(g, axis=0).reshape(N_DEV, IN_WIDTH // N_DEV, D_MODEL),
    "w_out": lambda g: g.reshape(N_DEV, 256, D_MODEL),
    "w_up": lambda g: g,
    "w_down": lambda g: g.reshape(N_DEV, 512, D_MODEL),
    "w_ple_gate": lambda g: g.astype(bf16).reshape(N_DEV, 128, D_MODEL),
    "w_ple": lambda g: _blocks_of_columns(g.astype(bf16)),
}


class _StepComm:
    def __init__(self, me, ready, gathers, token):
        self.me, self.ready, self.gathers, self.token = me, ready, gathers, token
        self.sent = []

    def start_token(self):
        return self.token

    def weight(self, name, after=()):
        if name not in self.ready:
            (land,), _ = _exchange_wait([self.gathers[name]], False, "gather_wait_" + name, list(after))
            self.ready[name] = _WHOLE[name](land)
        return self.ready[name]

    def send_grads(self, grads):
        names = list(grads)
        blocks = [_BY_BLOCK[n](grads[n]) for n in names]
        states, self.token = _exchange_start(blocks, True, "scatter_start_" + names[0], self.token)
        self.sent.append((names, states))
        return self.token


def kernel(x, p, mix_norm_g, w_in, ssd_conv_w, ssd_conv_b, dt_bias, A_log, D_skip, ssd_norm_g, conf_dw_w, conf_dw_b, conf_ln_g, conf_ln_b, w_out, mlp_norm_g, w_up, w_down, ple_gate_norm_g, w_ple_gate, b_ple_gate, w_ple, ple_norm_g, final_norm_g, loss_target, m_mix_norm_g, m_w_in, m_ssd_conv_w, m_ssd_conv_b, m_dt_bias, m_A_log, m_D_skip, m_ssd_norm_g, m_conf_dw_w, m_conf_dw_b, m_conf_ln_g, m_conf_ln_b, m_w_out, m_mlp_norm_g, m_w_up, m_w_down, m_ple_gate_norm_g, m_w_ple_gate, m_b_ple_gate, m_w_ple, m_ple_norm_g, m_final_norm_g, v_mix_norm_g, v_w_in, v_ssd_conv_w, v_ssd_conv_b, v_dt_bias, v_A_log, v_D_skip, v_ssd_norm_g, v_conf_dw_w, v_conf_dw_b, v_conf_ln_g, v_conf_ln_b, v_w_out, v_mlp_norm_g, v_w_up, v_w_down, v_ple_gate_norm_g, v_w_ple_gate, v_b_ple_gate, v_w_ple, v_ple_norm_g, v_final_norm_g):
    wts = dict(mix_norm_g=mix_norm_g, w_in=w_in, ssd_conv_w=ssd_conv_w, ssd_conv_b=ssd_conv_b, dt_bias=dt_bias, A_log=A_log,
               D_skip=D_skip, ssd_norm_g=ssd_norm_g, conf_dw_w=conf_dw_w, conf_dw_b=conf_dw_b, conf_ln_g=conf_ln_g,
               conf_ln_b=conf_ln_b, w_out=w_out, mlp_norm_g=mlp_norm_g, w_up=w_up, w_down=w_down,
               ple_gate_norm_g=ple_gate_norm_g, w_ple_gate=w_ple_gate, b_ple_gate=b_ple_gate, w_ple=w_ple,
               ple_norm_g=ple_norm_g, final_norm_g=final_norm_g)
    mom1 = dict(mix_norm_g=m_mix_norm_g, w_in=m_w_in, ssd_conv_w=m_ssd_conv_w, ssd_conv_b=m_ssd_conv_b, dt_bias=m_dt_bias,
                A_log=m_A_log, D_skip=m_D_skip, ssd_norm_g=m_ssd_norm_g, conf_dw_w=m_conf_dw_w, conf_dw_b=m_conf_dw_b,
                conf_ln_g=m_conf_ln_g, conf_ln_b=m_conf_ln_b, w_out=m_w_out, mlp_norm_g=m_mlp_norm_g, w_up=m_w_up,
                w_down=m_w_down, ple_gate_norm_g=m_ple_gate_norm_g, w_ple_gate=m_w_ple_gate, b_ple_gate=m_b_ple_gate,
                w_ple=m_w_ple, ple_norm_g=m_ple_norm_g, final_norm_g=m_final_norm_g)
    mom2 = dict(mix_norm_g=v_mix_norm_g, w_in=v_w_in, ssd_conv_w=v_ssd_conv_w, ssd_conv_b=v_ssd_conv_b, dt_bias=v_dt_bias,
                A_log=v_A_log, D_skip=v_D_skip, ssd_norm_g=v_ssd_norm_g, conf_dw_w=v_conf_dw_w, conf_dw_b=v_conf_dw_b,
                conf_ln_g=v_conf_ln_g, conf_ln_b=v_conf_ln_b, w_out=v_w_out, mlp_norm_g=v_mlp_norm_g, w_up=v_w_up,
                w_down=v_w_down, ple_gate_norm_g=v_ple_gate_norm_g, w_ple_gate=v_w_ple_gate, b_ple_gate=v_b_ple_gate,
                w_ple=v_w_ple, ple_norm_g=v_ple_norm_g, final_norm_g=v_final_norm_g)
    x_pos, y_pos, c_pos = _mesh_pos()
    me = 4 * x_pos + 2 * y_pos + c_pos

    first = _all_gather([wts["w_in"][0].T.astype(bf16), wts["ssd_conv_w"][0], wts["conf_dw_w"][0]], "gather_first")
    ready = {
        "w_in": first[0].reshape(IN_WIDTH, D_MODEL),
        "ssd_conv_w": jnp.pad(_columns_of_blocks(first[1]), ((0, 8 - SSD_CONV), (0, 0))),
        "conf_dw_w": jnp.pad(_columns_of_blocks(first[2]), ((0, 32 - CONF_KERNEL), (0, 0))),
    }
    shards = [wts[n][0].astype(bf16) for n in _LATER]
    states, token = _exchange_start(shards, False, "gather_start", first[1])
    comm = _StepComm(me, ready, dict(zip(_LATER, states)), token)
    small = {n: wts[n].reshape(1, -1) for n in _REPLICATED}

    grad_x, acc = _local_step(x[0], p[0, 0], loss_target[0], small, comm)

    packed = _pack_small(acc)
    (small_state,), _ = _exchange_start([packed], False, "small_start", comm.token)
    grads, delta, new_m, new_v = {}, {}, {}, {}
    me_index = me.astype(jnp.int32).reshape(1)

    def adamw_big(names, sent, after):
        lands, mine = _exchange_wait(sent, True, "scatter_wait_" + names[0], after)
        for n, land, own in zip(names, lands, mine):
            view = (lambda a: a[0].T) if n == "w_in" else (lambda a: a[0])
            back = (lambda a: a.T[None]) if n == "w_in" else (lambda a: a[None])
            out = _adamw_big(land, own, me_index, view(wts[n]), view(mom1[n]), view(mom2[n]), "adamw_" + n)
            grads[n], delta[n], new_m[n], new_v[n] = [back(a) for a in out]

    for names, sent in comm.sent[:-1]:
        adamw_big(names, sent, [grad_x])
    adamw_big(*comm.sent[-1], [delta[n] for n in _LATER])
    (all_small,), _ = _exchange_wait([small_state], False, "small_wait", [delta["w_in"]])
    as_row = lambda d: {n: d[n].reshape(1, -1) for n in _REPLICATED}
    loss, per_name, conv_w_sum, conf_w_sum = _small_update(all_small, as_row(wts), as_row(mom1), as_row(mom2))
    for n in _REPLICATED:
        grads[n], delta[n], new_m[n], new_v[n] = [a.reshape(wts[n].shape) for a in per_name[n]]
    filt_g = [lax.dynamic_slice_in_dim(conv_w_sum, me * 192, 192, axis=1),
              lax.dynamic_slice_in_dim(conf_w_sum, me * 128, 128, axis=1)]
    filt = _adamw_filters(filt_g, [wts[n][0] for n in _CONV_WEIGHTS], [mom1[n][0] for n in _CONV_WEIGHTS],
                          [mom2[n][0] for n in _CONV_WEIGHTS])
    for n, g, (d, mn, vn) in zip(_CONV_WEIGHTS, filt_g, filt):
        grads[n], delta[n], new_m[n], new_v[n] = g[None], d[None], mn[None], vn[None]

    return (loss.reshape(()), grad_x[None], *[grads[n] for n in _WEIGHT_ORDER], *[delta[n] for n in _WEIGHT_ORDER],
            *[new_m[n] for n in _WEIGHT_ORDER], *[new_v[n] for n in _WEIGHT_ORDER])
```

```python
import functools

import jax
import jax.numpy as jnp
from jax import lax
from jax.experimental import pallas as pl
from jax.experimental.pallas import tpu as pltpu

f32 = jnp.float32
bf16 = jnp.bfloat16

EPS = 1e-6
D_MODEL = 1024
SSD_HEADS = 16
SSD_HEAD_DIM = 64
SSD_GROUPS = 2
SSD_STATE = 128
SSD_CONV = 4
CHUNK = 128
XBC_WIDTH = 1536
CONF_KERNEL = 31
D_FF = 4096
PLE_DIM = 256
IN_WIDTH = 4624
N_DEV = 8
SEG_Z = (0, 1024)
SEG_XBC = (1024, 2560)
SEG_DT = (2560, 2576)
SEG_CV = (2576, 3600)
SEG_CG = (3600, 4624)
IN_SEGS = (SEG_Z, SEG_XBC, SEG_DT, SEG_CV, SEG_CG)
BWD_SEGS = (SEG_Z, SEG_XBC, SEG_DT, (SEG_CV[0], SEG_CG[1]))

ADAM_LR = 0.001
ADAM_B1 = 0.9
ADAM_B2 = 0.999
ADAM_EPS = 1e-08
ADAM_WD = 0.01
ADAM_STEP = 10

VMEM_LIMIT_BYTES = 56 * 1024 * 1024
TOKEN_TILE = 512
SMALL_ROWS = 16

HBM_SPEC = pl.BlockSpec(memory_space=pltpu.HBM)
MESH_ID = pl.DeviceIdType.MESH


def _params(*semantics):
    return pltpu.CompilerParams(dimension_semantics=semantics, vmem_limit_bytes=VMEM_LIMIT_BYTES)


def _nn(a, b):
    return lax.dot_general(a, b, (((1,), (0,)), ((), ())), preferred_element_type=f32)


def _nt(a, b):
    return lax.dot_general(a, b, (((1,), (1,)), ((), ())), preferred_element_type=f32)


def _tn(a, b):
    return lax.dot_general(a, b, (((0,), (0,)), ((), ())), preferred_element_type=f32)


@jax.custom_vjp
def bnn(a, b):
    return _nn(a.astype(bf16), b.astype(bf16))


def _bnn_fwd(a, b):
    ab, bb = a.astype(bf16), b.astype(bf16)
    return _nn(ab, bb), (ab, bb)


def _bnn_bwd(res, g):
    ab, bb = res
    gb = g.astype(bf16)
    return _nt(gb, bb), _tn(ab, gb)


bnn.defvjp(_bnn_fwd, _bnn_bwd)


@jax.custom_vjp
def bnt(a, b):
    return _nt(a.astype(bf16), b.astype(bf16))


def _bnt_fwd(a, b):
    ab, bb = a.astype(bf16), b.astype(bf16)
    return _nt(ab, bb), (ab, bb)


def _bnt_bwd(res, g):
    ab, bb = res
    gb = g.astype(bf16)
    return _nn(gb, bb), _tn(gb, ab)


bnt.defvjp(_bnt_fwd, _bnt_bwd)


@jax.custom_vjp
def btn(a, b):
    return _tn(a.astype(bf16), b.astype(bf16))


def _btn_fwd(a, b):
    ab, bb = a.astype(bf16), b.astype(bf16)
    return _tn(ab, bb), (ab, bb)


def _btn_bwd(res, g):
    ab, bb = res
    gb = g.astype(bf16)
    return _nt(bb, gb), _nn(ab, gb)


btn.defvjp(_btn_fwd, _btn_bwd)

SCAN_CHUNKS = 4
CUMSUM_PASSES = 3
EXPAND_PASSES = 2


def _bf16_pieces(a, passes):
    pieces, rest = [], a
    for _ in range(passes):
        piece = rest.astype(bf16)
        pieces.append(piece)
        rest = rest - piece.astype(f32)
    return pieces


def _make_dot01(form, passes):
    fwd, bwd = {"right": (lambda b, p: _nn(p, b), lambda b, g: _nt(g, b)),
                "left": (lambda b, p: _nn(b, p), lambda b, g: _tn(b, g)),
                "tn": (lambda b, p: _tn(p, b), lambda b, g: _nt(b, g))}[form]

    def run(op, b, v):
        out = None
        for piece in _bf16_pieces(v, passes):
            term = op(b, piece)
            out = term if out is None else out + term
        return out

    @jax.custom_vjp
    def product(b, v):
        return run(fwd, b, v)

    product.defvjp(lambda b, v: (run(fwd, b, v), b), lambda b, g: (jnp.zeros_like(b), run(bwd, b, g)))
    return product


_DOT01 = {(form, passes): _make_dot01(form, passes)
          for form, passes in (("left", CUMSUM_PASSES), ("tn", CUMSUM_PASSES), ("right", EXPAND_PASSES))}


def _dot01(b01, a, form, passes):
    return _DOT01[form, passes](b01.astype(bf16), a)


def _rms(x, g):
    return x * lax.rsqrt(jnp.mean(x * x, axis=-1, keepdims=True) + EPS) * g


def _gated_norm(y, z, g):
    v = y * jax.nn.silu(z)
    half = v.shape[-1] // SSD_GROUPS
    parts = []
    for k in range(SSD_GROUPS):
        vk = v[:, k * half:(k + 1) * half]
        parts.append(vk * lax.rsqrt(jnp.mean(vk * vk, axis=-1, keepdims=True) + EPS))
    return jnp.concatenate(parts, axis=-1) * g


def _ln_silu(v, g, b):
    mu = jnp.mean(v, axis=-1, keepdims=True)
    xc = v - mu
    y = xc * lax.rsqrt(jnp.mean(xc * xc, axis=-1, keepdims=True) + EPS) * g + b
    return jax.nn.silu(y)


def _acc_init(i, *refs):
    @pl.when(i == 0)
    def _():
        for r in refs:
            r[...] = jnp.zeros_like(r)


def _after_token(body, token):
    if token is None:
        return body, [], []

    def body_after(tok_ref, *refs):
        del tok_ref
        body(*refs)

    return body_after, [token], [pl.BlockSpec(memory_space=pl.ANY)]


def _row_spec(tm, n):
    return pl.BlockSpec((tm, n), lambda i: (i, 0))


def _const_spec(shape):
    nd = len(shape)
    return pl.BlockSpec(shape, lambda i: (0,) * nd)


def _prev_halo_spec(tm, halo, n):
    return pl.BlockSpec((halo, n), lambda i: (jnp.maximum(i * (tm // halo) - 1, 0), 0))


def _next_halo_spec(tm, halo, n, t):
    return pl.BlockSpec((halo, n), lambda i: (jnp.minimum((i + 1) * (tm // halo), t // halo - 1), 0))


def _in_proj(x, g1, w_in_p, token=None):
    t = x.shape[0]
    tm = min(TOKEN_TILE, t)

    def body(x_ref, g_ref, w_ref, u_ref, z_ref, xbc_ref, dt_ref, cv_ref, cg_ref):
        u = _rms(x_ref[...], g_ref[...]).astype(bf16)
        u_ref[...] = u
        for (lo, hi), o_ref in zip(IN_SEGS, (z_ref, xbc_ref, dt_ref, cv_ref, cg_ref)):
            o_ref[...] = _nt(u, w_ref[lo:hi, :])

    widths = [hi - lo for lo, hi in IN_SEGS]
    outs = [jax.ShapeDtypeStruct((t, D_MODEL), bf16)] + [jax.ShapeDtypeStruct((t, n), f32) for n in widths]
    body, tok, tok_spec = _after_token(body, token)
    return pl.pallas_call(
        body, name="in_proj", grid=(t // tm,),
        in_specs=tok_spec + [_row_spec(tm, D_MODEL), _const_spec((1, D_MODEL)), _const_spec((IN_WIDTH, D_MODEL))],
        out_specs=[_row_spec(tm, D_MODEL)] + [_row_spec(tm, n) for n in widths],
        out_shape=outs, compiler_params=_params("arbitrary"),
    )(*tok, x, g1, w_in_p)


ROW_BLOCK = 16
LANE_CHUNK = 512


def _blocks(rows, cols):
    return [(slice(r, r + ROW_BLOCK), slice(c, c + LANE_CHUNK))
            for c in range(0, cols, LANE_CHUNK) for r in range(0, rows, ROW_BLOCK)]


def _conv4_block(ext_ref, w_ref, b_ref, rs, ls):
    acc = b_ref[:, ls] + w_ref[0:1, ls] * ext_ref[rs.start + 8 - 3:rs.stop + 8 - 3, ls]
    for k in range(1, SSD_CONV):
        acc = acc + w_ref[k:k + 1, ls] * ext_ref[rs.start + 8 - 3 + k:rs.stop + 8 - 3 + k, ls]
    return acc


def _ssd_pre(xbc_raw, dt_raw, cw, cb, dt_bias):
    t = xbc_raw.shape[0]
    tm = min(TOKEN_TILE, t)

    def body(cur_ref, halo_ref, dt_ref, w_ref, b_ref, dtb_ref, act_ref, dto_ref, ext_ref):
        i = pl.program_id(0)
        ext_ref[0:8, :] = jnp.where(i == 0, 0.0, halo_ref[...])
        ext_ref[8:, :] = cur_ref[...]
        for rs, ls in _blocks(tm, XBC_WIDTH):
            act_ref[rs, ls] = jax.nn.silu(_conv4_block(ext_ref, w_ref, b_ref, rs, ls))
        dto_ref[...] = jax.nn.softplus(dt_ref[...] + dtb_ref[...])

    return pl.pallas_call(
        body, name="ssd_pre", grid=(t // tm,),
        in_specs=[_row_spec(tm, XBC_WIDTH), _prev_halo_spec(tm, 8, XBC_WIDTH), _row_spec(tm, SSD_HEADS),
                  _const_spec((8, XBC_WIDTH)), _const_spec((1, XBC_WIDTH)), _const_spec((1, SSD_HEADS))],
        out_specs=[_row_spec(tm, XBC_WIDTH), _row_spec(tm, SSD_HEADS)],
        out_shape=(jax.ShapeDtypeStruct((t, XBC_WIDTH), f32), jax.ShapeDtypeStruct((t, SSD_HEADS), f32)),
        scratch_shapes=[pltpu.VMEM((tm + 8, XBC_WIDTH), f32)],
        compiler_params=_params("arbitrary"),
    )(xbc_raw, xbc_raw, dt_raw, cw, cb, dt_bias)


def _ssd_consts():
    r = lax.broadcasted_iota(jnp.int32, (CHUNK, CHUNK), 0)
    c = lax.broadcasted_iota(jnp.int32, (CHUNK, CHUNK), 1)
    causal = r >= c
    tril = causal.astype(f32)
    triu = (r <= c).astype(f32)
    hh = lax.broadcasted_iota(jnp.int32, (SSD_HEADS, D_MODEL), 0)
    jj = lax.broadcasted_iota(jnp.int32, (SSD_HEADS, D_MODEL), 1)
    expand = (lax.shift_right_logical(jj, 6) == hh).astype(f32)
    lane = lax.broadcasted_iota(jnp.int32, (1, 2 * SSD_HEAD_DIM), 1)
    m0 = (lane < SSD_HEAD_DIM).astype(f32)
    return causal, tril, triu, expand, m0, 1.0 - m0


def _ssd_chunk(xs, bm, cm, dt, s_prev, a_log, d_skip, consts):
    causal, tril, triu, expand, m0, m1 = consts
    a = dt * (-jnp.exp(a_log))
    cs = _dot01(tril, a, "left", CUMSUM_PASSES)
    cs_t = _dot01(triu, a, "tn", CUMSUM_PASSES)
    cs_last = cs[CHUNK - 1:CHUNK, :]
    per_head = jnp.concatenate([dt, jnp.exp(cs_last - cs), jnp.exp(cs), jnp.broadcast_to(jnp.exp(cs_last), (8, SSD_HEADS)),
                                jnp.broadcast_to(d_skip, (8, SSD_HEADS))], axis=0)
    per_channel = _dot01(expand, per_head, "right", EXPAND_PASSES)
    dt_e, dec_end, dec_start = (per_channel[i * CHUNK:(i + 1) * CHUNK] for i in range(3))
    chunk_dec = per_channel[3 * CHUNK:3 * CHUNK + 1]
    d_e = per_channel[3 * CHUNK + 8:3 * CHUNK + 9]
    xc = xs * dt_e
    x_dec = xc * dec_end
    gw = D_MODEL // SSD_GROUPS
    ys, states = [], []
    for g in range(SSD_GROUPS):
        bg = bm[:, g * SSD_STATE:(g + 1) * SSD_STATE]
        cg = cm[:, g * SSD_STATE:(g + 1) * SSD_STATE]
        sp = s_prev[:, g * gw:(g + 1) * gw]
        states.append(sp * chunk_dec[:, g * gw:(g + 1) * gw] + btn(bg, x_dec[:, g * gw:(g + 1) * gw]))
        y_off = bnn(cg, sp) * dec_start[:, g * gw:(g + 1) * gw]
        scores = bnt(cg, bg)
        pieces = []
        for pr in range(gw // (2 * SSD_HEAD_DIM)):
            lo = g * gw + pr * 2 * SSD_HEAD_DIM
            xp = xc[:, lo:lo + 2 * SSD_HEAD_DIM]
            mats = []
            for h in (lo // SSD_HEAD_DIM, lo // SSD_HEAD_DIM + 1):
                seg = cs[:, h:h + 1] - cs_t[h:h + 1, :]
                mats.append(scores * jnp.exp(jnp.where(causal, seg, -jnp.inf)))
            pieces.append(bnn(jnp.concatenate(mats, axis=1), jnp.concatenate([xp * m0, xp * m1], axis=0)))
        ys.append(jnp.concatenate(pieces, axis=-1) + y_off)
    y = jnp.concatenate(ys, axis=-1) + xs * d_e
    return y, jnp.concatenate(states, axis=-1)


def _ssd_scan(xbc_act, dt, a_log, d_skip):
    t = xbc_act.shape[0]
    nc = t // CHUNK
    rows = SCAN_CHUNKS * CHUNK

    def body(xbc_ref, dt_ref, al_ref, dk_ref, y_ref, sp_ref, state_ref):
        c = pl.program_id(0)

        @pl.when(c == 0)
        def _():
            state_ref[...] = jnp.zeros_like(state_ref)

        state = state_ref[...]
        consts = _ssd_consts()
        for j in range(SCAN_CHUNKS):
            rs = slice(j * CHUNK, (j + 1) * CHUNK)
            sp_ref[j] = state
            y_ref[rs, :], state = _ssd_chunk(xbc_ref[rs, 0:1024], xbc_ref[rs, 1024:1280], xbc_ref[rs, 1280:1536],
                                             dt_ref[rs, :], state, al_ref[...], dk_ref[...], consts)
        state_ref[...] = state

    return pl.pallas_call(
        body, name="ssd_scan", grid=(nc // SCAN_CHUNKS,),
        in_specs=[_row_spec(rows, XBC_WIDTH), _row_spec(rows, SSD_HEADS), _const_spec((1, SSD_HEADS)),
                  _const_spec((1, SSD_HEADS))],
        out_specs=[_row_spec(rows, D_MODEL), pl.BlockSpec((SCAN_CHUNKS, SSD_STATE, D_MODEL), lambda c: (c, 0, 0))],
        out_shape=(jax.ShapeDtypeStruct((t, D_MODEL), f32), jax.ShapeDtypeStruct((nc, SSD_STATE, D_MODEL), f32)),
        scratch_shapes=[pltpu.VMEM((SSD_STATE, D_MODEL), f32)],
        compiler_params=_params("arbitrary"),
    )(xbc_act, dt, a_log, d_skip)


CONV_HALO = 32
CONV_BLOCK = 128
LANE_TILE = 128
SUBLANES = 8


def _fill_glu_ext(ext_ref, cv_ref, cg_ref, hcv_ref, hcg_ref, i):
    ext_ref[0:CONV_HALO, :] = jnp.where(i == 0, 0.0, hcv_ref[...] * jax.nn.sigmoid(hcg_ref[...]))
    ext_ref[CONV_HALO:, :] = cv_ref[...] * jax.nn.sigmoid(cg_ref[...])


def _shifted_copies(src_ref, dst_ref, rows):
    for s in range(1, SUBLANES):
        dst_ref[s, 0:rows, :] = src_ref[pl.ds(s, rows), :]


def _tap_source(src_ref, shifted_ref, offset):
    s = offset % SUBLANES
    return (src_ref if s == 0 else shifted_ref.at[s]), offset - s


def _conv_rows(src_ref, shifted_ref, taps, offsets, lanes, r0, init):
    accs = [init] * (CONV_BLOCK // SUBLANES)
    for k, off in enumerate(offsets):
        ref, base = _tap_source(src_ref, shifted_ref, off)
        for j in range(len(accs)):
            accs[j] = accs[j] + taps[k] * ref[pl.ds(r0 + base + SUBLANES * j, SUBLANES), lanes]
    return accs


def _conf_conv(cv, cg, w, b):
    t = cv.shape[0]
    tm = min(TOKEN_TILE, t)
    ext_rows = tm + CONV_HALO
    offsets = [CONV_HALO - (CONF_KERNEL - 1) + k for k in range(CONF_KERNEL)]

    def body(cv_ref, cg_ref, hcv_ref, hcg_ref, w_ref, b_ref, o_ref, ext_ref, shifted_ref):
        _fill_glu_ext(ext_ref, cv_ref, cg_ref, hcv_ref, hcg_ref, pl.program_id(0))
        _shifted_copies(ext_ref, shifted_ref, ext_rows - SUBLANES)
        for lb in range(1024 // LANE_TILE):
            lanes = slice(lb * LANE_TILE, (lb + 1) * LANE_TILE)
            taps = [jnp.broadcast_to(w_ref[k:k + 1, lanes], (SUBLANES, LANE_TILE)) for k in range(CONF_KERNEL)]
            bias = jnp.broadcast_to(b_ref[:, lanes], (SUBLANES, LANE_TILE))

            def rows(rb, carry, lanes=lanes, taps=taps, bias=bias):
                r0 = pl.multiple_of(rb * CONV_BLOCK, CONV_BLOCK)
                accs = _conv_rows(ext_ref, shifted_ref, taps, offsets, lanes, r0, bias)
                for j, a in enumerate(accs):
                    o_ref[pl.ds(r0 + SUBLANES * j, SUBLANES), lanes] = a
                return carry

            lax.fori_loop(0, tm // CONV_BLOCK, rows, 0)

    return pl.pallas_call(
        body, name="conf_conv", grid=(t // tm,),
        in_specs=[_row_spec(tm, 1024), _row_spec(tm, 1024), _prev_halo_spec(tm, CONV_HALO, 1024),
                  _prev_halo_spec(tm, CONV_HALO, 1024), _const_spec((32, 1024)), _const_spec((1, 1024))],
        out_specs=_row_spec(tm, 1024), out_shape=jax.ShapeDtypeStruct((t, 1024), f32),
        scratch_shapes=[pltpu.VMEM((ext_rows, 1024), f32), pltpu.VMEM((SUBLANES, ext_rows, 1024), f32)],
        compiler_params=_params("arbitrary"),
    )(cv, cg, cv, cg, w, b)


def _out_proj(y, z, v2, x, g_ssd, ln_g, ln_b, w_out, g2):
    t = x.shape[0]
    tm = min(TOKEN_TILE, t)

    def body(y_ref, z_ref, v_ref, x_ref, gs_ref, lg_ref, lb_ref, w_ref, g2_ref, mixed_ref, h1_ref, u2_ref):
        ys = _gated_norm(y_ref[...], z_ref[...], gs_ref[...]).astype(bf16)
        yc = _ln_silu(v_ref[...], lg_ref[...], lb_ref[...]).astype(bf16)
        mixed_ref[:, 0:1024] = ys
        mixed_ref[:, 1024:2048] = yc
        h1 = x_ref[...] + _nn(ys, w_ref[0:1024, :]) + _nn(yc, w_ref[1024:2048, :])
        h1_ref[...] = h1
        u2_ref[...] = _rms(h1, g2_ref[...]).astype(bf16)

    vec = _const_spec((1, 1024))
    row = _row_spec(tm, 1024)
    return pl.pallas_call(
        body, name="out_proj", grid=(t // tm,),
        in_specs=[row, row, row, row, vec, vec, vec, _const_spec((2048, 1024)), vec],
        out_specs=[_row_spec(tm, 2048), row, row],
        out_shape=(jax.ShapeDtypeStruct((t, 2048), bf16),
                   jax.ShapeDtypeStruct((t, 1024), f32), jax.ShapeDtypeStruct((t, 1024), bf16)),
        compiler_params=_params("arbitrary"),
    )(y, z, v2, x, g_ssd, ln_g, ln_b, w_out, g2)


def _mlp_up(u2, w_up):
    t = u2.shape[0]
    tm = min(TOKEN_TILE, t)
    blk = D_FF // N_DEV

    def body(u_ref, w_ref, pre_ref, hs_ref):
        u = u_ref[...]
        for k in range(N_DEV):
            pre = _nn(u, w_ref[k])
            pre_ref[:, k * blk:(k + 1) * blk] = pre.astype(bf16)
            r = jnp.maximum(pre, 0.0)
            hs_ref[:, k * blk:(k + 1) * blk] = (r * r).astype(bf16)

    return pl.pallas_call(
        body, name="mlp_up", grid=(t // tm,),
        in_specs=[_row_spec(tm, D_MODEL), _const_spec((N_DEV, D_MODEL, D_FF // N_DEV))],
        out_specs=[_row_spec(tm, D_FF), _row_spec(tm, D_FF)],
        out_shape=(jax.ShapeDtypeStruct((t, D_FF), bf16), jax.ShapeDtypeStruct((t, D_FF), bf16)),
        compiler_params=_params("arbitrary"),
    )(u2, w_up)


def _mlp_down(h1, hs, w_down):
    t = h1.shape[0]
    tm = min(TOKEN_TILE, t)

    def body(h_ref, hs_ref, w_ref, o_ref):
        o_ref[...] = h_ref[...] + _nn(hs_ref[...], w_ref[...])

    return pl.pallas_call(
        body, name="mlp_down", grid=(t // tm,),
        in_specs=[_row_spec(tm, D_MODEL), _row_spec(tm, D_FF), _const_spec((D_FF, D_MODEL))],
        out_specs=_row_spec(tm, D_MODEL), out_shape=jax.ShapeDtypeStruct((t, D_MODEL), f32),
        compiler_params=_params("arbitrary"),
    )(h1, hs, w_down)


TAIL_LOSS, TAIL_FINAL_G, TAIL_PLE_G, TAIL_GATE_B, TAIL_GATE_NORM_G = 0, 1, 2, 3, 4


def _tail(h2, p, target, g3, w_gate, b_gate, w_ple, ple_g, fin_g):
    t = h2.shape[0]
    tm = min(TOKEN_TILE, t)

    def body(h_ref, p_ref, t_ref, g3_ref, wg_ref, bg_ref, wp_ref, pg_ref, fg_ref,
             dh_ref, dhb_ref, dwg_ref, dwp_ref, acc_ref):
        i = pl.program_id(0)
        _acc_init(i, dwg_ref, dwp_ref, acc_ref)
        h2v = h_ref[...]
        tgt = t_ref[...]
        u3, vjp_u3 = jax.vjp(_rms, h2v, g3_ref[...])
        u3b = u3.astype(bf16)
        pb = p_ref[...].astype(bf16)
        gate_pre = _nn(u3b, wg_ref[...]) + bg_ref[...]
        emb_pre = _nn(pb, wp_ref[...])

        def tail_fn(hv, gp, ep, pg, fg):
            h3 = hv + jax.nn.sigmoid(gp) * _rms(ep, pg)
            err = _rms(h3, fg) - tgt
            return 0.5 * jnp.mean(err * err, axis=-1, keepdims=True)

        loss_tok, vjp_tail = jax.vjp(tail_fn, h2v, gate_pre, emb_pre, pg_ref[...], fg_ref[...])
        dh_a, dgp, dep, dpg, dfg = vjp_tail(jnp.ones_like(loss_tok))
        dgpb = dgp.astype(bf16)
        dh_b, dg3 = vjp_u3(_nt(dgpb, wg_ref[...]))
        dh = dh_a + dh_b
        dh_ref[...] = dh
        dhb_ref[...] = dh.astype(bf16)
        dwg_ref[...] += _tn(u3b, dgpb)
        dwp_ref[...] += _tn(pb, dep.astype(bf16))
        acc_ref[TAIL_LOSS:TAIL_LOSS + 1, :] += jnp.broadcast_to(jnp.sum(loss_tok, axis=0, keepdims=True), (1, 1024))
        acc_ref[TAIL_FINAL_G:TAIL_FINAL_G + 1, :] += dfg
        acc_ref[TAIL_PLE_G:TAIL_PLE_G + 1, :] += dpg
        acc_ref[TAIL_GATE_B:TAIL_GATE_B + 1, :] += jnp.sum(dgp, axis=0, keepdims=True)
        acc_ref[TAIL_GATE_NORM_G:TAIL_GATE_NORM_G + 1, :] += dg3

    vec = _const_spec((1, 1024))
    row = _row_spec(tm, 1024)
    return pl.pallas_call(
        body, name="tail", grid=(t // tm,),
        in_specs=[row, _row_spec(tm, PLE_DIM), row, vec, _const_spec((1024, 1024)), vec, _const_spec((PLE_DIM, 1024)), vec, vec],
        out_specs=[row, row, _const_spec((1024, 1024)), _const_spec((PLE_DIM, 1024)), _const_spec((8, 1024))],
        out_shape=(jax.ShapeDtypeStruct((t, 1024), f32), jax.ShapeDtypeStruct((t, 1024), bf16),
                   jax.ShapeDtypeStruct((1024, 1024), f32), jax.ShapeDtypeStruct((PLE_DIM, 1024), f32),
                   jax.ShapeDtypeStruct((8, 1024), f32)),
        compiler_params=_params("arbitrary"),
    )(h2, p, target, g3, w_gate, b_gate, w_ple, ple_g, fin_g)


def _mlp_bwd(dh2, dh2b, pre, h1, g2, w_down, w_up, token=None):
    t = dh2.shape[0]
    tm = min(TOKEN_TILE, t)
    blk = D_FF // N_DEV

    def body(dh_ref, dhb_ref, pre_ref, h1_ref, g_ref, wd_hbm, wu_hbm, dpre_ref, dh1_ref, dh1b_ref, acc_ref,
             wd_ref, wu_ref, wsem):
        i = pl.program_id(0)
        _acc_init(i, acc_ref)

        def weight_copies(k):
            rows = pl.ds(k * blk, blk)
            return (pltpu.make_async_copy(wd_hbm.at[rows, :], wd_ref.at[rows, :], wsem.at[0, k]),
                    pltpu.make_async_copy(wu_hbm.at[k], wu_ref.at[k], wsem.at[1, k]))

        @pl.when(i == 0)
        def _():
            for k in range(D_FF // blk):
                for cp in weight_copies(k):
                    cp.start()

        dhb = dhb_ref[...]
        du2 = jnp.zeros((tm, D_MODEL), f32)
        for k in range(D_FF // blk):
            @pl.when(i == 0)
            def _(k=k):
                for cp in weight_copies(k):
                    cp.wait()

            dhs = _nt(dhb, wd_ref[k * blk:(k + 1) * blk, :])
            dpre = (dhs * (2.0 * jnp.maximum(pre_ref[:, k * blk:(k + 1) * blk].astype(f32), 0.0))).astype(bf16)
            dpre_ref[:, k * blk:(k + 1) * blk] = dpre
            du2 = du2 + _nt(dpre, wu_ref[k])
        _, vjp_u2 = jax.vjp(_rms, h1_ref[...], g_ref[...])
        d, dg = vjp_u2(du2)
        dh1 = dh_ref[...] + d
        dh1_ref[...] = dh1
        dh1b_ref[...] = dh1.astype(bf16)
        acc_ref[0:1, :] += dg

    row = _row_spec(tm, 1024)
    body, tok, tok_spec = _after_token(body, token)
    return pl.pallas_call(
        body, name="mlp_bwd", grid=(t // tm,),
        in_specs=tok_spec + [row, row, _row_spec(tm, D_FF), row, _const_spec((1, 1024)), HBM_SPEC, HBM_SPEC],
        out_specs=[_row_spec(tm, D_FF), row, row, _const_spec((8, 1024))],
        out_shape=(jax.ShapeDtypeStruct((t, D_FF), bf16), jax.ShapeDtypeStruct((t, 1024), f32),
                   jax.ShapeDtypeStruct((t, 1024), bf16), jax.ShapeDtypeStruct((8, 1024), f32)),
        scratch_shapes=[pltpu.VMEM((D_FF, D_MODEL), bf16), pltpu.VMEM((N_DEV, D_MODEL, D_FF // N_DEV), bf16),
                        pltpu.SemaphoreType.DMA((2, N_DEV))],
        compiler_params=_params("arbitrary"),
    )(*tok, dh2, dh2b, pre, h1, g2, w_down, w_up)


OPB_SSD_G, OPB_LN_G, OPB_LN_B, OPB_CONV_B = 0, 1, 2, 3


def _out_proj_bwd(dh1b, y, z, v2, g_ssd, ln_g, ln_b, w_out, token=None):
    t = y.shape[0]
    tm = min(TOKEN_TILE, t)

    def body(dh_ref, y_ref, z_ref, v_ref, gs_ref, lg_ref, lb_ref, w_ref, dy_ref, dz_ref, dv_ref, acc_ref):
        i = pl.program_id(0)
        _acc_init(i, acc_ref)
        dhb = dh_ref[...]
        dmixed = _nt(dhb, w_ref[...])
        dys, dyc = dmixed[:, 0:1024], dmixed[:, 1024:2048]
        _, vjp_g = jax.vjp(_gated_norm, y_ref[...], z_ref[...], gs_ref[...])
        dy, dz, dgs = vjp_g(dys)
        _, vjp_l = jax.vjp(_ln_silu, v_ref[...], lg_ref[...], lb_ref[...])
        dv, dlg, dlb = vjp_l(dyc)
        dy_ref[...] = dy
        dz_ref[...] = dz.astype(bf16)
        dv_ref[...] = dv
        acc_ref[OPB_SSD_G:OPB_SSD_G + 1, :] += dgs
        acc_ref[OPB_LN_G:OPB_LN_G + 1, :] += dlg
        acc_ref[OPB_LN_B:OPB_LN_B + 1, :] += dlb
        acc_ref[OPB_CONV_B:OPB_CONV_B + 1, :] += jnp.sum(dv, axis=0, keepdims=True)

    vec = _const_spec((1, 1024))
    row = _row_spec(tm, 1024)
    body, tok, tok_spec = _after_token(body, token)
    return pl.pallas_call(
        body, name="out_proj_bwd", grid=(t // tm,),
        in_specs=tok_spec + [row, row, row, row, vec, vec, vec, _const_spec((2048, 1024))],
        out_specs=[row, row, row, _const_spec((8, 1024))],
        out_shape=(jax.ShapeDtypeStruct((t, 1024), f32), jax.ShapeDtypeStruct((t, 1024), bf16),
                   jax.ShapeDtypeStruct((t, 1024), f32), jax.ShapeDtypeStruct((8, 1024), f32)),
        compiler_params=_params("arbitrary"),
    )(*tok, dh1b, y, z, v2, g_ssd, ln_g, ln_b, w_out)


def _conf_conv_bwd(dv2, cv, cg, w):
    t = cv.shape[0]
    tm = min(TOKEN_TILE, t)
    ext_rows = tm + CONV_HALO
    offsets = [CONF_KERNEL - 1 - k for k in range(CONF_KERNEL)]

    def body(dv_ref, dvn_ref, cv_ref, cg_ref, w_ref, dcvg_ref, dw_ref, glu_ref, dext_ref, sg_ref, shifted_ref):
        i = pl.program_id(0)
        _acc_init(i, dw_ref)
        sg = jax.nn.sigmoid(cg_ref[...])
        sg_ref[...] = sg
        glu_ref[...] = cv_ref[...] * sg
        dext_ref[0:tm, :] = dv_ref[...]
        dext_ref[tm:, :] = jnp.where(i == pl.num_programs(0) - 1, 0.0, dvn_ref[...])
        _shifted_copies(dext_ref, shifted_ref, ext_rows - SUBLANES)

        for lb in range(1024 // LANE_TILE):
            lanes = slice(lb * LANE_TILE, (lb + 1) * LANE_TILE)

            def rows_w(rb, accs, lanes=lanes):
                r0 = pl.multiple_of(rb * CONV_BLOCK, CONV_BLOCK)
                accs = list(accs)
                for j in range(CONV_BLOCK // SUBLANES):
                    x = glu_ref[pl.ds(r0 + SUBLANES * j, SUBLANES), lanes]
                    for k, off in enumerate(offsets):
                        ref, base = _tap_source(dext_ref, shifted_ref, off)
                        accs[k] = accs[k] + x * ref[pl.ds(r0 + base + SUBLANES * j, SUBLANES), lanes]
                return tuple(accs)

            zero = jnp.zeros((SUBLANES, LANE_TILE), f32)
            accs = lax.fori_loop(0, tm // CONV_BLOCK, rows_w, (zero,) * CONF_KERNEL)
            for k in range(CONF_KERNEL):
                dw_ref[k:k + 1, lanes] += jnp.sum(accs[k], axis=0, keepdims=True)

        for lb in range(1024 // LANE_TILE):
            lanes = slice(lb * LANE_TILE, (lb + 1) * LANE_TILE)
            taps = [jnp.broadcast_to(w_ref[k:k + 1, lanes], (SUBLANES, LANE_TILE)) for k in range(CONF_KERNEL)]

            def rows_x(rb, carry, lanes=lanes, taps=taps):
                r0 = pl.multiple_of(rb * CONV_BLOCK, CONV_BLOCK)
                zero = jnp.zeros((SUBLANES, LANE_TILE), f32)
                dglu = jnp.concatenate(_conv_rows(dext_ref, shifted_ref, taps, offsets, lanes, r0, zero), axis=0)
                sg = sg_ref[pl.ds(r0, CONV_BLOCK), lanes]
                cvv = cv_ref[pl.ds(r0, CONV_BLOCK), lanes]
                gate_lanes = slice(lanes.start + 1024, lanes.stop + 1024)
                dcvg_ref[pl.ds(r0, CONV_BLOCK), lanes] = (dglu * sg).astype(bf16)
                dcvg_ref[pl.ds(r0, CONV_BLOCK), gate_lanes] = (dglu * cvv * sg * (1.0 - sg)).astype(bf16)
                return carry

            lax.fori_loop(0, tm // CONV_BLOCK, rows_x, 0)

    row = _row_spec(tm, 1024)
    return pl.pallas_call(
        body, name="conf_conv_bwd", grid=(t // tm,),
        in_specs=[row, _next_halo_spec(tm, CONV_HALO, 1024, t), row, row, _const_spec((32, 1024))],
        out_specs=[_row_spec(tm, 2048), _const_spec((32, 1024))],
        out_shape=(jax.ShapeDtypeStruct((t, 2048), bf16), jax.ShapeDtypeStruct((32, 1024), f32)),
        scratch_shapes=[pltpu.VMEM((tm, 1024), f32), pltpu.VMEM((ext_rows, 1024), f32), pltpu.VMEM((tm, 1024), f32),
                        pltpu.VMEM((SUBLANES, ext_rows, 1024), f32)],
        compiler_params=_params("arbitrary"),
    )(dv2, dv2, cv, cg, w)


def _ssd_scan_bwd(xbc_act, dt, s_prev, dy, a_log, d_skip):
    t = xbc_act.shape[0]
    steps = t // CHUNK // SCAN_CHUNKS
    rows = SCAN_CHUNKS * CHUNK

    def body(xbc_ref, dt_ref, sp_ref, dy_ref, al_ref, dk_ref, dxbc_ref, ddt_ref, dal_ref, ddk_ref, ds_ref):
        i = pl.program_id(0)
        _acc_init(i, ds_ref, dal_ref, ddk_ref)
        consts = _ssd_consts()
        ds = ds_ref[...]
        dal_sum, ddk_sum = dal_ref[...], ddk_ref[...]
        for j in reversed(range(SCAN_CHUNKS)):
            rs = slice(j * CHUNK, (j + 1) * CHUNK)
            _, vjp_c = jax.vjp(
                functools.partial(_ssd_chunk, consts=consts),
                xbc_ref[rs, 0:1024], xbc_ref[rs, 1024:1280], xbc_ref[rs, 1280:1536], dt_ref[rs, :], sp_ref[j],
                al_ref[...], dk_ref[...])
            dxs, dbm, dcm, ddt, ds, dal, ddk = vjp_c((dy_ref[rs, :], ds))
            dxbc_ref[rs, 0:1024] = dxs
            dxbc_ref[rs, 1024:1280] = dbm
            dxbc_ref[rs, 1280:1536] = dcm
            ddt_ref[rs, :] = ddt
            dal_sum, ddk_sum = dal_sum + dal, ddk_sum + ddk
        ds_ref[...] = ds
        dal_ref[...] = dal_sum
        ddk_ref[...] = ddk_sum

    rev = lambda i: (steps - 1 - i, 0)
    return pl.pallas_call(
        body, name="ssd_scan_bwd", grid=(steps,),
        in_specs=[pl.BlockSpec((rows, XBC_WIDTH), rev), pl.BlockSpec((rows, SSD_HEADS), rev),
                  pl.BlockSpec((SCAN_CHUNKS, SSD_STATE, D_MODEL), lambda i: (steps - 1 - i, 0, 0)),
                  pl.BlockSpec((rows, D_MODEL), rev), _const_spec((1, SSD_HEADS)), _const_spec((1, SSD_HEADS))],
        out_specs=[pl.BlockSpec((rows, XBC_WIDTH), rev), pl.BlockSpec((rows, SSD_HEADS), rev),
                   _const_spec((1, SSD_HEADS)), _const_spec((1, SSD_HEADS))],
        out_shape=(jax.ShapeDtypeStruct((t, XBC_WIDTH), f32), jax.ShapeDtypeStruct((t, SSD_HEADS), f32),
                   jax.ShapeDtypeStruct((1, SSD_HEADS), f32), jax.ShapeDtypeStruct((1, SSD_HEADS), f32)),
        scratch_shapes=[pltpu.VMEM((SSD_STATE, D_MODEL), f32)],
        compiler_params=_params("arbitrary"),
    )(xbc_act, dt, s_prev, dy, a_log, d_skip)


def _ssd_pre_bwd(xbc_raw, dxbc_act, ddt, dt_raw, cw, cb, dt_bias):
    t = xbc_raw.shape[0]
    tm = min(TOKEN_TILE, t)

    def body(cur_ref, halo_ref, dact_ref, ddt_ref, dtr_ref, w_ref, b_ref, dtb_ref,
             dco_ref, ddtr_ref, dw_ref, db_ref, ddtb_ref, ext_ref):
        i = pl.program_id(0)
        _acc_init(i, dw_ref, db_ref, ddtb_ref)
        ext_ref[0:8, :] = jnp.where(i == 0, 0.0, halo_ref[...])
        ext_ref[8:, :] = cur_ref[...]
        fold = lambda a: a[0:8] + a[8:ROW_BLOCK]
        for c in range(0, XBC_WIDTH, LANE_CHUNK):
            ls = slice(c, c + LANE_CHUNK)
            part_b = jnp.zeros((8, LANE_CHUNK), f32)
            part_w = [jnp.zeros((8, LANE_CHUNK), f32)] * SSD_CONV
            for r in range(0, tm, ROW_BLOCK):
                rs = slice(r, r + ROW_BLOCK)
                co = _conv4_block(ext_ref, w_ref, b_ref, rs, ls)
                sg = jax.nn.sigmoid(co)
                dco = dact_ref[rs, ls] * sg * (1.0 + co * (1.0 - sg))
                dco_ref[rs, ls] = dco
                part_b = part_b + fold(dco)
                part_w = [pw + fold(dco * ext_ref[r + 8 - 3 + k:r + ROW_BLOCK + 8 - 3 + k, ls])
                          for k, pw in enumerate(part_w)]
            db_ref[:, ls] += jnp.sum(part_b, axis=0, keepdims=True)
            for k in range(SSD_CONV):
                dw_ref[k:k + 1, ls] += jnp.sum(part_w[k], axis=0, keepdims=True)
        ddtr = ddt_ref[...] * jax.nn.sigmoid(dtr_ref[...] + dtb_ref[...])
        ddtb_ref[...] += jnp.sum(ddtr, axis=0, keepdims=True)
        ddtr_ref[...] = ddtr.astype(bf16)

    return pl.pallas_call(
        body, name="ssd_pre_bwd", grid=(t // tm,),
        in_specs=[_row_spec(tm, XBC_WIDTH), _prev_halo_spec(tm, 8, XBC_WIDTH), _row_spec(tm, XBC_WIDTH),
                  _row_spec(tm, SSD_HEADS), _row_spec(tm, SSD_HEADS), _const_spec((8, XBC_WIDTH)),
                  _const_spec((1, XBC_WIDTH)), _const_spec((1, SSD_HEADS))],
        out_specs=[_row_spec(tm, XBC_WIDTH), _row_spec(tm, SSD_HEADS), _const_spec((8, XBC_WIDTH)),
                   _const_spec((1, XBC_WIDTH)), _const_spec((1, SSD_HEADS))],
        out_shape=(jax.ShapeDtypeStruct((t, XBC_WIDTH), f32), jax.ShapeDtypeStruct((t, SSD_HEADS), bf16),
                   jax.ShapeDtypeStruct((8, XBC_WIDTH), f32), jax.ShapeDtypeStruct((1, XBC_WIDTH), f32),
                   jax.ShapeDtypeStruct((1, SSD_HEADS), f32)),
        scratch_shapes=[pltpu.VMEM((tm + 8, XBC_WIDTH), f32)],
        compiler_params=_params("arbitrary"),
    )(xbc_raw, xbc_raw, dxbc_act, ddt, dt_raw, cw, cb, dt_bias)


def _conv4_bwd_data(dco, cw):
    t = dco.shape[0]
    tm = min(TOKEN_TILE, t)

    def body(cur_ref, nxt_ref, w_ref, o_ref, ext_ref):
        i = pl.program_id(0)
        ext_ref[0:tm, :] = cur_ref[...]
        ext_ref[tm:, :] = jnp.where(i == pl.num_programs(0) - 1, 0.0, nxt_ref[...])
        for rs, ls in _blocks(tm, XBC_WIDTH):
            acc = w_ref[0:1, ls] * ext_ref[rs.start + SSD_CONV - 1:rs.stop + SSD_CONV - 1, ls]
            for k in range(1, SSD_CONV):
                acc = acc + w_ref[k:k + 1, ls] * ext_ref[rs.start + SSD_CONV - 1 - k:rs.stop + SSD_CONV - 1 - k, ls]
            o_ref[rs, ls] = acc.astype(bf16)

    return pl.pallas_call(
        body, name="conv4_bwd_data", grid=(t // tm,),
        in_specs=[_row_spec(tm, XBC_WIDTH), _next_halo_spec(tm, 8, XBC_WIDTH, t), _const_spec((8, XBC_WIDTH))],
        out_specs=_row_spec(tm, XBC_WIDTH), out_shape=jax.ShapeDtypeStruct((t, XBC_WIDTH), bf16),
        scratch_shapes=[pltpu.VMEM((tm + 8, XBC_WIDTH), f32)],
        compiler_params=_params("arbitrary"),
    )(dco, dco, cw)


def _in_proj_bwd(dproj, x, dh1, g1, w_in_p, token=None):
    t = x.shape[0]
    tm = min(TOKEN_TILE, t)

    def body(dz_ref, dxbc_ref, ddt_ref, dcvg_ref, x_ref, dh_ref, g_ref, w_ref, dx_ref, acc_ref):
        i = pl.program_id(0)
        _acc_init(i, acc_ref)
        du = jnp.zeros((tm, D_MODEL), f32)
        for (lo, hi), r in zip(BWD_SEGS, (dz_ref, dxbc_ref, ddt_ref, dcvg_ref)):
            du = du + _nn(r[...], w_ref[lo:hi, :])
        _, vjp_u = jax.vjp(_rms, x_ref[...], g_ref[...])
        d, dg = vjp_u(du)
        dx_ref[...] = dh_ref[...] + d
        acc_ref[0:1, :] += dg

    row = _row_spec(tm, 1024)
    body, tok, tok_spec = _after_token(body, token)
    return pl.pallas_call(
        body, name="in_proj_bwd", grid=(t // tm,),
        in_specs=tok_spec + [_row_spec(tm, hi - lo) for lo, hi in BWD_SEGS] + [row, row, _const_spec((1, 1024)),
                                                                               _const_spec((IN_WIDTH, D_MODEL))],
        out_specs=[row, _const_spec((8, 1024))],
        out_shape=(jax.ShapeDtypeStruct((t, 1024), f32), jax.ShapeDtypeStruct((8, 1024), f32)),
        compiler_params=_params("arbitrary"),
    )(*tok, *dproj, x, dh1, g1, w_in_p)


def _mm_tn(a, b, tk, tn, name, column_blocks=False):
    t, kk = a.shape
    n = b.shape[1]
    tk, tn = min(tk, kk), min(tn, n)

    def body(a_ref, b_ref, o_ref):
        o_ref[...] = _tn(a_ref[...], b_ref[...]).astype(bf16)

    if column_blocks:
        assert tk == kk
        out_spec = pl.BlockSpec((None, tk, tn), lambda i, j: (j, 0, 0))
        out_shape = jax.ShapeDtypeStruct((n // tn, kk, tn), bf16)
    else:
        out_spec = pl.BlockSpec((tk, tn), lambda i, j: (i, j))
        out_shape = jax.ShapeDtypeStruct((kk, n), bf16)
    return pl.pallas_call(
        body, name=name, grid=(kk // tk, n // tn),
        in_specs=[pl.BlockSpec((t, tk), lambda i, j: (0, i)), pl.BlockSpec((t, tn), lambda i, j: (0, j))],
        out_specs=out_spec, out_shape=out_shape,
        compiler_params=_params("arbitrary", "arbitrary"),
    )(a, b)


class _LocalWeights:
    def __init__(self, w):
        self.w = w
        self.sent = {}

    def start_token(self):
        return None

    def weight(self, name, after=()):
        del after
        return self.w[name]

    def send_grads(self, grads):
        self.sent.update(grads)
        return None


def _local_step(x, p, target, s, comm):
    conv_w, conf_w = comm.weight("ssd_conv_w"), comm.weight("conf_dw_w")
    w_in = comm.weight("w_in")
    u, z, xbc_raw, dt_raw, cv, cg = _in_proj(x, s["mix_norm_g"], w_in, token=comm.start_token())
    xbc_act, dt = _ssd_pre(xbc_raw, dt_raw, conv_w, s["ssd_conv_b"], s["dt_bias"])
    y, s_prev = _ssd_scan(xbc_act, dt, s["A_log"], s["D_skip"])
    v2 = _conf_conv(cv, cg, conf_w, s["conf_dw_b"])
    w_out = comm.weight("w_out", after=(y, v2))
    mixed, h1, u2 = _out_proj(y, z, v2, x, s["ssd_norm_g"], s["conf_ln_g"], s["conf_ln_b"], w_out, s["mlp_norm_g"])
    w_up = comm.weight("w_up", after=(u2,))
    pre, hs = _mlp_up(u2, w_up)
    w_down = comm.weight("w_down", after=(hs,))
    h2 = _mlp_down(h1, hs, w_down)
    w_gate, w_ple = comm.weight("w_ple_gate", after=(h2,)), comm.weight("w_ple", after=(h2,))
    dh2, dh2b, dwg, dwp, tail_acc = _tail(h2, p, target, s["ple_gate_norm_g"], w_gate, s["b_ple_gate"], w_ple,
                                          s["ple_norm_g"], s["final_norm_g"])
    token = comm.send_grads({"w_ple_gate": dwg, "w_ple": dwp})
    dpre, dh1, dh1b, mlp_acc = _mlp_bwd(dh2, dh2b, pre, h1, s["mlp_norm_g"], w_down, w_up, token=token)
    token = comm.send_grads({
        "w_down": _mm_tn(hs, dh2b, 512, 1024, "dw_down"),
        "w_up": _mm_tn(u2, dpre, 1024, D_FF // N_DEV, "dw_up", column_blocks=True),
        "w_out": _mm_tn(mixed, dh1b, 512, 1024, "dw_out"),
    })
    dy, dz, dv2, opb_acc = _out_proj_bwd(dh1b, y, z, v2, s["ssd_norm_g"], s["conf_ln_g"], s["conf_ln_b"], w_out,
                                         token=token)
    dcvg, dconf_w = _conf_conv_bwd(dv2, cv, cg, conf_w)
    dxbc_act, ddt, d_alog, d_dskip = _ssd_scan_bwd(xbc_act, dt, s_prev, dy, s["A_log"], s["D_skip"])
    dco, ddt_raw, dconv_w, dconv_b, d_dtb = _ssd_pre_bwd(xbc_raw, dxbc_act, ddt, dt_raw, conv_w, s["ssd_conv_b"],
                                                        s["dt_bias"])
    dxbc_raw = _conv4_bwd_data(dco, conv_w)
    dproj = (dz, dxbc_raw, ddt_raw, dcvg)
    token = comm.send_grads({"w_in": [_mm_tn(d, u, 512, 1024, "dw_in_" + n)
                                      for n, d in zip(("z", "xbc", "dt", "cvg"), dproj)]})
    grad_x, inp_acc = _in_proj_bwd(dproj, x, dh1, s["mix_norm_g"], w_in, token=token)
    acc = {"in_proj": inp_acc, "out_proj": opb_acc, "mlp": mlp_acc, "tail": tail_acc, "ssd_conv_b": dconv_b,
           "dt_bias": d_dtb, "A_log": d_alog, "D_skip": d_dskip, "ssd_conv_w": dconv_w, "conf_dw_w": dconf_w}
    return grad_x, acc


def _mesh_pos():
    return lax.axis_index("x"), lax.axis_index("y"), lax.axis_index("c")


def _other_chips(x, y):
    return [(1 - x, y), (x, 1 - y), (1 - x, 1 - y)]


def _all_gather(arrs, name):
    n = len(arrs)

    def body(*refs):
        ins, outs = refs[:n], refs[n:2 * n]
        send_sems, recv_sems, local_sems = refs[2 * n:]
        x, y, c = _mesh_pos()
        me = 4 * x + 2 * y + c
        sibling = (x, y, 1 - c)
        chips = _other_chips(x, y)

        def copy(a, k, block, to, src=None):
            dst = outs[a].at[block]
            return pltpu.make_async_remote_copy(
                src_ref=dst if src is None else src, dst_ref=dst, send_sem=send_sems.at[a, k],
                recv_sem=recv_sems.at[a, k], device_id=to, device_id_type=MESH_ID)

        mine = [pltpu.make_async_copy(ins[a], outs[a].at[me], local_sems.at[a]) for a in range(n)]
        for cp in mine:
            cp.start()
        first = []
        for a in range(n):
            first.append(copy(a, 0, me, sibling, src=ins[a]))
            first += [copy(a, 1 + j, me, (px, py, c), src=ins[a]) for j, (px, py) in enumerate(chips)]
        for cp in first:
            cp.start()
        passed = []
        for j, (px, py) in enumerate(chips):
            for a in range(n):
                blk = 4 * px + 2 * py + c
                copy(a, 1 + j, blk, (x, y, c)).wait_recv()
                cp = copy(a, 4 + j, blk, sibling)
                cp.start()
                passed.append(cp)
        for a in range(n):
            copy(a, 0, 4 * x + 2 * y + (1 - c), (x, y, c)).wait_recv()
        for j, (px, py) in enumerate(chips):
            for a in range(n):
                copy(a, 4 + j, 4 * px + 2 * py + (1 - c), (x, y, c)).wait_recv()
        for cp in first + passed:
            cp.wait_send()
        for cp in mine:
            cp.wait()

    return pl.pallas_call(
        body, name=name,
        in_specs=[HBM_SPEC] * n, out_specs=[HBM_SPEC] * n,
        out_shape=[jax.ShapeDtypeStruct((N_DEV,) + a.shape, a.dtype) for a in arrs],
        scratch_shapes=[pltpu.SemaphoreType.DMA((n, 7)), pltpu.SemaphoreType.DMA((n, 7)), pltpu.SemaphoreType.DMA((n,))],
    )(*arrs)


_PEER_FLIPS = ((0, 0, 1), (1, 0, 0), (0, 1, 0), (1, 1, 0), (1, 0, 1), (0, 1, 1), (1, 1, 1))
SEM_SPEC = pl.BlockSpec(memory_space=pltpu.SEMAPHORE)
ANY_SPEC = pl.BlockSpec(memory_space=pl.ANY)


def _flip(v, d):
    return 1 - v if d else v


def _peers(x, y, c):
    out = []
    for dx, dy, dc in _PEER_FLIPS:
        px, py, pc = _flip(x, dx), _flip(y, dy), _flip(c, dc)
        out.append(((px, py, pc), 4 * px + 2 * py + pc))
    return out


def _exchange_copy(src_ref, land_ref, send_sems, recv_sems, k, peer, peer_block, my_block, by_block, outgoing):
    src = src_ref.at[peer_block] if by_block else src_ref
    dst = land_ref.at[my_block if outgoing else peer_block]
    return pltpu.make_async_remote_copy(src_ref=src, dst_ref=dst, send_sem=send_sems.at[k], recv_sem=recv_sems.at[k],
                                        device_id=peer, device_id_type=MESH_ID)


def _exchange_start(srcs, by_block, name, after):
    n = len(srcs)
    x_pos, y_pos, c_pos = _mesh_pos()
    me_at = 4 * x_pos + 2 * y_pos + c_pos
    if by_block:
        lands = [lax.empty(a.shape, a.dtype) for a in srcs]
    else:
        lands = [lax.dynamic_update_slice(lax.empty((N_DEV,) + a.shape, a.dtype), a[None], (me_at,) + (0,) * a.ndim)
                 for a in srcs]

    def body(*refs):
        src_refs, land_refs = refs[1:1 + n], refs[1 + n:1 + 2 * n]
        outs = refs[1 + 2 * n:]
        send, recv, token = outs[:n], outs[n:2 * n], outs[4 * n]
        x, y, c = _mesh_pos()
        me = 4 * x + 2 * y + c
        for a in range(n):
            for k, (peer, blk) in enumerate(_peers(x, y, c)):
                _exchange_copy(src_refs[a], land_refs[a], send[a], recv[a], k, peer, blk, me, by_block, True).start()
        token[...] = jnp.zeros_like(token)

    sems = [pltpu.SemaphoreType.DMA((N_DEV - 1,))] * (2 * n)
    thru = [pltpu.HBM(a.shape, a.dtype) for a in list(srcs) + list(lands)]
    res = pl.pallas_call(
        body, name=name,
        in_specs=[ANY_SPEC] + [HBM_SPEC] * (2 * n),
        out_specs=[SEM_SPEC] * (2 * n) + [HBM_SPEC] * (2 * n) + [pl.BlockSpec(memory_space=pltpu.VMEM)],
        out_shape=sems + thru + [jax.ShapeDtypeStruct((8, 128), f32)],
        input_output_aliases={1 + i: 2 * n + i for i in range(2 * n)},
        compiler_params=pltpu.CompilerParams(has_side_effects=pltpu.SideEffectType.DATAFLOW_SIDE_EFFECTING),
    )(after, *[pltpu.with_memory_space_constraint(a, pltpu.HBM) for a in list(srcs) + list(lands)])
    states = [(res[2 * n + a], res[3 * n + a], res[a], res[n + a]) for a in range(n)]
    return states, res[4 * n]


def _exchange_wait(states, by_block, name, after):
    n, na = len(states), len(after)

    def body(*refs):
        src_refs, land_refs = refs[:n], refs[n:2 * n]
        send, recv = refs[2 * n:3 * n], refs[3 * n:4 * n]
        x, y, c = _mesh_pos()
        me = 4 * x + 2 * y + c
        for a in range(n):
            for k, (peer, blk) in enumerate(_peers(x, y, c)):
                _exchange_copy(src_refs[a], land_refs[a], send[a], recv[a], k, peer, blk, me, by_block, True).wait_send()
                _exchange_copy(src_refs[a], land_refs[a], send[a], recv[a], k, peer, blk, me, by_block, False).wait_recv()

    srcs, lands = [s[0] for s in states], [s[1] for s in states]
    res = pl.pallas_call(
        body, name=name,
        in_specs=[HBM_SPEC] * (2 * n) + [SEM_SPEC] * (2 * n) + [ANY_SPEC] * na,
        out_specs=[HBM_SPEC] * (2 * n),
        out_shape=[pltpu.HBM(a.shape, a.dtype) for a in srcs + lands],
        input_output_aliases={i: i for i in range(2 * n)},
        compiler_params=pltpu.CompilerParams(has_side_effects=pltpu.SideEffectType.DATAFLOW_SIDE_EFFECTING),
    )(*srcs, *lands, *[s[2] for s in states], *[s[3] for s in states], *after)
    return list(res[n:2 * n]), list(res[:n])


def _adamw_math(w, g, m, v):
    m = ADAM_B1 * m + (1.0 - ADAM_B1) * g
    v = ADAM_B2 * v + (1.0 - ADAM_B2) * (g * g)
    m_hat = m / (1.0 - ADAM_B1 ** ADAM_STEP)
    v_hat = v / (1.0 - ADAM_B2 ** ADAM_STEP)
    delta = -ADAM_LR * (m_hat / (jnp.sqrt(v_hat) + ADAM_EPS) + ADAM_WD * w)
    return delta, m, v


def _adamw_big(parts, mine, me, w, m, v, name):
    rows, cols = w.shape
    tr, tc = (256, cols) if rows % 256 == 0 else (rows, 256)

    def body(me_ref, p_ref, own_ref, w_ref, m_ref, v_ref, g_ref, d_ref, mo_ref, vo_ref):
        own = own_ref[...].astype(f32)
        g = None
        for j in range(N_DEV):
            part = jnp.where(me_ref[0] == j, own, p_ref[j].astype(f32))
            g = part if g is None else g + part
        d, mn, vn = _adamw_math(w_ref[...], g, m_ref[...], v_ref[...])
        g_ref[...] = g
        d_ref[...] = d
        mo_ref[...] = mn
        vo_ref[...] = vn

    tile = pl.BlockSpec((tr, tc), lambda i, j, me_ref: (i, j))
    shp = jax.ShapeDtypeStruct((rows, cols), f32)
    grid_spec = pltpu.PrefetchScalarGridSpec(
        num_scalar_prefetch=1, grid=(rows // tr, cols // tc),
        in_specs=[pl.BlockSpec((N_DEV, tr, tc), lambda i, j, me_ref: (0, i, j)),
                  pl.BlockSpec((None, tr, tc), lambda i, j, me_ref: (me_ref[0], i, j)), tile, tile, tile],
        out_specs=[tile, tile, tile, tile])
    return pl.pallas_call(
        body, name=name, grid_spec=grid_spec, out_shape=(shp, shp, shp, shp),
        compiler_params=_params("arbitrary", "arbitrary"),
    )(me, parts, mine, w, m, v)


PACK_ROWS = 56
_PACK_AT = {
    "mix_norm_g": (0, 0, 1024), "ssd_norm_g": (1, 0, 1024), "conf_ln_g": (2, 0, 1024), "conf_ln_b": (3, 0, 1024),
    "conf_dw_b": (4, 0, 1024), "mlp_norm_g": (5, 0, 1024), "final_norm_g": (7, 0, 1024), "ple_norm_g": (8, 0, 1024),
    "b_ple_gate": (9, 0, 1024), "ple_gate_norm_g": (10, 0, 1024), "dt_bias": (13, 0, 16), "A_log": (13, 128, 16),
    "D_skip": (13, 256, 16),
}
PACK_LOSS_ROW = 6
PACK_CONV_B_ROW = 11
PACK_CONV_W_ROW = 14
PACK_CONF_W_ROW = 24


def _pack_small(acc):
    def body(inp, opb, mlp, tail, cb, dtb, alog, dskip, cw, fw, o_ref):
        o_ref[...] = jnp.zeros_like(o_ref)
        rows = {"mix_norm_g": inp[0:1, :], "mlp_norm_g": mlp[0:1, :],
                "ssd_norm_g": opb[OPB_SSD_G:OPB_SSD_G + 1, :], "conf_ln_g": opb[OPB_LN_G:OPB_LN_G + 1, :],
                "conf_ln_b": opb[OPB_LN_B:OPB_LN_B + 1, :], "conf_dw_b": opb[OPB_CONV_B:OPB_CONV_B + 1, :],
                "final_norm_g": tail[TAIL_FINAL_G:TAIL_FINAL_G + 1, :], "ple_norm_g": tail[TAIL_PLE_G:TAIL_PLE_G + 1, :],
                "b_ple_gate": tail[TAIL_GATE_B:TAIL_GATE_B + 1, :],
                "ple_gate_norm_g": tail[TAIL_GATE_NORM_G:TAIL_GATE_NORM_G + 1, :],
                "dt_bias": dtb[...], "A_log": alog[...], "D_skip": dskip[...]}
        for name, val in rows.items():
            r, lo, width = _PACK_AT[name]
            o_ref[r:r + 1, lo:lo + width] = val
        o_ref[PACK_LOSS_ROW:PACK_LOSS_ROW + 1, :] = tail[TAIL_LOSS:TAIL_LOSS + 1, :]
        o_ref[PACK_CONV_B_ROW:PACK_CONV_B_ROW + 1, :] = cb[:, 0:1024]
        o_ref[PACK_CONV_B_ROW + 1:PACK_CONV_B_ROW + 2, 0:512] = cb[:, 1024:XBC_WIDTH]
        for k in range(SSD_CONV):
            o_ref[PACK_CONV_W_ROW + k:PACK_CONV_W_ROW + k + 1, :] = cw[k:k + 1, 0:1024]
            o_ref[PACK_CONV_W_ROW + SSD_CONV + k:PACK_CONV_W_ROW + SSD_CONV + k + 1, 0:512] = cw[k:k + 1, 1024:XBC_WIDTH]
        o_ref[PACK_CONF_W_ROW:PACK_CONF_W_ROW + 32, :] = fw[...]

    return pl.pallas_call(body, name="pack_small", out_shape=jax.ShapeDtypeStruct((PACK_ROWS, 1024), f32))(
        acc["in_proj"], acc["out_proj"], acc["mlp"], acc["tail"], acc["ssd_conv_b"], acc["dt_bias"], acc["A_log"],
        acc["D_skip"], acc["ssd_conv_w"], acc["conf_dw_w"])


def _small_update(all_small, w, m, v):
    names = _REPLICATED

    def body(all_ref, *refs):
        ins, outs = refs[:3 * len(names)], refs[3 * len(names):]
        s = all_ref[0]
        for j in range(1, N_DEV):
            s = s + all_ref[j]
        outs[0][...] = s[PACK_LOSS_ROW:PACK_LOSS_ROW + 1, 0:1]
        outs[1][...] = jnp.concatenate([s[PACK_CONV_W_ROW:PACK_CONV_W_ROW + SSD_CONV, :],
                                        s[PACK_CONV_W_ROW + SSD_CONV:PACK_CONV_W_ROW + 2 * SSD_CONV, 0:512]], axis=1)
        outs[2][...] = s[PACK_CONF_W_ROW:PACK_CONF_W_ROW + CONF_KERNEL, :]
        for i, name in enumerate(names):
            if name == "ssd_conv_b":
                g = jnp.concatenate([s[PACK_CONV_B_ROW:PACK_CONV_B_ROW + 1, :],
                                     s[PACK_CONV_B_ROW + 1:PACK_CONV_B_ROW + 2, 0:512]], axis=1)
            else:
                r, lo, width = _PACK_AT[name]
                g = s[r:r + 1, lo:lo + width]
            d, mn, vn = _adamw_math(ins[3 * i][...], g, ins[3 * i + 1][...], ins[3 * i + 2][...])
            for o_ref, val in zip(outs[3 + 4 * i:7 + 4 * i], (g, d, mn, vn)):
                o_ref[...] = val

    shapes = [jax.ShapeDtypeStruct((1, 1), f32), jax.ShapeDtypeStruct((SSD_CONV, XBC_WIDTH), f32),
              jax.ShapeDtypeStruct((CONF_KERNEL, D_MODEL), f32)]
    operands = []
    for name in names:
        operands += [w[name], m[name], v[name]]
        shapes += [jax.ShapeDtypeStruct(w[name].shape, f32)] * 4
    res = pl.pallas_call(body, name="small_update", out_shape=shapes)(all_small, *operands)
    per_name = {name: tuple(res[3 + 4 * i:7 + 4 * i]) for i, name in enumerate(names)}
    return res[0], per_name, res[1], res[2]


def _adamw_filters(g, w, m, v):
    n = len(g)

    def body(*refs):
        ins, outs = refs[:4 * n], refs[4 * n:]
        for i in range(n):
            d, mn, vn = _adamw_math(ins[4 * i + 1][...], ins[4 * i][...], ins[4 * i + 2][...], ins[4 * i + 3][...])
            for o_ref, val in zip(outs[3 * i:3 * i + 3], (d, mn, vn)):
                o_ref[...] = val

    operands, shapes = [], []
    for i in range(n):
        operands += [g[i], w[i], m[i], v[i]]
        shapes += [jax.ShapeDtypeStruct(w[i].shape, f32)] * 3
    res = pl.pallas_call(body, name="adamw_filters", out_shape=shapes)(*operands)
    return [tuple(res[3 * i:3 * i + 3]) for i in range(n)]


_REPLICATED = ("mix_norm_g", "ssd_conv_b", "dt_bias", "A_log", "D_skip", "ssd_norm_g", "conf_dw_b", "conf_ln_g",
               "conf_ln_b", "mlp_norm_g", "ple_gate_norm_g", "b_ple_gate", "ple_norm_g", "final_norm_g")
_CONV_WEIGHTS = ("ssd_conv_w", "conf_dw_w")
_WEIGHT_ORDER = ("mix_norm_g", "w_in", "ssd_conv_w", "ssd_conv_b", "dt_bias", "A_log", "D_skip", "ssd_norm_g", "conf_dw_w",
                 "conf_dw_b", "conf_ln_g", "conf_ln_b", "w_out", "mlp_norm_g", "w_up", "w_down", "ple_gate_norm_g",
                 "w_ple_gate", "b_ple_gate", "w_ple", "ple_norm_g", "final_norm_g")


def _blocks_of_columns(a):
    r, c8 = a.shape
    return jnp.transpose(a.reshape(r, N_DEV, c8 // N_DEV), (1, 0, 2))


def _columns_of_blocks(a):
    _, r, c = a.shape
    return jnp.transpose(a, (1, 0, 2)).reshape(r, N_DEV * c)


_LATER = ("w_out", "w_up", "w_down", "w_ple_gate", "w_ple")
_WHOLE = {
    "w_out": lambda a: a.reshape(2048, D_MODEL),
    "w_up": lambda a: a,
    "w_down": lambda a: a.reshape(D_FF, D_MODEL),
    "w_ple_gate": lambda a: a.reshape(D_MODEL, D_MODEL),
    "w_ple": _columns_of_blocks,
}
_BY_BLOCK = {
    "w_in": lambda g: jnp.concatenate(g, axis=0).reshape(N_DEV, IN_WIDTH // N_DEV, D_MODEL),
    "w_out": lambda g: g.reshape(N_DEV, 256, D_MODEL),
    "w_up": lambda g: g,
    "w_down": lambda g: g.reshape(N_DEV, 512, D_MODEL),
    "w_ple_gate": lambda g: g.astype(bf16).reshape(N_DEV, 128, D_MODEL),
    "w_ple": lambda g: _blocks_of_columns(g.astype(bf16)),
}


class _StepComm:
    def __init__(self, me, ready, gathers, token):
        self.me, self.ready, self.gathers, self.token = me, ready, gathers, token
        self.sent = []

    def start_token(self):
        return self.token

    def weight(self, name, after=()):
        if name not in self.ready:
            (land,), _ = _exchange_wait([self.gathers[name]], False, "gather_wait_" + name, list(after))
            self.ready[name] = _WHOLE[name](land)
        return self.ready[name]

    def send_grads(self, grads):
        names = list(grads)
        blocks = [_BY_BLOCK[n](grads[n]) for n in names]
        states, self.token = _exchange_start(blocks, True, "scatter_start_" + names[0], self.token)
        self.sent.append((names, states))
        return self.token


def kernel(x, p, mix_norm_g, w_in, ssd_conv_w, ssd_conv_b, dt_bias, A_log, D_skip, ssd_norm_g, conf_dw_w, conf_dw_b, conf_ln_g, conf_ln_b, w_out, mlp_norm_g, w_up, w_down, ple_gate_norm_g, w_ple_gate, b_ple_gate, w_ple, ple_norm_g, final_norm_g, loss_target, m_mix_norm_g, m_w_in, m_ssd_conv_w, m_ssd_conv_b, m_dt_bias, m_A_log, m_D_skip, m_ssd_norm_g, m_conf_dw_w, m_conf_dw_b, m_conf_ln_g, m_conf_ln_b, m_w_out, m_mlp_norm_g, m_w_up, m_w_down, m_ple_gate_norm_g, m_w_ple_gate, m_b_ple_gate, m_w_ple, m_ple_norm_g, m_final_norm_g, v_mix_norm_g, v_w_in, v_ssd_conv_w, v_ssd_conv_b, v_dt_bias, v_A_log, v_D_skip, v_ssd_norm_g, v_conf_dw_w, v_conf_dw_b, v_conf_ln_g, v_conf_ln_b, v_w_out, v_mlp_norm_g, v_w_up, v_w_down, v_ple_gate_norm_g, v_w_ple_gate, v_b_ple_gate, v_w_ple, v_ple_norm_g, v_final_norm_g):
    wts = dict(mix_norm_g=mix_norm_g, w_in=w_in, ssd_conv_w=ssd_conv_w, ssd_conv_b=ssd_conv_b, dt_bias=dt_bias, A_log=A_log,
               D_skip=D_skip, ssd_norm_g=ssd_norm_g, conf_dw_w=conf_dw_w, conf_dw_b=conf_dw_b, conf_ln_g=conf_ln_g,
               conf_ln_b=conf_ln_b, w_out=w_out, mlp_norm_g=mlp_norm_g, w_up=w_up, w_down=w_down,
               ple_gate_norm_g=ple_gate_norm_g, w_ple_gate=w_ple_gate, b_ple_gate=b_ple_gate, w_ple=w_ple,
               ple_norm_g=ple_norm_g, final_norm_g=final_norm_g)
    mom1 = dict(mix_norm_g=m_mix_norm_g, w_in=m_w_in, ssd_conv_w=m_ssd_conv_w, ssd_conv_b=m_ssd_conv_b, dt_bias=m_dt_bias,
                A_log=m_A_log, D_skip=m_D_skip, ssd_norm_g=m_ssd_norm_g, conf_dw_w=m_conf_dw_w, conf_dw_b=m_conf_dw_b,
                conf_ln_g=m_conf_ln_g, conf_ln_b=m_conf_ln_b, w_out=m_w_out, mlp_norm_g=m_mlp_norm_g, w_up=m_w_up,
                w_down=m_w_down, ple_gate_norm_g=m_ple_gate_norm_g, w_ple_gate=m_w_ple_gate, b_ple_gate=m_b_ple_gate,
                w_ple=m_w_ple, ple_norm_g=m_ple_norm_g, final_norm_g=m_final_norm_g)
    mom2 = dict(mix_norm_g=v_mix_norm_g, w_in=v_w_in, ssd_conv_w=v_ssd_conv_w, ssd_conv_b=v_ssd_conv_b, dt_bias=v_dt_bias,
                A_log=v_A_log, D_skip=v_D_skip, ssd_norm_g=v_ssd_norm_g, conf_dw_w=v_conf_dw_w, conf_dw_b=v_conf_dw_b,
                conf_ln_g=v_conf_ln_g, conf_ln_b=v_conf_ln_b, w_out=v_w_out, mlp_norm_g=v_mlp_norm_g, w_up=v_w_up,
                w_down=v_w_down, ple_gate_norm_g=v_ple_gate_norm_g, w_ple_gate=v_w_ple_gate, b_ple_gate=v_b_ple_gate,
                w_ple=v_w_ple, ple_norm_g=v_ple_norm_g, final_norm_g=v_final_norm_g)
    x_pos, y_pos, c_pos = _mesh_pos()
    me = 4 * x_pos + 2 * y_pos + c_pos

    first = _all_gather([wts["w_in"][0].T.astype(bf16), wts["ssd_conv_w"][0], wts["conf_dw_w"][0]], "gather_first")
    ready = {
        "w_in": first[0].reshape(IN_WIDTH, D_MODEL),
        "ssd_conv_w": jnp.pad(_columns_of_blocks(first[1]), ((0, 8 - SSD_CONV), (0, 0))),
        "conf_dw_w": jnp.pad(_columns_of_blocks(first[2]), ((0, 32 - CONF_KERNEL), (0, 0))),
    }
    shards = [wts[n][0].astype(bf16) for n in _LATER]
    states, token = _exchange_start(shards, False, "gather_start", first[1])
    comm = _StepComm(me, ready, dict(zip(_LATER, states)), token)
    small = {n: wts[n].reshape(1, -1) for n in _REPLICATED}

    grad_x, acc = _local_step(x[0], p[0, 0], loss_target[0], small, comm)

    packed = _pack_small(acc)
    (small_state,), _ = _exchange_start([packed], False, "small_start", comm.token)
    grads, delta, new_m, new_v = {}, {}, {}, {}
    me_index = me.astype(jnp.int32).reshape(1)

    def adamw_big(names, sent, after):
        lands, mine = _exchange_wait(sent, True, "scatter_wait_" + names[0], after)
        for n, land, own in zip(names, lands, mine):
            view = (lambda a: a[0].T) if n == "w_in" else (lambda a: a[0])
            back = (lambda a: a.T[None]) if n == "w_in" else (lambda a: a[None])
            out = _adamw_big(land, own, me_index, view(wts[n]), view(mom1[n]), view(mom2[n]), "adamw_" + n)
            grads[n], delta[n], new_m[n], new_v[n] = [back(a) for a in out]

    for names, sent in comm.sent[:-1]:
        adamw_big(names, sent, [grad_x])
    adamw_big(*comm.sent[-1], [delta[n] for n in _LATER])
    (all_small,), _ = _exchange_wait([small_state], False, "small_wait", [delta["w_in"]])
    as_row = lambda d: {n: d[n].reshape(1, -1) for n in _REPLICATED}
    loss, per_name, conv_w_sum, conf_w_sum = _small_update(all_small, as_row(wts), as_row(mom1), as_row(mom2))
    for n in _REPLICATED:
        grads[n], delta[n], new_m[n], new_v[n] = [a.reshape(wts[n].shape) for a in per_name[n]]
    filt_g = [lax.dynamic_slice_in_dim(conv_w_sum, me * 192, 192, axis=1),
              lax.dynamic_slice_in_dim(conf_w_sum, me * 128, 128, axis=1)]
    filt = _adamw_filters(filt_g, [wts[n][0] for n in _CONV_WEIGHTS], [mom1[n][0] for n in _CONV_WEIGHTS],
                          [mom2[n][0] for n in _CONV_WEIGHTS])
    for n, g, (d, mn, vn) in zip(_CONV_WEIGHTS, filt_g, filt):
        grads[n], delta[n], new_m[n], new_v[n] = g[None], d[None], mn[None], vn[None]

    return (loss.reshape(()), grad_x[None], *[grads[n] for n in _WEIGHT_ORDER], *[delta[n] for n in _WEIGHT_ORDER],
            *[new_m[n] for n in _WEIGHT_ORDER], *[new_v[n] for n in _WEIGHT_ORDER])
```
